```python
import math
import jax, jax.numpy as jnp
from jax import lax
import numpy as np

D_MODEL = 1024
BATCH = 8
SEQ = 4096
DEPTH = 4

N_A_LAYERS = DEPTH // 2
N_B_LAYERS = DEPTH - N_A_LAYERS
CONV_WIDTH = 31
HEAD_DIM = 64
N_HEADS = D_MODEL // HEAD_DIM
N_KV_HEADS = N_HEADS // 4
GROUP = N_HEADS // N_KV_HEADS
WINDOW = 128
BLOCK = 128
N_BUCKETS = 32
MAX_DISTANCE = 128
D_FF = -(-8 * D_MODEL // (3 * 256)) * 256
EPS = 1e-6
NEG_INF = -1e30

kernel_name = "yoco_conformer_swa_sink_hybrid"


def rmsnorm(x, g):
    xf = x.astype(jnp.float32)
    xf = xf * lax.rsqrt(jnp.mean(xf * xf, axis=-1, keepdims=True) + EPS)
    return (xf * g.astype(jnp.float32)).astype(x.dtype)


def layernorm(x, g, b):
    xf = x.astype(jnp.float32)
    mu = jnp.mean(xf, axis=-1, keepdims=True)
    var = jnp.mean(jnp.square(xf - mu), axis=-1, keepdims=True)
    y = (xf - mu) * lax.rsqrt(var + EPS) * g.astype(jnp.float32) + b.astype(jnp.float32)
    return y.astype(x.dtype)


def swiglu_ffn(x, w_up, w_down):
    gate, up = jnp.split(x @ w_up, 2, axis=-1)
    return (jax.nn.silu(gate) * up) @ w_down


def conformer_conv(x, w_pw1, b_pw1, w_dw, b_dw, ln_g, ln_b, w_pw2, b_pw2):
    a = jax.nn.glu(x @ w_pw1 + b_pw1, axis=-1)
    y = lax.conv_general_dilated(
        a, w_dw[:, None, :].astype(a.dtype), window_strides=(1,),
        padding=((CONV_WIDTH - 1, 0),),
        dimension_numbers=('NWC', 'WIO', 'NWC'),
        feature_group_count=D_MODEL) + b_dw
    y = jax.nn.silu(layernorm(y, ln_g, ln_b))
    return y @ w_pw2 + b_pw2


def t5_causal_bucket(dist):
    max_exact = N_BUCKETS // 2
    d = jnp.maximum(dist, 0)
    log_ratio = jnp.log(jnp.maximum(d, 1).astype(jnp.float32) / max_exact) / math.log(MAX_DISTANCE / max_exact)
    large = max_exact + (log_ratio * (N_BUCKETS - max_exact)).astype(jnp.int32)
    large = jnp.minimum(large, N_BUCKETS - 1)
    return jnp.where(d < max_exact, d, large)


def banded_sink_attention(q, k, v, sinks, rel_bias):
    B, S = q.shape[0], q.shape[1]
    nb = S // BLOCK
    qb = q.reshape(B, nb, BLOCK, N_KV_HEADS, GROUP, HEAD_DIM)
    kb = k.reshape(B, nb, BLOCK, N_KV_HEADS, HEAD_DIM)
    vb = v.reshape(B, nb, BLOCK, N_KV_HEADS, HEAD_DIM)
    pad = ((0, 0), (1, 0), (0, 0), (0, 0), (0, 0))
    k_band = jnp.concatenate([jnp.pad(kb, pad)[:, :-1], kb], axis=2)
    v_band = jnp.concatenate([jnp.pad(vb, pad)[:, :-1], vb], axis=2)

    s = jnp.einsum('bnqhgd,bnkhd->bnhgqk', qb, k_band,
                   preferred_element_type=jnp.float32) * (HEAD_DIM ** -0.5)

    qi = jnp.arange(BLOCK, dtype=jnp.int32)
    kj = jnp.arange(2 * BLOCK, dtype=jnp.int32)
    dist = qi[:, None] + BLOCK - kj[None, :]
    in_window = (dist >= 0) & (dist < WINDOW)
    bias = rel_bias.astype(jnp.float32)[t5_causal_bucket(dist)]
    bias = jnp.transpose(bias, (2, 0, 1)).reshape(N_KV_HEADS, GROUP, BLOCK, 2 * BLOCK)
    key_pos = (jnp.arange(nb, dtype=jnp.int32)[:, None] - 1) * BLOCK + kj[None, :]
    mask = in_window[None, :, :] & (key_pos >= 0)[:, None, :]

    s = jnp.where(mask[None, :, None, None], s + bias, NEG_INF)
    sink = sinks.astype(jnp.float32).reshape(N_KV_HEADS, GROUP, 1, 1)
    m = jnp.maximum(jnp.max(s, axis=-1, keepdims=True), sink)
    p = jnp.exp(s - m)
    probs = p / (jnp.sum(p, axis=-1, keepdims=True) + jnp.exp(sink - m))
    o = jnp.einsum('bnhgqk,bnkhd->bnqhgd', probs.astype(v.dtype), v_band)
    return o.reshape(B, S, N_HEADS * HEAD_DIM)


def _fwd_setup_inputs(seed: int = 0) -> dict:
    key = jax.random.key(seed)
    ks = jax.random.split(key, 24)
    f32 = jnp.float32
    D, HD, KVD = D_MODEL, N_HEADS * HEAD_DIM, N_KV_HEADS * HEAD_DIM
    nrm = lambda k, shape, scale: jax.random.normal(k, shape, f32) * scale
    gain = lambda k, shape: 1.0 + 0.05 * jax.random.normal(k, shape, f32)
    return {
        "x": jax.random.normal(ks[0], (BATCH, SEQ, D), f32),
        "norm_mix": gain(ks[1], (DEPTH, D)),
        "norm_ffn": gain(ks[2], (DEPTH, D)),
        "conv_w_pw1": nrm(ks[3], (N_A_LAYERS, D, 2 * D), D ** -0.5),
        "conv_b_pw1": nrm(ks[4], (N_A_LAYERS, 2 * D), 0.02),
        "conv_w_dw": nrm(ks[5], (N_A_LAYERS, CONV_WIDTH, D), CONV_WIDTH ** -0.5),
        "conv_b_dw": nrm(ks[6], (N_A_LAYERS, D), 0.02),
        "conv_ln_g": gain(ks[7], (N_A_LAYERS, D)),
        "conv_ln_b": nrm(ks[8], (N_A_LAYERS, D), 0.02),
        "conv_w_pw2": nrm(ks[9], (N_A_LAYERS, D, D), D ** -0.5),
        "conv_b_pw2": nrm(ks[10], (N_A_LAYERS, D), 0.02),
        "norm_kv": gain(ks[11], (D,)),
        "w_kv": nrm(ks[12], (D, 2 * KVD), D ** -0.5),
        "w_q": nrm(ks[13], (N_B_LAYERS, D, HD), D ** -0.5),
        "w_o": nrm(ks[14], (N_B_LAYERS, HD, D), HD ** -0.5),
        "sinks": nrm(ks[15], (N_B_LAYERS, N_HEADS), 0.5),
        "rel_bias": nrm(ks[16], (N_BUCKETS, N_HEADS), 0.5),
        "ffn_w_up": nrm(ks[17], (DEPTH, D, 2 * D_FF), D ** -0.5),
        "ffn_w_down": nrm(ks[18], (DEPTH, D_FF, D), D_FF ** -0.5),
        "norm_final": gain(ks[19], (D,)),
    }


def _fwd_reference(x, norm_mix, norm_ffn, conv_w_pw1, conv_b_pw1, conv_w_dw, conv_b_dw,
              conv_ln_g, conv_ln_b, conv_w_pw2, conv_b_pw2, norm_kv, w_kv, w_q, w_o,
              sinks, rel_bias, ffn_w_up, ffn_w_down, norm_final):
    B, S = x.shape[0], x.shape[1]
    h = x
    k_shared = v_shared = None
    for l in range(DEPTH):
        if l < N_A_LAYERS:
            i = l
            h = h + conformer_conv(rmsnorm(h, norm_mix[l]), conv_w_pw1[i], conv_b_pw1[i],
                                   conv_w_dw[i], conv_b_dw[i], conv_ln_g[i], conv_ln_b[i],
                                   conv_w_pw2[i], conv_b_pw2[i])
        else:
            if l == N_A_LAYERS:
                kv = rmsnorm(h, norm_kv) @ w_kv
                k_flat, v_flat = jnp.split(kv, 2, axis=-1)
                k_shared = k_flat.reshape(B, S, N_KV_HEADS, HEAD_DIM)
                v_shared = v_flat.reshape(B, S, N_KV_HEADS, HEAD_DIM)
            j = l - N_A_LAYERS
            q = (rmsnorm(h, norm_mix[l]) @ w_q[j]).reshape(B, S, N_HEADS, HEAD_DIM)
            attn = banded_sink_attention(q, k_shared, v_shared, sinks[j], rel_bias)
            h = h + attn @ w_o[j]
        h = h + swiglu_ffn(rmsnorm(h, norm_ffn[l]), ffn_w_up[l], ffn_w_down[l])
    return rmsnorm(h, norm_final)


import jax as _jax
import jax.numpy as _jnp

TWIN_FORMAT = 'train_step'
FWD_PARAMS = ['x', 'norm_mix', 'norm_ffn', 'conv_w_pw1', 'conv_b_pw1', 'conv_w_dw', 'conv_b_dw', 'conv_ln_g', 'conv_ln_b', 'conv_w_pw2', 'conv_b_pw2', 'norm_kv', 'w_kv', 'w_q', 'w_o', 'sinks', 'rel_bias', 'ffn_w_up', 'ffn_w_down', 'norm_final']
TWIN_WEIGHTS = ['norm_mix', 'norm_ffn', 'conv_w_pw1', 'conv_b_pw1', 'conv_w_dw', 'conv_b_dw', 'conv_ln_g', 'conv_ln_b', 'conv_w_pw2', 'conv_b_pw2', 'norm_kv', 'w_kv', 'w_q', 'w_o', 'sinks', 'rel_bias', 'ffn_w_up', 'ffn_w_down', 'norm_final']
TWIN_DIFF_INPUT = 'x'
TWIN_INPUTS = ['x', 'norm_mix', 'norm_ffn', 'conv_w_pw1', 'conv_b_pw1', 'conv_w_dw', 'conv_b_dw', 'conv_ln_g', 'conv_ln_b', 'conv_w_pw2', 'conv_b_pw2', 'norm_kv', 'w_kv', 'w_q', 'w_o', 'sinks', 'rel_bias', 'ffn_w_up', 'ffn_w_down', 'norm_final', 'loss_target', 'm_norm_mix', 'm_norm_ffn', 'm_conv_w_pw1', 'm_conv_b_pw1', 'm_conv_w_dw', 'm_conv_b_dw', 'm_conv_ln_g', 'm_conv_ln_b', 'm_conv_w_pw2', 'm_conv_b_pw2', 'm_norm_kv', 'm_w_kv', 'm_w_q', 'm_w_o', 'm_sinks', 'm_rel_bias', 'm_ffn_w_up', 'm_ffn_w_down', 'm_norm_final', 'v_norm_mix', 'v_norm_ffn', 'v_conv_w_pw1', 'v_conv_b_pw1', 'v_conv_w_dw', 'v_conv_b_dw', 'v_conv_ln_g', 'v_conv_ln_b', 'v_conv_w_pw2', 'v_conv_b_pw2', 'v_norm_kv', 'v_w_kv', 'v_w_q', 'v_w_o', 'v_sinks', 'v_rel_bias', 'v_ffn_w_up', 'v_ffn_w_down', 'v_norm_final']
TWIN_OUTPUTS = ['loss', 'grad_x', 'grad_norm_mix', 'grad_norm_ffn', 'grad_conv_w_pw1', 'grad_conv_b_pw1', 'grad_conv_w_dw', 'grad_conv_b_dw', 'grad_conv_ln_g', 'grad_conv_ln_b', 'grad_conv_w_pw2', 'grad_conv_b_pw2', 'grad_norm_kv', 'grad_w_kv', 'grad_w_q', 'grad_w_o', 'grad_sinks', 'grad_rel_bias', 'grad_ffn_w_up', 'grad_ffn_w_down', 'grad_norm_final', 'delta_norm_mix', 'delta_norm_ffn', 'delta_conv_w_pw1', 'delta_conv_b_pw1', 'delta_conv_w_dw', 'delta_conv_b_dw', 'delta_conv_ln_g', 'delta_conv_ln_b', 'delta_conv_w_pw2', 'delta_conv_b_pw2', 'delta_norm_kv', 'delta_w_kv', 'delta_w_q', 'delta_w_o', 'delta_sinks', 'delta_rel_bias', 'delta_ffn_w_up', 'delta_ffn_w_down', 'delta_norm_final', 'new_m_norm_mix', 'new_m_norm_ffn', 'new_m_conv_w_pw1', 'new_m_conv_b_pw1', 'new_m_conv_w_dw', 'new_m_conv_b_dw', 'new_m_conv_ln_g', 'new_m_conv_ln_b', 'new_m_conv_w_pw2', 'new_m_conv_b_pw2', 'new_m_norm_kv', 'new_m_w_kv', 'new_m_w_q', 'new_m_w_o', 'new_m_sinks', 'new_m_rel_bias', 'new_m_ffn_w_up', 'new_m_ffn_w_down', 'new_m_norm_final', 'new_v_norm_mix', 'new_v_norm_ffn', 'new_v_conv_w_pw1', 'new_v_conv_b_pw1', 'new_v_conv_w_dw', 'new_v_conv_b_dw', 'new_v_conv_ln_g', 'new_v_conv_ln_b', 'new_v_conv_w_pw2', 'new_v_conv_b_pw2', 'new_v_norm_kv', 'new_v_w_kv', 'new_v_w_q', 'new_v_w_o', 'new_v_sinks', 'new_v_rel_bias', 'new_v_ffn_w_up', 'new_v_ffn_w_down', 'new_v_norm_final']
TWIN_LEAF_KINDS = {'loss': 'loss', 'grad_x': 'grad_x', 'grad_norm_mix': 'grad_w', 'grad_norm_ffn': 'grad_w', 'grad_conv_w_pw1': 'grad_w', 'grad_conv_b_pw1': 'grad_w', 'grad_conv_w_dw': 'grad_w', 'grad_conv_b_dw': 'grad_w', 'grad_conv_ln_g': 'grad_w', 'grad_conv_ln_b': 'grad_w', 'grad_conv_w_pw2': 'grad_w', 'grad_conv_b_pw2': 'grad_w', 'grad_norm_kv': 'grad_w', 'grad_w_kv': 'grad_w', 'grad_w_q': 'grad_w', 'grad_w_o': 'grad_w', 'grad_sinks': 'grad_w', 'grad_rel_bias': 'grad_w', 'grad_ffn_w_up': 'grad_w', 'grad_ffn_w_down': 'grad_w', 'grad_norm_final': 'grad_w', 'delta_norm_mix': 'delta_w', 'delta_norm_ffn': 'delta_w', 'delta_conv_w_pw1': 'delta_w', 'delta_conv_b_pw1': 'delta_w', 'delta_conv_w_dw': 'delta_w', 'delta_conv_b_dw': 'delta_w', 'delta_conv_ln_g': 'delta_w', 'delta_conv_ln_b': 'delta_w', 'delta_conv_w_pw2': 'delta_w', 'delta_conv_b_pw2': 'delta_w', 'delta_norm_kv': 'delta_w', 'delta_w_kv': 'delta_w', 'delta_w_q': 'delta_w', 'delta_w_o': 'delta_w', 'delta_sinks': 'delta_w', 'delta_rel_bias': 'delta_w', 'delta_ffn_w_up': 'delta_w', 'delta_ffn_w_down': 'delta_w', 'delta_norm_final': 'delta_w', 'new_m_norm_mix': 'new_m', 'new_m_norm_ffn': 'new_m', 'new_m_conv_w_pw1': 'new_m', 'new_m_conv_b_pw1': 'new_m', 'new_m_conv_w_dw': 'new_m', 'new_m_conv_b_dw': 'new_m', 'new_m_conv_ln_g': 'new_m', 'new_m_conv_ln_b': 'new_m', 'new_m_conv_w_pw2': 'new_m', 'new_m_conv_b_pw2': 'new_m', 'new_m_norm_kv': 'new_m', 'new_m_w_kv': 'new_m', 'new_m_w_q': 'new_m', 'new_m_w_o': 'new_m', 'new_m_sinks': 'new_m', 'new_m_rel_bias': 'new_m', 'new_m_ffn_w_up': 'new_m', 'new_m_ffn_w_down': 'new_m', 'new_m_norm_final': 'new_m', 'new_v_norm_mix': 'new_v', 'new_v_norm_ffn': 'new_v', 'new_v_conv_w_pw1': 'new_v', 'new_v_conv_b_pw1': 'new_v', 'new_v_conv_w_dw': 'new_v', 'new_v_conv_b_dw': 'new_v', 'new_v_conv_ln_g': 'new_v', 'new_v_conv_ln_b': 'new_v', 'new_v_conv_w_pw2': 'new_v', 'new_v_conv_b_pw2': 'new_v', 'new_v_norm_kv': 'new_v', 'new_v_w_kv': 'new_v', 'new_v_w_q': 'new_v', 'new_v_w_o': 'new_v', 'new_v_sinks': 'new_v', 'new_v_rel_bias': 'new_v', 'new_v_ffn_w_up': 'new_v', 'new_v_ffn_w_down': 'new_v', 'new_v_norm_final': 'new_v'}


def _forward(args):
    return _fwd_reference(*[args[k] for k in FWD_PARAMS])


def _output_shape():
    out = _jax.eval_shape(lambda: _forward(_fwd_setup_inputs(0)))
    return out.shape, out.dtype

N_MICROBATCH = 1
ADAM_LR = 0.001
ADAM_B1 = 0.9
ADAM_B2 = 0.999
ADAM_EPS = 1e-08
ADAM_WD = 0.01
ADAM_STEP = 10
PER_EXAMPLE_BATCH_AXIS = {'x': 0, 'loss_target': 0}
SHARED_INPUTS = []
_WEIGHT_DTYPES = {'norm_mix': _jnp.float32, 'norm_ffn': _jnp.float32, 'conv_w_pw1': _jnp.float32, 'conv_b_pw1': _jnp.float32, 'conv_w_dw': _jnp.float32, 'conv_b_dw': _jnp.float32, 'conv_ln_g': _jnp.float32, 'conv_ln_b': _jnp.float32, 'conv_w_pw2': _jnp.float32, 'conv_b_pw2': _jnp.float32, 'norm_kv': _jnp.float32, 'w_kv': _jnp.float32, 'w_q': _jnp.float32, 'w_o': _jnp.float32, 'sinks': _jnp.float32, 'rel_bias': _jnp.float32, 'ffn_w_up': _jnp.float32, 'ffn_w_down': _jnp.float32, 'norm_final': _jnp.float32}
MOMENT_SCALE = {'norm_mix': 8.542115e-02, 'norm_ffn': 1.149033e-01, 'conv_w_pw1': 8.658367e-02, 'conv_b_pw1': 1.589419e-01, 'conv_w_dw': 1.159149e-01, 'conv_b_dw': 3.237915e-01, 'conv_ln_g': 1.832162e-01, 'conv_ln_b': 2.162590e-01, 'conv_w_pw2': 1.309749e-01, 'conv_b_pw2': 4.265874e-01, 'norm_kv': 7.292080e-02, 'w_kv': 1.032781e-01, 'w_q': 2.090423e-02, 'w_o': 6.146573e-02, 'sinks': 1.871335e-02, 'rel_bias': 3.824116e-02, 'ffn_w_up': 4.773662e-02, 'ffn_w_down': 7.860366e-02, 'norm_final': 3.211252e+01}


def _to_microbatches(a, axis):
    t = _jnp.moveaxis(a, axis, 0)
    t = t.reshape((N_MICROBATCH, t.shape[0] // N_MICROBATCH) + t.shape[1:])
    return _jnp.moveaxis(t, 1, axis + 1)


def setup_inputs(seed: int = 0) -> dict:
    inp = _fwd_setup_inputs(seed)
    key = _jax.random.fold_in(_jax.random.key(seed), 7919)
    shape, _ = _output_shape()
    out = dict(inp)
    out["loss_target"] = _jax.random.normal(_jax.random.fold_in(key, 0), shape, _jnp.float32)
    for i, name in enumerate(TWIN_WEIGHTS):
        w = inp[name].astype(_jnp.float32)
        if MOMENT_SCALE is None:
            s = _jnp.sqrt(_jnp.mean(_jnp.square(w)) + 1e-30)
        else:
            s = MOMENT_SCALE[name]
        km, kv = _jax.random.split(_jax.random.fold_in(key, i + 1))
        out[name] = w
        out["m_" + name] = s * _jax.random.normal(km, w.shape, _jnp.float32)
        out["v_" + name] = (s * s) * _jax.random.uniform(kv, w.shape, _jnp.float32, 0.5, 1.5)
    if N_MICROBATCH > 1:
        for name, axis in PER_EXAMPLE_BATCH_AXIS.items():
            out[name] = _to_microbatches(out[name], axis)
    return {'x': out['x'], 'norm_mix': out['norm_mix'], 'norm_ffn': out['norm_ffn'], 'conv_w_pw1': out['conv_w_pw1'], 'conv_b_pw1': out['conv_b_pw1'], 'conv_w_dw': out['conv_w_dw'], 'conv_b_dw': out['conv_b_dw'], 'conv_ln_g': out['conv_ln_g'], 'conv_ln_b': out['conv_ln_b'], 'conv_w_pw2': out['conv_w_pw2'], 'conv_b_pw2': out['conv_b_pw2'], 'norm_kv': out['norm_kv'], 'w_kv': out['w_kv'], 'w_q': out['w_q'], 'w_o': out['w_o'], 'sinks': out['sinks'], 'rel_bias': out['rel_bias'], 'ffn_w_up': out['ffn_w_up'], 'ffn_w_down': out['ffn_w_down'], 'norm_final': out['norm_final'], 'loss_target': out['loss_target'], 'm_norm_mix': out['m_norm_mix'], 'm_norm_ffn': out['m_norm_ffn'], 'm_conv_w_pw1': out['m_conv_w_pw1'], 'm_conv_b_pw1': out['m_conv_b_pw1'], 'm_conv_w_dw': out['m_conv_w_dw'], 'm_conv_b_dw': out['m_conv_b_dw'], 'm_conv_ln_g': out['m_conv_ln_g'], 'm_conv_ln_b': out['m_conv_ln_b'], 'm_conv_w_pw2': out['m_conv_w_pw2'], 'm_conv_b_pw2': out['m_conv_b_pw2'], 'm_norm_kv': out['m_norm_kv'], 'm_w_kv': out['m_w_kv'], 'm_w_q': out['m_w_q'], 'm_w_o': out['m_w_o'], 'm_sinks': out['m_sinks'], 'm_rel_bias': out['m_rel_bias'], 'm_ffn_w_up': out['m_ffn_w_up'], 'm_ffn_w_down': out['m_ffn_w_down'], 'm_norm_final': out['m_norm_final'], 'v_norm_mix': out['v_norm_mix'], 'v_norm_ffn': out['v_norm_ffn'], 'v_conv_w_pw1': out['v_conv_w_pw1'], 'v_conv_b_pw1': out['v_conv_b_pw1'], 'v_conv_w_dw': out['v_conv_w_dw'], 'v_conv_b_dw': out['v_conv_b_dw'], 'v_conv_ln_g': out['v_conv_ln_g'], 'v_conv_ln_b': out['v_conv_ln_b'], 'v_conv_w_pw2': out['v_conv_w_pw2'], 'v_conv_b_pw2': out['v_conv_b_pw2'], 'v_norm_kv': out['v_norm_kv'], 'v_w_kv': out['v_w_kv'], 'v_w_q': out['v_w_q'], 'v_w_o': out['v_w_o'], 'v_sinks': out['v_sinks'], 'v_rel_bias': out['v_rel_bias'], 'v_ffn_w_up': out['v_ffn_w_up'], 'v_ffn_w_down': out['v_ffn_w_down'], 'v_norm_final': out['v_norm_final']}


def _loss(weights, diff, rest, loss_target):
    with _jax.named_scope("forward"):
        args = {**rest, TWIN_DIFF_INPUT: diff, **{k: w.astype(_WEIGHT_DTYPES[k]) for k, w in weights.items()}}
        y = _forward(args)
    with _jax.named_scope("loss_head"):
        err = _jnp.square(y.astype(_jnp.float32) - loss_target)
        return 0.5 * _jnp.sum(_jnp.mean(err, axis=-1)) if err.ndim else 0.5 * err


def _adamw(w, g, m, v):
    m = ADAM_B1 * m + (1.0 - ADAM_B1) * g
    v = ADAM_B2 * v + (1.0 - ADAM_B2) * _jnp.square(g)
    m_hat = m / (1.0 - ADAM_B1 ** ADAM_STEP)
    v_hat = v / (1.0 - ADAM_B2 ** ADAM_STEP)
    delta = -ADAM_LR * (m_hat / (_jnp.sqrt(v_hat) + ADAM_EPS) + ADAM_WD * w)
    return delta, m, v


def reference(x, norm_mix, norm_ffn, conv_w_pw1, conv_b_pw1, conv_w_dw, conv_b_dw, conv_ln_g, conv_ln_b, conv_w_pw2, conv_b_pw2, norm_kv, w_kv, w_q, w_o, sinks, rel_bias, ffn_w_up, ffn_w_down, norm_final, loss_target, m_norm_mix, m_norm_ffn, m_conv_w_pw1, m_conv_b_pw1, m_conv_w_dw, m_conv_b_dw, m_conv_ln_g, m_conv_ln_b, m_conv_w_pw2, m_conv_b_pw2, m_norm_kv, m_w_kv, m_w_q, m_w_o, m_sinks, m_rel_bias, m_ffn_w_up, m_ffn_w_down, m_norm_final, v_norm_mix, v_norm_ffn, v_conv_w_pw1, v_conv_b_pw1, v_conv_w_dw, v_conv_b_dw, v_conv_ln_g, v_conv_ln_b, v_conv_w_pw2, v_conv_b_pw2, v_norm_kv, v_w_kv, v_w_q, v_w_o, v_sinks, v_rel_bias, v_ffn_w_up, v_ffn_w_down, v_norm_final):
    given = dict(x=x, norm_mix=norm_mix, norm_ffn=norm_ffn, conv_w_pw1=conv_w_pw1, conv_b_pw1=conv_b_pw1, conv_w_dw=conv_w_dw, conv_b_dw=conv_b_dw, conv_ln_g=conv_ln_g, conv_ln_b=conv_ln_b, conv_w_pw2=conv_w_pw2, conv_b_pw2=conv_b_pw2, norm_kv=norm_kv, w_kv=w_kv, w_q=w_q, w_o=w_o, sinks=sinks, rel_bias=rel_bias, ffn_w_up=ffn_w_up, ffn_w_down=ffn_w_down, norm_final=norm_final, loss_target=loss_target, m_norm_mix=m_norm_mix, m_norm_ffn=m_norm_ffn, m_conv_w_pw1=m_conv_w_pw1, m_conv_b_pw1=m_conv_b_pw1, m_conv_w_dw=m_conv_w_dw, m_conv_b_dw=m_conv_b_dw, m_conv_ln_g=m_conv_ln_g, m_conv_ln_b=m_conv_ln_b, m_conv_w_pw2=m_conv_w_pw2, m_conv_b_pw2=m_conv_b_pw2, m_norm_kv=m_norm_kv, m_w_kv=m_w_kv, m_w_q=m_w_q, m_w_o=m_w_o, m_sinks=m_sinks, m_rel_bias=m_rel_bias, m_ffn_w_up=m_ffn_w_up, m_ffn_w_down=m_ffn_w_down, m_norm_final=m_norm_final, v_norm_mix=v_norm_mix, v_norm_ffn=v_norm_ffn, v_conv_w_pw1=v_conv_w_pw1, v_conv_b_pw1=v_conv_b_pw1, v_conv_w_dw=v_conv_w_dw, v_conv_b_dw=v_conv_b_dw, v_conv_ln_g=v_conv_ln_g, v_conv_ln_b=v_conv_ln_b, v_conv_w_pw2=v_conv_w_pw2, v_conv_b_pw2=v_conv_b_pw2, v_norm_kv=v_norm_kv, v_w_kv=v_w_kv, v_w_q=v_w_q, v_w_o=v_w_o, v_sinks=v_sinks, v_rel_bias=v_rel_bias, v_ffn_w_up=v_ffn_w_up, v_ffn_w_down=v_ffn_w_down, v_norm_final=v_norm_final)
    weights = {n: given[n] for n in TWIN_WEIGHTS}
    shared = {n: given[n] for n in SHARED_INPUTS}
    per_example = {n: given[n] for n in ['x']}
    grad_fn = _jax.value_and_grad(_loss, argnums=(0, 1))

    def one_microbatch(ex, loss_target):
        ex = dict(ex)
        diff = ex.pop(TWIN_DIFF_INPUT)
        return grad_fn(weights, diff, {**shared, **ex}, loss_target)

    if N_MICROBATCH == 1:
        loss, (grad_w, grad_x) = one_microbatch(per_example, given["loss_target"])
    else:
        def body(carry, xs):
            loss_sum, grad_sum = carry
            l_k, (gw_k, gx_k) = one_microbatch(xs[0], xs[1])
            with _jax.named_scope("update"):
                return (loss_sum + l_k, _jax.tree.map(_jnp.add, grad_sum, gw_k)), gx_k

        init = (_jnp.zeros((), _jnp.float32), _jax.tree.map(_jnp.zeros_like, weights))
        (loss, grad_w), grad_x = _jax.lax.scan(body, init, (per_example, given["loss_target"]))
    with _jax.named_scope("update"):
        delta_w, new_m, new_v = {}, {}, {}
        for n in TWIN_WEIGHTS:
            delta_w[n], new_m[n], new_v[n] = _adamw(weights[n], grad_w[n], given["m_" + n], given["v_" + n])
    return (loss, grad_x, *[grad_w[n] for n in TWIN_WEIGHTS], *[delta_w[n] for n in TWIN_WEIGHTS],
            *[new_m[n] for n in TWIN_WEIGHTS], *[new_v[n] for n in TWIN_WEIGHTS])
```

```python
import functools
import math

import numpy as np
import jax
import jax.numpy as jnp
from jax import lax
from jax.experimental import pallas as pl
from jax.experimental.pallas import tpu as pltpu

F32 = jnp.float32
BF16 = jnp.bfloat16
MESH = pl.DeviceIdType.MESH

D = 1024
DFF = 2816
N_HEADS = 16
N_KV = 4
GROUP = 4
HD = 64
BLK = 128
CONV_W = 31
HALO = 32
N_BUCKETS = 32
MAX_DISTANCE = 128
EPS = 1e-6
NEG_INF = -1e30
TM = 256
VMEM_LIMIT = 56 * 2 ** 20

ADAM_LR, ADAM_B1, ADAM_B2, ADAM_EPS, ADAM_WD, ADAM_STEP = 0.001, 0.9, 0.999, 1e-08, 0.01, 10


def _cp(*sem):
    return pltpu.CompilerParams(dimension_semantics=sem, vmem_limit_bytes=VMEM_LIMIT)


def _sigmoid(x):
    return 1.0 / (1.0 + jnp.exp(-x))


def _row(tm, n):
    return pl.BlockSpec((tm, n), lambda i: (i, 0))


def _const(shape):
    nd = len(shape)
    return pl.BlockSpec(shape, lambda i: (0,) * nd)


def _layer(shape, l):
    nd = len(shape)
    return pl.BlockSpec((None,) + tuple(shape), lambda i: (l,) + (0,) * nd)


def _dot(a, b):
    return jnp.dot(a, b, preferred_element_type=F32)


def _dot_nt(a, b):
    return lax.dot_general(a, b, (((1,), (1,)), ((), ())), preferred_element_type=F32)


def _dot_tn(a, b):
    return lax.dot_general(a, b, (((0,), (0,)), ((), ())), preferred_element_type=F32)


def _rms(x):
    return lax.rsqrt(jnp.mean(x * x, axis=-1, keepdims=True) + EPS)


def norm_mm_glu(h, g, l, w, b, name):
    T = h.shape[0]
    ns = w.shape[-1]

    def body(h_ref, g_ref, w_ref, b_ref, xn_ref, u_ref, a_ref):
        x = h_ref[...]
        xn = (x * _rms(x) * g_ref[...]).astype(BF16)
        xn_ref[...] = xn
        for s in range(2):
            lo, hi = s * ns, (s + 1) * ns
            u1 = _dot(xn, w_ref[s]) + b_ref[:, lo:hi]
            u2 = _dot(xn, w_ref[2 + s]) + b_ref[:, D + lo:D + hi]
            u_ref[:, lo:hi] = u1.astype(BF16)
            u_ref[:, D + lo:D + hi] = u2.astype(BF16)
            a_ref[:, lo:hi] = u1 * _sigmoid(u2)

    return pl.pallas_call(
        body, name=name, grid=(T // TM,),
        in_specs=[_row(TM, D), _layer((1, D), l), _layer((4, D, ns), l), _layer((1, 2 * D), l)],
        out_specs=[_row(TM, D), _row(TM, 2 * D), _row(TM, D)],
        out_shape=[jax.ShapeDtypeStruct((T, D), BF16), jax.ShapeDtypeStruct((T, 2 * D), BF16),
                   jax.ShapeDtypeStruct((T, D), F32)],
        compiler_params=_cp("parallel"),
    )(h, g, w, b)


def _conv_taps(buf_ref, w_ref, out_ref, first):
    RB, LB = 32, 512
    for r0 in range(0, TM, RB):
        for c0 in range(0, D, LB):
            acc = jnp.zeros((RB, LB), F32)
            for k in range(CONV_W):
                acc = acc + w_ref[k:k + 1, c0:c0 + LB] * buf_ref[pl.ds(first + k + r0, RB), c0:c0 + LB]
            out_ref[r0:r0 + RB, c0:c0 + LB] = acc


def dwconv_ln_silu(a, sm, l, name):
    T = a.shape[0]
    nb = TM // HALO

    def body(cur_ref, prev_ref, sm_ref, y_ref, s_ref, buf):
        i = pl.program_id(0)
        buf[0:HALO, :] = jnp.where(i > 0, prev_ref[...], 0.0)
        buf[HALO:HALO + TM, :] = cur_ref[...]
        _conv_taps(buf, sm_ref, y_ref, HALO - (CONV_W - 1))
        y = y_ref[...] + sm_ref[31:32, :]
        y_ref[...] = y
        mu = jnp.mean(y, axis=-1, keepdims=True)
        yc = y - mu
        rstd = lax.rsqrt(jnp.mean(yc * yc, axis=-1, keepdims=True) + EPS)
        z = yc * rstd * sm_ref[32:33, :] + sm_ref[33:34, :]
        s_ref[...] = (z * _sigmoid(z)).astype(BF16)

    return pl.pallas_call(
        body, name=name, grid=(T // TM,),
        in_specs=[_row(TM, D), pl.BlockSpec((HALO, D), lambda i: (jnp.maximum(i * nb - 1, 0), 0)),
                  _layer((40, D), l)],
        out_specs=[_row(TM, D), _row(TM, D)],
        out_shape=[jax.ShapeDtypeStruct((T, D), F32), jax.ShapeDtypeStruct((T, D), BF16)],
        scratch_shapes=[pltpu.VMEM((TM + HALO, D), F32)],
        compiler_params=_cp("parallel"),
    )(a, a, sm)


def mm_bias_res(xb, w, l, b, bl, res, name):
    T, K = xb.shape

    def body(x_ref, w_ref, b_ref, r_ref, o_ref):
        o_ref[...] = _dot(x_ref[...], w_ref[...]) + b_ref[...] + r_ref[...]

    return pl.pallas_call(
        body, name=name, grid=(T // TM,),
        in_specs=[_row(TM, K), _layer((K, D), l), _layer((1, D), bl), _row(TM, D)],
        out_specs=_row(TM, D), out_shape=jax.ShapeDtypeStruct((T, D), F32),
        compiler_params=_cp("parallel"),
    )(xb, w, b, res)


def norm_mm_swiglu(h, g, l, w, name):
    T = h.shape[0]
    ns = w.shape[-1]

    def body(h_ref, g_ref, w_ref, xn_ref, gu_ref, f_ref):
        x = h_ref[...]
        xn = (x * _rms(x) * g_ref[...]).astype(BF16)
        xn_ref[...] = xn
        for s in range(2):
            lo, hi = s * ns, (s + 1) * ns
            gate = _dot(xn, w_ref[s])
            up = _dot(xn, w_ref[2 + s])
            gu_ref[:, lo:hi] = gate.astype(BF16)
            gu_ref[:, DFF + lo:DFF + hi] = up.astype(BF16)
            f_ref[:, lo:hi] = (gate * _sigmoid(gate) * up).astype(BF16)

    return pl.pallas_call(
        body, name=name, grid=(T // TM,),
        in_specs=[_row(TM, D), _layer((1, D), l), _layer((4, D, ns), l)],
        out_specs=[_row(TM, D), _row(TM, 2 * DFF), _row(TM, DFF)],
        out_shape=[jax.ShapeDtypeStruct((T, D), BF16), jax.ShapeDtypeStruct((T, 2 * DFF), BF16),
                   jax.ShapeDtypeStruct((T, DFF), BF16)],
        compiler_params=_cp("parallel"),
    )(h, g, w)


def norm_mm(h, g, gl, w, l, name):
    T = h.shape[0]
    N = w.shape[-1]

    def body(h_ref, g_ref, w_ref, xn_ref, o_ref):
        x = h_ref[...]
        xn = (x * _rms(x) * g_ref[...]).astype(BF16)
        xn_ref[...] = xn
        o_ref[...] = _dot(xn, w_ref[...]).astype(BF16)

    return pl.pallas_call(
        body, name=name, grid=(T // TM,),
        in_specs=[_row(TM, D), _layer((1, D), gl), _layer((D, N), l)],
        out_specs=[_row(TM, D), _row(TM, N)],
        out_shape=[jax.ShapeDtypeStruct((T, D), BF16), jax.ShapeDtypeStruct((T, N), BF16)],
        compiler_params=_cp("parallel"),
    )(h, g, w)


def _attn_probs(q4, kb, bias, sink, n):
    s = _dot_nt(q4, kb) * (HD ** -0.5) + bias
    qi = lax.broadcasted_iota(jnp.int32, (GROUP * BLK, 2 * BLK), 0) % BLK
    kj = lax.broadcasted_iota(jnp.int32, (GROUP * BLK, 2 * BLK), 1)
    first_key = jnp.where(n > 0, 0, BLK)
    band = (kj >= first_key) & (((kj < BLK) & (kj > qi)) | ((kj >= BLK) & (kj - BLK <= qi)))
    s = jnp.where(band, s, NEG_INF)
    m = jnp.maximum(jnp.max(s, axis=-1, keepdims=True), sink)
    p = jnp.exp(s - m)
    es = jnp.exp(sink - m)
    inv = 1.0 / (jnp.sum(p, axis=-1, keepdims=True) + es)
    return p * inv, es * inv


def _attn_specs(T):
    qspec = pl.BlockSpec((None, GROUP, BLK, HD), lambda kv, n: (kv, 0, n, 0))
    kspec = pl.BlockSpec((None, T + BLK, HD), lambda kv, n: (kv, 0, 0))
    bspec = pl.BlockSpec((None, GROUP * BLK, 2 * BLK), lambda kv, n: (kv, 0, 0))
    sspec = pl.BlockSpec((None, GROUP * BLK, 1), lambda kv, n: (kv, 0, 0))
    return qspec, kspec, bspec, sspec


def attn_fwd(q, kp, vp, bias, sink, name):
    T = q.shape[2]
    qspec, kspec, bspec, sspec = _attn_specs(T)

    def body(q_ref, k_ref, v_ref, b_ref, s_ref, o_ref):
        n = pl.program_id(1)
        rows = pl.ds(pl.multiple_of(n * BLK, BLK), 2 * BLK)
        q4 = q_ref[...].reshape(GROUP * BLK, HD)
        probs, _ = _attn_probs(q4, k_ref[rows, :], b_ref[...], s_ref[...], n)
        o = _dot(probs.astype(BF16), v_ref[rows, :])
        o_ref[...] = o.reshape(GROUP, BLK, HD).astype(BF16)

    return pl.pallas_call(
        body, name=name, grid=(N_KV, T // BLK),
        in_specs=[qspec, kspec, kspec, bspec, sspec], out_specs=qspec,
        out_shape=jax.ShapeDtypeStruct((N_KV, GROUP, T, HD), BF16),
        compiler_params=_cp("parallel", "parallel"),
    )(q, kp, vp, bias, sink)


def attn_bwd(q, kp, vp, bias, sink, do, name):
    T = q.shape[2]
    qspec, kspec, bspec, sspec = _attn_specs(T)

    def body(q_ref, k_ref, v_ref, b_ref, s_ref, do_ref, dq_ref, dk_ref, dv_ref, db_ref, ds_ref):
        n = pl.program_id(1)

        @pl.when(n == 0)
        def _():
            dk_ref[...] = jnp.zeros_like(dk_ref)
            dv_ref[...] = jnp.zeros_like(dv_ref)
            db_ref[...] = jnp.zeros_like(db_ref)
            ds_ref[...] = jnp.zeros_like(ds_ref)

        rows = pl.ds(pl.multiple_of(n * BLK, BLK), 2 * BLK)
        q4 = q_ref[...].reshape(GROUP * BLK, HD)
        do4 = do_ref[...].reshape(GROUP * BLK, HD)
        kb = k_ref[rows, :]
        vb = v_ref[rows, :]
        probs, psink = _attn_probs(q4, kb, b_ref[...], s_ref[...], n)
        dp = _dot_nt(do4, vb)
        delta = jnp.sum(probs * dp, axis=-1, keepdims=True)
        dS = probs * (dp - delta)
        ds_ref[...] += -psink * delta
        db_ref[...] += dS
        dSb = dS.astype(BF16)
        dq_ref[...] = (_dot(dSb, kb) * (HD ** -0.5)).reshape(GROUP, BLK, HD).astype(BF16)
        dk_ref[rows, :] += _dot_tn(dSb, q4) * (HD ** -0.5)
        dv_ref[rows, :] += _dot_tn(probs.astype(BF16), do4)

    kout = pl.BlockSpec((None, T + BLK, HD), lambda kv, n: (kv, 0, 0))
    return pl.pallas_call(
        body, name=name, grid=(N_KV, T // BLK),
        in_specs=[qspec, kspec, kspec, bspec, sspec, qspec],
        out_specs=[qspec, kout, kout, bspec, sspec],
        out_shape=[jax.ShapeDtypeStruct((N_KV, GROUP, T, HD), BF16),
                   jax.ShapeDtypeStruct((N_KV, T + BLK, HD), F32), jax.ShapeDtypeStruct((N_KV, T + BLK, HD), F32),
                   jax.ShapeDtypeStruct((N_KV, GROUP * BLK, 2 * BLK), F32),
                   jax.ShapeDtypeStruct((N_KV, GROUP * BLK, 1), F32)],
        compiler_params=_cp("parallel", "arbitrary"),
    )(q, kp, vp, bias, sink, do)


def final_loss(h, g, target, name):
    T = h.shape[0]

    def body(h_ref, g_ref, t_ref, dh_ref, st_ref):
        i = pl.program_id(0)

        @pl.when(i == 0)
        def _():
            st_ref[...] = jnp.zeros_like(st_ref)

        x = h_ref[...]
        r = _rms(x)
        xh = x * r
        e = xh * g_ref[...] - t_ref[...]
        loss = 0.5 * jnp.sum(jnp.mean(e * e, axis=-1, keepdims=True))
        dy = e * (1.0 / D)
        st_ref[0:1, :] += jnp.sum(dy * xh, axis=0, keepdims=True)
        lane = lax.broadcasted_iota(jnp.int32, (1, D), 1)
        st_ref[1:2, :] += jnp.where(lane == 0, loss, 0.0)
        dxh = dy * g_ref[...]
        dh_ref[...] = r * (dxh - xh * jnp.mean(dxh * xh, axis=-1, keepdims=True))

    return pl.pallas_call(
        body, name=name, grid=(T // TM,),
        in_specs=[_row(TM, D), _const((1, D)), _row(TM, D)],
        out_specs=[_row(TM, D), _const((8, D))],
        out_shape=[jax.ShapeDtypeStruct((T, D), F32), jax.ShapeDtypeStruct((8, D), F32)],
        compiler_params=_cp("arbitrary"),
    )(h, g, target)


def mm_dw(x, dy, name, tn, slots, colsum=False):
    T, K = x.shape
    N = dy.shape[1]
    tt = min(T, 1024)
    nt = T // tt
    ns = N // slots
    per = ns // tn

    def body(x_ref, dy_ref, *rest):
        if colsum:
            dw_ref, cs_ref, acc, cacc = rest
        else:
            dw_ref, acc = rest
        t = pl.program_id(1)

        @pl.when(t == 0)
        def _():
            acc[...] = jnp.zeros_like(acc)
            if colsum:
                cacc[...] = jnp.zeros_like(cacc)

        dyv = dy_ref[...]
        acc[...] += _dot_tn(x_ref[...].astype(BF16), dyv.astype(BF16))
        if colsum:
            cacc[...] += jnp.sum(dyv.astype(F32), axis=0, keepdims=True)

        @pl.when(t == nt - 1)
        def _():
            dw_ref[...] = acc[...].astype(BF16)
            if colsum:
                cs_ref[...] = cacc[...]

    out_specs = [pl.BlockSpec((None, K, tn), lambda j, t: (j // per, 0, j % per))]
    out_shape = [jax.ShapeDtypeStruct((slots, K, ns), BF16)]
    scratch = [pltpu.VMEM((K, tn), F32)]
    if colsum:
        out_specs.append(pl.BlockSpec((1, tn), lambda j, t: (0, j)))
        out_shape.append(jax.ShapeDtypeStruct((1, N), F32))
        scratch.append(pltpu.VMEM((1, tn), F32))
    res = pl.pallas_call(
        body, name=name, grid=(N // tn, nt),
        in_specs=[pl.BlockSpec((tt, K), lambda j, t: (t, 0)), pl.BlockSpec((tt, tn), lambda j, t: (t, j))],
        out_specs=out_specs, out_shape=out_shape, scratch_shapes=scratch,
        compiler_params=_cp("parallel", "arbitrary"),
    )(x, dy)
    return tuple(res) if colsum else res[0]


def mmT_swiglu_bwd(dh, w, l, gu, name):
    T = dh.shape[0]
    half = DFF // 2

    def body(dh_ref, w_ref, gu_ref, du_ref):
        dhb = dh_ref[...].astype(BF16)
        for s in range(2):
            lo, hi = s * half, (s + 1) * half
            df = _dot_nt(dhb, w_ref[lo:hi, :])
            gate = gu_ref[:, lo:hi].astype(F32)
            up = gu_ref[:, DFF + lo:DFF + hi].astype(F32)
            sg = _sigmoid(gate)
            du_ref[:, lo:hi] = (df * up * sg * (1.0 + gate * (1.0 - sg))).astype(BF16)
            du_ref[:, DFF + lo:DFF + hi] = (df * gate * sg).astype(BF16)

    return pl.pallas_call(
        body, name=name, grid=(T // TM,),
        in_specs=[_row(TM, D), _layer((DFF, D), l), _row(TM, 2 * DFF)],
        out_specs=_row(TM, 2 * DFF), out_shape=jax.ShapeDtypeStruct((T, 2 * DFF), BF16),
        compiler_params=_cp("parallel"),
    )(dh, w, gu)


def mmT_rmsbwd(du, w, l, slots, h, g, gl, dh_in, name):
    T, N = du.shape
    ns = N // slots

    def body(du_ref, w_ref, h_ref, g_ref, di_ref, dh_ref, dg_ref):
        i = pl.program_id(0)

        @pl.when(i == 0)
        def _():
            dg_ref[...] = jnp.zeros_like(dg_ref)

        dxn = _dot_nt(du_ref[:, 0:ns], w_ref[0])
        for s in range(1, slots):
            dxn = dxn + _dot_nt(du_ref[:, s * ns:(s + 1) * ns], w_ref[s])
        x = h_ref[...]
        r = _rms(x)
        xh = x * r
        dg_ref[0:1, :] += jnp.sum(dxn * xh, axis=0, keepdims=True)
        dxh = dxn * g_ref[...]
        dh_ref[...] = di_ref[...] + r * (dxh - xh * jnp.mean(dxh * xh, axis=-1, keepdims=True))

    return pl.pallas_call(
        body, name=name, grid=(T // TM,),
        in_specs=[_row(TM, N), _layer((slots, D, ns), l), _row(TM, D), _layer((1, D), gl), _row(TM, D)],
        out_specs=[_row(TM, D), _const((8, D))],
        out_shape=[jax.ShapeDtypeStruct((T, D), F32), jax.ShapeDtypeStruct((8, D), F32)],
        compiler_params=_cp("arbitrary"),
    )(du, w, h, g, dh_in)


def mmT(dh, w, l, name):
    T = dh.shape[0]
    N = w.shape[1]

    def body(dh_ref, w_ref, o_ref):
        o_ref[...] = _dot_nt(dh_ref[...].astype(BF16), w_ref[...]).astype(BF16)

    return pl.pallas_call(
        body, name=name, grid=(T // TM,),
        in_specs=[_row(TM, D), _layer((N, D), l)],
        out_specs=_row(TM, N), out_shape=jax.ShapeDtypeStruct((T, N), BF16),
        compiler_params=_cp("parallel"),
    )(dh, w)


def mmT_lnbwd(dh, w, l, y, sm, name):
    T = dh.shape[0]

    def body(dh_ref, w_ref, y_ref, sm_ref, dy_ref, st_ref):
        i = pl.program_id(0)

        @pl.when(i == 0)
        def _():
            st_ref[...] = jnp.zeros_like(st_ref)

        ds = _dot_nt(dh_ref[...].astype(BF16), w_ref[...])
        y = y_ref[...]
        mu = jnp.mean(y, axis=-1, keepdims=True)
        yc = y - mu
        rstd = lax.rsqrt(jnp.mean(yc * yc, axis=-1, keepdims=True) + EPS)
        xh = yc * rstd
        gam = sm_ref[32:33, :]
        z = xh * gam + sm_ref[33:34, :]
        sg = _sigmoid(z)
        dz = ds * sg * (1.0 + z * (1.0 - sg))
        st_ref[0:1, :] += jnp.sum(dz * xh, axis=0, keepdims=True)
        st_ref[1:2, :] += jnp.sum(dz, axis=0, keepdims=True)
        dxh = dz * gam
        dy = rstd * (dxh - jnp.mean(dxh, axis=-1, keepdims=True) - xh * jnp.mean(dxh * xh, axis=-1, keepdims=True))
        st_ref[2:3, :] += jnp.sum(dy, axis=0, keepdims=True)
        dy_ref[...] = dy

    return pl.pallas_call(
        body, name=name, grid=(T // TM,),
        in_specs=[_row(TM, D), _layer((D, D), l), _row(TM, D), _layer((40, D), l)],
        out_specs=[_row(TM, D), _const((8, D))],
        out_shape=[jax.ShapeDtypeStruct((T, D), F32), jax.ShapeDtypeStruct((8, D), F32)],
        compiler_params=_cp("arbitrary"),
    )(dh, w, y, sm)


def dwconv_glu_bwd(dy, a, u, sm, smrev, l, name):
    T = dy.shape[0]
    nb = TM // HALO
    last = T // HALO - 1

    def body(dy_ref, dyn_ref, a_ref, ap_ref, u_ref, sm_ref, rev_ref, du_ref, dw_ref, bufd, bufa, da):
        i = pl.program_id(0)

        @pl.when(i == 0)
        def _():
            dw_ref[...] = jnp.zeros_like(dw_ref)

        bufd[0:TM, :] = dy_ref[...]
        bufd[TM:TM + HALO, :] = jnp.where(i < pl.num_programs(0) - 1, dyn_ref[...], 0.0)
        bufa[0:HALO, :] = jnp.where(i > 0, ap_ref[...], 0.0)
        bufa[HALO:HALO + TM, :] = a_ref[...]
        _conv_taps(bufd, rev_ref, da, 0)
        LB = 512
        for c0 in range(0, D, LB):
            for k in range(CONV_W):
                acc = jnp.zeros((8, LB), F32)
                for r0 in range(0, TM, 8):
                    acc = acc + dy_ref[r0:r0 + 8, c0:c0 + LB] * bufa[pl.ds(HALO - (CONV_W - 1) + k + r0, 8), c0:c0 + LB]
                dw_ref[k:k + 1, c0:c0 + LB] += jnp.sum(acc, axis=0, keepdims=True)
        dav = da[...]
        u1 = u_ref[:, 0:D].astype(F32)
        sg = _sigmoid(u_ref[:, D:2 * D].astype(F32))
        du_ref[:, 0:D] = (dav * sg).astype(BF16)
        du_ref[:, D:2 * D] = (dav * u1 * sg * (1.0 - sg)).astype(BF16)

    return pl.pallas_call(
        body, name=name, grid=(T // TM,),
        in_specs=[_row(TM, D), pl.BlockSpec((HALO, D), lambda i: (jnp.minimum((i + 1) * nb, last), 0)),
                  _row(TM, D), pl.BlockSpec((HALO, D), lambda i: (jnp.maximum(i * nb - 1, 0), 0)),
                  _row(TM, 2 * D), _layer((40, D), l), _layer((40, D), l)],
        out_specs=[_row(TM, 2 * D), _const((32, D))],
        out_shape=[jax.ShapeDtypeStruct((T, 2 * D), BF16), jax.ShapeDtypeStruct((32, D), F32)],
        scratch_shapes=[pltpu.VMEM((TM + HALO, D), F32), pltpu.VMEM((TM + HALO, D), F32), pltpu.VMEM((TM, D), F32)],
        compiler_params=_cp("arbitrary"),
    )(dy, dy, a, a, u, sm, smrev)


def _rows_tile(R):
    for t in (512, 256, 128, 64, 32, 16, 8):
        if R % t == 0:
            return t
    return R


def add_n(xs, out_dtype, name):
    R, C = xs[0].shape
    tr = _rows_tile(R)

    def body(*refs):
        acc = refs[0][...].astype(F32)
        for r in refs[1:-1]:
            acc = acc + r[...].astype(F32)
        refs[-1][...] = acc.astype(out_dtype)

    return pl.pallas_call(
        body, name=name, grid=(R // tr,),
        in_specs=[_row(tr, C)] * len(xs), out_specs=_row(tr, C),
        out_shape=jax.ShapeDtypeStruct((R, C), out_dtype), compiler_params=_cp("parallel"),
    )(*xs)


def adamw(w, g, m, v, name):
    R, C = w.shape
    tr = _rows_tile(R)

    def body(w_ref, g_ref, m_ref, v_ref, d_ref, nm_ref, nv_ref):
        gv = g_ref[...]
        nm = ADAM_B1 * m_ref[...] + (1.0 - ADAM_B1) * gv
        nv = ADAM_B2 * v_ref[...] + (1.0 - ADAM_B2) * (gv * gv)
        m_hat = nm / (1.0 - ADAM_B1 ** ADAM_STEP)
        v_hat = nv / (1.0 - ADAM_B2 ** ADAM_STEP)
        d_ref[...] = -ADAM_LR * (m_hat / (jnp.sqrt(v_hat) + ADAM_EPS) + ADAM_WD * w_ref[...])
        nm_ref[...] = nm
        nv_ref[...] = nv

    sd = jax.ShapeDtypeStruct((R, C), F32)
    return pl.pallas_call(
        body, name=name, grid=(R // tr,),
        in_specs=[_row(tr, C)] * 4, out_specs=[_row(tr, C)] * 3, out_shape=[sd, sd, sd],
        compiler_params=_cp("parallel"),
    )(w, g, m, v)


ANY = pl.BlockSpec(memory_space=pl.ANY)


def _place():
    x, y, c = lax.axis_index("x"), lax.axis_index("y"), lax.axis_index("c")
    chips = [(1 - x, y), (x, 1 - y), (1 - x, 1 - y)]
    return x, y, c, chips


def _half(ref, hc):
    lh = ref.shape[0] // 2
    return ref.at[pl.ds(hc * lh, lh)]


def allgather_weights(shards):
    nt = len(shards)

    def body(*refs):
        ins, outs = refs[:nt], refs[nt:2 * nt]
        send, recv, local = refs[2 * nt:]
        x, y, c, chips = _place()
        me = 2 * x + y

        def slot(ref, hc, s):
            return _half(ref, hc).at[:, pl.ds(s, 1)]

        def rc(k, src, dst, to):
            return pltpu.make_async_remote_copy(src_ref=src, dst_ref=dst, send_sem=send.at[k], recv_sem=recv.at[k],
                                                device_id=to, device_id_type=MESH)

        mine, first, passed = [], [], []
        for t in range(nt):
            cp = pltpu.make_async_copy(ins[t], outs[t].at[:, pl.ds(me, 1)], local.at[t])
            cp.start()
            mine.append(cp)
            for j, (cx, cy) in enumerate(chips):
                cp = rc(6 * t + j, _half(ins[t], c), slot(outs[t], c, me), (cx, cy, c))
                cp.start()
                first.append(cp)
        for t in range(nt):
            for j, (cx, cy) in enumerate(chips):
                got = slot(outs[t], c, 2 * cx + cy)
                rc(6 * t + j, got, got, (cx, cy, c)).wait_recv()
                cp = rc(6 * t + 3 + j, got, got, (x, y, 1 - c))
                cp.start()
                passed.append(cp)
        for t in range(nt):
            for j, (cx, cy) in enumerate(chips):
                got = slot(outs[t], 1 - c, 2 * cx + cy)
                rc(6 * t + 3 + j, got, got, (x, y, 1 - c)).wait_recv()
        for cp in first + passed:
            cp.wait_send()
        for cp in mine:
            cp.wait()

    out_shape = [jax.ShapeDtypeStruct((s.shape[0], 4) + s.shape[2:], s.dtype) for s in shards]
    return pl.pallas_call(
        body, name="allgather_weights", in_specs=[ANY] * nt, out_specs=[ANY] * nt, out_shape=out_shape,
        scratch_shapes=[pltpu.SemaphoreType.DMA((6 * nt,)), pltpu.SemaphoreType.DMA((6 * nt,)),
                        pltpu.SemaphoreType.DMA((nt,))],
        compiler_params=pltpu.CompilerParams(has_side_effects=True),
    )(*shards)


def sibling_swap_halves(gs):
    nt = len(gs)

    def body(*refs):
        ins, outs = refs[:nt], refs[nt:2 * nt]
        send, recv = refs[2 * nt:]
        x, y, c, _ = _place()
        cps = []
        for t in range(nt):
            cp = pltpu.make_async_remote_copy(src_ref=_half(ins[t], 1 - c), dst_ref=outs[t], send_sem=send.at[t],
                                              recv_sem=recv.at[t], device_id=(x, y, 1 - c), device_id_type=MESH)
            cp.start()
            cps.append(cp)
        for cp in cps:
            cp.wait()

    out_shape = [jax.ShapeDtypeStruct((g.shape[0] // 2,) + g.shape[1:], g.dtype) for g in gs]
    return pl.pallas_call(
        body, name="sibling_swap_halves", in_specs=[ANY] * nt, out_specs=[ANY] * nt, out_shape=out_shape,
        scratch_shapes=[pltpu.SemaphoreType.DMA((nt,)), pltpu.SemaphoreType.DMA((nt,))],
        compiler_params=pltpu.CompilerParams(has_side_effects=True),
    )(*gs)


def chip_scatter(ps):
    nt = len(ps)

    def body(*refs):
        ins, outs = refs[:nt], refs[nt:2 * nt]
        send, recv = refs[2 * nt:]
        x, y, c, chips = _place()
        cps = []
        for t in range(nt):
            for j, (cx, cy) in enumerate(chips):
                cp = pltpu.make_async_remote_copy(
                    src_ref=ins[t].at[:, pl.ds(2 * cx + cy, 1)], dst_ref=outs[t].at[j], send_sem=send.at[3 * t + j],
                    recv_sem=recv.at[3 * t + j], device_id=(cx, cy, c), device_id_type=MESH)
                cp.start()
                cps.append(cp)
        for cp in cps:
            cp.wait()

    out_shape = [jax.ShapeDtypeStruct((3, p.shape[0], 1) + p.shape[2:], p.dtype) for p in ps]
    return pl.pallas_call(
        body, name="chip_scatter", in_specs=[ANY] * nt, out_specs=[ANY] * nt, out_shape=out_shape,
        scratch_shapes=[pltpu.SemaphoreType.DMA((3 * nt,)), pltpu.SemaphoreType.DMA((3 * nt,))],
        compiler_params=pltpu.CompilerParams(has_side_effects=True),
    )(*ps)


def sibling_join(rs):
    nt = len(rs)

    def body(*refs):
        ins, outs = refs[:nt], refs[nt:2 * nt]
        send, recv, local = refs[2 * nt:]
        x, y, c, _ = _place()
        cps, mine = [], []
        for t in range(nt):
            cp = pltpu.make_async_copy(ins[t], _half(outs[t], c), local.at[t])
            cp.start()
            mine.append(cp)
            cp = pltpu.make_async_remote_copy(src_ref=ins[t], dst_ref=_half(outs[t], c), send_sem=send.at[t],
                                              recv_sem=recv.at[t], device_id=(x, y, 1 - c), device_id_type=MESH)
            cp.start()
            cps.append(cp)
        for t in range(nt):
            cps[t].wait_send()
            got = _half(outs[t], 1 - c)
            pltpu.make_async_remote_copy(src_ref=ins[t], dst_ref=got, send_sem=send.at[t], recv_sem=recv.at[t],
                                         device_id=(x, y, 1 - c), device_id_type=MESH).wait_recv()
        for cp in mine:
            cp.wait()

    out_shape = [jax.ShapeDtypeStruct((2 * r.shape[0],) + r.shape[1:], r.dtype) for r in rs]
    return pl.pallas_call(
        body, name="sibling_join", in_specs=[ANY] * nt, out_specs=[ANY] * nt, out_shape=out_shape,
        scratch_shapes=[pltpu.SemaphoreType.DMA((nt,)), pltpu.SemaphoreType.DMA((nt,)),
                        pltpu.SemaphoreType.DMA((nt,))],
        compiler_params=pltpu.CompilerParams(has_side_effects=True),
    )(*rs)


def allreduce_small(v):
    R = v.shape[0]

    def body(v_ref, o_ref, all_ref, send, recv):
        x, y, c, _ = _place()
        me = 4 * x + 2 * y + c
        all_ref[me] = v_ref[...]

        def peer(k):
            return ((1 - x) if k & 4 else x, (1 - y) if k & 2 else y, (1 - c) if k & 1 else c)

        cps = []
        for k in range(1, 8):
            px, py, pc = peer(k)
            cp = pltpu.make_async_remote_copy(src_ref=v_ref, dst_ref=all_ref.at[me], send_sem=send.at[k - 1],
                                              recv_sem=recv.at[k - 1], device_id=(px, py, pc), device_id_type=MESH)
            cp.start()
            cps.append(cp)
        for k in range(1, 8):
            px, py, pc = peer(k)
            src = 4 * px + 2 * py + pc
            pltpu.make_async_remote_copy(src_ref=v_ref, dst_ref=all_ref.at[src], send_sem=send.at[k - 1],
                                         recv_sem=recv.at[k - 1], device_id=(px, py, pc),
                                         device_id_type=MESH).wait_recv()
        for cp in cps:
            cp.wait_send()
        acc = all_ref[0]
        for d in range(1, 8):
            acc = acc + all_ref[d]
        o_ref[...] = acc

    return pl.pallas_call(
        body, name="allreduce_small",
        in_specs=[pl.BlockSpec(memory_space=pltpu.VMEM)], out_specs=pl.BlockSpec(memory_space=pltpu.VMEM),
        out_shape=jax.ShapeDtypeStruct((R, D), F32),
        scratch_shapes=[pltpu.VMEM((8, R, D), F32), pltpu.SemaphoreType.DMA((7,)), pltpu.SemaphoreType.DMA((7,))],
        compiler_params=pltpu.CompilerParams(has_side_effects=True, vmem_limit_bytes=VMEM_LIMIT),
    )(v)


def _bucket_table():
    qi = np.arange(BLK)[:, None]
    kj = np.arange(2 * BLK)[None, :]
    d = np.maximum(qi + BLK - kj, 0)
    max_exact = N_BUCKETS // 2
    log_ratio = (np.log(np.maximum(d, 1).astype(np.float32) / np.float32(max_exact))
                 / np.float32(math.log(MAX_DISTANCE / max_exact))).astype(np.float32)
    large = max_exact + (log_ratio * np.float32(N_BUCKETS - max_exact)).astype(np.int32)
    large = np.minimum(large, N_BUCKETS - 1)
    return np.where(d < max_exact, d, large).astype(np.int32)


def _heads_major(a, nh):
    T = a.shape[0]
    return a.reshape(T, nh, HD).transpose(1, 0, 2)


def _heads_minor(a):
    nh, T, _ = a.shape
    return a.transpose(1, 0, 2).reshape(T, nh * HD)


def local_step(x, target, W, small):
    T = x.shape[0]
    h = x
    saved = []
    for l in range(2):
        xn, u, a = norm_mm_glu(h, small["norm_mix"], l, W["pw1"], small["b_pw1"], f"f_pw1_{l}")
        y, s = dwconv_ln_silu(a, small["conv"], l, f"f_conv_{l}")
        h1 = mm_bias_res(s, W["pw2"], l, small["b_pw2"], l, h, f"f_pw2_{l}")
        xn2, gu, f = norm_mm_swiglu(h1, small["norm_ffn"], l, W["up"], f"f_up_{l}")
        h2 = mm_bias_res(f, W["down"], l, small["zero"], 0, h1, f"f_down_{l}")
        saved.append(dict(h=h, xn=xn, u=u, a=a, y=y, s=s, h1=h1, xn2=xn2, gu=gu, f=f))
        h = h2
    h_kv = h
    kvn, kv = norm_mm(h, small["norm_kv"], 0, W["kv"], 0, "f_kv")
    kp = jnp.pad(_heads_major(kv[:, :N_KV * HD], N_KV), ((0, 0), (BLK, 0), (0, 0)))
    vp = jnp.pad(_heads_major(kv[:, N_KV * HD:], N_KV), ((0, 0), (BLK, 0), (0, 0)))
    bucket = _bucket_table()
    bias = small["rel_bias"][bucket]
    bias = jnp.transpose(bias, (2, 0, 1)).reshape(N_KV, GROUP * BLK, 2 * BLK)
    for j in range(2):
        l = 2 + j
        xn, q = norm_mm(h, small["norm_mix"], l, W["wq"], j, f"f_q_{j}")
        qh = _heads_major(q, N_HEADS).reshape(N_KV, GROUP, T, HD)
        sink = jnp.broadcast_to(small["sinks"][j].reshape(N_KV, GROUP, 1, 1), (N_KV, GROUP, BLK, 1))
        sink = sink.reshape(N_KV, GROUP * BLK, 1)
        oh = attn_fwd(qh, kp, vp, bias, sink, f"f_attn_{j}")
        attn = _heads_minor(oh.reshape(N_HEADS, T, HD))
        h1 = mm_bias_res(attn, W["wo"], j, small["zero"], 0, h, f"f_wo_{j}")
        xn2, gu, f = norm_mm_swiglu(h1, small["norm_ffn"], l, W["up"], f"f_up_{l}")
        h2 = mm_bias_res(f, W["down"], l, small["zero"], 0, h1, f"f_down_{l}")
        saved.append(dict(h=h, xn=xn, qh=qh, sink=sink, attn=attn, h1=h1, xn2=xn2, gu=gu, f=f))
        h = h2

    dh, st_final = final_loss(h, small["norm_final"], target, "loss_head")

    G = dict(up=[None] * 4, down=[None] * 4, pw1=[None] * 2, pw2=[None] * 2, wq=[None] * 2, wo=[None] * 2)
    S = dict(norm_ffn=[None] * 4, norm_mix=[None] * 4, conv=[None] * 2, taps=[None] * 2, b_pw1=[None] * 2,
             b_pw2=[None] * 2, sinks=[None] * 2)

    def ffn_bwd(dh, sv, l):
        du = mmT_swiglu_bwd(dh, W["down"], l, sv["gu"], f"b_down_{l}")
        G["down"][l] = mm_dw(sv["f"], dh, f"w_down_{l}", 512, 1)
        G["up"][l] = mm_dw(sv["xn2"], du, f"w_up_{l}", DFF // 2, 4)
        dh, dg = mmT_rmsbwd(du, W["up"], l, 4, sv["h1"], small["norm_ffn"], l, dh, f"b_up_{l}")
        S["norm_ffn"][l] = dg[0]
        return dh

    dk = dv = dbias = None
    for j in (1, 0):
        l = 2 + j
        sv = saved[l]
        dh = ffn_bwd(dh, sv, l)
        dattn = mmT(dh, W["wo"], j, f"b_wo_{j}")
        G["wo"][j] = mm_dw(sv["attn"], dh, f"w_wo_{j}", 512, 1)
        doh = _heads_major(dattn, N_HEADS).reshape(N_KV, GROUP, T, HD)
        dqh, dkj, dvj, dbj, dsj = attn_bwd(sv["qh"], kp, vp, bias, sv["sink"], doh, f"b_attn_{j}")
        dq = _heads_minor(dqh.reshape(N_HEADS, T, HD))
        G["wq"][j] = mm_dw(sv["xn"], dq, f"w_q_{j}", 512, 1)
        dh, dg = mmT_rmsbwd(dq, W["wq"].reshape(2, 1, D, D), j, 1, sv["h"], small["norm_mix"], l, dh, f"b_q_{j}")
        S["norm_mix"][l] = dg[0]
        S["sinks"][j] = jnp.sum(dsj.reshape(N_HEADS, BLK), axis=1)
        dk = dkj if dk is None else dk + dkj
        dv = dvj if dv is None else dv + dvj
        dbias = dbj if dbias is None else dbias + dbj

    dkv = jnp.concatenate([_heads_minor(dk[:, BLK:]), _heads_minor(dv[:, BLK:])], axis=1).astype(BF16)
    G["kv"] = mm_dw(kvn, dkv, "w_kv", 512, 1)
    dh, dg = mmT_rmsbwd(dkv, W["kv"].reshape(1, 1, D, 2 * N_KV * HD), 0, 1, h_kv, small["norm_kv"], 0, dh, "b_kv")
    S["norm_kv"] = dg[0]
    onehot = jnp.asarray(np.eye(N_BUCKETS, dtype=np.float32)[bucket])
    dbh = dbias.reshape(N_HEADS, BLK, 2 * BLK)
    S["rel_bias"] = jnp.einsum("hqk,qkb->bh", dbh, onehot, precision=lax.Precision.HIGHEST)

    for l in (1, 0):
        sv = saved[l]
        dh = ffn_bwd(dh, sv, l)
        dy, st = mmT_lnbwd(dh, W["pw2"], l, sv["y"], small["conv"], f"b_pw2_{l}")
        G["pw2"][l], S["b_pw2"][l] = mm_dw(sv["s"], dh, f"w_pw2_{l}", 512, 1, colsum=True)
        du, dtaps = dwconv_glu_bwd(dy, sv["a"], sv["u"], small["conv"], small["conv_rev"], l, f"b_conv_{l}")
        S["conv"][l] = st[0:3]
        S["taps"][l] = dtaps[0:CONV_W]
        G["pw1"][l], S["b_pw1"][l] = mm_dw(sv["xn"], du, f"w_pw1_{l}", 512, 4, colsum=True)
        dh, dg = mmT_rmsbwd(du, W["pw1"], l, 4, sv["h"], small["norm_mix"], l, dh, f"b_pw1_{l}")
        S["norm_mix"][l] = dg[0]
    S["norm_final"] = st_final[0]
    return st_final[1, 0], dh, G, S


R_CONV = 37
R_SMALL = 88


def _pack_small(S):
    rows = []
    for l in range(2):
        rows += [S["taps"][l], S["conv"][l][2:3], S["conv"][l][0:2], S["b_pw2"][l], S["b_pw1"][l].reshape(2, D)]
    rows += [jnp.stack(S["norm_mix"]), jnp.stack(S["norm_ffn"]), S["norm_kv"][None], S["norm_final"][None]]
    tail = jnp.concatenate([jnp.stack(S["sinks"]).reshape(-1), S["rel_bias"].reshape(-1)])
    rows.append(jnp.pad(tail, (0, D - tail.shape[0]))[None])
    v = jnp.concatenate(rows, axis=0)
    return jnp.pad(v, ((0, R_SMALL - v.shape[0]), (0, 0)))


def kernel(x, norm_mix, norm_ffn, conv_w_pw1, conv_b_pw1, conv_w_dw, conv_b_dw, conv_ln_g, conv_ln_b, conv_w_pw2, conv_b_pw2, norm_kv, w_kv, w_q, w_o, sinks, rel_bias, ffn_w_up, ffn_w_down, norm_final, loss_target, m_norm_mix, m_norm_ffn, m_conv_w_pw1, m_conv_b_pw1, m_conv_w_dw, m_conv_b_dw, m_conv_ln_g, m_conv_ln_b, m_conv_w_pw2, m_conv_b_pw2, m_norm_kv, m_w_kv, m_w_q, m_w_o, m_sinks, m_rel_bias, m_ffn_w_up, m_ffn_w_down, m_norm_final, v_norm_mix, v_norm_ffn, v_conv_w_pw1, v_conv_b_pw1, v_conv_w_dw, v_conv_b_dw, v_conv_ln_g, v_conv_ln_b, v_conv_w_pw2, v_conv_b_pw2, v_norm_kv, v_w_kv, v_w_q, v_w_o, v_sinks, v_rel_bias, v_ffn_w_up, v_ffn_w_down, v_norm_final):
    xi, yi, ci = lax.axis_index("x"), lax.axis_index("y"), lax.axis_index("c")
    me = 2 * xi + yi
    T = x.shape[1]
    weights = dict(norm_mix=norm_mix, norm_ffn=norm_ffn, conv_w_pw1=conv_w_pw1, conv_b_pw1=conv_b_pw1,
                   conv_w_dw=conv_w_dw, conv_b_dw=conv_b_dw, conv_ln_g=conv_ln_g, conv_ln_b=conv_ln_b,
                   conv_w_pw2=conv_w_pw2, conv_b_pw2=conv_b_pw2, norm_kv=norm_kv, w_kv=w_kv, w_q=w_q, w_o=w_o,
                   sinks=sinks, rel_bias=rel_bias, ffn_w_up=ffn_w_up, ffn_w_down=ffn_w_down, norm_final=norm_final)
    mom_m = dict(norm_mix=m_norm_mix, norm_ffn=m_norm_ffn, conv_w_pw1=m_conv_w_pw1, conv_b_pw1=m_conv_b_pw1,
                 conv_w_dw=m_conv_w_dw, conv_b_dw=m_conv_b_dw, conv_ln_g=m_conv_ln_g, conv_ln_b=m_conv_ln_b,
                 conv_w_pw2=m_conv_w_pw2, conv_b_pw2=m_conv_b_pw2, norm_kv=m_norm_kv, w_kv=m_w_kv, w_q=m_w_q,
                 w_o=m_w_o, sinks=m_sinks, rel_bias=m_rel_bias, ffn_w_up=m_ffn_w_up, ffn_w_down=m_ffn_w_down,
                 norm_final=m_norm_final)
    mom_v = dict(norm_mix=v_norm_mix, norm_ffn=v_norm_ffn, conv_w_pw1=v_conv_w_pw1, conv_b_pw1=v_conv_b_pw1,
                 conv_w_dw=v_conv_w_dw, conv_b_dw=v_conv_b_dw, conv_ln_g=v_conv_ln_g, conv_ln_b=v_conv_ln_b,
                 conv_w_pw2=v_conv_w_pw2, conv_b_pw2=v_conv_b_pw2, norm_kv=v_norm_kv, w_kv=v_w_kv, w_q=v_w_q,
                 w_o=v_w_o, sinks=v_sinks, rel_bias=v_rel_bias, ffn_w_up=v_ffn_w_up, ffn_w_down=v_ffn_w_down,
                 norm_final=v_norm_final)

    big = ["conv_w_pw1", "conv_w_pw2", "w_q", "w_o", "ffn_w_up", "ffn_w_down", "w_kv"]

    def slot_view(a):
        if a.ndim == 2:
            return a.reshape(2, 1, a.shape[0] // 2, a.shape[1])
        return a.reshape(a.shape[0], 1, a.shape[1], a.shape[2])

    smallpack = jnp.concatenate(
        [conv_w_dw, conv_b_dw[:, None], conv_ln_g[:, None], conv_ln_b[:, None], conv_b_pw2[:, None],
         conv_b_pw1.reshape(2, 2, 256), jnp.zeros((2, 3, 256), F32)], axis=1)
    shards = [slot_view(weights[n].astype(BF16)) for n in big] + [smallpack.reshape(2, 1, 40, 256)]
    fulls = allgather_weights(shards)
    f_pw1, f_pw2, f_wq, f_wo, f_up, f_down, f_kv, f_small = fulls
    W = dict(pw1=f_pw1, pw2=f_pw2.reshape(2, D, D), wq=f_wq.reshape(2, D, D), wo=f_wo.reshape(2, D, D),
             up=f_up, down=f_down.reshape(4, DFF, D),
             kv=f_kv.reshape(2, 4, 128, 512).transpose(1, 0, 2, 3).reshape(1, D, 512))
    fs = f_small.transpose(0, 2, 1, 3).reshape(2, 40, D)
    b_pw1_full = f_small[:, :, 35:37, :].transpose(0, 1, 2, 3).reshape(2, 1, 2 * D)
    conv_sm = fs.at[:, 34:].set(0.0)
    conv_rev = jnp.concatenate([fs[:, CONV_W - 1::-1][:, :CONV_W], jnp.zeros((2, 40 - CONV_W, D), F32)], axis=1)
    small = dict(norm_mix=norm_mix[:, None], norm_ffn=norm_ffn[:, None], norm_kv=norm_kv[None, None],
                 norm_final=norm_final[None], conv=conv_sm, conv_rev=conv_rev, b_pw1=b_pw1_full,
                 b_pw2=fs[:, 34:35], zero=jnp.zeros((1, 1, D), F32), sinks=sinks, rel_bias=rel_bias)

    loss_part, grad_x, G, S = local_step(x[0], loss_target[0], W, small)
    loss = lax.psum(loss_part, ("x", "y", "c"))

    gkv = G["kv"].reshape(4, 2, 128, 512).transpose(1, 0, 2, 3)
    gs = [jnp.stack(G["pw1"]), jnp.stack(G["pw2"]).reshape(2, 4, 256, D), jnp.stack(G["wq"]).reshape(2, 4, 256, D),
          jnp.stack(G["wo"]).reshape(2, 4, 256, D), jnp.stack(G["up"]), jnp.stack(G["down"]).reshape(4, 4, 704, D),
          gkv]
    theirs = sibling_swap_halves(gs)
    parts = []
    for t, (g, o) in enumerate(zip(gs, theirs)):
        lh = g.shape[0] // 2
        own = lax.dynamic_slice_in_dim(g, ci * lh, lh, axis=0)
        r2 = (lh * 4 * g.shape[2], g.shape[3])
        parts.append(add_n([own.reshape(r2), o.reshape(r2)], BF16, f"rs_add1_{t}").reshape(own.shape))
    got = chip_scatter(parts)
    halves = []
    for t, (p, o) in enumerate(zip(parts, got)):
        own = lax.dynamic_slice_in_dim(p, me, 1, axis=1)
        r2 = (own.shape[0] * own.shape[2], own.shape[3])
        halves.append(add_n([own.reshape(r2), o[0].reshape(r2), o[1].reshape(r2), o[2].reshape(r2)], F32,
                            f"rs_add2_{t}").reshape(own.shape))
    reduced = sibling_join(halves)

    vsum = allreduce_small(_pack_small(S))
    col = lambda a: lax.dynamic_slice_in_dim(a, me * 256, 256, axis=-1)
    grads = {}
    for l in range(2):
        base = l * R_CONV
        grads.setdefault("conv_w_dw", []).append(col(vsum[base:base + 31]))
        grads.setdefault("conv_b_dw", []).append(col(vsum[base + 31]))
        grads.setdefault("conv_ln_g", []).append(col(vsum[base + 32]))
        grads.setdefault("conv_ln_b", []).append(col(vsum[base + 33]))
        grads.setdefault("conv_b_pw2", []).append(col(vsum[base + 34]))
        grads.setdefault("conv_b_pw1", []).append(
            lax.dynamic_slice_in_dim(vsum[base + 35:base + 37].reshape(2 * D), me * 512, 512, axis=0))
    grads = {k: jnp.stack(v) for k, v in grads.items()}
    base = 2 * R_CONV
    grads["norm_mix"] = vsum[base:base + 4]
    grads["norm_ffn"] = vsum[base + 4:base + 8]
    grads["norm_kv"] = vsum[base + 8]
    grads["norm_final"] = vsum[base + 9]
    grads["sinks"] = vsum[base + 10, 0:32].reshape(2, 16)
    grads["rel_bias"] = vsum[base + 10, 32:32 + 512].reshape(32, 16)
    for n, r in zip(big, reduced):
        grads[n] = r.reshape(weights[n].shape)

    delta, new_m, new_v = {}, {}, {}
    for n in big:
        shp = weights[n].shape
        r2 = (int(np.prod(shp[:-1])), shp[-1])
        d, nm, nv = adamw(weights[n].reshape(r2), grads[n].reshape(r2), mom_m[n].reshape(r2), mom_v[n].reshape(r2),
                          f"adamw_{n}")
        delta[n], new_m[n], new_v[n] = d.reshape(shp), nm.reshape(shp), nv.reshape(shp)
    rest = [n for n in weights if n not in big]

    def pack(dct):
        flat = jnp.concatenate([dct[n].reshape(-1) for n in rest])
        return jnp.pad(flat, (0, (-flat.shape[0]) % (8 * 128))).reshape(-1, 128)

    d, nm, nv = adamw(pack(weights), pack(grads), pack(mom_m), pack(mom_v), "adamw_small")
    off = 0
    for n in rest:
        shp = weights[n].shape
        sz = int(np.prod(shp))
        delta[n] = d.reshape(-1)[off:off + sz].reshape(shp)
        new_m[n] = nm.reshape(-1)[off:off + sz].reshape(shp)
        new_v[n] = nv.reshape(-1)[off:off + sz].reshape(shp)
        off += sz

    order = list(weights)
    return (loss, grad_x[None], *[grads[n] for n in order], *[delta[n] for n in order],
            *[new_m[n] for n in order], *[new_v[n] for n in order])
```

```python
import functools
import math

import numpy as np
import jax
import jax.numpy as jnp
from jax import lax
from jax.experimental import pallas as pl
from jax.experimental.pallas import tpu as pltpu

F32 = jnp.float32
BF16 = jnp.bfloat16
MESH = pl.DeviceIdType.MESH

D = 1024
DFF = 2816
N_HEADS = 16
N_KV = 4
GROUP = 4
HD = 64
BLK = 128
CONV_W = 31
HALO = 32
N_BUCKETS = 32
MAX_DISTANCE = 128
EPS = 1e-6
NEG_INF = -1e30
TM = 256
VMEM_LIMIT = 56 * 2 ** 20

ADAM_LR, ADAM_B1, ADAM_B2, ADAM_EPS, ADAM_WD, ADAM_STEP = 0.001, 0.9, 0.999, 1e-08, 0.01, 10


def _cp(*sem):
    return pltpu.CompilerParams(dimension_semantics=sem, vmem_limit_bytes=VMEM_LIMIT)


def _sigmoid(x):
    return 1.0 / (1.0 + jnp.exp(-x))


def _row(tm, n):
    return pl.BlockSpec((tm, n), lambda i: (i, 0))


def _const(shape):
    nd = len(shape)
    return pl.BlockSpec(shape, lambda i: (0,) * nd)


def _layer(shape, l):
    nd = len(shape)
    return pl.BlockSpec((None,) + tuple(shape), lambda i: (l,) + (0,) * nd)


def _dot(a, b):
    return jnp.dot(a, b, preferred_element_type=F32)


def _dot_nt(a, b):
    return lax.dot_general(a, b, (((1,), (1,)), ((), ())), preferred_element_type=F32)


def _dot_tn(a, b):
    return lax.dot_general(a, b, (((0,), (0,)), ((), ())), preferred_element_type=F32)


def _rms(x):
    return lax.rsqrt(jnp.mean(x * x, axis=-1, keepdims=True) + EPS)


def norm_mm_glu(h, g, l, w, b, name):
    T = h.shape[0]
    ns = w.shape[-1]

    def body(h_ref, g_ref, w_ref, b_ref, xn_ref, u_ref, a_ref):
        x = h_ref[...]
        xn = (x * _rms(x) * g_ref[...]).astype(BF16)
        xn_ref[...] = xn
        for s in range(2):
            lo, hi = s * ns, (s + 1) * ns
            u1 = _dot(xn, w_ref[s]) + b_ref[:, lo:hi]
            u2 = _dot(xn, w_ref[2 + s]) + b_ref[:, D + lo:D + hi]
            u_ref[:, lo:hi] = u1.astype(BF16)
            u_ref[:, D + lo:D + hi] = u2.astype(BF16)
            a_ref[:, lo:hi] = u1 * _sigmoid(u2)

    return pl.pallas_call(
        body, name=name, grid=(T // TM,),
        in_specs=[_row(TM, D), _layer((1, D), l), _layer((4, D, ns), l), _layer((1, 2 * D), l)],
        out_specs=[_row(TM, D), _row(TM, 2 * D), _row(TM, D)],
        out_shape=[jax.ShapeDtypeStruct((T, D), BF16), jax.ShapeDtypeStruct((T, 2 * D), BF16),
                   jax.ShapeDtypeStruct((T, D), F32)],
        compiler_params=_cp("parallel"),
    )(h, g, w, b)


def _conv_taps(buf_ref, w_ref, out_ref, first):
    RB, LB = 32, 512
    for r0 in range(0, TM, RB):
        for c0 in range(0, D, LB):
            acc = jnp.zeros((RB, LB), F32)
            for k in range(CONV_W):
                acc = acc + w_ref[k:k + 1, c0:c0 + LB] * buf_ref[pl.ds(first + k + r0, RB), c0:c0 + LB]
            out_ref[r0:r0 + RB, c0:c0 + LB] = acc


def dwconv_ln_silu(a, sm, l, name):
    T = a.shape[0]
    nb = TM // HALO

    def body(cur_ref, prev_ref, sm_ref, y_ref, s_ref, buf):
        i = pl.program_id(0)
        buf[0:HALO, :] = jnp.where(i > 0, prev_ref[...], 0.0)
        buf[HALO:HALO + TM, :] = cur_ref[...]
        _conv_taps(buf, sm_ref, y_ref, HALO - (CONV_W - 1))
        y = y_ref[...] + sm_ref[31:32, :]
        y_ref[...] = y
        mu = jnp.mean(y, axis=-1, keepdims=True)
        yc = y - mu
        rstd = lax.rsqrt(jnp.mean(yc * yc, axis=-1, keepdims=True) + EPS)
        z = yc * rstd * sm_ref[32:33, :] + sm_ref[33:34, :]
        s_ref[...] = (z * _sigmoid(z)).astype(BF16)

    return pl.pallas_call(
        body, name=name, grid=(T // TM,),
        in_specs=[_row(TM, D), pl.BlockSpec((HALO, D), lambda i: (jnp.maximum(i * nb - 1, 0), 0)),
                  _layer((40, D), l)],
        out_specs=[_row(TM, D), _row(TM, D)],
        out_shape=[jax.ShapeDtypeStruct((T, D), F32), jax.ShapeDtypeStruct((T, D), BF16)],
        scratch_shapes=[pltpu.VMEM((TM + HALO, D), F32)],
        compiler_params=_cp("parallel"),
    )(a, a, sm)


def mm_bias_res(xb, w, l, b, bl, res, name):
    T, K = xb.shape

    def body(x_ref, w_ref, b_ref, r_ref, o_ref):
        o_ref[...] = _dot(x_ref[...], w_ref[...]) + b_ref[...] + r_ref[...]

    return pl.pallas_call(
        body, name=name, grid=(T // TM,),
        in_specs=[_row(TM, K), _layer((K, D), l), _layer((1, D), bl), _row(TM, D)],
        out_specs=_row(TM, D), out_shape=jax.ShapeDtypeStruct((T, D), F32),
        compiler_params=_cp("parallel"),
    )(xb, w, b, res)


def norm_mm_swiglu(h, g, l, w, name):
    T = h.shape[0]
    ns = w.shape[-1]

    def body(h_ref, g_ref, w_ref, xn_ref, gu_ref, f_ref):
        x = h_ref[...]
        xn = (x * _rms(x) * g_ref[...]).astype(BF16)
        xn_ref[...] = xn
        for s in range(2):
            lo, hi = s * ns, (s + 1) * ns
            gate = _dot(xn, w_ref[s])
            up = _dot(xn, w_ref[2 + s])
            gu_ref[:, lo:hi] = gate.astype(BF16)
            gu_ref[:, DFF + lo:DFF + hi] = up.astype(BF16)
            f_ref[:, lo:hi] = (gate * _sigmoid(gate) * up).astype(BF16)

    return pl.pallas_call(
        body, name=name, grid=(T // TM,),
        in_specs=[_row(TM, D), _layer((1, D), l), _layer((4, D, ns), l)],
        out_specs=[_row(TM, D), _row(TM, 2 * DFF), _row(TM, DFF)],
        out_shape=[jax.ShapeDtypeStruct((T, D), BF16), jax.ShapeDtypeStruct((T, 2 * DFF), BF16),
                   jax.ShapeDtypeStruct((T, DFF), BF16)],
        compiler_params=_cp("parallel"),
    )(h, g, w)


def norm_mm(h, g, gl, w, l, name, scale=1.0):
    T = h.shape[0]
    N = w.shape[-1]

    def body(h_ref, g_ref, w_ref, xn_ref, o_ref):
        x = h_ref[...]
        xn = (x * _rms(x) * g_ref[...]).astype(BF16)
        xn_ref[...] = xn
        o_ref[...] = (_dot(xn, w_ref[...]) * scale).astype(BF16)

    return pl.pallas_call(
        body, name=name, grid=(T // TM,),
        in_specs=[_row(TM, D), _layer((1, D), gl), _layer((D, N), l)],
        out_specs=[_row(TM, D), _row(TM, N)],
        out_shape=[jax.ShapeDtypeStruct((T, D), BF16), jax.ShapeDtypeStruct((T, N), BF16)],
        compiler_params=_cp("parallel"),
    )(h, g, w)


QB = 4


def band_mask():
    qi = np.arange(GROUP * BLK)[:, None] % BLK
    kj = np.arange(2 * BLK)[None, :]
    band = ((kj < BLK) & (kj > qi)) | ((kj >= BLK) & (kj - BLK <= qi))
    first = band & (kj >= BLK)
    return np.where(np.stack([first, band]), 0.0, NEG_INF).astype(np.float32)


def _attn_softmax(q4, kb, bias, sink):
    s = _dot_nt(q4, kb) + bias
    m = jnp.maximum(jnp.max(s, axis=-1, keepdims=True), sink)
    p = jnp.exp(s - m)
    es = jnp.exp(sink - m)
    inv = 1.0 / (jnp.sum(p, axis=-1, keepdims=True) + es)
    return p, inv, es


def _attn_specs(T):
    qspec = pl.BlockSpec((None, GROUP, QB * BLK, HD), lambda kv, n: (kv, 0, n, 0))
    kspec = pl.BlockSpec((None, T + BLK, HD), lambda kv, n: (kv, 0, 0))
    bspec = pl.BlockSpec((2, None, GROUP * BLK, 2 * BLK), lambda kv, n: (0, kv, 0, 0))
    sspec = pl.BlockSpec((None, GROUP * BLK, 1), lambda kv, n: (kv, 0, 0))
    return qspec, kspec, bspec, sspec


def _attn_block(n, b, q_ref, b_ref):
    blk = n * QB + b
    rows = pl.ds(pl.multiple_of(blk * BLK, BLK), 2 * BLK)
    q4 = q_ref[:, b * BLK:(b + 1) * BLK, :].reshape(GROUP * BLK, HD)
    bias = b_ref[jnp.minimum(blk, 1)] if b == 0 else b_ref[1]
    return rows, q4, bias


def attn_fwd(q, kp, vp, bias, sink, name):
    T = q.shape[2]
    qspec, kspec, bspec, sspec = _attn_specs(T)

    def body(q_ref, k_ref, v_ref, b_ref, s_ref, o_ref):
        n = pl.program_id(1)
        for b in range(QB):
            rows, q4, bias = _attn_block(n, b, q_ref, b_ref)
            p, inv, _ = _attn_softmax(q4, k_ref[rows, :], bias, s_ref[...])
            o = _dot(p.astype(BF16), v_ref[rows, :]) * inv
            o_ref[:, b * BLK:(b + 1) * BLK, :] = o.reshape(GROUP, BLK, HD).astype(BF16)

    return pl.pallas_call(
        body, name=name, grid=(N_KV, T // (QB * BLK)),
        in_specs=[qspec, kspec, kspec, bspec, sspec], out_specs=qspec,
        out_shape=jax.ShapeDtypeStruct((N_KV, GROUP, T, HD), BF16),
        compiler_params=_cp("parallel", "parallel"),
    )(q, kp, vp, bias, sink)


def attn_bwd(q, kp, vp, bias, sink, o, do, name):
    T = q.shape[2]
    qspec, kspec, bspec, sspec = _attn_specs(T)

    def body(q_ref, k_ref, v_ref, b_ref, s_ref, o_ref, do_ref, dq_ref, dk_ref, dv_ref, db_ref, ds_ref):
        n = pl.program_id(1)

        @pl.when(n == 0)
        def _():
            dk_ref[...] = jnp.zeros_like(dk_ref)
            dv_ref[...] = jnp.zeros_like(dv_ref)
            db_ref[...] = jnp.zeros_like(db_ref)
            ds_ref[...] = jnp.zeros_like(ds_ref)

        for b in range(QB):
            rows, q4, bias = _attn_block(n, b, q_ref, b_ref)
            do4 = do_ref[:, b * BLK:(b + 1) * BLK, :].reshape(GROUP * BLK, HD)
            o4 = o_ref[:, b * BLK:(b + 1) * BLK, :].reshape(GROUP * BLK, HD)
            kb = k_ref[rows, :]
            vb = v_ref[rows, :]
            p, inv, es = _attn_softmax(q4, kb, bias, s_ref[...])
            probs = p * inv
            dp = _dot_nt(do4, vb)
            delta = jnp.sum(do4.astype(F32) * o4.astype(F32), axis=-1, keepdims=True)
            dS = probs * (dp - delta)
            ds_ref[...] += -(es * inv) * delta
            db_ref[...] += dS
            dSb = dS.astype(BF16)
            dq_ref[:, b * BLK:(b + 1) * BLK, :] = (_dot(dSb, kb) * (HD ** -0.5)).reshape(GROUP, BLK, HD).astype(BF16)
            dk_ref[rows, :] += _dot_tn(dSb, q4)
            dv_ref[rows, :] += _dot_tn(probs.astype(BF16), do4)

    kout = pl.BlockSpec((None, T + BLK, HD), lambda kv, n: (kv, 0, 0))
    dbspec = pl.BlockSpec((None, GROUP * BLK, 2 * BLK), lambda kv, n: (kv, 0, 0))
    return pl.pallas_call(
        body, name=name, grid=(N_KV, T // (QB * BLK)),
        in_specs=[qspec, kspec, kspec, bspec, sspec, qspec, qspec],
        out_specs=[qspec, kout, kout, dbspec, sspec],
        out_shape=[jax.ShapeDtypeStruct((N_KV, GROUP, T, HD), BF16),
                   jax.ShapeDtypeStruct((N_KV, T + BLK, HD), F32), jax.ShapeDtypeStruct((N_KV, T + BLK, HD), F32),
                   jax.ShapeDtypeStruct((N_KV, GROUP * BLK, 2 * BLK), F32),
                   jax.ShapeDtypeStruct((N_KV, GROUP * BLK, 1), F32)],
        compiler_params=_cp("parallel", "arbitrary"),
    )(q, kp, vp, bias, sink, o, do)


def final_loss(h, g, target, name):
    T = h.shape[0]

    def body(h_ref, g_ref, t_ref, dh_ref, st_ref):
        i = pl.program_id(0)

        @pl.when(i == 0)
        def _():
            st_ref[...] = jnp.zeros_like(st_ref)

        x = h_ref[...]
        r = _rms(x)
        xh = x * r
        e = xh * g_ref[...] - t_ref[...]
        loss = 0.5 * jnp.sum(jnp.mean(e * e, axis=-1, keepdims=True))
        dy = e * (1.0 / D)
        st_ref[0:1, :] += jnp.sum(dy * xh, axis=0, keepdims=True)
        lane = lax.broadcasted_iota(jnp.int32, (1, D), 1)
        st_ref[1:2, :] += jnp.where(lane == 0, loss, 0.0)
        dxh = dy * g_ref[...]
        dh_ref[...] = r * (dxh - xh * jnp.mean(dxh * xh, axis=-1, keepdims=True))

    return pl.pallas_call(
        body, name=name, grid=(T // TM,),
        in_specs=[_row(TM, D), _const((1, D)), _row(TM, D)],
        out_specs=[_row(TM, D), _const((8, D))],
        out_shape=[jax.ShapeDtypeStruct((T, D), F32), jax.ShapeDtypeStruct((8, D), F32)],
        compiler_params=_cp("arbitrary"),
    )(h, g, target)


def mm_dw(x, dy, name, tn, slots, colsum=False):
    T, K = x.shape
    N = dy.shape[1]
    tt = min(T, 1024)
    nt = T // tt
    ns = N // slots
    per = ns // tn

    def body(x_ref, dy_ref, *rest):
        if colsum:
            dw_ref, cs_ref, acc, cacc = rest
        else:
            dw_ref, acc = rest
        t = pl.program_id(1)

        @pl.when(t == 0)
        def _():
            acc[...] = jnp.zeros_like(acc)
            if colsum:
                cacc[...] = jnp.zeros_like(cacc)

        dyv = dy_ref[...]
        acc[...] += _dot_tn(x_ref[...].astype(BF16), dyv.astype(BF16))
        if colsum:
            cacc[...] += jnp.sum(dyv.astype(F32), axis=0, keepdims=True)

        @pl.when(t == nt - 1)
        def _():
            dw_ref[...] = acc[...].astype(BF16)
            if colsum:
                cs_ref[...] = cacc[...]

    out_specs = [pl.BlockSpec((None, K, tn), lambda j, t: (j // per, 0, j % per))]
    out_shape = [jax.ShapeDtypeStruct((slots, K, ns), BF16)]
    scratch = [pltpu.VMEM((K, tn), F32)]
    if colsum:
        out_specs.append(pl.BlockSpec((1, tn), lambda j, t: (0, j)))
        out_shape.append(jax.ShapeDtypeStruct((1, N), F32))
        scratch.append(pltpu.VMEM((1, tn), F32))
    res = pl.pallas_call(
        body, name=name, grid=(N // tn, nt),
        in_specs=[pl.BlockSpec((tt, K), lambda j, t: (t, 0)), pl.BlockSpec((tt, tn), lambda j, t: (t, j))],
        out_specs=out_specs, out_shape=out_shape, scratch_shapes=scratch,
        compiler_params=_cp("parallel", "arbitrary"),
    )(x, dy)
    return tuple(res) if colsum else res[0]


def mmT_swiglu_bwd(dh, w, l, gu, name):
    T = dh.shape[0]
    half = DFF // 2

    def body(dh_ref, w_ref, gu_ref, du_ref):
        dhb = dh_ref[...].astype(BF16)
        for s in range(2):
            lo, hi = s * half, (s + 1) * half
            df = _dot_nt(dhb, w_ref[lo:hi, :])
            gate = gu_ref[:, lo:hi].astype(F32)
            up = gu_ref[:, DFF + lo:DFF + hi].astype(F32)
            sg = _sigmoid(gate)
            du_ref[:, lo:hi] = (df * up * sg * (1.0 + gate * (1.0 - sg))).astype(BF16)
            du_ref[:, DFF + lo:DFF + hi] = (df * gate * sg).astype(BF16)

    return pl.pallas_call(
        body, name=name, grid=(T // TM,),
        in_specs=[_row(TM, D), _layer((DFF, D), l), _row(TM, 2 * DFF)],
        out_specs=_row(TM, 2 * DFF), out_shape=jax.ShapeDtypeStruct((T, 2 * DFF), BF16),
        compiler_params=_cp("parallel"),
    )(dh, w, gu)


def mmT_rmsbwd(du, w, l, slots, h, g, gl, dh_in, name):
    T, N = du.shape
    ns = N // slots

    def body(du_ref, w_ref, h_ref, g_ref, di_ref, dh_ref, dg_ref):
        i = pl.program_id(0)

        @pl.when(i == 0)
        def _():
            dg_ref[...] = jnp.zeros_like(dg_ref)

        dxn = _dot_nt(du_ref[:, 0:ns], w_ref[0])
        for s in range(1, slots):
            dxn = dxn + _dot_nt(du_ref[:, s * ns:(s + 1) * ns], w_ref[s])
        x = h_ref[...]
        r = _rms(x)
        xh = x * r
        dg_ref[0:1, :] += jnp.sum(dxn * xh, axis=0, keepdims=True)
        dxh = dxn * g_ref[...]
        dh_ref[...] = di_ref[...] + r * (dxh - xh * jnp.mean(dxh * xh, axis=-1, keepdims=True))

    return pl.pallas_call(
        body, name=name, grid=(T // TM,),
        in_specs=[_row(TM, N), _layer((slots, D, ns), l), _row(TM, D), _layer((1, D), gl), _row(TM, D)],
        out_specs=[_row(TM, D), _const((8, D))],
        out_shape=[jax.ShapeDtypeStruct((T, D), F32), jax.ShapeDtypeStruct((8, D), F32)],
        compiler_params=_cp("arbitrary"),
    )(du, w, h, g, dh_in)


def mmT(dh, w, l, name):
    T = dh.shape[0]
    N = w.shape[1]

    def body(dh_ref, w_ref, o_ref):
        o_ref[...] = _dot_nt(dh_ref[...].astype(BF16), w_ref[...]).astype(BF16)

    return pl.pallas_call(
        body, name=name, grid=(T // TM,),
        in_specs=[_row(TM, D), _layer((N, D), l)],
        out_specs=_row(TM, N), out_shape=jax.ShapeDtypeStruct((T, N), BF16),
        compiler_params=_cp("parallel"),
    )(dh, w)


def mmT_lnbwd(dh, w, l, y, sm, name):
    T = dh.shape[0]

    def body(dh_ref, w_ref, y_ref, sm_ref, dy_ref, st_ref):
        i = pl.program_id(0)

        @pl.when(i == 0)
        def _():
            st_ref[...] = jnp.zeros_like(st_ref)

        ds = _dot_nt(dh_ref[...].astype(BF16), w_ref[...])
        y = y_ref[...]
        mu = jnp.mean(y, axis=-1, keepdims=True)
        yc = y - mu
        rstd = lax.rsqrt(jnp.mean(yc * yc, axis=-1, keepdims=True) + EPS)
        xh = yc * rstd
        gam = sm_ref[32:33, :]
        z = xh * gam + sm_ref[33:34, :]
        sg = _sigmoid(z)
        dz = ds * sg * (1.0 + z * (1.0 - sg))
        st_ref[0:1, :] += jnp.sum(dz * xh, axis=0, keepdims=True)
        st_ref[1:2, :] += jnp.sum(dz, axis=0, keepdims=True)
        dxh = dz * gam
        dy = rstd * (dxh - jnp.mean(dxh, axis=-1, keepdims=True) - xh * jnp.mean(dxh * xh, axis=-1, keepdims=True))
        st_ref[2:3, :] += jnp.sum(dy, axis=0, keepdims=True)
        dy_ref[...] = dy

    return pl.pallas_call(
        body, name=name, grid=(T // TM,),
        in_specs=[_row(TM, D), _layer((D, D), l), _row(TM, D), _layer((40, D), l)],
        out_specs=[_row(TM, D), _const((8, D))],
        out_shape=[jax.ShapeDtypeStruct((T, D), F32), jax.ShapeDtypeStruct((8, D), F32)],
        compiler_params=_cp("arbitrary"),
    )(dh, w, y, sm)


def dwconv_glu_bwd(dy, a, u, sm, smrev, l, name):
    T = dy.shape[0]
    nb = TM // HALO
    last = T // HALO - 1

    def body(dy_ref, dyn_ref, a_ref, ap_ref, u_ref, sm_ref, rev_ref, du_ref, dw_ref, bufd, bufa, da):
        i = pl.program_id(0)

        @pl.when(i == 0)
        def _():
            dw_ref[...] = jnp.zeros_like(dw_ref)

        bufd[0:TM, :] = dy_ref[...]
        bufd[TM:TM + HALO, :] = jnp.where(i < pl.num_programs(0) - 1, dyn_ref[...], 0.0)
        bufa[0:HALO, :] = jnp.where(i > 0, ap_ref[...], 0.0)
        bufa[HALO:HALO + TM, :] = a_ref[...]
        _conv_taps(bufd, rev_ref, da, 0)
        LB = 512
        for c0 in range(0, D, LB):
            for k in range(CONV_W):
                acc = jnp.zeros((8, LB), F32)
                for r0 in range(0, TM, 8):
                    acc = acc + dy_ref[r0:r0 + 8, c0:c0 + LB] * bufa[pl.ds(HALO - (CONV_W - 1) + k + r0, 8), c0:c0 + LB]
                dw_ref[k:k + 1, c0:c0 + LB] += jnp.sum(acc, axis=0, keepdims=True)
        dav = da[...]
        u1 = u_ref[:, 0:D].astype(F32)
        sg = _sigmoid(u_ref[:, D:2 * D].astype(F32))
        du_ref[:, 0:D] = (dav * sg).astype(BF16)
        du_ref[:, D:2 * D] = (dav * u1 * sg * (1.0 - sg)).astype(BF16)

    return pl.pallas_call(
        body, name=name, grid=(T // TM,),
        in_specs=[_row(TM, D), pl.BlockSpec((HALO, D), lambda i: (jnp.minimum((i + 1) * nb, last), 0)),
                  _row(TM, D), pl.BlockSpec((HALO, D), lambda i: (jnp.maximum(i * nb - 1, 0), 0)),
                  _row(TM, 2 * D), _layer((40, D), l), _layer((40, D), l)],
        out_specs=[_row(TM, 2 * D), _const((32, D))],
        out_shape=[jax.ShapeDtypeStruct((T, 2 * D), BF16), jax.ShapeDtypeStruct((32, D), F32)],
        scratch_shapes=[pltpu.VMEM((TM + HALO, D), F32), pltpu.VMEM((TM + HALO, D), F32), pltpu.VMEM((TM, D), F32)],
        compiler_params=_cp("arbitrary"),
    )(dy, dy, a, a, u, sm, smrev)


def _rows_tile(R):
    for t in (512, 256, 128, 64, 32, 16, 8):
        if R % t == 0:
            return t
    return R


def add_n(xs, out_dtype, name):
    R, C = xs[0].shape
    tr = _rows_tile(R)

    def body(*refs):
        acc = refs[0][...].astype(F32)
        for r in refs[1:-1]:
            acc = acc + r[...].astype(F32)
        refs[-1][...] = acc.astype(out_dtype)

    return pl.pallas_call(
        body, name=name, grid=(R // tr,),
        in_specs=[_row(tr, C)] * len(xs), out_specs=_row(tr, C),
        out_shape=jax.ShapeDtypeStruct((R, C), out_dtype), compiler_params=_cp("parallel"),
    )(*xs)


def adamw(w, g, m, v, name):
    R, C = w.shape
    tr = _rows_tile(R)

    def body(w_ref, g_ref, m_ref, v_ref, d_ref, nm_ref, nv_ref):
        gv = g_ref[...]
        nm = ADAM_B1 * m_ref[...] + (1.0 - ADAM_B1) * gv
        nv = ADAM_B2 * v_ref[...] + (1.0 - ADAM_B2) * (gv * gv)
        m_hat = nm / (1.0 - ADAM_B1 ** ADAM_STEP)
        v_hat = nv / (1.0 - ADAM_B2 ** ADAM_STEP)
        d_ref[...] = -ADAM_LR * (m_hat / (jnp.sqrt(v_hat) + ADAM_EPS) + ADAM_WD * w_ref[...])
        nm_ref[...] = nm
        nv_ref[...] = nv

    sd = jax.ShapeDtypeStruct((R, C), F32)
    return pl.pallas_call(
        body, name=name, grid=(R // tr,),
        in_specs=[_row(tr, C)] * 4, out_specs=[_row(tr, C)] * 3, out_shape=[sd, sd, sd],
        compiler_params=_cp("parallel"),
    )(w, g, m, v)


ANY = pl.BlockSpec(memory_space=pl.ANY)


def _place():
    x, y, c = lax.axis_index("x"), lax.axis_index("y"), lax.axis_index("c")
    chips = [(1 - x, y), (x, 1 - y), (1 - x, 1 - y)]
    return x, y, c, chips


def _half(ref, hc):
    lh = ref.shape[0] // 2
    return ref.at[pl.ds(hc * lh, lh)]


def allgather_weights(shards):
    nt = len(shards)

    def body(*refs):
        ins, outs = refs[:nt], refs[nt:2 * nt]
        send, recv = refs[2 * nt:]
        x, y, c, chips = _place()
        me = 2 * x + y

        def slot(ref, hc, s):
            return _half(ref, hc).at[:, pl.ds(s, 1)]

        def rc(k, src, dst, to):
            return pltpu.make_async_remote_copy(src_ref=src, dst_ref=dst, send_sem=send.at[k], recv_sem=recv.at[k],
                                                device_id=to, device_id_type=MESH)

        first, passed = [], []
        for t in range(nt):
            for j, (cx, cy) in enumerate(chips):
                cp = rc(6 * t + j, _half(ins[t], c), slot(outs[t], c, me), (cx, cy, c))
                cp.start()
                first.append(cp)
        for t in range(nt):
            for j, (cx, cy) in enumerate(chips):
                got = slot(outs[t], c, 2 * cx + cy)
                rc(6 * t + j, got, got, (cx, cy, c)).wait_recv()
                cp = rc(6 * t + 3 + j, got, got, (x, y, 1 - c))
                cp.start()
                passed.append(cp)
        for t in range(nt):
            for j, (cx, cy) in enumerate(chips):
                got = slot(outs[t], 1 - c, 2 * cx + cy)
                rc(6 * t + 3 + j, got, got, (x, y, 1 - c)).wait_recv()
        for cp in first + passed:
            cp.wait_send()

    out_shape = [jax.ShapeDtypeStruct((s.shape[0], 4) + s.shape[2:], s.dtype) for s in shards]
    outs = pl.pallas_call(
        body, name="allgather_weights", in_specs=[ANY] * nt, out_specs=[ANY] * nt, out_shape=out_shape,
        scratch_shapes=[pltpu.SemaphoreType.DMA((6 * nt,)), pltpu.SemaphoreType.DMA((6 * nt,))],
        compiler_params=pltpu.CompilerParams(has_side_effects=True),
    )(*shards)
    me = 2 * lax.axis_index("x") + lax.axis_index("y")
    return [lax.dynamic_update_slice_in_dim(o, s, me, axis=1) for o, s in zip(outs, shards)]


def sibling_swap_halves(gs):
    nt = len(gs)

    def body(*refs):
        ins, outs = refs[:nt], refs[nt:2 * nt]
        send, recv = refs[2 * nt:]
        x, y, c, _ = _place()
        cps = []
        for t in range(nt):
            cp = pltpu.make_async_remote_copy(src_ref=_half(ins[t], 1 - c), dst_ref=outs[t], send_sem=send.at[t],
                                              recv_sem=recv.at[t], device_id=(x, y, 1 - c), device_id_type=MESH)
            cp.start()
            cps.append(cp)
        for cp in cps:
            cp.wait()

    out_shape = [jax.ShapeDtypeStruct((g.shape[0] // 2,) + g.shape[1:], g.dtype) for g in gs]
    return pl.pallas_call(
        body, name="sibling_swap_halves", in_specs=[ANY] * nt, out_specs=[ANY] * nt, out_shape=out_shape,
        scratch_shapes=[pltpu.SemaphoreType.DMA((nt,)), pltpu.SemaphoreType.DMA((nt,))],
        compiler_params=pltpu.CompilerParams(has_side_effects=True),
    )(*gs)


def chip_scatter(ps):
    nt = len(ps)

    def body(*refs):
        ins, outs = refs[:nt], refs[nt:2 * nt]
        send, recv = refs[2 * nt:]
        x, y, c, chips = _place()
        cps = []
        for t in range(nt):
            for j, (cx, cy) in enumerate(chips):
                cp = pltpu.make_async_remote_copy(
                    src_ref=ins[t].at[:, pl.ds(2 * cx + cy, 1)], dst_ref=outs[t].at[j], send_sem=send.at[3 * t + j],
                    recv_sem=recv.at[3 * t + j], device_id=(cx, cy, c), device_id_type=MESH)
                cp.start()
                cps.append(cp)
        for cp in cps:
            cp.wait()

    out_shape = [jax.ShapeDtypeStruct((3, p.shape[0], 1) + p.shape[2:], p.dtype) for p in ps]
    return pl.pallas_call(
        body, name="chip_scatter", in_specs=[ANY] * nt, out_specs=[ANY] * nt, out_shape=out_shape,
        scratch_shapes=[pltpu.SemaphoreType.DMA((3 * nt,)), pltpu.SemaphoreType.DMA((3 * nt,))],
        compiler_params=pltpu.CompilerParams(has_side_effects=True),
    )(*ps)


def sibling_join(rs):
    nt = len(rs)

    def body(*refs):
        ins, outs = refs[:nt], refs[nt:2 * nt]
        send, recv = refs[2 * nt:]
        x, y, c, _ = _place()
        cps = []
        for t in range(nt):
            cp = pltpu.make_async_remote_copy(src_ref=ins[t], dst_ref=_half(outs[t], c), send_sem=send.at[t],
                                              recv_sem=recv.at[t], device_id=(x, y, 1 - c), device_id_type=MESH)
            cp.start()
            cps.append(cp)
        for t in range(nt):
            cps[t].wait_send()
            got = _half(outs[t], 1 - c)
            pltpu.make_async_remote_copy(src_ref=ins[t], dst_ref=got, send_sem=send.at[t], recv_sem=recv.at[t],
                                         device_id=(x, y, 1 - c), device_id_type=MESH).wait_recv()

    out_shape = [jax.ShapeDtypeStruct((2 * r.shape[0],) + r.shape[1:], r.dtype) for r in rs]
    outs = pl.pallas_call(
        body, name="sibling_join", in_specs=[ANY] * nt, out_specs=[ANY] * nt, out_shape=out_shape,
        scratch_shapes=[pltpu.SemaphoreType.DMA((nt,)), pltpu.SemaphoreType.DMA((nt,))],
        compiler_params=pltpu.CompilerParams(has_side_effects=True),
    )(*rs)
    c = lax.axis_index("c")
    return [lax.dynamic_update_slice_in_dim(o, r, c * r.shape[0], axis=0) for o, r in zip(outs, rs)]


def allreduce_small(v):
    R = v.shape[0]

    def body(v_ref, o_ref, all_ref, send, recv):
        x, y, c, _ = _place()
        me = 4 * x + 2 * y + c
        all_ref[me] = v_ref[...]

        def peer(k):
            return ((1 - x) if k & 4 else x, (1 - y) if k & 2 else y, (1 - c) if k & 1 else c)

        cps = []
        for k in range(1, 8):
            px, py, pc = peer(k)
            cp = pltpu.make_async_remote_copy(src_ref=v_ref, dst_ref=all_ref.at[me], send_sem=send.at[k - 1],
                                              recv_sem=recv.at[k - 1], device_id=(px, py, pc), device_id_type=MESH)
            cp.start()
            cps.append(cp)
        for k in range(1, 8):
            px, py, pc = peer(k)
            src = 4 * px + 2 * py + pc
            pltpu.make_async_remote_copy(src_ref=v_ref, dst_ref=all_ref.at[src], send_sem=send.at[k - 1],
                                         recv_sem=recv.at[k - 1], device_id=(px, py, pc),
                                         device_id_type=MESH).wait_recv()
        for cp in cps:
            cp.wait_send()
        acc = all_ref[0]
        for d in range(1, 8):
            acc = acc + all_ref[d]
        o_ref[...] = acc

    return pl.pallas_call(
        body, name="allreduce_small",
        in_specs=[pl.BlockSpec(memory_space=pltpu.VMEM)], out_specs=pl.BlockSpec(memory_space=pltpu.VMEM),
        out_shape=jax.ShapeDtypeStruct((R, D), F32),
        scratch_shapes=[pltpu.VMEM((8, R, D), F32), pltpu.SemaphoreType.DMA((7,)), pltpu.SemaphoreType.DMA((7,))],
        compiler_params=pltpu.CompilerParams(has_side_effects=True, vmem_limit_bytes=VMEM_LIMIT),
    )(v)


def _bucket_table():
    qi = np.arange(BLK)[:, None]
    kj = np.arange(2 * BLK)[None, :]
    d = np.maximum(qi + BLK - kj, 0)
    max_exact = N_BUCKETS // 2
    log_ratio = (np.log(np.maximum(d, 1).astype(np.float32) / np.float32(max_exact))
                 / np.float32(math.log(MAX_DISTANCE / max_exact))).astype(np.float32)
    large = max_exact + (log_ratio * np.float32(N_BUCKETS - max_exact)).astype(np.int32)
    large = np.minimum(large, N_BUCKETS - 1)
    return np.where(d < max_exact, d, large).astype(np.int32)


def _heads_major(a, nh):
    T = a.shape[0]
    return a.reshape(T, nh, HD).transpose(1, 0, 2)


def _heads_minor(a):
    nh, T, _ = a.shape
    return a.transpose(1, 0, 2).reshape(T, nh * HD)


def local_step(x, target, W, small):
    T = x.shape[0]
    h = x
    saved = []
    for l in range(2):
        xn, u, a = norm_mm_glu(h, small["norm_mix"], l, W["pw1"], small["b_pw1"], f"f_pw1_{l}")
        y, s = dwconv_ln_silu(a, small["conv"], l, f"f_conv_{l}")
        h1 = mm_bias_res(s, W["pw2"], l, small["b_pw2"], l, h, f"f_pw2_{l}")
        xn2, gu, f = norm_mm_swiglu(h1, small["norm_ffn"], l, W["up"], f"f_up_{l}")
        h2 = mm_bias_res(f, W["down"], l, small["zero"], 0, h1, f"f_down_{l}")
        saved.append(dict(h=h, xn=xn, u=u, a=a, y=y, s=s, h1=h1, xn2=xn2, gu=gu, f=f))
        h = h2
    h_kv = h
    kvn, kv = norm_mm(h, small["norm_kv"], 0, W["kv"], 0, "f_kv")
    kp = jnp.pad(_heads_major(kv[:, :N_KV * HD], N_KV), ((0, 0), (BLK, 0), (0, 0)))
    vp = jnp.pad(_heads_major(kv[:, N_KV * HD:], N_KV), ((0, 0), (BLK, 0), (0, 0)))
    bucket = _bucket_table()
    onehot = jnp.asarray(np.eye(N_BUCKETS, dtype=np.float32)[bucket])
    bias = jnp.einsum("qkb,bh->hqk", onehot, small["rel_bias"], precision=lax.Precision.HIGHEST)
    bias = bias.reshape(1, N_KV, GROUP * BLK, 2 * BLK) + jnp.asarray(band_mask())[:, None]
    for j in range(2):
        l = 2 + j
        xn, q = norm_mm(h, small["norm_mix"], l, W["wq"], j, f"f_q_{j}", scale=HD ** -0.5)
        qh = _heads_major(q, N_HEADS).reshape(N_KV, GROUP, T, HD)
        sink = jnp.broadcast_to(small["sinks"][j].reshape(N_KV, GROUP, 1, 1), (N_KV, GROUP, BLK, 1))
        sink = sink.reshape(N_KV, GROUP * BLK, 1)
        oh = attn_fwd(qh, kp, vp, bias, sink, f"f_attn_{j}")
        attn = _heads_minor(oh.reshape(N_HEADS, T, HD))
        h1 = mm_bias_res(attn, W["wo"], j, small["zero"], 0, h, f"f_wo_{j}")
        xn2, gu, f = norm_mm_swiglu(h1, small["norm_ffn"], l, W["up"], f"f_up_{l}")
        h2 = mm_bias_res(f, W["down"], l, small["zero"], 0, h1, f"f_down_{l}")
        saved.append(dict(h=h, xn=xn, qh=qh, oh=oh, sink=sink, attn=attn, h1=h1, xn2=xn2, gu=gu, f=f))
        h = h2

    dh, st_final = final_loss(h, small["norm_final"], target, "loss_head")

    G = dict(up=[None] * 4, down=[None] * 4, pw1=[None] * 2, pw2=[None] * 2, wq=[None] * 2, wo=[None] * 2)
    S = dict(norm_ffn=[None] * 4, norm_mix=[None] * 4, conv=[None] * 2, taps=[None] * 2, b_pw1=[None] * 2,
             b_pw2=[None] * 2, sinks=[None] * 2)

    def ffn_bwd(dh, sv, l):
        du = mmT_swiglu_bwd(dh, W["down"], l, sv["gu"], f"b_down_{l}")
        G["down"][l] = mm_dw(sv["f"], dh, f"w_down_{l}", 512, 1)
        G["up"][l] = mm_dw(sv["xn2"], du, f"w_up_{l}", DFF // 2, 4)
        dh, dg = mmT_rmsbwd(du, W["up"], l, 4, sv["h1"], small["norm_ffn"], l, dh, f"b_up_{l}")
        S["norm_ffn"][l] = dg[0]
        return dh

    dk = dv = dbias = None
    for j in (1, 0):
        l = 2 + j
        sv = saved[l]
        dh = ffn_bwd(dh, sv, l)
        dattn = mmT(dh, W["wo"], j, f"b_wo_{j}")
        G["wo"][j] = mm_dw(sv["attn"], dh, f"w_wo_{j}", 512, 1)
        doh = _heads_major(dattn, N_HEADS).reshape(N_KV, GROUP, T, HD)
        dqh, dkj, dvj, dbj, dsj = attn_bwd(sv["qh"], kp, vp, bias, sv["sink"], sv["oh"], doh, f"b_attn_{j}")
        dq = _heads_minor(dqh.reshape(N_HEADS, T, HD))
        G["wq"][j] = mm_dw(sv["xn"], dq, f"w_q_{j}", 512, 1)
        dh, dg = mmT_rmsbwd(dq, W["wq"].reshape(2, 1, D, D), j, 1, sv["h"], small["norm_mix"], l, dh, f"b_q_{j}")
        S["norm_mix"][l] = dg[0]
        S["sinks"][j] = jnp.sum(dsj.reshape(N_HEADS, BLK), axis=1)
        dk = dkj if dk is None else dk + dkj
        dv = dvj if dv is None else dv + dvj
        dbias = dbj if dbias is None else dbias + dbj

    dkv = jnp.concatenate([_heads_minor(dk[:, BLK:]), _heads_minor(dv[:, BLK:])], axis=1).astype(BF16)
    G["kv"] = mm_dw(kvn, dkv, "w_kv", 512, 1)
    dh, dg = mmT_rmsbwd(dkv, W["kv"].reshape(1, 1, D, 2 * N_KV * HD), 0, 1, h_kv, small["norm_kv"], 0, dh, "b_kv")
    S["norm_kv"] = dg[0]
    dbh = dbias.reshape(N_HEADS, BLK, 2 * BLK)
    S["rel_bias"] = jnp.einsum("hqk,qkb->bh", dbh, onehot, precision=lax.Precision.HIGHEST)

    for l in (1, 0):
        sv = saved[l]
        dh = ffn_bwd(dh, sv, l)
        dy, st = mmT_lnbwd(dh, W["pw2"], l, sv["y"], small["conv"], f"b_pw2_{l}")
        G["pw2"][l], S["b_pw2"][l] = mm_dw(sv["s"], dh, f"w_pw2_{l}", 512, 1, colsum=True)
        du, dtaps = dwconv_glu_bwd(dy, sv["a"], sv["u"], small["conv"], small["conv_rev"], l, f"b_conv_{l}")
        S["conv"][l] = st[0:3]
        S["taps"][l] = dtaps[0:CONV_W]
        G["pw1"][l], S["b_pw1"][l] = mm_dw(sv["xn"], du, f"w_pw1_{l}", 512, 4, colsum=True)
        dh, dg = mmT_rmsbwd(du, W["pw1"], l, 4, sv["h"], small["norm_mix"], l, dh, f"b_pw1_{l}")
        S["norm_mix"][l] = dg[0]
    S["norm_final"] = st_final[0]
    return st_final[1, 0], dh, G, S


R_CONV = 37
R_SMALL = 88


def _pack_small(S):
    rows = []
    for l in range(2):
        rows += [S["taps"][l], S["conv"][l][2:3], S["conv"][l][0:2], S["b_pw2"][l], S["b_pw1"][l].reshape(2, D)]
    rows += [jnp.stack(S["norm_mix"]), jnp.stack(S["norm_ffn"]), S["norm_kv"][None], S["norm_final"][None]]
    tail = jnp.concatenate([jnp.stack(S["sinks"]).reshape(-1), S["rel_bias"].reshape(-1)])
    rows.append(jnp.pad(tail, (0, D - tail.shape[0]))[None])
    v = jnp.concatenate(rows, axis=0)
    return jnp.pad(v, ((0, R_SMALL - v.shape[0]), (0, 0)))


def kernel(x, norm_mix, norm_ffn, conv_w_pw1, conv_b_pw1, conv_w_dw, conv_b_dw, conv_ln_g, conv_ln_b, conv_w_pw2, conv_b_pw2, norm_kv, w_kv, w_q, w_o, sinks, rel_bias, ffn_w_up, ffn_w_down, norm_final, loss_target, m_norm_mix, m_norm_ffn, m_conv_w_pw1, m_conv_b_pw1, m_conv_w_dw, m_conv_b_dw, m_conv_ln_g, m_conv_ln_b, m_conv_w_pw2, m_conv_b_pw2, m_norm_kv, m_w_kv, m_w_q, m_w_o, m_sinks, m_rel_bias, m_ffn_w_up, m_ffn_w_down, m_norm_final, v_norm_mix, v_norm_ffn, v_conv_w_pw1, v_conv_b_pw1, v_conv_w_dw, v_conv_b_dw, v_conv_ln_g, v_conv_ln_b, v_conv_w_pw2, v_conv_b_pw2, v_norm_kv, v_w_kv, v_w_q, v_w_o, v_sinks, v_rel_bias, v_ffn_w_up, v_ffn_w_down, v_norm_final):
    xi, yi, ci = lax.axis_index("x"), lax.axis_index("y"), lax.axis_index("c")
    me = 2 * xi + yi
    T = x.shape[1]
    weights = dict(norm_mix=norm_mix, norm_ffn=norm_ffn, conv_w_pw1=conv_w_pw1, conv_b_pw1=conv_b_pw1,
                   conv_w_dw=conv_w_dw, conv_b_dw=conv_b_dw, conv_ln_g=conv_ln_g, conv_ln_b=conv_ln_b,
                   conv_w_pw2=conv_w_pw2, conv_b_pw2=conv_b_pw2, norm_kv=norm_kv, w_kv=w_kv, w_q=w_q, w_o=w_o,
                   sinks=sinks, rel_bias=rel_bias, ffn_w_up=ffn_w_up, ffn_w_down=ffn_w_down, norm_final=norm_final)
    mom_m = dict(norm_mix=m_norm_mix, norm_ffn=m_norm_ffn, conv_w_pw1=m_conv_w_pw1, conv_b_pw1=m_conv_b_pw1,
                 conv_w_dw=m_conv_w_dw, conv_b_dw=m_conv_b_dw, conv_ln_g=m_conv_ln_g, conv_ln_b=m_conv_ln_b,
                 conv_w_pw2=m_conv_w_pw2, conv_b_pw2=m_conv_b_pw2, norm_kv=m_norm_kv, w_kv=m_w_kv, w_q=m_w_q,
                 w_o=m_w_o, sinks=m_sinks, rel_bias=m_rel_bias, ffn_w_up=m_ffn_w_up, ffn_w_down=m_ffn_w_down,
                 norm_final=m_norm_final)
    mom_v = dict(norm_mix=v_norm_mix, norm_ffn=v_norm_ffn, conv_w_pw1=v_conv_w_pw1, conv_b_pw1=v_conv_b_pw1,
                 conv_w_dw=v_conv_w_dw, conv_b_dw=v_conv_b_dw, conv_ln_g=v_conv_ln_g, conv_ln_b=v_conv_ln_b,
                 conv_w_pw2=v_conv_w_pw2, conv_b_pw2=v_conv_b_pw2, norm_kv=v_norm_kv, w_kv=v_w_kv, w_q=v_w_q,
                 w_o=v_w_o, sinks=v_sinks, rel_bias=v_rel_bias, ffn_w_up=v_ffn_w_up, ffn_w_down=v_ffn_w_down,
                 norm_final=v_norm_final)

    big = ["conv_w_pw1", "conv_w_pw2", "w_q", "w_o", "ffn_w_up", "ffn_w_down", "w_kv"]

    def slot_view(a):
        if a.ndim == 2:
            return a.reshape(2, 1, a.shape[0] // 2, a.shape[1])
        return a.reshape(a.shape[0], 1, a.shape[1], a.shape[2])

    smallpack = jnp.concatenate(
        [conv_w_dw, conv_b_dw[:, None], conv_ln_g[:, None], conv_ln_b[:, None], conv_b_pw2[:, None],
         conv_b_pw1.reshape(2, 2, 256), jnp.zeros((2, 3, 256), F32)], axis=1)
    shards = [slot_view(weights[n].astype(BF16)) for n in big] + [smallpack.reshape(2, 1, 40, 256)]
    fulls = allgather_weights(shards)
    f_pw1, f_pw2, f_wq, f_wo, f_up, f_down, f_kv, f_small = fulls
    W = dict(pw1=f_pw1, pw2=f_pw2.reshape(2, D, D), wq=f_wq.reshape(2, D, D), wo=f_wo.reshape(2, D, D),
             up=f_up, down=f_down.reshape(4, DFF, D),
             kv=f_kv.reshape(2, 4, 128, 512).transpose(1, 0, 2, 3).reshape(1, D, 512))
    fs = f_small.transpose(0, 2, 1, 3).reshape(2, 40, D)
    b_pw1_full = f_small[:, :, 35:37, :].transpose(0, 1, 2, 3).reshape(2, 1, 2 * D)
    conv_sm = fs.at[:, 34:].set(0.0)
    conv_rev = jnp.concatenate([fs[:, CONV_W - 1::-1][:, :CONV_W], jnp.zeros((2, 40 - CONV_W, D), F32)], axis=1)
    small = dict(norm_mix=norm_mix[:, None], norm_ffn=norm_ffn[:, None], norm_kv=norm_kv[None, None],
                 norm_final=norm_final[None], conv=conv_sm, conv_rev=conv_rev, b_pw1=b_pw1_full,
                 b_pw2=fs[:, 34:35], zero=jnp.zeros((1, 1, D), F32), sinks=sinks, rel_bias=rel_bias)

    loss_part, grad_x, G, S = local_step(x[0], loss_target[0], W, small)
    loss = lax.psum(loss_part, ("x", "y", "c"))

    gkv = G["kv"].reshape(4, 2, 128, 512).transpose(1, 0, 2, 3)
    gs = [jnp.stack(G["pw1"]), jnp.stack(G["pw2"]).reshape(2, 4, 256, D), jnp.stack(G["wq"]).reshape(2, 4, 256, D),
          jnp.stack(G["wo"]).reshape(2, 4, 256, D), jnp.stack(G["up"]), jnp.stack(G["down"]).reshape(4, 4, 704, D),
          gkv]
    theirs = sibling_swap_halves(gs)
    parts = []
    for t, (g, o) in enumerate(zip(gs, theirs)):
        lh = g.shape[0] // 2
        own = lax.dynamic_slice_in_dim(g, ci * lh, lh, axis=0)
        r2 = (lh * 4 * g.shape[2], g.shape[3])
        parts.append(add_n([own.reshape(r2), o.reshape(r2)], BF16, f"rs_add1_{t}").reshape(own.shape))
    got = chip_scatter(parts)
    halves = []
    for t, (p, o) in enumerate(zip(parts, got)):
        own = lax.dynamic_slice_in_dim(p, me, 1, axis=1)
        r2 = (own.shape[0] * own.shape[2], own.shape[3])
        halves.append(add_n([own.reshape(r2), o[0].reshape(r2), o[1].reshape(r2), o[2].reshape(r2)], F32,
                            f"rs_add2_{t}").reshape(own.shape))
    reduced = sibling_join(halves)

    vsum = allreduce_small(_pack_small(S))
    col = lambda a: lax.dynamic_slice_in_dim(a, me * 256, 256, axis=-1)
    grads = {}
    for l in range(2):
        base = l * R_CONV
        grads.setdefault("conv_w_dw", []).append(col(vsum[base:base + 31]))
        grads.setdefault("conv_b_dw", []).append(col(vsum[base + 31]))
        grads.setdefault("conv_ln_g", []).append(col(vsum[base + 32]))
        grads.setdefault("conv_ln_b", []).append(col(vsum[base + 33]))
        grads.setdefault("conv_b_pw2", []).append(col(vsum[base + 34]))
        grads.setdefault("conv_b_pw1", []).append(
            lax.dynamic_slice_in_dim(vsum[base + 35:base + 37].reshape(2 * D), me * 512, 512, axis=0))
    grads = {k: jnp.stack(v) for k, v in grads.items()}
    base = 2 * R_CONV
    grads["norm_mix"] = vsum[base:base + 4]
    grads["norm_ffn"] = vsum[base + 4:base + 8]
    grads["norm_kv"] = vsum[base + 8]
    grads["norm_final"] = vsum[base + 9]
    grads["sinks"] = vsum[base + 10, 0:32].reshape(2, 16)
    grads["rel_bias"] = vsum[base + 10, 32:32 + 512].reshape(32, 16)
    for n, r in zip(big, reduced):
        grads[n] = r.reshape(weights[n].shape)

    delta, new_m, new_v = {}, {}, {}
    for n in big:
        shp = weights[n].shape
        r2 = (int(np.prod(shp[:-1])), shp[-1])
        d, nm, nv = adamw(weights[n].reshape(r2), grads[n].reshape(r2), mom_m[n].reshape(r2), mom_v[n].reshape(r2),
                          f"adamw_{n}")
        delta[n], new_m[n], new_v[n] = d.reshape(shp), nm.reshape(shp), nv.reshape(shp)
    rest = [n for n in weights if n not in big]

    def pack(dct):
        flat = jnp.concatenate([dct[n].reshape(-1) for n in rest])
        return jnp.pad(flat, (0, (-flat.shape[0]) % (8 * 128))).reshape(-1, 128)

    d, nm, nv = adamw(pack(weights), pack(grads), pack(mom_m), pack(mom_v), "adamw_small")
    off = 0
    for n in rest:
        shp = weights[n].shape
        sz = int(np.prod(shp))
        delta[n] = d.reshape(-1)[off:off + sz].reshape(shp)
        new_m[n] = nm.reshape(-1)[off:off + sz].reshape(shp)
        new_v[n] = nv.reshape(-1)[off:off + sz].reshape(shp)
        off += sz

    order = list(weights)
    return (loss, grad_x[None], *[grads[n] for n in order], *[delta[n] for n in order],
            *[new_m[n] for n in order], *[new_v[n] for n in order])
```

```python
import functools
import math

import numpy as np
import jax
import jax.numpy as jnp
from jax import lax
from jax.experimental import pallas as pl
from jax.experimental.pallas import tpu as pltpu

F32 = jnp.float32
BF16 = jnp.bfloat16
MESH = pl.DeviceIdType.MESH

D = 1024
DFF = 2816
N_HEADS = 16
N_KV = 4
GROUP = 4
HD = 64
BLK = 128
CONV_W = 31
HALO = 32
N_BUCKETS = 32
MAX_DISTANCE = 128
EPS = 1e-6
NEG_INF = -1e30
TM = 256
VMEM_LIMIT = 56 * 2 ** 20

ADAM_LR, ADAM_B1, ADAM_B2, ADAM_EPS, ADAM_WD, ADAM_STEP = 0.001, 0.9, 0.999, 1e-08, 0.01, 10


def _cp(*sem):
    return pltpu.CompilerParams(dimension_semantics=sem, vmem_limit_bytes=VMEM_LIMIT)


def _sigmoid(x):
    return 1.0 / (1.0 + jnp.exp(-x))


def _row(tm, n):
    return pl.BlockSpec((tm, n), lambda i: (i, 0))


def _const(shape):
    nd = len(shape)
    return pl.BlockSpec(shape, lambda i: (0,) * nd)


def _layer(shape, l):
    nd = len(shape)
    return pl.BlockSpec((None,) + tuple(shape), lambda i: (l,) + (0,) * nd)


def _dot(a, b):
    return jnp.dot(a, b, preferred_element_type=F32)


def _dot_nt(a, b):
    return lax.dot_general(a, b, (((1,), (1,)), ((), ())), preferred_element_type=F32)


def _dot_tn(a, b):
    return lax.dot_general(a, b, (((0,), (0,)), ((), ())), preferred_element_type=F32)


def _rms(x):
    return lax.rsqrt(jnp.mean(x * x, axis=-1, keepdims=True) + EPS)


def norm_mm_glu(h, g, l, w, b, name):
    T = h.shape[0]
    ns = w.shape[-1]

    def body(h_ref, g_ref, w_ref, b_ref, xn_ref, u_ref, a_ref):
        x = h_ref[...]
        xn = (x * _rms(x) * g_ref[...]).astype(BF16)
        xn_ref[...] = xn
        for s in range(2):
            lo, hi = s * ns, (s + 1) * ns
            u1 = _dot(xn, w_ref[s]) + b_ref[:, lo:hi]
            u2 = _dot(xn, w_ref[2 + s]) + b_ref[:, D + lo:D + hi]
            u_ref[:, lo:hi] = u1.astype(BF16)
            u_ref[:, D + lo:D + hi] = u2.astype(BF16)
            a_ref[:, lo:hi] = u1 * _sigmoid(u2)

    return pl.pallas_call(
        body, name=name, grid=(T // TM,),
        in_specs=[_row(TM, D), _layer((1, D), l), _const((4, D, ns)), _layer((1, 2 * D), l)],
        out_specs=[_row(TM, D), _row(TM, 2 * D), _row(TM, D)],
        out_shape=[jax.ShapeDtypeStruct((T, D), BF16), jax.ShapeDtypeStruct((T, 2 * D), BF16),
                   jax.ShapeDtypeStruct((T, D), F32)],
        compiler_params=_cp("parallel"),
    )(h, g, w, b)


def _conv_taps(buf_ref, w_ref, out_ref, first):
    RB, LB = 32, 512
    for r0 in range(0, TM, RB):
        for c0 in range(0, D, LB):
            acc = jnp.zeros((RB, LB), F32)
            for k in range(CONV_W):
                acc = acc + w_ref[k:k + 1, c0:c0 + LB] * buf_ref[pl.ds(first + k + r0, RB), c0:c0 + LB]
            out_ref[r0:r0 + RB, c0:c0 + LB] = acc


def dwconv_ln_silu(a, sm, l, name):
    T = a.shape[0]
    nb = TM // HALO

    def body(cur_ref, prev_ref, sm_ref, y_ref, s_ref, buf):
        i = pl.program_id(0)
        buf[0:HALO, :] = jnp.where(i > 0, prev_ref[...], 0.0)
        buf[HALO:HALO + TM, :] = cur_ref[...]
        _conv_taps(buf, sm_ref, y_ref, HALO - (CONV_W - 1))
        y = y_ref[...] + sm_ref[31:32, :]
        y_ref[...] = y
        mu = jnp.mean(y, axis=-1, keepdims=True)
        yc = y - mu
        rstd = lax.rsqrt(jnp.mean(yc * yc, axis=-1, keepdims=True) + EPS)
        z = yc * rstd * sm_ref[32:33, :] + sm_ref[33:34, :]
        s_ref[...] = (z * _sigmoid(z)).astype(BF16)

    return pl.pallas_call(
        body, name=name, grid=(T // TM,),
        in_specs=[_row(TM, D), pl.BlockSpec((HALO, D), lambda i: (jnp.maximum(i * nb - 1, 0), 0)),
                  _layer((40, D), l)],
        out_specs=[_row(TM, D), _row(TM, D)],
        out_shape=[jax.ShapeDtypeStruct((T, D), F32), jax.ShapeDtypeStruct((T, D), BF16)],
        scratch_shapes=[pltpu.VMEM((TM + HALO, D), F32)],
        compiler_params=_cp("parallel"),
    )(a, a, sm)


def mm_bias_res(xb, w, b, bl, res, name):
    T, K = xb.shape

    def body(x_ref, w_ref, b_ref, r_ref, o_ref):
        o_ref[...] = _dot(x_ref[...], w_ref[...]) + b_ref[...] + r_ref[...]

    return pl.pallas_call(
        body, name=name, grid=(T // TM,),
        in_specs=[_row(TM, K), _const((K, D)), _layer((1, D), bl), _row(TM, D)],
        out_specs=_row(TM, D), out_shape=jax.ShapeDtypeStruct((T, D), F32),
        compiler_params=_cp("parallel"),
    )(xb, w, b, res)


def norm_mm_swiglu(h, g, l, w, name):
    T = h.shape[0]
    ns = w.shape[-1]

    def body(h_ref, g_ref, w_ref, xn_ref, gu_ref, f_ref):
        x = h_ref[...]
        xn = (x * _rms(x) * g_ref[...]).astype(BF16)
        xn_ref[...] = xn
        for s in range(2):
            lo, hi = s * ns, (s + 1) * ns
            gate = _dot(xn, w_ref[s])
            up = _dot(xn, w_ref[2 + s])
            gu_ref[:, lo:hi] = gate.astype(BF16)
            gu_ref[:, DFF + lo:DFF + hi] = up.astype(BF16)
            f_ref[:, lo:hi] = (gate * _sigmoid(gate) * up).astype(BF16)

    return pl.pallas_call(
        body, name=name, grid=(T // TM,),
        in_specs=[_row(TM, D), _layer((1, D), l), _const((4, D, ns))],
        out_specs=[_row(TM, D), _row(TM, 2 * DFF), _row(TM, DFF)],
        out_shape=[jax.ShapeDtypeStruct((T, D), BF16), jax.ShapeDtypeStruct((T, 2 * DFF), BF16),
                   jax.ShapeDtypeStruct((T, DFF), BF16)],
        compiler_params=_cp("parallel"),
    )(h, g, w)


def norm_mm(h, g, gl, w, name, scale=1.0):
    T = h.shape[0]
    N = w.shape[-1]

    def body(h_ref, g_ref, w_ref, xn_ref, o_ref):
        x = h_ref[...]
        xn = (x * _rms(x) * g_ref[...]).astype(BF16)
        xn_ref[...] = xn
        o_ref[...] = (_dot(xn, w_ref[...]) * scale).astype(BF16)

    return pl.pallas_call(
        body, name=name, grid=(T // TM,),
        in_specs=[_row(TM, D), _layer((1, D), gl), _const((D, N))],
        out_specs=[_row(TM, D), _row(TM, N)],
        out_shape=[jax.ShapeDtypeStruct((T, D), BF16), jax.ShapeDtypeStruct((T, N), BF16)],
        compiler_params=_cp("parallel"),
    )(h, g, w)


QB = 4


def band_mask():
    qi = np.arange(GROUP * BLK)[:, None] % BLK
    kj = np.arange(2 * BLK)[None, :]
    band = ((kj < BLK) & (kj > qi)) | ((kj >= BLK) & (kj - BLK <= qi))
    first = band & (kj >= BLK)
    return np.where(np.stack([first, band]), 0.0, NEG_INF).astype(np.float32)


def _attn_softmax(q4, kb, bias, sink):
    s = _dot_nt(q4, kb) + bias
    m = jnp.maximum(jnp.max(s, axis=-1, keepdims=True), sink)
    p = jnp.exp(s - m)
    es = jnp.exp(sink - m)
    inv = 1.0 / (jnp.sum(p, axis=-1, keepdims=True) + es)
    return p, inv, es


def _attn_specs(T):
    qspec = pl.BlockSpec((None, GROUP, QB * BLK, HD), lambda kv, n: (kv, 0, n, 0))
    kspec = pl.BlockSpec((None, T + BLK, HD), lambda kv, n: (kv, 0, 0))
    bspec = pl.BlockSpec((2, None, GROUP * BLK, 2 * BLK), lambda kv, n: (0, kv, 0, 0))
    sspec = pl.BlockSpec((None, GROUP * BLK, 1), lambda kv, n: (kv, 0, 0))
    return qspec, kspec, bspec, sspec


def _attn_block(n, b, q_ref, b_ref):
    blk = n * QB + b
    rows = pl.ds(pl.multiple_of(blk * BLK, BLK), 2 * BLK)
    q4 = q_ref[:, b * BLK:(b + 1) * BLK, :].reshape(GROUP * BLK, HD)
    bias = b_ref[jnp.minimum(blk, 1)] if b == 0 else b_ref[1]
    return rows, q4, bias


def attn_fwd(q, kp, vp, bias, sink, name):
    T = q.shape[2]
    qspec, kspec, bspec, sspec = _attn_specs(T)

    def body(q_ref, k_ref, v_ref, b_ref, s_ref, o_ref):
        n = pl.program_id(1)
        for b in range(QB):
            rows, q4, bias = _attn_block(n, b, q_ref, b_ref)
            p, inv, _ = _attn_softmax(q4, k_ref[rows, :], bias, s_ref[...])
            o = _dot(p.astype(BF16), v_ref[rows, :]) * inv
            o_ref[:, b * BLK:(b + 1) * BLK, :] = o.reshape(GROUP, BLK, HD).astype(BF16)

    return pl.pallas_call(
        body, name=name, grid=(N_KV, T // (QB * BLK)),
        in_specs=[qspec, kspec, kspec, bspec, sspec], out_specs=qspec,
        out_shape=jax.ShapeDtypeStruct((N_KV, GROUP, T, HD), BF16),
        compiler_params=_cp("parallel", "parallel"),
    )(q, kp, vp, bias, sink)


def attn_bwd(q, kp, vp, bias, sink, o, do, name):
    T = q.shape[2]
    qspec, kspec, bspec, sspec = _attn_specs(T)

    def body(q_ref, k_ref, v_ref, b_ref, s_ref, o_ref, do_ref, dq_ref, dk_ref, dv_ref, db_ref, ds_ref):
        n = pl.program_id(1)

        @pl.when(n == 0)
        def _():
            dk_ref[...] = jnp.zeros_like(dk_ref)
            dv_ref[...] = jnp.zeros_like(dv_ref)
            db_ref[...] = jnp.zeros_like(db_ref)
            ds_ref[...] = jnp.zeros_like(ds_ref)

        for b in range(QB):
            rows, q4, bias = _attn_block(n, b, q_ref, b_ref)
            do4 = do_ref[:, b * BLK:(b + 1) * BLK, :].reshape(GROUP * BLK, HD)
            o4 = o_ref[:, b * BLK:(b + 1) * BLK, :].reshape(GROUP * BLK, HD)
            kb = k_ref[rows, :]
            vb = v_ref[rows, :]
            p, inv, es = _attn_softmax(q4, kb, bias, s_ref[...])
            probs = p * inv
            dp = _dot_nt(do4, vb)
            delta = jnp.sum(do4.astype(F32) * o4.astype(F32), axis=-1, keepdims=True)
            dS = probs * (dp - delta)
            ds_ref[...] += -(es * inv) * delta
            db_ref[...] += dS
            dSb = dS.astype(BF16)
            dq_ref[:, b * BLK:(b + 1) * BLK, :] = (_dot(dSb, kb) * (HD ** -0.5)).reshape(GROUP, BLK, HD).astype(BF16)
            dk_ref[rows, :] += _dot_tn(dSb, q4)
            dv_ref[rows, :] += _dot_tn(probs.astype(BF16), do4)

    kout = pl.BlockSpec((None, T + BLK, HD), lambda kv, n: (kv, 0, 0))
    dbspec = pl.BlockSpec((None, GROUP * BLK, 2 * BLK), lambda kv, n: (kv, 0, 0))
    return pl.pallas_call(
        body, name=name, grid=(N_KV, T // (QB * BLK)),
        in_specs=[qspec, kspec, kspec, bspec, sspec, qspec, qspec],
        out_specs=[qspec, kout, kout, dbspec, sspec],
        out_shape=[jax.ShapeDtypeStruct((N_KV, GROUP, T, HD), BF16),
                   jax.ShapeDtypeStruct((N_KV, T + BLK, HD), F32), jax.ShapeDtypeStruct((N_KV, T + BLK, HD), F32),
                   jax.ShapeDtypeStruct((N_KV, GROUP * BLK, 2 * BLK), F32),
                   jax.ShapeDtypeStruct((N_KV, GROUP * BLK, 1), F32)],
        compiler_params=_cp("parallel", "arbitrary"),
    )(q, kp, vp, bias, sink, o, do)


def final_loss(h, g, target, name):
    T = h.shape[0]

    def body(h_ref, g_ref, t_ref, dh_ref, st_ref):
        i = pl.program_id(0)

        @pl.when(i == 0)
        def _():
            st_ref[...] = jnp.zeros_like(st_ref)

        x = h_ref[...]
        r = _rms(x)
        xh = x * r
        e = xh * g_ref[...] - t_ref[...]
        loss = 0.5 * jnp.sum(jnp.mean(e * e, axis=-1, keepdims=True))
        dy = e * (1.0 / D)
        st_ref[0:1, :] += jnp.sum(dy * xh, axis=0, keepdims=True)
        lane = lax.broadcasted_iota(jnp.int32, (1, D), 1)
        st_ref[1:2, :] += jnp.where(lane == 0, loss, 0.0)
        dxh = dy * g_ref[...]
        dh_ref[...] = r * (dxh - xh * jnp.mean(dxh * xh, axis=-1, keepdims=True))

    return pl.pallas_call(
        body, name=name, grid=(T // TM,),
        in_specs=[_row(TM, D), _const((1, D)), _row(TM, D)],
        out_specs=[_row(TM, D), _const((8, D))],
        out_shape=[jax.ShapeDtypeStruct((T, D), F32), jax.ShapeDtypeStruct((8, D), F32)],
        compiler_params=_cp("arbitrary"),
    )(h, g, target)


def mm_dw(x, dy, name, tn, slots, colsum=False):
    T, K = x.shape
    N = dy.shape[1]
    tt = min(T, 1024)
    nt = T // tt
    ns = N // slots
    per = ns // tn

    def body(x_ref, dy_ref, *rest):
        if colsum:
            dw_ref, cs_ref, acc, cacc = rest
        else:
            dw_ref, acc = rest
        t = pl.program_id(1)

        @pl.when(t == 0)
        def _():
            acc[...] = jnp.zeros_like(acc)
            if colsum:
                cacc[...] = jnp.zeros_like(cacc)

        dyv = dy_ref[...]
        acc[...] += _dot_tn(x_ref[...].astype(BF16), dyv.astype(BF16))
        if colsum:
            cacc[...] += jnp.sum(dyv.astype(F32), axis=0, keepdims=True)

        @pl.when(t == nt - 1)
        def _():
            dw_ref[...] = acc[...].astype(BF16)
            if colsum:
                cs_ref[...] = cacc[...]

    out_specs = [pl.BlockSpec((None, K, tn), lambda j, t: (j // per, 0, j % per))]
    out_shape = [jax.ShapeDtypeStruct((slots, K, ns), BF16)]
    scratch = [pltpu.VMEM((K, tn), F32)]
    if colsum:
        out_specs.append(pl.BlockSpec((1, tn), lambda j, t: (0, j)))
        out_shape.append(jax.ShapeDtypeStruct((1, N), F32))
        scratch.append(pltpu.VMEM((1, tn), F32))
    res = pl.pallas_call(
        body, name=name, grid=(N // tn, nt),
        in_specs=[pl.BlockSpec((tt, K), lambda j, t: (t, 0)), pl.BlockSpec((tt, tn), lambda j, t: (t, j))],
        out_specs=out_specs, out_shape=out_shape, scratch_shapes=scratch,
        compiler_params=_cp("parallel", "arbitrary"),
    )(x, dy)
    return tuple(res) if colsum else res[0]


def mmT_swiglu_bwd(dh, w, gu, name):
    T = dh.shape[0]
    half = DFF // 2

    def body(dh_ref, w_ref, gu_ref, du_ref):
        dhb = dh_ref[...].astype(BF16)
        for s in range(2):
            lo, hi = s * half, (s + 1) * half
            df = _dot_nt(dhb, w_ref[lo:hi, :])
            gate = gu_ref[:, lo:hi].astype(F32)
            up = gu_ref[:, DFF + lo:DFF + hi].astype(F32)
            sg = _sigmoid(gate)
            du_ref[:, lo:hi] = (df * up * sg * (1.0 + gate * (1.0 - sg))).astype(BF16)
            du_ref[:, DFF + lo:DFF + hi] = (df * gate * sg).astype(BF16)

    return pl.pallas_call(
        body, name=name, grid=(T // TM,),
        in_specs=[_row(TM, D), _const((DFF, D)), _row(TM, 2 * DFF)],
        out_specs=_row(TM, 2 * DFF), out_shape=jax.ShapeDtypeStruct((T, 2 * DFF), BF16),
        compiler_params=_cp("parallel"),
    )(dh, w, gu)


def mmT_rmsbwd(du, w, h, g, gl, dh_in, name):
    T, N = du.shape
    slots = w.shape[0]
    ns = N // slots

    def body(du_ref, w_ref, h_ref, g_ref, di_ref, dh_ref, dg_ref):
        i = pl.program_id(0)

        @pl.when(i == 0)
        def _():
            dg_ref[...] = jnp.zeros_like(dg_ref)

        dxn = _dot_nt(du_ref[:, 0:ns], w_ref[0])
        for s in range(1, slots):
            dxn = dxn + _dot_nt(du_ref[:, s * ns:(s + 1) * ns], w_ref[s])
        x = h_ref[...]
        r = _rms(x)
        xh = x * r
        dg_ref[0:1, :] += jnp.sum(dxn * xh, axis=0, keepdims=True)
        dxh = dxn * g_ref[...]
        dh_ref[...] = di_ref[...] + r * (dxh - xh * jnp.mean(dxh * xh, axis=-1, keepdims=True))

    return pl.pallas_call(
        body, name=name, grid=(T // TM,),
        in_specs=[_row(TM, N), _const((slots, D, ns)), _row(TM, D), _layer((1, D), gl), _row(TM, D)],
        out_specs=[_row(TM, D), _const((8, D))],
        out_shape=[jax.ShapeDtypeStruct((T, D), F32), jax.ShapeDtypeStruct((8, D), F32)],
        compiler_params=_cp("arbitrary"),
    )(du, w, h, g, dh_in)


def mmT(dh, w, name):
    T = dh.shape[0]
    N = w.shape[0]

    def body(dh_ref, w_ref, o_ref):
        o_ref[...] = _dot_nt(dh_ref[...].astype(BF16), w_ref[...]).astype(BF16)

    return pl.pallas_call(
        body, name=name, grid=(T // TM,),
        in_specs=[_row(TM, D), _const((N, D))],
        out_specs=_row(TM, N), out_shape=jax.ShapeDtypeStruct((T, N), BF16),
        compiler_params=_cp("parallel"),
    )(dh, w)


def mmT_lnbwd(dh, w, y, sm, l, name):
    T = dh.shape[0]

    def body(dh_ref, w_ref, y_ref, sm_ref, dy_ref, st_ref):
        i = pl.program_id(0)

        @pl.when(i == 0)
        def _():
            st_ref[...] = jnp.zeros_like(st_ref)

        ds = _dot_nt(dh_ref[...].astype(BF16), w_ref[...])
        y = y_ref[...]
        mu = jnp.mean(y, axis=-1, keepdims=True)
        yc = y - mu
        rstd = lax.rsqrt(jnp.mean(yc * yc, axis=-1, keepdims=True) + EPS)
        xh = yc * rstd
        gam = sm_ref[32:33, :]
        z = xh * gam + sm_ref[33:34, :]
        sg = _sigmoid(z)
        dz = ds * sg * (1.0 + z * (1.0 - sg))
        st_ref[0:1, :] += jnp.sum(dz * xh, axis=0, keepdims=True)
        st_ref[1:2, :] += jnp.sum(dz, axis=0, keepdims=True)
        dxh = dz * gam
        dy = rstd * (dxh - jnp.mean(dxh, axis=-1, keepdims=True) - xh * jnp.mean(dxh * xh, axis=-1, keepdims=True))
        st_ref[2:3, :] += jnp.sum(dy, axis=0, keepdims=True)
        dy_ref[...] = dy

    return pl.pallas_call(
        body, name=name, grid=(T // TM,),
        in_specs=[_row(TM, D), _const((D, D)), _row(TM, D), _layer((40, D), l)],
        out_specs=[_row(TM, D), _const((8, D))],
        out_shape=[jax.ShapeDtypeStruct((T, D), F32), jax.ShapeDtypeStruct((8, D), F32)],
        compiler_params=_cp("arbitrary"),
    )(dh, w, y, sm)


def dwconv_glu_bwd(dy, a, u, sm, smrev, l, name):
    T = dy.shape[0]
    nb = TM // HALO
    last = T // HALO - 1

    def body(dy_ref, dyn_ref, a_ref, ap_ref, u_ref, sm_ref, rev_ref, du_ref, dw_ref, bufd, bufa, da):
        i = pl.program_id(0)

        @pl.when(i == 0)
        def _():
            dw_ref[...] = jnp.zeros_like(dw_ref)

        bufd[0:TM, :] = dy_ref[...]
        bufd[TM:TM + HALO, :] = jnp.where(i < pl.num_programs(0) - 1, dyn_ref[...], 0.0)
        bufa[0:HALO, :] = jnp.where(i > 0, ap_ref[...], 0.0)
        bufa[HALO:HALO + TM, :] = a_ref[...]
        _conv_taps(bufd, rev_ref, da, 0)
        LB = 512
        for c0 in range(0, D, LB):
            for k in range(CONV_W):
                acc = jnp.zeros((8, LB), F32)
                for r0 in range(0, TM, 8):
                    acc = acc + dy_ref[r0:r0 + 8, c0:c0 + LB] * bufa[pl.ds(HALO - (CONV_W - 1) + k + r0, 8), c0:c0 + LB]
                dw_ref[k:k + 1, c0:c0 + LB] += jnp.sum(acc, axis=0, keepdims=True)
        dav = da[...]
        u1 = u_ref[:, 0:D].astype(F32)
        sg = _sigmoid(u_ref[:, D:2 * D].astype(F32))
        du_ref[:, 0:D] = (dav * sg).astype(BF16)
        du_ref[:, D:2 * D] = (dav * u1 * sg * (1.0 - sg)).astype(BF16)

    return pl.pallas_call(
        body, name=name, grid=(T // TM,),
        in_specs=[_row(TM, D), pl.BlockSpec((HALO, D), lambda i: (jnp.minimum((i + 1) * nb, last), 0)),
                  _row(TM, D), pl.BlockSpec((HALO, D), lambda i: (jnp.maximum(i * nb - 1, 0), 0)),
                  _row(TM, 2 * D), _layer((40, D), l), _layer((40, D), l)],
        out_specs=[_row(TM, 2 * D), _const((32, D))],
        out_shape=[jax.ShapeDtypeStruct((T, 2 * D), BF16), jax.ShapeDtypeStruct((32, D), F32)],
        scratch_shapes=[pltpu.VMEM((TM + HALO, D), F32), pltpu.VMEM((TM + HALO, D), F32), pltpu.VMEM((TM, D), F32)],
        compiler_params=_cp("arbitrary"),
    )(dy, dy, a, a, u, sm, smrev)


def _rows_tile(R):
    for t in (512, 256, 128, 64, 32, 16, 8):
        if R % t == 0:
            return t
    return R


def add8(own, others, name):
    R, C = own.shape
    tr = _rows_tile(R)

    def body(o_ref, x_ref, out_ref):
        acc = o_ref[...].astype(F32)
        for k in range(7):
            acc = acc + x_ref[k].astype(F32)
        out_ref[...] = acc

    return pl.pallas_call(
        body, name=name, grid=(R // tr,),
        in_specs=[_row(tr, C), pl.BlockSpec((7, tr, C), lambda i: (0, i, 0))], out_specs=_row(tr, C),
        out_shape=jax.ShapeDtypeStruct((R, C), F32), compiler_params=_cp("parallel"),
    )(own, others)


def adamw(w, g, m, v, name):
    R, C = w.shape
    tr = _rows_tile(R)

    def body(w_ref, g_ref, m_ref, v_ref, d_ref, nm_ref, nv_ref):
        gv = g_ref[...]
        nm = ADAM_B1 * m_ref[...] + (1.0 - ADAM_B1) * gv
        nv = ADAM_B2 * v_ref[...] + (1.0 - ADAM_B2) * (gv * gv)
        m_hat = nm / (1.0 - ADAM_B1 ** ADAM_STEP)
        v_hat = nv / (1.0 - ADAM_B2 ** ADAM_STEP)
        d_ref[...] = -ADAM_LR * (m_hat / (jnp.sqrt(v_hat) + ADAM_EPS) + ADAM_WD * w_ref[...])
        nm_ref[...] = nm
        nv_ref[...] = nv

    sd = jax.ShapeDtypeStruct((R, C), F32)
    return pl.pallas_call(
        body, name=name, grid=(R // tr,),
        in_specs=[_row(tr, C)] * 4, out_specs=[_row(tr, C)] * 3, out_shape=[sd, sd, sd],
        compiler_params=_cp("parallel"),
    )(w, g, m, v)


ANY = pl.BlockSpec(memory_space=pl.ANY)
HBM = pl.BlockSpec(memory_space=pltpu.HBM)
SEM = pl.BlockSpec(memory_space=pltpu.SEMAPHORE)
EFFECT = pltpu.SideEffectType.DATAFLOW_SIDE_EFFECTING


def _place():
    x, y, c = lax.axis_index("x"), lax.axis_index("y"), lax.axis_index("c")
    chips = [(1 - x, y), (x, 1 - y), (1 - x, 1 - y)]
    return x, y, c, chips


def _copy(src, dst, send, recv, k, to):
    return pltpu.make_async_remote_copy(src_ref=src, dst_ref=dst, send_sem=send.at[k], recv_sem=recv.at[k],
                                        device_id=to, device_id_type=MESH)


def xchg_start(name, bufs, plan, n):
    nb = len(bufs)

    def body(*refs):
        send, recv, token = refs[nb], refs[nb + 1], refs[-1]
        for k, (src, dst, to) in enumerate(plan(refs[:nb])):
            _copy(src, dst, send, recv, k, to).start()
        token[...] = jnp.zeros_like(token)

    outs = pl.pallas_call(
        body, name=name,
        out_shape=(pltpu.SemaphoreType.DMA((n,)), pltpu.SemaphoreType.DMA((n,)),
                   *[pltpu.HBM(b.shape, b.dtype) for b in bufs], jax.ShapeDtypeStruct((8, 128), F32)),
        in_specs=[HBM] * nb,
        out_specs=(SEM, SEM, *[HBM] * nb, pl.BlockSpec(memory_space=pltpu.VMEM)),
        input_output_aliases={i: 2 + i for i in range(nb)},
        compiler_params=pltpu.CompilerParams(has_side_effects=EFFECT),
    )(*[pltpu.with_memory_space_constraint(b, pltpu.HBM) for b in bufs])
    return dict(name=name, send=outs[0], recv=outs[1], bufs=list(outs[2:2 + nb]), plan=plan), outs[-1]


def xchg_wait(flight, after):
    bufs, plan = flight["bufs"], flight["plan"]
    nb = len(bufs)

    def body(*refs):
        send, recv = refs[nb], refs[nb + 1]
        for k, (src, dst, to) in enumerate(plan(refs[:nb])):
            cp = _copy(src, dst, send, recv, k, to)
            cp.wait_send()
            cp.wait_recv()

    outs = pl.pallas_call(
        body, name=flight["name"] + "_wait",
        out_shape=tuple(pltpu.HBM(b.shape, b.dtype) for b in bufs),
        in_specs=[HBM] * nb + [SEM, SEM] + [ANY] * len(after),
        out_specs=tuple([HBM] * nb), input_output_aliases={i: i for i in range(nb)},
        compiler_params=pltpu.CompilerParams(has_side_effects=EFFECT),
    )(*bufs, flight["send"], flight["recv"], *after)
    return list(outs)


def _flip(k, x, y, c):
    return ((1 - x) if k & 4 else x, (1 - y) if k & 2 else y, (1 - c) if k & 1 else c)


class WeightGather:
    def __init__(self, shards, groups):
        me = 2 * lax.axis_index("x") + lax.axis_index("y")
        self.names = dict(groups)
        self.ici, self.d2d = {}, {}
        for gname, names in groups:
            nt = len(names)
            srcs = [shards[n] for n in names]
            lands = [lax.dynamic_update_slice(lax.empty((4,) + s.shape, s.dtype), s[None], (me, 0, 0, 0))
                     for s in srcs]

            def plan(refs, nt=nt):
                x, y, c, chips = _place()
                return [(refs[t].at[c], refs[nt + t].at[2 * x + y, c], (cx, cy, c))
                        for t in range(nt) for cx, cy in chips]

            self.ici[gname], _ = xchg_start(f"ag_ici_{gname}", srcs + lands, plan, 3 * nt)

    def forward(self, gname, after):
        nt = len(self.names[gname])
        lands = xchg_wait(self.ici.pop(gname), after)[nt:]

        def plan(refs):
            x, y, c, chips = _place()
            out = []
            for t in range(nt):
                for cx, cy in chips:
                    piece = refs[t].at[2 * cx + cy, c]
                    out.append((piece, piece, (x, y, 1 - c)))
            return out

        self.d2d[gname], token = xchg_start(f"ag_d2d_{gname}", lands, plan, 3 * nt)
        return token

    def get(self, gname, after):
        lands = xchg_wait(self.d2d.pop(gname), after)
        return dict(zip(self.names[gname], lands))


class GradReduce:
    def __init__(self, kinds):
        self.J = {k: lax.empty((L, 2, a2, b), F32) for k, (L, a2, b) in kinds.items()}
        self.x, self.j = {}, {}

    @staticmethod
    def _where(name):
        kind, _, l = name.partition("_")
        return kind, int(l or 0)

    def send(self, gname, grads):
        names = list(grads)
        nt = len(names)
        gs = [grads[n] for n in names]
        xs = [lax.empty((7,) + g.shape[2:], g.dtype) for g in gs]

        def plan(refs):
            x, y, c, _ = _place()
            out = []
            for t in range(nt):
                for k in range(1, 8):
                    px, py, pc = _flip(k, x, y, c)
                    out.append((refs[t].at[2 * px + py, pc], refs[nt + t].at[k - 1], (px, py, pc)))
            return out

        flight, token = xchg_start(f"rs_x_{gname}", gs + xs, plan, 7 * nt)
        self.x[gname] = (names, flight)
        return token

    def reduce(self, gname, after):
        names, flight = self.x.pop(gname)
        nt = len(names)
        bufs = xchg_wait(flight, after)
        me, c = 2 * lax.axis_index("x") + lax.axis_index("y"), lax.axis_index("c")
        hs = []
        for t, n in enumerate(names):
            g = bufs[t]
            own = lax.dynamic_slice(g, (me, c, 0, 0), (1, 1) + g.shape[2:])[0, 0]
            hs.append(add8(own, bufs[nt + t], f"rs_add_{n}"))
        where = [self._where(n) for n in names]

        def plan(refs):
            x, y, c, _ = _place()
            return [(refs[t], refs[nt + t].at[where[t][1], c], (x, y, 1 - c)) for t in range(nt)]

        flight, token = xchg_start(f"rs_join_{gname}", hs + [self.J[k] for k, _ in where], plan, nt)
        self.j[gname] = (where, flight)
        return token

    def finish(self, gname, after):
        where, flight = self.j.pop(gname)
        nt = len(where)
        bufs = xchg_wait(flight, after)
        c = lax.axis_index("c")
        for t, (kind, l) in enumerate(where):
            self.J[kind] = lax.dynamic_update_slice(bufs[nt + t], bufs[t][None, None], (l, c, 0, 0))


def allreduce_small(v):
    R = v.shape[0]

    def body(v_ref, o_ref, all_ref, send, recv):
        x, y, c, _ = _place()
        me = 4 * x + 2 * y + c
        all_ref[me] = v_ref[...]
        cps = []
        for k in range(1, 8):
            cp = _copy(v_ref, all_ref.at[me], send, recv, k - 1, _flip(k, x, y, c))
            cp.start()
            cps.append(cp)
        for k in range(1, 8):
            px, py, pc = _flip(k, x, y, c)
            _copy(v_ref, all_ref.at[4 * px + 2 * py + pc], send, recv, k - 1, (px, py, pc)).wait_recv()
        for cp in cps:
            cp.wait_send()
        acc = all_ref[0]
        for d in range(1, 8):
            acc = acc + all_ref[d]
        o_ref[...] = acc

    return pl.pallas_call(
        body, name="allreduce_small",
        in_specs=[pl.BlockSpec(memory_space=pltpu.VMEM)], out_specs=pl.BlockSpec(memory_space=pltpu.VMEM),
        out_shape=jax.ShapeDtypeStruct((R, D), F32),
        scratch_shapes=[pltpu.VMEM((8, R, D), F32), pltpu.SemaphoreType.DMA((7,)), pltpu.SemaphoreType.DMA((7,))],
        compiler_params=pltpu.CompilerParams(has_side_effects=True, vmem_limit_bytes=VMEM_LIMIT),
    )(v)


AG_GROUPS = (("a0", ("pw1_0", "pw2_0", "small")), ("f0", ("up_0", "down_0")),
             ("l1", ("pw1_1", "pw2_1", "up_1", "down_1")), ("l2", ("kv", "wq_0", "wo_0", "up_2", "down_2")),
             ("l3", ("wq_1", "wo_1", "up_3", "down_3")))


def _bucket_table():
    qi = np.arange(BLK)[:, None]
    kj = np.arange(2 * BLK)[None, :]
    d = np.maximum(qi + BLK - kj, 0)
    max_exact = N_BUCKETS // 2
    log_ratio = (np.log(np.maximum(d, 1).astype(np.float32) / np.float32(max_exact))
                 / np.float32(math.log(MAX_DISTANCE / max_exact))).astype(np.float32)
    large = max_exact + (log_ratio * np.float32(N_BUCKETS - max_exact)).astype(np.int32)
    large = np.minimum(large, N_BUCKETS - 1)
    return np.where(d < max_exact, d, large).astype(np.int32)


def _heads_major(a, nh):
    T = a.shape[0]
    return a.reshape(T, nh, HD).transpose(1, 0, 2)


def _heads_minor(a):
    nh, T, _ = a.shape
    return a.transpose(1, 0, 2).reshape(T, nh * HD)


def _slots(land):
    return land.reshape(4, 2 * land.shape[2], land.shape[3])


def _rows(land):
    return land.reshape(8 * land.shape[2], land.shape[3])


def _gview(g):
    s, K, n = g.shape
    return g.reshape(4, 2, K // 2, n) if s == 4 else g.reshape(4, 2, K // 8, n)


def _gate(a, token):
    return a + token[0, 0]


def _conv_small(f_small):
    fs = f_small.transpose(1, 2, 0, 3).reshape(2, 40, D)
    b_pw1 = f_small[:, :, 35:37, :].transpose(1, 0, 2, 3).reshape(2, 1, 2 * D)
    rev = jnp.concatenate([fs[:, CONV_W - 1::-1], jnp.zeros((2, 40 - CONV_W, D), F32)], axis=1)
    return dict(conv=fs, conv_rev=rev, b_pw1=b_pw1, b_pw2=fs[:, 34:35])


def run_step(x, target, P, ag, rs):
    T = x.shape[0]
    zero = jnp.zeros((1, 1, D), F32)
    nm, nf = P["norm_mix"], P["norm_ffn"]
    ag.forward("a0", [])
    W = ag.get("a0", [])
    sm = _conv_small(W["small"])
    h = x
    saved = []
    for l in range(2):
        xn, u, a = norm_mm_glu(h, nm, l, _slots(W[f"pw1_{l}"]), sm["b_pw1"], f"f_pw1_{l}")
        y, s = dwconv_ln_silu(a, sm["conv"], l, f"f_conv_{l}")
        b2 = sm["b_pw2"]
        if l == 0:
            b2 = _gate(b2, ag.forward("f0", [s]))
        h1 = mm_bias_res(s, _rows(W[f"pw2_{l}"]), b2, l, h, f"f_pw2_{l}")
        if l == 0:
            W.update(ag.get("f0", [h1]))
        xn2, gu, f = norm_mm_swiglu(h1, nf, l, _slots(W[f"up_{l}"]), f"f_up_{l}")
        nxt = "l1" if l == 0 else "l2"
        h2 = mm_bias_res(f, _rows(W[f"down_{l}"]), _gate(zero, ag.forward(nxt, [f])), 0, h1, f"f_down_{l}")
        W.update(ag.get(nxt, [h2]))
        saved.append(dict(h=h, xn=xn, u=u, a=a, y=y, s=s, h1=h1, xn2=xn2, gu=gu, f=f))
        h = h2
    h_kv = h
    kvn, kv = norm_mm(h, P["norm_kv"], 0, _rows(W["kv"]), "f_kv")
    kp = jnp.pad(_heads_major(kv[:, :N_KV * HD], N_KV), ((0, 0), (BLK, 0), (0, 0)))
    vp = jnp.pad(_heads_major(kv[:, N_KV * HD:], N_KV), ((0, 0), (BLK, 0), (0, 0)))
    bucket = _bucket_table()
    onehot = jnp.asarray(np.eye(N_BUCKETS, dtype=np.float32)[bucket])
    bias = jnp.einsum("qkb,bh->hqk", onehot, P["rel_bias"], precision=lax.Precision.HIGHEST)
    bias = bias.reshape(1, N_KV, GROUP * BLK, 2 * BLK) + jnp.asarray(band_mask())[:, None]
    for j in range(2):
        l = 2 + j
        xn, q = norm_mm(h, nm, l, _rows(W[f"wq_{j}"]), f"f_q_{j}", scale=HD ** -0.5)
        qh = _heads_major(q, N_HEADS).reshape(N_KV, GROUP, T, HD)
        sink = jnp.broadcast_to(P["sinks"][j].reshape(N_KV, GROUP, 1, 1), (N_KV, GROUP, BLK, 1))
        sink = sink.reshape(N_KV, GROUP * BLK, 1)
        oh = attn_fwd(qh, kp, vp, bias, sink, f"f_attn_{j}")
        attn = _heads_minor(oh.reshape(N_HEADS, T, HD))
        h1 = mm_bias_res(attn, _rows(W[f"wo_{j}"]), zero, 0, h, f"f_wo_{j}")
        xn2, gu, f = norm_mm_swiglu(h1, nf, l, _slots(W[f"up_{l}"]), f"f_up_{l}")
        zg = _gate(zero, ag.forward("l3", [f])) if j == 0 else zero
        h2 = mm_bias_res(f, _rows(W[f"down_{l}"]), zg, 0, h1, f"f_down_{l}")
        if j == 0:
            W.update(ag.get("l3", [h2]))
        saved.append(dict(h=h, xn=xn, qh=qh, oh=oh, sink=sink, attn=attn, h1=h1, xn2=xn2, gu=gu, f=f))
        h = h2

    dh, st_final = final_loss(h, P["norm_final"], target, "loss_head")

    S = dict(norm_ffn=[None] * 4, norm_mix=[None] * 4, conv=[None] * 2, taps=[None] * 2, b_pw1=[None] * 2,
             b_pw2=[None] * 2, sinks=[None] * 2)

    def ffn_bwd(dh, sv, l, nf):
        du = mmT_swiglu_bwd(dh, _rows(W[f"down_{l}"]), sv["gu"], f"b_down_{l}")
        gd = mm_dw(sv["f"], dh, f"w_down_{l}", 512, 1)
        gu = mm_dw(sv["xn2"], du, f"w_up_{l}", DFF // 2, 4)
        dh, dg = mmT_rmsbwd(du, _slots(W[f"up_{l}"]), sv["h1"], nf, l, dh, f"b_up_{l}")
        S["norm_ffn"][l] = dg[0]
        return dh, {f"down_{l}": _gview(gd), f"up_{l}": _gview(gu)}

    dk = dv = dbias = None
    for j in (1, 0):
        l = 2 + j
        sv = saved[l]
        dh, grads = ffn_bwd(dh, sv, l, nf)
        dattn = mmT(dh, _rows(W[f"wo_{j}"]), f"b_wo_{j}")
        grads[f"wo_{j}"] = _gview(mm_dw(sv["attn"], dh, f"w_wo_{j}", 512, 1))
        doh = _heads_major(dattn, N_HEADS).reshape(N_KV, GROUP, T, HD)
        dqh, dkj, dvj, dbj, dsj = attn_bwd(sv["qh"], kp, vp, bias, sv["sink"], sv["oh"], doh, f"b_attn_{j}")
        dq = _heads_minor(dqh.reshape(N_HEADS, T, HD))
        grads[f"wq_{j}"] = _gview(mm_dw(sv["xn"], dq, f"w_q_{j}", 512, 1))
        dh, dg = mmT_rmsbwd(dq, _rows(W[f"wq_{j}"])[None], sv["h"], nm, l, dh, f"b_q_{j}")
        S["norm_mix"][l] = dg[0]
        S["sinks"][j] = jnp.sum(dsj.reshape(N_HEADS, BLK), axis=1)
        dk = dkj if dk is None else dk + dkj
        dv = dvj if dv is None else dv + dvj
        dbias = dbj if dbias is None else dbias + dbj
        if j == 1:
            nf = _gate(nf, rs.send("l3", grads))

    dkv = jnp.concatenate([_heads_minor(dk[:, BLK:]), _heads_minor(dv[:, BLK:])], axis=1).astype(BF16)
    grads["kv"] = _gview(mm_dw(kvn, dkv, "w_kv", 512, 1))
    dh, dg = mmT_rmsbwd(dkv, _rows(W["kv"])[None], h_kv, P["norm_kv"], 0, dh, "b_kv")
    S["norm_kv"] = dg[0]
    dbh = dbias.reshape(N_HEADS, BLK, 2 * BLK)
    S["rel_bias"] = jnp.einsum("hqk,qkb->bh", dbh, onehot, precision=lax.Precision.HIGHEST)
    nf = _gate(nf, rs.send("l2", grads) + rs.reduce("l3", [dh]))

    for l in (1, 0):
        sv = saved[l]
        dh, grads = ffn_bwd(dh, sv, l, nf)
        if l == 0:
            tok = rs.send("f0", grads)
            rs.finish("l2", [dh])
            nm = _gate(nm, tok + rs.reduce("l1", [dh]))
            grads = {}
        dy, st = mmT_lnbwd(dh, _rows(W[f"pw2_{l}"]), sv["y"], sm["conv"], l, f"b_pw2_{l}")
        g2, S["b_pw2"][l] = mm_dw(sv["s"], dh, f"w_pw2_{l}", 512, 1, colsum=True)
        du, dtaps = dwconv_glu_bwd(dy, sv["a"], sv["u"], sm["conv"], sm["conv_rev"], l, f"b_conv_{l}")
        S["conv"][l] = st[0:3]
        S["taps"][l] = dtaps[0:CONV_W]
        g1, S["b_pw1"][l] = mm_dw(sv["xn"], du, f"w_pw1_{l}", 512, 4, colsum=True)
        grads[f"pw2_{l}"], grads[f"pw1_{l}"] = _gview(g2), _gview(g1)
        dh, dg = mmT_rmsbwd(du, _slots(W[f"pw1_{l}"]), sv["h"], nm, l, dh, f"b_pw1_{l}")
        S["norm_mix"][l] = dg[0]
        if l == 1:
            tok = rs.send("l1", grads)
            rs.finish("l3", [dh])
            nf = _gate(nf, tok + rs.reduce("l2", [dh]))
    rs.send("c0", grads)
    S["norm_final"] = st_final[0]
    return st_final[1, 0], dh, S


R_CONV = 37
R_SMALL = 88


def _pack_small(S):
    rows = []
    for l in range(2):
        rows += [S["taps"][l], S["conv"][l][2:3], S["conv"][l][0:2], S["b_pw2"][l], S["b_pw1"][l].reshape(2, D)]
    rows += [jnp.stack(S["norm_mix"]), jnp.stack(S["norm_ffn"]), S["norm_kv"][None], S["norm_final"][None]]
    tail = jnp.concatenate([jnp.stack(S["sinks"]).reshape(-1), S["rel_bias"].reshape(-1)])
    rows.append(jnp.pad(tail, (0, D - tail.shape[0]))[None])
    v = jnp.concatenate(rows, axis=0)
    return jnp.pad(v, ((0, R_SMALL - v.shape[0]), (0, 0)))


def kernel(x, norm_mix, norm_ffn, conv_w_pw1, conv_b_pw1, conv_w_dw, conv_b_dw, conv_ln_g, conv_ln_b, conv_w_pw2, conv_b_pw2, norm_kv, w_kv, w_q, w_o, sinks, rel_bias, ffn_w_up, ffn_w_down, norm_final, loss_target, m_norm_mix, m_norm_ffn, m_conv_w_pw1, m_conv_b_pw1, m_conv_w_dw, m_conv_b_dw, m_conv_ln_g, m_conv_ln_b, m_conv_w_pw2, m_conv_b_pw2, m_norm_kv, m_w_kv, m_w_q, m_w_o, m_sinks, m_rel_bias, m_ffn_w_up, m_ffn_w_down, m_norm_final, v_norm_mix, v_norm_ffn, v_conv_w_pw1, v_conv_b_pw1, v_conv_w_dw, v_conv_b_dw, v_conv_ln_g, v_conv_ln_b, v_conv_w_pw2, v_conv_b_pw2, v_norm_kv, v_w_kv, v_w_q, v_w_o, v_sinks, v_rel_bias, v_ffn_w_up, v_ffn_w_down, v_norm_final):
    me = 2 * lax.axis_index("x") + lax.axis_index("y")
    weights = dict(norm_mix=norm_mix, norm_ffn=norm_ffn, conv_w_pw1=conv_w_pw1, conv_b_pw1=conv_b_pw1,
                   conv_w_dw=conv_w_dw, conv_b_dw=conv_b_dw, conv_ln_g=conv_ln_g, conv_ln_b=conv_ln_b,
                   conv_w_pw2=conv_w_pw2, conv_b_pw2=conv_b_pw2, norm_kv=norm_kv, w_kv=w_kv, w_q=w_q, w_o=w_o,
                   sinks=sinks, rel_bias=rel_bias, ffn_w_up=ffn_w_up, ffn_w_down=ffn_w_down, norm_final=norm_final)
    mom_m = dict(norm_mix=m_norm_mix, norm_ffn=m_norm_ffn, conv_w_pw1=m_conv_w_pw1, conv_b_pw1=m_conv_b_pw1,
                 conv_w_dw=m_conv_w_dw, conv_b_dw=m_conv_b_dw, conv_ln_g=m_conv_ln_g, conv_ln_b=m_conv_ln_b,
                 conv_w_pw2=m_conv_w_pw2, conv_b_pw2=m_conv_b_pw2, norm_kv=m_norm_kv, w_kv=m_w_kv, w_q=m_w_q,
                 w_o=m_w_o, sinks=m_sinks, rel_bias=m_rel_bias, ffn_w_up=m_ffn_w_up, ffn_w_down=m_ffn_w_down,
                 norm_final=m_norm_final)
    mom_v = dict(norm_mix=v_norm_mix, norm_ffn=v_norm_ffn, conv_w_pw1=v_conv_w_pw1, conv_b_pw1=v_conv_b_pw1,
                 conv_w_dw=v_conv_w_dw, conv_b_dw=v_conv_b_dw, conv_ln_g=v_conv_ln_g, conv_ln_b=v_conv_ln_b,
                 conv_w_pw2=v_conv_w_pw2, conv_b_pw2=v_conv_b_pw2, norm_kv=v_norm_kv, w_kv=v_w_kv, w_q=v_w_q,
                 w_o=v_w_o, sinks=v_sinks, rel_bias=v_rel_bias, ffn_w_up=v_ffn_w_up, ffn_w_down=v_ffn_w_down,
                 norm_final=v_norm_final)

    def halves(a):
        return a.astype(BF16).reshape(2, a.shape[0] // 2, a.shape[1])

    shards = {"kv": halves(w_kv)}
    for l in range(2):
        shards[f"pw1_{l}"], shards[f"pw2_{l}"] = halves(conv_w_pw1[l]), halves(conv_w_pw2[l])
        shards[f"wq_{l}"], shards[f"wo_{l}"] = halves(w_q[l]), halves(w_o[l])
    for l in range(4):
        shards[f"up_{l}"], shards[f"down_{l}"] = halves(ffn_w_up[l]), halves(ffn_w_down[l])
    shards["small"] = jnp.concatenate(
        [conv_w_dw, conv_b_dw[:, None], conv_ln_g[:, None], conv_ln_b[:, None], conv_b_pw2[:, None],
         conv_b_pw1.reshape(2, 2, 256), jnp.zeros((2, 3, 256), F32)], axis=1)
    ag = WeightGather(shards, AG_GROUPS)
    big = {"conv_w_pw1": "pw1", "conv_w_pw2": "pw2", "w_q": "wq", "w_o": "wo", "ffn_w_up": "up",
           "ffn_w_down": "down", "w_kv": "kv"}
    rs = GradReduce({"pw1": (2, 512, 512), "pw2": (2, 128, D), "wq": (2, 128, D), "wo": (2, 128, D),
                     "up": (4, 512, DFF // 2), "down": (4, DFF // 8, D), "kv": (1, 128, 512)})

    P = dict(norm_mix=norm_mix[:, None], norm_ffn=norm_ffn[:, None], norm_kv=norm_kv[None, None],
             norm_final=norm_final[None], sinks=sinks, rel_bias=rel_bias)
    loss_part, grad_x, S = run_step(x[0], loss_target[0], P, ag, rs)
    loss = lax.psum(loss_part, ("x", "y", "c"))

    rs.finish("l1", [grad_x])
    token = rs.reduce("f0", [grad_x])
    vsum = allreduce_small(_gate(_pack_small(S), token))
    col = lambda a: lax.dynamic_slice_in_dim(a, me * 256, 256, axis=-1)
    grads = {}
    for l in range(2):
        base = l * R_CONV
        grads.setdefault("conv_w_dw", []).append(col(vsum[base:base + 31]))
        grads.setdefault("conv_b_dw", []).append(col(vsum[base + 31]))
        grads.setdefault("conv_ln_g", []).append(col(vsum[base + 32]))
        grads.setdefault("conv_ln_b", []).append(col(vsum[base + 33]))
        grads.setdefault("conv_b_pw2", []).append(col(vsum[base + 34]))
        grads.setdefault("conv_b_pw1", []).append(
            lax.dynamic_slice_in_dim(vsum[base + 35:base + 37].reshape(2 * D), me * 512, 512, axis=0))
    grads = {k: jnp.stack(v) for k, v in grads.items()}
    base = 2 * R_CONV
    grads["norm_mix"] = vsum[base:base + 4]
    grads["norm_ffn"] = vsum[base + 4:base + 8]
    grads["norm_kv"] = vsum[base + 8]
    grads["norm_final"] = vsum[base + 9]
    grads["sinks"] = vsum[base + 10, 0:32].reshape(2, 16)
    grads["rel_bias"] = vsum[base + 10, 32:32 + 512].reshape(32, 16)

    delta, new_m, new_v = {}, {}, {}
    rest = [n for n in weights if n not in big]

    def pack(dct):
        flat = jnp.concatenate([dct[n].reshape(-1) for n in rest])
        return jnp.pad(flat, (0, (-flat.shape[0]) % (8 * 128))).reshape(-1, 128)

    d, nm, nv = adamw(pack(weights), pack(grads), pack(mom_m), pack(mom_v), "adamw_small")
    off = 0
    for n in rest:
        shp = weights[n].shape
        sz = int(np.prod(shp))
        delta[n] = d.reshape(-1)[off:off + sz].reshape(shp)
        new_m[n] = nm.reshape(-1)[off:off + sz].reshape(shp)
        new_v[n] = nv.reshape(-1)[off:off + sz].reshape(shp)
        off += sz

    def update(n):
        shp = weights[n].shape
        r2 = (int(np.prod(shp[:-1])), shp[-1])
        grads[n] = rs.J[big[n]].reshape(shp)
        d, nm, nv = adamw(weights[n].reshape(r2), grads[n].reshape(r2), mom_m[n].reshape(r2), mom_v[n].reshape(r2),
                          f"adamw_{n}")
        delta[n], new_m[n], new_v[n] = d.reshape(shp), nm.reshape(shp), nv.reshape(shp)

    rs.finish("f0", [vsum])
    for n in ("ffn_w_up", "ffn_w_down"):
        update(n)
    rs.reduce("c0", [delta["ffn_w_down"]])
    for n in ("w_q", "w_o", "w_kv"):
        update(n)
    rs.finish("c0", [delta["w_kv"]])
    for n in ("conv_w_pw1", "conv_w_pw2"):
        update(n)

    order = list(weights)
    return (loss, grad_x[None], *[grads[n] for n in order], *[delta[n] for n in order],
            *[new_m[n] for n in order], *[new_v[n] for n in order])
```

```python
import functools
import math

import numpy as np
import jax
import jax.numpy as jnp
from jax import lax
from jax.experimental import pallas as pl
from jax.experimental.pallas import tpu as pltpu

F32 = jnp.float32
BF16 = jnp.bfloat16
MESH = pl.DeviceIdType.MESH

D = 1024
DFF = 2816
N_HEADS = 16
N_KV = 4
GROUP = 4
HD = 64
BLK = 128
CONV_W = 31
HALO = 32
N_BUCKETS = 32
MAX_DISTANCE = 128
EPS = 1e-6
NEG_INF = -1e30
TM = 256
VMEM_LIMIT = 56 * 2 ** 20

ADAM_LR, ADAM_B1, ADAM_B2, ADAM_EPS, ADAM_WD, ADAM_STEP = 0.001, 0.9, 0.999, 1e-08, 0.01, 10


def _cp(*sem):
    return pltpu.CompilerParams(dimension_semantics=sem, vmem_limit_bytes=VMEM_LIMIT)


def _sigmoid(x):
    return 1.0 / (1.0 + jnp.exp(-x))


def _row(tm, n):
    return pl.BlockSpec((tm, n), lambda i: (i, 0))


def _const(shape):
    nd = len(shape)
    return pl.BlockSpec(shape, lambda i: (0,) * nd)


def _layer(shape, l):
    nd = len(shape)
    return pl.BlockSpec((None,) + tuple(shape), lambda i: (l,) + (0,) * nd)


def _dot(a, b):
    return jnp.dot(a, b, preferred_element_type=F32)


def _dot_nt(a, b):
    return lax.dot_general(a, b, (((1,), (1,)), ((), ())), preferred_element_type=F32)


def _dot_tn(a, b):
    return lax.dot_general(a, b, (((0,), (0,)), ((), ())), preferred_element_type=F32)


def _rms(x):
    return lax.rsqrt(jnp.mean(x * x, axis=-1, keepdims=True) + EPS)


def norm_mm_glu(h, g, l, w, b, name):
    T = h.shape[0]
    ns = w.shape[-1]

    def body(h_ref, g_ref, w_ref, b_ref, xn_ref, u_ref, a_ref):
        x = h_ref[...]
        xn = (x * _rms(x) * g_ref[...]).astype(BF16)
        xn_ref[...] = xn
        for s in range(2):
            lo, hi = s * ns, (s + 1) * ns
            u1 = _dot(xn, w_ref[s]) + b_ref[:, lo:hi]
            u2 = _dot(xn, w_ref[2 + s]) + b_ref[:, D + lo:D + hi]
            u_ref[:, lo:hi] = u1.astype(BF16)
            u_ref[:, D + lo:D + hi] = u2.astype(BF16)
            a_ref[:, lo:hi] = u1 * _sigmoid(u2)

    return pl.pallas_call(
        body, name=name, grid=(T // TM,),
        in_specs=[_row(TM, D), _layer((1, D), l), _const((4, D, ns)), _layer((1, 2 * D), l)],
        out_specs=[_row(TM, D), _row(TM, 2 * D), _row(TM, D)],
        out_shape=[jax.ShapeDtypeStruct((T, D), BF16), jax.ShapeDtypeStruct((T, 2 * D), BF16),
                   jax.ShapeDtypeStruct((T, D), F32)],
        compiler_params=_cp("parallel"),
    )(h, g, w, b)


def _conv_taps(buf_ref, w_ref, out_ref, first):
    RB, LB = 32, 512
    for r0 in range(0, TM, RB):
        for c0 in range(0, D, LB):
            acc = jnp.zeros((RB, LB), F32)
            for k in range(CONV_W):
                acc = acc + w_ref[k:k + 1, c0:c0 + LB] * buf_ref[pl.ds(first + k + r0, RB), c0:c0 + LB]
            out_ref[r0:r0 + RB, c0:c0 + LB] = acc


def dwconv_ln_silu(a, sm, l, name):
    T = a.shape[0]
    nb = TM // HALO

    def body(cur_ref, prev_ref, sm_ref, y_ref, s_ref, buf):
        i = pl.program_id(0)
        buf[0:HALO, :] = jnp.where(i > 0, prev_ref[...], 0.0)
        buf[HALO:HALO + TM, :] = cur_ref[...]
        _conv_taps(buf, sm_ref, y_ref, HALO - (CONV_W - 1))
        y = y_ref[...] + sm_ref[31:32, :]
        y_ref[...] = y
        mu = jnp.mean(y, axis=-1, keepdims=True)
        yc = y - mu
        rstd = lax.rsqrt(jnp.mean(yc * yc, axis=-1, keepdims=True) + EPS)
        z = yc * rstd * sm_ref[32:33, :] + sm_ref[33:34, :]
        s_ref[...] = (z * _sigmoid(z)).astype(BF16)

    return pl.pallas_call(
        body, name=name, grid=(T // TM,),
        in_specs=[_row(TM, D), pl.BlockSpec((HALO, D), lambda i: (jnp.maximum(i * nb - 1, 0), 0)),
                  _layer((40, D), l)],
        out_specs=[_row(TM, D), _row(TM, D)],
        out_shape=[jax.ShapeDtypeStruct((T, D), F32), jax.ShapeDtypeStruct((T, D), BF16)],
        scratch_shapes=[pltpu.VMEM((TM + HALO, D), F32)],
        compiler_params=_cp("parallel"),
    )(a, a, sm)


def mm_bias_res(xb, w, b, bl, res, name):
    T, K = xb.shape

    def body(x_ref, w_ref, b_ref, r_ref, o_ref):
        o_ref[...] = _dot(x_ref[...], w_ref[...]) + b_ref[...] + r_ref[...]

    return pl.pallas_call(
        body, name=name, grid=(T // TM,),
        in_specs=[_row(TM, K), _const((K, D)), _layer((1, D), bl), _row(TM, D)],
        out_specs=_row(TM, D), out_shape=jax.ShapeDtypeStruct((T, D), F32),
        compiler_params=_cp("parallel"),
    )(xb, w, b, res)


def norm_mm_swiglu(h, g, l, w, name):
    T = h.shape[0]
    ns = w.shape[-1]

    def body(h_ref, g_ref, w_ref, xn_ref, gu_ref, f_ref):
        x = h_ref[...]
        xn = (x * _rms(x) * g_ref[...]).astype(BF16)
        xn_ref[...] = xn
        for s in range(2):
            lo, hi = s * ns, (s + 1) * ns
            gate = _dot(xn, w_ref[s])
            up = _dot(xn, w_ref[2 + s])
            gu_ref[:, lo:hi] = gate.astype(BF16)
            gu_ref[:, DFF + lo:DFF + hi] = up.astype(BF16)
            f_ref[:, lo:hi] = (gate * _sigmoid(gate) * up).astype(BF16)

    return pl.pallas_call(
        body, name=name, grid=(T // TM,),
        in_specs=[_row(TM, D), _layer((1, D), l), _const((4, D, ns))],
        out_specs=[_row(TM, D), _row(TM, 2 * DFF), _row(TM, DFF)],
        out_shape=[jax.ShapeDtypeStruct((T, D), BF16), jax.ShapeDtypeStruct((T, 2 * DFF), BF16),
                   jax.ShapeDtypeStruct((T, DFF), BF16)],
        compiler_params=_cp("parallel"),
    )(h, g, w)


def norm_mm(h, g, gl, w, name, scale=1.0):
    T = h.shape[0]
    N = w.shape[-1]

    def body(h_ref, g_ref, w_ref, xn_ref, o_ref):
        x = h_ref[...]
        xn = (x * _rms(x) * g_ref[...]).astype(BF16)
        xn_ref[...] = xn
        o_ref[...] = (_dot(xn, w_ref[...]) * scale).astype(BF16)

    return pl.pallas_call(
        body, name=name, grid=(T // TM,),
        in_specs=[_row(TM, D), _layer((1, D), gl), _const((D, N))],
        out_specs=[_row(TM, D), _row(TM, N)],
        out_shape=[jax.ShapeDtypeStruct((T, D), BF16), jax.ShapeDtypeStruct((T, N), BF16)],
        compiler_params=_cp("parallel"),
    )(h, g, w)


QB = 4


def band_mask():
    qi = np.arange(GROUP * BLK)[:, None] % BLK
    kj = np.arange(2 * BLK)[None, :]
    band = ((kj < BLK) & (kj > qi)) | ((kj >= BLK) & (kj - BLK <= qi))
    first = band & (kj >= BLK)
    return np.where(np.stack([first, band]), 0.0, NEG_INF).astype(np.float32)


def _attn_softmax(q4, kb, bias, sink):
    s = _dot_nt(q4, kb) + bias
    m = jnp.maximum(jnp.max(s, axis=-1, keepdims=True), sink)
    p = jnp.exp(s - m)
    es = jnp.exp(sink - m)
    inv = 1.0 / (jnp.sum(p, axis=-1, keepdims=True) + es)
    return p, inv, es


def _attn_specs(T):
    qspec = pl.BlockSpec((None, GROUP, QB * BLK, HD), lambda kv, n: (kv, 0, n, 0))
    kspec = pl.BlockSpec((None, T + BLK, HD), lambda kv, n: (kv, 0, 0))
    bspec = pl.BlockSpec((2, None, GROUP * BLK, 2 * BLK), lambda kv, n: (0, kv, 0, 0))
    sspec = pl.BlockSpec((None, GROUP * BLK, 1), lambda kv, n: (kv, 0, 0))
    return qspec, kspec, bspec, sspec


def _attn_block(n, b, q_ref, b_ref):
    blk = n * QB + b
    rows = pl.ds(pl.multiple_of(blk * BLK, BLK), 2 * BLK)
    q4 = q_ref[:, b * BLK:(b + 1) * BLK, :].reshape(GROUP * BLK, HD)
    bias = b_ref[jnp.minimum(blk, 1)] if b == 0 else b_ref[1]
    return rows, q4, bias


def attn_fwd(q, kp, vp, bias, sink, name):
    T = q.shape[2]
    qspec, kspec, bspec, sspec = _attn_specs(T)

    def body(q_ref, k_ref, v_ref, b_ref, s_ref, o_ref):
        n = pl.program_id(1)
        for b in range(QB):
            rows, q4, bias = _attn_block(n, b, q_ref, b_ref)
            p, inv, _ = _attn_softmax(q4, k_ref[rows, :], bias, s_ref[...])
            o = _dot(p.astype(BF16), v_ref[rows, :]) * inv
            o_ref[:, b * BLK:(b + 1) * BLK, :] = o.reshape(GROUP, BLK, HD).astype(BF16)

    return pl.pallas_call(
        body, name=name, grid=(N_KV, T // (QB * BLK)),
        in_specs=[qspec, kspec, kspec, bspec, sspec], out_specs=qspec,
        out_shape=jax.ShapeDtypeStruct((N_KV, GROUP, T, HD), BF16),
        compiler_params=_cp("parallel", "parallel"),
    )(q, kp, vp, bias, sink)


def attn_bwd(q, kp, vp, bias, sink, o, do, name):
    T = q.shape[2]
    qspec, kspec, bspec, sspec = _attn_specs(T)

    def body(q_ref, k_ref, v_ref, b_ref, s_ref, o_ref, do_ref, dq_ref, dk_ref, dv_ref, db_ref, ds_ref):
        n = pl.program_id(1)

        @pl.when(n == 0)
        def _():
            dk_ref[...] = jnp.zeros_like(dk_ref)
            dv_ref[...] = jnp.zeros_like(dv_ref)
            db_ref[...] = jnp.zeros_like(db_ref)
            ds_ref[...] = jnp.zeros_like(ds_ref)

        for b in range(QB):
            rows, q4, bias = _attn_block(n, b, q_ref, b_ref)
            do4 = do_ref[:, b * BLK:(b + 1) * BLK, :].reshape(GROUP * BLK, HD)
            o4 = o_ref[:, b * BLK:(b + 1) * BLK, :].reshape(GROUP * BLK, HD)
            kb = k_ref[rows, :]
            vb = v_ref[rows, :]
            p, inv, es = _attn_softmax(q4, kb, bias, s_ref[...])
            probs = p * inv
            dp = _dot_nt(do4, vb)
            delta = jnp.sum(do4.astype(F32) * o4.astype(F32), axis=-1, keepdims=True)
            dS = probs * (dp - delta)
            ds_ref[...] += -(es * inv) * delta
            db_ref[...] += dS
            dSb = dS.astype(BF16)
            dq_ref[:, b * BLK:(b + 1) * BLK, :] = (_dot(dSb, kb) * (HD ** -0.5)).reshape(GROUP, BLK, HD).astype(BF16)
            dk_ref[rows, :] += _dot_tn(dSb, q4)
            dv_ref[rows, :] += _dot_tn(probs.astype(BF16), do4)

    kout = pl.BlockSpec((None, T + BLK, HD), lambda kv, n: (kv, 0, 0))
    dbspec = pl.BlockSpec((None, GROUP * BLK, 2 * BLK), lambda kv, n: (kv, 0, 0))
    return pl.pallas_call(
        body, name=name, grid=(N_KV, T // (QB * BLK)),
        in_specs=[qspec, kspec, kspec, bspec, sspec, qspec, qspec],
        out_specs=[qspec, kout, kout, dbspec, sspec],
        out_shape=[jax.ShapeDtypeStruct((N_KV, GROUP, T, HD), BF16),
                   jax.ShapeDtypeStruct((N_KV, T + BLK, HD), F32), jax.ShapeDtypeStruct((N_KV, T + BLK, HD), F32),
                   jax.ShapeDtypeStruct((N_KV, GROUP * BLK, 2 * BLK), F32),
                   jax.ShapeDtypeStruct((N_KV, GROUP * BLK, 1), F32)],
        compiler_params=_cp("parallel", "arbitrary"),
    )(q, kp, vp, bias, sink, o, do)


def final_loss(h, g, target, name):
    T = h.shape[0]

    def body(h_ref, g_ref, t_ref, dh_ref, st_ref):
        i = pl.program_id(0)

        @pl.when(i == 0)
        def _():
            st_ref[...] = jnp.zeros_like(st_ref)

        x = h_ref[...]
        r = _rms(x)
        xh = x * r
        e = xh * g_ref[...] - t_ref[...]
        loss = 0.5 * jnp.sum(jnp.mean(e * e, axis=-1, keepdims=True))
        dy = e * (1.0 / D)
        st_ref[0:1, :] += jnp.sum(dy * xh, axis=0, keepdims=True)
        lane = lax.broadcasted_iota(jnp.int32, (1, D), 1)
        st_ref[1:2, :] += jnp.where(lane == 0, loss, 0.0)
        dxh = dy * g_ref[...]
        dh_ref[...] = r * (dxh - xh * jnp.mean(dxh * xh, axis=-1, keepdims=True))

    return pl.pallas_call(
        body, name=name, grid=(T // TM,),
        in_specs=[_row(TM, D), _const((1, D)), _row(TM, D)],
        out_specs=[_row(TM, D), _const((8, D))],
        out_shape=[jax.ShapeDtypeStruct((T, D), F32), jax.ShapeDtypeStruct((8, D), F32)],
        compiler_params=_cp("arbitrary"),
    )(h, g, target)


def mm_dw(x, dy, name, tn, slots, colsum=False):
    T, K = x.shape
    N = dy.shape[1]
    tt = min(T, 1024)
    nt = T // tt
    ns = N // slots
    per = ns // tn

    def body(x_ref, dy_ref, *rest):
        if colsum:
            dw_ref, cs_ref, acc, cacc = rest
        else:
            dw_ref, acc = rest
        t = pl.program_id(1)

        @pl.when(t == 0)
        def _():
            acc[...] = jnp.zeros_like(acc)
            if colsum:
                cacc[...] = jnp.zeros_like(cacc)

        dyv = dy_ref[...]
        acc[...] += _dot_tn(x_ref[...].astype(BF16), dyv.astype(BF16))
        if colsum:
            cacc[...] += jnp.sum(dyv.astype(F32), axis=0, keepdims=True)

        @pl.when(t == nt - 1)
        def _():
            dw_ref[...] = acc[...].astype(BF16)
            if colsum:
                cs_ref[...] = cacc[...]

    out_specs = [pl.BlockSpec((None, K, tn), lambda j, t: (j // per, 0, j % per))]
    out_shape = [jax.ShapeDtypeStruct((slots, K, ns), BF16)]
    scratch = [pltpu.VMEM((K, tn), F32)]
    if colsum:
        out_specs.append(pl.BlockSpec((1, tn), lambda j, t: (0, j)))
        out_shape.append(jax.ShapeDtypeStruct((1, N), F32))
        scratch.append(pltpu.VMEM((1, tn), F32))
    res = pl.pallas_call(
        body, name=name, grid=(N // tn, nt),
        in_specs=[pl.BlockSpec((tt, K), lambda j, t: (t, 0)), pl.BlockSpec((tt, tn), lambda j, t: (t, j))],
        out_specs=out_specs, out_shape=out_shape, scratch_shapes=scratch,
        compiler_params=_cp("parallel", "arbitrary"),
    )(x, dy)
    return tuple(res) if colsum else res[0]


def mmT_swiglu_bwd(dh, w, gu, name):
    T = dh.shape[0]
    half = DFF // 2

    def body(dh_ref, w_ref, gu_ref, du_ref):
        dhb = dh_ref[...].astype(BF16)
        for s in range(2):
            lo, hi = s * half, (s + 1) * half
            df = _dot_nt(dhb, w_ref[lo:hi, :])
            gate = gu_ref[:, lo:hi].astype(F32)
            up = gu_ref[:, DFF + lo:DFF + hi].astype(F32)
            sg = _sigmoid(gate)
            du_ref[:, lo:hi] = (df * up * sg * (1.0 + gate * (1.0 - sg))).astype(BF16)
            du_ref[:, DFF + lo:DFF + hi] = (df * gate * sg).astype(BF16)

    return pl.pallas_call(
        body, name=name, grid=(T // TM,),
        in_specs=[_row(TM, D), _const((DFF, D)), _row(TM, 2 * DFF)],
        out_specs=_row(TM, 2 * DFF), out_shape=jax.ShapeDtypeStruct((T, 2 * DFF), BF16),
        compiler_params=_cp("parallel"),
    )(dh, w, gu)


def mmT_rmsbwd(du, w, h, g, gl, dh_in, name):
    T, N = du.shape
    slots = w.shape[0]
    ns = N // slots

    def body(du_ref, w_ref, h_ref, g_ref, di_ref, dh_ref, dg_ref):
        i = pl.program_id(0)

        @pl.when(i == 0)
        def _():
            dg_ref[...] = jnp.zeros_like(dg_ref)

        dxn = _dot_nt(du_ref[:, 0:ns], w_ref[0])
        for s in range(1, slots):
            dxn = dxn + _dot_nt(du_ref[:, s * ns:(s + 1) * ns], w_ref[s])
        x = h_ref[...]
        r = _rms(x)
        xh = x * r
        dg_ref[0:1, :] += jnp.sum(dxn * xh, axis=0, keepdims=True)
        dxh = dxn * g_ref[...]
        dh_ref[...] = di_ref[...] + r * (dxh - xh * jnp.mean(dxh * xh, axis=-1, keepdims=True))

    return pl.pallas_call(
        body, name=name, grid=(T // TM,),
        in_specs=[_row(TM, N), _const((slots, D, ns)), _row(TM, D), _layer((1, D), gl), _row(TM, D)],
        out_specs=[_row(TM, D), _const((8, D))],
        out_shape=[jax.ShapeDtypeStruct((T, D), F32), jax.ShapeDtypeStruct((8, D), F32)],
        compiler_params=_cp("arbitrary"),
    )(du, w, h, g, dh_in)


def mmT(dh, w, name):
    T = dh.shape[0]
    N = w.shape[0]

    def body(dh_ref, w_ref, o_ref):
        o_ref[...] = _dot_nt(dh_ref[...].astype(BF16), w_ref[...]).astype(BF16)

    return pl.pallas_call(
        body, name=name, grid=(T // TM,),
        in_specs=[_row(TM, D), _const((N, D))],
        out_specs=_row(TM, N), out_shape=jax.ShapeDtypeStruct((T, N), BF16),
        compiler_params=_cp("parallel"),
    )(dh, w)


def mmT_lnbwd(dh, w, y, sm, l, name):
    T = dh.shape[0]

    def body(dh_ref, w_ref, y_ref, sm_ref, dy_ref, st_ref):
        i = pl.program_id(0)

        @pl.when(i == 0)
        def _():
            st_ref[...] = jnp.zeros_like(st_ref)

        ds = _dot_nt(dh_ref[...].astype(BF16), w_ref[...])
        y = y_ref[...]
        mu = jnp.mean(y, axis=-1, keepdims=True)
        yc = y - mu
        rstd = lax.rsqrt(jnp.mean(yc * yc, axis=-1, keepdims=True) + EPS)
        xh = yc * rstd
        gam = sm_ref[32:33, :]
        z = xh * gam + sm_ref[33:34, :]
        sg = _sigmoid(z)
        dz = ds * sg * (1.0 + z * (1.0 - sg))
        st_ref[0:1, :] += jnp.sum(dz * xh, axis=0, keepdims=True)
        st_ref[1:2, :] += jnp.sum(dz, axis=0, keepdims=True)
        dxh = dz * gam
        dy = rstd * (dxh - jnp.mean(dxh, axis=-1, keepdims=True) - xh * jnp.mean(dxh * xh, axis=-1, keepdims=True))
        st_ref[2:3, :] += jnp.sum(dy, axis=0, keepdims=True)
        dy_ref[...] = dy

    return pl.pallas_call(
        body, name=name, grid=(T // TM,),
        in_specs=[_row(TM, D), _const((D, D)), _row(TM, D), _layer((40, D), l)],
        out_specs=[_row(TM, D), _const((8, D))],
        out_shape=[jax.ShapeDtypeStruct((T, D), F32), jax.ShapeDtypeStruct((8, D), F32)],
        compiler_params=_cp("arbitrary"),
    )(dh, w, y, sm)


def dwconv_glu_bwd(dy, a, u, sm, smrev, l, name):
    T = dy.shape[0]
    nb = TM // HALO
    last = T // HALO - 1

    def body(dy_ref, dyn_ref, a_ref, ap_ref, u_ref, sm_ref, rev_ref, du_ref, dw_ref, bufd, bufa, da):
        i = pl.program_id(0)

        @pl.when(i == 0)
        def _():
            dw_ref[...] = jnp.zeros_like(dw_ref)

        bufd[0:TM, :] = dy_ref[...]
        bufd[TM:TM + HALO, :] = jnp.where(i < pl.num_programs(0) - 1, dyn_ref[...], 0.0)
        bufa[0:HALO, :] = jnp.where(i > 0, ap_ref[...], 0.0)
        bufa[HALO:HALO + TM, :] = a_ref[...]
        _conv_taps(bufd, rev_ref, da, 0)
        LB = 512
        for c0 in range(0, D, LB):
            for k in range(CONV_W):
                acc = jnp.zeros((8, LB), F32)
                for r0 in range(0, TM, 8):
                    acc = acc + dy_ref[r0:r0 + 8, c0:c0 + LB] * bufa[pl.ds(HALO - (CONV_W - 1) + k + r0, 8), c0:c0 + LB]
                dw_ref[k:k + 1, c0:c0 + LB] += jnp.sum(acc, axis=0, keepdims=True)
        dav = da[...]
        u1 = u_ref[:, 0:D].astype(F32)
        sg = _sigmoid(u_ref[:, D:2 * D].astype(F32))
        du_ref[:, 0:D] = (dav * sg).astype(BF16)
        du_ref[:, D:2 * D] = (dav * u1 * sg * (1.0 - sg)).astype(BF16)

    return pl.pallas_call(
        body, name=name, grid=(T // TM,),
        in_specs=[_row(TM, D), pl.BlockSpec((HALO, D), lambda i: (jnp.minimum((i + 1) * nb, last), 0)),
                  _row(TM, D), pl.BlockSpec((HALO, D), lambda i: (jnp.maximum(i * nb - 1, 0), 0)),
                  _row(TM, 2 * D), _layer((40, D), l), _layer((40, D), l)],
        out_specs=[_row(TM, 2 * D), _const((32, D))],
        out_shape=[jax.ShapeDtypeStruct((T, 2 * D), BF16), jax.ShapeDtypeStruct((32, D), F32)],
        scratch_shapes=[pltpu.VMEM((TM + HALO, D), F32), pltpu.VMEM((TM + HALO, D), F32), pltpu.VMEM((TM, D), F32)],
        compiler_params=_cp("arbitrary"),
    )(dy, dy, a, a, u, sm, smrev)


def _rows_tile(R):
    for t in (512, 256, 128, 64, 32, 16, 8):
        if R % t == 0:
            return t
    return R


def add8(own, others, name):
    R, C = own.shape
    tr = _rows_tile(R)

    def body(o_ref, x_ref, out_ref):
        acc = o_ref[...].astype(F32)
        for k in range(7):
            acc = acc + x_ref[k].astype(F32)
        out_ref[...] = acc

    return pl.pallas_call(
        body, name=name, grid=(R // tr,),
        in_specs=[_row(tr, C), pl.BlockSpec((7, tr, C), lambda i: (0, i, 0))], out_specs=_row(tr, C),
        out_shape=jax.ShapeDtypeStruct((R, C), F32), compiler_params=_cp("parallel"),
    )(own, others)


def adamw(w, g, m, v, name):
    R, C = w.shape
    tr = _rows_tile(R)

    def body(w_ref, g_ref, m_ref, v_ref, d_ref, nm_ref, nv_ref):
        gv = g_ref[...]
        nm = ADAM_B1 * m_ref[...] + (1.0 - ADAM_B1) * gv
        nv = ADAM_B2 * v_ref[...] + (1.0 - ADAM_B2) * (gv * gv)
        m_hat = nm / (1.0 - ADAM_B1 ** ADAM_STEP)
        v_hat = nv / (1.0 - ADAM_B2 ** ADAM_STEP)
        d_ref[...] = -ADAM_LR * (m_hat / (jnp.sqrt(v_hat) + ADAM_EPS) + ADAM_WD * w_ref[...])
        nm_ref[...] = nm
        nv_ref[...] = nv

    sd = jax.ShapeDtypeStruct((R, C), F32)
    return pl.pallas_call(
        body, name=name, grid=(R // tr,),
        in_specs=[_row(tr, C)] * 4, out_specs=[_row(tr, C)] * 3, out_shape=[sd, sd, sd],
        compiler_params=_cp("parallel"),
    )(w, g, m, v)


ANY = pl.BlockSpec(memory_space=pl.ANY)
HBM = pl.BlockSpec(memory_space=pltpu.HBM)
SEM = pl.BlockSpec(memory_space=pltpu.SEMAPHORE)
EFFECT = pltpu.SideEffectType.DATAFLOW_SIDE_EFFECTING


def _place():
    x, y, c = lax.axis_index("x"), lax.axis_index("y"), lax.axis_index("c")
    chips = [(1 - x, y), (x, 1 - y), (1 - x, 1 - y)]
    return x, y, c, chips


def _copy(src, dst, send, recv, k, to):
    return pltpu.make_async_remote_copy(src_ref=src, dst_ref=dst, send_sem=send.at[k], recv_sem=recv.at[k],
                                        device_id=to, device_id_type=MESH)


def xchg_start(name, bufs, plan, n, after=()):
    nb = len(bufs)

    na = len(after)

    def body(*refs):
        send, recv, token = refs[nb + na], refs[nb + na + 1], refs[-1]
        for k, (src, dst, to) in enumerate(plan(refs[:nb])):
            _copy(src, dst, send, recv, k, to).start()
        token[...] = jnp.zeros_like(token)

    outs = pl.pallas_call(
        body, name=name,
        out_shape=(pltpu.SemaphoreType.DMA((n,)), pltpu.SemaphoreType.DMA((n,)),
                   *[pltpu.HBM(b.shape, b.dtype) for b in bufs], jax.ShapeDtypeStruct((8, 128), F32)),
        in_specs=[HBM] * nb + [ANY] * na,
        out_specs=(SEM, SEM, *[HBM] * nb, pl.BlockSpec(memory_space=pltpu.VMEM)),
        input_output_aliases={i: 2 + i for i in range(nb)},
        compiler_params=pltpu.CompilerParams(has_side_effects=EFFECT),
    )(*[pltpu.with_memory_space_constraint(b, pltpu.HBM) for b in bufs], *after)
    return dict(name=name, send=outs[0], recv=outs[1], bufs=list(outs[2:2 + nb]), plan=plan), outs[-1]


def xchg_wait(flight, after):
    bufs, plan = flight["bufs"], flight["plan"]
    nb = len(bufs)

    def body(*refs):
        send, recv = refs[nb], refs[nb + 1]
        for k, (src, dst, to) in enumerate(plan(refs[:nb])):
            cp = _copy(src, dst, send, recv, k, to)
            cp.wait_send()
            cp.wait_recv()

    outs = pl.pallas_call(
        body, name=flight["name"] + "_wait",
        out_shape=tuple(pltpu.HBM(b.shape, b.dtype) for b in bufs),
        in_specs=[HBM] * nb + [SEM, SEM] + [ANY] * len(after),
        out_specs=tuple([HBM] * nb), input_output_aliases={i: i for i in range(nb)},
        compiler_params=pltpu.CompilerParams(has_side_effects=EFFECT),
    )(*bufs, flight["send"], flight["recv"], *after)
    return list(outs)


def _flip(k, x, y, c):
    return ((1 - x) if k & 4 else x, (1 - y) if k & 2 else y, (1 - c) if k & 1 else c)


class WeightGather:
    def __init__(self, shards, groups):
        me = 2 * lax.axis_index("x") + lax.axis_index("y")
        self.names = dict(groups)
        self.ici, self.d2d = {}, {}
        self.token = None
        for gname, names in groups:
            nt = len(names)
            srcs = [shards[n] for n in names]
            lands = [lax.dynamic_update_slice(lax.empty((4,) + s.shape, s.dtype), s[None], (me, 0, 0, 0))
                     for s in srcs]

            def plan(refs, nt=nt):
                x, y, c, chips = _place()
                return [(refs[t].at[c], refs[nt + t].at[2 * x + y, c], (cx, cy, c))
                        for t in range(nt) for cx, cy in chips]

            self.ici[gname], self.token = xchg_start(f"ag_ici_{gname}", srcs + lands, plan, 3 * nt,
                                                     after=[] if self.token is None else [self.token])

    def forward(self, gname, after):
        nt = len(self.names[gname])
        lands = xchg_wait(self.ici.pop(gname), after)[nt:]

        def plan(refs):
            x, y, c, chips = _place()
            out = []
            for t in range(nt):
                for cx, cy in chips:
                    piece = refs[t].at[2 * cx + cy, c]
                    out.append((piece, piece, (x, y, 1 - c)))
            return out

        self.d2d[gname], token = xchg_start(f"ag_d2d_{gname}", lands, plan, 3 * nt)
        return token

    def get(self, gname, after):
        lands = xchg_wait(self.d2d.pop(gname), after)
        return dict(zip(self.names[gname], lands))


class GradReduce:
    def __init__(self, kinds):
        self.J = {k: lax.empty((L, 2, a2, b), F32) for k, (L, a2, b) in kinds.items()}
        self.x, self.j = {}, {}

    @staticmethod
    def _where(name):
        kind, _, l = name.partition("_")
        return kind, int(l or 0)

    def send(self, gname, grads):
        names = list(grads)
        nt = len(names)
        gs = [grads[n] for n in names]
        xs = [lax.empty((7,) + g.shape[2:], g.dtype) for g in gs]

        def plan(refs):
            x, y, c, _ = _place()
            out = []
            for t in range(nt):
                for k in range(1, 8):
                    px, py, pc = _flip(k, x, y, c)
                    out.append((refs[t].at[2 * px + py, pc], refs[nt + t].at[k - 1], (px, py, pc)))
            return out

        flight, token = xchg_start(f"rs_x_{gname}", gs + xs, plan, 7 * nt)
        self.x[gname] = (names, flight)
        return token

    def reduce(self, gname, after):
        names, flight = self.x.pop(gname)
        nt = len(names)
        bufs = xchg_wait(flight, after)
        me, c = 2 * lax.axis_index("x") + lax.axis_index("y"), lax.axis_index("c")
        hs = []
        for t, n in enumerate(names):
            g = bufs[t]
            own = lax.dynamic_slice(g, (me, c, 0, 0), (1, 1) + g.shape[2:])[0, 0]
            hs.append(add8(own, bufs[nt + t], f"rs_add_{n}"))
        where = [self._where(n) for n in names]

        def plan(refs):
            x, y, c, _ = _place()
            return [(refs[t], refs[nt + t].at[where[t][1], c], (x, y, 1 - c)) for t in range(nt)]

        flight, token = xchg_start(f"rs_join_{gname}", hs + [self.J[k] for k, _ in where], plan, nt)
        self.j[gname] = (where, flight)
        return token

    def finish(self, gname, after):
        where, flight = self.j.pop(gname)
        nt = len(where)
        bufs = xchg_wait(flight, after)
        c = lax.axis_index("c")
        for t, (kind, l) in enumerate(where):
            self.J[kind] = lax.dynamic_update_slice(bufs[nt + t], bufs[t][None, None], (l, c, 0, 0))


def allreduce_small(v):
    R = v.shape[0]

    def body(v_ref, o_ref, all_ref, send, recv):
        x, y, c, _ = _place()
        me = 4 * x + 2 * y + c
        all_ref[me] = v_ref[...]
        cps = []
        for k in range(1, 8):
            cp = _copy(v_ref, all_ref.at[me], send, recv, k - 1, _flip(k, x, y, c))
            cp.start()
            cps.append(cp)
        for k in range(1, 8):
            px, py, pc = _flip(k, x, y, c)
            _copy(v_ref, all_ref.at[4 * px + 2 * py + pc], send, recv, k - 1, (px, py, pc)).wait_recv()
        for cp in cps:
            cp.wait_send()
        acc = all_ref[0]
        for d in range(1, 8):
            acc = acc + all_ref[d]
        o_ref[...] = acc

    return pl.pallas_call(
        body, name="allreduce_small",
        in_specs=[pl.BlockSpec(memory_space=pltpu.VMEM)], out_specs=pl.BlockSpec(memory_space=pltpu.VMEM),
        out_shape=jax.ShapeDtypeStruct((R, D), F32),
        scratch_shapes=[pltpu.VMEM((8, R, D), F32), pltpu.SemaphoreType.DMA((7,)), pltpu.SemaphoreType.DMA((7,))],
        compiler_params=pltpu.CompilerParams(has_side_effects=True, vmem_limit_bytes=VMEM_LIMIT),
    )(v)


AG_GROUPS = (("a0", ("pw1_0", "pw2_0", "small")), ("f0", ("up_0", "down_0")),
             ("l1", ("pw1_1", "pw2_1", "up_1", "down_1")), ("l2", ("kv", "wq_0", "wo_0", "up_2", "down_2")),
             ("l3", ("wq_1", "wo_1", "up_3", "down_3")))


def _bucket_table():
    qi = np.arange(BLK)[:, None]
    kj = np.arange(2 * BLK)[None, :]
    d = np.maximum(qi + BLK - kj, 0)
    max_exact = N_BUCKETS // 2
    log_ratio = (np.log(np.maximum(d, 1).astype(np.float32) / np.float32(max_exact))
                 / np.float32(math.log(MAX_DISTANCE / max_exact))).astype(np.float32)
    large = max_exact + (log_ratio * np.float32(N_BUCKETS - max_exact)).astype(np.int32)
    large = np.minimum(large, N_BUCKETS - 1)
    return np.where(d < max_exact, d, large).astype(np.int32)


def _heads_major(a, nh):
    T = a.shape[0]
    return a.reshape(T, nh, HD).transpose(1, 0, 2)


def _heads_minor(a):
    nh, T, _ = a.shape
    return a.transpose(1, 0, 2).reshape(T, nh * HD)


def _slots(land):
    return land.reshape(4, 2 * land.shape[2], land.shape[3])


def _rows(land):
    return land.reshape(8 * land.shape[2], land.shape[3])


def _gview(g):
    s, K, n = g.shape
    return g.reshape(4, 2, K // 2, n) if s == 4 else g.reshape(4, 2, K // 8, n)


def _gate(a, token):
    return a + token[0, 0]


def _conv_small(f_small):
    fs = f_small.transpose(1, 2, 0, 3).reshape(2, 40, D)
    b_pw1 = f_small[:, :, 35:37, :].transpose(1, 0, 2, 3).reshape(2, 1, 2 * D)
    rev = jnp.concatenate([fs[:, CONV_W - 1::-1], jnp.zeros((2, 40 - CONV_W, D), F32)], axis=1)
    return dict(conv=fs, conv_rev=rev, b_pw1=b_pw1, b_pw2=fs[:, 34:35])


def run_step(x, target, P, ag, rs):
    T = x.shape[0]
    zero = jnp.zeros((1, 1, D), F32)
    nm, nf = P["norm_mix"], P["norm_ffn"]
    ag.forward("a0", [ag.token])
    W = ag.get("a0", [])
    sm = _conv_small(W["small"])
    h = x
    saved = []
    for l in range(2):
        xn, u, a = norm_mm_glu(h, nm, l, _slots(W[f"pw1_{l}"]), sm["b_pw1"], f"f_pw1_{l}")
        y, s = dwconv_ln_silu(a, sm["conv"], l, f"f_conv_{l}")
        b2 = sm["b_pw2"]
        if l == 0:
            b2 = _gate(b2, ag.forward("f0", [s]))
        h1 = mm_bias_res(s, _rows(W[f"pw2_{l}"]), b2, l, h, f"f_pw2_{l}")
        if l == 0:
            W.update(ag.get("f0", [h1]))
        xn2, gu, f = norm_mm_swiglu(h1, nf, l, _slots(W[f"up_{l}"]), f"f_up_{l}")
        nxt = "l1" if l == 0 else "l2"
        h2 = mm_bias_res(f, _rows(W[f"down_{l}"]), _gate(zero, ag.forward(nxt, [f])), 0, h1, f"f_down_{l}")
        W.update(ag.get(nxt, [h2]))
        saved.append(dict(h=h, xn=xn, u=u, a=a, y=y, s=s, h1=h1, xn2=xn2, gu=gu, f=f))
        h = h2
    h_kv = h
    kvn, kv = norm_mm(h, P["norm_kv"], 0, _rows(W["kv"]), "f_kv")
    kp = jnp.pad(_heads_major(kv[:, :N_KV * HD], N_KV), ((0, 0), (BLK, 0), (0, 0)))
    vp = jnp.pad(_heads_major(kv[:, N_KV * HD:], N_KV), ((0, 0), (BLK, 0), (0, 0)))
    bucket = _bucket_table()
    onehot = jnp.asarray(np.eye(N_BUCKETS, dtype=np.float32)[bucket])
    bias = jnp.einsum("qkb,bh->hqk", onehot, P["rel_bias"], precision=lax.Precision.HIGHEST)
    bias = bias.reshape(1, N_KV, GROUP * BLK, 2 * BLK) + jnp.asarray(band_mask())[:, None]
    for j in range(2):
        l = 2 + j
        xn, q = norm_mm(h, nm, l, _rows(W[f"wq_{j}"]), f"f_q_{j}", scale=HD ** -0.5)
        qh = _heads_major(q, N_HEADS).reshape(N_KV, GROUP, T, HD)
        sink = jnp.broadcast_to(P["sinks"][j].reshape(N_KV, GROUP, 1, 1), (N_KV, GROUP, BLK, 1))
        sink = sink.reshape(N_KV, GROUP * BLK, 1)
        oh = attn_fwd(qh, kp, vp, bias, sink, f"f_attn_{j}")
        attn = _heads_minor(oh.reshape(N_HEADS, T, HD))
        h1 = mm_bias_res(attn, _rows(W[f"wo_{j}"]), zero, 0, h, f"f_wo_{j}")
        xn2, gu, f = norm_mm_swiglu(h1, nf, l, _slots(W[f"up_{l}"]), f"f_up_{l}")
        zg = _gate(zero, ag.forward("l3", [f])) if j == 0 else zero
        h2 = mm_bias_res(f, _rows(W[f"down_{l}"]), zg, 0, h1, f"f_down_{l}")
        if j == 0:
            W.update(ag.get("l3", [h2]))
        saved.append(dict(h=h, xn=xn, qh=qh, oh=oh, sink=sink, attn=attn, h1=h1, xn2=xn2, gu=gu, f=f))
        h = h2

    dh, st_final = final_loss(h, P["norm_final"], target, "loss_head")

    S = dict(norm_ffn=[None] * 4, norm_mix=[None] * 4, conv=[None] * 2, taps=[None] * 2, b_pw1=[None] * 2,
             b_pw2=[None] * 2, sinks=[None] * 2)

    def ffn_bwd(dh, sv, l, nf):
        du = mmT_swiglu_bwd(dh, _rows(W[f"down_{l}"]), sv["gu"], f"b_down_{l}")
        gd = mm_dw(sv["f"], dh, f"w_down_{l}", 512, 1)
        gu = mm_dw(sv["xn2"], du, f"w_up_{l}", DFF // 2, 4)
        dh, dg = mmT_rmsbwd(du, _slots(W[f"up_{l}"]), sv["h1"], nf, l, dh, f"b_up_{l}")
        S["norm_ffn"][l] = dg[0]
        return dh, {f"down_{l}": _gview(gd), f"up_{l}": _gview(gu)}

    dk = dv = dbias = None
    for j in (1, 0):
        l = 2 + j
        sv = saved[l]
        dh, grads = ffn_bwd(dh, sv, l, nf)
        dattn = mmT(dh, _rows(W[f"wo_{j}"]), f"b_wo_{j}")
        grads[f"wo_{j}"] = _gview(mm_dw(sv["attn"], dh, f"w_wo_{j}", 512, 1))
        doh = _heads_major(dattn, N_HEADS).reshape(N_KV, GROUP, T, HD)
        dqh, dkj, dvj, dbj, dsj = attn_bwd(sv["qh"], kp, vp, bias, sv["sink"], sv["oh"], doh, f"b_attn_{j}")
        dq = _heads_minor(dqh.reshape(N_HEADS, T, HD))
        grads[f"wq_{j}"] = _gview(mm_dw(sv["xn"], dq, f"w_q_{j}", 512, 1))
        dh, dg = mmT_rmsbwd(dq, _rows(W[f"wq_{j}"])[None], sv["h"], nm, l, dh, f"b_q_{j}")
        S["norm_mix"][l] = dg[0]
        S["sinks"][j] = jnp.sum(dsj.reshape(N_HEADS, BLK), axis=1)
        dk = dkj if dk is None else dk + dkj
        dv = dvj if dv is None else dv + dvj
        dbias = dbj if dbias is None else dbias + dbj
        if j == 1:
            nf = _gate(nf, rs.send("l3", grads))

    dkv = jnp.concatenate([_heads_minor(dk[:, BLK:]), _heads_minor(dv[:, BLK:])], axis=1).astype(BF16)
    grads["kv"] = _gview(mm_dw(kvn, dkv, "w_kv", 512, 1))
    dh, dg = mmT_rmsbwd(dkv, _rows(W["kv"])[None], h_kv, P["norm_kv"], 0, dh, "b_kv")
    S["norm_kv"] = dg[0]
    dbh = dbias.reshape(N_HEADS, BLK, 2 * BLK)
    S["rel_bias"] = jnp.einsum("hqk,qkb->bh", dbh, onehot, precision=lax.Precision.HIGHEST)
    nf = _gate(nf, rs.send("l2", grads) + rs.reduce("l3", [dh]))

    for l in (1, 0):
        sv = saved[l]
        dh, grads = ffn_bwd(dh, sv, l, nf)
        conv = sm["conv"]
        if l == 0:
            conv = _gate(conv, rs.send("f0", grads))
            grads = {}
        dy, st = mmT_lnbwd(dh, _rows(W[f"pw2_{l}"]), sv["y"], conv, l, f"b_pw2_{l}")
        g2, S["b_pw2"][l] = mm_dw(sv["s"], dh, f"w_pw2_{l}", 512, 1, colsum=True)
        du, dtaps = dwconv_glu_bwd(dy, sv["a"], sv["u"], sm["conv"], sm["conv_rev"], l, f"b_conv_{l}")
        S["conv"][l] = st[0:3]
        S["taps"][l] = dtaps[0:CONV_W]
        if l == 0:
            rs.finish("l2", [du])
            nm = _gate(nm, rs.reduce("l1", [du]))
        g1, S["b_pw1"][l] = mm_dw(sv["xn"], du, f"w_pw1_{l}", 512, 4, colsum=True)
        grads[f"pw2_{l}"], grads[f"pw1_{l}"] = _gview(g2), _gview(g1)
        dh, dg = mmT_rmsbwd(du, _slots(W[f"pw1_{l}"]), sv["h"], nm, l, dh, f"b_pw1_{l}")
        S["norm_mix"][l] = dg[0]
        if l == 1:
            tok = rs.send("l1", grads)
            rs.finish("l3", [dh])
            nf = _gate(nf, tok + rs.reduce("l2", [dh]))
    token = rs.send("c0", grads)
    S["norm_final"] = st_final[0]
    S["loss"] = st_final[1]
    return token, dh, S


R_CONV = 37
R_SMALL = 88


def _pack_small(S):
    rows = []
    for l in range(2):
        rows += [S["taps"][l], S["conv"][l][2:3], S["conv"][l][0:2], S["b_pw2"][l], S["b_pw1"][l].reshape(2, D)]
    rows += [jnp.stack(S["norm_mix"]), jnp.stack(S["norm_ffn"]), S["norm_kv"][None], S["norm_final"][None]]
    tail = jnp.concatenate([jnp.stack(S["sinks"]).reshape(-1), S["rel_bias"].reshape(-1)])
    rows += [jnp.pad(tail, (0, D - tail.shape[0]))[None], S["loss"][None]]
    v = jnp.concatenate(rows, axis=0)
    return jnp.pad(v, ((0, R_SMALL - v.shape[0]), (0, 0)))


def kernel(x, norm_mix, norm_ffn, conv_w_pw1, conv_b_pw1, conv_w_dw, conv_b_dw, conv_ln_g, conv_ln_b, conv_w_pw2, conv_b_pw2, norm_kv, w_kv, w_q, w_o, sinks, rel_bias, ffn_w_up, ffn_w_down, norm_final, loss_target, m_norm_mix, m_norm_ffn, m_conv_w_pw1, m_conv_b_pw1, m_conv_w_dw, m_conv_b_dw, m_conv_ln_g, m_conv_ln_b, m_conv_w_pw2, m_conv_b_pw2, m_norm_kv, m_w_kv, m_w_q, m_w_o, m_sinks, m_rel_bias, m_ffn_w_up, m_ffn_w_down, m_norm_final, v_norm_mix, v_norm_ffn, v_conv_w_pw1, v_conv_b_pw1, v_conv_w_dw, v_conv_b_dw, v_conv_ln_g, v_conv_ln_b, v_conv_w_pw2, v_conv_b_pw2, v_norm_kv, v_w_kv, v_w_q, v_w_o, v_sinks, v_rel_bias, v_ffn_w_up, v_ffn_w_down, v_norm_final):
    me = 2 * lax.axis_index("x") + lax.axis_index("y")
    weights = dict(norm_mix=norm_mix, norm_ffn=norm_ffn, conv_w_pw1=conv_w_pw1, conv_b_pw1=conv_b_pw1,
                   conv_w_dw=conv_w_dw, conv_b_dw=conv_b_dw, conv_ln_g=conv_ln_g, conv_ln_b=conv_ln_b,
                   conv_w_pw2=conv_w_pw2, conv_b_pw2=conv_b_pw2, norm_kv=norm_kv, w_kv=w_kv, w_q=w_q, w_o=w_o,
                   sinks=sinks, rel_bias=rel_bias, ffn_w_up=ffn_w_up, ffn_w_down=ffn_w_down, norm_final=norm_final)
    mom_m = dict(norm_mix=m_norm_mix, norm_ffn=m_norm_ffn, conv_w_pw1=m_conv_w_pw1, conv_b_pw1=m_conv_b_pw1,
                 conv_w_dw=m_conv_w_dw, conv_b_dw=m_conv_b_dw, conv_ln_g=m_conv_ln_g, conv_ln_b=m_conv_ln_b,
                 conv_w_pw2=m_conv_w_pw2, conv_b_pw2=m_conv_b_pw2, norm_kv=m_norm_kv, w_kv=m_w_kv, w_q=m_w_q,
                 w_o=m_w_o, sinks=m_sinks, rel_bias=m_rel_bias, ffn_w_up=m_ffn_w_up, ffn_w_down=m_ffn_w_down,
                 norm_final=m_norm_final)
    mom_v = dict(norm_mix=v_norm_mix, norm_ffn=v_norm_ffn, conv_w_pw1=v_conv_w_pw1, conv_b_pw1=v_conv_b_pw1,
                 conv_w_dw=v_conv_w_dw, conv_b_dw=v_conv_b_dw, conv_ln_g=v_conv_ln_g, conv_ln_b=v_conv_ln_b,
                 conv_w_pw2=v_conv_w_pw2, conv_b_pw2=v_conv_b_pw2, norm_kv=v_norm_kv, w_kv=v_w_kv, w_q=v_w_q,
                 w_o=v_w_o, sinks=v_sinks, rel_bias=v_rel_bias, ffn_w_up=v_ffn_w_up, ffn_w_down=v_ffn_w_down,
                 norm_final=v_norm_final)

    def halves(a):
        return a.astype(BF16).reshape(2, a.shape[0] // 2, a.shape[1])

    shards = {"kv": halves(w_kv)}
    for l in range(2):
        shards[f"pw1_{l}"], shards[f"pw2_{l}"] = halves(conv_w_pw1[l]), halves(conv_w_pw2[l])
        shards[f"wq_{l}"], shards[f"wo_{l}"] = halves(w_q[l]), halves(w_o[l])
    for l in range(4):
        shards[f"up_{l}"], shards[f"down_{l}"] = halves(ffn_w_up[l]), halves(ffn_w_down[l])
    shards["small"] = jnp.concatenate(
        [conv_w_dw, conv_b_dw[:, None], conv_ln_g[:, None], conv_ln_b[:, None], conv_b_pw2[:, None],
         conv_b_pw1.reshape(2, 2, 256), jnp.zeros((2, 3, 256), F32)], axis=1)
    ag = WeightGather(shards, AG_GROUPS)
    big = {"conv_w_pw1": "pw1", "conv_w_pw2": "pw2", "w_q": "wq", "w_o": "wo", "ffn_w_up": "up",
           "ffn_w_down": "down", "w_kv": "kv"}
    rs = GradReduce({"pw1": (2, 512, 512), "pw2": (2, 128, D), "wq": (2, 128, D), "wo": (2, 128, D),
                     "up": (4, 512, DFF // 2), "down": (4, DFF // 8, D), "kv": (1, 128, 512)})

    P = dict(norm_mix=norm_mix[:, None], norm_ffn=norm_ffn[:, None], norm_kv=norm_kv[None, None],
             norm_final=norm_final[None], sinks=sinks, rel_bias=rel_bias)
    token, grad_x, S = run_step(x[0], loss_target[0], P, ag, rs)

    rs.finish("l1", [grad_x])
    vsum = allreduce_small(_gate(_pack_small(S), token + rs.reduce("f0", [grad_x])))
    col = lambda a: lax.dynamic_slice_in_dim(a, me * 256, 256, axis=-1)
    grads = {}
    for l in range(2):
        base = l * R_CONV
        grads.setdefault("conv_w_dw", []).append(col(vsum[base:base + 31]))
        grads.setdefault("conv_b_dw", []).append(col(vsum[base + 31]))
        grads.setdefault("conv_ln_g", []).append(col(vsum[base + 32]))
        grads.setdefault("conv_ln_b", []).append(col(vsum[base + 33]))
        grads.setdefault("conv_b_pw2", []).append(col(vsum[base + 34]))
        grads.setdefault("conv_b_pw1", []).append(
            lax.dynamic_slice_in_dim(vsum[base + 35:base + 37].reshape(2 * D), me * 512, 512, axis=0))
    grads = {k: jnp.stack(v) for k, v in grads.items()}
    base = 2 * R_CONV
    grads["norm_mix"] = vsum[base:base + 4]
    grads["norm_ffn"] = vsum[base + 4:base + 8]
    grads["norm_kv"] = vsum[base + 8]
    grads["norm_final"] = vsum[base + 9]
    grads["sinks"] = vsum[base + 10, 0:32].reshape(2, 16)
    grads["rel_bias"] = vsum[base + 10, 32:32 + 512].reshape(32, 16)
    loss = vsum[base + 11, 0]

    delta, new_m, new_v = {}, {}, {}
    rest = [n for n in weights if n not in big]

    def pack(dct):
        flat = jnp.concatenate([dct[n].reshape(-1) for n in rest])
        return jnp.pad(flat, (0, (-flat.shape[0]) % (8 * 128))).reshape(-1, 128)

    d, nm, nv = adamw(pack(weights), pack(grads), pack(mom_m), pack(mom_v), "adamw_small")
    off = 0
    for n in rest:
        shp = weights[n].shape
        sz = int(np.prod(shp))
        delta[n] = d.reshape(-1)[off:off + sz].reshape(shp)
        new_m[n] = nm.reshape(-1)[off:off + sz].reshape(shp)
        new_v[n] = nv.reshape(-1)[off:off + sz].reshape(shp)
        off += sz

    def update(n):
        shp = weights[n].shape
        r2 = (int(np.prod(shp[:-1])), shp[-1])
        grads[n] = rs.J[big[n]].reshape(shp)
        d, nm, nv = adamw(weights[n].reshape(r2), grads[n].reshape(r2), mom_m[n].reshape(r2), mom_v[n].reshape(r2),
                          f"adamw_{n}")
        delta[n], new_m[n], new_v[n] = d.reshape(shp), nm.reshape(shp), nv.reshape(shp)

    rs.finish("f0", [vsum])
    for n in ("ffn_w_up", "ffn_w_down"):
        update(n)
    rs.reduce("c0", [delta["ffn_w_down"]])
    for n in ("w_q", "w_o", "w_kv"):
        update(n)
    rs.finish("c0", [delta["w_kv"]])
    for n in ("conv_w_pw1", "conv_w_pw2"):
        update(n)

    order = list(weights)
    return (loss, grad_x[None], *[grads[n] for n in order], *[delta[n] for n in order],
            *[new_m[n] for n in order], *[new_v[n] for n in order])
```

```python
import functools
import math

import numpy as np
import jax
import jax.numpy as jnp
from jax import lax
from jax.experimental import pallas as pl
from jax.experimental.pallas import tpu as pltpu

F32 = jnp.float32
BF16 = jnp.bfloat16
MESH = pl.DeviceIdType.MESH

D = 1024
DFF = 2816
N_HEADS = 16
N_KV = 4
GROUP = 4
HD = 64
BLK = 128
CONV_W = 31
HALO = 32
N_BUCKETS = 32
MAX_DISTANCE = 128
EPS = 1e-6
NEG_INF = -1e30
TM = 256
VMEM_LIMIT = 56 * 2 ** 20

ADAM_LR, ADAM_B1, ADAM_B2, ADAM_EPS, ADAM_WD, ADAM_STEP = 0.001, 0.9, 0.999, 1e-08, 0.01, 10


def _cp(*sem):
    return pltpu.CompilerParams(dimension_semantics=sem, vmem_limit_bytes=VMEM_LIMIT)


def _sigmoid(x):
    return 1.0 / (1.0 + jnp.exp(-x))


def _row(tm, n):
    return pl.BlockSpec((tm, n), lambda i: (i, 0))


def _const(shape):
    nd = len(shape)
    return pl.BlockSpec(shape, lambda i: (0,) * nd)


def _layer(shape, l):
    nd = len(shape)
    return pl.BlockSpec((None,) + tuple(shape), lambda i: (l,) + (0,) * nd)


def _dot(a, b):
    return jnp.dot(a, b, preferred_element_type=F32)


def _dot_nt(a, b):
    return lax.dot_general(a, b, (((1,), (1,)), ((), ())), preferred_element_type=F32)


def _dot_tn(a, b):
    return lax.dot_general(a, b, (((0,), (0,)), ((), ())), preferred_element_type=F32)


def _rms(x):
    return lax.rsqrt(jnp.mean(x * x, axis=-1, keepdims=True) + EPS)


def norm_mm_glu(h, g, l, w, b, name):
    T = h.shape[0]
    ns = w.shape[-1]

    def body(h_ref, g_ref, w_ref, b_ref, xn_ref, u_ref, a_ref):
        x = h_ref[...]
        xn = (x * _rms(x) * g_ref[...]).astype(BF16)
        xn_ref[...] = xn
        for s in range(2):
            lo, hi = s * ns, (s + 1) * ns
            u1 = _dot(xn, w_ref[s]) + b_ref[:, lo:hi]
            u2 = _dot(xn, w_ref[2 + s]) + b_ref[:, D + lo:D + hi]
            u_ref[:, lo:hi] = u1.astype(BF16)
            u_ref[:, D + lo:D + hi] = u2.astype(BF16)
            a_ref[:, lo:hi] = u1 * _sigmoid(u2)

    return pl.pallas_call(
        body, name=name, grid=(T // TM,),
        in_specs=[_row(TM, D), _layer((1, D), l), _const((4, D, ns)), _layer((1, 2 * D), l)],
        out_specs=[_row(TM, D), _row(TM, 2 * D), _row(TM, D)],
        out_shape=[jax.ShapeDtypeStruct((T, D), BF16), jax.ShapeDtypeStruct((T, 2 * D), BF16),
                   jax.ShapeDtypeStruct((T, D), F32)],
        compiler_params=_cp("parallel"),
    )(h, g, w, b)


SUB = 8


def _make_shifts(sh):
    n = TM + HALO - SUB
    for r in range(1, SUB):
        for r0 in range(0, n, 40):
            sh[r, r0:r0 + 40, :] = sh[0, pl.ds(r + r0, 40), :]


def _shifted(sh, off, rows, cols):
    return sh[off % SUB, pl.ds(off - off % SUB, rows), cols]


def _conv_taps(sh, w_ref, out_ref, first):
    RB, LB = 32, 512
    for r0 in range(0, TM, RB):
        for c0 in range(0, D, LB):
            acc = jnp.zeros((RB, LB), F32)
            for k in range(CONV_W):
                acc = acc + w_ref[k:k + 1, c0:c0 + LB] * _shifted(sh, first + k + r0, RB, slice(c0, c0 + LB))
            out_ref[r0:r0 + RB, c0:c0 + LB] = acc


def dwconv_ln_silu(a, sm, l, name):
    T = a.shape[0]
    nb = TM // HALO

    def body(cur_ref, prev_ref, sm_ref, y_ref, s_ref, sh):
        i = pl.program_id(0)
        sh[0, 0:HALO, :] = jnp.where(i > 0, prev_ref[...], 0.0)
        sh[0, HALO:HALO + TM, :] = cur_ref[...]
        _make_shifts(sh)
        _conv_taps(sh, sm_ref, y_ref, HALO - (CONV_W - 1))
        y = y_ref[...] + sm_ref[31:32, :]
        y_ref[...] = y
        mu = jnp.mean(y, axis=-1, keepdims=True)
        yc = y - mu
        rstd = lax.rsqrt(jnp.mean(yc * yc, axis=-1, keepdims=True) + EPS)
        z = yc * rstd * sm_ref[32:33, :] + sm_ref[33:34, :]
        s_ref[...] = (z * _sigmoid(z)).astype(BF16)

    return pl.pallas_call(
        body, name=name, grid=(T // TM,),
        in_specs=[_row(TM, D), pl.BlockSpec((HALO, D), lambda i: (jnp.maximum(i * nb - 1, 0), 0)),
                  _layer((40, D), l)],
        out_specs=[_row(TM, D), _row(TM, D)],
        out_shape=[jax.ShapeDtypeStruct((T, D), F32), jax.ShapeDtypeStruct((T, D), BF16)],
        scratch_shapes=[pltpu.VMEM((SUB, TM + HALO, D), F32)],
        compiler_params=_cp("parallel"),
    )(a, a, sm)


def mm_bias_res(xb, w, b, bl, res, name):
    T, K = xb.shape

    def body(x_ref, w_ref, b_ref, r_ref, o_ref):
        o_ref[...] = _dot(x_ref[...], w_ref[...]) + b_ref[...] + r_ref[...]

    return pl.pallas_call(
        body, name=name, grid=(T // TM,),
        in_specs=[_row(TM, K), _const((K, D)), _layer((1, D), bl), _row(TM, D)],
        out_specs=_row(TM, D), out_shape=jax.ShapeDtypeStruct((T, D), F32),
        compiler_params=_cp("parallel"),
    )(xb, w, b, res)


def norm_mm_swiglu(h, g, l, w, name):
    T = h.shape[0]
    ns = w.shape[-1]

    def body(h_ref, g_ref, w_ref, xn_ref, gu_ref, f_ref):
        x = h_ref[...]
        xn = (x * _rms(x) * g_ref[...]).astype(BF16)
        xn_ref[...] = xn
        for s in range(2):
            lo, hi = s * ns, (s + 1) * ns
            gate = _dot(xn, w_ref[s])
            up = _dot(xn, w_ref[2 + s])
            gu_ref[:, lo:hi] = gate.astype(BF16)
            gu_ref[:, DFF + lo:DFF + hi] = up.astype(BF16)
            f_ref[:, lo:hi] = (gate * _sigmoid(gate) * up).astype(BF16)

    return pl.pallas_call(
        body, name=name, grid=(T // TM,),
        in_specs=[_row(TM, D), _layer((1, D), l), _const((4, D, ns))],
        out_specs=[_row(TM, D), _row(TM, 2 * DFF), _row(TM, DFF)],
        out_shape=[jax.ShapeDtypeStruct((T, D), BF16), jax.ShapeDtypeStruct((T, 2 * DFF), BF16),
                   jax.ShapeDtypeStruct((T, DFF), BF16)],
        compiler_params=_cp("parallel"),
    )(h, g, w)


def norm_mm(h, g, gl, w, name, scale=1.0):
    T = h.shape[0]
    N = w.shape[-1]

    def body(h_ref, g_ref, w_ref, xn_ref, o_ref):
        x = h_ref[...]
        xn = (x * _rms(x) * g_ref[...]).astype(BF16)
        xn_ref[...] = xn
        o_ref[...] = (_dot(xn, w_ref[...]) * scale).astype(BF16)

    return pl.pallas_call(
        body, name=name, grid=(T // TM,),
        in_specs=[_row(TM, D), _layer((1, D), gl), _const((D, N))],
        out_specs=[_row(TM, D), _row(TM, N)],
        out_shape=[jax.ShapeDtypeStruct((T, D), BF16), jax.ShapeDtypeStruct((T, N), BF16)],
        compiler_params=_cp("parallel"),
    )(h, g, w)


QB = 4
QW = GROUP * BLK


def band_mask():
    qi = np.arange(QW)[None, :] % BLK
    kj = np.arange(2 * BLK)[:, None]
    band = ((kj < BLK) & (kj > qi)) | ((kj >= BLK) & (kj - BLK <= qi))
    first = band & (kj >= BLK)
    return np.where(np.stack([first, band]), 0.0, NEG_INF).astype(np.float32)


def _softmax_cols(s, sink):
    m = jnp.maximum(jnp.max(s, axis=0, keepdims=True), sink)
    p = jnp.exp(s - m)
    es = jnp.exp(sink - m)
    inv = 1.0 / (jnp.sum(p, axis=0, keepdims=True) + es)
    return p, inv, es


def _attn_specs(T):
    W = QB * BLK
    qspec = pl.BlockSpec((None, GROUP, HD, W), lambda kv, n: (kv, 0, 0, n))
    kspec = pl.BlockSpec((None, T + BLK, HD), lambda kv, n: (kv, 0, 0))
    ktspec = [pl.BlockSpec((None, HD, W), lambda kv, n: (kv, 0, n)),
              pl.BlockSpec((None, HD, BLK), lambda kv, n: (kv, 0, (n + 1) * QB))]
    bspec = pl.BlockSpec((2, None, 2 * BLK, QW), lambda kv, n: (0, kv, 0, 0))
    sspec = pl.BlockSpec((None, 1, QW), lambda kv, n: (kv, 0, 0))
    return qspec, kspec, ktspec, bspec, sspec


def _attn_block(n, b):
    blk = n * QB + b
    rows = pl.ds(pl.multiple_of(blk * BLK, BLK), 2 * BLK)
    return rows, (jnp.minimum(blk, 1) if b == 0 else 1)


def _band_cols(main_ref, tail_ref, b):
    if b < QB - 1:
        return main_ref[:, b * BLK:(b + 2) * BLK]
    return jnp.concatenate([main_ref[:, b * BLK:], tail_ref[...]], axis=1)


def _heads_side_by_side(ref, qs):
    return jnp.concatenate([ref[g, :, qs] for g in range(GROUP)], axis=1)


def attn_fwd(q, kp, vt, bias, sink, name):
    T = q.shape[3]
    qspec, kspec, ktspec, bspec, sspec = _attn_specs(T)

    def body(q_ref, k_ref, vt_ref, vtt_ref, b_ref, s_ref, o_ref, pb):
        n = pl.program_id(1)
        for b in range(QB):
            rows, table = _attn_block(n, b)
            qs = slice(b * BLK, (b + 1) * BLK)
            st = _dot(k_ref[rows, :], _heads_side_by_side(q_ref, qs))
            for g in range(GROUP):
                hs = slice(g * BLK, (g + 1) * BLK)
                p, inv, _ = _softmax_cols(st[:, hs] + b_ref[table, :, hs], s_ref[:, hs])
                pb[:, hs] = (p * inv).astype(BF16)
            ot = _dot(_band_cols(vt_ref, vtt_ref, b), pb[...])
            for g in range(GROUP):
                o_ref[g, :, qs] = ot[:, g * BLK:(g + 1) * BLK].astype(BF16)

    return pl.pallas_call(
        body, name=name, grid=(N_KV, T // (QB * BLK)),
        in_specs=[qspec, kspec, *ktspec, bspec, sspec], out_specs=qspec,
        out_shape=jax.ShapeDtypeStruct((N_KV, GROUP, HD, T), BF16),
        scratch_shapes=[pltpu.VMEM((2 * BLK, QW), BF16)],
        compiler_params=_cp("parallel", "parallel"),
    )(q, kp, vt, vt, bias, sink)


def attn_bwd(q, kp, kt, vp, bias, sink, o, do, name):
    T = q.shape[3]
    qspec, kspec, ktspec, bspec, sspec = _attn_specs(T)

    def body(q_ref, k_ref, kt_ref, ktt_ref, v_ref, b_ref, s_ref, o_ref, do_ref,
             dq_ref, dk_ref, dv_ref, db_ref, ds_ref, pb, dsb):
        n = pl.program_id(1)

        @pl.when(n == 0)
        def _():
            dk_ref[...] = jnp.zeros_like(dk_ref)
            dv_ref[...] = jnp.zeros_like(dv_ref)
            db_ref[...] = jnp.zeros_like(db_ref)
            ds_ref[...] = jnp.zeros_like(ds_ref)

        for b in range(QB):
            rows, table = _attn_block(n, b)
            qs = slice(b * BLK, (b + 1) * BLK)
            q4 = _heads_side_by_side(q_ref, qs)
            do4 = _heads_side_by_side(do_ref, qs)
            st = _dot(k_ref[rows, :], q4)
            dpt = _dot(v_ref[rows, :], do4)
            for g in range(GROUP):
                hs = slice(g * BLK, (g + 1) * BLK)
                p, inv, es = _softmax_cols(st[:, hs] + b_ref[table, :, hs], s_ref[:, hs])
                probs = p * inv
                delta = jnp.sum(do_ref[g, :, qs].astype(F32) * o_ref[g, :, qs].astype(F32), axis=0, keepdims=True)
                dS = probs * (dpt[:, hs] - delta)
                ds_ref[:, hs] += -(es * inv) * delta
                db_ref[:, hs] += dS
                pb[:, hs] = probs.astype(BF16)
                dsb[:, hs] = dS.astype(BF16)
            dqt = _dot(_band_cols(kt_ref, ktt_ref, b), dsb[...]) * (HD ** -0.5)
            for g in range(GROUP):
                dq_ref[g, :, qs] = dqt[:, g * BLK:(g + 1) * BLK].astype(BF16)
            dk_ref[rows, :] += _dot_nt(dsb[...], q4)
            dv_ref[rows, :] += _dot_nt(pb[...], do4)

    kout = pl.BlockSpec((None, T + BLK, HD), lambda kv, n: (kv, 0, 0))
    dbspec = pl.BlockSpec((None, 2 * BLK, QW), lambda kv, n: (kv, 0, 0))
    return pl.pallas_call(
        body, name=name, grid=(N_KV, T // (QB * BLK)),
        in_specs=[qspec, kspec, *ktspec, kspec, bspec, sspec, qspec, qspec],
        out_specs=[qspec, kout, kout, dbspec, sspec],
        out_shape=[jax.ShapeDtypeStruct((N_KV, GROUP, HD, T), BF16),
                   jax.ShapeDtypeStruct((N_KV, T + BLK, HD), F32), jax.ShapeDtypeStruct((N_KV, T + BLK, HD), F32),
                   jax.ShapeDtypeStruct((N_KV, 2 * BLK, QW), F32), jax.ShapeDtypeStruct((N_KV, 1, QW), F32)],
        scratch_shapes=[pltpu.VMEM((2 * BLK, QW), BF16), pltpu.VMEM((2 * BLK, QW), BF16)],
        compiler_params=_cp("parallel", "arbitrary"),
    )(q, kp, kt, kt, vp, bias, sink, o, do)


def final_loss(h, g, target, name):
    T = h.shape[0]

    def body(h_ref, g_ref, t_ref, dh_ref, st_ref):
        i = pl.program_id(0)

        @pl.when(i == 0)
        def _():
            st_ref[...] = jnp.zeros_like(st_ref)

        x = h_ref[...]
        r = _rms(x)
        xh = x * r
        e = xh * g_ref[...] - t_ref[...]
        loss = 0.5 * jnp.sum(jnp.mean(e * e, axis=-1, keepdims=True))
        dy = e * (1.0 / D)
        st_ref[0:1, :] += jnp.sum(dy * xh, axis=0, keepdims=True)
        lane = lax.broadcasted_iota(jnp.int32, (1, D), 1)
        st_ref[1:2, :] += jnp.where(lane == 0, loss, 0.0)
        dxh = dy * g_ref[...]
        dh_ref[...] = r * (dxh - xh * jnp.mean(dxh * xh, axis=-1, keepdims=True))

    return pl.pallas_call(
        body, name=name, grid=(T // TM,),
        in_specs=[_row(TM, D), _const((1, D)), _row(TM, D)],
        out_specs=[_row(TM, D), _const((8, D))],
        out_shape=[jax.ShapeDtypeStruct((T, D), F32), jax.ShapeDtypeStruct((8, D), F32)],
        compiler_params=_cp("arbitrary"),
    )(h, g, target)


def mm_dw(x, dy, name, tn, slots, colsum=False):
    T, K = x.shape
    N = dy.shape[1]
    tt = min(T, 1024)
    nt = T // tt
    ns = N // slots
    per = ns // tn

    def body(x_ref, dy_ref, *rest):
        if colsum:
            dw_ref, cs_ref, acc, cacc = rest
        else:
            dw_ref, acc = rest
        t = pl.program_id(1)

        @pl.when(t == 0)
        def _():
            acc[...] = jnp.zeros_like(acc)
            if colsum:
                cacc[...] = jnp.zeros_like(cacc)

        dyv = dy_ref[...]
        acc[...] += _dot_tn(x_ref[...].astype(BF16), dyv.astype(BF16))
        if colsum:
            cacc[...] += jnp.sum(dyv.astype(F32), axis=0, keepdims=True)

        @pl.when(t == nt - 1)
        def _():
            dw_ref[...] = acc[...].astype(BF16)
            if colsum:
                cs_ref[...] = cacc[...]

    out_specs = [pl.BlockSpec((None, K, tn), lambda j, t: (j // per, 0, j % per))]
    out_shape = [jax.ShapeDtypeStruct((slots, K, ns), BF16)]
    scratch = [pltpu.VMEM((K, tn), F32)]
    if colsum:
        out_specs.append(pl.BlockSpec((1, tn), lambda j, t: (0, j)))
        out_shape.append(jax.ShapeDtypeStruct((1, N), F32))
        scratch.append(pltpu.VMEM((1, tn), F32))
    res = pl.pallas_call(
        body, name=name, grid=(N // tn, nt),
        in_specs=[pl.BlockSpec((tt, K), lambda j, t: (t, 0)), pl.BlockSpec((tt, tn), lambda j, t: (t, j))],
        out_specs=out_specs, out_shape=out_shape, scratch_shapes=scratch,
        compiler_params=_cp("parallel", "arbitrary"),
    )(x, dy)
    return tuple(res) if colsum else res[0]


def mmT_swiglu_bwd(dh, w, gu, name):
    T = dh.shape[0]
    half = DFF // 2

    def body(dh_ref, w_ref, gu_ref, du_ref):
        dhb = dh_ref[...].astype(BF16)
        for s in range(2):
            lo, hi = s * half, (s + 1) * half
            df = _dot_nt(dhb, w_ref[lo:hi, :])
            gate = gu_ref[:, lo:hi].astype(F32)
            up = gu_ref[:, DFF + lo:DFF + hi].astype(F32)
            sg = _sigmoid(gate)
            du_ref[:, lo:hi] = (df * up * sg * (1.0 + gate * (1.0 - sg))).astype(BF16)
            du_ref[:, DFF + lo:DFF + hi] = (df * gate * sg).astype(BF16)

    return pl.pallas_call(
        body, name=name, grid=(T // TM,),
        in_specs=[_row(TM, D), _const((DFF, D)), _row(TM, 2 * DFF)],
        out_specs=_row(TM, 2 * DFF), out_shape=jax.ShapeDtypeStruct((T, 2 * DFF), BF16),
        compiler_params=_cp("parallel"),
    )(dh, w, gu)


def mmT_rmsbwd(du, w, h, g, gl, dh_in, name):
    T, N = du.shape
    slots = w.shape[0]
    ns = N // slots

    def body(du_ref, w_ref, h_ref, g_ref, di_ref, dh_ref, dg_ref):
        i = pl.program_id(0)

        @pl.when(i == 0)
        def _():
            dg_ref[...] = jnp.zeros_like(dg_ref)

        dxn = _dot_nt(du_ref[:, 0:ns], w_ref[0])
        for s in range(1, slots):
            dxn = dxn + _dot_nt(du_ref[:, s * ns:(s + 1) * ns], w_ref[s])
        x = h_ref[...]
        r = _rms(x)
        xh = x * r
        dg_ref[0:1, :] += jnp.sum(dxn * xh, axis=0, keepdims=True)
        dxh = dxn * g_ref[...]
        dh_ref[...] = di_ref[...] + r * (dxh - xh * jnp.mean(dxh * xh, axis=-1, keepdims=True))

    return pl.pallas_call(
        body, name=name, grid=(T // TM,),
        in_specs=[_row(TM, N), _const((slots, D, ns)), _row(TM, D), _layer((1, D), gl), _row(TM, D)],
        out_specs=[_row(TM, D), _const((8, D))],
        out_shape=[jax.ShapeDtypeStruct((T, D), F32), jax.ShapeDtypeStruct((8, D), F32)],
        compiler_params=_cp("arbitrary"),
    )(du, w, h, g, dh_in)


def mmT(dh, w, name):
    T = dh.shape[0]
    N = w.shape[0]

    def body(dh_ref, w_ref, o_ref):
        o_ref[...] = _dot_nt(dh_ref[...].astype(BF16), w_ref[...]).astype(BF16)

    return pl.pallas_call(
        body, name=name, grid=(T // TM,),
        in_specs=[_row(TM, D), _const((N, D))],
        out_specs=_row(TM, N), out_shape=jax.ShapeDtypeStruct((T, N), BF16),
        compiler_params=_cp("parallel"),
    )(dh, w)


def mmT_lnbwd(dh, w, y, sm, l, name):
    T = dh.shape[0]

    def body(dh_ref, w_ref, y_ref, sm_ref, dy_ref, st_ref):
        i = pl.program_id(0)

        @pl.when(i == 0)
        def _():
            st_ref[...] = jnp.zeros_like(st_ref)

        ds = _dot_nt(dh_ref[...].astype(BF16), w_ref[...])
        y = y_ref[...]
        mu = jnp.mean(y, axis=-1, keepdims=True)
        yc = y - mu
        rstd = lax.rsqrt(jnp.mean(yc * yc, axis=-1, keepdims=True) + EPS)
        xh = yc * rstd
        gam = sm_ref[32:33, :]
        z = xh * gam + sm_ref[33:34, :]
        sg = _sigmoid(z)
        dz = ds * sg * (1.0 + z * (1.0 - sg))
        st_ref[0:1, :] += jnp.sum(dz * xh, axis=0, keepdims=True)
        st_ref[1:2, :] += jnp.sum(dz, axis=0, keepdims=True)
        dxh = dz * gam
        dy = rstd * (dxh - jnp.mean(dxh, axis=-1, keepdims=True) - xh * jnp.mean(dxh * xh, axis=-1, keepdims=True))
        st_ref[2:3, :] += jnp.sum(dy, axis=0, keepdims=True)
        dy_ref[...] = dy

    return pl.pallas_call(
        body, name=name, grid=(T // TM,),
        in_specs=[_row(TM, D), _const((D, D)), _row(TM, D), _layer((40, D), l)],
        out_specs=[_row(TM, D), _const((8, D))],
        out_shape=[jax.ShapeDtypeStruct((T, D), F32), jax.ShapeDtypeStruct((8, D), F32)],
        compiler_params=_cp("arbitrary"),
    )(dh, w, y, sm)


def dwconv_glu_bwd(dy, a, u, sm, smrev, l, name):
    T = dy.shape[0]
    nb = TM // HALO
    last = T // HALO - 1

    def body(dy_ref, dyn_ref, a_ref, ap_ref, u_ref, sm_ref, rev_ref, du_ref, dw_ref, shd, sha, da):
        i = pl.program_id(0)

        @pl.when(i == 0)
        def _():
            dw_ref[...] = jnp.zeros_like(dw_ref)

        shd[0, 0:TM, :] = dy_ref[...]
        shd[0, TM:TM + HALO, :] = jnp.where(i < pl.num_programs(0) - 1, dyn_ref[...], 0.0)
        sha[0, 0:HALO, :] = jnp.where(i > 0, ap_ref[...], 0.0)
        sha[0, HALO:HALO + TM, :] = a_ref[...]
        _make_shifts(shd)
        _make_shifts(sha)
        _conv_taps(shd, rev_ref, da, 0)
        LB = 512
        for c0 in range(0, D, LB):
            for k in range(CONV_W):
                acc = jnp.zeros((SUB, LB), F32)
                for r0 in range(0, TM, SUB):
                    acc = acc + dy_ref[r0:r0 + SUB, c0:c0 + LB] * _shifted(sha, HALO - (CONV_W - 1) + k + r0, SUB,
                                                                           slice(c0, c0 + LB))
                dw_ref[k:k + 1, c0:c0 + LB] += jnp.sum(acc, axis=0, keepdims=True)
        dav = da[...]
        u1 = u_ref[:, 0:D].astype(F32)
        sg = _sigmoid(u_ref[:, D:2 * D].astype(F32))
        du_ref[:, 0:D] = (dav * sg).astype(BF16)
        du_ref[:, D:2 * D] = (dav * u1 * sg * (1.0 - sg)).astype(BF16)

    return pl.pallas_call(
        body, name=name, grid=(T // TM,),
        in_specs=[_row(TM, D), pl.BlockSpec((HALO, D), lambda i: (jnp.minimum((i + 1) * nb, last), 0)),
                  _row(TM, D), pl.BlockSpec((HALO, D), lambda i: (jnp.maximum(i * nb - 1, 0), 0)),
                  _row(TM, 2 * D), _layer((40, D), l), _layer((40, D), l)],
        out_specs=[_row(TM, 2 * D), _const((32, D))],
        out_shape=[jax.ShapeDtypeStruct((T, 2 * D), BF16), jax.ShapeDtypeStruct((32, D), F32)],
        scratch_shapes=[pltpu.VMEM((SUB, TM + HALO, D), F32), pltpu.VMEM((SUB, TM + HALO, D), F32),
                        pltpu.VMEM((TM, D), F32)],
        compiler_params=_cp("arbitrary"),
    )(dy, dy, a, a, u, sm, smrev)


def _rows_tile(R):
    for t in (512, 256, 128, 64, 32, 16, 8):
        if R % t == 0:
            return t
    return R


def add8(own, others, name):
    R, C = own.shape
    tr = _rows_tile(R)

    def body(o_ref, x_ref, out_ref):
        acc = o_ref[...].astype(F32)
        for k in range(7):
            acc = acc + x_ref[k].astype(F32)
        out_ref[...] = acc

    return pl.pallas_call(
        body, name=name, grid=(R // tr,),
        in_specs=[_row(tr, C), pl.BlockSpec((7, tr, C), lambda i: (0, i, 0))], out_specs=_row(tr, C),
        out_shape=jax.ShapeDtypeStruct((R, C), F32), compiler_params=_cp("parallel"),
    )(own, others)


def adamw(w, g, m, v, name):
    R, C = w.shape
    tr = _rows_tile(R)

    def body(w_ref, g_ref, m_ref, v_ref, d_ref, nm_ref, nv_ref):
        gv = g_ref[...]
        nm = ADAM_B1 * m_ref[...] + (1.0 - ADAM_B1) * gv
        nv = ADAM_B2 * v_ref[...] + (1.0 - ADAM_B2) * (gv * gv)
        m_hat = nm / (1.0 - ADAM_B1 ** ADAM_STEP)
        v_hat = nv / (1.0 - ADAM_B2 ** ADAM_STEP)
        d_ref[...] = -ADAM_LR * (m_hat / (jnp.sqrt(v_hat) + ADAM_EPS) + ADAM_WD * w_ref[...])
        nm_ref[...] = nm
        nv_ref[...] = nv

    sd = jax.ShapeDtypeStruct((R, C), F32)
    return pl.pallas_call(
        body, name=name, grid=(R // tr,),
        in_specs=[_row(tr, C)] * 4, out_specs=[_row(tr, C)] * 3, out_shape=[sd, sd, sd],
        compiler_params=_cp("parallel"),
    )(w, g, m, v)


ANY = pl.BlockSpec(memory_space=pl.ANY)
HBM = pl.BlockSpec(memory_space=pltpu.HBM)
SEM = pl.BlockSpec(memory_space=pltpu.SEMAPHORE)
EFFECT = pltpu.SideEffectType.DATAFLOW_SIDE_EFFECTING


def _place():
    x, y, c = lax.axis_index("x"), lax.axis_index("y"), lax.axis_index("c")
    chips = [(1 - x, y), (x, 1 - y), (1 - x, 1 - y)]
    return x, y, c, chips


def _copy(src, dst, send, recv, k, to):
    return pltpu.make_async_remote_copy(src_ref=src, dst_ref=dst, send_sem=send.at[k], recv_sem=recv.at[k],
                                        device_id=to, device_id_type=MESH)


def xchg_start(name, bufs, plan, n, after=()):
    nb = len(bufs)

    na = len(after)

    def body(*refs):
        send, recv, token = refs[nb + na], refs[nb + na + 1], refs[-1]
        for k, (src, dst, to) in enumerate(plan(refs[:nb])):
            _copy(src, dst, send, recv, k, to).start()
        token[...] = jnp.zeros_like(token)

    outs = pl.pallas_call(
        body, name=name,
        out_shape=(pltpu.SemaphoreType.DMA((n,)), pltpu.SemaphoreType.DMA((n,)),
                   *[pltpu.HBM(b.shape, b.dtype) for b in bufs], jax.ShapeDtypeStruct((8, 128), F32)),
        in_specs=[HBM] * nb + [ANY] * na,
        out_specs=(SEM, SEM, *[HBM] * nb, pl.BlockSpec(memory_space=pltpu.VMEM)),
        input_output_aliases={i: 2 + i for i in range(nb)},
        compiler_params=pltpu.CompilerParams(has_side_effects=EFFECT),
    )(*[pltpu.with_memory_space_constraint(b, pltpu.HBM) for b in bufs], *after)
    return dict(name=name, send=outs[0], recv=outs[1], bufs=list(outs[2:2 + nb]), plan=plan), outs[-1]


def xchg_wait(flight, after):
    bufs, plan = flight["bufs"], flight["plan"]
    nb = len(bufs)

    def body(*refs):
        send, recv = refs[nb], refs[nb + 1]
        for k, (src, dst, to) in enumerate(plan(refs[:nb])):
            cp = _copy(src, dst, send, recv, k, to)
            cp.wait_send()
            cp.wait_recv()

    outs = pl.pallas_call(
        body, name=flight["name"] + "_wait",
        out_shape=tuple(pltpu.HBM(b.shape, b.dtype) for b in bufs),
        in_specs=[HBM] * nb + [SEM, SEM] + [ANY] * len(after),
        out_specs=tuple([HBM] * nb), input_output_aliases={i: i for i in range(nb)},
        compiler_params=pltpu.CompilerParams(has_side_effects=EFFECT),
    )(*bufs, flight["send"], flight["recv"], *after)
    return list(outs)


def _flip(k, x, y, c):
    return ((1 - x) if k & 4 else x, (1 - y) if k & 2 else y, (1 - c) if k & 1 else c)


class WeightGather:
    def __init__(self, shards, groups):
        me = 2 * lax.axis_index("x") + lax.axis_index("y")
        self.names = dict(groups)
        self.ici, self.d2d = {}, {}
        self.token = None
        for gname, names in groups:
            nt = len(names)
            srcs = [shards[n] for n in names]
            lands = [lax.dynamic_update_slice(lax.empty((4,) + s.shape, s.dtype), s[None], (me, 0, 0, 0))
                     for s in srcs]

            def plan(refs, nt=nt):
                x, y, c, chips = _place()
                return [(refs[t].at[c], refs[nt + t].at[2 * x + y, c], (cx, cy, c))
                        for t in range(nt) for cx, cy in chips]

            self.ici[gname], self.token = xchg_start(f"ag_ici_{gname}", srcs + lands, plan, 3 * nt,
                                                     after=[] if self.token is None else [self.token])

    def forward(self, gname, after):
        nt = len(self.names[gname])
        lands = xchg_wait(self.ici.pop(gname), after)[nt:]

        def plan(refs):
            x, y, c, chips = _place()
            out = []
            for t in range(nt):
                for cx, cy in chips:
                    piece = refs[t].at[2 * cx + cy, c]
                    out.append((piece, piece, (x, y, 1 - c)))
            return out

        self.d2d[gname], token = xchg_start(f"ag_d2d_{gname}", lands, plan, 3 * nt)
        return token

    def get(self, gname, after):
        lands = xchg_wait(self.d2d.pop(gname), after)
        return dict(zip(self.names[gname], lands))


class GradReduce:
    def __init__(self, kinds):
        self.J = {k: lax.empty((L, 2, a2, b), F32) for k, (L, a2, b) in kinds.items()}
        self.x, self.j = {}, {}

    @staticmethod
    def _where(name):
        kind, _, l = name.partition("_")
        return kind, int(l or 0)

    def send(self, gname, grads):
        names = list(grads)
        nt = len(names)
        gs = [grads[n] for n in names]
        xs = [lax.empty((7,) + g.shape[2:], g.dtype) for g in gs]

        def plan(refs):
            x, y, c, _ = _place()
            out = []
            for t in range(nt):
                for k in range(1, 8):
                    px, py, pc = _flip(k, x, y, c)
                    out.append((refs[t].at[2 * px + py, pc], refs[nt + t].at[k - 1], (px, py, pc)))
            return out

        flight, token = xchg_start(f"rs_x_{gname}", gs + xs, plan, 7 * nt)
        self.x[gname] = (names, flight)
        return token

    def reduce(self, gname, after):
        names, flight = self.x.pop(gname)
        nt = len(names)
        bufs = xchg_wait(flight, after)
        me, c = 2 * lax.axis_index("x") + lax.axis_index("y"), lax.axis_index("c")
        hs = []
        for t, n in enumerate(names):
            g = bufs[t]
            own = lax.dynamic_slice(g, (me, c, 0, 0), (1, 1) + g.shape[2:])[0, 0]
            hs.append(add8(own, bufs[nt + t], f"rs_add_{n}"))
        where = [self._where(n) for n in names]

        def plan(refs):
            x, y, c, _ = _place()
            return [(refs[t], refs[nt + t].at[where[t][1], c], (x, y, 1 - c)) for t in range(nt)]

        flight, token = xchg_start(f"rs_join_{gname}", hs + [self.J[k] for k, _ in where], plan, nt)
        self.j[gname] = (where, flight)
        return token

    def finish(self, gname, after):
        where, flight = self.j.pop(gname)
        nt = len(where)
        bufs = xchg_wait(flight, after)
        c = lax.axis_index("c")
        for t, (kind, l) in enumerate(where):
            self.J[kind] = lax.dynamic_update_slice(bufs[nt + t], bufs[t][None, None], (l, c, 0, 0))


def allreduce_small(v):
    R = v.shape[0]

    def body(v_ref, o_ref, all_ref, send, recv):
        x, y, c, _ = _place()
        me = 4 * x + 2 * y + c
        all_ref[me] = v_ref[...]
        cps = []
        for k in range(1, 8):
            cp = _copy(v_ref, all_ref.at[me], send, recv, k - 1, _flip(k, x, y, c))
            cp.start()
            cps.append(cp)
        for k in range(1, 8):
            px, py, pc = _flip(k, x, y, c)
            _copy(v_ref, all_ref.at[4 * px + 2 * py + pc], send, recv, k - 1, (px, py, pc)).wait_recv()
        for cp in cps:
            cp.wait_send()
        acc = all_ref[0]
        for d in range(1, 8):
            acc = acc + all_ref[d]
        o_ref[...] = acc

    return pl.pallas_call(
        body, name="allreduce_small",
        in_specs=[pl.BlockSpec(memory_space=pltpu.VMEM)], out_specs=pl.BlockSpec(memory_space=pltpu.VMEM),
        out_shape=jax.ShapeDtypeStruct((R, D), F32),
        scratch_shapes=[pltpu.VMEM((8, R, D), F32), pltpu.SemaphoreType.DMA((7,)), pltpu.SemaphoreType.DMA((7,))],
        compiler_params=pltpu.CompilerParams(has_side_effects=True, vmem_limit_bytes=VMEM_LIMIT),
    )(v)


AG_GROUPS = (("a0", ("pw1_0", "pw2_0", "small")), ("f0", ("up_0", "down_0")),
             ("l1", ("pw1_1", "pw2_1", "up_1", "down_1")), ("l2", ("kv", "wq_0", "wo_0", "up_2", "down_2")),
             ("l3", ("wq_1", "wo_1", "up_3", "down_3")))


def _bucket_table():
    qi = np.arange(BLK)[:, None]
    kj = np.arange(2 * BLK)[None, :]
    d = np.maximum(qi + BLK - kj, 0)
    max_exact = N_BUCKETS // 2
    log_ratio = (np.log(np.maximum(d, 1).astype(np.float32) / np.float32(max_exact))
                 / np.float32(math.log(MAX_DISTANCE / max_exact))).astype(np.float32)
    large = max_exact + (log_ratio * np.float32(N_BUCKETS - max_exact)).astype(np.int32)
    large = np.minimum(large, N_BUCKETS - 1)
    return np.where(d < max_exact, d, large).astype(np.int32)


def _heads_major(a, nh):
    T = a.shape[0]
    return a.reshape(T, nh, HD).transpose(1, 0, 2)


def _heads_minor(a):
    nh, T, _ = a.shape
    return a.transpose(1, 0, 2).reshape(T, nh * HD)


def _slots(land):
    return land.reshape(4, 2 * land.shape[2], land.shape[3])


def _rows(land):
    return land.reshape(8 * land.shape[2], land.shape[3])


def _gview(g):
    s, K, n = g.shape
    return g.reshape(4, 2, K // 2, n) if s == 4 else g.reshape(4, 2, K // 8, n)


def _gate(a, token):
    return a + token[0, 0]


def _conv_small(f_small):
    fs = f_small.transpose(1, 2, 0, 3).reshape(2, 40, D)
    b_pw1 = f_small[:, :, 35:37, :].transpose(1, 0, 2, 3).reshape(2, 1, 2 * D)
    rev = jnp.concatenate([fs[:, CONV_W - 1::-1], jnp.zeros((2, 40 - CONV_W, D), F32)], axis=1)
    return dict(conv=fs, conv_rev=rev, b_pw1=b_pw1, b_pw2=fs[:, 34:35])


def run_step(x, target, P, ag, rs):
    T = x.shape[0]
    zero = jnp.zeros((1, 1, D), F32)
    nm, nf = P["norm_mix"], P["norm_ffn"]
    ag.forward("a0", [ag.token])
    W = ag.get("a0", [])
    sm = _conv_small(W["small"])
    h = x
    saved = []
    for l in range(2):
        xn, u, a = norm_mm_glu(h, nm, l, _slots(W[f"pw1_{l}"]), sm["b_pw1"], f"f_pw1_{l}")
        y, s = dwconv_ln_silu(a, sm["conv"], l, f"f_conv_{l}")
        b2 = sm["b_pw2"]
        if l == 0:
            b2 = _gate(b2, ag.forward("f0", [s]))
        h1 = mm_bias_res(s, _rows(W[f"pw2_{l}"]), b2, l, h, f"f_pw2_{l}")
        if l == 0:
            W.update(ag.get("f0", [h1]))
        xn2, gu, f = norm_mm_swiglu(h1, nf, l, _slots(W[f"up_{l}"]), f"f_up_{l}")
        nxt = "l1" if l == 0 else "l2"
        h2 = mm_bias_res(f, _rows(W[f"down_{l}"]), _gate(zero, ag.forward(nxt, [f])), 0, h1, f"f_down_{l}")
        W.update(ag.get(nxt, [h2]))
        saved.append(dict(h=h, xn=xn, u=u, a=a, y=y, s=s, h1=h1, xn2=xn2, gu=gu, f=f))
        h = h2
    h_kv = h
    kvn, kv = norm_mm(h, P["norm_kv"], 0, _rows(W["kv"]), "f_kv")
    kp = jnp.pad(_heads_major(kv[:, :N_KV * HD], N_KV), ((0, 0), (BLK, 0), (0, 0)))
    vp = jnp.pad(_heads_major(kv[:, N_KV * HD:], N_KV), ((0, 0), (BLK, 0), (0, 0)))
    kvt = jnp.pad(kv.T.reshape(2, N_KV, HD, T), ((0, 0), (0, 0), (0, 0), (BLK, 0)))
    kt, vt = kvt[0], kvt[1]
    bucket = _bucket_table()
    onehot = jnp.asarray(np.eye(N_BUCKETS, dtype=np.float32)[bucket])
    bias = jnp.einsum("qkb,bh->hkq", onehot, P["rel_bias"], precision=lax.Precision.HIGHEST)
    bias = bias.reshape(N_KV, GROUP, 2 * BLK, BLK).transpose(0, 2, 1, 3).reshape(1, N_KV, 2 * BLK, QW)
    bias = bias + jnp.asarray(band_mask())[:, None]
    for j in range(2):
        l = 2 + j
        xn, q = norm_mm(h, nm, l, _rows(W[f"wq_{j}"]), f"f_q_{j}", scale=HD ** -0.5)
        qh = q.T.reshape(N_KV, GROUP, HD, T)
        sink = jnp.broadcast_to(P["sinks"][j].reshape(N_KV, GROUP, 1), (N_KV, GROUP, BLK)).reshape(N_KV, 1, QW)
        oh = attn_fwd(qh, kp, vt, bias, sink, f"f_attn_{j}")
        attn = oh.reshape(N_HEADS * HD, T).T
        h1 = mm_bias_res(attn, _rows(W[f"wo_{j}"]), zero, 0, h, f"f_wo_{j}")
        xn2, gu, f = norm_mm_swiglu(h1, nf, l, _slots(W[f"up_{l}"]), f"f_up_{l}")
        zg = _gate(zero, ag.forward("l3", [f])) if j == 0 else zero
        h2 = mm_bias_res(f, _rows(W[f"down_{l}"]), zg, 0, h1, f"f_down_{l}")
        if j == 0:
            W.update(ag.get("l3", [h2]))
        saved.append(dict(h=h, xn=xn, qh=qh, oh=oh, sink=sink, attn=attn, h1=h1, xn2=xn2, gu=gu, f=f))
        h = h2

    dh, st_final = final_loss(h, P["norm_final"], target, "loss_head")

    S = dict(norm_ffn=[None] * 4, norm_mix=[None] * 4, conv=[None] * 2, taps=[None] * 2, b_pw1=[None] * 2,
             b_pw2=[None] * 2, sinks=[None] * 2)

    def ffn_bwd(dh, sv, l, nf):
        du = mmT_swiglu_bwd(dh, _rows(W[f"down_{l}"]), sv["gu"], f"b_down_{l}")
        gd = mm_dw(sv["f"], dh, f"w_down_{l}", 512, 1)
        gu = mm_dw(sv["xn2"], du, f"w_up_{l}", DFF // 2, 4)
        dh, dg = mmT_rmsbwd(du, _slots(W[f"up_{l}"]), sv["h1"], nf, l, dh, f"b_up_{l}")
        S["norm_ffn"][l] = dg[0]
        return dh, {f"down_{l}": _gview(gd), f"up_{l}": _gview(gu)}

    dk = dv = dbias = None
    for j in (1, 0):
        l = 2 + j
        sv = saved[l]
        dh, grads = ffn_bwd(dh, sv, l, nf)
        dattn = mmT(dh, _rows(W[f"wo_{j}"]), f"b_wo_{j}")
        grads[f"wo_{j}"] = _gview(mm_dw(sv["attn"], dh, f"w_wo_{j}", 512, 1))
        doh = dattn.T.reshape(N_KV, GROUP, HD, T)
        dqh, dkj, dvj, dbj, dsj = attn_bwd(sv["qh"], kp, kt, vp, bias, sv["sink"], sv["oh"], doh, f"b_attn_{j}")
        dq = dqh.reshape(N_HEADS * HD, T).T
        grads[f"wq_{j}"] = _gview(mm_dw(sv["xn"], dq, f"w_q_{j}", 512, 1))
        dh, dg = mmT_rmsbwd(dq, _rows(W[f"wq_{j}"])[None], sv["h"], nm, l, dh, f"b_q_{j}")
        S["norm_mix"][l] = dg[0]
        S["sinks"][j] = jnp.sum(dsj.reshape(N_HEADS, BLK), axis=1)
        dk = dkj if dk is None else dk + dkj
        dv = dvj if dv is None else dv + dvj
        dbias = dbj if dbias is None else dbias + dbj
        if j == 1:
            nf = _gate(nf, rs.send("l3", grads))

    dkv = jnp.concatenate([_heads_minor(dk[:, BLK:]), _heads_minor(dv[:, BLK:])], axis=1).astype(BF16)
    grads["kv"] = _gview(mm_dw(kvn, dkv, "w_kv", 512, 1))
    dh, dg = mmT_rmsbwd(dkv, _rows(W["kv"])[None], h_kv, P["norm_kv"], 0, dh, "b_kv")
    S["norm_kv"] = dg[0]
    dbh = dbias.reshape(N_KV, 2 * BLK, GROUP, BLK)
    S["rel_bias"] = jnp.einsum("vkgq,qkb->bvg", dbh, onehot, precision=lax.Precision.HIGHEST).reshape(N_BUCKETS, N_HEADS)
    nf = _gate(nf, rs.send("l2", grads) + rs.reduce("l3", [dh]))

    for l in (1, 0):
        sv = saved[l]
        dh, grads = ffn_bwd(dh, sv, l, nf)
        conv = sm["conv"]
        if l == 0:
            conv = _gate(conv, rs.send("f0", grads))
            grads = {}
        dy, st = mmT_lnbwd(dh, _rows(W[f"pw2_{l}"]), sv["y"], conv, l, f"b_pw2_{l}")
        g2, S["b_pw2"][l] = mm_dw(sv["s"], dh, f"w_pw2_{l}", 512, 1, colsum=True)
        du, dtaps = dwconv_glu_bwd(dy, sv["a"], sv["u"], sm["conv"], sm["conv_rev"], l, f"b_conv_{l}")
        S["conv"][l] = st[0:3]
        S["taps"][l] = dtaps[0:CONV_W]
        if l == 0:
            rs.finish("l2", [du])
            nm = _gate(nm, rs.reduce("l1", [du]))
        g1, S["b_pw1"][l] = mm_dw(sv["xn"], du, f"w_pw1_{l}", 512, 4, colsum=True)
        grads[f"pw2_{l}"], grads[f"pw1_{l}"] = _gview(g2), _gview(g1)
        dh, dg = mmT_rmsbwd(du, _slots(W[f"pw1_{l}"]), sv["h"], nm, l, dh, f"b_pw1_{l}")
        S["norm_mix"][l] = dg[0]
        if l == 1:
            tok = rs.send("l1", grads)
            rs.finish("l3", [dh])
            nf = _gate(nf, tok + rs.reduce("l2", [dh]))
    token = rs.send("c0", grads)
    S["norm_final"] = st_final[0]
    S["loss"] = st_final[1]
    return token, dh, S


R_CONV = 37
R_SMALL = 88


def _pack_small(S):
    rows = []
    for l in range(2):
        rows += [S["taps"][l], S["conv"][l][2:3], S["conv"][l][0:2], S["b_pw2"][l], S["b_pw1"][l].reshape(2, D)]
    rows += [jnp.stack(S["norm_mix"]), jnp.stack(S["norm_ffn"]), S["norm_kv"][None], S["norm_final"][None]]
    tail = jnp.concatenate([jnp.stack(S["sinks"]).reshape(-1), S["rel_bias"].reshape(-1)])
    rows += [jnp.pad(tail, (0, D - tail.shape[0]))[None], S["loss"][None]]
    v = jnp.concatenate(rows, axis=0)
    return jnp.pad(v, ((0, R_SMALL - v.shape[0]), (0, 0)))


def kernel(x, norm_mix, norm_ffn, conv_w_pw1, conv_b_pw1, conv_w_dw, conv_b_dw, conv_ln_g, conv_ln_b, conv_w_pw2, conv_b_pw2, norm_kv, w_kv, w_q, w_o, sinks, rel_bias, ffn_w_up, ffn_w_down, norm_final, loss_target, m_norm_mix, m_norm_ffn, m_conv_w_pw1, m_conv_b_pw1, m_conv_w_dw, m_conv_b_dw, m_conv_ln_g, m_conv_ln_b, m_conv_w_pw2, m_conv_b_pw2, m_norm_kv, m_w_kv, m_w_q, m_w_o, m_sinks, m_rel_bias, m_ffn_w_up, m_ffn_w_down, m_norm_final, v_norm_mix, v_norm_ffn, v_conv_w_pw1, v_conv_b_pw1, v_conv_w_dw, v_conv_b_dw, v_conv_ln_g, v_conv_ln_b, v_conv_w_pw2, v_conv_b_pw2, v_norm_kv, v_w_kv, v_w_q, v_w_o, v_sinks, v_rel_bias, v_ffn_w_up, v_ffn_w_down, v_norm_final):
    me = 2 * lax.axis_index("x") + lax.axis_index("y")
    weights = dict(norm_mix=norm_mix, norm_ffn=norm_ffn, conv_w_pw1=conv_w_pw1, conv_b_pw1=conv_b_pw1,
                   conv_w_dw=conv_w_dw, conv_b_dw=conv_b_dw, conv_ln_g=conv_ln_g, conv_ln_b=conv_ln_b,
                   conv_w_pw2=conv_w_pw2, conv_b_pw2=conv_b_pw2, norm_kv=norm_kv, w_kv=w_kv, w_q=w_q, w_o=w_o,
                   sinks=sinks, rel_bias=rel_bias, ffn_w_up=ffn_w_up, ffn_w_down=ffn_w_down, norm_final=norm_final)
    mom_m = dict(norm_mix=m_norm_mix, norm_ffn=m_norm_ffn, conv_w_pw1=m_conv_w_pw1, conv_b_pw1=m_conv_b_pw1,
                 conv_w_dw=m_conv_w_dw, conv_b_dw=m_conv_b_dw, conv_ln_g=m_conv_ln_g, conv_ln_b=m_conv_ln_b,
                 conv_w_pw2=m_conv_w_pw2, conv_b_pw2=m_conv_b_pw2, norm_kv=m_norm_kv, w_kv=m_w_kv, w_q=m_w_q,
                 w_o=m_w_o, sinks=m_sinks, rel_bias=m_rel_bias, ffn_w_up=m_ffn_w_up, ffn_w_down=m_ffn_w_down,
                 norm_final=m_norm_final)
    mom_v = dict(norm_mix=v_norm_mix, norm_ffn=v_norm_ffn, conv_w_pw1=v_conv_w_pw1, conv_b_pw1=v_conv_b_pw1,
                 conv_w_dw=v_conv_w_dw, conv_b_dw=v_conv_b_dw, conv_ln_g=v_conv_ln_g, conv_ln_b=v_conv_ln_b,
                 conv_w_pw2=v_conv_w_pw2, conv_b_pw2=v_conv_b_pw2, norm_kv=v_norm_kv, w_kv=v_w_kv, w_q=v_w_q,
                 w_o=v_w_o, sinks=v_sinks, rel_bias=v_rel_bias, ffn_w_up=v_ffn_w_up, ffn_w_down=v_ffn_w_down,
                 norm_final=v_norm_final)

    def halves(a):
        return a.astype(BF16).reshape(2, a.shape[0] // 2, a.shape[1])

    shards = {"kv": halves(w_kv)}
    for l in range(2):
        shards[f"pw1_{l}"], shards[f"pw2_{l}"] = halves(conv_w_pw1[l]), halves(conv_w_pw2[l])
        shards[f"wq_{l}"], shards[f"wo_{l}"] = halves(w_q[l]), halves(w_o[l])
    for l in range(4):
        shards[f"up_{l}"], shards[f"down_{l}"] = halves(ffn_w_up[l]), halves(ffn_w_down[l])
    shards["small"] = jnp.concatenate(
        [conv_w_dw, conv_b_dw[:, None], conv_ln_g[:, None], conv_ln_b[:, None], conv_b_pw2[:, None],
         conv_b_pw1.reshape(2, 2, 256), jnp.zeros((2, 3, 256), F32)], axis=1)
    ag = WeightGather(shards, AG_GROUPS)
    big = {"conv_w_pw1": "pw1", "conv_w_pw2": "pw2", "w_q": "wq", "w_o": "wo", "ffn_w_up": "up",
           "ffn_w_down": "down", "w_kv": "kv"}
    rs = GradReduce({"pw1": (2, 512, 512), "pw2": (2, 128, D), "wq": (2, 128, D), "wo": (2, 128, D),
                     "up": (4, 512, DFF // 2), "down": (4, DFF // 8, D), "kv": (1, 128, 512)})

    P = dict(norm_mix=norm_mix[:, None], norm_ffn=norm_ffn[:, None], norm_kv=norm_kv[None, None],
             norm_final=norm_final[None], sinks=sinks, rel_bias=rel_bias)
    token, grad_x, S = run_step(x[0], loss_target[0], P, ag, rs)

    rs.finish("l1", [grad_x])
    vsum = allreduce_small(_gate(_pack_small(S), token + rs.reduce("f0", [grad_x])))
    col = lambda a: lax.dynamic_slice_in_dim(a, me * 256, 256, axis=-1)
    grads = {}
    for l in range(2):
        base = l * R_CONV
        grads.setdefault("conv_w_dw", []).append(col(vsum[base:base + 31]))
        grads.setdefault("conv_b_dw", []).append(col(vsum[base + 31]))
        grads.setdefault("conv_ln_g", []).append(col(vsum[base + 32]))
        grads.setdefault("conv_ln_b", []).append(col(vsum[base + 33]))
        grads.setdefault("conv_b_pw2", []).append(col(vsum[base + 34]))
        grads.setdefault("conv_b_pw1", []).append(
            lax.dynamic_slice_in_dim(vsum[base + 35:base + 37].reshape(2 * D), me * 512, 512, axis=0))
    grads = {k: jnp.stack(v) for k, v in grads.items()}
    base = 2 * R_CONV
    grads["norm_mix"] = vsum[base:base + 4]
    grads["norm_ffn"] = vsum[base + 4:base + 8]
    grads["norm_kv"] = vsum[base + 8]
    grads["norm_final"] = vsum[base + 9]
    grads["sinks"] = vsum[base + 10, 0:32].reshape(2, 16)
    grads["rel_bias"] = vsum[base + 10, 32:32 + 512].reshape(32, 16)
    loss = vsum[base + 11, 0]

    delta, new_m, new_v = {}, {}, {}
    rest = [n for n in weights if n not in big]

    def pack(dct):
        flat = jnp.concatenate([dct[n].reshape(-1) for n in rest])
        return jnp.pad(flat, (0, (-flat.shape[0]) % (8 * 128))).reshape(-1, 128)

    d, nm, nv = adamw(pack(weights), pack(grads), pack(mom_m), pack(mom_v), "adamw_small")
    off = 0
    for n in rest:
        shp = weights[n].shape
        sz = int(np.prod(shp))
        delta[n] = d.reshape(-1)[off:off + sz].reshape(shp)
        new_m[n] = nm.reshape(-1)[off:off + sz].reshape(shp)
        new_v[n] = nv.reshape(-1)[off:off + sz].reshape(shp)
        off += sz

    def update(n):
        shp = weights[n].shape
        r2 = (int(np.prod(shp[:-1])), shp[-1])
        grads[n] = rs.J[big[n]].reshape(shp)
        d, nm, nv = adamw(weights[n].reshape(r2), grads[n].reshape(r2), mom_m[n].reshape(r2), mom_v[n].reshape(r2),
                          f"adamw_{n}")
        delta[n], new_m[n], new_v[n] = d.reshape(shp), nm.reshape(shp), nv.reshape(shp)

    rs.finish("f0", [vsum])
    for n in ("ffn_w_up", "ffn_w_down"):
        update(n)
    rs.reduce("c0", [delta["ffn_w_down"]])
    for n in ("w_q", "w_o", "w_kv"):
        update(n)
    rs.finish("c0", [delta["w_kv"]])
    for n in ("conv_w_pw1", "conv_w_pw2"):
        update(n)

    order = list(weights)
    return (loss, grad_x[None], *[grads[n] for n in order], *[delta[n] for n in order],
            *[new_m[n] for n in order], *[new_v[n] for n in order])
```

```python
import functools
import math

import numpy as np
import jax
import jax.numpy as jnp
from jax import lax
from jax.experimental import pallas as pl
from jax.experimental.pallas import tpu as pltpu

F32 = jnp.float32
BF16 = jnp.bfloat16
MESH = pl.DeviceIdType.MESH

D = 1024
DFF = 2816
N_HEADS = 16
N_KV = 4
GROUP = 4
HD = 64
BLK = 128
CONV_W = 31
HALO = 32
N_BUCKETS = 32
MAX_DISTANCE = 128
EPS = 1e-6
NEG_INF = -1e30
TM = 256
VMEM_LIMIT = 56 * 2 ** 20

ADAM_LR, ADAM_B1, ADAM_B2, ADAM_EPS, ADAM_WD, ADAM_STEP = 0.001, 0.9, 0.999, 1e-08, 0.01, 10


def _cp(*sem):
    return pltpu.CompilerParams(dimension_semantics=sem, vmem_limit_bytes=VMEM_LIMIT)


def _sigmoid(x):
    return 1.0 / (1.0 + jnp.exp(-x))


def _row(tm, n):
    return pl.BlockSpec((tm, n), lambda i: (i, 0))


def _const(shape):
    nd = len(shape)
    return pl.BlockSpec(shape, lambda i: (0,) * nd)


def _layer(shape, l):
    nd = len(shape)
    return pl.BlockSpec((None,) + tuple(shape), lambda i: (l,) + (0,) * nd)


def _dot(a, b):
    return jnp.dot(a, b, preferred_element_type=F32)


def _dot_nt(a, b):
    return lax.dot_general(a, b, (((1,), (1,)), ((), ())), preferred_element_type=F32)


def _dot_tn(a, b):
    return lax.dot_general(a, b, (((0,), (0,)), ((), ())), preferred_element_type=F32)


def _rms(x):
    return lax.rsqrt(jnp.mean(x * x, axis=-1, keepdims=True) + EPS)


def norm_mm_glu(h, g, l, w, b, name):
    T = h.shape[0]
    ns = w.shape[-1]

    def body(h_ref, g_ref, w_ref, b_ref, xn_ref, u_ref, a_ref):
        x = h_ref[...]
        xn = (x * _rms(x) * g_ref[...]).astype(BF16)
        xn_ref[...] = xn
        for s in range(2):
            lo, hi = s * ns, (s + 1) * ns
            u1 = _dot(xn, w_ref[s]) + b_ref[:, lo:hi]
            u2 = _dot(xn, w_ref[2 + s]) + b_ref[:, D + lo:D + hi]
            u_ref[:, lo:hi] = u1.astype(BF16)
            u_ref[:, D + lo:D + hi] = u2.astype(BF16)
            a_ref[:, lo:hi] = u1 * _sigmoid(u2)

    return pl.pallas_call(
        body, name=name, grid=(T // TM,),
        in_specs=[_row(TM, D), _layer((1, D), l), _const((4, D, ns)), _layer((1, 2 * D), l)],
        out_specs=[_row(TM, D), _row(TM, 2 * D), _row(TM, D)],
        out_shape=[jax.ShapeDtypeStruct((T, D), BF16), jax.ShapeDtypeStruct((T, 2 * D), BF16),
                   jax.ShapeDtypeStruct((T, D), F32)],
        compiler_params=_cp("parallel"),
    )(h, g, w, b)


SUB = 8


def _make_shifts(sh):
    n = TM + HALO - SUB
    for r in range(1, SUB):
        for r0 in range(0, n, 40):
            sh[r, r0:r0 + 40, :] = sh[0, pl.ds(r + r0, 40), :]


def _shifted(sh, off, rows, cols):
    return sh[off % SUB, pl.ds(off - off % SUB, rows), cols]


def _conv_taps(sh, w_ref, out_ref, first):
    RB, LB = 32, 512
    for r0 in range(0, TM, RB):
        for c0 in range(0, D, LB):
            acc = jnp.zeros((RB, LB), F32)
            for k in range(CONV_W):
                acc = acc + w_ref[k:k + 1, c0:c0 + LB] * _shifted(sh, first + k + r0, RB, slice(c0, c0 + LB))
            out_ref[r0:r0 + RB, c0:c0 + LB] = acc


def dwconv_ln_silu(a, sm, l, name):
    T = a.shape[0]
    nb = TM // HALO

    def body(cur_ref, prev_ref, sm_ref, y_ref, s_ref, sh):
        i = pl.program_id(0)
        sh[0, 0:HALO, :] = jnp.where(i > 0, prev_ref[...], 0.0)
        sh[0, HALO:HALO + TM, :] = cur_ref[...]
        _make_shifts(sh)
        _conv_taps(sh, sm_ref, y_ref, HALO - (CONV_W - 1))
        y = y_ref[...] + sm_ref[31:32, :]
        y_ref[...] = y
        mu = jnp.mean(y, axis=-1, keepdims=True)
        yc = y - mu
        rstd = lax.rsqrt(jnp.mean(yc * yc, axis=-1, keepdims=True) + EPS)
        z = yc * rstd * sm_ref[32:33, :] + sm_ref[33:34, :]
        s_ref[...] = (z * _sigmoid(z)).astype(BF16)

    return pl.pallas_call(
        body, name=name, grid=(T // TM,),
        in_specs=[_row(TM, D), pl.BlockSpec((HALO, D), lambda i: (jnp.maximum(i * nb - 1, 0), 0)),
                  _layer((40, D), l)],
        out_specs=[_row(TM, D), _row(TM, D)],
        out_shape=[jax.ShapeDtypeStruct((T, D), F32), jax.ShapeDtypeStruct((T, D), BF16)],
        scratch_shapes=[pltpu.VMEM((SUB, TM + HALO, D), F32)],
        compiler_params=_cp("parallel"),
    )(a, a, sm)


def mm_bias_res(xb, w, b, bl, res, name):
    T, K = xb.shape

    def body(x_ref, w_ref, b_ref, r_ref, o_ref):
        o_ref[...] = _dot(x_ref[...], w_ref[...]) + b_ref[...] + r_ref[...]

    return pl.pallas_call(
        body, name=name, grid=(T // TM,),
        in_specs=[_row(TM, K), _const((K, D)), _layer((1, D), bl), _row(TM, D)],
        out_specs=_row(TM, D), out_shape=jax.ShapeDtypeStruct((T, D), F32),
        compiler_params=_cp("parallel"),
    )(xb, w, b, res)


def norm_mm_swiglu(h, g, l, w, name):
    T = h.shape[0]
    ns = w.shape[-1]

    def body(h_ref, g_ref, w_ref, xn_ref, gu_ref, f_ref):
        x = h_ref[...]
        xn = (x * _rms(x) * g_ref[...]).astype(BF16)
        xn_ref[...] = xn
        for s in range(2):
            lo, hi = s * ns, (s + 1) * ns
            gate = _dot(xn, w_ref[s])
            up = _dot(xn, w_ref[2 + s])
            gu_ref[:, lo:hi] = gate.astype(BF16)
            gu_ref[:, DFF + lo:DFF + hi] = up.astype(BF16)
            f_ref[:, lo:hi] = (gate * _sigmoid(gate) * up).astype(BF16)

    return pl.pallas_call(
        body, name=name, grid=(T // TM,),
        in_specs=[_row(TM, D), _layer((1, D), l), _const((4, D, ns))],
        out_specs=[_row(TM, D), _row(TM, 2 * DFF), _row(TM, DFF)],
        out_shape=[jax.ShapeDtypeStruct((T, D), BF16), jax.ShapeDtypeStruct((T, 2 * DFF), BF16),
                   jax.ShapeDtypeStruct((T, DFF), BF16)],
        compiler_params=_cp("parallel"),
    )(h, g, w)


def norm_mm(h, g, gl, w, name, scale=1.0):
    T = h.shape[0]
    N = w.shape[-1]

    def body(h_ref, g_ref, w_ref, xn_ref, o_ref):
        x = h_ref[...]
        xn = (x * _rms(x) * g_ref[...]).astype(BF16)
        xn_ref[...] = xn
        o_ref[...] = (_dot(xn, w_ref[...]) * scale).astype(BF16)

    return pl.pallas_call(
        body, name=name, grid=(T // TM,),
        in_specs=[_row(TM, D), _layer((1, D), gl), _const((D, N))],
        out_specs=[_row(TM, D), _row(TM, N)],
        out_shape=[jax.ShapeDtypeStruct((T, D), BF16), jax.ShapeDtypeStruct((T, N), BF16)],
        compiler_params=_cp("parallel"),
    )(h, g, w)


QB = 4
QW = GROUP * BLK


def band_mask():
    qi = np.arange(QW)[None, :] % BLK
    kj = np.arange(2 * BLK)[:, None]
    band = ((kj < BLK) & (kj > qi)) | ((kj >= BLK) & (kj - BLK <= qi))
    first = band & (kj >= BLK)
    return np.where(np.stack([first, band]), 0.0, NEG_INF).astype(np.float32)


def _softmax_cols(s, sink):
    m = jnp.maximum(jnp.max(s, axis=0, keepdims=True), sink)
    p = jnp.exp(s - m)
    es = jnp.exp(sink - m)
    inv = 1.0 / (jnp.sum(p, axis=0, keepdims=True) + es)
    return p, inv, es


def _attn_specs(T):
    W = QB * BLK
    qspec = pl.BlockSpec((None, GROUP, HD, W), lambda kv, n: (kv, 0, 0, n))
    kspec = pl.BlockSpec((None, T + BLK, HD), lambda kv, n: (kv, 0, 0))
    ktspec = [pl.BlockSpec((None, HD, W), lambda kv, n: (kv, 0, n)),
              pl.BlockSpec((None, HD, BLK), lambda kv, n: (kv, 0, (n + 1) * QB))]
    bspec = pl.BlockSpec((2, None, 2 * BLK, QW), lambda kv, n: (0, kv, 0, 0))
    sspec = pl.BlockSpec((None, 1, QW), lambda kv, n: (kv, 0, 0))
    return qspec, kspec, ktspec, bspec, sspec


def _attn_block(n, b):
    blk = n * QB + b
    rows = pl.ds(pl.multiple_of(blk * BLK, BLK), 2 * BLK)
    return rows, (jnp.minimum(blk, 1) if b == 0 else 1)


def _band_cols(main_ref, tail_ref, b):
    if b < QB - 1:
        return main_ref[:, b * BLK:(b + 2) * BLK]
    return jnp.concatenate([main_ref[:, b * BLK:], tail_ref[...]], axis=1)


def _heads_side_by_side(ref, qs):
    return jnp.concatenate([ref[g, :, qs] for g in range(GROUP)], axis=1)


def attn_fwd(q, kp, vt, bias, sink, name):
    T = q.shape[3]
    qspec, kspec, ktspec, bspec, sspec = _attn_specs(T)

    def body(q_ref, k_ref, vt_ref, vtt_ref, b_ref, s_ref, o_ref, pb):
        n = pl.program_id(1)
        for b in range(QB):
            rows, table = _attn_block(n, b)
            qs = slice(b * BLK, (b + 1) * BLK)
            st = _dot(k_ref[rows, :], _heads_side_by_side(q_ref, qs))
            for g in range(GROUP):
                hs = slice(g * BLK, (g + 1) * BLK)
                p, inv, _ = _softmax_cols(st[:, hs] + b_ref[table, :, hs], s_ref[:, hs])
                pb[:, hs] = (p * inv).astype(BF16)
            ot = _dot(_band_cols(vt_ref, vtt_ref, b), pb[...])
            for g in range(GROUP):
                o_ref[g, :, qs] = ot[:, g * BLK:(g + 1) * BLK].astype(BF16)

    return pl.pallas_call(
        body, name=name, grid=(N_KV, T // (QB * BLK)),
        in_specs=[qspec, kspec, *ktspec, bspec, sspec], out_specs=qspec,
        out_shape=jax.ShapeDtypeStruct((N_KV, GROUP, HD, T), BF16),
        scratch_shapes=[pltpu.VMEM((2 * BLK, QW), BF16)],
        compiler_params=_cp("parallel", "parallel"),
    )(q, kp, vt, vt, bias, sink)


def attn_bwd(q, kp, kt, vp, bias, sink, o, do, name):
    T = q.shape[3]
    qspec, kspec, ktspec, bspec, sspec = _attn_specs(T)

    def body(q_ref, k_ref, kt_ref, ktt_ref, v_ref, b_ref, s_ref, o_ref, do_ref,
             dq_ref, dk_ref, dv_ref, db_ref, ds_ref, pb, dsb):
        n = pl.program_id(1)

        @pl.when(n == 0)
        def _():
            dk_ref[...] = jnp.zeros_like(dk_ref)
            dv_ref[...] = jnp.zeros_like(dv_ref)
            db_ref[...] = jnp.zeros_like(db_ref)
            ds_ref[...] = jnp.zeros_like(ds_ref)

        for b in range(QB):
            rows, table = _attn_block(n, b)
            qs = slice(b * BLK, (b + 1) * BLK)
            q4 = _heads_side_by_side(q_ref, qs)
            do4 = _heads_side_by_side(do_ref, qs)
            st = _dot(k_ref[rows, :], q4)
            dpt = _dot(v_ref[rows, :], do4)
            for g in range(GROUP):
                hs = slice(g * BLK, (g + 1) * BLK)
                p, inv, es = _softmax_cols(st[:, hs] + b_ref[table, :, hs], s_ref[:, hs])
                probs = p * inv
                delta = jnp.sum(do_ref[g, :, qs].astype(F32) * o_ref[g, :, qs].astype(F32), axis=0, keepdims=True)
                dS = probs * (dpt[:, hs] - delta)
                ds_ref[:, hs] += -(es * inv) * delta
                db_ref[:, hs] += dS
                pb[:, hs] = probs.astype(BF16)
                dsb[:, hs] = dS.astype(BF16)
            dqt = _dot(_band_cols(kt_ref, ktt_ref, b), dsb[...]) * (HD ** -0.5)
            for g in range(GROUP):
                dq_ref[g, :, qs] = dqt[:, g * BLK:(g + 1) * BLK].astype(BF16)
            dk_ref[rows, :] += _dot_nt(dsb[...], q4)
            dv_ref[rows, :] += _dot_nt(pb[...], do4)

    kout = pl.BlockSpec((None, T + BLK, HD), lambda kv, n: (kv, 0, 0))
    dbspec = pl.BlockSpec((None, 2 * BLK, QW), lambda kv, n: (kv, 0, 0))
    return pl.pallas_call(
        body, name=name, grid=(N_KV, T // (QB * BLK)),
        in_specs=[qspec, kspec, *ktspec, kspec, bspec, sspec, qspec, qspec],
        out_specs=[qspec, kout, kout, dbspec, sspec],
        out_shape=[jax.ShapeDtypeStruct((N_KV, GROUP, HD, T), BF16),
                   jax.ShapeDtypeStruct((N_KV, T + BLK, HD), F32), jax.ShapeDtypeStruct((N_KV, T + BLK, HD), F32),
                   jax.ShapeDtypeStruct((N_KV, 2 * BLK, QW), F32), jax.ShapeDtypeStruct((N_KV, 1, QW), F32)],
        scratch_shapes=[pltpu.VMEM((2 * BLK, QW), BF16), pltpu.VMEM((2 * BLK, QW), BF16)],
        compiler_params=_cp("parallel", "arbitrary"),
    )(q, kp, kt, kt, vp, bias, sink, o, do)


def final_loss(h, g, target, name):
    T = h.shape[0]

    def body(h_ref, g_ref, t_ref, dh_ref, st_ref):
        i = pl.program_id(0)

        @pl.when(i == 0)
        def _():
            st_ref[...] = jnp.zeros_like(st_ref)

        x = h_ref[...]
        r = _rms(x)
        xh = x * r
        e = xh * g_ref[...] - t_ref[...]
        loss = 0.5 * jnp.sum(jnp.mean(e * e, axis=-1, keepdims=True))
        dy = e * (1.0 / D)
        st_ref[0:1, :] += jnp.sum(dy * xh, axis=0, keepdims=True)
        lane = lax.broadcasted_iota(jnp.int32, (1, D), 1)
        st_ref[1:2, :] += jnp.where(lane == 0, loss, 0.0)
        dxh = dy * g_ref[...]
        dh_ref[...] = r * (dxh - xh * jnp.mean(dxh * xh, axis=-1, keepdims=True))

    return pl.pallas_call(
        body, name=name, grid=(T // TM,),
        in_specs=[_row(TM, D), _const((1, D)), _row(TM, D)],
        out_specs=[_row(TM, D), _const((8, D))],
        out_shape=[jax.ShapeDtypeStruct((T, D), F32), jax.ShapeDtypeStruct((8, D), F32)],
        compiler_params=_cp("arbitrary"),
    )(h, g, target)


def mm_dw(x, dy, name, tn, slots, colsum=False):
    T, K = x.shape
    N = dy.shape[1]
    tt = min(T, 1024)
    nt = T // tt
    ns = N // slots
    per = ns // tn

    def body(x_ref, dy_ref, *rest):
        if colsum:
            dw_ref, cs_ref, acc, cacc = rest
        else:
            dw_ref, acc = rest
        t = pl.program_id(1)

        @pl.when(t == 0)
        def _():
            acc[...] = jnp.zeros_like(acc)
            if colsum:
                cacc[...] = jnp.zeros_like(cacc)

        dyv = dy_ref[...]
        acc[...] += _dot_tn(x_ref[...].astype(BF16), dyv.astype(BF16))
        if colsum:
            cacc[...] += jnp.sum(dyv.astype(F32), axis=0, keepdims=True)

        @pl.when(t == nt - 1)
        def _():
            dw_ref[...] = acc[...].astype(BF16)
            if colsum:
                cs_ref[...] = cacc[...]

    out_specs = [pl.BlockSpec((None, K, tn), lambda j, t: (j // per, 0, j % per))]
    out_shape = [jax.ShapeDtypeStruct((slots, K, ns), BF16)]
    scratch = [pltpu.VMEM((K, tn), F32)]
    if colsum:
        out_specs.append(pl.BlockSpec((1, tn), lambda j, t: (0, j)))
        out_shape.append(jax.ShapeDtypeStruct((1, N), F32))
        scratch.append(pltpu.VMEM((1, tn), F32))
    res = pl.pallas_call(
        body, name=name, grid=(N // tn, nt),
        in_specs=[pl.BlockSpec((tt, K), lambda j, t: (t, 0)), pl.BlockSpec((tt, tn), lambda j, t: (t, j))],
        out_specs=out_specs, out_shape=out_shape, scratch_shapes=scratch,
        compiler_params=_cp("parallel", "arbitrary"),
    )(x, dy)
    return tuple(res) if colsum else res[0]


def mmT_swiglu_bwd(dh, w, gu, name, after=()):
    T = dh.shape[0]
    half = DFF // 2

    def body(dh_ref, w_ref, gu_ref, *rest):
        du_ref = rest[-1]
        dhb = dh_ref[...].astype(BF16)
        for s in range(2):
            lo, hi = s * half, (s + 1) * half
            df = _dot_nt(dhb, w_ref[lo:hi, :])
            gate = gu_ref[:, lo:hi].astype(F32)
            up = gu_ref[:, DFF + lo:DFF + hi].astype(F32)
            sg = _sigmoid(gate)
            du_ref[:, lo:hi] = (df * up * sg * (1.0 + gate * (1.0 - sg))).astype(BF16)
            du_ref[:, DFF + lo:DFF + hi] = (df * gate * sg).astype(BF16)

    return pl.pallas_call(
        body, name=name, grid=(T // TM,),
        in_specs=[_row(TM, D), _const((DFF, D)), _row(TM, 2 * DFF)] + [ANY] * len(after),
        out_specs=_row(TM, 2 * DFF), out_shape=jax.ShapeDtypeStruct((T, 2 * DFF), BF16),
        compiler_params=_cp("parallel"),
    )(dh, w, gu, *after)


def mmT_rmsbwd(du, w, h, g, gl, dh_in, name):
    T, N = du.shape
    slots = w.shape[0]
    ns = N // slots

    def body(du_ref, w_ref, h_ref, g_ref, di_ref, dh_ref, dg_ref):
        i = pl.program_id(0)

        @pl.when(i == 0)
        def _():
            dg_ref[...] = jnp.zeros_like(dg_ref)

        dxn = _dot_nt(du_ref[:, 0:ns], w_ref[0])
        for s in range(1, slots):
            dxn = dxn + _dot_nt(du_ref[:, s * ns:(s + 1) * ns], w_ref[s])
        x = h_ref[...]
        r = _rms(x)
        xh = x * r
        dg_ref[0:1, :] += jnp.sum(dxn * xh, axis=0, keepdims=True)
        dxh = dxn * g_ref[...]
        dh_ref[...] = di_ref[...] + r * (dxh - xh * jnp.mean(dxh * xh, axis=-1, keepdims=True))

    return pl.pallas_call(
        body, name=name, grid=(T // TM,),
        in_specs=[_row(TM, N), _const((slots, D, ns)), _row(TM, D), _layer((1, D), gl), _row(TM, D)],
        out_specs=[_row(TM, D), _const((8, D))],
        out_shape=[jax.ShapeDtypeStruct((T, D), F32), jax.ShapeDtypeStruct((8, D), F32)],
        compiler_params=_cp("arbitrary"),
    )(du, w, h, g, dh_in)


def mmT(dh, w, name):
    T = dh.shape[0]
    N = w.shape[0]

    def body(dh_ref, w_ref, o_ref):
        o_ref[...] = _dot_nt(dh_ref[...].astype(BF16), w_ref[...]).astype(BF16)

    return pl.pallas_call(
        body, name=name, grid=(T // TM,),
        in_specs=[_row(TM, D), _const((N, D))],
        out_specs=_row(TM, N), out_shape=jax.ShapeDtypeStruct((T, N), BF16),
        compiler_params=_cp("parallel"),
    )(dh, w)


def mmT_lnbwd(dh, w, y, sm, l, name):
    T = dh.shape[0]

    def body(dh_ref, w_ref, y_ref, sm_ref, dy_ref, st_ref):
        i = pl.program_id(0)

        @pl.when(i == 0)
        def _():
            st_ref[...] = jnp.zeros_like(st_ref)

        ds = _dot_nt(dh_ref[...].astype(BF16), w_ref[...])
        y = y_ref[...]
        mu = jnp.mean(y, axis=-1, keepdims=True)
        yc = y - mu
        rstd = lax.rsqrt(jnp.mean(yc * yc, axis=-1, keepdims=True) + EPS)
        xh = yc * rstd
        gam = sm_ref[32:33, :]
        z = xh * gam + sm_ref[33:34, :]
        sg = _sigmoid(z)
        dz = ds * sg * (1.0 + z * (1.0 - sg))
        st_ref[0:1, :] += jnp.sum(dz * xh, axis=0, keepdims=True)
        st_ref[1:2, :] += jnp.sum(dz, axis=0, keepdims=True)
        dxh = dz * gam
        dy = rstd * (dxh - jnp.mean(dxh, axis=-1, keepdims=True) - xh * jnp.mean(dxh * xh, axis=-1, keepdims=True))
        st_ref[2:3, :] += jnp.sum(dy, axis=0, keepdims=True)
        dy_ref[...] = dy

    return pl.pallas_call(
        body, name=name, grid=(T // TM,),
        in_specs=[_row(TM, D), _const((D, D)), _row(TM, D), _layer((40, D), l)],
        out_specs=[_row(TM, D), _const((8, D))],
        out_shape=[jax.ShapeDtypeStruct((T, D), F32), jax.ShapeDtypeStruct((8, D), F32)],
        compiler_params=_cp("arbitrary"),
    )(dh, w, y, sm)


def dwconv_glu_bwd(dy, a, u, sm, smrev, l, name):
    T = dy.shape[0]
    nb = TM // HALO
    last = T // HALO - 1

    def body(dy_ref, dyn_ref, a_ref, ap_ref, u_ref, sm_ref, rev_ref, du_ref, dw_ref, shd, sha, da):
        i = pl.program_id(0)

        @pl.when(i == 0)
        def _():
            dw_ref[...] = jnp.zeros_like(dw_ref)

        shd[0, 0:TM, :] = dy_ref[...]
        shd[0, TM:TM + HALO, :] = jnp.where(i < pl.num_programs(0) - 1, dyn_ref[...], 0.0)
        sha[0, 0:HALO, :] = jnp.where(i > 0, ap_ref[...], 0.0)
        sha[0, HALO:HALO + TM, :] = a_ref[...]
        _make_shifts(shd)
        _make_shifts(sha)
        _conv_taps(shd, rev_ref, da, 0)
        LB = 512
        for c0 in range(0, D, LB):
            for k in range(CONV_W):
                acc = jnp.zeros((SUB, LB), F32)
                for r0 in range(0, TM, SUB):
                    acc = acc + dy_ref[r0:r0 + SUB, c0:c0 + LB] * _shifted(sha, HALO - (CONV_W - 1) + k + r0, SUB,
                                                                           slice(c0, c0 + LB))
                dw_ref[k:k + 1, c0:c0 + LB] += jnp.sum(acc, axis=0, keepdims=True)
        dav = da[...]
        u1 = u_ref[:, 0:D].astype(F32)
        sg = _sigmoid(u_ref[:, D:2 * D].astype(F32))
        du_ref[:, 0:D] = (dav * sg).astype(BF16)
        du_ref[:, D:2 * D] = (dav * u1 * sg * (1.0 - sg)).astype(BF16)

    return pl.pallas_call(
        body, name=name, grid=(T // TM,),
        in_specs=[_row(TM, D), pl.BlockSpec((HALO, D), lambda i: (jnp.minimum((i + 1) * nb, last), 0)),
                  _row(TM, D), pl.BlockSpec((HALO, D), lambda i: (jnp.maximum(i * nb - 1, 0), 0)),
                  _row(TM, 2 * D), _layer((40, D), l), _layer((40, D), l)],
        out_specs=[_row(TM, 2 * D), _const((32, D))],
        out_shape=[jax.ShapeDtypeStruct((T, 2 * D), BF16), jax.ShapeDtypeStruct((32, D), F32)],
        scratch_shapes=[pltpu.VMEM((SUB, TM + HALO, D), F32), pltpu.VMEM((SUB, TM + HALO, D), F32),
                        pltpu.VMEM((TM, D), F32)],
        compiler_params=_cp("arbitrary"),
    )(dy, dy, a, a, u, sm, smrev)


def _rows_tile(R):
    for t in (512, 256, 128, 64, 32, 16, 8):
        if R % t == 0:
            return t
    return R


def add8(own, others, name):
    R, C = own.shape
    tr = _rows_tile(R)

    def body(o_ref, x_ref, out_ref):
        acc = o_ref[...].astype(F32)
        for k in range(7):
            acc = acc + x_ref[k].astype(F32)
        out_ref[...] = acc

    return pl.pallas_call(
        body, name=name, grid=(R // tr,),
        in_specs=[_row(tr, C), pl.BlockSpec((7, tr, C), lambda i: (0, i, 0))], out_specs=_row(tr, C),
        out_shape=jax.ShapeDtypeStruct((R, C), F32), compiler_params=_cp("parallel"),
    )(own, others)


def adamw(w, g, m, v, name):
    R, C = w.shape
    tr = _rows_tile(R)

    def body(w_ref, g_ref, m_ref, v_ref, d_ref, nm_ref, nv_ref):
        gv = g_ref[...]
        nm = ADAM_B1 * m_ref[...] + (1.0 - ADAM_B1) * gv
        nv = ADAM_B2 * v_ref[...] + (1.0 - ADAM_B2) * (gv * gv)
        m_hat = nm / (1.0 - ADAM_B1 ** ADAM_STEP)
        v_hat = nv / (1.0 - ADAM_B2 ** ADAM_STEP)
        d_ref[...] = -ADAM_LR * (m_hat / (jnp.sqrt(v_hat) + ADAM_EPS) + ADAM_WD * w_ref[...])
        nm_ref[...] = nm
        nv_ref[...] = nv

    sd = jax.ShapeDtypeStruct((R, C), F32)
    return pl.pallas_call(
        body, name=name, grid=(R // tr,),
        in_specs=[_row(tr, C)] * 4, out_specs=[_row(tr, C)] * 3, out_shape=[sd, sd, sd],
        compiler_params=_cp("parallel"),
    )(w, g, m, v)


ANY = pl.BlockSpec(memory_space=pl.ANY)
HBM = pl.BlockSpec(memory_space=pltpu.HBM)
SEM = pl.BlockSpec(memory_space=pltpu.SEMAPHORE)
EFFECT = pltpu.SideEffectType.DATAFLOW_SIDE_EFFECTING


def _place():
    x, y, c = lax.axis_index("x"), lax.axis_index("y"), lax.axis_index("c")
    chips = [(1 - x, y), (x, 1 - y), (1 - x, 1 - y)]
    return x, y, c, chips


def _copy(src, dst, send, recv, k, to):
    return pltpu.make_async_remote_copy(src_ref=src, dst_ref=dst, send_sem=send.at[k], recv_sem=recv.at[k],
                                        device_id=to, device_id_type=MESH)


def xchg_start(name, bufs, plan, n, after=()):
    nb = len(bufs)

    na = len(after)

    def body(*refs):
        send, recv, token = refs[nb + na], refs[nb + na + 1], refs[-1]
        for k, (src, dst, to) in enumerate(plan(refs[:nb])):
            _copy(src, dst, send, recv, k, to).start()
        token[...] = jnp.zeros_like(token)

    outs = pl.pallas_call(
        body, name=name,
        out_shape=(pltpu.SemaphoreType.DMA((n,)), pltpu.SemaphoreType.DMA((n,)),
                   *[pltpu.HBM(b.shape, b.dtype) for b in bufs], jax.ShapeDtypeStruct((8, 128), F32)),
        in_specs=[HBM] * nb + [ANY] * na,
        out_specs=(SEM, SEM, *[HBM] * nb, pl.BlockSpec(memory_space=pltpu.VMEM)),
        input_output_aliases={i: 2 + i for i in range(nb)},
        compiler_params=pltpu.CompilerParams(has_side_effects=EFFECT),
    )(*[pltpu.with_memory_space_constraint(b, pltpu.HBM) for b in bufs], *after)
    return dict(name=name, send=outs[0], recv=outs[1], bufs=list(outs[2:2 + nb]), plan=plan), outs[-1]


def xchg_wait(flight, after):
    bufs, plan = flight["bufs"], flight["plan"]
    nb = len(bufs)

    def body(*refs):
        send, recv = refs[nb], refs[nb + 1]
        for k, (src, dst, to) in enumerate(plan(refs[:nb])):
            cp = _copy(src, dst, send, recv, k, to)
            cp.wait_send()
            cp.wait_recv()

    outs = pl.pallas_call(
        body, name=flight["name"] + "_wait",
        out_shape=tuple(pltpu.HBM(b.shape, b.dtype) for b in bufs),
        in_specs=[HBM] * nb + [SEM, SEM] + [ANY] * len(after),
        out_specs=tuple([HBM] * nb), input_output_aliases={i: i for i in range(nb)},
        compiler_params=pltpu.CompilerParams(has_side_effects=EFFECT),
    )(*bufs, flight["send"], flight["recv"], *after)
    return list(outs)


def _flip(k, x, y, c):
    return ((1 - x) if k & 4 else x, (1 - y) if k & 2 else y, (1 - c) if k & 1 else c)


class WeightGather:
    def __init__(self, shards, groups):
        me = 2 * lax.axis_index("x") + lax.axis_index("y")
        self.names = dict(groups)
        self.ici, self.d2d = {}, {}
        self.token = None
        for gname, names in groups:
            nt = len(names)
            srcs = [shards[n] for n in names]
            lands = [lax.dynamic_update_slice(lax.empty((4,) + s.shape, s.dtype), s[None], (me, 0, 0, 0))
                     for s in srcs]

            def plan(refs, nt=nt):
                x, y, c, chips = _place()
                return [(refs[t].at[c], refs[nt + t].at[2 * x + y, c], (cx, cy, c))
                        for t in range(nt) for cx, cy in chips]

            self.ici[gname], self.token = xchg_start(f"ag_ici_{gname}", srcs + lands, plan, 3 * nt,
                                                     after=[] if self.token is None else [self.token])

    def forward(self, gname, after):
        nt = len(self.names[gname])
        lands = xchg_wait(self.ici.pop(gname), after)[nt:]

        def plan(refs):
            x, y, c, chips = _place()
            out = []
            for t in range(nt):
                for cx, cy in chips:
                    piece = refs[t].at[2 * cx + cy, c]
                    out.append((piece, piece, (x, y, 1 - c)))
            return out

        self.d2d[gname], token = xchg_start(f"ag_d2d_{gname}", lands, plan, 3 * nt)
        return token

    def get(self, gname, after):
        lands = xchg_wait(self.d2d.pop(gname), after)
        return dict(zip(self.names[gname], lands))


class GradReduce:
    def __init__(self, kinds):
        self.J = {k: lax.empty((L, 2, a2, b), F32) for k, (L, a2, b) in kinds.items()}
        self.x, self.j = {}, {}

    @staticmethod
    def _where(name):
        kind, _, l = name.partition("_")
        return kind, int(l or 0)

    def send(self, gname, grads, after=()):
        names = list(grads)
        nt = len(names)
        gs = [grads[n] for n in names]
        xs = [lax.empty((7,) + g.shape[2:], g.dtype) for g in gs]

        def plan(refs):
            x, y, c, _ = _place()
            out = []
            for t in range(nt):
                for k in range(1, 8):
                    px, py, pc = _flip(k, x, y, c)
                    out.append((refs[t].at[2 * px + py, pc], refs[nt + t].at[k - 1], (px, py, pc)))
            return out

        flight, token = xchg_start(f"rs_x_{gname}", gs + xs, plan, 7 * nt, after=after)
        self.x[gname] = (names, flight)
        return token

    def reduce(self, gname, after):
        names, flight = self.x.pop(gname)
        nt = len(names)
        bufs = xchg_wait(flight, after)
        me, c = 2 * lax.axis_index("x") + lax.axis_index("y"), lax.axis_index("c")
        hs = []
        for t, n in enumerate(names):
            g = bufs[t]
            own = lax.dynamic_slice(g, (me, c, 0, 0), (1, 1) + g.shape[2:])[0, 0]
            hs.append(add8(own, bufs[nt + t], f"rs_add_{n}"))
        where = [self._where(n) for n in names]

        def plan(refs):
            x, y, c, _ = _place()
            return [(refs[t], refs[nt + t].at[where[t][1], c], (x, y, 1 - c)) for t in range(nt)]

        flight, token = xchg_start(f"rs_join_{gname}", hs + [self.J[k] for k, _ in where], plan, nt)
        self.j[gname] = (where, flight)
        return token

    def finish(self, gname, after):
        where, flight = self.j.pop(gname)
        nt = len(where)
        bufs = xchg_wait(flight, after)
        c = lax.axis_index("c")
        for t, (kind, l) in enumerate(where):
            self.J[kind] = lax.dynamic_update_slice(bufs[nt + t], bufs[t][None, None], (l, c, 0, 0))


def allreduce_small(v):
    R = v.shape[0]

    def body(v_ref, o_ref, all_ref, send, recv):
        x, y, c, _ = _place()
        me = 4 * x + 2 * y + c
        all_ref[me] = v_ref[...]
        cps = []
        for k in range(1, 8):
            cp = _copy(v_ref, all_ref.at[me], send, recv, k - 1, _flip(k, x, y, c))
            cp.start()
            cps.append(cp)
        for k in range(1, 8):
            px, py, pc = _flip(k, x, y, c)
            _copy(v_ref, all_ref.at[4 * px + 2 * py + pc], send, recv, k - 1, (px, py, pc)).wait_recv()
        for cp in cps:
            cp.wait_send()
        acc = all_ref[0]
        for d in range(1, 8):
            acc = acc + all_ref[d]
        o_ref[...] = acc

    return pl.pallas_call(
        body, name="allreduce_small",
        in_specs=[pl.BlockSpec(memory_space=pltpu.VMEM)], out_specs=pl.BlockSpec(memory_space=pltpu.VMEM),
        out_shape=jax.ShapeDtypeStruct((R, D), F32),
        scratch_shapes=[pltpu.VMEM((8, R, D), F32), pltpu.SemaphoreType.DMA((7,)), pltpu.SemaphoreType.DMA((7,))],
        compiler_params=pltpu.CompilerParams(has_side_effects=True, vmem_limit_bytes=VMEM_LIMIT),
    )(v)


AG_GROUPS = (("a0", ("pw1_0", "pw2_0", "small")), ("f0", ("up_0", "down_0")),
             ("l1", ("pw1_1", "pw2_1", "up_1", "down_1")), ("l2", ("kv", "wq_0", "wo_0", "up_2", "down_2")),
             ("l3", ("wq_1", "wo_1", "up_3", "down_3")))


def _bucket_table():
    qi = np.arange(BLK)[:, None]
    kj = np.arange(2 * BLK)[None, :]
    d = np.maximum(qi + BLK - kj, 0)
    max_exact = N_BUCKETS // 2
    log_ratio = (np.log(np.maximum(d, 1).astype(np.float32) / np.float32(max_exact))
                 / np.float32(math.log(MAX_DISTANCE / max_exact))).astype(np.float32)
    large = max_exact + (log_ratio * np.float32(N_BUCKETS - max_exact)).astype(np.int32)
    large = np.minimum(large, N_BUCKETS - 1)
    return np.where(d < max_exact, d, large).astype(np.int32)


def _heads_major(a, nh):
    T = a.shape[0]
    return a.reshape(T, nh, HD).transpose(1, 0, 2)


def _heads_minor(a):
    nh, T, _ = a.shape
    return a.transpose(1, 0, 2).reshape(T, nh * HD)


def _slots(land):
    return land.reshape(4, 2 * land.shape[2], land.shape[3])


def _rows(land):
    return land.reshape(8 * land.shape[2], land.shape[3])


def _gview(g):
    s, K, n = g.shape
    return g.reshape(4, 2, K // 2, n) if s == 4 else g.reshape(4, 2, K // 8, n)


def _gate(a, token):
    return a + token[0, 0]


def _conv_small(f_small):
    fs = f_small.transpose(1, 2, 0, 3).reshape(2, 40, D)
    b_pw1 = f_small[:, :, 35:37, :].transpose(1, 0, 2, 3).reshape(2, 1, 2 * D)
    rev = jnp.concatenate([fs[:, CONV_W - 1::-1], jnp.zeros((2, 40 - CONV_W, D), F32)], axis=1)
    return dict(conv=fs, conv_rev=rev, b_pw1=b_pw1, b_pw2=fs[:, 34:35])


def run_step(x, target, P, ag, rs):
    T = x.shape[0]
    zero = jnp.zeros((1, 1, D), F32)
    nm, nf = P["norm_mix"], P["norm_ffn"]
    ag.forward("a0", [ag.token])
    W = ag.get("a0", [])
    sm = _conv_small(W["small"])
    h = x
    saved = []
    for l in range(2):
        xn, u, a = norm_mm_glu(h, nm, l, _slots(W[f"pw1_{l}"]), sm["b_pw1"], f"f_pw1_{l}")
        y, s = dwconv_ln_silu(a, sm["conv"], l, f"f_conv_{l}")
        b2 = sm["b_pw2"]
        if l == 0:
            b2 = _gate(b2, ag.forward("f0", [s]))
        h1 = mm_bias_res(s, _rows(W[f"pw2_{l}"]), b2, l, h, f"f_pw2_{l}")
        if l == 0:
            W.update(ag.get("f0", [h1]))
        xn2, gu, f = norm_mm_swiglu(h1, nf, l, _slots(W[f"up_{l}"]), f"f_up_{l}")
        nxt = "l1" if l == 0 else "l2"
        h2 = mm_bias_res(f, _rows(W[f"down_{l}"]), _gate(zero, ag.forward(nxt, [f])), 0, h1, f"f_down_{l}")
        W.update(ag.get(nxt, [h2]))
        saved.append(dict(h=h, xn=xn, u=u, a=a, y=y, s=s, h1=h1, xn2=xn2, gu=gu, f=f))
        h = h2
    h_kv = h
    kvn, kv = norm_mm(h, P["norm_kv"], 0, _rows(W["kv"]), "f_kv")
    kp = jnp.pad(_heads_major(kv[:, :N_KV * HD], N_KV), ((0, 0), (BLK, 0), (0, 0)))
    vp = jnp.pad(_heads_major(kv[:, N_KV * HD:], N_KV), ((0, 0), (BLK, 0), (0, 0)))
    kvt = jnp.pad(kv.T.reshape(2, N_KV, HD, T), ((0, 0), (0, 0), (0, 0), (BLK, 0)))
    kt, vt = kvt[0], kvt[1]
    bucket = _bucket_table()
    onehot = jnp.asarray(np.eye(N_BUCKETS, dtype=np.float32)[bucket])
    bias = jnp.einsum("qkb,bh->hkq", onehot, P["rel_bias"], precision=lax.Precision.HIGHEST)
    bias = bias.reshape(N_KV, GROUP, 2 * BLK, BLK).transpose(0, 2, 1, 3).reshape(1, N_KV, 2 * BLK, QW)
    bias = bias + jnp.asarray(band_mask())[:, None]
    for j in range(2):
        l = 2 + j
        xn, q = norm_mm(h, nm, l, _rows(W[f"wq_{j}"]), f"f_q_{j}", scale=HD ** -0.5)
        qh = q.T.reshape(N_KV, GROUP, HD, T)
        sink = jnp.broadcast_to(P["sinks"][j].reshape(N_KV, GROUP, 1), (N_KV, GROUP, BLK)).reshape(N_KV, 1, QW)
        oh = attn_fwd(qh, kp, vt, bias, sink, f"f_attn_{j}")
        attn = oh.reshape(N_HEADS * HD, T).T
        h1 = mm_bias_res(attn, _rows(W[f"wo_{j}"]), zero, 0, h, f"f_wo_{j}")
        xn2, gu, f = norm_mm_swiglu(h1, nf, l, _slots(W[f"up_{l}"]), f"f_up_{l}")
        zg = _gate(zero, ag.forward("l3", [f])) if j == 0 else zero
        h2 = mm_bias_res(f, _rows(W[f"down_{l}"]), zg, 0, h1, f"f_down_{l}")
        if j == 0:
            W.update(ag.get("l3", [h2]))
        saved.append(dict(h=h, xn=xn, qh=qh, oh=oh, sink=sink, attn=attn, h1=h1, xn2=xn2, gu=gu, f=f))
        h = h2

    dh, st_final = final_loss(h, P["norm_final"], target, "loss_head")

    S = dict(norm_ffn=[None] * 4, norm_mix=[None] * 4, conv=[None] * 2, taps=[None] * 2, b_pw1=[None] * 2,
             b_pw2=[None] * 2, sinks=[None] * 2)

    def ffn_bwd(dh, sv, l, nf, after=()):
        du = mmT_swiglu_bwd(dh, _rows(W[f"down_{l}"]), sv["gu"], f"b_down_{l}", after)
        gd = mm_dw(sv["f"], dh, f"w_down_{l}", 512, 1)
        gu = mm_dw(sv["xn2"], du, f"w_up_{l}", DFF // 2, 4)
        dh, dg = mmT_rmsbwd(du, _slots(W[f"up_{l}"]), sv["h1"], nf, l, dh, f"b_up_{l}")
        S["norm_ffn"][l] = dg[0]
        return dh, {f"down_{l}": _gview(gd), f"up_{l}": _gview(gu)}

    dk = dv = dbias = None
    sent = []
    for j in (1, 0):
        l = 2 + j
        sv = saved[l]
        dh, grads = ffn_bwd(dh, sv, l, nf, sent)
        dattn = mmT(dh, _rows(W[f"wo_{j}"]), f"b_wo_{j}")
        grads[f"wo_{j}"] = _gview(mm_dw(sv["attn"], dh, f"w_wo_{j}", 512, 1))
        doh = dattn.T.reshape(N_KV, GROUP, HD, T)
        dqh, dkj, dvj, dbj, dsj = attn_bwd(sv["qh"], kp, kt, vp, bias, sv["sink"], sv["oh"], doh, f"b_attn_{j}")
        dq = dqh.reshape(N_HEADS * HD, T).T
        grads[f"wq_{j}"] = _gview(mm_dw(sv["xn"], dq, f"w_q_{j}", 512, 1))
        dh, dg = mmT_rmsbwd(dq, _rows(W[f"wq_{j}"])[None], sv["h"], nm, l, dh, f"b_q_{j}")
        S["norm_mix"][l] = dg[0]
        S["sinks"][j] = jnp.sum(dsj.reshape(N_HEADS, BLK), axis=1)
        dk = dkj if dk is None else dk + dkj
        dv = dvj if dv is None else dv + dvj
        dbias = dbj if dbias is None else dbias + dbj
        if j == 1:
            sent = [rs.send("l3", grads)]

    dkv = jnp.concatenate([_heads_minor(dk[:, BLK:]), _heads_minor(dv[:, BLK:])], axis=1).astype(BF16)
    grads["kv"] = _gview(mm_dw(kvn, dkv, "w_kv", 512, 1))
    dh, dg = mmT_rmsbwd(dkv, _rows(W["kv"])[None], h_kv, P["norm_kv"], 0, dh, "b_kv")
    S["norm_kv"] = dg[0]
    dbh = dbias.reshape(N_KV, 2 * BLK, GROUP, BLK)
    S["rel_bias"] = jnp.einsum("vkgq,qkb->bvg", dbh, onehot, precision=lax.Precision.HIGHEST).reshape(N_BUCKETS, N_HEADS)
    sent = [rs.send("l2", grads)]
    nf = _gate(nf, rs.reduce("l3", [dh]))

    for l in (1, 0):
        sv = saved[l]
        dh, grads = ffn_bwd(dh, sv, l, nf, sent)
        conv = sm["conv"]
        if l == 0:
            conv = _gate(conv, rs.send("f0", grads))
            grads = {}
        dy, st = mmT_lnbwd(dh, _rows(W[f"pw2_{l}"]), sv["y"], conv, l, f"b_pw2_{l}")
        g2, S["b_pw2"][l] = mm_dw(sv["s"], dh, f"w_pw2_{l}", 512, 1, colsum=True)
        du, dtaps = dwconv_glu_bwd(dy, sv["a"], sv["u"], sm["conv"], sm["conv_rev"], l, f"b_conv_{l}")
        S["conv"][l] = st[0:3]
        S["taps"][l] = dtaps[0:CONV_W]
        if l == 0:
            rs.finish("l2", [du])
            nm = _gate(nm, rs.reduce("l1", [du]))
        g1, S["b_pw1"][l] = mm_dw(sv["xn"], du, f"w_pw1_{l}", 512, 4, colsum=True)
        grads[f"pw2_{l}"], grads[f"pw1_{l}"] = _gview(g2), _gview(g1)
        dh, dg = mmT_rmsbwd(du, _slots(W[f"pw1_{l}"]), sv["h"], nm, l, dh, f"b_pw1_{l}")
        S["norm_mix"][l] = dg[0]
        if l == 1:
            sent = [rs.send("l1", grads)]
            rs.finish("l3", [dh])
            nf = _gate(nf, rs.reduce("l2", [dh]))
    S["norm_final"] = st_final[0]
    S["loss"] = st_final[1]
    return grads, dh, S


R_CONV = 37
R_SMALL = 88


def _pack_small(S):
    rows = []
    for l in range(2):
        rows += [S["taps"][l], S["conv"][l][2:3], S["conv"][l][0:2], S["b_pw2"][l], S["b_pw1"][l].reshape(2, D)]
    rows += [jnp.stack(S["norm_mix"]), jnp.stack(S["norm_ffn"]), S["norm_kv"][None], S["norm_final"][None]]
    tail = jnp.concatenate([jnp.stack(S["sinks"]).reshape(-1), S["rel_bias"].reshape(-1)])
    rows += [jnp.pad(tail, (0, D - tail.shape[0]))[None], S["loss"][None]]
    v = jnp.concatenate(rows, axis=0)
    return jnp.pad(v, ((0, R_SMALL - v.shape[0]), (0, 0)))


def kernel(x, norm_mix, norm_ffn, conv_w_pw1, conv_b_pw1, conv_w_dw, conv_b_dw, conv_ln_g, conv_ln_b, conv_w_pw2, conv_b_pw2, norm_kv, w_kv, w_q, w_o, sinks, rel_bias, ffn_w_up, ffn_w_down, norm_final, loss_target, m_norm_mix, m_norm_ffn, m_conv_w_pw1, m_conv_b_pw1, m_conv_w_dw, m_conv_b_dw, m_conv_ln_g, m_conv_ln_b, m_conv_w_pw2, m_conv_b_pw2, m_norm_kv, m_w_kv, m_w_q, m_w_o, m_sinks, m_rel_bias, m_ffn_w_up, m_ffn_w_down, m_norm_final, v_norm_mix, v_norm_ffn, v_conv_w_pw1, v_conv_b_pw1, v_conv_w_dw, v_conv_b_dw, v_conv_ln_g, v_conv_ln_b, v_conv_w_pw2, v_conv_b_pw2, v_norm_kv, v_w_kv, v_w_q, v_w_o, v_sinks, v_rel_bias, v_ffn_w_up, v_ffn_w_down, v_norm_final):
    me = 2 * lax.axis_index("x") + lax.axis_index("y")
    weights = dict(norm_mix=norm_mix, norm_ffn=norm_ffn, conv_w_pw1=conv_w_pw1, conv_b_pw1=conv_b_pw1,
                   conv_w_dw=conv_w_dw, conv_b_dw=conv_b_dw, conv_ln_g=conv_ln_g, conv_ln_b=conv_ln_b,
                   conv_w_pw2=conv_w_pw2, conv_b_pw2=conv_b_pw2, norm_kv=norm_kv, w_kv=w_kv, w_q=w_q, w_o=w_o,
                   sinks=sinks, rel_bias=rel_bias, ffn_w_up=ffn_w_up, ffn_w_down=ffn_w_down, norm_final=norm_final)
    mom_m = dict(norm_mix=m_norm_mix, norm_ffn=m_norm_ffn, conv_w_pw1=m_conv_w_pw1, conv_b_pw1=m_conv_b_pw1,
                 conv_w_dw=m_conv_w_dw, conv_b_dw=m_conv_b_dw, conv_ln_g=m_conv_ln_g, conv_ln_b=m_conv_ln_b,
                 conv_w_pw2=m_conv_w_pw2, conv_b_pw2=m_conv_b_pw2, norm_kv=m_norm_kv, w_kv=m_w_kv, w_q=m_w_q,
                 w_o=m_w_o, sinks=m_sinks, rel_bias=m_rel_bias, ffn_w_up=m_ffn_w_up, ffn_w_down=m_ffn_w_down,
                 norm_final=m_norm_final)
    mom_v = dict(norm_mix=v_norm_mix, norm_ffn=v_norm_ffn, conv_w_pw1=v_conv_w_pw1, conv_b_pw1=v_conv_b_pw1,
                 conv_w_dw=v_conv_w_dw, conv_b_dw=v_conv_b_dw, conv_ln_g=v_conv_ln_g, conv_ln_b=v_conv_ln_b,
                 conv_w_pw2=v_conv_w_pw2, conv_b_pw2=v_conv_b_pw2, norm_kv=v_norm_kv, w_kv=v_w_kv, w_q=v_w_q,
                 w_o=v_w_o, sinks=v_sinks, rel_bias=v_rel_bias, ffn_w_up=v_ffn_w_up, ffn_w_down=v_ffn_w_down,
                 norm_final=v_norm_final)

    def halves(a):
        return a.astype(BF16).reshape(2, a.shape[0] // 2, a.shape[1])

    shards = {"kv": halves(w_kv)}
    for l in range(2):
        shards[f"pw1_{l}"], shards[f"pw2_{l}"] = halves(conv_w_pw1[l]), halves(conv_w_pw2[l])
        shards[f"wq_{l}"], shards[f"wo_{l}"] = halves(w_q[l]), halves(w_o[l])
    for l in range(4):
        shards[f"up_{l}"], shards[f"down_{l}"] = halves(ffn_w_up[l]), halves(ffn_w_down[l])
    shards["small"] = jnp.concatenate(
        [conv_w_dw, conv_b_dw[:, None], conv_ln_g[:, None], conv_ln_b[:, None], conv_b_pw2[:, None],
         conv_b_pw1.reshape(2, 2, 256), jnp.zeros((2, 3, 256), F32)], axis=1)
    ag = WeightGather(shards, AG_GROUPS)
    big = {"conv_w_pw1": "pw1", "conv_w_pw2": "pw2", "w_q": "wq", "w_o": "wo", "ffn_w_up": "up",
           "ffn_w_down": "down", "w_kv": "kv"}
    rs = GradReduce({"pw1": (2, 512, 512), "pw2": (2, 128, D), "wq": (2, 128, D), "wo": (2, 128, D),
                     "up": (4, 512, DFF // 2), "down": (4, DFF // 8, D), "kv": (1, 128, 512)})

    P = dict(norm_mix=norm_mix[:, None], norm_ffn=norm_ffn[:, None], norm_kv=norm_kv[None, None],
             norm_final=norm_final[None], sinks=sinks, rel_bias=rel_bias)
    last, grad_x, S = run_step(x[0], loss_target[0], P, ag, rs)

    rs.finish("l1", [grad_x])
    vsum = allreduce_small(_gate(_pack_small(S), rs.reduce("f0", [grad_x])))
    rs.send("c0", last, after=[vsum])
    col = lambda a: lax.dynamic_slice_in_dim(a, me * 256, 256, axis=-1)
    grads = {}
    for l in range(2):
        base = l * R_CONV
        grads.setdefault("conv_w_dw", []).append(col(vsum[base:base + 31]))
        grads.setdefault("conv_b_dw", []).append(col(vsum[base + 31]))
        grads.setdefault("conv_ln_g", []).append(col(vsum[base + 32]))
        grads.setdefault("conv_ln_b", []).append(col(vsum[base + 33]))
        grads.setdefault("conv_b_pw2", []).append(col(vsum[base + 34]))
        grads.setdefault("conv_b_pw1", []).append(
            lax.dynamic_slice_in_dim(vsum[base + 35:base + 37].reshape(2 * D), me * 512, 512, axis=0))
    grads = {k: jnp.stack(v) for k, v in grads.items()}
    base = 2 * R_CONV
    grads["norm_mix"] = vsum[base:base + 4]
    grads["norm_ffn"] = vsum[base + 4:base + 8]
    grads["norm_kv"] = vsum[base + 8]
    grads["norm_final"] = vsum[base + 9]
    grads["sinks"] = vsum[base + 10, 0:32].reshape(2, 16)
    grads["rel_bias"] = vsum[base + 10, 32:32 + 512].reshape(32, 16)
    loss = vsum[base + 11, 0]

    delta, new_m, new_v = {}, {}, {}
    rest = [n for n in weights if n not in big]

    def pack(dct):
        flat = jnp.concatenate([dct[n].reshape(-1) for n in rest])
        return jnp.pad(flat, (0, (-flat.shape[0]) % (8 * 128))).reshape(-1, 128)

    d, nm, nv = adamw(pack(weights), pack(grads), pack(mom_m), pack(mom_v), "adamw_small")
    off = 0
    for n in rest:
        shp = weights[n].shape
        sz = int(np.prod(shp))
        delta[n] = d.reshape(-1)[off:off + sz].reshape(shp)
        new_m[n] = nm.reshape(-1)[off:off + sz].reshape(shp)
        new_v[n] = nv.reshape(-1)[off:off + sz].reshape(shp)
        off += sz

    def update(n):
        shp = weights[n].shape
        r2 = (int(np.prod(shp[:-1])), shp[-1])
        grads[n] = rs.J[big[n]].reshape(shp)
        d, nm, nv = adamw(weights[n].reshape(r2), grads[n].reshape(r2), mom_m[n].reshape(r2), mom_v[n].reshape(r2),
                          f"adamw_{n}")
        delta[n], new_m[n], new_v[n] = d.reshape(shp), nm.reshape(shp), nv.reshape(shp)

    rs.finish("f0", [vsum])
    for n in ("ffn_w_up", "ffn_w_down"):
        update(n)
    rs.reduce("c0", [delta["ffn_w_down"]])
    for n in ("w_q", "w_o", "w_kv"):
        update(n)
    rs.finish("c0", [delta["w_kv"]])
    for n in ("conv_w_pw1", "conv_w_pw2"):
        update(n)

    order = list(weights)
    return (loss, grad_x[None], *[grads[n] for n in order], *[delta[n] for n in order],
            *[new_m[n] for n in order], *[new_v[n] for n in order])
```

```python
import functools
import math

import numpy as np
import jax
import jax.numpy as jnp
from jax import lax
from jax.experimental import pallas as pl
from jax.experimental.pallas import tpu as pltpu

F32 = jnp.float32
BF16 = jnp.bfloat16
MESH = pl.DeviceIdType.MESH

D = 1024
DFF = 2816
N_HEADS = 16
N_KV = 4
GROUP = 4
HD = 64
BLK = 128
CONV_W = 31
HALO = 32
N_BUCKETS = 32
MAX_DISTANCE = 128
EPS = 1e-6
NEG_INF = -1e30
TM = 512
TCV = 256
VMEM_LIMIT = 56 * 2 ** 20

ADAM_LR, ADAM_B1, ADAM_B2, ADAM_EPS, ADAM_WD, ADAM_STEP = 0.001, 0.9, 0.999, 1e-08, 0.01, 10


def _cp(*sem):
    return pltpu.CompilerParams(dimension_semantics=sem, vmem_limit_bytes=VMEM_LIMIT)


def _sigmoid(x):
    return 1.0 / (1.0 + jnp.exp(-x))


def _row(tm, n):
    return pl.BlockSpec((tm, n), lambda i: (i, 0))


def _const(shape):
    nd = len(shape)
    return pl.BlockSpec(shape, lambda i: (0,) * nd)


def _weight(shape):
    nd = len(shape)
    return pl.BlockSpec(shape, lambda i: (0,) * nd, pipeline_mode=pl.Buffered(1))


def _layer(shape, l):
    nd = len(shape)
    return pl.BlockSpec((None,) + tuple(shape), lambda i: (l,) + (0,) * nd)


def _dot(a, b):
    return jnp.dot(a, b, preferred_element_type=F32)


def _dot_nt(a, b):
    return lax.dot_general(a, b, (((1,), (1,)), ((), ())), preferred_element_type=F32)


def _dot_tn(a, b):
    return lax.dot_general(a, b, (((0,), (0,)), ((), ())), preferred_element_type=F32)


def _rms(x):
    return lax.rsqrt(jnp.mean(x * x, axis=-1, keepdims=True) + EPS)


def norm_mm_glu(h, g, l, w, b, name):
    T = h.shape[0]
    ns = w.shape[-1]

    def body(h_ref, g_ref, w_ref, b_ref, xn_ref, u_ref, a_ref):
        x = h_ref[...]
        xn = (x * _rms(x) * g_ref[...]).astype(BF16)
        xn_ref[...] = xn
        for s in range(2):
            lo, hi = s * ns, (s + 1) * ns
            u1 = _dot(xn, w_ref[s]) + b_ref[:, lo:hi]
            u2 = _dot(xn, w_ref[2 + s]) + b_ref[:, D + lo:D + hi]
            u_ref[:, lo:hi] = u1.astype(BF16)
            u_ref[:, D + lo:D + hi] = u2.astype(BF16)
            a_ref[:, lo:hi] = u1 * _sigmoid(u2)

    return pl.pallas_call(
        body, name=name, grid=(T // TM,),
        in_specs=[_row(TM, D), _layer((1, D), l), _weight((4, D, ns)), _layer((1, 2 * D), l)],
        out_specs=[_row(TM, D), _row(TM, 2 * D), _row(TM, D)],
        out_shape=[jax.ShapeDtypeStruct((T, D), BF16), jax.ShapeDtypeStruct((T, 2 * D), BF16),
                   jax.ShapeDtypeStruct((T, D), F32)],
        compiler_params=_cp("parallel"),
    )(h, g, w, b)


SUB = 8


def _make_shifts(sh):
    n = TCV + HALO - SUB
    for r in range(1, SUB):
        for r0 in range(0, n, 40):
            sh[r, r0:r0 + 40, :] = sh[0, pl.ds(r + r0, 40), :]


def _shifted(sh, off, rows, cols):
    return sh[off % SUB, pl.ds(off - off % SUB, rows), cols]


def _conv_taps(sh, w_ref, out_ref, first):
    RB, LB = 32, 512
    for r0 in range(0, TCV, RB):
        for c0 in range(0, D, LB):
            acc = jnp.zeros((RB, LB), F32)
            for k in range(CONV_W):
                acc = acc + w_ref[k:k + 1, c0:c0 + LB] * _shifted(sh, first + k + r0, RB, slice(c0, c0 + LB))
            out_ref[r0:r0 + RB, c0:c0 + LB] = acc


def dwconv_ln_silu(a, sm, l, name):
    T = a.shape[0]
    nb = TCV // HALO

    def body(cur_ref, prev_ref, sm_ref, y_ref, s_ref, sh):
        i = pl.program_id(0)
        sh[0, 0:HALO, :] = jnp.where(i > 0, prev_ref[...], 0.0)
        sh[0, HALO:HALO + TCV, :] = cur_ref[...]
        _make_shifts(sh)
        _conv_taps(sh, sm_ref, y_ref, HALO - (CONV_W - 1))
        y = y_ref[...] + sm_ref[31:32, :]
        y_ref[...] = y
        mu = jnp.mean(y, axis=-1, keepdims=True)
        yc = y - mu
        rstd = lax.rsqrt(jnp.mean(yc * yc, axis=-1, keepdims=True) + EPS)
        z = yc * rstd * sm_ref[32:33, :] + sm_ref[33:34, :]
        s_ref[...] = (z * _sigmoid(z)).astype(BF16)

    return pl.pallas_call(
        body, name=name, grid=(T // TCV,),
        in_specs=[_row(TCV, D), pl.BlockSpec((HALO, D), lambda i: (jnp.maximum(i * nb - 1, 0), 0)),
                  _layer((40, D), l)],
        out_specs=[_row(TCV, D), _row(TCV, D)],
        out_shape=[jax.ShapeDtypeStruct((T, D), F32), jax.ShapeDtypeStruct((T, D), BF16)],
        scratch_shapes=[pltpu.VMEM((SUB, TCV + HALO, D), F32)],
        compiler_params=_cp("parallel"),
    )(a, a, sm)


def mm_bias_res(xb, w, b, bl, res, name):
    T, K = xb.shape

    def body(x_ref, w_ref, b_ref, r_ref, o_ref):
        o_ref[...] = _dot(x_ref[...], w_ref[...]) + b_ref[...] + r_ref[...]

    return pl.pallas_call(
        body, name=name, grid=(T // TM,),
        in_specs=[_row(TM, K), _weight((K, D)), _layer((1, D), bl), _row(TM, D)],
        out_specs=_row(TM, D), out_shape=jax.ShapeDtypeStruct((T, D), F32),
        compiler_params=_cp("parallel"),
    )(xb, w, b, res)


def norm_mm_swiglu(h, g, l, w, name):
    T = h.shape[0]
    ns = w.shape[-1]

    def body(h_ref, g_ref, w_ref, xn_ref, gu_ref, f_ref):
        x = h_ref[...]
        xn = (x * _rms(x) * g_ref[...]).astype(BF16)
        xn_ref[...] = xn
        for s in range(2):
            lo, hi = s * ns, (s + 1) * ns
            gate = _dot(xn, w_ref[s])
            up = _dot(xn, w_ref[2 + s])
            gu_ref[:, lo:hi] = gate.astype(BF16)
            gu_ref[:, DFF + lo:DFF + hi] = up.astype(BF16)
            f_ref[:, lo:hi] = (gate * _sigmoid(gate) * up).astype(BF16)

    return pl.pallas_call(
        body, name=name, grid=(T // TM,),
        in_specs=[_row(TM, D), _layer((1, D), l), _weight((4, D, ns))],
        out_specs=[_row(TM, D), _row(TM, 2 * DFF), _row(TM, DFF)],
        out_shape=[jax.ShapeDtypeStruct((T, D), BF16), jax.ShapeDtypeStruct((T, 2 * DFF), BF16),
                   jax.ShapeDtypeStruct((T, DFF), BF16)],
        compiler_params=_cp("parallel"),
    )(h, g, w)


def norm_mm(h, g, gl, w, name, scale=1.0):
    T = h.shape[0]
    N = w.shape[-1]

    def body(h_ref, g_ref, w_ref, xn_ref, o_ref):
        x = h_ref[...]
        xn = (x * _rms(x) * g_ref[...]).astype(BF16)
        xn_ref[...] = xn
        o_ref[...] = (_dot(xn, w_ref[...]) * scale).astype(BF16)

    return pl.pallas_call(
        body, name=name, grid=(T // TM,),
        in_specs=[_row(TM, D), _layer((1, D), gl), _weight((D, N))],
        out_specs=[_row(TM, D), _row(TM, N)],
        out_shape=[jax.ShapeDtypeStruct((T, D), BF16), jax.ShapeDtypeStruct((T, N), BF16)],
        compiler_params=_cp("parallel"),
    )(h, g, w)


QB = 4
QW = GROUP * BLK


def band_mask():
    qi = np.arange(QW)[None, :] % BLK
    kj = np.arange(2 * BLK)[:, None]
    band = ((kj < BLK) & (kj > qi)) | ((kj >= BLK) & (kj - BLK <= qi))
    first = band & (kj >= BLK)
    return np.where(np.stack([first, band]), 0.0, NEG_INF).astype(np.float32)


def _softmax_cols(s, sink):
    m = jnp.maximum(jnp.max(s, axis=0, keepdims=True), sink)
    p = jnp.exp(s - m)
    es = jnp.exp(sink - m)
    inv = 1.0 / (jnp.sum(p, axis=0, keepdims=True) + es)
    return p, inv, es


def _attn_specs(T):
    W = QB * BLK
    qspec = pl.BlockSpec((None, GROUP, HD, W), lambda kv, n: (kv, 0, 0, n))
    kspec = pl.BlockSpec((None, T + BLK, HD), lambda kv, n: (kv, 0, 0))
    ktspec = [pl.BlockSpec((None, HD, W), lambda kv, n: (kv, 0, n)),
              pl.BlockSpec((None, HD, BLK), lambda kv, n: (kv, 0, (n + 1) * QB))]
    bspec = pl.BlockSpec((2, None, 2 * BLK, QW), lambda kv, n: (0, kv, 0, 0))
    sspec = pl.BlockSpec((None, 1, QW), lambda kv, n: (kv, 0, 0))
    return qspec, kspec, ktspec, bspec, sspec


def _attn_block(n, b):
    blk = n * QB + b
    rows = pl.ds(pl.multiple_of(blk * BLK, BLK), 2 * BLK)
    return rows, (jnp.minimum(blk, 1) if b == 0 else 1)


def _band_cols(main_ref, tail_ref, b):
    if b < QB - 1:
        return main_ref[:, b * BLK:(b + 2) * BLK]
    return jnp.concatenate([main_ref[:, b * BLK:], tail_ref[...]], axis=1)


def _heads_side_by_side(ref, qs):
    return jnp.concatenate([ref[g, :, qs] for g in range(GROUP)], axis=1)


def attn_fwd(q, kp, vt, bias, sink, name):
    T = q.shape[3]
    qspec, kspec, ktspec, bspec, sspec = _attn_specs(T)

    def body(q_ref, k_ref, vt_ref, vtt_ref, b_ref, s_ref, o_ref, pb):
        n = pl.program_id(1)
        for b in range(QB):
            rows, table = _attn_block(n, b)
            qs = slice(b * BLK, (b + 1) * BLK)
            st = _dot(k_ref[rows, :], _heads_side_by_side(q_ref, qs))
            for g in range(GROUP):
                hs = slice(g * BLK, (g + 1) * BLK)
                p, inv, _ = _softmax_cols(st[:, hs] + b_ref[table, :, hs], s_ref[:, hs])
                pb[:, hs] = (p * inv).astype(BF16)
            ot = _dot(_band_cols(vt_ref, vtt_ref, b), pb[...])
            for g in range(GROUP):
                o_ref[g, :, qs] = ot[:, g * BLK:(g + 1) * BLK].astype(BF16)

    return pl.pallas_call(
        body, name=name, grid=(N_KV, T // (QB * BLK)),
        in_specs=[qspec, kspec, *ktspec, bspec, sspec], out_specs=qspec,
        out_shape=jax.ShapeDtypeStruct((N_KV, GROUP, HD, T), BF16),
        scratch_shapes=[pltpu.VMEM((2 * BLK, QW), BF16)],
        compiler_params=_cp("parallel", "parallel"),
    )(q, kp, vt, vt, bias, sink)


def attn_bwd(q, kp, kt, vp, bias, sink, o, do, name):
    T = q.shape[3]
    qspec, kspec, ktspec, bspec, sspec = _attn_specs(T)

    def body(q_ref, k_ref, kt_ref, ktt_ref, v_ref, b_ref, s_ref, o_ref, do_ref,
             dq_ref, dk_ref, dv_ref, db_ref, ds_ref, pb, dsb):
        n = pl.program_id(1)

        @pl.when(n == 0)
        def _():
            dk_ref[...] = jnp.zeros_like(dk_ref)
            dv_ref[...] = jnp.zeros_like(dv_ref)
            db_ref[...] = jnp.zeros_like(db_ref)
            ds_ref[...] = jnp.zeros_like(ds_ref)

        for b in range(QB):
            rows, table = _attn_block(n, b)
            qs = slice(b * BLK, (b + 1) * BLK)
            q4 = _heads_side_by_side(q_ref, qs)
            do4 = _heads_side_by_side(do_ref, qs)
            st = _dot(k_ref[rows, :], q4)
            dpt = _dot(v_ref[rows, :], do4)
            for g in range(GROUP):
                hs = slice(g * BLK, (g + 1) * BLK)
                p, inv, es = _softmax_cols(st[:, hs] + b_ref[table, :, hs], s_ref[:, hs])
                probs = p * inv
                delta = jnp.sum(do_ref[g, :, qs].astype(F32) * o_ref[g, :, qs].astype(F32), axis=0, keepdims=True)
                dS = probs * (dpt[:, hs] - delta)
                ds_ref[:, hs] += -(es * inv) * delta
                db_ref[:, hs] += dS
                pb[:, hs] = probs.astype(BF16)
                dsb[:, hs] = dS.astype(BF16)
            dqt = _dot(_band_cols(kt_ref, ktt_ref, b), dsb[...]) * (HD ** -0.5)
            for g in range(GROUP):
                dq_ref[g, :, qs] = dqt[:, g * BLK:(g + 1) * BLK].astype(BF16)
            dk_ref[rows, :] += _dot_nt(dsb[...], q4)
            dv_ref[rows, :] += _dot_nt(pb[...], do4)

    kout = pl.BlockSpec((None, T + BLK, HD), lambda kv, n: (kv, 0, 0))
    dbspec = pl.BlockSpec((None, 2 * BLK, QW), lambda kv, n: (kv, 0, 0))
    return pl.pallas_call(
        body, name=name, grid=(N_KV, T // (QB * BLK)),
        in_specs=[qspec, kspec, *ktspec, kspec, bspec, sspec, qspec, qspec],
        out_specs=[qspec, kout, kout, dbspec, sspec],
        out_shape=[jax.ShapeDtypeStruct((N_KV, GROUP, HD, T), BF16),
                   jax.ShapeDtypeStruct((N_KV, T + BLK, HD), F32), jax.ShapeDtypeStruct((N_KV, T + BLK, HD), F32),
                   jax.ShapeDtypeStruct((N_KV, 2 * BLK, QW), F32), jax.ShapeDtypeStruct((N_KV, 1, QW), F32)],
        scratch_shapes=[pltpu.VMEM((2 * BLK, QW), BF16), pltpu.VMEM((2 * BLK, QW), BF16)],
        compiler_params=_cp("parallel", "arbitrary"),
    )(q, kp, kt, kt, vp, bias, sink, o, do)


def final_loss(h, g, target, name):
    T = h.shape[0]

    def body(h_ref, g_ref, t_ref, dh_ref, st_ref):
        i = pl.program_id(0)

        @pl.when(i == 0)
        def _():
            st_ref[...] = jnp.zeros_like(st_ref)

        x = h_ref[...]
        r = _rms(x)
        xh = x * r
        e = xh * g_ref[...] - t_ref[...]
        loss = 0.5 * jnp.sum(jnp.mean(e * e, axis=-1, keepdims=True))
        dy = e * (1.0 / D)
        st_ref[0:1, :] += jnp.sum(dy * xh, axis=0, keepdims=True)
        lane = lax.broadcasted_iota(jnp.int32, (1, D), 1)
        st_ref[1:2, :] += jnp.where(lane == 0, loss, 0.0)
        dxh = dy * g_ref[...]
        dh_ref[...] = r * (dxh - xh * jnp.mean(dxh * xh, axis=-1, keepdims=True))

    return pl.pallas_call(
        body, name=name, grid=(T // TM,),
        in_specs=[_row(TM, D), _const((1, D)), _row(TM, D)],
        out_specs=[_row(TM, D), _const((8, D))],
        out_shape=[jax.ShapeDtypeStruct((T, D), F32), jax.ShapeDtypeStruct((8, D), F32)],
        compiler_params=_cp("arbitrary"),
    )(h, g, target)


def mm_dw(x, dy, name, tn, slots, colsum=False):
    T, K = x.shape
    N = dy.shape[1]
    tt = min(T, 1024)
    nt = T // tt
    ns = N // slots
    per = ns // tn

    def body(x_ref, dy_ref, *rest):
        if colsum:
            dw_ref, cs_ref, acc, cacc = rest
        else:
            dw_ref, acc = rest
        t = pl.program_id(1)

        @pl.when(t == 0)
        def _():
            acc[...] = jnp.zeros_like(acc)
            if colsum:
                cacc[...] = jnp.zeros_like(cacc)

        dyv = dy_ref[...]
        acc[...] += _dot_tn(x_ref[...].astype(BF16), dyv.astype(BF16))
        if colsum:
            cacc[...] += jnp.sum(dyv.astype(F32), axis=0, keepdims=True)

        @pl.when(t == nt - 1)
        def _():
            dw_ref[...] = acc[...].astype(BF16)
            if colsum:
                cs_ref[...] = cacc[...]

    out_specs = [pl.BlockSpec((None, K, tn), lambda j, t: (j // per, 0, j % per))]
    out_shape = [jax.ShapeDtypeStruct((slots, K, ns), BF16)]
    scratch = [pltpu.VMEM((K, tn), F32)]
    if colsum:
        out_specs.append(pl.BlockSpec((1, tn), lambda j, t: (0, j)))
        out_shape.append(jax.ShapeDtypeStruct((1, N), F32))
        scratch.append(pltpu.VMEM((1, tn), F32))
    res = pl.pallas_call(
        body, name=name, grid=(N // tn, nt),
        in_specs=[pl.BlockSpec((tt, K), lambda j, t: (t, 0)), pl.BlockSpec((tt, tn), lambda j, t: (t, j))],
        out_specs=out_specs, out_shape=out_shape, scratch_shapes=scratch,
        compiler_params=_cp("parallel", "arbitrary"),
    )(x, dy)
    return tuple(res) if colsum else res[0]


def mmT_swiglu_bwd(dh, w, gu, name, after=()):
    T = dh.shape[0]
    half = DFF // 2

    def body(dh_ref, w_ref, gu_ref, *rest):
        du_ref = rest[-1]
        dhb = dh_ref[...].astype(BF16)
        for s in range(2):
            lo, hi = s * half, (s + 1) * half
            df = _dot_nt(dhb, w_ref[lo:hi, :])
            gate = gu_ref[:, lo:hi].astype(F32)
            up = gu_ref[:, DFF + lo:DFF + hi].astype(F32)
            sg = _sigmoid(gate)
            du_ref[:, lo:hi] = (df * up * sg * (1.0 + gate * (1.0 - sg))).astype(BF16)
            du_ref[:, DFF + lo:DFF + hi] = (df * gate * sg).astype(BF16)

    return pl.pallas_call(
        body, name=name, grid=(T // TM,),
        in_specs=[_row(TM, D), _weight((DFF, D)), _row(TM, 2 * DFF)] + [ANY] * len(after),
        out_specs=_row(TM, 2 * DFF), out_shape=jax.ShapeDtypeStruct((T, 2 * DFF), BF16),
        compiler_params=_cp("parallel"),
    )(dh, w, gu, *after)


def mmT_rmsbwd(du, w, h, g, gl, dh_in, name):
    T, N = du.shape
    slots = w.shape[0]
    ns = N // slots

    def body(du_ref, w_ref, h_ref, g_ref, di_ref, dh_ref, dg_ref):
        i = pl.program_id(0)

        @pl.when(i == 0)
        def _():
            dg_ref[...] = jnp.zeros_like(dg_ref)

        dxn = _dot_nt(du_ref[:, 0:ns], w_ref[0])
        for s in range(1, slots):
            dxn = dxn + _dot_nt(du_ref[:, s * ns:(s + 1) * ns], w_ref[s])
        x = h_ref[...]
        r = _rms(x)
        xh = x * r
        dg_ref[0:1, :] += jnp.sum(dxn * xh, axis=0, keepdims=True)
        dxh = dxn * g_ref[...]
        dh_ref[...] = di_ref[...] + r * (dxh - xh * jnp.mean(dxh * xh, axis=-1, keepdims=True))

    return pl.pallas_call(
        body, name=name, grid=(T // TM,),
        in_specs=[_row(TM, N), _weight((slots, D, ns)), _row(TM, D), _layer((1, D), gl), _row(TM, D)],
        out_specs=[_row(TM, D), _const((8, D))],
        out_shape=[jax.ShapeDtypeStruct((T, D), F32), jax.ShapeDtypeStruct((8, D), F32)],
        compiler_params=_cp("arbitrary"),
    )(du, w, h, g, dh_in)


def mmT(dh, w, name):
    T = dh.shape[0]
    N = w.shape[0]

    def body(dh_ref, w_ref, o_ref):
        o_ref[...] = _dot_nt(dh_ref[...].astype(BF16), w_ref[...]).astype(BF16)

    return pl.pallas_call(
        body, name=name, grid=(T // TM,),
        in_specs=[_row(TM, D), _weight((N, D))],
        out_specs=_row(TM, N), out_shape=jax.ShapeDtypeStruct((T, N), BF16),
        compiler_params=_cp("parallel"),
    )(dh, w)


def mmT_lnbwd(dh, w, y, sm, l, name):
    T = dh.shape[0]

    def body(dh_ref, w_ref, y_ref, sm_ref, dy_ref, st_ref):
        i = pl.program_id(0)

        @pl.when(i == 0)
        def _():
            st_ref[...] = jnp.zeros_like(st_ref)

        ds = _dot_nt(dh_ref[...].astype(BF16), w_ref[...])
        y = y_ref[...]
        mu = jnp.mean(y, axis=-1, keepdims=True)
        yc = y - mu
        rstd = lax.rsqrt(jnp.mean(yc * yc, axis=-1, keepdims=True) + EPS)
        xh = yc * rstd
        gam = sm_ref[32:33, :]
        z = xh * gam + sm_ref[33:34, :]
        sg = _sigmoid(z)
        dz = ds * sg * (1.0 + z * (1.0 - sg))
        st_ref[0:1, :] += jnp.sum(dz * xh, axis=0, keepdims=True)
        st_ref[1:2, :] += jnp.sum(dz, axis=0, keepdims=True)
        dxh = dz * gam
        dy = rstd * (dxh - jnp.mean(dxh, axis=-1, keepdims=True) - xh * jnp.mean(dxh * xh, axis=-1, keepdims=True))
        st_ref[2:3, :] += jnp.sum(dy, axis=0, keepdims=True)
        dy_ref[...] = dy

    return pl.pallas_call(
        body, name=name, grid=(T // TM,),
        in_specs=[_row(TM, D), _weight((D, D)), _row(TM, D), _layer((40, D), l)],
        out_specs=[_row(TM, D), _const((8, D))],
        out_shape=[jax.ShapeDtypeStruct((T, D), F32), jax.ShapeDtypeStruct((8, D), F32)],
        compiler_params=_cp("arbitrary"),
    )(dh, w, y, sm)


def dwconv_glu_bwd(dy, a, u, sm, smrev, l, name):
    T = dy.shape[0]
    nb = TCV // HALO
    last = T // HALO - 1

    def body(dy_ref, dyn_ref, a_ref, ap_ref, u_ref, sm_ref, rev_ref, du_ref, dw_ref, shd, sha, da):
        i = pl.program_id(0)

        @pl.when(i == 0)
        def _():
            dw_ref[...] = jnp.zeros_like(dw_ref)

        shd[0, 0:TCV, :] = dy_ref[...]
        shd[0, TCV:TCV + HALO, :] = jnp.where(i < pl.num_programs(0) - 1, dyn_ref[...], 0.0)
        sha[0, 0:HALO, :] = jnp.where(i > 0, ap_ref[...], 0.0)
        sha[0, HALO:HALO + TCV, :] = a_ref[...]
        _make_shifts(shd)
        _make_shifts(sha)
        _conv_taps(shd, rev_ref, da, 0)
        LB = 512
        for c0 in range(0, D, LB):
            for k in range(CONV_W):
                acc = jnp.zeros((SUB, LB), F32)
                for r0 in range(0, TCV, SUB):
                    acc = acc + dy_ref[r0:r0 + SUB, c0:c0 + LB] * _shifted(sha, HALO - (CONV_W - 1) + k + r0, SUB,
                                                                           slice(c0, c0 + LB))
                dw_ref[k:k + 1, c0:c0 + LB] += jnp.sum(acc, axis=0, keepdims=True)
        dav = da[...]
        u1 = u_ref[:, 0:D].astype(F32)
        sg = _sigmoid(u_ref[:, D:2 * D].astype(F32))
        du_ref[:, 0:D] = (dav * sg).astype(BF16)
        du_ref[:, D:2 * D] = (dav * u1 * sg * (1.0 - sg)).astype(BF16)

    return pl.pallas_call(
        body, name=name, grid=(T // TCV,),
        in_specs=[_row(TCV, D), pl.BlockSpec((HALO, D), lambda i: (jnp.minimum((i + 1) * nb, last), 0)),
                  _row(TCV, D), pl.BlockSpec((HALO, D), lambda i: (jnp.maximum(i * nb - 1, 0), 0)),
                  _row(TCV, 2 * D), _layer((40, D), l), _layer((40, D), l)],
        out_specs=[_row(TCV, 2 * D), _const((32, D))],
        out_shape=[jax.ShapeDtypeStruct((T, 2 * D), BF16), jax.ShapeDtypeStruct((32, D), F32)],
        scratch_shapes=[pltpu.VMEM((SUB, TCV + HALO, D), F32), pltpu.VMEM((SUB, TCV + HALO, D), F32),
                        pltpu.VMEM((TCV, D), F32)],
        compiler_params=_cp("arbitrary"),
    )(dy, dy, a, a, u, sm, smrev)


def _rows_tile(R):
    for t in (512, 256, 128, 64, 32, 16, 8):
        if R % t == 0:
            return t
    return R


def add8(own, others, name):
    R, C = own.shape
    tr = _rows_tile(R)

    def body(o_ref, x_ref, out_ref):
        acc = o_ref[...].astype(F32)
        for k in range(7):
            acc = acc + x_ref[k].astype(F32)
        out_ref[...] = acc

    return pl.pallas_call(
        body, name=name, grid=(R // tr,),
        in_specs=[_row(tr, C), pl.BlockSpec((7, tr, C), lambda i: (0, i, 0))], out_specs=_row(tr, C),
        out_shape=jax.ShapeDtypeStruct((R, C), F32), compiler_params=_cp("parallel"),
    )(own, others)


def adamw(w, g, m, v, name):
    R, C = w.shape
    tr = _rows_tile(R)

    def body(w_ref, g_ref, m_ref, v_ref, d_ref, nm_ref, nv_ref):
        gv = g_ref[...]
        nm = ADAM_B1 * m_ref[...] + (1.0 - ADAM_B1) * gv
        nv = ADAM_B2 * v_ref[...] + (1.0 - ADAM_B2) * (gv * gv)
        m_hat = nm / (1.0 - ADAM_B1 ** ADAM_STEP)
        v_hat = nv / (1.0 - ADAM_B2 ** ADAM_STEP)
        d_ref[...] = -ADAM_LR * (m_hat / (jnp.sqrt(v_hat) + ADAM_EPS) + ADAM_WD * w_ref[...])
        nm_ref[...] = nm
        nv_ref[...] = nv

    sd = jax.ShapeDtypeStruct((R, C), F32)
    return pl.pallas_call(
        body, name=name, grid=(R // tr,),
        in_specs=[_row(tr, C)] * 4, out_specs=[_row(tr, C)] * 3, out_shape=[sd, sd, sd],
        compiler_params=_cp("parallel"),
    )(w, g, m, v)


ANY = pl.BlockSpec(memory_space=pl.ANY)
HBM = pl.BlockSpec(memory_space=pltpu.HBM)
SEM = pl.BlockSpec(memory_space=pltpu.SEMAPHORE)
EFFECT = pltpu.SideEffectType.DATAFLOW_SIDE_EFFECTING


def _place():
    x, y, c = lax.axis_index("x"), lax.axis_index("y"), lax.axis_index("c")
    chips = [(1 - x, y), (x, 1 - y), (1 - x, 1 - y)]
    return x, y, c, chips


def _copy(src, dst, send, recv, k, to):
    return pltpu.make_async_remote_copy(src_ref=src, dst_ref=dst, send_sem=send.at[k], recv_sem=recv.at[k],
                                        device_id=to, device_id_type=MESH)


def xchg_start(name, bufs, plan, n, after=()):
    nb = len(bufs)

    na = len(after)

    def body(*refs):
        send, recv, token = refs[nb + na], refs[nb + na + 1], refs[-1]
        for k, (src, dst, to) in enumerate(plan(refs[:nb])):
            _copy(src, dst, send, recv, k, to).start()
        token[...] = jnp.zeros_like(token)

    outs = pl.pallas_call(
        body, name=name,
        out_shape=(pltpu.SemaphoreType.DMA((n,)), pltpu.SemaphoreType.DMA((n,)),
                   *[pltpu.HBM(b.shape, b.dtype) for b in bufs], jax.ShapeDtypeStruct((8, 128), F32)),
        in_specs=[HBM] * nb + [ANY] * na,
        out_specs=(SEM, SEM, *[HBM] * nb, pl.BlockSpec(memory_space=pltpu.VMEM)),
        input_output_aliases={i: 2 + i for i in range(nb)},
        compiler_params=pltpu.CompilerParams(has_side_effects=EFFECT),
    )(*[pltpu.with_memory_space_constraint(b, pltpu.HBM) for b in bufs], *after)
    return dict(name=name, send=outs[0], recv=outs[1], bufs=list(outs[2:2 + nb]), plan=plan), outs[-1]


def xchg_wait(flight, after):
    bufs, plan = flight["bufs"], flight["plan"]
    nb = len(bufs)

    def body(*refs):
        send, recv = refs[nb], refs[nb + 1]
        for k, (src, dst, to) in enumerate(plan(refs[:nb])):
            cp = _copy(src, dst, send, recv, k, to)
            cp.wait_send()
            cp.wait_recv()

    outs = pl.pallas_call(
        body, name=flight["name"] + "_wait",
        out_shape=tuple(pltpu.HBM(b.shape, b.dtype) for b in bufs),
        in_specs=[HBM] * nb + [SEM, SEM] + [ANY] * len(after),
        out_specs=tuple([HBM] * nb), input_output_aliases={i: i for i in range(nb)},
        compiler_params=pltpu.CompilerParams(has_side_effects=EFFECT),
    )(*bufs, flight["send"], flight["recv"], *after)
    return list(outs)


def _flip(k, x, y, c):
    return ((1 - x) if k & 4 else x, (1 - y) if k & 2 else y, (1 - c) if k & 1 else c)


class WeightGather:
    def __init__(self, shards, groups):
        me = 2 * lax.axis_index("x") + lax.axis_index("y")
        self.names = dict(groups)
        self.ici, self.d2d = {}, {}
        self.token = None
        for gname, names in groups:
            nt = len(names)
            srcs = [shards[n] for n in names]
            lands = [lax.dynamic_update_slice(lax.empty((4,) + s.shape, s.dtype), s[None], (me, 0, 0, 0))
                     for s in srcs]

            def plan(refs, nt=nt):
                x, y, c, chips = _place()
                return [(refs[t].at[c], refs[nt + t].at[2 * x + y, c], (cx, cy, c))
                        for t in range(nt) for cx, cy in chips]

            self.ici[gname], self.token = xchg_start(f"ag_ici_{gname}", srcs + lands, plan, 3 * nt,
                                                     after=[] if self.token is None else [self.token])

    def forward(self, gname, after):
        nt = len(self.names[gname])
        lands = xchg_wait(self.ici.pop(gname), after)[nt:]

        def plan(refs):
            x, y, c, chips = _place()
            out = []
            for t in range(nt):
                for cx, cy in chips:
                    piece = refs[t].at[2 * cx + cy, c]
                    out.append((piece, piece, (x, y, 1 - c)))
            return out

        self.d2d[gname], token = xchg_start(f"ag_d2d_{gname}", lands, plan, 3 * nt)
        return token

    def get(self, gname, after):
        lands = xchg_wait(self.d2d.pop(gname), after)
        return dict(zip(self.names[gname], lands))


class GradReduce:
    def __init__(self, kinds):
        self.J = {k: lax.empty((L, 2, a2, b), F32) for k, (L, a2, b) in kinds.items()}
        self.x, self.j = {}, {}

    @staticmethod
    def _where(name):
        kind, _, l = name.partition("_")
        return kind, int(l or 0)

    def send(self, gname, grads, after=()):
        names = list(grads)
        nt = len(names)
        gs = [grads[n] for n in names]
        xs = [lax.empty((7,) + g.shape[2:], g.dtype) for g in gs]

        def plan(refs):
            x, y, c, _ = _place()
            out = []
            for t in range(nt):
                for k in range(1, 8):
                    px, py, pc = _flip(k, x, y, c)
                    out.append((refs[t].at[2 * px + py, pc], refs[nt + t].at[k - 1], (px, py, pc)))
            return out

        flight, token = xchg_start(f"rs_x_{gname}", gs + xs, plan, 7 * nt, after=after)
        self.x[gname] = (names, flight)
        return token

    def reduce(self, gname, after):
        names, flight = self.x.pop(gname)
        nt = len(names)
        bufs = xchg_wait(flight, after)
        me, c = 2 * lax.axis_index("x") + lax.axis_index("y"), lax.axis_index("c")
        hs = []
        for t, n in enumerate(names):
            g = bufs[t]
            own = lax.dynamic_slice(g, (me, c, 0, 0), (1, 1) + g.shape[2:])[0, 0]
            hs.append(add8(own, bufs[nt + t], f"rs_add_{n}"))
        where = [self._where(n) for n in names]

        def plan(refs):
            x, y, c, _ = _place()
            return [(refs[t], refs[nt + t].at[where[t][1], c], (x, y, 1 - c)) for t in range(nt)]

        flight, token = xchg_start(f"rs_join_{gname}", hs + [self.J[k] for k, _ in where], plan, nt)
        self.j[gname] = (where, flight)
        return token

    def finish(self, gname, after):
        where, flight = self.j.pop(gname)
        nt = len(where)
        bufs = xchg_wait(flight, after)
        c = lax.axis_index("c")
        for t, (kind, l) in enumerate(where):
            self.J[kind] = lax.dynamic_update_slice(bufs[nt + t], bufs[t][None, None], (l, c, 0, 0))


def allreduce_small(v):
    R = v.shape[0]

    def body(v_ref, o_ref, all_ref, send, recv):
        x, y, c, _ = _place()
        me = 4 * x + 2 * y + c
        all_ref[me] = v_ref[...]
        cps = []
        for k in range(1, 8):
            cp = _copy(v_ref, all_ref.at[me], send, recv, k - 1, _flip(k, x, y, c))
            cp.start()
            cps.append(cp)
        for k in range(1, 8):
            px, py, pc = _flip(k, x, y, c)
            _copy(v_ref, all_ref.at[4 * px + 2 * py + pc], send, recv, k - 1, (px, py, pc)).wait_recv()
        for cp in cps:
            cp.wait_send()
        acc = all_ref[0]
        for d in range(1, 8):
            acc = acc + all_ref[d]
        o_ref[...] = acc

    return pl.pallas_call(
        body, name="allreduce_small",
        in_specs=[pl.BlockSpec(memory_space=pltpu.VMEM)], out_specs=pl.BlockSpec(memory_space=pltpu.VMEM),
        out_shape=jax.ShapeDtypeStruct((R, D), F32),
        scratch_shapes=[pltpu.VMEM((8, R, D), F32), pltpu.SemaphoreType.DMA((7,)), pltpu.SemaphoreType.DMA((7,))],
        compiler_params=pltpu.CompilerParams(has_side_effects=True, vmem_limit_bytes=VMEM_LIMIT),
    )(v)


AG_GROUPS = (("a0", ("pw1_0", "pw2_0", "small")), ("f0", ("up_0", "down_0")),
             ("l1", ("pw1_1", "pw2_1", "up_1", "down_1")), ("l2", ("kv", "wq_0", "wo_0", "up_2", "down_2")),
             ("l3", ("wq_1", "wo_1", "up_3", "down_3")))


def _bucket_table():
    qi = np.arange(BLK)[:, None]
    kj = np.arange(2 * BLK)[None, :]
    d = np.maximum(qi + BLK - kj, 0)
    max_exact = N_BUCKETS // 2
    log_ratio = (np.log(np.maximum(d, 1).astype(np.float32) / np.float32(max_exact))
                 / np.float32(math.log(MAX_DISTANCE / max_exact))).astype(np.float32)
    large = max_exact + (log_ratio * np.float32(N_BUCKETS - max_exact)).astype(np.int32)
    large = np.minimum(large, N_BUCKETS - 1)
    return np.where(d < max_exact, d, large).astype(np.int32)


def _heads_major(a, nh):
    T = a.shape[0]
    return a.reshape(T, nh, HD).transpose(1, 0, 2)


def _heads_minor(a):
    nh, T, _ = a.shape
    return a.transpose(1, 0, 2).reshape(T, nh * HD)


def _slots(land):
    return land.reshape(4, 2 * land.shape[2], land.shape[3])


def _rows(land):
    return land.reshape(8 * land.shape[2], land.shape[3])


def _gview(g):
    s, K, n = g.shape
    return g.reshape(4, 2, K // 2, n) if s == 4 else g.reshape(4, 2, K // 8, n)


def _gate(a, token):
    return a + token[0, 0]


def _conv_small(f_small):
    fs = f_small.transpose(1, 2, 0, 3).reshape(2, 40, D)
    b_pw1 = f_small[:, :, 35:37, :].transpose(1, 0, 2, 3).reshape(2, 1, 2 * D)
    rev = jnp.concatenate([fs[:, CONV_W - 1::-1], jnp.zeros((2, 40 - CONV_W, D), F32)], axis=1)
    return dict(conv=fs, conv_rev=rev, b_pw1=b_pw1, b_pw2=fs[:, 34:35])


def run_step(x, target, P, ag, rs):
    T = x.shape[0]
    zero = jnp.zeros((1, 1, D), F32)
    nm, nf = P["norm_mix"], P["norm_ffn"]
    ag.forward("a0", [ag.token])
    W = ag.get("a0", [])
    sm = _conv_small(W["small"])
    h = x
    saved = []
    for l in range(2):
        xn, u, a = norm_mm_glu(h, nm, l, _slots(W[f"pw1_{l}"]), sm["b_pw1"], f"f_pw1_{l}")
        y, s = dwconv_ln_silu(a, sm["conv"], l, f"f_conv_{l}")
        b2 = sm["b_pw2"]
        if l == 0:
            b2 = _gate(b2, ag.forward("f0", [s]))
        h1 = mm_bias_res(s, _rows(W[f"pw2_{l}"]), b2, l, h, f"f_pw2_{l}")
        if l == 0:
            W.update(ag.get("f0", [h1]))
        xn2, gu, f = norm_mm_swiglu(h1, nf, l, _slots(W[f"up_{l}"]), f"f_up_{l}")
        nxt = "l1" if l == 0 else "l2"
        h2 = mm_bias_res(f, _rows(W[f"down_{l}"]), _gate(zero, ag.forward(nxt, [f])), 0, h1, f"f_down_{l}")
        W.update(ag.get(nxt, [h2]))
        saved.append(dict(h=h, xn=xn, u=u, a=a, y=y, s=s, h1=h1, xn2=xn2, gu=gu, f=f))
        h = h2
    h_kv = h
    kvn, kv = norm_mm(h, P["norm_kv"], 0, _rows(W["kv"]), "f_kv")
    kp = jnp.pad(_heads_major(kv[:, :N_KV * HD], N_KV), ((0, 0), (BLK, 0), (0, 0)))
    vp = jnp.pad(_heads_major(kv[:, N_KV * HD:], N_KV), ((0, 0), (BLK, 0), (0, 0)))
    kvt = jnp.pad(kv.T.reshape(2, N_KV, HD, T), ((0, 0), (0, 0), (0, 0), (BLK, 0)))
    kt, vt = kvt[0], kvt[1]
    bucket = _bucket_table()
    onehot = jnp.asarray(np.eye(N_BUCKETS, dtype=np.float32)[bucket])
    bias = jnp.einsum("qkb,bh->hkq", onehot, P["rel_bias"], precision=lax.Precision.HIGHEST)
    bias = bias.reshape(N_KV, GROUP, 2 * BLK, BLK).transpose(0, 2, 1, 3).reshape(1, N_KV, 2 * BLK, QW)
    bias = bias + jnp.asarray(band_mask())[:, None]
    for j in range(2):
        l = 2 + j
        xn, q = norm_mm(h, nm, l, _rows(W[f"wq_{j}"]), f"f_q_{j}", scale=HD ** -0.5)
        qh = q.T.reshape(N_KV, GROUP, HD, T)
        sink = jnp.broadcast_to(P["sinks"][j].reshape(N_KV, GROUP, 1), (N_KV, GROUP, BLK)).reshape(N_KV, 1, QW)
        oh = attn_fwd(qh, kp, vt, bias, sink, f"f_attn_{j}")
        attn = oh.reshape(N_HEADS * HD, T).T
        h1 = mm_bias_res(attn, _rows(W[f"wo_{j}"]), zero, 0, h, f"f_wo_{j}")
        xn2, gu, f = norm_mm_swiglu(h1, nf, l, _slots(W[f"up_{l}"]), f"f_up_{l}")
        zg = _gate(zero, ag.forward("l3", [f])) if j == 0 else zero
        h2 = mm_bias_res(f, _rows(W[f"down_{l}"]), zg, 0, h1, f"f_down_{l}")
        if j == 0:
            W.update(ag.get("l3", [h2]))
        saved.append(dict(h=h, xn=xn, qh=qh, oh=oh, sink=sink, attn=attn, h1=h1, xn2=xn2, gu=gu, f=f))
        h = h2

    dh, st_final = final_loss(h, P["norm_final"], target, "loss_head")

    S = dict(norm_ffn=[None] * 4, norm_mix=[None] * 4, conv=[None] * 2, taps=[None] * 2, b_pw1=[None] * 2,
             b_pw2=[None] * 2, sinks=[None] * 2)

    def ffn_bwd(dh, sv, l, nf, after=()):
        du = mmT_swiglu_bwd(dh, _rows(W[f"down_{l}"]), sv["gu"], f"b_down_{l}", after)
        gd = mm_dw(sv["f"], dh, f"w_down_{l}", 512, 1)
        gu = mm_dw(sv["xn2"], du, f"w_up_{l}", DFF // 2, 4)
        dh, dg = mmT_rmsbwd(du, _slots(W[f"up_{l}"]), sv["h1"], nf, l, dh, f"b_up_{l}")
        S["norm_ffn"][l] = dg[0]
        return dh, {f"down_{l}": _gview(gd), f"up_{l}": _gview(gu)}

    dk = dv = dbias = None
    sent = []
    for j in (1, 0):
        l = 2 + j
        sv = saved[l]
        dh, grads = ffn_bwd(dh, sv, l, nf, sent)
        dattn = mmT(dh, _rows(W[f"wo_{j}"]), f"b_wo_{j}")
        grads[f"wo_{j}"] = _gview(mm_dw(sv["attn"], dh, f"w_wo_{j}", 512, 1))
        doh = dattn.T.reshape(N_KV, GROUP, HD, T)
        dqh, dkj, dvj, dbj, dsj = attn_bwd(sv["qh"], kp, kt, vp, bias, sv["sink"], sv["oh"], doh, f"b_attn_{j}")
        dq = dqh.reshape(N_HEADS * HD, T).T
        grads[f"wq_{j}"] = _gview(mm_dw(sv["xn"], dq, f"w_q_{j}", 512, 1))
        dh, dg = mmT_rmsbwd(dq, _rows(W[f"wq_{j}"])[None], sv["h"], nm, l, dh, f"b_q_{j}")
        S["norm_mix"][l] = dg[0]
        S["sinks"][j] = jnp.sum(dsj.reshape(N_HEADS, BLK), axis=1)
        dk = dkj if dk is None else dk + dkj
        dv = dvj if dv is None else dv + dvj
        dbias = dbj if dbias is None else dbias + dbj
        if j == 1:
            sent = [rs.send("l3", grads)]

    dkv = jnp.concatenate([_heads_minor(dk[:, BLK:]), _heads_minor(dv[:, BLK:])], axis=1).astype(BF16)
    grads["kv"] = _gview(mm_dw(kvn, dkv, "w_kv", 512, 1))
    dh, dg = mmT_rmsbwd(dkv, _rows(W["kv"])[None], h_kv, P["norm_kv"], 0, dh, "b_kv")
    S["norm_kv"] = dg[0]
    dbh = dbias.reshape(N_KV, 2 * BLK, GROUP, BLK)
    S["rel_bias"] = jnp.einsum("vkgq,qkb->bvg", dbh, onehot, precision=lax.Precision.HIGHEST).reshape(N_BUCKETS, N_HEADS)
    sent = [rs.send("l2", grads)]
    nf = _gate(nf, rs.reduce("l3", [dh]))

    for l in (1, 0):
        sv = saved[l]
        dh, grads = ffn_bwd(dh, sv, l, nf, sent)
        conv = sm["conv"]
        if l == 0:
            conv = _gate(conv, rs.send("f0", grads))
            grads = {}
        dy, st = mmT_lnbwd(dh, _rows(W[f"pw2_{l}"]), sv["y"], conv, l, f"b_pw2_{l}")
        g2, S["b_pw2"][l] = mm_dw(sv["s"], dh, f"w_pw2_{l}", 512, 1, colsum=True)
        du, dtaps = dwconv_glu_bwd(dy, sv["a"], sv["u"], sm["conv"], sm["conv_rev"], l, f"b_conv_{l}")
        S["conv"][l] = st[0:3]
        S["taps"][l] = dtaps[0:CONV_W]
        if l == 0:
            rs.finish("l2", [du])
            nm = _gate(nm, rs.reduce("l1", [du]))
        g1, S["b_pw1"][l] = mm_dw(sv["xn"], du, f"w_pw1_{l}", 512, 4, colsum=True)
        grads[f"pw2_{l}"], grads[f"pw1_{l}"] = _gview(g2), _gview(g1)
        dh, dg = mmT_rmsbwd(du, _slots(W[f"pw1_{l}"]), sv["h"], nm, l, dh, f"b_pw1_{l}")
        S["norm_mix"][l] = dg[0]
        if l == 1:
            sent = [rs.send("l1", grads)]
            rs.finish("l3", [dh])
            nf = _gate(nf, rs.reduce("l2", [dh]))
    S["norm_final"] = st_final[0]
    S["loss"] = st_final[1]
    return grads, dh, S


R_CONV = 37
R_SMALL = 88


def _pack_small(S):
    rows = []
    for l in range(2):
        rows += [S["taps"][l], S["conv"][l][2:3], S["conv"][l][0:2], S["b_pw2"][l], S["b_pw1"][l].reshape(2, D)]
    rows += [jnp.stack(S["norm_mix"]), jnp.stack(S["norm_ffn"]), S["norm_kv"][None], S["norm_final"][None]]
    tail = jnp.concatenate([jnp.stack(S["sinks"]).reshape(-1), S["rel_bias"].reshape(-1)])
    rows += [jnp.pad(tail, (0, D - tail.shape[0]))[None], S["loss"][None]]
    v = jnp.concatenate(rows, axis=0)
    return jnp.pad(v, ((0, R_SMALL - v.shape[0]), (0, 0)))


def kernel(x, norm_mix, norm_ffn, conv_w_pw1, conv_b_pw1, conv_w_dw, conv_b_dw, conv_ln_g, conv_ln_b, conv_w_pw2, conv_b_pw2, norm_kv, w_kv, w_q, w_o, sinks, rel_bias, ffn_w_up, ffn_w_down, norm_final, loss_target, m_norm_mix, m_norm_ffn, m_conv_w_pw1, m_conv_b_pw1, m_conv_w_dw, m_conv_b_dw, m_conv_ln_g, m_conv_ln_b, m_conv_w_pw2, m_conv_b_pw2, m_norm_kv, m_w_kv, m_w_q, m_w_o, m_sinks, m_rel_bias, m_ffn_w_up, m_ffn_w_down, m_norm_final, v_norm_mix, v_norm_ffn, v_conv_w_pw1, v_conv_b_pw1, v_conv_w_dw, v_conv_b_dw, v_conv_ln_g, v_conv_ln_b, v_conv_w_pw2, v_conv_b_pw2, v_norm_kv, v_w_kv, v_w_q, v_w_o, v_sinks, v_rel_bias, v_ffn_w_up, v_ffn_w_down, v_norm_final):
    me = 2 * lax.axis_index("x") + lax.axis_index("y")
    weights = dict(norm_mix=norm_mix, norm_ffn=norm_ffn, conv_w_pw1=conv_w_pw1, conv_b_pw1=conv_b_pw1,
                   conv_w_dw=conv_w_dw, conv_b_dw=conv_b_dw, conv_ln_g=conv_ln_g, conv_ln_b=conv_ln_b,
                   conv_w_pw2=conv_w_pw2, conv_b_pw2=conv_b_pw2, norm_kv=norm_kv, w_kv=w_kv, w_q=w_q, w_o=w_o,
                   sinks=sinks, rel_bias=rel_bias, ffn_w_up=ffn_w_up, ffn_w_down=ffn_w_down, norm_final=norm_final)
    mom_m = dict(norm_mix=m_norm_mix, norm_ffn=m_norm_ffn, conv_w_pw1=m_conv_w_pw1, conv_b_pw1=m_conv_b_pw1,
                 conv_w_dw=m_conv_w_dw, conv_b_dw=m_conv_b_dw, conv_ln_g=m_conv_ln_g, conv_ln_b=m_conv_ln_b,
                 conv_w_pw2=m_conv_w_pw2, conv_b_pw2=m_conv_b_pw2, norm_kv=m_norm_kv, w_kv=m_w_kv, w_q=m_w_q,
                 w_o=m_w_o, sinks=m_sinks, rel_bias=m_rel_bias, ffn_w_up=m_ffn_w_up, ffn_w_down=m_ffn_w_down,
                 norm_final=m_norm_final)
    mom_v = dict(norm_mix=v_norm_mix, norm_ffn=v_norm_ffn, conv_w_pw1=v_conv_w_pw1, conv_b_pw1=v_conv_b_pw1,
                 conv_w_dw=v_conv_w_dw, conv_b_dw=v_conv_b_dw, conv_ln_g=v_conv_ln_g, conv_ln_b=v_conv_ln_b,
                 conv_w_pw2=v_conv_w_pw2, conv_b_pw2=v_conv_b_pw2, norm_kv=v_norm_kv, w_kv=v_w_kv, w_q=v_w_q,
                 w_o=v_w_o, sinks=v_sinks, rel_bias=v_rel_bias, ffn_w_up=v_ffn_w_up, ffn_w_down=v_ffn_w_down,
                 norm_final=v_norm_final)

    def halves(a):
        return a.astype(BF16).reshape(2, a.shape[0] // 2, a.shape[1])

    shards = {"kv": halves(w_kv)}
    for l in range(2):
        shards[f"pw1_{l}"], shards[f"pw2_{l}"] = halves(conv_w_pw1[l]), halves(conv_w_pw2[l])
        shards[f"wq_{l}"], shards[f"wo_{l}"] = halves(w_q[l]), halves(w_o[l])
    for l in range(4):
        shards[f"up_{l}"], shards[f"down_{l}"] = halves(ffn_w_up[l]), halves(ffn_w_down[l])
    shards["small"] = jnp.concatenate(
        [conv_w_dw, conv_b_dw[:, None], conv_ln_g[:, None], conv_ln_b[:, None], conv_b_pw2[:, None],
         conv_b_pw1.reshape(2, 2, 256), jnp.zeros((2, 3, 256), F32)], axis=1)
    ag = WeightGather(shards, AG_GROUPS)
    big = {"conv_w_pw1": "pw1", "conv_w_pw2": "pw2", "w_q": "wq", "w_o": "wo", "ffn_w_up": "up",
           "ffn_w_down": "down", "w_kv": "kv"}
    rs = GradReduce({"pw1": (2, 512, 512), "pw2": (2, 128, D), "wq": (2, 128, D), "wo": (2, 128, D),
                     "up": (4, 512, DFF // 2), "down": (4, DFF // 8, D), "kv": (1, 128, 512)})

    P = dict(norm_mix=norm_mix[:, None], norm_ffn=norm_ffn[:, None], norm_kv=norm_kv[None, None],
             norm_final=norm_final[None], sinks=sinks, rel_bias=rel_bias)
    last, grad_x, S = run_step(x[0], loss_target[0], P, ag, rs)

    rs.finish("l1", [grad_x])
    vsum = allreduce_small(_gate(_pack_small(S), rs.reduce("f0", [grad_x])))
    token = rs.send("c0", last, after=[vsum])
    col = lambda a: lax.dynamic_slice_in_dim(a, me * 256, 256, axis=-1)
    grads = {}
    for l in range(2):
        base = l * R_CONV
        grads.setdefault("conv_w_dw", []).append(col(vsum[base:base + 31]))
        grads.setdefault("conv_b_dw", []).append(col(vsum[base + 31]))
        grads.setdefault("conv_ln_g", []).append(col(vsum[base + 32]))
        grads.setdefault("conv_ln_b", []).append(col(vsum[base + 33]))
        grads.setdefault("conv_b_pw2", []).append(col(vsum[base + 34]))
        grads.setdefault("conv_b_pw1", []).append(
            lax.dynamic_slice_in_dim(vsum[base + 35:base + 37].reshape(2 * D), me * 512, 512, axis=0))
    grads = {k: jnp.stack(v) for k, v in grads.items()}
    base = 2 * R_CONV
    grads["norm_mix"] = vsum[base:base + 4]
    grads["norm_ffn"] = vsum[base + 4:base + 8]
    grads["norm_kv"] = vsum[base + 8]
    grads["norm_final"] = vsum[base + 9]
    grads["sinks"] = vsum[base + 10, 0:32].reshape(2, 16)
    grads["rel_bias"] = vsum[base + 10, 32:32 + 512].reshape(32, 16)
    loss = vsum[base + 11, 0]

    delta, new_m, new_v = {}, {}, {}
    rest = [n for n in weights if n not in big]

    def pack(dct):
        flat = jnp.concatenate([dct[n].reshape(-1) for n in rest])
        return jnp.pad(flat, (0, (-flat.shape[0]) % (8 * 128))).reshape(-1, 128)

    d, nm, nv = adamw(pack(weights), _gate(pack(grads), token), pack(mom_m), pack(mom_v), "adamw_small")
    off = 0
    for n in rest:
        shp = weights[n].shape
        sz = int(np.prod(shp))
        delta[n] = d.reshape(-1)[off:off + sz].reshape(shp)
        new_m[n] = nm.reshape(-1)[off:off + sz].reshape(shp)
        new_v[n] = nv.reshape(-1)[off:off + sz].reshape(shp)
        off += sz

    def update(n):
        shp = weights[n].shape
        r2 = (int(np.prod(shp[:-1])), shp[-1])
        grads[n] = rs.J[big[n]].reshape(shp)
        d, nm, nv = adamw(weights[n].reshape(r2), grads[n].reshape(r2), mom_m[n].reshape(r2), mom_v[n].reshape(r2),
                          f"adamw_{n}")
        delta[n], new_m[n], new_v[n] = d.reshape(shp), nm.reshape(shp), nv.reshape(shp)

    rs.finish("f0", [vsum])
    for n in ("ffn_w_up", "ffn_w_down"):
        update(n)
    rs.reduce("c0", [delta["ffn_w_up"], delta["ffn_w_down"]])
    for n in ("w_q", "w_o", "w_kv"):
        update(n)
    rs.finish("c0", [delta["w_kv"]])
    for n in ("conv_w_pw1", "conv_w_pw2"):
        update(n)

    order = list(weights)
    return (loss, grad_x[None], *[grads[n] for n in order], *[delta[n] for n in order],
            *[new_m[n] for n in order], *[new_v[n] for n in order])
```

```python
import functools
import math

import numpy as np
import jax
import jax.numpy as jnp
from jax import lax
from jax.experimental import pallas as pl
from jax.experimental.pallas import tpu as pltpu

F32 = jnp.float32
BF16 = jnp.bfloat16
MESH = pl.DeviceIdType.MESH

D = 1024
DFF = 2816
N_HEADS = 16
N_KV = 4
GROUP = 4
HD = 64
BLK = 128
CONV_W = 31
HALO = 32
N_BUCKETS = 32
MAX_DISTANCE = 128
EPS = 1e-6
NEG_INF = -1e30
TM = 512
TCV = 256
VMEM_LIMIT = 56 * 2 ** 20

ADAM_LR, ADAM_B1, ADAM_B2, ADAM_EPS, ADAM_WD, ADAM_STEP = 0.001, 0.9, 0.999, 1e-08, 0.01, 10


def _cp(*sem):
    return pltpu.CompilerParams(dimension_semantics=sem, vmem_limit_bytes=VMEM_LIMIT)


def _sigmoid(x):
    return 1.0 / (1.0 + jnp.exp(-x))


def _row(tm, n):
    return pl.BlockSpec((tm, n), lambda i: (i, 0))


def _const(shape):
    nd = len(shape)
    return pl.BlockSpec(shape, lambda i: (0,) * nd)


def _weight(shape):
    nd = len(shape)
    return pl.BlockSpec(shape, lambda i: (0,) * nd, pipeline_mode=pl.Buffered(1))


def _layer(shape, l):
    nd = len(shape)
    return pl.BlockSpec((None,) + tuple(shape), lambda i: (l,) + (0,) * nd)


def _dot(a, b):
    return jnp.dot(a, b, preferred_element_type=F32)


def _dot_nt(a, b):
    return lax.dot_general(a, b, (((1,), (1,)), ((), ())), preferred_element_type=F32)


def _dot_tn(a, b):
    return lax.dot_general(a, b, (((0,), (0,)), ((), ())), preferred_element_type=F32)


def _rms(x):
    return lax.rsqrt(jnp.mean(x * x, axis=-1, keepdims=True) + EPS)


def norm_mm_glu(h, g, l, w, b, name):
    T = h.shape[0]
    ns = w.shape[-1]

    def body(h_ref, g_ref, w_ref, b_ref, xn_ref, u_ref, a_ref):
        x = h_ref[...]
        xn = (x * _rms(x) * g_ref[...]).astype(BF16)
        xn_ref[...] = xn
        for s in range(2):
            lo, hi = s * ns, (s + 1) * ns
            u1 = _dot(xn, w_ref[s]) + b_ref[:, lo:hi]
            u2 = _dot(xn, w_ref[2 + s]) + b_ref[:, D + lo:D + hi]
            u_ref[:, lo:hi] = u1.astype(BF16)
            u_ref[:, D + lo:D + hi] = u2.astype(BF16)
            a_ref[:, lo:hi] = u1 * _sigmoid(u2)

    return pl.pallas_call(
        body, name=name, grid=(T // TM,),
        in_specs=[_row(TM, D), _layer((1, D), l), _weight((4, D, ns)), _layer((1, 2 * D), l)],
        out_specs=[_row(TM, D), _row(TM, 2 * D), _row(TM, D)],
        out_shape=[jax.ShapeDtypeStruct((T, D), BF16), jax.ShapeDtypeStruct((T, 2 * D), BF16),
                   jax.ShapeDtypeStruct((T, D), F32)],
        compiler_params=_cp("parallel"),
    )(h, g, w, b)


SUB = 8


def _make_shifts(sh):
    n = TCV + HALO - SUB
    for r in range(1, SUB):
        for r0 in range(0, n, 40):
            sh[r, r0:r0 + 40, :] = sh[0, pl.ds(r + r0, 40), :]


def _shifted(sh, off, rows, cols):
    return sh[off % SUB, pl.ds(off - off % SUB, rows), cols]


def _conv_taps(sh, w_ref, out_ref, first):
    RB, LB = 32, 512
    for r0 in range(0, TCV, RB):
        for c0 in range(0, out_ref.shape[1], LB):
            acc = jnp.zeros((RB, LB), F32)
            for k in range(CONV_W):
                acc = acc + w_ref[k:k + 1, c0:c0 + LB] * _shifted(sh, first + k + r0, RB, slice(c0, c0 + LB))
            out_ref[r0:r0 + RB, c0:c0 + LB] = acc


def dwconv_ln_silu(a, sm, l, name):
    T = a.shape[0]
    nb = TCV // HALO

    def body(cur_ref, prev_ref, sm_ref, y_ref, s_ref, sh):
        i = pl.program_id(0)
        sh[0, 0:HALO, :] = jnp.where(i > 0, prev_ref[...], 0.0)
        sh[0, HALO:HALO + TCV, :] = cur_ref[...]
        _make_shifts(sh)
        _conv_taps(sh, sm_ref, y_ref, HALO - (CONV_W - 1))
        y = y_ref[...] + sm_ref[31:32, :]
        y_ref[...] = y
        mu = jnp.mean(y, axis=-1, keepdims=True)
        yc = y - mu
        rstd = lax.rsqrt(jnp.mean(yc * yc, axis=-1, keepdims=True) + EPS)
        z = yc * rstd * sm_ref[32:33, :] + sm_ref[33:34, :]
        s_ref[...] = (z * _sigmoid(z)).astype(BF16)

    return pl.pallas_call(
        body, name=name, grid=(T // TCV,),
        in_specs=[_row(TCV, D), pl.BlockSpec((HALO, D), lambda i: (jnp.maximum(i * nb - 1, 0), 0)),
                  _layer((40, D), l)],
        out_specs=[_row(TCV, D), _row(TCV, D)],
        out_shape=[jax.ShapeDtypeStruct((T, D), F32), jax.ShapeDtypeStruct((T, D), BF16)],
        scratch_shapes=[pltpu.VMEM((SUB, TCV + HALO, D), F32)],
        compiler_params=_cp("parallel"),
    )(a, a, sm)


def mm_bias_res(xb, w, b, bl, res, name):
    T, K = xb.shape

    def body(x_ref, w_ref, b_ref, r_ref, o_ref):
        o_ref[...] = _dot(x_ref[...], w_ref[...]) + b_ref[...] + r_ref[...]

    return pl.pallas_call(
        body, name=name, grid=(T // TM,),
        in_specs=[_row(TM, K), _weight((K, D)), _layer((1, D), bl), _row(TM, D)],
        out_specs=_row(TM, D), out_shape=jax.ShapeDtypeStruct((T, D), F32),
        compiler_params=_cp("parallel"),
    )(xb, w, b, res)


def norm_mm_swiglu(h, g, l, w, name):
    T = h.shape[0]
    ns = w.shape[-1]

    def body(h_ref, g_ref, w_ref, xn_ref, gu_ref, f_ref):
        x = h_ref[...]
        xn = (x * _rms(x) * g_ref[...]).astype(BF16)
        xn_ref[...] = xn
        for s in range(2):
            lo, hi = s * ns, (s + 1) * ns
            gate = _dot(xn, w_ref[s])
            up = _dot(xn, w_ref[2 + s])
            gu_ref[:, lo:hi] = gate.astype(BF16)
            gu_ref[:, DFF + lo:DFF + hi] = up.astype(BF16)
            f_ref[:, lo:hi] = (gate * _sigmoid(gate) * up).astype(BF16)

    return pl.pallas_call(
        body, name=name, grid=(T // TM,),
        in_specs=[_row(TM, D), _layer((1, D), l), _weight((4, D, ns))],
        out_specs=[_row(TM, D), _row(TM, 2 * DFF), _row(TM, DFF)],
        out_shape=[jax.ShapeDtypeStruct((T, D), BF16), jax.ShapeDtypeStruct((T, 2 * DFF), BF16),
                   jax.ShapeDtypeStruct((T, DFF), BF16)],
        compiler_params=_cp("parallel"),
    )(h, g, w)


def norm_mm(h, g, gl, w, name, scale=1.0):
    T = h.shape[0]
    N = w.shape[-1]

    def body(h_ref, g_ref, w_ref, xn_ref, o_ref):
        x = h_ref[...]
        xn = (x * _rms(x) * g_ref[...]).astype(BF16)
        xn_ref[...] = xn
        o_ref[...] = (_dot(xn, w_ref[...]) * scale).astype(BF16)

    return pl.pallas_call(
        body, name=name, grid=(T // TM,),
        in_specs=[_row(TM, D), _layer((1, D), gl), _weight((D, N))],
        out_specs=[_row(TM, D), _row(TM, N)],
        out_shape=[jax.ShapeDtypeStruct((T, D), BF16), jax.ShapeDtypeStruct((T, N), BF16)],
        compiler_params=_cp("parallel"),
    )(h, g, w)


QB = 4
QW = GROUP * BLK


def band_mask():
    qi = np.arange(QW)[None, :] % BLK
    kj = np.arange(2 * BLK)[:, None]
    band = ((kj < BLK) & (kj > qi)) | ((kj >= BLK) & (kj - BLK <= qi))
    first = band & (kj >= BLK)
    return np.where(np.stack([first, band]), 0.0, NEG_INF).astype(np.float32)


def _softmax_cols(s, sink):
    m = jnp.maximum(jnp.max(s, axis=0, keepdims=True), sink)
    p = jnp.exp(s - m)
    es = jnp.exp(sink - m)
    inv = 1.0 / (jnp.sum(p, axis=0, keepdims=True) + es)
    return p, inv, es


def _attn_specs(T):
    W = QB * BLK
    qspec = pl.BlockSpec((None, GROUP, HD, W), lambda kv, n: (kv, 0, 0, n))
    kspec = pl.BlockSpec((None, T + BLK, HD), lambda kv, n: (kv, 0, 0))
    ktspec = [pl.BlockSpec((None, HD, W), lambda kv, n: (kv, 0, n)),
              pl.BlockSpec((None, HD, BLK), lambda kv, n: (kv, 0, (n + 1) * QB))]
    bspec = pl.BlockSpec((2, None, 2 * BLK, QW), lambda kv, n: (0, kv, 0, 0))
    sspec = pl.BlockSpec((None, 1, QW), lambda kv, n: (kv, 0, 0))
    return qspec, kspec, ktspec, bspec, sspec


def _attn_block(n, b):
    blk = n * QB + b
    rows = pl.ds(pl.multiple_of(blk * BLK, BLK), 2 * BLK)
    return rows, (jnp.minimum(blk, 1) if b == 0 else 1)


def _band_cols(main_ref, tail_ref, b):
    if b < QB - 1:
        return main_ref[:, b * BLK:(b + 2) * BLK]
    return jnp.concatenate([main_ref[:, b * BLK:], tail_ref[...]], axis=1)


def _heads_side_by_side(ref, qs):
    return jnp.concatenate([ref[g, :, qs] for g in range(GROUP)], axis=1)


def attn_fwd(q, kp, vt, bias, sink, name):
    T = q.shape[3]
    qspec, kspec, ktspec, bspec, sspec = _attn_specs(T)

    def body(q_ref, k_ref, vt_ref, vtt_ref, b_ref, s_ref, o_ref, pb):
        n = pl.program_id(1)
        for b in range(QB):
            rows, table = _attn_block(n, b)
            qs = slice(b * BLK, (b + 1) * BLK)
            st = _dot(k_ref[rows, :], _heads_side_by_side(q_ref, qs))
            for g in range(GROUP):
                hs = slice(g * BLK, (g + 1) * BLK)
                p, inv, _ = _softmax_cols(st[:, hs] + b_ref[table, :, hs], s_ref[:, hs])
                pb[:, hs] = (p * inv).astype(BF16)
            ot = _dot(_band_cols(vt_ref, vtt_ref, b), pb[...])
            for g in range(GROUP):
                o_ref[g, :, qs] = ot[:, g * BLK:(g + 1) * BLK].astype(BF16)

    return pl.pallas_call(
        body, name=name, grid=(N_KV, T // (QB * BLK)),
        in_specs=[qspec, kspec, *ktspec, bspec, sspec], out_specs=qspec,
        out_shape=jax.ShapeDtypeStruct((N_KV, GROUP, HD, T), BF16),
        scratch_shapes=[pltpu.VMEM((2 * BLK, QW), BF16)],
        compiler_params=_cp("parallel", "parallel"),
    )(q, kp, vt, vt, bias, sink)


def attn_bwd(q, kp, kt, vp, bias, sink, o, do, name):
    T = q.shape[3]
    qspec, kspec, ktspec, bspec, sspec = _attn_specs(T)

    def body(q_ref, k_ref, kt_ref, ktt_ref, v_ref, b_ref, s_ref, o_ref, do_ref,
             dq_ref, dk_ref, dv_ref, db_ref, ds_ref, pb, dsb):
        n = pl.program_id(1)

        @pl.when(n == 0)
        def _():
            dk_ref[...] = jnp.zeros_like(dk_ref)
            dv_ref[...] = jnp.zeros_like(dv_ref)
            db_ref[...] = jnp.zeros_like(db_ref)
            ds_ref[...] = jnp.zeros_like(ds_ref)

        for b in range(QB):
            rows, table = _attn_block(n, b)
            qs = slice(b * BLK, (b + 1) * BLK)
            q4 = _heads_side_by_side(q_ref, qs)
            do4 = _heads_side_by_side(do_ref, qs)
            st = _dot(k_ref[rows, :], q4)
            dpt = _dot(v_ref[rows, :], do4)
            for g in range(GROUP):
                hs = slice(g * BLK, (g + 1) * BLK)
                p, inv, es = _softmax_cols(st[:, hs] + b_ref[table, :, hs], s_ref[:, hs])
                probs = p * inv
                delta = jnp.sum(do_ref[g, :, qs].astype(F32) * o_ref[g, :, qs].astype(F32), axis=0, keepdims=True)
                dS = probs * (dpt[:, hs] - delta)
                ds_ref[:, hs] += -(es * inv) * delta
                db_ref[:, hs] += dS
                pb[:, hs] = probs.astype(BF16)
                dsb[:, hs] = dS.astype(BF16)
            dqt = _dot(_band_cols(kt_ref, ktt_ref, b), dsb[...]) * (HD ** -0.5)
            for g in range(GROUP):
                dq_ref[g, :, qs] = dqt[:, g * BLK:(g + 1) * BLK].astype(BF16)
            dk_ref[rows, :] += _dot_nt(dsb[...], q4)
            dv_ref[rows, :] += _dot_nt(pb[...], do4)

    kout = pl.BlockSpec((None, T + BLK, HD), lambda kv, n: (kv, 0, 0))
    dbspec = pl.BlockSpec((None, 2 * BLK, QW), lambda kv, n: (kv, 0, 0))
    return pl.pallas_call(
        body, name=name, grid=(N_KV, T // (QB * BLK)),
        in_specs=[qspec, kspec, *ktspec, kspec, bspec, sspec, qspec, qspec],
        out_specs=[qspec, kout, kout, dbspec, sspec],
        out_shape=[jax.ShapeDtypeStruct((N_KV, GROUP, HD, T), BF16),
                   jax.ShapeDtypeStruct((N_KV, T + BLK, HD), F32), jax.ShapeDtypeStruct((N_KV, T + BLK, HD), F32),
                   jax.ShapeDtypeStruct((N_KV, 2 * BLK, QW), F32), jax.ShapeDtypeStruct((N_KV, 1, QW), F32)],
        scratch_shapes=[pltpu.VMEM((2 * BLK, QW), BF16), pltpu.VMEM((2 * BLK, QW), BF16)],
        compiler_params=_cp("parallel", "arbitrary"),
    )(q, kp, kt, kt, vp, bias, sink, o, do)


def final_loss(h, g, target, name):
    T = h.shape[0]

    def body(h_ref, g_ref, t_ref, dh_ref, st_ref):
        i = pl.program_id(0)

        @pl.when(i == 0)
        def _():
            st_ref[...] = jnp.zeros_like(st_ref)

        x = h_ref[...]
        r = _rms(x)
        xh = x * r
        e = xh * g_ref[...] - t_ref[...]
        loss = 0.5 * jnp.sum(jnp.mean(e * e, axis=-1, keepdims=True))
        dy = e * (1.0 / D)
        st_ref[0:1, :] += jnp.sum(dy * xh, axis=0, keepdims=True)
        lane = lax.broadcasted_iota(jnp.int32, (1, D), 1)
        st_ref[1:2, :] += jnp.where(lane == 0, loss, 0.0)
        dxh = dy * g_ref[...]
        dh_ref[...] = r * (dxh - xh * jnp.mean(dxh * xh, axis=-1, keepdims=True))

    return pl.pallas_call(
        body, name=name, grid=(T // TM,),
        in_specs=[_row(TM, D), _const((1, D)), _row(TM, D)],
        out_specs=[_row(TM, D), _const((8, D))],
        out_shape=[jax.ShapeDtypeStruct((T, D), F32), jax.ShapeDtypeStruct((8, D), F32)],
        compiler_params=_cp("arbitrary"),
    )(h, g, target)


def mm_dw(x, dy, name, tn, slots, colsum=False):
    T, K = x.shape
    split = dy.ndim == 3
    N = dy.shape[-1] * (2 if split else 1)
    tt = min(T, 2048 if K <= 1024 else 1024)
    nt = T // tt
    ns = N // slots
    per = ns // tn

    def body(x_ref, dy_ref, *rest):
        if colsum:
            dw_ref, cs_ref, acc, cacc = rest
        else:
            dw_ref, acc = rest
        t = pl.program_id(1)

        @pl.when(t == 0)
        def _():
            acc[...] = jnp.zeros_like(acc)
            if colsum:
                cacc[...] = jnp.zeros_like(cacc)

        dyv = dy_ref[...]
        acc[...] += _dot_tn(x_ref[...].astype(BF16), dyv.astype(BF16))
        if colsum:
            cacc[...] += jnp.sum(dyv.astype(F32), axis=0, keepdims=True)

        @pl.when(t == nt - 1)
        def _():
            dw_ref[...] = acc[...].astype(BF16)
            if colsum:
                cs_ref[...] = cacc[...]

    if split:
        half = N // 2 // tn
        dy_spec = pl.BlockSpec((None, tt, tn), lambda j, t: (j // half, t, j % half))
    else:
        dy_spec = pl.BlockSpec((tt, tn), lambda j, t: (t, j))
    out_specs = [pl.BlockSpec((None, K, tn), lambda j, t: (j // per, 0, j % per))]
    out_shape = [jax.ShapeDtypeStruct((slots, K, ns), BF16)]
    scratch = [pltpu.VMEM((K, tn), F32)]
    if colsum:
        out_specs.append(pl.BlockSpec((1, tn), lambda j, t: (0, j)))
        out_shape.append(jax.ShapeDtypeStruct((1, N), F32))
        scratch.append(pltpu.VMEM((1, tn), F32))
    res = pl.pallas_call(
        body, name=name, grid=(N // tn, nt),
        in_specs=[pl.BlockSpec((tt, K), lambda j, t: (t, 0)), dy_spec],
        out_specs=out_specs, out_shape=out_shape, scratch_shapes=scratch,
        compiler_params=_cp("parallel", "arbitrary"),
    )(x, dy)
    return tuple(res) if colsum else res[0]


def mmT_swiglu_bwd(dh, w, gu, name, after=()):
    T = dh.shape[0]
    cw = 256

    def body(dh_ref, w_ref, gu_ref, *rest):
        du_ref = rest[-1]
        dhb = dh_ref[...].astype(BF16)
        for lo in range(0, DFF, cw):
            hi = lo + cw
            df = _dot_nt(dhb, w_ref[lo:hi, :])
            gate = gu_ref[:, lo:hi].astype(F32)
            up = gu_ref[:, DFF + lo:DFF + hi].astype(F32)
            sg = _sigmoid(gate)
            silu = gate * sg
            du_ref[:, lo:hi] = (df * (up * (sg + silu * (1.0 - sg)))).astype(BF16)
            du_ref[:, DFF + lo:DFF + hi] = (df * silu).astype(BF16)

    return pl.pallas_call(
        body, name=name, grid=(T // TM,),
        in_specs=[_row(TM, D), _weight((DFF, D)), _row(TM, 2 * DFF)] + [ANY] * len(after),
        out_specs=_row(TM, 2 * DFF), out_shape=jax.ShapeDtypeStruct((T, 2 * DFF), BF16),
        compiler_params=_cp("parallel"),
    )(dh, w, gu, *after)


def mmT_rmsbwd(du, w, h, g, gl, dh_in, name):
    split = du.ndim == 3
    T = du.shape[-2]
    N = du.shape[-1] * (2 if split else 1)
    slots = w.shape[0]
    ns = N // slots

    def piece(du_ref, s):
        if split:
            per = slots // 2
            return du_ref[s // per, :, (s % per) * ns:(s % per + 1) * ns]
        return du_ref[:, s * ns:(s + 1) * ns]

    def body(du_ref, w_ref, h_ref, g_ref, di_ref, dh_ref, dg_ref):
        i = pl.program_id(0)

        @pl.when(i == 0)
        def _():
            dg_ref[...] = jnp.zeros_like(dg_ref)

        dxn = _dot_nt(piece(du_ref, 0), w_ref[0])
        for s in range(1, slots):
            dxn = dxn + _dot_nt(piece(du_ref, s), w_ref[s])
        x = h_ref[...]
        r = _rms(x)
        xh = x * r
        dg_ref[0:1, :] += jnp.sum(dxn * xh, axis=0, keepdims=True)
        dxh = dxn * g_ref[...]
        dh_ref[...] = di_ref[...] + r * (dxh - xh * jnp.mean(dxh * xh, axis=-1, keepdims=True))

    return pl.pallas_call(
        body, name=name, grid=(T // TM,),
        in_specs=[pl.BlockSpec((2, TM, N // 2), lambda i: (0, i, 0)) if split else _row(TM, N),
                  _weight((slots, D, ns)), _row(TM, D), _layer((1, D), gl), _row(TM, D)],
        out_specs=[_row(TM, D), _const((8, D))],
        out_shape=[jax.ShapeDtypeStruct((T, D), F32), jax.ShapeDtypeStruct((8, D), F32)],
        compiler_params=_cp("arbitrary"),
    )(du, w, h, g, dh_in)


def mmT(dh, w, name):
    T = dh.shape[0]
    N = w.shape[0]

    def body(dh_ref, w_ref, o_ref):
        o_ref[...] = _dot_nt(dh_ref[...].astype(BF16), w_ref[...]).astype(BF16)

    return pl.pallas_call(
        body, name=name, grid=(T // TM,),
        in_specs=[_row(TM, D), _weight((N, D))],
        out_specs=_row(TM, N), out_shape=jax.ShapeDtypeStruct((T, N), BF16),
        compiler_params=_cp("parallel"),
    )(dh, w)


def mmT_lnbwd(dh, w, y, sm, l, name):
    T = dh.shape[0]

    def body(dh_ref, w_ref, y_ref, sm_ref, dy_ref, st_ref):
        i = pl.program_id(0)

        @pl.when(i == 0)
        def _():
            st_ref[...] = jnp.zeros_like(st_ref)

        ds = _dot_nt(dh_ref[...].astype(BF16), w_ref[...])
        y = y_ref[...]
        mu = jnp.mean(y, axis=-1, keepdims=True)
        yc = y - mu
        rstd = lax.rsqrt(jnp.mean(yc * yc, axis=-1, keepdims=True) + EPS)
        xh = yc * rstd
        gam = sm_ref[32:33, :]
        z = xh * gam + sm_ref[33:34, :]
        sg = _sigmoid(z)
        dz = ds * sg * (1.0 + z * (1.0 - sg))
        st_ref[0:1, :] += jnp.sum(dz * xh, axis=0, keepdims=True)
        st_ref[1:2, :] += jnp.sum(dz, axis=0, keepdims=True)
        dxh = dz * gam
        dy = rstd * (dxh - jnp.mean(dxh, axis=-1, keepdims=True) - xh * jnp.mean(dxh * xh, axis=-1, keepdims=True))
        st_ref[2:3, :] += jnp.sum(dy, axis=0, keepdims=True)
        dy_ref[...] = dy

    return pl.pallas_call(
        body, name=name, grid=(T // TM,),
        in_specs=[_row(TM, D), _weight((D, D)), _row(TM, D), _layer((40, D), l)],
        out_specs=[_row(TM, D), _const((8, D))],
        out_shape=[jax.ShapeDtypeStruct((T, D), F32), jax.ShapeDtypeStruct((8, D), F32)],
        compiler_params=_cp("arbitrary"),
    )(dh, w, y, sm)


CH = 512


def dwconv_glu_bwd(dy, a, u, sm, smrev, l, name):
    T = dy.shape[0]
    nr, nc = T // TCV, D // CH
    nb = TCV // HALO
    last = T // HALO - 1

    def body(dy_ref, dyn_ref, a_ref, ap_ref, u1_ref, u2_ref, sm_ref, rev_ref, du_ref, dw_ref, shd, sha, da):
        i = pl.program_id(0)
        r = i % nr

        @pl.when(r == 0)
        def _():
            dw_ref[...] = jnp.zeros_like(dw_ref)

        shd[0, 0:TCV, :] = dy_ref[...]
        shd[0, TCV:TCV + HALO, :] = jnp.where(r < nr - 1, dyn_ref[...], 0.0)
        sha[0, 0:HALO, :] = jnp.where(r > 0, ap_ref[...], 0.0)
        sha[0, HALO:HALO + TCV, :] = a_ref[...]
        _make_shifts(shd)
        _make_shifts(sha)
        _conv_taps(shd, rev_ref, da, 0)
        for k in range(CONV_W):
            part = jnp.zeros((SUB, CH), F32)
            for r0 in range(0, TCV, SUB):
                part = part + dy_ref[r0:r0 + SUB, :] * _shifted(sha, HALO - (CONV_W - 1) + k + r0, SUB, slice(None))
            dw_ref[k:k + 1, :] += jnp.sum(part, axis=0, keepdims=True)
        dav = da[...]
        u1 = u1_ref[...].astype(F32)
        sg = _sigmoid(u2_ref[...].astype(F32))
        du_ref[0] = (dav * sg).astype(BF16)
        du_ref[1] = (dav * u1 * sg * (1.0 - sg)).astype(BF16)

    tile = lambda i: (i % nr, i // nr)
    in_specs = [pl.BlockSpec((TCV, CH), tile),
                pl.BlockSpec((HALO, CH), lambda i: (jnp.minimum((i % nr + 1) * nb, last), i // nr)),
                pl.BlockSpec((TCV, CH), tile),
                pl.BlockSpec((HALO, CH), lambda i: (jnp.maximum((i % nr) * nb - 1, 0), i // nr)),
                pl.BlockSpec((TCV, CH), tile), pl.BlockSpec((TCV, CH), lambda i: (i % nr, nc + i // nr)),
                pl.BlockSpec((None, 40, CH), lambda i: (l, 0, i // nr)),
                pl.BlockSpec((None, 40, CH), lambda i: (l, 0, i // nr))]
    return pl.pallas_call(
        body, name=name, grid=(nr * nc,), in_specs=in_specs,
        out_specs=[pl.BlockSpec((2, TCV, CH), lambda i: (0, i % nr, i // nr)),
                   pl.BlockSpec((32, CH), lambda i: (0, i // nr))],
        out_shape=[jax.ShapeDtypeStruct((2, T, D), BF16), jax.ShapeDtypeStruct((32, D), F32)],
        scratch_shapes=[pltpu.VMEM((SUB, TCV + HALO, CH), F32), pltpu.VMEM((SUB, TCV + HALO, CH), F32),
                        pltpu.VMEM((TCV, CH), F32)],
        compiler_params=_cp("arbitrary"),
    )(dy, dy, a, a, u, u, sm, smrev)


def _rows_tile(R):
    for t in (512, 256, 128, 64, 32, 16, 8):
        if R % t == 0:
            return t
    return R


def add8(own, others, name):
    R, C = own.shape
    tr = _rows_tile(R)

    def body(o_ref, x_ref, out_ref):
        acc = o_ref[...].astype(F32)
        for k in range(7):
            acc = acc + x_ref[k].astype(F32)
        out_ref[...] = acc

    return pl.pallas_call(
        body, name=name, grid=(R // tr,),
        in_specs=[_row(tr, C), pl.BlockSpec((7, tr, C), lambda i: (0, i, 0))], out_specs=_row(tr, C),
        out_shape=jax.ShapeDtypeStruct((R, C), F32), compiler_params=_cp("parallel"),
    )(own, others)


def adamw(w, g, m, v, name):
    R, C = w.shape
    tr = _rows_tile(R)

    def body(w_ref, g_ref, m_ref, v_ref, d_ref, nm_ref, nv_ref):
        gv = g_ref[...]
        nm = ADAM_B1 * m_ref[...] + (1.0 - ADAM_B1) * gv
        nv = ADAM_B2 * v_ref[...] + (1.0 - ADAM_B2) * (gv * gv)
        m_hat = nm / (1.0 - ADAM_B1 ** ADAM_STEP)
        v_hat = nv / (1.0 - ADAM_B2 ** ADAM_STEP)
        d_ref[...] = -ADAM_LR * (m_hat / (jnp.sqrt(v_hat) + ADAM_EPS) + ADAM_WD * w_ref[...])
        nm_ref[...] = nm
        nv_ref[...] = nv

    sd = jax.ShapeDtypeStruct((R, C), F32)
    return pl.pallas_call(
        body, name=name, grid=(R // tr,),
        in_specs=[_row(tr, C)] * 4, out_specs=[_row(tr, C)] * 3, out_shape=[sd, sd, sd],
        compiler_params=_cp("parallel"),
    )(w, g, m, v)


ANY = pl.BlockSpec(memory_space=pl.ANY)
HBM = pl.BlockSpec(memory_space=pltpu.HBM)
SEM = pl.BlockSpec(memory_space=pltpu.SEMAPHORE)
EFFECT = pltpu.SideEffectType.DATAFLOW_SIDE_EFFECTING


def _place():
    x, y, c = lax.axis_index("x"), lax.axis_index("y"), lax.axis_index("c")
    chips = [(1 - x, y), (x, 1 - y), (1 - x, 1 - y)]
    return x, y, c, chips


def _copy(src, dst, send, recv, k, to):
    return pltpu.make_async_remote_copy(src_ref=src, dst_ref=dst, send_sem=send.at[k], recv_sem=recv.at[k],
                                        device_id=to, device_id_type=MESH)


def xchg_start(name, bufs, plan, n, after=()):
    nb = len(bufs)

    na = len(after)

    def body(*refs):
        send, recv, token = refs[nb + na], refs[nb + na + 1], refs[-1]
        for k, (src, dst, to) in enumerate(plan(refs[:nb])):
            _copy(src, dst, send, recv, k, to).start()
        token[...] = jnp.zeros_like(token)

    outs = pl.pallas_call(
        body, name=name,
        out_shape=(pltpu.SemaphoreType.DMA((n,)), pltpu.SemaphoreType.DMA((n,)),
                   *[pltpu.HBM(b.shape, b.dtype) for b in bufs], jax.ShapeDtypeStruct((8, 128), F32)),
        in_specs=[HBM] * nb + [ANY] * na,
        out_specs=(SEM, SEM, *[HBM] * nb, pl.BlockSpec(memory_space=pltpu.VMEM)),
        input_output_aliases={i: 2 + i for i in range(nb)},
        compiler_params=pltpu.CompilerParams(has_side_effects=EFFECT),
    )(*[pltpu.with_memory_space_constraint(b, pltpu.HBM) for b in bufs], *after)
    return dict(name=name, send=outs[0], recv=outs[1], bufs=list(outs[2:2 + nb]), plan=plan), outs[-1]


def xchg_wait(flight, after):
    bufs, plan = flight["bufs"], flight["plan"]
    nb = len(bufs)

    def body(*refs):
        send, recv = refs[nb], refs[nb + 1]
        for k, (src, dst, to) in enumerate(plan(refs[:nb])):
            cp = _copy(src, dst, send, recv, k, to)
            cp.wait_send()
            cp.wait_recv()

    outs = pl.pallas_call(
        body, name=flight["name"] + "_wait",
        out_shape=tuple(pltpu.HBM(b.shape, b.dtype) for b in bufs),
        in_specs=[HBM] * nb + [SEM, SEM] + [ANY] * len(after),
        out_specs=tuple([HBM] * nb), input_output_aliases={i: i for i in range(nb)},
        compiler_params=pltpu.CompilerParams(has_side_effects=EFFECT),
    )(*bufs, flight["send"], flight["recv"], *after)
    return list(outs)


def _flip(k, x, y, c):
    return ((1 - x) if k & 4 else x, (1 - y) if k & 2 else y, (1 - c) if k & 1 else c)


class WeightGather:
    def __init__(self, shards, groups):
        me = 2 * lax.axis_index("x") + lax.axis_index("y")
        self.names = dict(groups)
        self.ici, self.d2d = {}, {}
        self.token = None
        for gname, names in groups:
            nt = len(names)
            srcs = [shards[n] for n in names]
            lands = [lax.dynamic_update_slice(lax.empty((4,) + s.shape, s.dtype), s[None], (me, 0, 0, 0))
                     for s in srcs]

            def plan(refs, nt=nt):
                x, y, c, chips = _place()
                return [(refs[t].at[c], refs[nt + t].at[2 * x + y, c], (cx, cy, c))
                        for t in range(nt) for cx, cy in chips]

            self.ici[gname], self.token = xchg_start(f"ag_ici_{gname}", srcs + lands, plan, 3 * nt,
                                                     after=[] if self.token is None else [self.token])

    def forward(self, gname, after):
        nt = len(self.names[gname])
        lands = xchg_wait(self.ici.pop(gname), after)[nt:]

        def plan(refs):
            x, y, c, chips = _place()
            out = []
            for t in range(nt):
                for cx, cy in chips:
                    piece = refs[t].at[2 * cx + cy, c]
                    out.append((piece, piece, (x, y, 1 - c)))
            return out

        self.d2d[gname], token = xchg_start(f"ag_d2d_{gname}", lands, plan, 3 * nt)
        return token

    def get(self, gname, after):
        lands = xchg_wait(self.d2d.pop(gname), after)
        return dict(zip(self.names[gname], lands))


class GradReduce:
    def __init__(self, kinds):
        self.J = {k: lax.empty((L, 2, a2, b), F32) for k, (L, a2, b) in kinds.items()}
        self.x, self.j = {}, {}

    @staticmethod
    def _where(name):
        kind, _, l = name.partition("_")
        return kind, int(l or 0)

    def send(self, gname, grads, after=()):
        names = list(grads)
        nt = len(names)
        gs = [grads[n] for n in names]
        xs = [lax.empty((7,) + g.shape[2:], g.dtype) for g in gs]

        def plan(refs):
            x, y, c, _ = _place()
            out = []
            for t in range(nt):
                for k in range(1, 8):
                    px, py, pc = _flip(k, x, y, c)
                    out.append((refs[t].at[2 * px + py, pc], refs[nt + t].at[k - 1], (px, py, pc)))
            return out

        flight, token = xchg_start(f"rs_x_{gname}", gs + xs, plan, 7 * nt, after=after)
        self.x[gname] = (names, flight)
        return token

    def reduce(self, gname, after):
        names, flight = self.x.pop(gname)
        nt = len(names)
        bufs = xchg_wait(flight, after)
        me, c = 2 * lax.axis_index("x") + lax.axis_index("y"), lax.axis_index("c")
        hs = []
        for t, n in enumerate(names):
            g = bufs[t]
            own = lax.dynamic_slice(g, (me, c, 0, 0), (1, 1) + g.shape[2:])[0, 0]
            hs.append(add8(own, bufs[nt + t], f"rs_add_{n}"))
        where = [self._where(n) for n in names]

        def plan(refs):
            x, y, c, _ = _place()
            return [(refs[t], refs[nt + t].at[where[t][1], c], (x, y, 1 - c)) for t in range(nt)]

        flight, token = xchg_start(f"rs_join_{gname}", hs + [self.J[k] for k, _ in where], plan, nt)
        self.j[gname] = (where, flight)
        return token

    def finish(self, gname, after):
        where, flight = self.j.pop(gname)
        nt = len(where)
        bufs = xchg_wait(flight, after)
        c = lax.axis_index("c")
        for t, (kind, l) in enumerate(where):
            self.J[kind] = lax.dynamic_update_slice(bufs[nt + t], bufs[t][None, None], (l, c, 0, 0))


def allreduce_small(v):
    R = v.shape[0]

    def body(v_ref, o_ref, all_ref, send, recv):
        x, y, c, _ = _place()
        me = 4 * x + 2 * y + c
        all_ref[me] = v_ref[...]
        cps = []
        for k in range(1, 8):
            cp = _copy(v_ref, all_ref.at[me], send, recv, k - 1, _flip(k, x, y, c))
            cp.start()
            cps.append(cp)
        for k in range(1, 8):
            px, py, pc = _flip(k, x, y, c)
            _copy(v_ref, all_ref.at[4 * px + 2 * py + pc], send, recv, k - 1, (px, py, pc)).wait_recv()
        for cp in cps:
            cp.wait_send()
        acc = all_ref[0]
        for d in range(1, 8):
            acc = acc + all_ref[d]
        o_ref[...] = acc

    return pl.pallas_call(
        body, name="allreduce_small",
        in_specs=[pl.BlockSpec(memory_space=pltpu.VMEM)], out_specs=pl.BlockSpec(memory_space=pltpu.VMEM),
        out_shape=jax.ShapeDtypeStruct((R, D), F32),
        scratch_shapes=[pltpu.VMEM((8, R, D), F32), pltpu.SemaphoreType.DMA((7,)), pltpu.SemaphoreType.DMA((7,))],
        compiler_params=pltpu.CompilerParams(has_side_effects=True, vmem_limit_bytes=VMEM_LIMIT),
    )(v)


AG_GROUPS = (("a0", ("pw1_0", "pw2_0", "small")), ("f0", ("up_0", "down_0")),
             ("l1", ("pw1_1", "pw2_1", "up_1", "down_1")), ("l2", ("kv", "wq_0", "wo_0", "up_2", "down_2")),
             ("l3", ("wq_1", "wo_1", "up_3", "down_3")))


def _bucket_table():
    qi = np.arange(BLK)[:, None]
    kj = np.arange(2 * BLK)[None, :]
    d = np.maximum(qi + BLK - kj, 0)
    max_exact = N_BUCKETS // 2
    log_ratio = (np.log(np.maximum(d, 1).astype(np.float32) / np.float32(max_exact))
                 / np.float32(math.log(MAX_DISTANCE / max_exact))).astype(np.float32)
    large = max_exact + (log_ratio * np.float32(N_BUCKETS - max_exact)).astype(np.int32)
    large = np.minimum(large, N_BUCKETS - 1)
    return np.where(d < max_exact, d, large).astype(np.int32)


def _heads_major(a, nh):
    T = a.shape[0]
    return a.reshape(T, nh, HD).transpose(1, 0, 2)


def _heads_minor(a):
    nh, T, _ = a.shape
    return a.transpose(1, 0, 2).reshape(T, nh * HD)


def _slots(land):
    return land.reshape(4, 2 * land.shape[2], land.shape[3])


def _rows(land):
    return land.reshape(8 * land.shape[2], land.shape[3])


def _gview(g):
    s, K, n = g.shape
    return g.reshape(4, 2, K // 2, n) if s == 4 else g.reshape(4, 2, K // 8, n)


def _gate(a, token):
    return a + token[0, 0]


def _conv_small(f_small):
    fs = f_small.transpose(1, 2, 0, 3).reshape(2, 40, D)
    b_pw1 = f_small[:, :, 35:37, :].transpose(1, 0, 2, 3).reshape(2, 1, 2 * D)
    rev = jnp.concatenate([fs[:, CONV_W - 1::-1], jnp.zeros((2, 40 - CONV_W, D), F32)], axis=1)
    return dict(conv=fs, conv_rev=rev, b_pw1=b_pw1, b_pw2=fs[:, 34:35])


def run_step(x, target, P, ag, rs):
    T = x.shape[0]
    zero = jnp.zeros((1, 1, D), F32)
    nm, nf = P["norm_mix"], P["norm_ffn"]
    ag.forward("a0", [ag.token])
    W = ag.get("a0", [])
    sm = _conv_small(W["small"])
    h = x
    saved = []
    for l in range(2):
        xn, u, a = norm_mm_glu(h, nm, l, _slots(W[f"pw1_{l}"]), sm["b_pw1"], f"f_pw1_{l}")
        y, s = dwconv_ln_silu(a, sm["conv"], l, f"f_conv_{l}")
        b2 = sm["b_pw2"]
        if l == 0:
            b2 = _gate(b2, ag.forward("f0", [s]))
        h1 = mm_bias_res(s, _rows(W[f"pw2_{l}"]), b2, l, h, f"f_pw2_{l}")
        if l == 0:
            W.update(ag.get("f0", [h1]))
        xn2, gu, f = norm_mm_swiglu(h1, nf, l, _slots(W[f"up_{l}"]), f"f_up_{l}")
        nxt = "l1" if l == 0 else "l2"
        h2 = mm_bias_res(f, _rows(W[f"down_{l}"]), _gate(zero, ag.forward(nxt, [f])), 0, h1, f"f_down_{l}")
        W.update(ag.get(nxt, [h2]))
        saved.append(dict(h=h, xn=xn, u=u, a=a, y=y, s=s, h1=h1, xn2=xn2, gu=gu, f=f))
        h = h2
    h_kv = h
    kvn, kv = norm_mm(h, P["norm_kv"], 0, _rows(W["kv"]), "f_kv")
    kp = jnp.pad(_heads_major(kv[:, :N_KV * HD], N_KV), ((0, 0), (BLK, 0), (0, 0)))
    vp = jnp.pad(_heads_major(kv[:, N_KV * HD:], N_KV), ((0, 0), (BLK, 0), (0, 0)))
    kvt = jnp.pad(kv.T.reshape(2, N_KV, HD, T), ((0, 0), (0, 0), (0, 0), (BLK, 0)))
    kt, vt = kvt[0], kvt[1]
    bucket = _bucket_table()
    onehot = jnp.asarray(np.eye(N_BUCKETS, dtype=np.float32)[bucket])
    bias = jnp.einsum("qkb,bh->hkq", onehot, P["rel_bias"], precision=lax.Precision.HIGHEST)
    bias = bias.reshape(N_KV, GROUP, 2 * BLK, BLK).transpose(0, 2, 1, 3).reshape(1, N_KV, 2 * BLK, QW)
    bias = bias + jnp.asarray(band_mask())[:, None]
    for j in range(2):
        l = 2 + j
        xn, q = norm_mm(h, nm, l, _rows(W[f"wq_{j}"]), f"f_q_{j}", scale=HD ** -0.5)
        qh = q.T.reshape(N_KV, GROUP, HD, T)
        sink = jnp.broadcast_to(P["sinks"][j].reshape(N_KV, GROUP, 1), (N_KV, GROUP, BLK)).reshape(N_KV, 1, QW)
        oh = attn_fwd(qh, kp, vt, bias, sink, f"f_attn_{j}")
        attn = oh.reshape(N_HEADS * HD, T).T
        h1 = mm_bias_res(attn, _rows(W[f"wo_{j}"]), zero, 0, h, f"f_wo_{j}")
        xn2, gu, f = norm_mm_swiglu(h1, nf, l, _slots(W[f"up_{l}"]), f"f_up_{l}")
        zg = _gate(zero, ag.forward("l3", [f])) if j == 0 else zero
        h2 = mm_bias_res(f, _rows(W[f"down_{l}"]), zg, 0, h1, f"f_down_{l}")
        if j == 0:
            W.update(ag.get("l3", [h2]))
        saved.append(dict(h=h, xn=xn, qh=qh, oh=oh, sink=sink, attn=attn, h1=h1, xn2=xn2, gu=gu, f=f))
        h = h2

    dh, st_final = final_loss(h, P["norm_final"], target, "loss_head")

    S = dict(norm_ffn=[None] * 4, norm_mix=[None] * 4, conv=[None] * 2, taps=[None] * 2, b_pw1=[None] * 2,
             b_pw2=[None] * 2, sinks=[None] * 2)

    def ffn_bwd(dh, sv, l, nf, after=()):
        du = mmT_swiglu_bwd(dh, _rows(W[f"down_{l}"]), sv["gu"], f"b_down_{l}", after)
        gd = mm_dw(sv["f"], dh, f"w_down_{l}", 512, 1)
        gu = mm_dw(sv["xn2"], du, f"w_up_{l}", DFF // 2, 4)
        dh, dg = mmT_rmsbwd(du, _slots(W[f"up_{l}"]), sv["h1"], nf, l, dh, f"b_up_{l}")
        S["norm_ffn"][l] = dg[0]
        return dh, {f"down_{l}": _gview(gd), f"up_{l}": _gview(gu)}

    dk = dv = dbias = None
    sent = []
    for j in (1, 0):
        l = 2 + j
        sv = saved[l]
        dh, grads = ffn_bwd(dh, sv, l, nf, sent)
        dattn = mmT(dh, _rows(W[f"wo_{j}"]), f"b_wo_{j}")
        grads[f"wo_{j}"] = _gview(mm_dw(sv["attn"], dh, f"w_wo_{j}", 512, 1))
        doh = dattn.T.reshape(N_KV, GROUP, HD, T)
        dqh, dkj, dvj, dbj, dsj = attn_bwd(sv["qh"], kp, kt, vp, bias, sv["sink"], sv["oh"], doh, f"b_attn_{j}")
        dq = dqh.reshape(N_HEADS * HD, T).T
        grads[f"wq_{j}"] = _gview(mm_dw(sv["xn"], dq, f"w_q_{j}", 512, 1))
        dh, dg = mmT_rmsbwd(dq, _rows(W[f"wq_{j}"])[None], sv["h"], nm, l, dh, f"b_q_{j}")
        S["norm_mix"][l] = dg[0]
        S["sinks"][j] = jnp.sum(dsj.reshape(N_HEADS, BLK), axis=1)
        dk = dkj if dk is None else dk + dkj
        dv = dvj if dv is None else dv + dvj
        dbias = dbj if dbias is None else dbias + dbj
        if j == 1:
            sent = [rs.send("l3", grads)]

    dkv = jnp.concatenate([_heads_minor(dk[:, BLK:]), _heads_minor(dv[:, BLK:])], axis=1).astype(BF16)
    grads["kv"] = _gview(mm_dw(kvn, dkv, "w_kv", 512, 1))
    dh, dg = mmT_rmsbwd(dkv, _rows(W["kv"])[None], h_kv, P["norm_kv"], 0, dh, "b_kv")
    S["norm_kv"] = dg[0]
    dbh = dbias.reshape(N_KV, 2 * BLK, GROUP, BLK)
    S["rel_bias"] = jnp.einsum("vkgq,qkb->bvg", dbh, onehot, precision=lax.Precision.HIGHEST).reshape(N_BUCKETS, N_HEADS)
    sent = [rs.send("l2", grads)]
    nf = _gate(nf, rs.reduce("l3", [dh]))

    for l in (1, 0):
        sv = saved[l]
        dh, grads = ffn_bwd(dh, sv, l, nf, sent)
        conv = sm["conv"]
        if l == 0:
            conv = _gate(conv, rs.send("f0", grads))
            grads = {}
        dy, st = mmT_lnbwd(dh, _rows(W[f"pw2_{l}"]), sv["y"], conv, l, f"b_pw2_{l}")
        g2, S["b_pw2"][l] = mm_dw(sv["s"], dh, f"w_pw2_{l}", 512, 1, colsum=True)
        du, dtaps = dwconv_glu_bwd(dy, sv["a"], sv["u"], sm["conv"], sm["conv_rev"], l, f"b_conv_{l}")
        S["conv"][l] = st[0:3]
        S["taps"][l] = dtaps[0:CONV_W]
        if l == 0:
            rs.finish("l2", [du])
            nm = _gate(nm, rs.reduce("l1", [du]))
        g1, S["b_pw1"][l] = mm_dw(sv["xn"], du, f"w_pw1_{l}", 512, 4, colsum=True)
        grads[f"pw2_{l}"], grads[f"pw1_{l}"] = _gview(g2), _gview(g1)
        dh, dg = mmT_rmsbwd(du, _slots(W[f"pw1_{l}"]), sv["h"], nm, l, dh, f"b_pw1_{l}")
        S["norm_mix"][l] = dg[0]
        if l == 1:
            sent = [rs.send("l1", grads)]
            rs.finish("l3", [dh])
            nf = _gate(nf, rs.reduce("l2", [dh]))
    S["norm_final"] = st_final[0]
    S["loss"] = st_final[1]
    return grads, dh, S


R_CONV = 37
R_SMALL = 88


def _pack_small(S):
    rows = []
    for l in range(2):
        rows += [S["taps"][l], S["conv"][l][2:3], S["conv"][l][0:2], S["b_pw2"][l], S["b_pw1"][l].reshape(2, D)]
    rows += [jnp.stack(S["norm_mix"]), jnp.stack(S["norm_ffn"]), S["norm_kv"][None], S["norm_final"][None]]
    tail = jnp.concatenate([jnp.stack(S["sinks"]).reshape(-1), S["rel_bias"].reshape(-1)])
    rows += [jnp.pad(tail, (0, D - tail.shape[0]))[None], S["loss"][None]]
    v = jnp.concatenate(rows, axis=0)
    return jnp.pad(v, ((0, R_SMALL - v.shape[0]), (0, 0)))


def kernel(x, norm_mix, norm_ffn, conv_w_pw1, conv_b_pw1, conv_w_dw, conv_b_dw, conv_ln_g, conv_ln_b, conv_w_pw2, conv_b_pw2, norm_kv, w_kv, w_q, w_o, sinks, rel_bias, ffn_w_up, ffn_w_down, norm_final, loss_target, m_norm_mix, m_norm_ffn, m_conv_w_pw1, m_conv_b_pw1, m_conv_w_dw, m_conv_b_dw, m_conv_ln_g, m_conv_ln_b, m_conv_w_pw2, m_conv_b_pw2, m_norm_kv, m_w_kv, m_w_q, m_w_o, m_sinks, m_rel_bias, m_ffn_w_up, m_ffn_w_down, m_norm_final, v_norm_mix, v_norm_ffn, v_conv_w_pw1, v_conv_b_pw1, v_conv_w_dw, v_conv_b_dw, v_conv_ln_g, v_conv_ln_b, v_conv_w_pw2, v_conv_b_pw2, v_norm_kv, v_w_kv, v_w_q, v_w_o, v_sinks, v_rel_bias, v_ffn_w_up, v_ffn_w_down, v_norm_final):
    me = 2 * lax.axis_index("x") + lax.axis_index("y")
    weights = dict(norm_mix=norm_mix, norm_ffn=norm_ffn, conv_w_pw1=conv_w_pw1, conv_b_pw1=conv_b_pw1,
                   conv_w_dw=conv_w_dw, conv_b_dw=conv_b_dw, conv_ln_g=conv_ln_g, conv_ln_b=conv_ln_b,
                   conv_w_pw2=conv_w_pw2, conv_b_pw2=conv_b_pw2, norm_kv=norm_kv, w_kv=w_kv, w_q=w_q, w_o=w_o,
                   sinks=sinks, rel_bias=rel_bias, ffn_w_up=ffn_w_up, ffn_w_down=ffn_w_down, norm_final=norm_final)
    mom_m = dict(norm_mix=m_norm_mix, norm_ffn=m_norm_ffn, conv_w_pw1=m_conv_w_pw1, conv_b_pw1=m_conv_b_pw1,
                 conv_w_dw=m_conv_w_dw, conv_b_dw=m_conv_b_dw, conv_ln_g=m_conv_ln_g, conv_ln_b=m_conv_ln_b,
                 conv_w_pw2=m_conv_w_pw2, conv_b_pw2=m_conv_b_pw2, norm_kv=m_norm_kv, w_kv=m_w_kv, w_q=m_w_q,
                 w_o=m_w_o, sinks=m_sinks, rel_bias=m_rel_bias, ffn_w_up=m_ffn_w_up, ffn_w_down=m_ffn_w_down,
                 norm_final=m_norm_final)
    mom_v = dict(norm_mix=v_norm_mix, norm_ffn=v_norm_ffn, conv_w_pw1=v_conv_w_pw1, conv_b_pw1=v_conv_b_pw1,
                 conv_w_dw=v_conv_w_dw, conv_b_dw=v_conv_b_dw, conv_ln_g=v_conv_ln_g, conv_ln_b=v_conv_ln_b,
                 conv_w_pw2=v_conv_w_pw2, conv_b_pw2=v_conv_b_pw2, norm_kv=v_norm_kv, w_kv=v_w_kv, w_q=v_w_q,
                 w_o=v_w_o, sinks=v_sinks, rel_bias=v_rel_bias, ffn_w_up=v_ffn_w_up, ffn_w_down=v_ffn_w_down,
                 norm_final=v_norm_final)

    def halves(a):
        return a.astype(BF16).reshape(2, a.shape[0] // 2, a.shape[1])

    shards = {"kv": halves(w_kv)}
    for l in range(2):
        shards[f"pw1_{l}"], shards[f"pw2_{l}"] = halves(conv_w_pw1[l]), halves(conv_w_pw2[l])
        shards[f"wq_{l}"], shards[f"wo_{l}"] = halves(w_q[l]), halves(w_o[l])
    for l in range(4):
        shards[f"up_{l}"], shards[f"down_{l}"] = halves(ffn_w_up[l]), halves(ffn_w_down[l])
    shards["small"] = jnp.concatenate(
        [conv_w_dw, conv_b_dw[:, None], conv_ln_g[:, None], conv_ln_b[:, None], conv_b_pw2[:, None],
         conv_b_pw1.reshape(2, 2, 256), jnp.zeros((2, 3, 256), F32)], axis=1)
    ag = WeightGather(shards, AG_GROUPS)
    big = {"conv_w_pw1": "pw1", "conv_w_pw2": "pw2", "w_q": "wq", "w_o": "wo", "ffn_w_up": "up",
           "ffn_w_down": "down", "w_kv": "kv"}
    rs = GradReduce({"pw1": (2, 512, 512), "pw2": (2, 128, D), "wq": (2, 128, D), "wo": (2, 128, D),
                     "up": (4, 512, DFF // 2), "down": (4, DFF // 8, D), "kv": (1, 128, 512)})

    P = dict(norm_mix=norm_mix[:, None], norm_ffn=norm_ffn[:, None], norm_kv=norm_kv[None, None],
             norm_final=norm_final[None], sinks=sinks, rel_bias=rel_bias)
    last, grad_x, S = run_step(x[0], loss_target[0], P, ag, rs)

    rs.finish("l1", [grad_x])
    vsum = allreduce_small(_gate(_pack_small(S), rs.reduce("f0", [grad_x])))
    token = rs.send("c0", last, after=[vsum])
    col = lambda a: lax.dynamic_slice_in_dim(a, me * 256, 256, axis=-1)
    grads = {}
    for l in range(2):
        base = l * R_CONV
        grads.setdefault("conv_w_dw", []).append(col(vsum[base:base + 31]))
        grads.setdefault("conv_b_dw", []).append(col(vsum[base + 31]))
        grads.setdefault("conv_ln_g", []).append(col(vsum[base + 32]))
        grads.setdefault("conv_ln_b", []).append(col(vsum[base + 33]))
        grads.setdefault("conv_b_pw2", []).append(col(vsum[base + 34]))
        grads.setdefault("conv_b_pw1", []).append(
            lax.dynamic_slice_in_dim(vsum[base + 35:base + 37].reshape(2 * D), me * 512, 512, axis=0))
    grads = {k: jnp.stack(v) for k, v in grads.items()}
    base = 2 * R_CONV
    grads["norm_mix"] = vsum[base:base + 4]
    grads["norm_ffn"] = vsum[base + 4:base + 8]
    grads["norm_kv"] = vsum[base + 8]
    grads["norm_final"] = vsum[base + 9]
    grads["sinks"] = vsum[base + 10, 0:32].reshape(2, 16)
    grads["rel_bias"] = vsum[base + 10, 32:32 + 512].reshape(32, 16)
    loss = vsum[base + 11, 0]

    delta, new_m, new_v = {}, {}, {}
    rest = [n for n in weights if n not in big]

    def pack(dct):
        flat = jnp.concatenate([dct[n].reshape(-1) for n in rest])
        return jnp.pad(flat, (0, (-flat.shape[0]) % (8 * 128))).reshape(-1, 128)

    d, nm, nv = adamw(pack(weights), _gate(pack(grads), token), pack(mom_m), pack(mom_v), "adamw_small")
    off = 0
    for n in rest:
        shp = weights[n].shape
        sz = int(np.prod(shp))
        delta[n] = d.reshape(-1)[off:off + sz].reshape(shp)
        new_m[n] = nm.reshape(-1)[off:off + sz].reshape(shp)
        new_v[n] = nv.reshape(-1)[off:off + sz].reshape(shp)
        off += sz

    def update(n):
        shp = weights[n].shape
        r2 = (int(np.prod(shp[:-1])), shp[-1])
        grads[n] = rs.J[big[n]].reshape(shp)
        d, nm, nv = adamw(weights[n].reshape(r2), grads[n].reshape(r2), mom_m[n].reshape(r2), mom_v[n].reshape(r2),
                          f"adamw_{n}")
        delta[n], new_m[n], new_v[n] = d.reshape(shp), nm.reshape(shp), nv.reshape(shp)

    rs.finish("f0", [vsum])
    for n in ("ffn_w_up", "ffn_w_down"):
        update(n)
    rs.reduce("c0", [delta["ffn_w_up"], delta["ffn_w_down"]])
    for n in ("w_q", "w_o", "w_kv"):
        update(n)
    rs.finish("c0", [delta["w_kv"]])
    for n in ("conv_w_pw1", "conv_w_pw2"):
        update(n)

    order = list(weights)
    return (loss, grad_x[None], *[grads[n] for n in order], *[delta[n] for n in order],
            *[new_m[n] for n in order], *[new_v[n] for n in order])
```

```python
import functools
import math

import numpy as np
import jax
import jax.numpy as jnp
from jax import lax
from jax.experimental import pallas as pl
from jax.experimental.pallas import tpu as pltpu

F32 = jnp.float32
BF16 = jnp.bfloat16
MESH = pl.DeviceIdType.MESH

D = 1024
DFF = 2816
N_HEADS = 16
N_KV = 4
GROUP = 4
HD = 64
BLK = 128
CONV_W = 31
HALO = 32
N_BUCKETS = 32
MAX_DISTANCE = 128
EPS = 1e-6
NEG_INF = -1e30
TM = 512
TCV = 256
VMEM_LIMIT = 56 * 2 ** 20

ADAM_LR, ADAM_B1, ADAM_B2, ADAM_EPS, ADAM_WD, ADAM_STEP = 0.001, 0.9, 0.999, 1e-08, 0.01, 10


def _cp(*sem):
    return pltpu.CompilerParams(dimension_semantics=sem, vmem_limit_bytes=VMEM_LIMIT)


def _sigmoid(x):
    return 1.0 / (1.0 + jnp.exp(-x))


def _row(tm, n):
    return pl.BlockSpec((tm, n), lambda i: (i, 0))


def _const(shape):
    nd = len(shape)
    return pl.BlockSpec(shape, lambda i: (0,) * nd)


def _weight(shape):
    nd = len(shape)
    return pl.BlockSpec(shape, lambda i: (0,) * nd, pipeline_mode=pl.Buffered(1))


def _layer(shape, l):
    nd = len(shape)
    return pl.BlockSpec((None,) + tuple(shape), lambda i: (l,) + (0,) * nd)


def _dot(a, b):
    return jnp.dot(a, b, preferred_element_type=F32)


def _dot_nt(a, b):
    return lax.dot_general(a, b, (((1,), (1,)), ((), ())), preferred_element_type=F32)


def _dot_tn(a, b):
    return lax.dot_general(a, b, (((0,), (0,)), ((), ())), preferred_element_type=F32)


def _rms(x):
    return lax.rsqrt(jnp.mean(x * x, axis=-1, keepdims=True) + EPS)


def norm_mm_glu(h, g, l, w, b, name):
    T = h.shape[0]
    ns = w.shape[-1]

    def body(h_ref, g_ref, w_ref, b_ref, xn_ref, u_ref, a_ref):
        x = h_ref[...]
        xn = (x * _rms(x) * g_ref[...]).astype(BF16)
        xn_ref[...] = xn
        for s in range(2):
            lo, hi = s * ns, (s + 1) * ns
            u1 = _dot(xn, w_ref[s]) + b_ref[:, lo:hi]
            u2 = _dot(xn, w_ref[2 + s]) + b_ref[:, D + lo:D + hi]
            u_ref[:, lo:hi] = u1.astype(BF16)
            u_ref[:, D + lo:D + hi] = u2.astype(BF16)
            a_ref[:, lo:hi] = u1 * _sigmoid(u2)

    return pl.pallas_call(
        body, name=name, grid=(T // TM,),
        in_specs=[_row(TM, D), _layer((1, D), l), _weight((4, D, ns)), _layer((1, 2 * D), l)],
        out_specs=[_row(TM, D), _row(TM, 2 * D), _row(TM, D)],
        out_shape=[jax.ShapeDtypeStruct((T, D), BF16), jax.ShapeDtypeStruct((T, 2 * D), BF16),
                   jax.ShapeDtypeStruct((T, D), F32)],
        compiler_params=_cp("parallel"),
    )(h, g, w, b)


SUB = 8


def _make_shifts(sh):
    n = TCV + HALO - SUB
    for r in range(1, SUB):
        for r0 in range(0, n, 40):
            sh[r, r0:r0 + 40, :] = sh[0, pl.ds(r + r0, 40), :]


def _shifted(sh, off, rows, cols):
    return sh[off % SUB, pl.ds(off - off % SUB, rows), cols]


def _conv_taps(sh, w_ref, out_ref, first):
    RB, LB = 32, 512
    for r0 in range(0, TCV, RB):
        for c0 in range(0, out_ref.shape[1], LB):
            acc = jnp.zeros((RB, LB), F32)
            for k in range(CONV_W):
                acc = acc + w_ref[k:k + 1, c0:c0 + LB] * _shifted(sh, first + k + r0, RB, slice(c0, c0 + LB))
            out_ref[r0:r0 + RB, c0:c0 + LB] = acc


def dwconv_ln_silu(a, sm, l, name):
    T = a.shape[0]
    nb = TCV // HALO

    def body(cur_ref, prev_ref, sm_ref, y_ref, s_ref, sh):
        i = pl.program_id(0)
        sh[0, 0:HALO, :] = jnp.where(i > 0, prev_ref[...], 0.0)
        sh[0, HALO:HALO + TCV, :] = cur_ref[...]
        _make_shifts(sh)
        _conv_taps(sh, sm_ref, y_ref, HALO - (CONV_W - 1))
        y = y_ref[...] + sm_ref[31:32, :]
        y_ref[...] = y
        mu = jnp.mean(y, axis=-1, keepdims=True)
        yc = y - mu
        rstd = lax.rsqrt(jnp.mean(yc * yc, axis=-1, keepdims=True) + EPS)
        z = yc * rstd * sm_ref[32:33, :] + sm_ref[33:34, :]
        s_ref[...] = (z * _sigmoid(z)).astype(BF16)

    return pl.pallas_call(
        body, name=name, grid=(T // TCV,),
        in_specs=[_row(TCV, D), pl.BlockSpec((HALO, D), lambda i: (jnp.maximum(i * nb - 1, 0), 0)),
                  _layer((40, D), l)],
        out_specs=[_row(TCV, D), _row(TCV, D)],
        out_shape=[jax.ShapeDtypeStruct((T, D), F32), jax.ShapeDtypeStruct((T, D), BF16)],
        scratch_shapes=[pltpu.VMEM((SUB, TCV + HALO, D), F32)],
        compiler_params=_cp("parallel"),
    )(a, a, sm)


def mm_bias_res(xb, w, b, bl, res, name):
    T, K = xb.shape

    def body(x_ref, w_ref, b_ref, r_ref, o_ref):
        o_ref[...] = _dot(x_ref[...], w_ref[...]) + b_ref[...] + r_ref[...]

    return pl.pallas_call(
        body, name=name, grid=(T // TM,),
        in_specs=[_row(TM, K), _weight((K, D)), _layer((1, D), bl), _row(TM, D)],
        out_specs=_row(TM, D), out_shape=jax.ShapeDtypeStruct((T, D), F32),
        compiler_params=_cp("parallel"),
    )(xb, w, b, res)


def norm_mm_swiglu(h, g, l, w, name):
    T = h.shape[0]
    ns = w.shape[-1]

    def body(h_ref, g_ref, w_ref, xn_ref, gu_ref, f_ref):
        x = h_ref[...]
        xn = (x * _rms(x) * g_ref[...]).astype(BF16)
        xn_ref[...] = xn
        for s in range(2):
            lo, hi = s * ns, (s + 1) * ns
            gate = _dot(xn, w_ref[s])
            up = _dot(xn, w_ref[2 + s])
            gu_ref[:, lo:hi] = gate.astype(BF16)
            gu_ref[:, DFF + lo:DFF + hi] = up.astype(BF16)
            f_ref[:, lo:hi] = (gate * _sigmoid(gate) * up).astype(BF16)

    return pl.pallas_call(
        body, name=name, grid=(T // TM,),
        in_specs=[_row(TM, D), _layer((1, D), l), _weight((4, D, ns))],
        out_specs=[_row(TM, D), _row(TM, 2 * DFF), _row(TM, DFF)],
        out_shape=[jax.ShapeDtypeStruct((T, D), BF16), jax.ShapeDtypeStruct((T, 2 * DFF), BF16),
                   jax.ShapeDtypeStruct((T, DFF), BF16)],
        compiler_params=_cp("parallel"),
    )(h, g, w)


def norm_mm(h, g, gl, w, name, scale=1.0):
    T = h.shape[0]
    N = w.shape[-1]

    def body(h_ref, g_ref, w_ref, xn_ref, o_ref):
        x = h_ref[...]
        xn = (x * _rms(x) * g_ref[...]).astype(BF16)
        xn_ref[...] = xn
        o_ref[...] = (_dot(xn, w_ref[...]) * scale).astype(BF16)

    return pl.pallas_call(
        body, name=name, grid=(T // TM,),
        in_specs=[_row(TM, D), _layer((1, D), gl), _weight((D, N))],
        out_specs=[_row(TM, D), _row(TM, N)],
        out_shape=[jax.ShapeDtypeStruct((T, D), BF16), jax.ShapeDtypeStruct((T, N), BF16)],
        compiler_params=_cp("parallel"),
    )(h, g, w)


QB = 4
QW = GROUP * BLK


def band_mask():
    qi = np.arange(QW)[None, :] % BLK
    kj = np.arange(2 * BLK)[:, None]
    band = ((kj < BLK) & (kj > qi)) | ((kj >= BLK) & (kj - BLK <= qi))
    first = band & (kj >= BLK)
    return np.where(np.stack([first, band]), 0.0, NEG_INF).astype(np.float32)


def _softmax_cols(s, sink):
    m = jnp.maximum(jnp.max(s, axis=0, keepdims=True), sink)
    p = jnp.exp(s - m)
    es = jnp.exp(sink - m)
    inv = 1.0 / (jnp.sum(p, axis=0, keepdims=True) + es)
    return p, inv, es


def _attn_specs(T):
    W = QB * BLK
    qspec = pl.BlockSpec((None, GROUP, HD, W), lambda kv, n: (kv, 0, 0, n))
    kspec = pl.BlockSpec((None, T + BLK, HD), lambda kv, n: (kv, 0, 0))
    ktspec = [pl.BlockSpec((None, HD, W), lambda kv, n: (kv, 0, n)),
              pl.BlockSpec((None, HD, BLK), lambda kv, n: (kv, 0, (n + 1) * QB))]
    bspec = pl.BlockSpec((2, None, 2 * BLK, QW), lambda kv, n: (0, kv, 0, 0))
    sspec = pl.BlockSpec((None, 1, QW), lambda kv, n: (kv, 0, 0))
    return qspec, kspec, ktspec, bspec, sspec


def _attn_block(n, b):
    blk = n * QB + b
    rows = pl.ds(pl.multiple_of(blk * BLK, BLK), 2 * BLK)
    return rows, (jnp.minimum(blk, 1) if b == 0 else 1)


def _band_cols(main_ref, tail_ref, b):
    if b < QB - 1:
        return main_ref[:, b * BLK:(b + 2) * BLK]
    return jnp.concatenate([main_ref[:, b * BLK:], tail_ref[...]], axis=1)


def _heads_side_by_side(ref, qs):
    return jnp.concatenate([ref[g, :, qs] for g in range(GROUP)], axis=1)


def attn_fwd(q, kp, vt, bias, sink, name):
    T = q.shape[3]
    qspec, kspec, ktspec, bspec, sspec = _attn_specs(T)

    def body(q_ref, k_ref, vt_ref, vtt_ref, b_ref, s_ref, o_ref, pb):
        n = pl.program_id(1)
        for b in range(QB):
            rows, table = _attn_block(n, b)
            qs = slice(b * BLK, (b + 1) * BLK)
            st = _dot(k_ref[rows, :], _heads_side_by_side(q_ref, qs))
            for g in range(GROUP):
                hs = slice(g * BLK, (g + 1) * BLK)
                p, inv, _ = _softmax_cols(st[:, hs] + b_ref[table, :, hs], s_ref[:, hs])
                pb[:, hs] = (p * inv).astype(BF16)
            ot = _dot(_band_cols(vt_ref, vtt_ref, b), pb[...])
            for g in range(GROUP):
                o_ref[g, :, qs] = ot[:, g * BLK:(g + 1) * BLK].astype(BF16)

    return pl.pallas_call(
        body, name=name, grid=(N_KV, T // (QB * BLK)),
        in_specs=[qspec, kspec, *ktspec, bspec, sspec], out_specs=qspec,
        out_shape=jax.ShapeDtypeStruct((N_KV, GROUP, HD, T), BF16),
        scratch_shapes=[pltpu.VMEM((2 * BLK, QW), BF16)],
        compiler_params=_cp("parallel", "parallel"),
    )(q, kp, vt, vt, bias, sink)


def attn_bwd(q, kp, kt, vp, bias, sink, o, do, name):
    T = q.shape[3]
    qspec, kspec, ktspec, bspec, sspec = _attn_specs(T)

    def body(q_ref, k_ref, kt_ref, ktt_ref, v_ref, b_ref, s_ref, o_ref, do_ref,
             dq_ref, dk_ref, dv_ref, db_ref, ds_ref, pb, dsb):
        n = pl.program_id(1)

        @pl.when(n == 0)
        def _():
            dk_ref[...] = jnp.zeros_like(dk_ref)
            dv_ref[...] = jnp.zeros_like(dv_ref)
            db_ref[...] = jnp.zeros_like(db_ref)
            ds_ref[...] = jnp.zeros_like(ds_ref)

        for b in range(QB):
            rows, table = _attn_block(n, b)
            qs = slice(b * BLK, (b + 1) * BLK)
            q4 = _heads_side_by_side(q_ref, qs)
            do4 = _heads_side_by_side(do_ref, qs)
            st = _dot(k_ref[rows, :], q4)
            dpt = _dot(v_ref[rows, :], do4)
            for g in range(GROUP):
                hs = slice(g * BLK, (g + 1) * BLK)
                p, inv, es = _softmax_cols(st[:, hs] + b_ref[table, :, hs], s_ref[:, hs])
                probs = p * inv
                delta = jnp.sum(do_ref[g, :, qs].astype(F32) * o_ref[g, :, qs].astype(F32), axis=0, keepdims=True)
                dS = probs * (dpt[:, hs] - delta)
                ds_ref[:, hs] += -(es * inv) * delta
                db_ref[:, hs] += dS
                pb[:, hs] = probs.astype(BF16)
                dsb[:, hs] = dS.astype(BF16)
            dqt = _dot(_band_cols(kt_ref, ktt_ref, b), dsb[...]) * (HD ** -0.5)
            for g in range(GROUP):
                dq_ref[g, :, qs] = dqt[:, g * BLK:(g + 1) * BLK].astype(BF16)
            dk_ref[rows, :] += _dot_nt(dsb[...], q4)
            dv_ref[rows, :] += _dot_nt(pb[...], do4)

    kout = pl.BlockSpec((None, T + BLK, HD), lambda kv, n: (kv, 0, 0))
    dbspec = pl.BlockSpec((None, 2 * BLK, QW), lambda kv, n: (kv, 0, 0))
    return pl.pallas_call(
        body, name=name, grid=(N_KV, T // (QB * BLK)),
        in_specs=[qspec, kspec, *ktspec, kspec, bspec, sspec, qspec, qspec],
        out_specs=[qspec, kout, kout, dbspec, sspec],
        out_shape=[jax.ShapeDtypeStruct((N_KV, GROUP, HD, T), BF16),
                   jax.ShapeDtypeStruct((N_KV, T + BLK, HD), F32), jax.ShapeDtypeStruct((N_KV, T + BLK, HD), F32),
                   jax.ShapeDtypeStruct((N_KV, 2 * BLK, QW), F32), jax.ShapeDtypeStruct((N_KV, 1, QW), F32)],
        scratch_shapes=[pltpu.VMEM((2 * BLK, QW), BF16), pltpu.VMEM((2 * BLK, QW), BF16)],
        compiler_params=_cp("parallel", "arbitrary"),
    )(q, kp, kt, kt, vp, bias, sink, o, do)


def final_loss(h, g, target, name):
    T = h.shape[0]

    def body(h_ref, g_ref, t_ref, dh_ref, st_ref):
        i = pl.program_id(0)

        @pl.when(i == 0)
        def _():
            st_ref[...] = jnp.zeros_like(st_ref)

        x = h_ref[...]
        r = _rms(x)
        xh = x * r
        e = xh * g_ref[...] - t_ref[...]
        loss = 0.5 * jnp.sum(jnp.mean(e * e, axis=-1, keepdims=True))
        dy = e * (1.0 / D)
        st_ref[0:1, :] += jnp.sum(dy * xh, axis=0, keepdims=True)
        lane = lax.broadcasted_iota(jnp.int32, (1, D), 1)
        st_ref[1:2, :] += jnp.where(lane == 0, loss, 0.0)
        dxh = dy * g_ref[...]
        dh_ref[...] = r * (dxh - xh * jnp.mean(dxh * xh, axis=-1, keepdims=True))

    return pl.pallas_call(
        body, name=name, grid=(T // TM,),
        in_specs=[_row(TM, D), _const((1, D)), _row(TM, D)],
        out_specs=[_row(TM, D), _const((8, D))],
        out_shape=[jax.ShapeDtypeStruct((T, D), F32), jax.ShapeDtypeStruct((8, D), F32)],
        compiler_params=_cp("arbitrary"),
    )(h, g, target)


def mm_dw(x, dy, name, tn, slots, colsum=False):
    T, K = x.shape
    split = dy.ndim == 3
    N = dy.shape[-1] * (2 if split else 1)
    tt = min(T, 2048 if K <= 1024 else 1024)
    nt = T // tt
    ns = N // slots
    per = ns // tn

    def body(x_ref, dy_ref, *rest):
        if colsum:
            dw_ref, cs_ref, acc, cacc = rest
        else:
            dw_ref, acc = rest
        t = pl.program_id(1)

        @pl.when(t == 0)
        def _():
            acc[...] = jnp.zeros_like(acc)
            if colsum:
                cacc[...] = jnp.zeros_like(cacc)

        dyv = dy_ref[...]
        acc[...] += _dot_tn(x_ref[...].astype(BF16), dyv.astype(BF16))
        if colsum:
            cacc[...] += jnp.sum(dyv.astype(F32), axis=0, keepdims=True)

        @pl.when(t == nt - 1)
        def _():
            dw_ref[...] = acc[...].astype(BF16)
            if colsum:
                cs_ref[...] = cacc[...]

    if split:
        half = N // 2 // tn
        dy_spec = pl.BlockSpec((None, tt, tn), lambda j, t: (j // half, t, j % half))
    else:
        dy_spec = pl.BlockSpec((tt, tn), lambda j, t: (t, j))
    out_specs = [pl.BlockSpec((None, K, tn), lambda j, t: (j // per, 0, j % per))]
    out_shape = [jax.ShapeDtypeStruct((slots, K, ns), BF16)]
    scratch = [pltpu.VMEM((K, tn), F32)]
    if colsum:
        out_specs.append(pl.BlockSpec((1, tn), lambda j, t: (0, j)))
        out_shape.append(jax.ShapeDtypeStruct((1, N), F32))
        scratch.append(pltpu.VMEM((1, tn), F32))
    res = pl.pallas_call(
        body, name=name, grid=(N // tn, nt),
        in_specs=[pl.BlockSpec((tt, K), lambda j, t: (t, 0)), dy_spec],
        out_specs=out_specs, out_shape=out_shape, scratch_shapes=scratch,
        compiler_params=_cp("parallel", "arbitrary"),
    )(x, dy)
    return tuple(res) if colsum else res[0]


def mmT_swiglu_bwd(dh, w, gu, name, after=()):
    T = dh.shape[0]
    cw = 256

    def body(dh_ref, w_ref, gu_ref, *rest):
        du_ref = rest[-1]
        dhb = dh_ref[...].astype(BF16)
        for lo in range(0, DFF, cw):
            hi = lo + cw
            df = _dot_nt(dhb, w_ref[lo:hi, :])
            gate = gu_ref[:, lo:hi].astype(F32)
            up = gu_ref[:, DFF + lo:DFF + hi].astype(F32)
            sg = _sigmoid(gate)
            silu = gate * sg
            du_ref[:, lo:hi] = (df * (up * (sg + silu * (1.0 - sg)))).astype(BF16)
            du_ref[:, DFF + lo:DFF + hi] = (df * silu).astype(BF16)

    return pl.pallas_call(
        body, name=name, grid=(T // TM,),
        in_specs=[_row(TM, D), _weight((DFF, D)), _row(TM, 2 * DFF)] + [ANY] * len(after),
        out_specs=_row(TM, 2 * DFF), out_shape=jax.ShapeDtypeStruct((T, 2 * DFF), BF16),
        compiler_params=_cp("parallel"),
    )(dh, w, gu, *after)


def mmT_rmsbwd(du, w, h, g, gl, dh_in, name):
    split = du.ndim == 3
    T = du.shape[-2]
    N = du.shape[-1] * (2 if split else 1)
    slots = w.shape[0]
    ns = N // slots

    def piece(du_ref, s):
        if split:
            per = slots // 2
            return du_ref[s // per, :, (s % per) * ns:(s % per + 1) * ns]
        return du_ref[:, s * ns:(s + 1) * ns]

    def body(du_ref, w_ref, h_ref, g_ref, di_ref, dh_ref, dg_ref):
        i = pl.program_id(0)

        @pl.when(i == 0)
        def _():
            dg_ref[...] = jnp.zeros_like(dg_ref)

        dxn = _dot_nt(piece(du_ref, 0), w_ref[0])
        for s in range(1, slots):
            dxn = dxn + _dot_nt(piece(du_ref, s), w_ref[s])
        x = h_ref[...]
        r = _rms(x)
        xh = x * r
        dg_ref[0:1, :] += jnp.sum(dxn * xh, axis=0, keepdims=True)
        dxh = dxn * g_ref[...]
        dh_ref[...] = di_ref[...] + r * (dxh - xh * jnp.mean(dxh * xh, axis=-1, keepdims=True))

    return pl.pallas_call(
        body, name=name, grid=(T // TM,),
        in_specs=[pl.BlockSpec((2, TM, N // 2), lambda i: (0, i, 0)) if split else _row(TM, N),
                  _weight((slots, D, ns)), _row(TM, D), _layer((1, D), gl), _row(TM, D)],
        out_specs=[_row(TM, D), _const((8, D))],
        out_shape=[jax.ShapeDtypeStruct((T, D), F32), jax.ShapeDtypeStruct((8, D), F32)],
        compiler_params=_cp("arbitrary"),
    )(du, w, h, g, dh_in)


def mmT(dh, w, name):
    T = dh.shape[0]
    N = w.shape[0]

    def body(dh_ref, w_ref, o_ref):
        o_ref[...] = _dot_nt(dh_ref[...].astype(BF16), w_ref[...]).astype(BF16)

    return pl.pallas_call(
        body, name=name, grid=(T // TM,),
        in_specs=[_row(TM, D), _weight((N, D))],
        out_specs=_row(TM, N), out_shape=jax.ShapeDtypeStruct((T, N), BF16),
        compiler_params=_cp("parallel"),
    )(dh, w)


def mmT_lnbwd(dh, w, y, sm, l, name):
    T = dh.shape[0]

    def body(dh_ref, w_ref, y_ref, sm_ref, dy_ref, st_ref):
        i = pl.program_id(0)

        @pl.when(i == 0)
        def _():
            st_ref[...] = jnp.zeros_like(st_ref)

        ds = _dot_nt(dh_ref[...].astype(BF16), w_ref[...])
        y = y_ref[...]
        mu = jnp.mean(y, axis=-1, keepdims=True)
        yc = y - mu
        rstd = lax.rsqrt(jnp.mean(yc * yc, axis=-1, keepdims=True) + EPS)
        xh = yc * rstd
        gam = sm_ref[32:33, :]
        z = xh * gam + sm_ref[33:34, :]
        sg = _sigmoid(z)
        dz = ds * sg * (1.0 + z * (1.0 - sg))
        st_ref[0:1, :] += jnp.sum(dz * xh, axis=0, keepdims=True)
        st_ref[1:2, :] += jnp.sum(dz, axis=0, keepdims=True)
        dxh = dz * gam
        dy = rstd * (dxh - jnp.mean(dxh, axis=-1, keepdims=True) - xh * jnp.mean(dxh * xh, axis=-1, keepdims=True))
        st_ref[2:3, :] += jnp.sum(dy, axis=0, keepdims=True)
        dy_ref[...] = dy

    return pl.pallas_call(
        body, name=name, grid=(T // TM,),
        in_specs=[_row(TM, D), _weight((D, D)), _row(TM, D), _layer((40, D), l)],
        out_specs=[_row(TM, D), _const((8, D))],
        out_shape=[jax.ShapeDtypeStruct((T, D), F32), jax.ShapeDtypeStruct((8, D), F32)],
        compiler_params=_cp("arbitrary"),
    )(dh, w, y, sm)


CH = 512


def dwconv_glu_bwd(dy, a, u, sm, smrev, l, name):
    T = dy.shape[0]
    nr, nc = T // TCV, D // CH
    nb = TCV // HALO
    last = T // HALO - 1

    def body(dy_ref, dyn_ref, a_ref, ap_ref, u1_ref, u2_ref, sm_ref, rev_ref, du_ref, dw_ref, shd, sha, da):
        i = pl.program_id(0)
        r = i % nr

        @pl.when(r == 0)
        def _():
            dw_ref[...] = jnp.zeros_like(dw_ref)

        shd[0, 0:TCV, :] = dy_ref[...]
        shd[0, TCV:TCV + HALO, :] = jnp.where(r < nr - 1, dyn_ref[...], 0.0)
        sha[0, 0:HALO, :] = jnp.where(r > 0, ap_ref[...], 0.0)
        sha[0, HALO:HALO + TCV, :] = a_ref[...]
        _make_shifts(shd)
        _make_shifts(sha)
        _conv_taps(shd, rev_ref, da, 0)
        for k in range(CONV_W):
            part = jnp.zeros((SUB, CH), F32)
            for r0 in range(0, TCV, SUB):
                part = part + dy_ref[r0:r0 + SUB, :] * _shifted(sha, HALO - (CONV_W - 1) + k + r0, SUB, slice(None))
            dw_ref[k:k + 1, :] += jnp.sum(part, axis=0, keepdims=True)
        dav = da[...]
        u1 = u1_ref[...].astype(F32)
        sg = _sigmoid(u2_ref[...].astype(F32))
        du_ref[0] = (dav * sg).astype(BF16)
        du_ref[1] = (dav * u1 * sg * (1.0 - sg)).astype(BF16)

    tile = lambda i: (i % nr, i // nr)
    in_specs = [pl.BlockSpec((TCV, CH), tile),
                pl.BlockSpec((HALO, CH), lambda i: (jnp.minimum((i % nr + 1) * nb, last), i // nr)),
                pl.BlockSpec((TCV, CH), tile),
                pl.BlockSpec((HALO, CH), lambda i: (jnp.maximum((i % nr) * nb - 1, 0), i // nr)),
                pl.BlockSpec((TCV, CH), tile), pl.BlockSpec((TCV, CH), lambda i: (i % nr, nc + i // nr)),
                pl.BlockSpec((None, 40, CH), lambda i: (l, 0, i // nr)),
                pl.BlockSpec((None, 40, CH), lambda i: (l, 0, i // nr))]
    return pl.pallas_call(
        body, name=name, grid=(nr * nc,), in_specs=in_specs,
        out_specs=[pl.BlockSpec((2, TCV, CH), lambda i: (0, i % nr, i // nr)),
                   pl.BlockSpec((32, CH), lambda i: (0, i // nr))],
        out_shape=[jax.ShapeDtypeStruct((2, T, D), BF16), jax.ShapeDtypeStruct((32, D), F32)],
        scratch_shapes=[pltpu.VMEM((SUB, TCV + HALO, CH), F32), pltpu.VMEM((SUB, TCV + HALO, CH), F32),
                        pltpu.VMEM((TCV, CH), F32)],
        compiler_params=_cp("arbitrary"),
    )(dy, dy, a, a, u, u, sm, smrev)


def _rows_tile(R):
    for t in (512, 256, 128, 64, 32, 16, 8):
        if R % t == 0:
            return t
    return R


def add8_into(J, l, g, others, where, name):
    R, C = g.shape[2:]
    tr = _rows_tile(R)

    def body(w_ref, g_ref, x_ref, j_in, j_ref):
        acc = g_ref[...].astype(F32)
        for k in range(7):
            acc = acc + x_ref[k].astype(F32)
        j_ref[...] = acc

    return pl.pallas_call(
        body, name=name,
        grid_spec=pltpu.PrefetchScalarGridSpec(
            num_scalar_prefetch=1, grid=(R // tr,),
            in_specs=[pl.BlockSpec((None, None, tr, C), lambda i, w: (w[0], w[1], i, 0)),
                      pl.BlockSpec((7, tr, C), lambda i, w: (0, i, 0)), ANY],
            out_specs=pl.BlockSpec((None, None, tr, C), lambda i, w: (l, w[1], i, 0))),
        out_shape=jax.ShapeDtypeStruct(J.shape, F32), input_output_aliases={3: 0},
        compiler_params=_cp("parallel"),
    )(where, g, others, J)


def adamw(w, g, m, v, name):
    R, C = w.shape
    tr = _rows_tile(R)

    def body(w_ref, g_ref, m_ref, v_ref, d_ref, nm_ref, nv_ref):
        gv = g_ref[...]
        nm = ADAM_B1 * m_ref[...] + (1.0 - ADAM_B1) * gv
        nv = ADAM_B2 * v_ref[...] + (1.0 - ADAM_B2) * (gv * gv)
        m_hat = nm / (1.0 - ADAM_B1 ** ADAM_STEP)
        v_hat = nv / (1.0 - ADAM_B2 ** ADAM_STEP)
        d_ref[...] = -ADAM_LR * (m_hat / (jnp.sqrt(v_hat) + ADAM_EPS) + ADAM_WD * w_ref[...])
        nm_ref[...] = nm
        nv_ref[...] = nv

    sd = jax.ShapeDtypeStruct((R, C), F32)
    return pl.pallas_call(
        body, name=name, grid=(R // tr,),
        in_specs=[_row(tr, C)] * 4, out_specs=[_row(tr, C)] * 3, out_shape=[sd, sd, sd],
        compiler_params=_cp("parallel"),
    )(w, g, m, v)


ANY = pl.BlockSpec(memory_space=pl.ANY)
HBM = pl.BlockSpec(memory_space=pltpu.HBM)
SEM = pl.BlockSpec(memory_space=pltpu.SEMAPHORE)
EFFECT = pltpu.SideEffectType.DATAFLOW_SIDE_EFFECTING


def _place():
    x, y, c = lax.axis_index("x"), lax.axis_index("y"), lax.axis_index("c")
    chips = [(1 - x, y), (x, 1 - y), (1 - x, 1 - y)]
    return x, y, c, chips


def _copy(src, dst, send, recv, k, to):
    return pltpu.make_async_remote_copy(src_ref=src, dst_ref=dst, send_sem=send.at[k], recv_sem=recv.at[k],
                                        device_id=to, device_id_type=MESH)


def xchg_start(name, bufs, plan, n, after=()):
    nb = len(bufs)

    na = len(after)

    def body(*refs):
        send, recv, token = refs[nb + na], refs[nb + na + 1], refs[-1]
        for k, (src, dst, to) in enumerate(plan(refs[:nb])):
            _copy(src, dst, send, recv, k, to).start()
        token[...] = jnp.zeros_like(token)

    outs = pl.pallas_call(
        body, name=name,
        out_shape=(pltpu.SemaphoreType.DMA((n,)), pltpu.SemaphoreType.DMA((n,)),
                   *[pltpu.HBM(b.shape, b.dtype) for b in bufs], jax.ShapeDtypeStruct((8, 128), F32)),
        in_specs=[HBM] * nb + [ANY] * na,
        out_specs=(SEM, SEM, *[HBM] * nb, pl.BlockSpec(memory_space=pltpu.VMEM)),
        input_output_aliases={i: 2 + i for i in range(nb)},
        compiler_params=pltpu.CompilerParams(has_side_effects=EFFECT),
    )(*[pltpu.with_memory_space_constraint(b, pltpu.HBM) for b in bufs], *after)
    return dict(name=name, send=outs[0], recv=outs[1], bufs=list(outs[2:2 + nb]), plan=plan), outs[-1]


def xchg_wait(flight, after):
    bufs, plan = flight["bufs"], flight["plan"]
    nb = len(bufs)

    def body(*refs):
        send, recv = refs[nb], refs[nb + 1]
        for k, (src, dst, to) in enumerate(plan(refs[:nb])):
            cp = _copy(src, dst, send, recv, k, to)
            cp.wait_send()
            cp.wait_recv()

    outs = pl.pallas_call(
        body, name=flight["name"] + "_wait",
        out_shape=tuple(pltpu.HBM(b.shape, b.dtype) for b in bufs),
        in_specs=[HBM] * nb + [SEM, SEM] + [ANY] * len(after),
        out_specs=tuple([HBM] * nb), input_output_aliases={i: i for i in range(nb)},
        compiler_params=pltpu.CompilerParams(has_side_effects=EFFECT),
    )(*bufs, flight["send"], flight["recv"], *after)
    return list(outs)


def _flip(k, x, y, c):
    return ((1 - x) if k & 4 else x, (1 - y) if k & 2 else y, (1 - c) if k & 1 else c)


class WeightGather:
    def __init__(self, shards, groups):
        me = 2 * lax.axis_index("x") + lax.axis_index("y")
        self.names = dict(groups)
        self.ici, self.d2d = {}, {}
        self.token = None
        for gname, names in groups:
            nt = len(names)
            srcs = [shards[n] for n in names]
            lands = [lax.dynamic_update_slice(lax.empty((4,) + s.shape, s.dtype), s[None], (me, 0, 0, 0))
                     for s in srcs]

            def plan(refs, nt=nt):
                x, y, c, chips = _place()
                return [(refs[t].at[c], refs[nt + t].at[2 * x + y, c], (cx, cy, c))
                        for t in range(nt) for cx, cy in chips]

            self.ici[gname], self.token = xchg_start(f"ag_ici_{gname}", srcs + lands, plan, 3 * nt,
                                                     after=[] if self.token is None else [self.token])

    def forward(self, gname, after):
        nt = len(self.names[gname])
        lands = xchg_wait(self.ici.pop(gname), after)[nt:]

        def plan(refs):
            x, y, c, chips = _place()
            out = []
            for t in range(nt):
                for cx, cy in chips:
                    piece = refs[t].at[2 * cx + cy, c]
                    out.append((piece, piece, (x, y, 1 - c)))
            return out

        self.d2d[gname], token = xchg_start(f"ag_d2d_{gname}", lands, plan, 3 * nt)
        return token

    def get(self, gname, after):
        lands = xchg_wait(self.d2d.pop(gname), after)
        return dict(zip(self.names[gname], lands))


class GradReduce:
    def __init__(self, kinds):
        self.J = {k: lax.empty((L, 2, a2, b), F32) for k, (L, a2, b) in kinds.items()}
        self.x, self.j = {}, {}

    @staticmethod
    def _where(name):
        kind, _, l = name.partition("_")
        return kind, int(l or 0)

    def send(self, gname, grads, after=()):
        names = list(grads)
        nt = len(names)
        gs = [grads[n] for n in names]
        xs = [lax.empty((7,) + g.shape[2:], g.dtype) for g in gs]

        def plan(refs):
            x, y, c, _ = _place()
            out = []
            for t in range(nt):
                for k in range(1, 8):
                    px, py, pc = _flip(k, x, y, c)
                    out.append((refs[t].at[2 * px + py, pc], refs[nt + t].at[k - 1], (px, py, pc)))
            return out

        flight, token = xchg_start(f"rs_x_{gname}", gs + xs, plan, 7 * nt, after=after)
        self.x[gname] = (names, flight)
        return token

    def reduce(self, gname, after):
        names, flight = self.x.pop(gname)
        nt = len(names)
        bufs = xchg_wait(flight, after)
        mine = jnp.stack([2 * lax.axis_index("x") + lax.axis_index("y"), lax.axis_index("c")]).astype(jnp.int32)
        where = [self._where(n) for n in names]
        js = [add8_into(self.J[kind], l, bufs[t], bufs[nt + t], mine, f"rs_add_{names[t]}")
              for t, (kind, l) in enumerate(where)]

        def plan(refs):
            x, y, c, _ = _place()
            out = []
            for t in range(nt):
                half = refs[t].at[where[t][1], c]
                out.append((half, half, (x, y, 1 - c)))
            return out

        flight, token = xchg_start(f"rs_join_{gname}", js, plan, nt)
        self.j[gname] = (where, flight)
        return token

    def finish(self, gname, after):
        where, flight = self.j.pop(gname)
        for (kind, _), j in zip(where, xchg_wait(flight, after)):
            self.J[kind] = j


def allreduce_small(v):
    R = v.shape[0]

    def body(v_ref, o_ref, all_ref, send, recv):
        x, y, c, _ = _place()
        me = 4 * x + 2 * y + c
        all_ref[me] = v_ref[...]
        cps = []
        for k in range(1, 8):
            cp = _copy(v_ref, all_ref.at[me], send, recv, k - 1, _flip(k, x, y, c))
            cp.start()
            cps.append(cp)
        for k in range(1, 8):
            px, py, pc = _flip(k, x, y, c)
            _copy(v_ref, all_ref.at[4 * px + 2 * py + pc], send, recv, k - 1, (px, py, pc)).wait_recv()
        for cp in cps:
            cp.wait_send()
        acc = all_ref[0]
        for d in range(1, 8):
            acc = acc + all_ref[d]
        o_ref[...] = acc

    return pl.pallas_call(
        body, name="allreduce_small",
        in_specs=[pl.BlockSpec(memory_space=pltpu.VMEM)], out_specs=pl.BlockSpec(memory_space=pltpu.VMEM),
        out_shape=jax.ShapeDtypeStruct((R, D), F32),
        scratch_shapes=[pltpu.VMEM((8, R, D), F32), pltpu.SemaphoreType.DMA((7,)), pltpu.SemaphoreType.DMA((7,))],
        compiler_params=pltpu.CompilerParams(has_side_effects=True, vmem_limit_bytes=VMEM_LIMIT),
    )(v)


AG_GROUPS = (("a0", ("pw1_0", "pw2_0", "small")), ("f0", ("up_0", "down_0")),
             ("l1", ("pw1_1", "pw2_1", "up_1", "down_1")), ("l2", ("kv", "wq_0", "wo_0", "up_2", "down_2")),
             ("l3", ("wq_1", "wo_1", "up_3", "down_3")))


def _bucket_table():
    qi = np.arange(BLK)[:, None]
    kj = np.arange(2 * BLK)[None, :]
    d = np.maximum(qi + BLK - kj, 0)
    max_exact = N_BUCKETS // 2
    log_ratio = (np.log(np.maximum(d, 1).astype(np.float32) / np.float32(max_exact))
                 / np.float32(math.log(MAX_DISTANCE / max_exact))).astype(np.float32)
    large = max_exact + (log_ratio * np.float32(N_BUCKETS - max_exact)).astype(np.int32)
    large = np.minimum(large, N_BUCKETS - 1)
    return np.where(d < max_exact, d, large).astype(np.int32)


def _heads_major(a, nh):
    T = a.shape[0]
    return a.reshape(T, nh, HD).transpose(1, 0, 2)


def _heads_minor(a):
    nh, T, _ = a.shape
    return a.transpose(1, 0, 2).reshape(T, nh * HD)


def _slots(land):
    return land.reshape(4, 2 * land.shape[2], land.shape[3])


def _rows(land):
    return land.reshape(8 * land.shape[2], land.shape[3])


def _gview(g):
    s, K, n = g.shape
    return g.reshape(4, 2, K // 2, n) if s == 4 else g.reshape(4, 2, K // 8, n)


def _gate(a, token):
    return a + token[0, 0]


def _conv_small(f_small):
    fs = f_small.transpose(1, 2, 0, 3).reshape(2, 40, D)
    b_pw1 = f_small[:, :, 35:37, :].transpose(1, 0, 2, 3).reshape(2, 1, 2 * D)
    rev = jnp.concatenate([fs[:, CONV_W - 1::-1], jnp.zeros((2, 40 - CONV_W, D), F32)], axis=1)
    return dict(conv=fs, conv_rev=rev, b_pw1=b_pw1, b_pw2=fs[:, 34:35])


def run_step(x, target, P, ag, rs):
    T = x.shape[0]
    zero = jnp.zeros((1, 1, D), F32)
    nm, nf = P["norm_mix"], P["norm_ffn"]
    ag.forward("a0", [ag.token])
    W = ag.get("a0", [])
    sm = _conv_small(W["small"])
    h = x
    saved = []
    for l in range(2):
        xn, u, a = norm_mm_glu(h, nm, l, _slots(W[f"pw1_{l}"]), sm["b_pw1"], f"f_pw1_{l}")
        y, s = dwconv_ln_silu(a, sm["conv"], l, f"f_conv_{l}")
        b2 = sm["b_pw2"]
        if l == 0:
            b2 = _gate(b2, ag.forward("f0", [s]))
        h1 = mm_bias_res(s, _rows(W[f"pw2_{l}"]), b2, l, h, f"f_pw2_{l}")
        if l == 0:
            W.update(ag.get("f0", [h1]))
        xn2, gu, f = norm_mm_swiglu(h1, nf, l, _slots(W[f"up_{l}"]), f"f_up_{l}")
        nxt = "l1" if l == 0 else "l2"
        h2 = mm_bias_res(f, _rows(W[f"down_{l}"]), _gate(zero, ag.forward(nxt, [f])), 0, h1, f"f_down_{l}")
        W.update(ag.get(nxt, [h2]))
        saved.append(dict(h=h, xn=xn, u=u, a=a, y=y, s=s, h1=h1, xn2=xn2, gu=gu, f=f))
        h = h2
    h_kv = h
    kvn, kv = norm_mm(h, P["norm_kv"], 0, _rows(W["kv"]), "f_kv")
    kp = jnp.pad(_heads_major(kv[:, :N_KV * HD], N_KV), ((0, 0), (BLK, 0), (0, 0)))
    vp = jnp.pad(_heads_major(kv[:, N_KV * HD:], N_KV), ((0, 0), (BLK, 0), (0, 0)))
    kvt = jnp.pad(kv.T.reshape(2, N_KV, HD, T), ((0, 0), (0, 0), (0, 0), (BLK, 0)))
    kt, vt = kvt[0], kvt[1]
    bucket = _bucket_table()
    onehot = jnp.asarray(np.eye(N_BUCKETS, dtype=np.float32)[bucket])
    bias = jnp.einsum("qkb,bh->hkq", onehot, P["rel_bias"], precision=lax.Precision.HIGHEST)
    bias = bias.reshape(N_KV, GROUP, 2 * BLK, BLK).transpose(0, 2, 1, 3).reshape(1, N_KV, 2 * BLK, QW)
    bias = bias + jnp.asarray(band_mask())[:, None]
    for j in range(2):
        l = 2 + j
        xn, q = norm_mm(h, nm, l, _rows(W[f"wq_{j}"]), f"f_q_{j}", scale=HD ** -0.5)
        qh = q.T.reshape(N_KV, GROUP, HD, T)
        sink = jnp.broadcast_to(P["sinks"][j].reshape(N_KV, GROUP, 1), (N_KV, GROUP, BLK)).reshape(N_KV, 1, QW)
        oh = attn_fwd(qh, kp, vt, bias, sink, f"f_attn_{j}")
        attn = oh.reshape(N_HEADS * HD, T).T
        h1 = mm_bias_res(attn, _rows(W[f"wo_{j}"]), zero, 0, h, f"f_wo_{j}")
        xn2, gu, f = norm_mm_swiglu(h1, nf, l, _slots(W[f"up_{l}"]), f"f_up_{l}")
        zg = _gate(zero, ag.forward("l3", [f])) if j == 0 else zero
        h2 = mm_bias_res(f, _rows(W[f"down_{l}"]), zg, 0, h1, f"f_down_{l}")
        if j == 0:
            W.update(ag.get("l3", [h2]))
        saved.append(dict(h=h, xn=xn, qh=qh, oh=oh, sink=sink, attn=attn, h1=h1, xn2=xn2, gu=gu, f=f))
        h = h2

    dh, st_final = final_loss(h, P["norm_final"], target, "loss_head")

    S = dict(norm_ffn=[None] * 4, norm_mix=[None] * 4, conv=[None] * 2, taps=[None] * 2, b_pw1=[None] * 2,
             b_pw2=[None] * 2, sinks=[None] * 2)

    def ffn_bwd(dh, sv, l, nf, after=()):
        du = mmT_swiglu_bwd(dh, _rows(W[f"down_{l}"]), sv["gu"], f"b_down_{l}", after)
        gd = mm_dw(sv["f"], dh, f"w_down_{l}", 512, 1)
        gu = mm_dw(sv["xn2"], du, f"w_up_{l}", DFF // 2, 4)
        dh, dg = mmT_rmsbwd(du, _slots(W[f"up_{l}"]), sv["h1"], nf, l, dh, f"b_up_{l}")
        S["norm_ffn"][l] = dg[0]
        return dh, {f"down_{l}": _gview(gd), f"up_{l}": _gview(gu)}

    dk = dv = dbias = None
    sent = []
    for j in (1, 0):
        l = 2 + j
        sv = saved[l]
        dh, grads = ffn_bwd(dh, sv, l, nf, sent)
        dattn = mmT(dh, _rows(W[f"wo_{j}"]), f"b_wo_{j}")
        grads[f"wo_{j}"] = _gview(mm_dw(sv["attn"], dh, f"w_wo_{j}", 512, 1))
        doh = dattn.T.reshape(N_KV, GROUP, HD, T)
        dqh, dkj, dvj, dbj, dsj = attn_bwd(sv["qh"], kp, kt, vp, bias, sv["sink"], sv["oh"], doh, f"b_attn_{j}")
        dq = dqh.reshape(N_HEADS * HD, T).T
        grads[f"wq_{j}"] = _gview(mm_dw(sv["xn"], dq, f"w_q_{j}", 512, 1))
        dh, dg = mmT_rmsbwd(dq, _rows(W[f"wq_{j}"])[None], sv["h"], nm, l, dh, f"b_q_{j}")
        S["norm_mix"][l] = dg[0]
        S["sinks"][j] = jnp.sum(dsj.reshape(N_HEADS, BLK), axis=1)
        dk = dkj if dk is None else dk + dkj
        dv = dvj if dv is None else dv + dvj
        dbias = dbj if dbias is None else dbias + dbj
        if j == 1:
            sent = [rs.send("l3", grads)]

    dkv = jnp.concatenate([_heads_minor(dk[:, BLK:]), _heads_minor(dv[:, BLK:])], axis=1).astype(BF16)
    grads["kv"] = _gview(mm_dw(kvn, dkv, "w_kv", 512, 1))
    dh, dg = mmT_rmsbwd(dkv, _rows(W["kv"])[None], h_kv, P["norm_kv"], 0, dh, "b_kv")
    S["norm_kv"] = dg[0]
    dbh = dbias.reshape(N_KV, 2 * BLK, GROUP, BLK)
    S["rel_bias"] = jnp.einsum("vkgq,qkb->bvg", dbh, onehot, precision=lax.Precision.HIGHEST).reshape(N_BUCKETS, N_HEADS)
    sent = [rs.send("l2", grads)]
    nf = _gate(nf, rs.reduce("l3", [dh]))

    for l in (1, 0):
        sv = saved[l]
        dh, grads = ffn_bwd(dh, sv, l, nf, sent)
        conv = sm["conv"]
        if l == 0:
            conv = _gate(conv, rs.send("f0", grads))
            grads = {}
        dy, st = mmT_lnbwd(dh, _rows(W[f"pw2_{l}"]), sv["y"], conv, l, f"b_pw2_{l}")
        g2, S["b_pw2"][l] = mm_dw(sv["s"], dh, f"w_pw2_{l}", 512, 1, colsum=True)
        du, dtaps = dwconv_glu_bwd(dy, sv["a"], sv["u"], sm["conv"], sm["conv_rev"], l, f"b_conv_{l}")
        S["conv"][l] = st[0:3]
        S["taps"][l] = dtaps[0:CONV_W]
        if l == 0:
            rs.finish("l2", [du])
            nm = _gate(nm, rs.reduce("l1", [du]))
        g1, S["b_pw1"][l] = mm_dw(sv["xn"], du, f"w_pw1_{l}", 512, 4, colsum=True)
        grads[f"pw2_{l}"], grads[f"pw1_{l}"] = _gview(g2), _gview(g1)
        dh, dg = mmT_rmsbwd(du, _slots(W[f"pw1_{l}"]), sv["h"], nm, l, dh, f"b_pw1_{l}")
        S["norm_mix"][l] = dg[0]
        if l == 1:
            sent = [rs.send("l1", grads)]
            rs.finish("l3", [dh])
            nf = _gate(nf, rs.reduce("l2", [dh]))
    S["norm_final"] = st_final[0]
    S["loss"] = st_final[1]
    return grads, dh, S


R_CONV = 37
R_SMALL = 88


def _pack_small(S):
    rows = []
    for l in range(2):
        rows += [S["taps"][l], S["conv"][l][2:3], S["conv"][l][0:2], S["b_pw2"][l], S["b_pw1"][l].reshape(2, D)]
    rows += [jnp.stack(S["norm_mix"]), jnp.stack(S["norm_ffn"]), S["norm_kv"][None], S["norm_final"][None]]
    tail = jnp.concatenate([jnp.stack(S["sinks"]).reshape(-1), S["rel_bias"].reshape(-1)])
    rows += [jnp.pad(tail, (0, D - tail.shape[0]))[None], S["loss"][None]]
    v = jnp.concatenate(rows, axis=0)
    return jnp.pad(v, ((0, R_SMALL - v.shape[0]), (0, 0)))


def kernel(x, norm_mix, norm_ffn, conv_w_pw1, conv_b_pw1, conv_w_dw, conv_b_dw, conv_ln_g, conv_ln_b, conv_w_pw2, conv_b_pw2, norm_kv, w_kv, w_q, w_o, sinks, rel_bias, ffn_w_up, ffn_w_down, norm_final, loss_target, m_norm_mix, m_norm_ffn, m_conv_w_pw1, m_conv_b_pw1, m_conv_w_dw, m_conv_b_dw, m_conv_ln_g, m_conv_ln_b, m_conv_w_pw2, m_conv_b_pw2, m_norm_kv, m_w_kv, m_w_q, m_w_o, m_sinks, m_rel_bias, m_ffn_w_up, m_ffn_w_down, m_norm_final, v_norm_mix, v_norm_ffn, v_conv_w_pw1, v_conv_b_pw1, v_conv_w_dw, v_conv_b_dw, v_conv_ln_g, v_conv_ln_b, v_conv_w_pw2, v_conv_b_pw2, v_norm_kv, v_w_kv, v_w_q, v_w_o, v_sinks, v_rel_bias, v_ffn_w_up, v_ffn_w_down, v_norm_final):
    me = 2 * lax.axis_index("x") + lax.axis_index("y")
    weights = dict(norm_mix=norm_mix, norm_ffn=norm_ffn, conv_w_pw1=conv_w_pw1, conv_b_pw1=conv_b_pw1,
                   conv_w_dw=conv_w_dw, conv_b_dw=conv_b_dw, conv_ln_g=conv_ln_g, conv_ln_b=conv_ln_b,
                   conv_w_pw2=conv_w_pw2, conv_b_pw2=conv_b_pw2, norm_kv=norm_kv, w_kv=w_kv, w_q=w_q, w_o=w_o,
                   sinks=sinks, rel_bias=rel_bias, ffn_w_up=ffn_w_up, ffn_w_down=ffn_w_down, norm_final=norm_final)
    mom_m = dict(norm_mix=m_norm_mix, norm_ffn=m_norm_ffn, conv_w_pw1=m_conv_w_pw1, conv_b_pw1=m_conv_b_pw1,
                 conv_w_dw=m_conv_w_dw, conv_b_dw=m_conv_b_dw, conv_ln_g=m_conv_ln_g, conv_ln_b=m_conv_ln_b,
                 conv_w_pw2=m_conv_w_pw2, conv_b_pw2=m_conv_b_pw2, norm_kv=m_norm_kv, w_kv=m_w_kv, w_q=m_w_q,
                 w_o=m_w_o, sinks=m_sinks, rel_bias=m_rel_bias, ffn_w_up=m_ffn_w_up, ffn_w_down=m_ffn_w_down,
                 norm_final=m_norm_final)
    mom_v = dict(norm_mix=v_norm_mix, norm_ffn=v_norm_ffn, conv_w_pw1=v_conv_w_pw1, conv_b_pw1=v_conv_b_pw1,
                 conv_w_dw=v_conv_w_dw, conv_b_dw=v_conv_b_dw, conv_ln_g=v_conv_ln_g, conv_ln_b=v_conv_ln_b,
                 conv_w_pw2=v_conv_w_pw2, conv_b_pw2=v_conv_b_pw2, norm_kv=v_norm_kv, w_kv=v_w_kv, w_q=v_w_q,
                 w_o=v_w_o, sinks=v_sinks, rel_bias=v_rel_bias, ffn_w_up=v_ffn_w_up, ffn_w_down=v_ffn_w_down,
                 norm_final=v_norm_final)

    def halves(a):
        return a.astype(BF16).reshape(2, a.shape[0] // 2, a.shape[1])

    shards = {"kv": halves(w_kv)}
    for l in range(2):
        shards[f"pw1_{l}"], shards[f"pw2_{l}"] = halves(conv_w_pw1[l]), halves(conv_w_pw2[l])
        shards[f"wq_{l}"], shards[f"wo_{l}"] = halves(w_q[l]), halves(w_o[l])
    for l in range(4):
        shards[f"up_{l}"], shards[f"down_{l}"] = halves(ffn_w_up[l]), halves(ffn_w_down[l])
    shards["small"] = jnp.concatenate(
        [conv_w_dw, conv_b_dw[:, None], conv_ln_g[:, None], conv_ln_b[:, None], conv_b_pw2[:, None],
         conv_b_pw1.reshape(2, 2, 256), jnp.zeros((2, 3, 256), F32)], axis=1)
    ag = WeightGather(shards, AG_GROUPS)
    big = {"conv_w_pw1": "pw1", "conv_w_pw2": "pw2", "w_q": "wq", "w_o": "wo", "ffn_w_up": "up",
           "ffn_w_down": "down", "w_kv": "kv"}
    rs = GradReduce({"pw1": (2, 512, 512), "pw2": (2, 128, D), "wq": (2, 128, D), "wo": (2, 128, D),
                     "up": (4, 512, DFF // 2), "down": (4, DFF // 8, D), "kv": (1, 128, 512)})

    P = dict(norm_mix=norm_mix[:, None], norm_ffn=norm_ffn[:, None], norm_kv=norm_kv[None, None],
             norm_final=norm_final[None], sinks=sinks, rel_bias=rel_bias)
    last, grad_x, S = run_step(x[0], loss_target[0], P, ag, rs)

    rs.finish("l1", [grad_x])
    vsum = allreduce_small(_gate(_pack_small(S), rs.reduce("f0", [grad_x])))
    token = rs.send("c0", last, after=[vsum])
    col = lambda a: lax.dynamic_slice_in_dim(a, me * 256, 256, axis=-1)
    grads = {}
    for l in range(2):
        base = l * R_CONV
        grads.setdefault("conv_w_dw", []).append(col(vsum[base:base + 31]))
        grads.setdefault("conv_b_dw", []).append(col(vsum[base + 31]))
        grads.setdefault("conv_ln_g", []).append(col(vsum[base + 32]))
        grads.setdefault("conv_ln_b", []).append(col(vsum[base + 33]))
        grads.setdefault("conv_b_pw2", []).append(col(vsum[base + 34]))
        grads.setdefault("conv_b_pw1", []).append(
            lax.dynamic_slice_in_dim(vsum[base + 35:base + 37].reshape(2 * D), me * 512, 512, axis=0))
    grads = {k: jnp.stack(v) for k, v in grads.items()}
    base = 2 * R_CONV
    grads["norm_mix"] = vsum[base:base + 4]
    grads["norm_ffn"] = vsum[base + 4:base + 8]
    grads["norm_kv"] = vsum[base + 8]
    grads["norm_final"] = vsum[base + 9]
    grads["sinks"] = vsum[base + 10, 0:32].reshape(2, 16)
    grads["rel_bias"] = vsum[base + 10, 32:32 + 512].reshape(32, 16)
    loss = vsum[base + 11, 0]

    delta, new_m, new_v = {}, {}, {}
    rest = [n for n in weights if n not in big]

    def pack(dct):
        flat = jnp.concatenate([dct[n].reshape(-1) for n in rest])
        return jnp.pad(flat, (0, (-flat.shape[0]) % (8 * 128))).reshape(-1, 128)

    d, nm, nv = adamw(pack(weights), _gate(pack(grads), token), pack(mom_m), pack(mom_v), "adamw_small")
    off = 0
    for n in rest:
        shp = weights[n].shape
        sz = int(np.prod(shp))
        delta[n] = d.reshape(-1)[off:off + sz].reshape(shp)
        new_m[n] = nm.reshape(-1)[off:off + sz].reshape(shp)
        new_v[n] = nv.reshape(-1)[off:off + sz].reshape(shp)
        off += sz

    def update(n):
        shp = weights[n].shape
        r2 = (int(np.prod(shp[:-1])), shp[-1])
        grads[n] = rs.J[big[n]].reshape(shp)
        d, nm, nv = adamw(weights[n].reshape(r2), grads[n].reshape(r2), mom_m[n].reshape(r2), mom_v[n].reshape(r2),
                          f"adamw_{n}")
        delta[n], new_m[n], new_v[n] = d.reshape(shp), nm.reshape(shp), nv.reshape(shp)

    rs.finish("f0", [vsum])
    for n in ("ffn_w_up", "ffn_w_down"):
        update(n)
    rs.reduce("c0", [delta["ffn_w_up"], delta["ffn_w_down"]])
    for n in ("w_q", "w_o", "w_kv"):
        update(n)
    rs.finish("c0", [delta["w_kv"]])
    for n in ("conv_w_pw1", "conv_w_pw2"):
        update(n)

    order = list(weights)
    return (loss, grad_x[None], *[grads[n] for n in order], *[delta[n] for n in order],
            *[new_m[n] for n in order], *[new_v[n] for n in order])
```

```python
import functools
import math

import numpy as np
import jax
import jax.numpy as jnp
from jax import lax
from jax.experimental import pallas as pl
from jax.experimental.pallas import tpu as pltpu

F32 = jnp.float32
BF16 = jnp.bfloat16
MESH = pl.DeviceIdType.MESH

D = 1024
DFF = 2816
N_HEADS = 16
N_KV = 4
GROUP = 4
HD = 64
BLK = 128
CONV_W = 31
HALO = 32
N_BUCKETS = 32
MAX_DISTANCE = 128
EPS = 1e-6
NEG_INF = -1e30
TM = 512
TCV = 256
VMEM_LIMIT = 56 * 2 ** 20

ADAM_LR, ADAM_B1, ADAM_B2, ADAM_EPS, ADAM_WD, ADAM_STEP = 0.001, 0.9, 0.999, 1e-08, 0.01, 10


def _cp(*sem):
    return pltpu.CompilerParams(dimension_semantics=sem, vmem_limit_bytes=VMEM_LIMIT)


def _sigmoid(x):
    return 1.0 / (1.0 + jnp.exp(-x))


def _row(tm, n):
    return pl.BlockSpec((tm, n), lambda i: (i, 0))


def _const(shape):
    nd = len(shape)
    return pl.BlockSpec(shape, lambda i: (0,) * nd)


def _weight(shape):
    nd = len(shape)
    return pl.BlockSpec(shape, lambda i: (0,) * nd, pipeline_mode=pl.Buffered(1))


def _layer(shape, l):
    nd = len(shape)
    return pl.BlockSpec((None,) + tuple(shape), lambda i: (l,) + (0,) * nd)


def _dot(a, b):
    return jnp.dot(a, b, preferred_element_type=F32)


def _dot_nt(a, b):
    return lax.dot_general(a, b, (((1,), (1,)), ((), ())), preferred_element_type=F32)


def _dot_tn(a, b):
    return lax.dot_general(a, b, (((0,), (0,)), ((), ())), preferred_element_type=F32)


def _rms(x):
    return lax.rsqrt(jnp.mean(x * x, axis=-1, keepdims=True) + EPS)


def norm_mm_glu(h, g, l, w, b, name):
    T = h.shape[0]
    ns = w.shape[-1]

    def body(h_ref, g_ref, w_ref, b_ref, xn_ref, u_ref, a_ref):
        x = h_ref[...]
        xn = (x * _rms(x) * g_ref[...]).astype(BF16)
        xn_ref[...] = xn
        for s in range(2):
            lo, hi = s * ns, (s + 1) * ns
            u1 = _dot(xn, w_ref[s]) + b_ref[:, lo:hi]
            u2 = _dot(xn, w_ref[2 + s]) + b_ref[:, D + lo:D + hi]
            u_ref[:, lo:hi] = u1.astype(BF16)
            u_ref[:, D + lo:D + hi] = u2.astype(BF16)
            a_ref[:, lo:hi] = u1 * _sigmoid(u2)

    return pl.pallas_call(
        body, name=name, grid=(T // TM,),
        in_specs=[_row(TM, D), _layer((1, D), l), _weight((4, D, ns)), _layer((1, 2 * D), l)],
        out_specs=[_row(TM, D), _row(TM, 2 * D), _row(TM, D)],
        out_shape=[jax.ShapeDtypeStruct((T, D), BF16), jax.ShapeDtypeStruct((T, 2 * D), BF16),
                   jax.ShapeDtypeStruct((T, D), F32)],
        compiler_params=_cp("parallel"),
    )(h, g, w, b)


SUB = 8


def _make_shifts(sh):
    n = TCV + HALO - SUB
    for r in range(1, SUB):
        for r0 in range(0, n, 40):
            sh[r, r0:r0 + 40, :] = sh[0, pl.ds(r + r0, 40), :]


def _shifted(sh, off, rows, cols):
    return sh[off % SUB, pl.ds(off - off % SUB, rows), cols]


def _conv_taps(sh, w_ref, out_ref, first):
    RB, LB = 32, 512
    for r0 in range(0, TCV, RB):
        for c0 in range(0, out_ref.shape[1], LB):
            acc = jnp.zeros((RB, LB), F32)
            for k in range(CONV_W):
                acc = acc + w_ref[k:k + 1, c0:c0 + LB] * _shifted(sh, first + k + r0, RB, slice(c0, c0 + LB))
            out_ref[r0:r0 + RB, c0:c0 + LB] = acc


def dwconv_ln_silu(a, sm, l, name):
    T = a.shape[0]
    nb = TCV // HALO

    def body(cur_ref, prev_ref, sm_ref, y_ref, s_ref, sh):
        i = pl.program_id(0)
        sh[0, 0:HALO, :] = jnp.where(i > 0, prev_ref[...], 0.0)
        sh[0, HALO:HALO + TCV, :] = cur_ref[...]
        _make_shifts(sh)
        _conv_taps(sh, sm_ref, y_ref, HALO - (CONV_W - 1))
        y = y_ref[...] + sm_ref[31:32, :]
        y_ref[...] = y
        mu = jnp.mean(y, axis=-1, keepdims=True)
        yc = y - mu
        rstd = lax.rsqrt(jnp.mean(yc * yc, axis=-1, keepdims=True) + EPS)
        z = yc * rstd * sm_ref[32:33, :] + sm_ref[33:34, :]
        s_ref[...] = (z * _sigmoid(z)).astype(BF16)

    return pl.pallas_call(
        body, name=name, grid=(T // TCV,),
        in_specs=[_row(TCV, D), pl.BlockSpec((HALO, D), lambda i: (jnp.maximum(i * nb - 1, 0), 0)),
                  _layer((40, D), l)],
        out_specs=[_row(TCV, D), _row(TCV, D)],
        out_shape=[jax.ShapeDtypeStruct((T, D), F32), jax.ShapeDtypeStruct((T, D), BF16)],
        scratch_shapes=[pltpu.VMEM((SUB, TCV + HALO, D), F32)],
        compiler_params=_cp("parallel"),
    )(a, a, sm)


def mm_bias_res(xb, w, b, bl, res, name):
    T, K = xb.shape

    def body(x_ref, w_ref, b_ref, r_ref, o_ref):
        o_ref[...] = _dot(x_ref[...], w_ref[...]) + b_ref[...] + r_ref[...]

    return pl.pallas_call(
        body, name=name, grid=(T // TM,),
        in_specs=[_row(TM, K), _weight((K, D)), _layer((1, D), bl), _row(TM, D)],
        out_specs=_row(TM, D), out_shape=jax.ShapeDtypeStruct((T, D), F32),
        compiler_params=_cp("parallel"),
    )(xb, w, b, res)


def norm_mm_swiglu(h, g, l, w, name):
    T = h.shape[0]
    ns = w.shape[-1]

    def body(h_ref, g_ref, w_ref, xn_ref, gu_ref, f_ref):
        x = h_ref[...]
        xn = (x * _rms(x) * g_ref[...]).astype(BF16)
        xn_ref[...] = xn
        for s in range(2):
            lo, hi = s * ns, (s + 1) * ns
            gate = _dot(xn, w_ref[s])
            up = _dot(xn, w_ref[2 + s])
            gu_ref[:, lo:hi] = gate.astype(BF16)
            gu_ref[:, DFF + lo:DFF + hi] = up.astype(BF16)
            f_ref[:, lo:hi] = (gate * _sigmoid(gate) * up).astype(BF16)

    return pl.pallas_call(
        body, name=name, grid=(T // TM,),
        in_specs=[_row(TM, D), _layer((1, D), l), _weight((4, D, ns))],
        out_specs=[_row(TM, D), _row(TM, 2 * DFF), _row(TM, DFF)],
        out_shape=[jax.ShapeDtypeStruct((T, D), BF16), jax.ShapeDtypeStruct((T, 2 * DFF), BF16),
                   jax.ShapeDtypeStruct((T, DFF), BF16)],
        compiler_params=_cp("parallel"),
    )(h, g, w)


def norm_mm(h, g, gl, w, name, scale=1.0):
    T = h.shape[0]
    N = w.shape[-1]

    def body(h_ref, g_ref, w_ref, xn_ref, o_ref):
        x = h_ref[...]
        xn = (x * _rms(x) * g_ref[...]).astype(BF16)
        xn_ref[...] = xn
        o_ref[...] = (_dot(xn, w_ref[...]) * scale).astype(BF16)

    return pl.pallas_call(
        body, name=name, grid=(T // TM,),
        in_specs=[_row(TM, D), _layer((1, D), gl), _weight((D, N))],
        out_specs=[_row(TM, D), _row(TM, N)],
        out_shape=[jax.ShapeDtypeStruct((T, D), BF16), jax.ShapeDtypeStruct((T, N), BF16)],
        compiler_params=_cp("parallel"),
    )(h, g, w)


QB = 4
QW = GROUP * BLK


def band_mask():
    qi = np.arange(QW)[None, :] % BLK
    kj = np.arange(2 * BLK)[:, None]
    band = ((kj < BLK) & (kj > qi)) | ((kj >= BLK) & (kj - BLK <= qi))
    first = band & (kj >= BLK)
    return np.where(np.stack([first, band]), 0.0, NEG_INF).astype(np.float32)


def _softmax_cols(s, sink):
    m = jnp.maximum(jnp.max(s, axis=0, keepdims=True), sink)
    p = jnp.exp(s - m)
    es = jnp.exp(sink - m)
    inv = 1.0 / (jnp.sum(p, axis=0, keepdims=True) + es)
    return p, inv, es


def _attn_specs(T):
    W = QB * BLK
    qspec = pl.BlockSpec((None, GROUP, HD, W), lambda kv, n: (kv, 0, 0, n))
    kspec = pl.BlockSpec((None, T + BLK, HD), lambda kv, n: (kv, 0, 0))
    ktspec = [pl.BlockSpec((None, HD, W), lambda kv, n: (kv, 0, n)),
              pl.BlockSpec((None, HD, BLK), lambda kv, n: (kv, 0, (n + 1) * QB))]
    bspec = pl.BlockSpec((2, None, 2 * BLK, QW), lambda kv, n: (0, kv, 0, 0))
    sspec = pl.BlockSpec((None, 1, QW), lambda kv, n: (kv, 0, 0))
    return qspec, kspec, ktspec, bspec, sspec


def _attn_block(n, b):
    blk = n * QB + b
    rows = pl.ds(pl.multiple_of(blk * BLK, BLK), 2 * BLK)
    return rows, (jnp.minimum(blk, 1) if b == 0 else 1)


def _band_cols(main_ref, tail_ref, b):
    if b < QB - 1:
        return main_ref[:, b * BLK:(b + 2) * BLK]
    return jnp.concatenate([main_ref[:, b * BLK:], tail_ref[...]], axis=1)


def _heads_side_by_side(ref, qs):
    return jnp.concatenate([ref[g, :, qs] for g in range(GROUP)], axis=1)


def attn_fwd(q, kp, vt, bias, sink, name):
    T = q.shape[3]
    qspec, kspec, ktspec, bspec, sspec = _attn_specs(T)

    def body(q_ref, k_ref, vt_ref, vtt_ref, b_ref, s_ref, o_ref, pb):
        n = pl.program_id(1)
        for b in range(QB):
            rows, table = _attn_block(n, b)
            qs = slice(b * BLK, (b + 1) * BLK)
            st = _dot(k_ref[rows, :], _heads_side_by_side(q_ref, qs))
            for g in range(GROUP):
                hs = slice(g * BLK, (g + 1) * BLK)
                p, inv, _ = _softmax_cols(st[:, hs] + b_ref[table, :, hs], s_ref[:, hs])
                pb[:, hs] = (p * inv).astype(BF16)
            ot = _dot(_band_cols(vt_ref, vtt_ref, b), pb[...])
            for g in range(GROUP):
                o_ref[g, :, qs] = ot[:, g * BLK:(g + 1) * BLK].astype(BF16)

    return pl.pallas_call(
        body, name=name, grid=(N_KV, T // (QB * BLK)),
        in_specs=[qspec, kspec, *ktspec, bspec, sspec], out_specs=qspec,
        out_shape=jax.ShapeDtypeStruct((N_KV, GROUP, HD, T), BF16),
        scratch_shapes=[pltpu.VMEM((2 * BLK, QW), BF16)],
        compiler_params=_cp("parallel", "parallel"),
    )(q, kp, vt, vt, bias, sink)


def attn_bwd(q, kp, kt, vp, bias, sink, o, do, name):
    T = q.shape[3]
    qspec, kspec, ktspec, bspec, sspec = _attn_specs(T)

    def body(q_ref, k_ref, kt_ref, ktt_ref, v_ref, b_ref, s_ref, o_ref, do_ref,
             dq_ref, dk_ref, dv_ref, db_ref, ds_ref, pb, dsb):
        n = pl.program_id(1)

        @pl.when(n == 0)
        def _():
            dk_ref[...] = jnp.zeros_like(dk_ref)
            dv_ref[...] = jnp.zeros_like(dv_ref)
            db_ref[...] = jnp.zeros_like(db_ref)
            ds_ref[...] = jnp.zeros_like(ds_ref)

        for b in range(QB):
            rows, table = _attn_block(n, b)
            qs = slice(b * BLK, (b + 1) * BLK)
            q4 = _heads_side_by_side(q_ref, qs)
            do4 = _heads_side_by_side(do_ref, qs)
            st = _dot(k_ref[rows, :], q4)
            dpt = _dot(v_ref[rows, :], do4)
            for g in range(GROUP):
                hs = slice(g * BLK, (g + 1) * BLK)
                p, inv, es = _softmax_cols(st[:, hs] + b_ref[table, :, hs], s_ref[:, hs])
                probs = p * inv
                delta = jnp.sum(do_ref[g, :, qs].astype(F32) * o_ref[g, :, qs].astype(F32), axis=0, keepdims=True)
                dS = probs * (dpt[:, hs] - delta)
                ds_ref[:, hs] += -(es * inv) * delta
                db_ref[:, hs] += dS
                pb[:, hs] = probs.astype(BF16)
                dsb[:, hs] = dS.astype(BF16)
            dqt = _dot(_band_cols(kt_ref, ktt_ref, b), dsb[...]) * (HD ** -0.5)
            for g in range(GROUP):
                dq_ref[g, :, qs] = dqt[:, g * BLK:(g + 1) * BLK].astype(BF16)
            dk_ref[rows, :] += _dot_nt(dsb[...], q4)
            dv_ref[rows, :] += _dot_nt(pb[...], do4)

    kout = pl.BlockSpec((None, T + BLK, HD), lambda kv, n: (kv, 0, 0))
    dbspec = pl.BlockSpec((None, 2 * BLK, QW), lambda kv, n: (kv, 0, 0))
    return pl.pallas_call(
        body, name=name, grid=(N_KV, T // (QB * BLK)),
        in_specs=[qspec, kspec, *ktspec, kspec, bspec, sspec, qspec, qspec],
        out_specs=[qspec, kout, kout, dbspec, sspec],
        out_shape=[jax.ShapeDtypeStruct((N_KV, GROUP, HD, T), BF16),
                   jax.ShapeDtypeStruct((N_KV, T + BLK, HD), F32), jax.ShapeDtypeStruct((N_KV, T + BLK, HD), F32),
                   jax.ShapeDtypeStruct((N_KV, 2 * BLK, QW), F32), jax.ShapeDtypeStruct((N_KV, 1, QW), F32)],
        scratch_shapes=[pltpu.VMEM((2 * BLK, QW), BF16), pltpu.VMEM((2 * BLK, QW), BF16)],
        compiler_params=_cp("parallel", "arbitrary"),
    )(q, kp, kt, kt, vp, bias, sink, o, do)


def final_loss(h, g, target, name):
    T = h.shape[0]

    def body(h_ref, g_ref, t_ref, dh_ref, st_ref):
        i = pl.program_id(0)

        @pl.when(i == 0)
        def _():
            st_ref[...] = jnp.zeros_like(st_ref)

        x = h_ref[...]
        r = _rms(x)
        xh = x * r
        e = xh * g_ref[...] - t_ref[...]
        loss = 0.5 * jnp.sum(jnp.mean(e * e, axis=-1, keepdims=True))
        dy = e * (1.0 / D)
        st_ref[0:1, :] += jnp.sum(dy * xh, axis=0, keepdims=True)
        lane = lax.broadcasted_iota(jnp.int32, (1, D), 1)
        st_ref[1:2, :] += jnp.where(lane == 0, loss, 0.0)
        dxh = dy * g_ref[...]
        dh_ref[...] = r * (dxh - xh * jnp.mean(dxh * xh, axis=-1, keepdims=True))

    return pl.pallas_call(
        body, name=name, grid=(T // TM,),
        in_specs=[_row(TM, D), _const((1, D)), _row(TM, D)],
        out_specs=[_row(TM, D), _const((8, D))],
        out_shape=[jax.ShapeDtypeStruct((T, D), F32), jax.ShapeDtypeStruct((8, D), F32)],
        compiler_params=_cp("arbitrary"),
    )(h, g, target)


def mm_dw(x, dy, name, tn, slots, colsum=False):
    T, K = x.shape
    split = dy.ndim == 3
    N = dy.shape[-1] * (2 if split else 1)
    tt = min(T, 2048 if K <= 1024 else 1024)
    nt = T // tt
    ns = N // slots
    per = ns // tn

    def body(x_ref, dy_ref, *rest):
        if colsum:
            dw_ref, cs_ref, acc, cacc = rest
        else:
            dw_ref, acc = rest
        t = pl.program_id(1)

        @pl.when(t == 0)
        def _():
            acc[...] = jnp.zeros_like(acc)
            if colsum:
                cacc[...] = jnp.zeros_like(cacc)

        dyv = dy_ref[...]
        acc[...] += _dot_tn(x_ref[...].astype(BF16), dyv.astype(BF16))
        if colsum:
            cacc[...] += jnp.sum(dyv.astype(F32), axis=0, keepdims=True)

        @pl.when(t == nt - 1)
        def _():
            dw_ref[...] = acc[...].astype(BF16)
            if colsum:
                cs_ref[...] = cacc[...]

    if split:
        half = N // 2 // tn
        dy_spec = pl.BlockSpec((None, tt, tn), lambda j, t: (j // half, t, j % half))
    else:
        dy_spec = pl.BlockSpec((tt, tn), lambda j, t: (t, j))
    out_specs = [pl.BlockSpec((None, K, tn), lambda j, t: (j // per, 0, j % per))]
    out_shape = [jax.ShapeDtypeStruct((slots, K, ns), BF16)]
    scratch = [pltpu.VMEM((K, tn), F32)]
    if colsum:
        out_specs.append(pl.BlockSpec((1, tn), lambda j, t: (0, j)))
        out_shape.append(jax.ShapeDtypeStruct((1, N), F32))
        scratch.append(pltpu.VMEM((1, tn), F32))
    res = pl.pallas_call(
        body, name=name, grid=(N // tn, nt),
        in_specs=[pl.BlockSpec((tt, K), lambda j, t: (t, 0)), dy_spec],
        out_specs=out_specs, out_shape=out_shape, scratch_shapes=scratch,
        compiler_params=_cp("parallel", "arbitrary"),
    )(x, dy)
    return tuple(res) if colsum else res[0]


def mmT_swiglu_bwd(dh, w, gu, name, after=()):
    T = dh.shape[0]
    cw = 256

    def body(dh_ref, w_ref, gu_ref, *rest):
        du_ref = rest[-1]
        dhb = dh_ref[...].astype(BF16)
        for lo in range(0, DFF, cw):
            hi = lo + cw
            df = _dot_nt(dhb, w_ref[lo:hi, :])
            gate = gu_ref[:, lo:hi].astype(F32)
            up = gu_ref[:, DFF + lo:DFF + hi].astype(F32)
            sg = _sigmoid(gate)
            silu = gate * sg
            du_ref[:, lo:hi] = (df * (up * (sg + silu * (1.0 - sg)))).astype(BF16)
            du_ref[:, DFF + lo:DFF + hi] = (df * silu).astype(BF16)

    return pl.pallas_call(
        body, name=name, grid=(T // TM,),
        in_specs=[_row(TM, D), _weight((DFF, D)), _row(TM, 2 * DFF)] + [ANY] * len(after),
        out_specs=_row(TM, 2 * DFF), out_shape=jax.ShapeDtypeStruct((T, 2 * DFF), BF16),
        compiler_params=_cp("parallel"),
    )(dh, w, gu, *after)


def mmT_rmsbwd(du, w, h, g, gl, dh_in, name):
    split = du.ndim == 3
    T = du.shape[-2]
    N = du.shape[-1] * (2 if split else 1)
    slots = w.shape[0]
    ns = N // slots

    def piece(du_ref, s):
        if split:
            per = slots // 2
            return du_ref[s // per, :, (s % per) * ns:(s % per + 1) * ns]
        return du_ref[:, s * ns:(s + 1) * ns]

    def body(du_ref, w_ref, h_ref, g_ref, di_ref, dh_ref, dg_ref):
        i = pl.program_id(0)

        @pl.when(i == 0)
        def _():
            dg_ref[...] = jnp.zeros_like(dg_ref)

        dxn = _dot_nt(piece(du_ref, 0), w_ref[0])
        for s in range(1, slots):
            dxn = dxn + _dot_nt(piece(du_ref, s), w_ref[s])
        x = h_ref[...]
        r = _rms(x)
        xh = x * r
        dg_ref[0:1, :] += jnp.sum(dxn * xh, axis=0, keepdims=True)
        dxh = dxn * g_ref[...]
        dh_ref[...] = di_ref[...] + r * (dxh - xh * jnp.mean(dxh * xh, axis=-1, keepdims=True))

    return pl.pallas_call(
        body, name=name, grid=(T // TM,),
        in_specs=[pl.BlockSpec((2, TM, N // 2), lambda i: (0, i, 0)) if split else _row(TM, N),
                  _weight((slots, D, ns)), _row(TM, D), _layer((1, D), gl), _row(TM, D)],
        out_specs=[_row(TM, D), _const((8, D))],
        out_shape=[jax.ShapeDtypeStruct((T, D), F32), jax.ShapeDtypeStruct((8, D), F32)],
        compiler_params=_cp("arbitrary"),
    )(du, w, h, g, dh_in)


def mmT(dh, w, name):
    T = dh.shape[0]
    N = w.shape[0]

    def body(dh_ref, w_ref, o_ref):
        o_ref[...] = _dot_nt(dh_ref[...].astype(BF16), w_ref[...]).astype(BF16)

    return pl.pallas_call(
        body, name=name, grid=(T // TM,),
        in_specs=[_row(TM, D), _weight((N, D))],
        out_specs=_row(TM, N), out_shape=jax.ShapeDtypeStruct((T, N), BF16),
        compiler_params=_cp("parallel"),
    )(dh, w)


def mmT_lnbwd(dh, w, y, sm, l, name):
    T = dh.shape[0]

    def body(dh_ref, w_ref, y_ref, sm_ref, dy_ref, st_ref):
        i = pl.program_id(0)

        @pl.when(i == 0)
        def _():
            st_ref[...] = jnp.zeros_like(st_ref)

        ds = _dot_nt(dh_ref[...].astype(BF16), w_ref[...])
        y = y_ref[...]
        mu = jnp.mean(y, axis=-1, keepdims=True)
        yc = y - mu
        rstd = lax.rsqrt(jnp.mean(yc * yc, axis=-1, keepdims=True) + EPS)
        xh = yc * rstd
        gam = sm_ref[32:33, :]
        z = xh * gam + sm_ref[33:34, :]
        sg = _sigmoid(z)
        dz = ds * sg * (1.0 + z * (1.0 - sg))
        st_ref[0:1, :] += jnp.sum(dz * xh, axis=0, keepdims=True)
        st_ref[1:2, :] += jnp.sum(dz, axis=0, keepdims=True)
        dxh = dz * gam
        dy = rstd * (dxh - jnp.mean(dxh, axis=-1, keepdims=True) - xh * jnp.mean(dxh * xh, axis=-1, keepdims=True))
        st_ref[2:3, :] += jnp.sum(dy, axis=0, keepdims=True)
        dy_ref[...] = dy

    return pl.pallas_call(
        body, name=name, grid=(T // TM,),
        in_specs=[_row(TM, D), _weight((D, D)), _row(TM, D), _layer((40, D), l)],
        out_specs=[_row(TM, D), _const((8, D))],
        out_shape=[jax.ShapeDtypeStruct((T, D), F32), jax.ShapeDtypeStruct((8, D), F32)],
        compiler_params=_cp("arbitrary"),
    )(dh, w, y, sm)


CH = 512


def dwconv_glu_bwd(dy, a, u, sm, smrev, l, name):
    T = dy.shape[0]
    nr, nc = T // TCV, D // CH
    nb = TCV // HALO
    last = T // HALO - 1

    def body(dy_ref, dyn_ref, a_ref, ap_ref, u1_ref, u2_ref, sm_ref, rev_ref, du_ref, dw_ref, shd, sha, da):
        i = pl.program_id(0)
        r = i % nr

        @pl.when(r == 0)
        def _():
            dw_ref[...] = jnp.zeros_like(dw_ref)

        shd[0, 0:TCV, :] = dy_ref[...]
        shd[0, TCV:TCV + HALO, :] = jnp.where(r < nr - 1, dyn_ref[...], 0.0)
        sha[0, 0:HALO, :] = jnp.where(r > 0, ap_ref[...], 0.0)
        sha[0, HALO:HALO + TCV, :] = a_ref[...]
        _make_shifts(shd)
        _make_shifts(sha)
        _conv_taps(shd, rev_ref, da, 0)
        for k in range(CONV_W):
            part = jnp.zeros((SUB, CH), F32)
            for r0 in range(0, TCV, SUB):
                part = part + dy_ref[r0:r0 + SUB, :] * _shifted(sha, HALO - (CONV_W - 1) + k + r0, SUB, slice(None))
            dw_ref[k:k + 1, :] += jnp.sum(part, axis=0, keepdims=True)
        dav = da[...]
        u1 = u1_ref[...].astype(F32)
        sg = _sigmoid(u2_ref[...].astype(F32))
        du_ref[0] = (dav * sg).astype(BF16)
        du_ref[1] = (dav * u1 * sg * (1.0 - sg)).astype(BF16)

    tile = lambda i: (i % nr, i // nr)
    in_specs = [pl.BlockSpec((TCV, CH), tile),
                pl.BlockSpec((HALO, CH), lambda i: (jnp.minimum((i % nr + 1) * nb, last), i // nr)),
                pl.BlockSpec((TCV, CH), tile),
                pl.BlockSpec((HALO, CH), lambda i: (jnp.maximum((i % nr) * nb - 1, 0), i // nr)),
                pl.BlockSpec((TCV, CH), tile), pl.BlockSpec((TCV, CH), lambda i: (i % nr, nc + i // nr)),
                pl.BlockSpec((None, 40, CH), lambda i: (l, 0, i // nr)),
                pl.BlockSpec((None, 40, CH), lambda i: (l, 0, i // nr))]
    return pl.pallas_call(
        body, name=name, grid=(nr * nc,), in_specs=in_specs,
        out_specs=[pl.BlockSpec((2, TCV, CH), lambda i: (0, i % nr, i // nr)),
                   pl.BlockSpec((32, CH), lambda i: (0, i // nr))],
        out_shape=[jax.ShapeDtypeStruct((2, T, D), BF16), jax.ShapeDtypeStruct((32, D), F32)],
        scratch_shapes=[pltpu.VMEM((SUB, TCV + HALO, CH), F32), pltpu.VMEM((SUB, TCV + HALO, CH), F32),
                        pltpu.VMEM((TCV, CH), F32)],
        compiler_params=_cp("arbitrary"),
    )(dy, dy, a, a, u, u, sm, smrev)


def _rows_tile(R):
    for t in (512, 256, 128, 64, 32, 16, 8):
        if R % t == 0:
            return t
    return R


def add8_into(J, l, g, others, where, name):
    R, C = g.shape[2:]
    tr = R // 2

    def body(w_ref, g_ref, x_ref, j_in, j_ref):
        acc = g_ref[...].astype(F32)
        for k in range(7):
            acc = acc + x_ref[k].astype(F32)
        j_ref[...] = acc

    return pl.pallas_call(
        body, name=name,
        grid_spec=pltpu.PrefetchScalarGridSpec(
            num_scalar_prefetch=1, grid=(R // tr,),
            in_specs=[pl.BlockSpec((None, None, tr, C), lambda i, w: (w[0], w[1], i, 0)),
                      pl.BlockSpec((7, tr, C), lambda i, w: (0, i, 0)), ANY],
            out_specs=pl.BlockSpec((None, None, tr, C), lambda i, w: (l, w[1], i, 0))),
        out_shape=jax.ShapeDtypeStruct(J.shape, F32), input_output_aliases={3: 0},
        compiler_params=_cp("parallel"),
    )(where, g, others, J)


def adamw(w, g, m, v, name, copy_g=False):
    R, C = w.shape
    tr = _rows_tile(R)

    def body(w_ref, g_ref, m_ref, v_ref, *outs):
        d_ref, nm_ref, nv_ref = outs[-3:]
        gv = g_ref[...]
        if copy_g:
            outs[0][...] = gv
        nm = ADAM_B1 * m_ref[...] + (1.0 - ADAM_B1) * gv
        nv = ADAM_B2 * v_ref[...] + (1.0 - ADAM_B2) * (gv * gv)
        m_hat = nm / (1.0 - ADAM_B1 ** ADAM_STEP)
        v_hat = nv / (1.0 - ADAM_B2 ** ADAM_STEP)
        d_ref[...] = -ADAM_LR * (m_hat / (jnp.sqrt(v_hat) + ADAM_EPS) + ADAM_WD * w_ref[...])
        nm_ref[...] = nm
        nv_ref[...] = nv

    sd = jax.ShapeDtypeStruct((R, C), F32)
    n_out = 4 if copy_g else 3
    return pl.pallas_call(
        body, name=name, grid=(R // tr,),
        in_specs=[_row(tr, C)] * 4, out_specs=[_row(tr, C)] * n_out, out_shape=[sd] * n_out,
        compiler_params=_cp("parallel"),
    )(w, g, m, v)


ANY = pl.BlockSpec(memory_space=pl.ANY)
HBM = pl.BlockSpec(memory_space=pltpu.HBM)
SEM = pl.BlockSpec(memory_space=pltpu.SEMAPHORE)
EFFECT = pltpu.SideEffectType.DATAFLOW_SIDE_EFFECTING


def _place():
    x, y, c = lax.axis_index("x"), lax.axis_index("y"), lax.axis_index("c")
    chips = [(1 - x, y), (x, 1 - y), (1 - x, 1 - y)]
    return x, y, c, chips


def _copy(src, dst, send, recv, k, to):
    return pltpu.make_async_remote_copy(src_ref=src, dst_ref=dst, send_sem=send.at[k], recv_sem=recv.at[k],
                                        device_id=to, device_id_type=MESH)


def xchg_start(name, bufs, plan, n, after=()):
    nb = len(bufs)

    na = len(after)

    def body(*refs):
        send, recv, token = refs[nb + na], refs[nb + na + 1], refs[-1]
        for k, (src, dst, to) in enumerate(plan(refs[:nb])):
            _copy(src, dst, send, recv, k, to).start()
        token[...] = jnp.zeros_like(token)

    outs = pl.pallas_call(
        body, name=name,
        out_shape=(pltpu.SemaphoreType.DMA((n,)), pltpu.SemaphoreType.DMA((n,)),
                   *[pltpu.HBM(b.shape, b.dtype) for b in bufs], jax.ShapeDtypeStruct((8, 128), F32)),
        in_specs=[HBM] * nb + [ANY] * na,
        out_specs=(SEM, SEM, *[HBM] * nb, pl.BlockSpec(memory_space=pltpu.VMEM)),
        input_output_aliases={i: 2 + i for i in range(nb)},
        compiler_params=pltpu.CompilerParams(has_side_effects=EFFECT),
    )(*[pltpu.with_memory_space_constraint(b, pltpu.HBM) for b in bufs], *after)
    return dict(name=name, send=outs[0], recv=outs[1], bufs=list(outs[2:2 + nb]), plan=plan), outs[-1]


def xchg_wait(flight, after):
    bufs, plan = flight["bufs"], flight["plan"]
    nb = len(bufs)

    def body(*refs):
        send, recv = refs[nb], refs[nb + 1]
        for k, (src, dst, to) in enumerate(plan(refs[:nb])):
            cp = _copy(src, dst, send, recv, k, to)
            cp.wait_send()
            cp.wait_recv()

    outs = pl.pallas_call(
        body, name=flight["name"] + "_wait",
        out_shape=tuple(pltpu.HBM(b.shape, b.dtype) for b in bufs),
        in_specs=[HBM] * nb + [SEM, SEM] + [ANY] * len(after),
        out_specs=tuple([HBM] * nb), input_output_aliases={i: i for i in range(nb)},
        compiler_params=pltpu.CompilerParams(has_side_effects=EFFECT),
    )(*bufs, flight["send"], flight["recv"], *after)
    return list(outs)


def _flip(k, x, y, c):
    return ((1 - x) if k & 4 else x, (1 - y) if k & 2 else y, (1 - c) if k & 1 else c)


class WeightGather:
    def __init__(self, shard, groups):
        me = 2 * lax.axis_index("x") + lax.axis_index("y")
        self.names = dict(groups)
        self.ici, self.d2d = {}, {}
        self.token = None
        for gname, names in groups:
            nt = len(names)
            srcs = [shard(n, self.token) for n in names]
            lands = [lax.dynamic_update_slice(lax.empty((4,) + s.shape, s.dtype), s[None], (me, 0, 0, 0))
                     for s in srcs]

            def plan(refs, nt=nt):
                x, y, c, chips = _place()
                return [(refs[t].at[c], refs[nt + t].at[2 * x + y, c], (cx, cy, c))
                        for t in range(nt) for cx, cy in chips]

            self.ici[gname], self.token = xchg_start(f"ag_ici_{gname}", srcs + lands, plan, 3 * nt,
                                                     after=[] if self.token is None else [self.token])

    def forward(self, gname, after):
        nt = len(self.names[gname])
        lands = xchg_wait(self.ici.pop(gname), after)[nt:]

        def plan(refs):
            x, y, c, chips = _place()
            out = []
            for t in range(nt):
                for cx, cy in chips:
                    piece = refs[t].at[2 * cx + cy, c]
                    out.append((piece, piece, (x, y, 1 - c)))
            return out

        self.d2d[gname], token = xchg_start(f"ag_d2d_{gname}", lands, plan, 3 * nt)
        return token

    def get(self, gname, after):
        lands = xchg_wait(self.d2d.pop(gname), after)
        return dict(zip(self.names[gname], lands))


class GradReduce:
    def __init__(self, kinds):
        self.J = {k: lax.empty((L, 2, a2, b), F32) for k, (L, a2, b) in kinds.items()}
        self.x, self.j = {}, {}

    @staticmethod
    def _where(name):
        kind, _, l = name.partition("_")
        return kind, int(l or 0)

    def send(self, gname, grads, after=()):
        names = list(grads)
        nt = len(names)
        gs = [grads[n] for n in names]
        xs = [lax.empty((7,) + g.shape[2:], g.dtype) for g in gs]

        def plan(refs):
            x, y, c, _ = _place()
            out = []
            for t in range(nt):
                for k in range(1, 8):
                    px, py, pc = _flip(k, x, y, c)
                    out.append((refs[t].at[2 * px + py, pc], refs[nt + t].at[k - 1], (px, py, pc)))
            return out

        flight, token = xchg_start(f"rs_x_{gname}", gs + xs, plan, 7 * nt, after=after)
        self.x[gname] = (names, flight)
        return token

    def reduce(self, gname, after):
        names, flight = self.x.pop(gname)
        nt = len(names)
        bufs = xchg_wait(flight, after)
        mine = jnp.stack([2 * lax.axis_index("x") + lax.axis_index("y"), lax.axis_index("c")]).astype(jnp.int32)
        where = [self._where(n) for n in names]
        js = [add8_into(self.J[kind], l, bufs[t], bufs[nt + t], mine, f"rs_add_{names[t]}")
              for t, (kind, l) in enumerate(where)]

        def plan(refs):
            x, y, c, _ = _place()
            out = []
            for t in range(nt):
                half = refs[t].at[where[t][1], c]
                out.append((half, half, (x, y, 1 - c)))
            return out

        flight, token = xchg_start(f"rs_join_{gname}", js, plan, nt)
        self.j[gname] = (where, flight)
        return token

    def finish(self, gname, after):
        where, flight = self.j.pop(gname)
        for (kind, _), j in zip(where, xchg_wait(flight, after)):
            self.J[kind] = j


def allreduce_small(v):
    R = v.shape[0]

    def body(v_ref, o_ref, all_ref, send, recv):
        x, y, c, _ = _place()
        me = 4 * x + 2 * y + c
        all_ref[me] = v_ref[...]
        cps = []
        for k in range(1, 8):
            cp = _copy(v_ref, all_ref.at[me], send, recv, k - 1, _flip(k, x, y, c))
            cp.start()
            cps.append(cp)
        for k in range(1, 8):
            px, py, pc = _flip(k, x, y, c)
            _copy(v_ref, all_ref.at[4 * px + 2 * py + pc], send, recv, k - 1, (px, py, pc)).wait_recv()
        for cp in cps:
            cp.wait_send()
        acc = all_ref[0]
        for d in range(1, 8):
            acc = acc + all_ref[d]
        o_ref[...] = acc

    return pl.pallas_call(
        body, name="allreduce_small",
        in_specs=[pl.BlockSpec(memory_space=pltpu.VMEM)], out_specs=pl.BlockSpec(memory_space=pltpu.VMEM),
        out_shape=jax.ShapeDtypeStruct((R, D), F32),
        scratch_shapes=[pltpu.VMEM((8, R, D), F32), pltpu.SemaphoreType.DMA((7,)), pltpu.SemaphoreType.DMA((7,))],
        compiler_params=pltpu.CompilerParams(has_side_effects=True, vmem_limit_bytes=VMEM_LIMIT),
    )(v)


AG_GROUPS = (("a0", ("pw1_0", "pw2_0", "small")), ("f0", ("up_0", "down_0")),
             ("l1", ("pw1_1", "pw2_1", "up_1", "down_1")), ("l2", ("kv", "wq_0", "wo_0", "up_2", "down_2")),
             ("l3", ("wq_1", "wo_1", "up_3", "down_3")))


def _bucket_table():
    qi = np.arange(BLK)[:, None]
    kj = np.arange(2 * BLK)[None, :]
    d = np.maximum(qi + BLK - kj, 0)
    max_exact = N_BUCKETS // 2
    log_ratio = (np.log(np.maximum(d, 1).astype(np.float32) / np.float32(max_exact))
                 / np.float32(math.log(MAX_DISTANCE / max_exact))).astype(np.float32)
    large = max_exact + (log_ratio * np.float32(N_BUCKETS - max_exact)).astype(np.int32)
    large = np.minimum(large, N_BUCKETS - 1)
    return np.where(d < max_exact, d, large).astype(np.int32)


def _heads_major(a, nh):
    T = a.shape[0]
    return a.reshape(T, nh, HD).transpose(1, 0, 2)


def _heads_minor(a):
    nh, T, _ = a.shape
    return a.transpose(1, 0, 2).reshape(T, nh * HD)


def _slots(land):
    return land.reshape(4, 2 * land.shape[2], land.shape[3])


def _rows(land):
    return land.reshape(8 * land.shape[2], land.shape[3])


def _gview(g):
    s, K, n = g.shape
    return g.reshape(4, 2, K // 2, n) if s == 4 else g.reshape(4, 2, K // 8, n)


def _gate(a, token):
    return a * (1.0 + token[0, 0])


def _conv_small(f_small):
    fs = f_small.transpose(1, 2, 0, 3).reshape(2, 40, D)
    b_pw1 = f_small[:, :, 35:37, :].transpose(1, 0, 2, 3).reshape(2, 1, 2 * D)
    rev = jnp.concatenate([fs[:, CONV_W - 1::-1], jnp.zeros((2, 40 - CONV_W, D), F32)], axis=1)
    return dict(conv=fs, conv_rev=rev, b_pw1=b_pw1, b_pw2=fs[:, 34:35])


def run_step(x, target, P, ag, rs):
    T = x.shape[0]
    zero = jnp.zeros((1, 1, D), F32)
    nm, nf = P["norm_mix"], P["norm_ffn"]
    ag.forward("a0", [ag.token])
    W = ag.get("a0", [])
    sm = _conv_small(W["small"])
    h = x
    saved = []
    for l in range(2):
        xn, u, a = norm_mm_glu(h, nm, l, _slots(W[f"pw1_{l}"]), sm["b_pw1"], f"f_pw1_{l}")
        y, s = dwconv_ln_silu(a, sm["conv"], l, f"f_conv_{l}")
        b2 = sm["b_pw2"]
        if l == 0:
            b2 = _gate(b2, ag.forward("f0", [s]))
        h1 = mm_bias_res(s, _rows(W[f"pw2_{l}"]), b2, l, h, f"f_pw2_{l}")
        if l == 0:
            W.update(ag.get("f0", [h1]))
        xn2, gu, f = norm_mm_swiglu(h1, nf, l, _slots(W[f"up_{l}"]), f"f_up_{l}")
        nxt = "l1" if l == 0 else "l2"
        h2 = mm_bias_res(f, _rows(W[f"down_{l}"]), _gate(zero, ag.forward(nxt, [f])), 0, h1, f"f_down_{l}")
        W.update(ag.get(nxt, [h2]))
        saved.append(dict(h=h, xn=xn, u=u, a=a, y=y, s=s, h1=h1, xn2=xn2, gu=gu, f=f))
        h = h2
    h_kv = h
    kvn, kv = norm_mm(h, P["norm_kv"], 0, _rows(W["kv"]), "f_kv")
    kp = jnp.pad(_heads_major(kv[:, :N_KV * HD], N_KV), ((0, 0), (BLK, 0), (0, 0)))
    vp = jnp.pad(_heads_major(kv[:, N_KV * HD:], N_KV), ((0, 0), (BLK, 0), (0, 0)))
    kvt = jnp.pad(kv.T.reshape(2, N_KV, HD, T), ((0, 0), (0, 0), (0, 0), (BLK, 0)))
    kt, vt = kvt[0], kvt[1]
    bucket = _bucket_table()
    onehot = jnp.asarray(np.eye(N_BUCKETS, dtype=np.float32)[bucket])
    bias = jnp.einsum("qkb,bh->hkq", onehot, P["rel_bias"], precision=lax.Precision.HIGHEST)
    bias = bias.reshape(N_KV, GROUP, 2 * BLK, BLK).transpose(0, 2, 1, 3).reshape(1, N_KV, 2 * BLK, QW)
    bias = bias + jnp.asarray(band_mask())[:, None]
    for j in range(2):
        l = 2 + j
        xn, q = norm_mm(h, nm, l, _rows(W[f"wq_{j}"]), f"f_q_{j}", scale=HD ** -0.5)
        qh = q.T.reshape(N_KV, GROUP, HD, T)
        sink = jnp.broadcast_to(P["sinks"][j].reshape(N_KV, GROUP, 1), (N_KV, GROUP, BLK)).reshape(N_KV, 1, QW)
        oh = attn_fwd(qh, kp, vt, bias, sink, f"f_attn_{j}")
        attn = oh.reshape(N_HEADS * HD, T).T
        h1 = mm_bias_res(attn, _rows(W[f"wo_{j}"]), zero, 0, h, f"f_wo_{j}")
        xn2, gu, f = norm_mm_swiglu(h1, nf, l, _slots(W[f"up_{l}"]), f"f_up_{l}")
        zg = _gate(zero, ag.forward("l3", [f])) if j == 0 else zero
        h2 = mm_bias_res(f, _rows(W[f"down_{l}"]), zg, 0, h1, f"f_down_{l}")
        if j == 0:
            W.update(ag.get("l3", [h2]))
        saved.append(dict(h=h, xn=xn, qh=qh, oh=oh, sink=sink, attn=attn, h1=h1, xn2=xn2, gu=gu, f=f))
        h = h2

    dh, st_final = final_loss(h, P["norm_final"], target, "loss_head")

    S = dict(norm_ffn=[None] * 4, norm_mix=[None] * 4, conv=[None] * 2, taps=[None] * 2, b_pw1=[None] * 2,
             b_pw2=[None] * 2, sinks=[None] * 2)

    def ffn_bwd(dh, sv, l, nf, after=()):
        du = mmT_swiglu_bwd(dh, _rows(W[f"down_{l}"]), sv["gu"], f"b_down_{l}", after)
        gd = mm_dw(sv["f"], dh, f"w_down_{l}", 512, 1)
        gu = mm_dw(sv["xn2"], du, f"w_up_{l}", DFF // 2, 4)
        dh, dg = mmT_rmsbwd(du, _slots(W[f"up_{l}"]), sv["h1"], nf, l, dh, f"b_up_{l}")
        S["norm_ffn"][l] = dg[0]
        return dh, {f"down_{l}": _gview(gd), f"up_{l}": _gview(gu)}

    dk = dv = dbias = None
    sent = []
    for j in (1, 0):
        l = 2 + j
        sv = saved[l]
        dh, grads = ffn_bwd(dh, sv, l, nf, sent)
        dattn = mmT(dh, _rows(W[f"wo_{j}"]), f"b_wo_{j}")
        grads[f"wo_{j}"] = _gview(mm_dw(sv["attn"], dh, f"w_wo_{j}", 512, 1))
        doh = dattn.T.reshape(N_KV, GROUP, HD, T)
        dqh, dkj, dvj, dbj, dsj = attn_bwd(sv["qh"], kp, kt, vp, bias, sv["sink"], sv["oh"], doh, f"b_attn_{j}")
        dq = dqh.reshape(N_HEADS * HD, T).T
        grads[f"wq_{j}"] = _gview(mm_dw(sv["xn"], dq, f"w_q_{j}", 512, 1))
        dh, dg = mmT_rmsbwd(dq, _rows(W[f"wq_{j}"])[None], sv["h"], nm, l, dh, f"b_q_{j}")
        S["norm_mix"][l] = dg[0]
        S["sinks"][j] = jnp.sum(dsj.reshape(N_HEADS, BLK), axis=1)
        dk = dkj if dk is None else dk + dkj
        dv = dvj if dv is None else dv + dvj
        dbias = dbj if dbias is None else dbias + dbj
        if j == 1:
            sent = [rs.send("l3", grads)]

    dkv = jnp.concatenate([_heads_minor(dk[:, BLK:]), _heads_minor(dv[:, BLK:])], axis=1).astype(BF16)
    grads["kv"] = _gview(mm_dw(kvn, dkv, "w_kv", 512, 1))
    dh, dg = mmT_rmsbwd(dkv, _rows(W["kv"])[None], h_kv, P["norm_kv"], 0, dh, "b_kv")
    S["norm_kv"] = dg[0]
    dbh = dbias.reshape(N_KV, 2 * BLK, GROUP, BLK)
    S["rel_bias"] = jnp.einsum("vkgq,qkb->bvg", dbh, onehot, precision=lax.Precision.HIGHEST).reshape(N_BUCKETS, N_HEADS)
    sent = [rs.send("l2", grads)]
    nf = _gate(nf, rs.reduce("l3", [dh]))

    for l in (1, 0):
        sv = saved[l]
        dh, grads = ffn_bwd(dh, sv, l, nf, sent)
        conv = sm["conv"]
        if l == 0:
            conv = _gate(conv, rs.send("f0", grads))
            grads = {}
        dy, st = mmT_lnbwd(dh, _rows(W[f"pw2_{l}"]), sv["y"], conv, l, f"b_pw2_{l}")
        g2, S["b_pw2"][l] = mm_dw(sv["s"], dh, f"w_pw2_{l}", 512, 1, colsum=True)
        du, dtaps = dwconv_glu_bwd(dy, sv["a"], sv["u"], sm["conv"], sm["conv_rev"], l, f"b_conv_{l}")
        S["conv"][l] = st[0:3]
        S["taps"][l] = dtaps[0:CONV_W]
        if l == 0:
            rs.finish("l2", [du])
            nm = _gate(nm, rs.reduce("l1", [du]))
        g1, S["b_pw1"][l] = mm_dw(sv["xn"], du, f"w_pw1_{l}", 512, 4, colsum=True)
        grads[f"pw2_{l}"], grads[f"pw1_{l}"] = _gview(g2), _gview(g1)
        dh, dg = mmT_rmsbwd(du, _slots(W[f"pw1_{l}"]), sv["h"], nm, l, dh, f"b_pw1_{l}")
        S["norm_mix"][l] = dg[0]
        if l == 1:
            sent = [rs.send("l1", grads)]
            rs.finish("l3", [dh])
            nf = _gate(nf, rs.reduce("l2", [dh]))
    S["norm_final"] = st_final[0]
    S["loss"] = st_final[1]
    return grads, dh, S


R_CONV = 37
R_SMALL = 88


def _pack_small(S):
    rows = []
    for l in range(2):
        rows += [S["taps"][l], S["conv"][l][2:3], S["conv"][l][0:2], S["b_pw2"][l], S["b_pw1"][l].reshape(2, D)]
    rows += [jnp.stack(S["norm_mix"]), jnp.stack(S["norm_ffn"]), S["norm_kv"][None], S["norm_final"][None]]
    tail = jnp.concatenate([jnp.stack(S["sinks"]).reshape(-1), S["rel_bias"].reshape(-1)])
    rows += [jnp.pad(tail, (0, D - tail.shape[0]))[None], S["loss"][None]]
    v = jnp.concatenate(rows, axis=0)
    return jnp.pad(v, ((0, R_SMALL - v.shape[0]), (0, 0)))


def kernel(x, norm_mix, norm_ffn, conv_w_pw1, conv_b_pw1, conv_w_dw, conv_b_dw, conv_ln_g, conv_ln_b, conv_w_pw2, conv_b_pw2, norm_kv, w_kv, w_q, w_o, sinks, rel_bias, ffn_w_up, ffn_w_down, norm_final, loss_target, m_norm_mix, m_norm_ffn, m_conv_w_pw1, m_conv_b_pw1, m_conv_w_dw, m_conv_b_dw, m_conv_ln_g, m_conv_ln_b, m_conv_w_pw2, m_conv_b_pw2, m_norm_kv, m_w_kv, m_w_q, m_w_o, m_sinks, m_rel_bias, m_ffn_w_up, m_ffn_w_down, m_norm_final, v_norm_mix, v_norm_ffn, v_conv_w_pw1, v_conv_b_pw1, v_conv_w_dw, v_conv_b_dw, v_conv_ln_g, v_conv_ln_b, v_conv_w_pw2, v_conv_b_pw2, v_norm_kv, v_w_kv, v_w_q, v_w_o, v_sinks, v_rel_bias, v_ffn_w_up, v_ffn_w_down, v_norm_final):
    me = 2 * lax.axis_index("x") + lax.axis_index("y")
    weights = dict(norm_mix=norm_mix, norm_ffn=norm_ffn, conv_w_pw1=conv_w_pw1, conv_b_pw1=conv_b_pw1,
                   conv_w_dw=conv_w_dw, conv_b_dw=conv_b_dw, conv_ln_g=conv_ln_g, conv_ln_b=conv_ln_b,
                   conv_w_pw2=conv_w_pw2, conv_b_pw2=conv_b_pw2, norm_kv=norm_kv, w_kv=w_kv, w_q=w_q, w_o=w_o,
                   sinks=sinks, rel_bias=rel_bias, ffn_w_up=ffn_w_up, ffn_w_down=ffn_w_down, norm_final=norm_final)
    mom_m = dict(norm_mix=m_norm_mix, norm_ffn=m_norm_ffn, conv_w_pw1=m_conv_w_pw1, conv_b_pw1=m_conv_b_pw1,
                 conv_w_dw=m_conv_w_dw, conv_b_dw=m_conv_b_dw, conv_ln_g=m_conv_ln_g, conv_ln_b=m_conv_ln_b,
                 conv_w_pw2=m_conv_w_pw2, conv_b_pw2=m_conv_b_pw2, norm_kv=m_norm_kv, w_kv=m_w_kv, w_q=m_w_q,
                 w_o=m_w_o, sinks=m_sinks, rel_bias=m_rel_bias, ffn_w_up=m_ffn_w_up, ffn_w_down=m_ffn_w_down,
                 norm_final=m_norm_final)
    mom_v = dict(norm_mix=v_norm_mix, norm_ffn=v_norm_ffn, conv_w_pw1=v_conv_w_pw1, conv_b_pw1=v_conv_b_pw1,
                 conv_w_dw=v_conv_w_dw, conv_b_dw=v_conv_b_dw, conv_ln_g=v_conv_ln_g, conv_ln_b=v_conv_ln_b,
                 conv_w_pw2=v_conv_w_pw2, conv_b_pw2=v_conv_b_pw2, norm_kv=v_norm_kv, w_kv=v_w_kv, w_q=v_w_q,
                 w_o=v_w_o, sinks=v_sinks, rel_bias=v_rel_bias, ffn_w_up=v_ffn_w_up, ffn_w_down=v_ffn_w_down,
                 norm_final=v_norm_final)

    big = {"conv_w_pw1": "pw1", "conv_w_pw2": "pw2", "w_q": "wq", "w_o": "wo", "ffn_w_up": "up",
           "ffn_w_down": "down", "w_kv": "kv"}
    of_kind = {k: n for n, k in big.items()}

    def shard(name, token):
        if name == "small":
            a = jnp.concatenate(
                [conv_w_dw, conv_b_dw[:, None], conv_ln_g[:, None], conv_ln_b[:, None], conv_b_pw2[:, None],
                 conv_b_pw1.reshape(2, 2, 256), jnp.zeros((2, 3, 256), F32)], axis=1)
            return a if token is None else _gate(a, token)
        kind, _, l = name.partition("_")
        a = weights[of_kind[kind]]
        a = a[int(l)] if l else a
        if token is not None:
            a = _gate(a, token)
        return a.astype(BF16).reshape(2, a.shape[0] // 2, a.shape[1])

    ag = WeightGather(shard, AG_GROUPS)
    rs = GradReduce({"pw1": (2, 512, 512), "pw2": (2, 128, D), "wq": (2, 128, D), "wo": (2, 128, D),
                     "up": (4, 512, DFF // 2), "down": (4, DFF // 8, D), "kv": (1, 128, 512)})

    P = dict(norm_mix=norm_mix[:, None], norm_ffn=norm_ffn[:, None], norm_kv=norm_kv[None, None],
             norm_final=norm_final[None], sinks=sinks, rel_bias=rel_bias)
    last, grad_x, S = run_step(x[0], loss_target[0], P, ag, rs)

    rs.finish("l1", [grad_x])
    vsum = allreduce_small(_gate(_pack_small(S), rs.reduce("f0", [grad_x])))
    token = rs.send("c0", last, after=[vsum])
    col = lambda a: lax.dynamic_slice_in_dim(a, me * 256, 256, axis=-1)
    grads = {}
    for l in range(2):
        base = l * R_CONV
        grads.setdefault("conv_w_dw", []).append(col(vsum[base:base + 31]))
        grads.setdefault("conv_b_dw", []).append(col(vsum[base + 31]))
        grads.setdefault("conv_ln_g", []).append(col(vsum[base + 32]))
        grads.setdefault("conv_ln_b", []).append(col(vsum[base + 33]))
        grads.setdefault("conv_b_pw2", []).append(col(vsum[base + 34]))
        grads.setdefault("conv_b_pw1", []).append(
            lax.dynamic_slice_in_dim(vsum[base + 35:base + 37].reshape(2 * D), me * 512, 512, axis=0))
    grads = {k: jnp.stack(v) for k, v in grads.items()}
    base = 2 * R_CONV
    grads["norm_mix"] = vsum[base:base + 4]
    grads["norm_ffn"] = vsum[base + 4:base + 8]
    grads["norm_kv"] = vsum[base + 8]
    grads["norm_final"] = vsum[base + 9]
    grads["sinks"] = vsum[base + 10, 0:32].reshape(2, 16)
    grads["rel_bias"] = vsum[base + 10, 32:32 + 512].reshape(32, 16)
    loss = vsum[base + 11, 0]

    delta, new_m, new_v = {}, {}, {}
    rest = [n for n in weights if n not in big]

    def pack(dct):
        flat = jnp.concatenate([dct[n].reshape(-1) for n in rest])
        return jnp.pad(flat, (0, (-flat.shape[0]) % (8 * 128))).reshape(-1, 128)

    d, nm, nv = adamw(pack(weights), _gate(pack(grads), token), pack(mom_m), pack(mom_v), "adamw_small")
    off = 0
    for n in rest:
        shp = weights[n].shape
        sz = int(np.prod(shp))
        delta[n] = d.reshape(-1)[off:off + sz].reshape(shp)
        new_m[n] = nm.reshape(-1)[off:off + sz].reshape(shp)
        new_v[n] = nv.reshape(-1)[off:off + sz].reshape(shp)
        off += sz

    def update(n):
        shp = weights[n].shape
        r2 = (int(np.prod(shp[:-1])), shp[-1])
        g, d, nm, nv = adamw(weights[n].reshape(r2), rs.J[big[n]].reshape(r2), mom_m[n].reshape(r2),
                             mom_v[n].reshape(r2), f"adamw_{n}", copy_g=True)
        grads[n], delta[n], new_m[n], new_v[n] = g.reshape(shp), d.reshape(shp), nm.reshape(shp), nv.reshape(shp)

    rs.finish("f0", [vsum])
    for n in ("ffn_w_up", "ffn_w_down"):
        update(n)
    rs.reduce("c0", [delta["ffn_w_up"], delta["ffn_w_down"]])
    for n in ("w_q", "w_o", "w_kv"):
        update(n)
    rs.finish("c0", [delta["w_kv"]])
    for n in ("conv_w_pw1", "conv_w_pw2"):
        update(n)

    order = list(weights)
    return (loss, grad_x[None], *[grads[n] for n in order], *[delta[n] for n in order],
            *[new_m[n] for n in order], *[new_v[n] for n in order])
```

```python
import functools
import math

import numpy as np
import jax
import jax.numpy as jnp
from jax import lax
from jax.experimental import pallas as pl
from jax.experimental.pallas import tpu as pltpu

F32 = jnp.float32
BF16 = jnp.bfloat16
MESH = pl.DeviceIdType.MESH

D = 1024
DFF = 2816
N_HEADS = 16
N_KV = 4
GROUP = 4
HD = 64
BLK = 128
CONV_W = 31
HALO = 32
N_BUCKETS = 32
MAX_DISTANCE = 128
EPS = 1e-6
NEG_INF = -1e30
TM = 512
TCV = 256
VMEM_LIMIT = 56 * 2 ** 20

ADAM_LR, ADAM_B1, ADAM_B2, ADAM_EPS, ADAM_WD, ADAM_STEP = 0.001, 0.9, 0.999, 1e-08, 0.01, 10


def _cp(*sem):
    return pltpu.CompilerParams(dimension_semantics=sem, vmem_limit_bytes=VMEM_LIMIT)


def _sigmoid(x):
    return 1.0 / (1.0 + jnp.exp(-x))


def _row(tm, n):
    return pl.BlockSpec((tm, n), lambda i: (i, 0))


def _const(shape):
    nd = len(shape)
    return pl.BlockSpec(shape, lambda i: (0,) * nd)


def _weight(shape):
    nd = len(shape)
    return pl.BlockSpec(shape, lambda i: (0,) * nd, pipeline_mode=pl.Buffered(1))


def _layer(shape, l):
    nd = len(shape)
    return pl.BlockSpec((None,) + tuple(shape), lambda i: (l,) + (0,) * nd)


def _dot(a, b):
    return jnp.dot(a, b, preferred_element_type=F32)


def _dot_nt(a, b):
    return lax.dot_general(a, b, (((1,), (1,)), ((), ())), preferred_element_type=F32)


def _dot_tn(a, b):
    return lax.dot_general(a, b, (((0,), (0,)), ((), ())), preferred_element_type=F32)


def _rms(x):
    return lax.rsqrt(jnp.mean(x * x, axis=-1, keepdims=True) + EPS)


def norm_mm_glu(h, g, l, w, b, name):
    T = h.shape[0]
    ns = w.shape[-1]

    def body(h_ref, g_ref, w_ref, b_ref, xn_ref, u_ref, a_ref):
        x = h_ref[...]
        xn = (x * _rms(x) * g_ref[...]).astype(BF16)
        xn_ref[...] = xn
        for s in range(2):
            lo, hi = s * ns, (s + 1) * ns
            u1 = _dot(xn, w_ref[s]) + b_ref[:, lo:hi]
            u2 = _dot(xn, w_ref[2 + s]) + b_ref[:, D + lo:D + hi]
            u_ref[:, lo:hi] = u1.astype(BF16)
            u_ref[:, D + lo:D + hi] = u2.astype(BF16)
            a_ref[:, lo:hi] = (u1 * _sigmoid(u2)).astype(BF16)

    return pl.pallas_call(
        body, name=name, grid=(T // TM,),
        in_specs=[_row(TM, D), _layer((1, D), l), _weight((4, D, ns)), _layer((1, 2 * D), l)],
        out_specs=[_row(TM, D), _row(TM, 2 * D), _row(TM, D)],
        out_shape=[jax.ShapeDtypeStruct((T, D), BF16), jax.ShapeDtypeStruct((T, 2 * D), BF16),
                   jax.ShapeDtypeStruct((T, D), BF16)],
        compiler_params=_cp("parallel"),
    )(h, g, w, b)


SUB = 8


def _make_shifts(sh):
    n = TCV + HALO - SUB
    for r in range(1, SUB):
        for r0 in range(0, n, 40):
            sh[r, r0:r0 + 40, :] = sh[0, pl.ds(r + r0, 40), :]


def _shifted(sh, off, rows, cols):
    return sh[off % SUB, pl.ds(off - off % SUB, rows), cols]


def _conv_taps(sh, w_ref, out_ref, first):
    RB, LB = 32, 512
    for r0 in range(0, TCV, RB):
        for c0 in range(0, out_ref.shape[1], LB):
            acc = jnp.zeros((RB, LB), F32)
            for k in range(CONV_W):
                acc = acc + w_ref[k:k + 1, c0:c0 + LB] * _shifted(sh, first + k + r0, RB, slice(c0, c0 + LB))
            out_ref[r0:r0 + RB, c0:c0 + LB] = acc


def dwconv_ln_silu(a, sm, l, name):
    T = a.shape[0]
    nb = TCV // HALO

    def body(cur_ref, prev_ref, sm_ref, y_ref, s_ref, sh, yb):
        i = pl.program_id(0)
        sh[0, 0:HALO, :] = jnp.where(i > 0, prev_ref[...].astype(F32), 0.0)
        sh[0, HALO:HALO + TCV, :] = cur_ref[...].astype(F32)
        _make_shifts(sh)
        _conv_taps(sh, sm_ref, yb, HALO - (CONV_W - 1))
        y = yb[...] + sm_ref[31:32, :]
        y_ref[...] = y.astype(BF16)
        mu = jnp.mean(y, axis=-1, keepdims=True)
        yc = y - mu
        rstd = lax.rsqrt(jnp.mean(yc * yc, axis=-1, keepdims=True) + EPS)
        z = yc * rstd * sm_ref[32:33, :] + sm_ref[33:34, :]
        s_ref[...] = (z * _sigmoid(z)).astype(BF16)

    return pl.pallas_call(
        body, name=name, grid=(T // TCV,),
        in_specs=[_row(TCV, D), pl.BlockSpec((HALO, D), lambda i: (jnp.maximum(i * nb - 1, 0), 0)),
                  _layer((40, D), l)],
        out_specs=[_row(TCV, D), _row(TCV, D)],
        out_shape=[jax.ShapeDtypeStruct((T, D), BF16), jax.ShapeDtypeStruct((T, D), BF16)],
        scratch_shapes=[pltpu.VMEM((SUB, TCV + HALO, D), F32), pltpu.VMEM((TCV, D), F32)],
        compiler_params=_cp("parallel"),
    )(a, a, sm)


def mm_bias_res(xb, w, b, bl, res, name):
    T, K = xb.shape

    def body(x_ref, w_ref, b_ref, r_ref, o_ref):
        o_ref[...] = _dot(x_ref[...], w_ref[...]) + b_ref[...] + r_ref[...]

    return pl.pallas_call(
        body, name=name, grid=(T // TM,),
        in_specs=[_row(TM, K), _weight((K, D)), _layer((1, D), bl), _row(TM, D)],
        out_specs=_row(TM, D), out_shape=jax.ShapeDtypeStruct((T, D), F32),
        compiler_params=_cp("parallel"),
    )(xb, w, b, res)


def norm_mm_swiglu(h, g, l, w, name):
    T = h.shape[0]
    ns = w.shape[-1]

    def body(h_ref, g_ref, w_ref, xn_ref, gu_ref, f_ref):
        x = h_ref[...]
        xn = (x * _rms(x) * g_ref[...]).astype(BF16)
        xn_ref[...] = xn
        for s in range(2):
            lo, hi = s * ns, (s + 1) * ns
            gate = _dot(xn, w_ref[s])
            up = _dot(xn, w_ref[2 + s])
            gu_ref[:, lo:hi] = gate.astype(BF16)
            gu_ref[:, DFF + lo:DFF + hi] = up.astype(BF16)
            f_ref[:, lo:hi] = (gate * _sigmoid(gate) * up).astype(BF16)

    return pl.pallas_call(
        body, name=name, grid=(T // TM,),
        in_specs=[_row(TM, D), _layer((1, D), l), _weight((4, D, ns))],
        out_specs=[_row(TM, D), _row(TM, 2 * DFF), _row(TM, DFF)],
        out_shape=[jax.ShapeDtypeStruct((T, D), BF16), jax.ShapeDtypeStruct((T, 2 * DFF), BF16),
                   jax.ShapeDtypeStruct((T, DFF), BF16)],
        compiler_params=_cp("parallel"),
    )(h, g, w)


def norm_mm(h, g, gl, w, name, scale=1.0):
    T = h.shape[0]
    N = w.shape[-1]

    def body(h_ref, g_ref, w_ref, xn_ref, o_ref):
        x = h_ref[...]
        xn = (x * _rms(x) * g_ref[...]).astype(BF16)
        xn_ref[...] = xn
        o_ref[...] = (_dot(xn, w_ref[...]) * scale).astype(BF16)

    return pl.pallas_call(
        body, name=name, grid=(T // TM,),
        in_specs=[_row(TM, D), _layer((1, D), gl), _weight((D, N))],
        out_specs=[_row(TM, D), _row(TM, N)],
        out_shape=[jax.ShapeDtypeStruct((T, D), BF16), jax.ShapeDtypeStruct((T, N), BF16)],
        compiler_params=_cp("parallel"),
    )(h, g, w)


QB = 4
QW = GROUP * BLK


def band_mask():
    qi = np.arange(QW)[None, :] % BLK
    kj = np.arange(2 * BLK)[:, None]
    band = ((kj < BLK) & (kj > qi)) | ((kj >= BLK) & (kj - BLK <= qi))
    first = band & (kj >= BLK)
    return np.where(np.stack([first, band]), 0.0, NEG_INF).astype(np.float32)


def _softmax_cols(s, sink):
    m = jnp.maximum(jnp.max(s, axis=0, keepdims=True), sink)
    p = jnp.exp(s - m)
    es = jnp.exp(sink - m)
    inv = 1.0 / (jnp.sum(p, axis=0, keepdims=True) + es)
    return p, inv, es


def _attn_specs(T):
    W = QB * BLK
    qspec = pl.BlockSpec((None, GROUP, HD, W), lambda kv, n: (kv, 0, 0, n))
    kspec = pl.BlockSpec((None, T + BLK, HD), lambda kv, n: (kv, 0, 0))
    ktspec = [pl.BlockSpec((None, HD, W), lambda kv, n: (kv, 0, n)),
              pl.BlockSpec((None, HD, BLK), lambda kv, n: (kv, 0, (n + 1) * QB))]
    bspec = pl.BlockSpec((2, None, 2 * BLK, QW), lambda kv, n: (0, kv, 0, 0))
    sspec = pl.BlockSpec((None, 1, QW), lambda kv, n: (kv, 0, 0))
    return qspec, kspec, ktspec, bspec, sspec


def _attn_block(n, b):
    blk = n * QB + b
    rows = pl.ds(pl.multiple_of(blk * BLK, BLK), 2 * BLK)
    return rows, (jnp.minimum(blk, 1) if b == 0 else 1)


def _band_cols(main_ref, tail_ref, b):
    if b < QB - 1:
        return main_ref[:, b * BLK:(b + 2) * BLK]
    return jnp.concatenate([main_ref[:, b * BLK:], tail_ref[...]], axis=1)


def _heads_side_by_side(ref, qs):
    return jnp.concatenate([ref[g, :, qs] for g in range(GROUP)], axis=1)


def attn_fwd(q, kp, vt, bias, sink, name):
    T = q.shape[3]
    qspec, kspec, ktspec, bspec, sspec = _attn_specs(T)

    def body(q_ref, k_ref, vt_ref, vtt_ref, b_ref, s_ref, o_ref, pb):
        n = pl.program_id(1)
        for b in range(QB):
            rows, table = _attn_block(n, b)
            qs = slice(b * BLK, (b + 1) * BLK)
            st = _dot(k_ref[rows, :], _heads_side_by_side(q_ref, qs))
            for g in range(GROUP):
                hs = slice(g * BLK, (g + 1) * BLK)
                p, inv, _ = _softmax_cols(st[:, hs] + b_ref[table, :, hs], s_ref[:, hs])
                pb[:, hs] = (p * inv).astype(BF16)
            ot = _dot(_band_cols(vt_ref, vtt_ref, b), pb[...])
            for g in range(GROUP):
                o_ref[g, :, qs] = ot[:, g * BLK:(g + 1) * BLK].astype(BF16)

    return pl.pallas_call(
        body, name=name, grid=(N_KV, T // (QB * BLK)),
        in_specs=[qspec, kspec, *ktspec, bspec, sspec], out_specs=qspec,
        out_shape=jax.ShapeDtypeStruct((N_KV, GROUP, HD, T), BF16),
        scratch_shapes=[pltpu.VMEM((2 * BLK, QW), BF16)],
        compiler_params=_cp("parallel", "parallel"),
    )(q, kp, vt, vt, bias, sink)


def attn_bwd(q, kp, kt, vp, bias, sink, o, do, name):
    T = q.shape[3]
    qspec, kspec, ktspec, bspec, sspec = _attn_specs(T)

    def body(q_ref, k_ref, kt_ref, ktt_ref, v_ref, b_ref, s_ref, o_ref, do_ref,
             dq_ref, dk_ref, dv_ref, db_ref, ds_ref, pb, dsb):
        n = pl.program_id(1)

        @pl.when(n == 0)
        def _():
            dk_ref[...] = jnp.zeros_like(dk_ref)
            dv_ref[...] = jnp.zeros_like(dv_ref)
            db_ref[...] = jnp.zeros_like(db_ref)
            ds_ref[...] = jnp.zeros_like(ds_ref)

        for b in range(QB):
            rows, table = _attn_block(n, b)
            qs = slice(b * BLK, (b + 1) * BLK)
            q4 = _heads_side_by_side(q_ref, qs)
            do4 = _heads_side_by_side(do_ref, qs)
            st = _dot(k_ref[rows, :], q4)
            dpt = _dot(v_ref[rows, :], do4)
            for g in range(GROUP):
                hs = slice(g * BLK, (g + 1) * BLK)
                p, inv, es = _softmax_cols(st[:, hs] + b_ref[table, :, hs], s_ref[:, hs])
                probs = p * inv
                delta = jnp.sum(do_ref[g, :, qs].astype(F32) * o_ref[g, :, qs].astype(F32), axis=0, keepdims=True)
                dS = probs * (dpt[:, hs] - delta)
                ds_ref[:, hs] += -(es * inv) * delta
                db_ref[:, hs] += dS
                pb[:, hs] = probs.astype(BF16)
                dsb[:, hs] = dS.astype(BF16)
            dqt = _dot(_band_cols(kt_ref, ktt_ref, b), dsb[...]) * (HD ** -0.5)
            for g in range(GROUP):
                dq_ref[g, :, qs] = dqt[:, g * BLK:(g + 1) * BLK].astype(BF16)
            dk_ref[rows, :] += _dot_nt(dsb[...], q4)
            dv_ref[rows, :] += _dot_nt(pb[...], do4)

    kout = pl.BlockSpec((None, T + BLK, HD), lambda kv, n: (kv, 0, 0))
    dbspec = pl.BlockSpec((None, 2 * BLK, QW), lambda kv, n: (kv, 0, 0))
    return pl.pallas_call(
        body, name=name, grid=(N_KV, T // (QB * BLK)),
        in_specs=[qspec, kspec, *ktspec, kspec, bspec, sspec, qspec, qspec],
        out_specs=[qspec, kout, kout, dbspec, sspec],
        out_shape=[jax.ShapeDtypeStruct((N_KV, GROUP, HD, T), BF16),
                   jax.ShapeDtypeStruct((N_KV, T + BLK, HD), F32), jax.ShapeDtypeStruct((N_KV, T + BLK, HD), F32),
                   jax.ShapeDtypeStruct((N_KV, 2 * BLK, QW), F32), jax.ShapeDtypeStruct((N_KV, 1, QW), F32)],
        scratch_shapes=[pltpu.VMEM((2 * BLK, QW), BF16), pltpu.VMEM((2 * BLK, QW), BF16)],
        compiler_params=_cp("parallel", "arbitrary"),
    )(q, kp, kt, kt, vp, bias, sink, o, do)


def final_loss(h, g, target, name):
    T = h.shape[0]

    def body(h_ref, g_ref, t_ref, dh_ref, st_ref):
        i = pl.program_id(0)

        @pl.when(i == 0)
        def _():
            st_ref[...] = jnp.zeros_like(st_ref)

        x = h_ref[...]
        r = _rms(x)
        xh = x * r
        e = xh * g_ref[...] - t_ref[...]
        loss = 0.5 * jnp.sum(jnp.mean(e * e, axis=-1, keepdims=True))
        dy = e * (1.0 / D)
        st_ref[0:1, :] += jnp.sum(dy * xh, axis=0, keepdims=True)
        lane = lax.broadcasted_iota(jnp.int32, (1, D), 1)
        st_ref[1:2, :] += jnp.where(lane == 0, loss, 0.0)
        dxh = dy * g_ref[...]
        dh_ref[...] = r * (dxh - xh * jnp.mean(dxh * xh, axis=-1, keepdims=True))

    return pl.pallas_call(
        body, name=name, grid=(T // TM,),
        in_specs=[_row(TM, D), _const((1, D)), _row(TM, D)],
        out_specs=[_row(TM, D), _const((8, D))],
        out_shape=[jax.ShapeDtypeStruct((T, D), F32), jax.ShapeDtypeStruct((8, D), F32)],
        compiler_params=_cp("arbitrary"),
    )(h, g, target)


def mm_dw(x, dy, name, tn, slots, colsum=False):
    T, K = x.shape
    split = dy.ndim == 3
    N = dy.shape[-1] * (2 if split else 1)
    tt = min(T, 2048 if K <= 1024 else 1024)
    nt = T // tt
    ns = N // slots
    per = ns // tn

    def body(x_ref, dy_ref, *rest):
        if colsum:
            dw_ref, cs_ref, acc, cacc = rest
        else:
            dw_ref, acc = rest
        t = pl.program_id(1)

        @pl.when(t == 0)
        def _():
            acc[...] = jnp.zeros_like(acc)
            if colsum:
                cacc[...] = jnp.zeros_like(cacc)

        dyv = dy_ref[...]
        acc[...] += _dot_tn(x_ref[...].astype(BF16), dyv.astype(BF16))
        if colsum:
            cacc[...] += jnp.sum(dyv.astype(F32), axis=0, keepdims=True)

        @pl.when(t == nt - 1)
        def _():
            dw_ref[...] = acc[...].astype(BF16)
            if colsum:
                cs_ref[...] = cacc[...]

    if split:
        half = N // 2 // tn
        dy_spec = pl.BlockSpec((None, tt, tn), lambda j, t: (j // half, t, j % half))
    else:
        dy_spec = pl.BlockSpec((tt, tn), lambda j, t: (t, j))
    out_specs = [pl.BlockSpec((None, K, tn), lambda j, t: (j // per, 0, j % per))]
    out_shape = [jax.ShapeDtypeStruct((slots, K, ns), BF16)]
    scratch = [pltpu.VMEM((K, tn), F32)]
    if colsum:
        out_specs.append(pl.BlockSpec((1, tn), lambda j, t: (0, j)))
        out_shape.append(jax.ShapeDtypeStruct((1, N), F32))
        scratch.append(pltpu.VMEM((1, tn), F32))
    res = pl.pallas_call(
        body, name=name, grid=(N // tn, nt),
        in_specs=[pl.BlockSpec((tt, K), lambda j, t: (t, 0)), dy_spec],
        out_specs=out_specs, out_shape=out_shape, scratch_shapes=scratch,
        compiler_params=_cp("parallel", "arbitrary"),
    )(x, dy)
    return tuple(res) if colsum else res[0]


def mmT_swiglu_bwd(dh, w, gu, name, after=()):
    T = dh.shape[0]
    cw = 256

    def body(dh_ref, w_ref, gu_ref, *rest):
        du_ref = rest[-1]
        dhb = dh_ref[...].astype(BF16)
        for lo in range(0, DFF, cw):
            hi = lo + cw
            df = _dot_nt(dhb, w_ref[lo:hi, :])
            gate = gu_ref[:, lo:hi].astype(F32)
            up = gu_ref[:, DFF + lo:DFF + hi].astype(F32)
            sg = _sigmoid(gate)
            silu = gate * sg
            du_ref[:, lo:hi] = (df * (up * (sg + silu * (1.0 - sg)))).astype(BF16)
            du_ref[:, DFF + lo:DFF + hi] = (df * silu).astype(BF16)

    return pl.pallas_call(
        body, name=name, grid=(T // TM,),
        in_specs=[_row(TM, D), _weight((DFF, D)), _row(TM, 2 * DFF)] + [ANY] * len(after),
        out_specs=_row(TM, 2 * DFF), out_shape=jax.ShapeDtypeStruct((T, 2 * DFF), BF16),
        compiler_params=_cp("parallel"),
    )(dh, w, gu, *after)


def mmT_rmsbwd(du, w, h, g, gl, dh_in, name):
    split = du.ndim == 3
    T = du.shape[-2]
    N = du.shape[-1] * (2 if split else 1)
    slots = w.shape[0]
    ns = N // slots

    def piece(du_ref, s):
        if split:
            per = slots // 2
            return du_ref[s // per, :, (s % per) * ns:(s % per + 1) * ns]
        return du_ref[:, s * ns:(s + 1) * ns]

    def body(du_ref, w_ref, h_ref, g_ref, di_ref, dh_ref, dg_ref):
        i = pl.program_id(0)

        @pl.when(i == 0)
        def _():
            dg_ref[...] = jnp.zeros_like(dg_ref)

        dxn = _dot_nt(piece(du_ref, 0), w_ref[0])
        for s in range(1, slots):
            dxn = dxn + _dot_nt(piece(du_ref, s), w_ref[s])
        x = h_ref[...]
        r = _rms(x)
        xh = x * r
        dg_ref[0:1, :] += jnp.sum(dxn * xh, axis=0, keepdims=True)
        dxh = dxn * g_ref[...]
        dh_ref[...] = di_ref[...] + r * (dxh - xh * jnp.mean(dxh * xh, axis=-1, keepdims=True))

    return pl.pallas_call(
        body, name=name, grid=(T // TM,),
        in_specs=[pl.BlockSpec((2, TM, N // 2), lambda i: (0, i, 0)) if split else _row(TM, N),
                  _weight((slots, D, ns)), _row(TM, D), _layer((1, D), gl), _row(TM, D)],
        out_specs=[_row(TM, D), _const((8, D))],
        out_shape=[jax.ShapeDtypeStruct((T, D), F32), jax.ShapeDtypeStruct((8, D), F32)],
        compiler_params=_cp("arbitrary"),
    )(du, w, h, g, dh_in)


def mmT(dh, w, name):
    T = dh.shape[0]
    N = w.shape[0]

    def body(dh_ref, w_ref, o_ref):
        o_ref[...] = _dot_nt(dh_ref[...].astype(BF16), w_ref[...]).astype(BF16)

    return pl.pallas_call(
        body, name=name, grid=(T // TM,),
        in_specs=[_row(TM, D), _weight((N, D))],
        out_specs=_row(TM, N), out_shape=jax.ShapeDtypeStruct((T, N), BF16),
        compiler_params=_cp("parallel"),
    )(dh, w)


def mmT_lnbwd(dh, w, y, sm, l, name):
    T = dh.shape[0]

    def body(dh_ref, w_ref, y_ref, sm_ref, dy_ref, st_ref):
        i = pl.program_id(0)

        @pl.when(i == 0)
        def _():
            st_ref[...] = jnp.zeros_like(st_ref)

        ds = _dot_nt(dh_ref[...].astype(BF16), w_ref[...])
        y = y_ref[...].astype(F32)
        mu = jnp.mean(y, axis=-1, keepdims=True)
        yc = y - mu
        rstd = lax.rsqrt(jnp.mean(yc * yc, axis=-1, keepdims=True) + EPS)
        xh = yc * rstd
        gam = sm_ref[32:33, :]
        z = xh * gam + sm_ref[33:34, :]
        sg = _sigmoid(z)
        dz = ds * sg * (1.0 + z * (1.0 - sg))
        st_ref[0:1, :] += jnp.sum(dz * xh, axis=0, keepdims=True)
        st_ref[1:2, :] += jnp.sum(dz, axis=0, keepdims=True)
        dxh = dz * gam
        dy = rstd * (dxh - jnp.mean(dxh, axis=-1, keepdims=True) - xh * jnp.mean(dxh * xh, axis=-1, keepdims=True))
        st_ref[2:3, :] += jnp.sum(dy, axis=0, keepdims=True)
        dy_ref[...] = dy.astype(BF16)

    return pl.pallas_call(
        body, name=name, grid=(T // TM,),
        in_specs=[_row(TM, D), _weight((D, D)), _row(TM, D), _layer((40, D), l)],
        out_specs=[_row(TM, D), _const((8, D))],
        out_shape=[jax.ShapeDtypeStruct((T, D), BF16), jax.ShapeDtypeStruct((8, D), F32)],
        compiler_params=_cp("arbitrary"),
    )(dh, w, y, sm)


CH = 512


def dwconv_glu_bwd(dy, a, u, sm, smrev, l, name):
    T = dy.shape[0]
    nr, nc = T // TCV, D // CH
    nb = TCV // HALO
    last = T // HALO - 1

    def body(dy_ref, dyn_ref, a_ref, ap_ref, u1_ref, u2_ref, sm_ref, rev_ref, du_ref, dw_ref, shd, sha, da):
        i = pl.program_id(0)
        r = i % nr

        @pl.when(r == 0)
        def _():
            dw_ref[...] = jnp.zeros_like(dw_ref)

        shd[0, 0:TCV, :] = dy_ref[...].astype(F32)
        shd[0, TCV:TCV + HALO, :] = jnp.where(r < nr - 1, dyn_ref[...].astype(F32), 0.0)
        sha[0, 0:HALO, :] = jnp.where(r > 0, ap_ref[...].astype(F32), 0.0)
        sha[0, HALO:HALO + TCV, :] = a_ref[...].astype(F32)
        _make_shifts(shd)
        _make_shifts(sha)
        _conv_taps(shd, rev_ref, da, 0)
        for k in range(CONV_W):
            part = jnp.zeros((SUB, CH), F32)
            for r0 in range(0, TCV, SUB):
                part = part + shd[0, r0:r0 + SUB, :] * _shifted(sha, HALO - (CONV_W - 1) + k + r0, SUB, slice(None))
            dw_ref[k:k + 1, :] += jnp.sum(part, axis=0, keepdims=True)
        dav = da[...]
        u1 = u1_ref[...].astype(F32)
        sg = _sigmoid(u2_ref[...].astype(F32))
        du_ref[0] = (dav * sg).astype(BF16)
        du_ref[1] = (dav * u1 * sg * (1.0 - sg)).astype(BF16)

    tile = lambda i: (i % nr, i // nr)
    in_specs = [pl.BlockSpec((TCV, CH), tile),
                pl.BlockSpec((HALO, CH), lambda i: (jnp.minimum((i % nr + 1) * nb, last), i // nr)),
                pl.BlockSpec((TCV, CH), tile),
                pl.BlockSpec((HALO, CH), lambda i: (jnp.maximum((i % nr) * nb - 1, 0), i // nr)),
                pl.BlockSpec((TCV, CH), tile), pl.BlockSpec((TCV, CH), lambda i: (i % nr, nc + i // nr)),
                pl.BlockSpec((None, 40, CH), lambda i: (l, 0, i // nr)),
                pl.BlockSpec((None, 40, CH), lambda i: (l, 0, i // nr))]
    return pl.pallas_call(
        body, name=name, grid=(nr * nc,), in_specs=in_specs,
        out_specs=[pl.BlockSpec((2, TCV, CH), lambda i: (0, i % nr, i // nr)),
                   pl.BlockSpec((32, CH), lambda i: (0, i // nr))],
        out_shape=[jax.ShapeDtypeStruct((2, T, D), BF16), jax.ShapeDtypeStruct((32, D), F32)],
        scratch_shapes=[pltpu.VMEM((SUB, TCV + HALO, CH), F32), pltpu.VMEM((SUB, TCV + HALO, CH), F32),
                        pltpu.VMEM((TCV, CH), F32)],
        compiler_params=_cp("arbitrary"),
    )(dy, dy, a, a, u, u, sm, smrev)


def _rows_tile(R):
    for t in (512, 256, 128, 64, 32, 16, 8):
        if R % t == 0:
            return t
    return R


def add8_into(J, l, g, others, where, name):
    R, C = g.shape[2:]
    tr = R // 2

    def body(w_ref, g_ref, x_ref, j_in, j_ref):
        acc = g_ref[...].astype(F32)
        for k in range(7):
            acc = acc + x_ref[k].astype(F32)
        j_ref[...] = acc

    return pl.pallas_call(
        body, name=name,
        grid_spec=pltpu.PrefetchScalarGridSpec(
            num_scalar_prefetch=1, grid=(R // tr,),
            in_specs=[pl.BlockSpec((None, None, tr, C), lambda i, w: (w[0], w[1], i, 0)),
                      pl.BlockSpec((7, tr, C), lambda i, w: (0, i, 0)), ANY],
            out_specs=pl.BlockSpec((None, None, tr, C), lambda i, w: (l, w[1], i, 0))),
        out_shape=jax.ShapeDtypeStruct(J.shape, F32), input_output_aliases={3: 0},
        compiler_params=_cp("parallel"),
    )(where, g, others, J)


def adamw(w, g, m, v, name, copy_g=False):
    R, C = w.shape
    tr = _rows_tile(R)

    def body(w_ref, g_ref, m_ref, v_ref, *outs):
        d_ref, nm_ref, nv_ref = outs[-3:]
        gv = g_ref[...]
        if copy_g:
            outs[0][...] = gv
        nm = ADAM_B1 * m_ref[...] + (1.0 - ADAM_B1) * gv
        nv = ADAM_B2 * v_ref[...] + (1.0 - ADAM_B2) * (gv * gv)
        m_hat = nm / (1.0 - ADAM_B1 ** ADAM_STEP)
        v_hat = nv / (1.0 - ADAM_B2 ** ADAM_STEP)
        d_ref[...] = -ADAM_LR * (m_hat / (jnp.sqrt(v_hat) + ADAM_EPS) + ADAM_WD * w_ref[...])
        nm_ref[...] = nm
        nv_ref[...] = nv

    sd = jax.ShapeDtypeStruct((R, C), F32)
    n_out = 4 if copy_g else 3
    return pl.pallas_call(
        body, name=name, grid=(R // tr,),
        in_specs=[_row(tr, C)] * 4, out_specs=[_row(tr, C)] * n_out, out_shape=[sd] * n_out,
        compiler_params=_cp("parallel"),
    )(w, g, m, v)


ANY = pl.BlockSpec(memory_space=pl.ANY)
HBM = pl.BlockSpec(memory_space=pltpu.HBM)
SEM = pl.BlockSpec(memory_space=pltpu.SEMAPHORE)
EFFECT = pltpu.SideEffectType.DATAFLOW_SIDE_EFFECTING


def _place():
    x, y, c = lax.axis_index("x"), lax.axis_index("y"), lax.axis_index("c")
    chips = [(1 - x, y), (x, 1 - y), (1 - x, 1 - y)]
    return x, y, c, chips


def _copy(src, dst, send, recv, k, to):
    return pltpu.make_async_remote_copy(src_ref=src, dst_ref=dst, send_sem=send.at[k], recv_sem=recv.at[k],
                                        device_id=to, device_id_type=MESH)


def xchg_start(name, bufs, plan, n, after=()):
    nb = len(bufs)

    na = len(after)

    def body(*refs):
        send, recv, token = refs[nb + na], refs[nb + na + 1], refs[-1]
        for k, (src, dst, to) in enumerate(plan(refs[:nb])):
            _copy(src, dst, send, recv, k, to).start()
        token[...] = jnp.zeros_like(token)

    outs = pl.pallas_call(
        body, name=name,
        out_shape=(pltpu.SemaphoreType.DMA((n,)), pltpu.SemaphoreType.DMA((n,)),
                   *[pltpu.HBM(b.shape, b.dtype) for b in bufs], jax.ShapeDtypeStruct((8, 128), F32)),
        in_specs=[HBM] * nb + [ANY] * na,
        out_specs=(SEM, SEM, *[HBM] * nb, pl.BlockSpec(memory_space=pltpu.VMEM)),
        input_output_aliases={i: 2 + i for i in range(nb)},
        compiler_params=pltpu.CompilerParams(has_side_effects=EFFECT),
    )(*[pltpu.with_memory_space_constraint(b, pltpu.HBM) for b in bufs], *after)
    return dict(name=name, send=outs[0], recv=outs[1], bufs=list(outs[2:2 + nb]), plan=plan), outs[-1]


def xchg_wait(flight, after):
    bufs, plan = flight["bufs"], flight["plan"]
    nb = len(bufs)

    def body(*refs):
        send, recv = refs[nb], refs[nb + 1]
        for k, (src, dst, to) in enumerate(plan(refs[:nb])):
            cp = _copy(src, dst, send, recv, k, to)
            cp.wait_send()
            cp.wait_recv()

    outs = pl.pallas_call(
        body, name=flight["name"] + "_wait",
        out_shape=tuple(pltpu.HBM(b.shape, b.dtype) for b in bufs),
        in_specs=[HBM] * nb + [SEM, SEM] + [ANY] * len(after),
        out_specs=tuple([HBM] * nb), input_output_aliases={i: i for i in range(nb)},
        compiler_params=pltpu.CompilerParams(has_side_effects=EFFECT),
    )(*bufs, flight["send"], flight["recv"], *after)
    return list(outs)


def _flip(k, x, y, c):
    return ((1 - x) if k & 4 else x, (1 - y) if k & 2 else y, (1 - c) if k & 1 else c)


class WeightGather:
    def __init__(self, shard, groups):
        me = 2 * lax.axis_index("x") + lax.axis_index("y")
        self.names = dict(groups)
        self.ici, self.d2d = {}, {}
        self.token = None
        for gname, names in groups:
            nt = len(names)
            srcs = [shard(n, self.token) for n in names]
            lands = [lax.dynamic_update_slice(lax.empty((4,) + s.shape, s.dtype), s[None], (me, 0, 0, 0))
                     for s in srcs]

            def plan(refs, nt=nt):
                x, y, c, chips = _place()
                return [(refs[t].at[c], refs[nt + t].at[2 * x + y, c], (cx, cy, c))
                        for t in range(nt) for cx, cy in chips]

            self.ici[gname], self.token = xchg_start(f"ag_ici_{gname}", srcs + lands, plan, 3 * nt,
                                                     after=[] if self.token is None else [self.token])

    def forward(self, gname, after):
        nt = len(self.names[gname])
        lands = xchg_wait(self.ici.pop(gname), after)[nt:]

        def plan(refs):
            x, y, c, chips = _place()
            out = []
            for t in range(nt):
                for cx, cy in chips:
                    piece = refs[t].at[2 * cx + cy, c]
                    out.append((piece, piece, (x, y, 1 - c)))
            return out

        self.d2d[gname], token = xchg_start(f"ag_d2d_{gname}", lands, plan, 3 * nt)
        return token

    def get(self, gname, after):
        lands = xchg_wait(self.d2d.pop(gname), after)
        return dict(zip(self.names[gname], lands))


class GradReduce:
    def __init__(self, kinds):
        self.J = {k: lax.empty((L, 2, a2, b), F32) for k, (L, a2, b) in kinds.items()}
        self.x, self.j = {}, {}

    @staticmethod
    def _where(name):
        kind, _, l = name.partition("_")
        return kind, int(l or 0)

    def send(self, gname, grads, after=()):
        names = list(grads)
        nt = len(names)
        gs = [grads[n] for n in names]
        xs = [lax.empty((7,) + g.shape[2:], g.dtype) for g in gs]

        def plan(refs):
            x, y, c, _ = _place()
            out = []
            for t in range(nt):
                for k in range(1, 8):
                    px, py, pc = _flip(k, x, y, c)
                    out.append((refs[t].at[2 * px + py, pc], refs[nt + t].at[k - 1], (px, py, pc)))
            return out

        flight, token = xchg_start(f"rs_x_{gname}", gs + xs, plan, 7 * nt, after=after)
        self.x[gname] = (names, flight)
        return token

    def reduce(self, gname, after):
        names, flight = self.x.pop(gname)
        nt = len(names)
        bufs = xchg_wait(flight, after)
        mine = jnp.stack([2 * lax.axis_index("x") + lax.axis_index("y"), lax.axis_index("c")]).astype(jnp.int32)
        where = [self._where(n) for n in names]
        js = [add8_into(self.J[kind], l, bufs[t], bufs[nt + t], mine, f"rs_add_{names[t]}")
              for t, (kind, l) in enumerate(where)]

        def plan(refs):
            x, y, c, _ = _place()
            out = []
            for t in range(nt):
                half = refs[t].at[where[t][1], c]
                out.append((half, half, (x, y, 1 - c)))
            return out

        flight, token = xchg_start(f"rs_join_{gname}", js, plan, nt)
        self.j[gname] = (where, flight)
        return token

    def finish(self, gname, after):
        where, flight = self.j.pop(gname)
        for (kind, _), j in zip(where, xchg_wait(flight, after)):
            self.J[kind] = j


def small_allreduce_start(v, after):
    me = 4 * lax.axis_index("x") + 2 * lax.axis_index("y") + lax.axis_index("c")
    land = lax.dynamic_update_slice(lax.empty((8,) + v.shape, v.dtype), v[None], (me, 0, 0))

    def plan(refs):
        x, y, c, _ = _place()
        return [(refs[0], refs[1].at[4 * x + 2 * y + c], _flip(k, x, y, c)) for k in range(1, 8)]

    return xchg_start("small_allreduce", [v, land], plan, 7, after=after)


def sum8(all8, name):
    def body(x_ref, o_ref):
        acc = x_ref[0]
        for d in range(1, 8):
            acc = acc + x_ref[d]
        o_ref[...] = acc

    return pl.pallas_call(
        body, name=name,
        in_specs=[pl.BlockSpec(memory_space=pltpu.VMEM)], out_specs=pl.BlockSpec(memory_space=pltpu.VMEM),
        out_shape=jax.ShapeDtypeStruct(all8.shape[1:], F32),
        compiler_params=pltpu.CompilerParams(vmem_limit_bytes=VMEM_LIMIT),
    )(all8)


AG_GROUPS = (("a0", ("pw1_0", "pw2_0", "small")), ("f0", ("up_0", "down_0")),
             ("l1", ("pw1_1", "pw2_1", "up_1", "down_1")), ("l2", ("kv", "wq_0", "wo_0", "up_2", "down_2")),
             ("l3", ("wq_1", "wo_1", "up_3", "down_3")))


def _bucket_table():
    qi = np.arange(BLK)[:, None]
    kj = np.arange(2 * BLK)[None, :]
    d = np.maximum(qi + BLK - kj, 0)
    max_exact = N_BUCKETS // 2
    log_ratio = (np.log(np.maximum(d, 1).astype(np.float32) / np.float32(max_exact))
                 / np.float32(math.log(MAX_DISTANCE / max_exact))).astype(np.float32)
    large = max_exact + (log_ratio * np.float32(N_BUCKETS - max_exact)).astype(np.int32)
    large = np.minimum(large, N_BUCKETS - 1)
    return np.where(d < max_exact, d, large).astype(np.int32)


def _heads_major(a, nh):
    T = a.shape[0]
    return a.reshape(T, nh, HD).transpose(1, 0, 2)


def _heads_minor(a):
    nh, T, _ = a.shape
    return a.transpose(1, 0, 2).reshape(T, nh * HD)


def _slots(land):
    return land.reshape(4, 2 * land.shape[2], land.shape[3])


def _rows(land):
    return land.reshape(8 * land.shape[2], land.shape[3])


def _gview(g):
    s, K, n = g.shape
    return g.reshape(4, 2, K // 2, n) if s == 4 else g.reshape(4, 2, K // 8, n)


def _gate(a, token):
    return a * (1.0 + token[0, 0])


def _conv_small(f_small):
    fs = f_small.transpose(1, 2, 0, 3).reshape(2, 40, D)
    b_pw1 = f_small[:, :, 35:37, :].transpose(1, 0, 2, 3).reshape(2, 1, 2 * D)
    rev = jnp.concatenate([fs[:, CONV_W - 1::-1], jnp.zeros((2, 40 - CONV_W, D), F32)], axis=1)
    return dict(conv=fs, conv_rev=rev, b_pw1=b_pw1, b_pw2=fs[:, 34:35])


def run_step(x, target, P, ag, rs):
    T = x.shape[0]
    zero = jnp.zeros((1, 1, D), F32)
    nm, nf = P["norm_mix"], P["norm_ffn"]
    ag.forward("a0", [ag.token])
    W = ag.get("a0", [])
    sm = _conv_small(W["small"])
    h = x
    saved = []
    for l in range(2):
        xn, u, a = norm_mm_glu(h, nm, l, _slots(W[f"pw1_{l}"]), sm["b_pw1"], f"f_pw1_{l}")
        y, s = dwconv_ln_silu(a, sm["conv"], l, f"f_conv_{l}")
        b2 = sm["b_pw2"]
        if l == 0:
            b2 = _gate(b2, ag.forward("f0", [s]))
        h1 = mm_bias_res(s, _rows(W[f"pw2_{l}"]), b2, l, h, f"f_pw2_{l}")
        if l == 0:
            W.update(ag.get("f0", [h1]))
        xn2, gu, f = norm_mm_swiglu(h1, nf, l, _slots(W[f"up_{l}"]), f"f_up_{l}")
        nxt = "l1" if l == 0 else "l2"
        h2 = mm_bias_res(f, _rows(W[f"down_{l}"]), _gate(zero, ag.forward(nxt, [f])), 0, h1, f"f_down_{l}")
        W.update(ag.get(nxt, [h2]))
        saved.append(dict(h=h, xn=xn, u=u, a=a, y=y, s=s, h1=h1, xn2=xn2, gu=gu, f=f))
        h = h2
    h_kv = h
    kvn, kv = norm_mm(h, P["norm_kv"], 0, _rows(W["kv"]), "f_kv")
    kp = jnp.pad(_heads_major(kv[:, :N_KV * HD], N_KV), ((0, 0), (BLK, 0), (0, 0)))
    vp = jnp.pad(_heads_major(kv[:, N_KV * HD:], N_KV), ((0, 0), (BLK, 0), (0, 0)))
    kvt = jnp.pad(kv.T.reshape(2, N_KV, HD, T), ((0, 0), (0, 0), (0, 0), (BLK, 0)))
    kt, vt = kvt[0], kvt[1]
    bucket = _bucket_table()
    onehot = jnp.asarray(np.eye(N_BUCKETS, dtype=np.float32)[bucket])
    bias = jnp.einsum("qkb,bh->hkq", onehot, P["rel_bias"], precision=lax.Precision.HIGHEST)
    bias = bias.reshape(N_KV, GROUP, 2 * BLK, BLK).transpose(0, 2, 1, 3).reshape(1, N_KV, 2 * BLK, QW)
    bias = bias + jnp.asarray(band_mask())[:, None]
    for j in range(2):
        l = 2 + j
        xn, q = norm_mm(h, nm, l, _rows(W[f"wq_{j}"]), f"f_q_{j}", scale=HD ** -0.5)
        qh = q.T.reshape(N_KV, GROUP, HD, T)
        sink = jnp.broadcast_to(P["sinks"][j].reshape(N_KV, GROUP, 1), (N_KV, GROUP, BLK)).reshape(N_KV, 1, QW)
        oh = attn_fwd(qh, kp, vt, bias, sink, f"f_attn_{j}")
        attn = oh.reshape(N_HEADS * HD, T).T
        h1 = mm_bias_res(attn, _rows(W[f"wo_{j}"]), zero, 0, h, f"f_wo_{j}")
        xn2, gu, f = norm_mm_swiglu(h1, nf, l, _slots(W[f"up_{l}"]), f"f_up_{l}")
        zg = _gate(zero, ag.forward("l3", [f])) if j == 0 else zero
        h2 = mm_bias_res(f, _rows(W[f"down_{l}"]), zg, 0, h1, f"f_down_{l}")
        if j == 0:
            W.update(ag.get("l3", [h2]))
        saved.append(dict(h=h, xn=xn, qh=qh, oh=oh, sink=sink, attn=attn, h1=h1, xn2=xn2, gu=gu, f=f))
        h = h2

    dh, st_final = final_loss(h, P["norm_final"], target, "loss_head")

    S = dict(norm_ffn=[None] * 4, norm_mix=[None] * 4, conv=[None] * 2, taps=[None] * 2, b_pw1=[None] * 2,
             b_pw2=[None] * 2, sinks=[None] * 2)

    def ffn_bwd(dh, sv, l, nf, after=()):
        du = mmT_swiglu_bwd(dh, _rows(W[f"down_{l}"]), sv["gu"], f"b_down_{l}", after)
        gd = mm_dw(sv["f"], dh, f"w_down_{l}", 512, 1)
        gu = mm_dw(sv["xn2"], du, f"w_up_{l}", DFF // 2, 4)
        dh, dg = mmT_rmsbwd(du, _slots(W[f"up_{l}"]), sv["h1"], nf, l, dh, f"b_up_{l}")
        S["norm_ffn"][l] = dg[0]
        return dh, {f"down_{l}": _gview(gd), f"up_{l}": _gview(gu)}

    dk = dv = dbias = None
    sent = []
    for j in (1, 0):
        l = 2 + j
        sv = saved[l]
        dh, grads = ffn_bwd(dh, sv, l, nf, sent)
        dattn = mmT(dh, _rows(W[f"wo_{j}"]), f"b_wo_{j}")
        grads[f"wo_{j}"] = _gview(mm_dw(sv["attn"], dh, f"w_wo_{j}", 512, 1))
        doh = dattn.T.reshape(N_KV, GROUP, HD, T)
        dqh, dkj, dvj, dbj, dsj = attn_bwd(sv["qh"], kp, kt, vp, bias, sv["sink"], sv["oh"], doh, f"b_attn_{j}")
        dq = dqh.reshape(N_HEADS * HD, T).T
        grads[f"wq_{j}"] = _gview(mm_dw(sv["xn"], dq, f"w_q_{j}", 512, 1))
        dh, dg = mmT_rmsbwd(dq, _rows(W[f"wq_{j}"])[None], sv["h"], nm, l, dh, f"b_q_{j}")
        S["norm_mix"][l] = dg[0]
        S["sinks"][j] = jnp.sum(dsj.reshape(N_HEADS, BLK), axis=1)
        dk = dkj if dk is None else dk + dkj
        dv = dvj if dv is None else dv + dvj
        dbias = dbj if dbias is None else dbias + dbj
        if j == 1:
            sent = [rs.send("l3", grads)]

    dkv = jnp.concatenate([_heads_minor(dk[:, BLK:]), _heads_minor(dv[:, BLK:])], axis=1).astype(BF16)
    grads["kv"] = _gview(mm_dw(kvn, dkv, "w_kv", 512, 1))
    dh, dg = mmT_rmsbwd(dkv, _rows(W["kv"])[None], h_kv, P["norm_kv"], 0, dh, "b_kv")
    S["norm_kv"] = dg[0]
    dbh = dbias.reshape(N_KV, 2 * BLK, GROUP, BLK)
    S["rel_bias"] = jnp.einsum("vkgq,qkb->bvg", dbh, onehot, precision=lax.Precision.HIGHEST).reshape(N_BUCKETS, N_HEADS)
    sent = [rs.send("l2", grads)]
    nf = _gate(nf, rs.reduce("l3", [dh]))

    for l in (1, 0):
        sv = saved[l]
        dh, grads = ffn_bwd(dh, sv, l, nf, sent)
        conv = sm["conv"]
        if l == 0:
            conv = _gate(conv, rs.send("f0", grads))
            grads = {}
        dy, st = mmT_lnbwd(dh, _rows(W[f"pw2_{l}"]), sv["y"], conv, l, f"b_pw2_{l}")
        g2, S["b_pw2"][l] = mm_dw(sv["s"], dh, f"w_pw2_{l}", 512, 1, colsum=True)
        du, dtaps = dwconv_glu_bwd(dy, sv["a"], sv["u"], sm["conv"], sm["conv_rev"], l, f"b_conv_{l}")
        S["conv"][l] = st[0:3]
        S["taps"][l] = dtaps[0:CONV_W]
        if l == 0:
            rs.finish("l2", [du])
            nm = _gate(nm, rs.reduce("l1", [du]))
        g1, S["b_pw1"][l] = mm_dw(sv["xn"], du, f"w_pw1_{l}", 512, 4, colsum=True)
        grads[f"pw2_{l}"], grads[f"pw1_{l}"] = _gview(g2), _gview(g1)
        dh, dg = mmT_rmsbwd(du, _slots(W[f"pw1_{l}"]), sv["h"], nm, l, dh, f"b_pw1_{l}")
        S["norm_mix"][l] = dg[0]
        if l == 1:
            sent = [rs.send("l1", grads)]
            rs.finish("l3", [dh])
            nf = _gate(nf, rs.reduce("l2", [dh]))
    S["norm_final"] = st_final[0]
    S["loss"] = st_final[1]
    return grads, dh, S


R_CONV = 37
R_SMALL = 88


def _pack_small(S):
    rows = []
    for l in range(2):
        rows += [S["taps"][l], S["conv"][l][2:3], S["conv"][l][0:2], S["b_pw2"][l], S["b_pw1"][l].reshape(2, D)]
    rows += [jnp.stack(S["norm_mix"]), jnp.stack(S["norm_ffn"]), S["norm_kv"][None], S["norm_final"][None]]
    tail = jnp.concatenate([jnp.stack(S["sinks"]).reshape(-1), S["rel_bias"].reshape(-1)])
    rows += [jnp.pad(tail, (0, D - tail.shape[0]))[None], S["loss"][None]]
    v = jnp.concatenate(rows, axis=0)
    return jnp.pad(v, ((0, R_SMALL - v.shape[0]), (0, 0)))


def kernel(x, norm_mix, norm_ffn, conv_w_pw1, conv_b_pw1, conv_w_dw, conv_b_dw, conv_ln_g, conv_ln_b, conv_w_pw2, conv_b_pw2, norm_kv, w_kv, w_q, w_o, sinks, rel_bias, ffn_w_up, ffn_w_down, norm_final, loss_target, m_norm_mix, m_norm_ffn, m_conv_w_pw1, m_conv_b_pw1, m_conv_w_dw, m_conv_b_dw, m_conv_ln_g, m_conv_ln_b, m_conv_w_pw2, m_conv_b_pw2, m_norm_kv, m_w_kv, m_w_q, m_w_o, m_sinks, m_rel_bias, m_ffn_w_up, m_ffn_w_down, m_norm_final, v_norm_mix, v_norm_ffn, v_conv_w_pw1, v_conv_b_pw1, v_conv_w_dw, v_conv_b_dw, v_conv_ln_g, v_conv_ln_b, v_conv_w_pw2, v_conv_b_pw2, v_norm_kv, v_w_kv, v_w_q, v_w_o, v_sinks, v_rel_bias, v_ffn_w_up, v_ffn_w_down, v_norm_final):
    me = 2 * lax.axis_index("x") + lax.axis_index("y")
    weights = dict(norm_mix=norm_mix, norm_ffn=norm_ffn, conv_w_pw1=conv_w_pw1, conv_b_pw1=conv_b_pw1,
                   conv_w_dw=conv_w_dw, conv_b_dw=conv_b_dw, conv_ln_g=conv_ln_g, conv_ln_b=conv_ln_b,
                   conv_w_pw2=conv_w_pw2, conv_b_pw2=conv_b_pw2, norm_kv=norm_kv, w_kv=w_kv, w_q=w_q, w_o=w_o,
                   sinks=sinks, rel_bias=rel_bias, ffn_w_up=ffn_w_up, ffn_w_down=ffn_w_down, norm_final=norm_final)
    mom_m = dict(norm_mix=m_norm_mix, norm_ffn=m_norm_ffn, conv_w_pw1=m_conv_w_pw1, conv_b_pw1=m_conv_b_pw1,
                 conv_w_dw=m_conv_w_dw, conv_b_dw=m_conv_b_dw, conv_ln_g=m_conv_ln_g, conv_ln_b=m_conv_ln_b,
                 conv_w_pw2=m_conv_w_pw2, conv_b_pw2=m_conv_b_pw2, norm_kv=m_norm_kv, w_kv=m_w_kv, w_q=m_w_q,
                 w_o=m_w_o, sinks=m_sinks, rel_bias=m_rel_bias, ffn_w_up=m_ffn_w_up, ffn_w_down=m_ffn_w_down,
                 norm_final=m_norm_final)
    mom_v = dict(norm_mix=v_norm_mix, norm_ffn=v_norm_ffn, conv_w_pw1=v_conv_w_pw1, conv_b_pw1=v_conv_b_pw1,
                 conv_w_dw=v_conv_w_dw, conv_b_dw=v_conv_b_dw, conv_ln_g=v_conv_ln_g, conv_ln_b=v_conv_ln_b,
                 conv_w_pw2=v_conv_w_pw2, conv_b_pw2=v_conv_b_pw2, norm_kv=v_norm_kv, w_kv=v_w_kv, w_q=v_w_q,
                 w_o=v_w_o, sinks=v_sinks, rel_bias=v_rel_bias, ffn_w_up=v_ffn_w_up, ffn_w_down=v_ffn_w_down,
                 norm_final=v_norm_final)

    big = {"conv_w_pw1": "pw1", "conv_w_pw2": "pw2", "w_q": "wq", "w_o": "wo", "ffn_w_up": "up",
           "ffn_w_down": "down", "w_kv": "kv"}
    of_kind = {k: n for n, k in big.items()}

    def shard(name, token):
        if name == "small":
            a = jnp.concatenate(
                [conv_w_dw, conv_b_dw[:, None], conv_ln_g[:, None], conv_ln_b[:, None], conv_b_pw2[:, None],
                 conv_b_pw1.reshape(2, 2, 256), jnp.zeros((2, 3, 256), F32)], axis=1)
            return a if token is None else _gate(a, token)
        kind, _, l = name.partition("_")
        a = weights[of_kind[kind]]
        a = a[int(l)] if l else a
        if token is not None:
            a = _gate(a, token)
        return a.astype(BF16).reshape(2, a.shape[0] // 2, a.shape[1])

    ag = WeightGather(shard, AG_GROUPS)
    rs = GradReduce({"pw1": (2, 512, 512), "pw2": (2, 128, D), "wq": (2, 128, D), "wo": (2, 128, D),
                     "up": (4, 512, DFF // 2), "down": (4, DFF // 8, D), "kv": (1, 128, 512)})

    P = dict(norm_mix=norm_mix[:, None], norm_ffn=norm_ffn[:, None], norm_kv=norm_kv[None, None],
             norm_final=norm_final[None], sinks=sinks, rel_bias=rel_bias)
    last, grad_x, S = run_step(x[0], loss_target[0], P, ag, rs)

    rs.finish("l1", [grad_x])
    small_flight, token = small_allreduce_start(_gate(_pack_small(S), rs.reduce("f0", [grad_x])), [])
    token = rs.send("c0", last, after=[token])
    delta, new_m, new_v, big_grads = {}, {}, {}, {}

    def update(n):
        shp = weights[n].shape
        r2 = (int(np.prod(shp[:-1])), shp[-1])
        g, d, nm, nv = adamw(weights[n].reshape(r2), rs.J[big[n]].reshape(r2), mom_m[n].reshape(r2),
                             mom_v[n].reshape(r2), f"adamw_{n}", copy_g=True)
        big_grads[n], delta[n], new_m[n], new_v[n] = g.reshape(shp), d.reshape(shp), nm.reshape(shp), nv.reshape(shp)

    rs.finish("f0", [token])
    for n in ("ffn_w_up", "ffn_w_down"):
        update(n)
    vsum = sum8(xchg_wait(small_flight, [delta["ffn_w_up"], delta["ffn_w_down"]])[1], "small_sum")

    col = lambda a: lax.dynamic_slice_in_dim(a, me * 256, 256, axis=-1)
    grads = {}
    for l in range(2):
        base = l * R_CONV
        grads.setdefault("conv_w_dw", []).append(col(vsum[base:base + 31]))
        grads.setdefault("conv_b_dw", []).append(col(vsum[base + 31]))
        grads.setdefault("conv_ln_g", []).append(col(vsum[base + 32]))
        grads.setdefault("conv_ln_b", []).append(col(vsum[base + 33]))
        grads.setdefault("conv_b_pw2", []).append(col(vsum[base + 34]))
        grads.setdefault("conv_b_pw1", []).append(
            lax.dynamic_slice_in_dim(vsum[base + 35:base + 37].reshape(2 * D), me * 512, 512, axis=0))
    grads = {k: jnp.stack(v) for k, v in grads.items()}
    base = 2 * R_CONV
    grads["norm_mix"] = vsum[base:base + 4]
    grads["norm_ffn"] = vsum[base + 4:base + 8]
    grads["norm_kv"] = vsum[base + 8]
    grads["norm_final"] = vsum[base + 9]
    grads["sinks"] = vsum[base + 10, 0:32].reshape(2, 16)
    grads["rel_bias"] = vsum[base + 10, 32:32 + 512].reshape(32, 16)
    loss = vsum[base + 11, 0]

    rest = [n for n in weights if n not in big]

    def pack(dct):
        flat = jnp.concatenate([dct[n].reshape(-1) for n in rest])
        return jnp.pad(flat, (0, (-flat.shape[0]) % (8 * 128))).reshape(-1, 128)

    d, nm, nv = adamw(pack(weights), pack(grads), pack(mom_m), pack(mom_v), "adamw_small")
    off = 0
    for n in rest:
        shp = weights[n].shape
        sz = int(np.prod(shp))
        delta[n] = d.reshape(-1)[off:off + sz].reshape(shp)
        new_m[n] = nm.reshape(-1)[off:off + sz].reshape(shp)
        new_v[n] = nv.reshape(-1)[off:off + sz].reshape(shp)
        off += sz

    rs.reduce("c0", [vsum])
    for n in ("w_q", "w_o", "w_kv"):
        update(n)
    rs.finish("c0", [delta["w_kv"]])
    for n in ("conv_w_pw1", "conv_w_pw2"):
        update(n)
    grads.update(big_grads)

    order = list(weights)
    return (loss, grad_x[None], *[grads[n] for n in order], *[delta[n] for n in order],
            *[new_m[n] for n in order], *[new_v[n] for n in order])
```

```python
import functools
import math

import numpy as np
import jax
import jax.numpy as jnp
from jax import lax
from jax.experimental import pallas as pl
from jax.experimental.pallas import tpu as pltpu

F32 = jnp.float32
BF16 = jnp.bfloat16
MESH = pl.DeviceIdType.MESH

D = 1024
DFF = 2816
N_HEADS = 16
N_KV = 4
GROUP = 4
HD = 64
BLK = 128
CONV_W = 31
HALO = 32
N_BUCKETS = 32
MAX_DISTANCE = 128
EPS = 1e-6
NEG_INF = -1e30
TM = 512
TCV = 256
VMEM_LIMIT = 56 * 2 ** 20

ADAM_LR, ADAM_B1, ADAM_B2, ADAM_EPS, ADAM_WD, ADAM_STEP = 0.001, 0.9, 0.999, 1e-08, 0.01, 10


def _cp(*sem):
    return pltpu.CompilerParams(dimension_semantics=sem, vmem_limit_bytes=VMEM_LIMIT)


def _sigmoid(x):
    return 1.0 / (1.0 + jnp.exp(-x))


def _row(tm, n):
    return pl.BlockSpec((tm, n), lambda i: (i, 0))


def _const(shape):
    nd = len(shape)
    return pl.BlockSpec(shape, lambda i: (0,) * nd)


def _weight(shape):
    nd = len(shape)
    return pl.BlockSpec(shape, lambda i: (0,) * nd, pipeline_mode=pl.Buffered(1))


def _layer(shape, l):
    nd = len(shape)
    return pl.BlockSpec((None,) + tuple(shape), lambda i: (l,) + (0,) * nd)


def _dot(a, b):
    return jnp.dot(a, b, preferred_element_type=F32)


def _dot_nt(a, b):
    return lax.dot_general(a, b, (((1,), (1,)), ((), ())), preferred_element_type=F32)


def _dot_tn(a, b):
    return lax.dot_general(a, b, (((0,), (0,)), ((), ())), preferred_element_type=F32)


def _rms(x):
    return lax.rsqrt(jnp.mean(x * x, axis=-1, keepdims=True) + EPS)


def norm_mm_glu(h, g, l, w, b, name):
    T = h.shape[0]
    ns = w.shape[-1]

    def body(h_ref, g_ref, w_ref, b_ref, xn_ref, u_ref, a_ref):
        x = h_ref[...]
        xn = (x * _rms(x) * g_ref[...]).astype(BF16)
        xn_ref[...] = xn
        for s in range(2):
            lo, hi = s * ns, (s + 1) * ns
            u1 = _dot(xn, w_ref[s]) + b_ref[:, lo:hi]
            u2 = _dot(xn, w_ref[2 + s]) + b_ref[:, D + lo:D + hi]
            u_ref[:, lo:hi] = u1.astype(BF16)
            u_ref[:, D + lo:D + hi] = u2.astype(BF16)
            a_ref[:, lo:hi] = (u1 * _sigmoid(u2)).astype(BF16)

    return pl.pallas_call(
        body, name=name, grid=(T // TM,),
        in_specs=[_row(TM, D), _layer((1, D), l), _weight((4, D, ns)), _layer((1, 2 * D), l)],
        out_specs=[_row(TM, D), _row(TM, 2 * D), _row(TM, D)],
        out_shape=[jax.ShapeDtypeStruct((T, D), BF16), jax.ShapeDtypeStruct((T, 2 * D), BF16),
                   jax.ShapeDtypeStruct((T, D), BF16)],
        compiler_params=_cp("parallel"),
    )(h, g, w, b)


SUB = 8


def _make_shifts(sh):
    n = TCV + HALO - SUB
    for r in range(1, SUB):
        for r0 in range(0, n, 40):
            sh[r, r0:r0 + 40, :] = sh[0, pl.ds(r + r0, 40), :]


def _shifted(sh, off, rows, cols):
    return sh[off % SUB, pl.ds(off - off % SUB, rows), cols]


def _conv_taps(sh, w_ref, out_ref, first):
    RB, LB = 32, 512
    for r0 in range(0, TCV, RB):
        for c0 in range(0, out_ref.shape[1], LB):
            acc = jnp.zeros((RB, LB), F32)
            for k in range(CONV_W):
                acc = acc + w_ref[k:k + 1, c0:c0 + LB] * _shifted(sh, first + k + r0, RB, slice(c0, c0 + LB))
            out_ref[r0:r0 + RB, c0:c0 + LB] = acc


def dwconv_ln_silu(a, sm, l, name):
    T = a.shape[0]
    nb = TCV // HALO

    def body(cur_ref, prev_ref, sm_ref, y_ref, s_ref, sh, yb):
        i = pl.program_id(0)
        sh[0, 0:HALO, :] = jnp.where(i > 0, prev_ref[...].astype(F32), 0.0)
        sh[0, HALO:HALO + TCV, :] = cur_ref[...].astype(F32)
        _make_shifts(sh)
        _conv_taps(sh, sm_ref, yb, HALO - (CONV_W - 1))
        y = yb[...] + sm_ref[31:32, :]
        y_ref[...] = y.astype(BF16)
        mu = jnp.mean(y, axis=-1, keepdims=True)
        yc = y - mu
        rstd = lax.rsqrt(jnp.mean(yc * yc, axis=-1, keepdims=True) + EPS)
        z = yc * rstd * sm_ref[32:33, :] + sm_ref[33:34, :]
        s_ref[...] = (z * _sigmoid(z)).astype(BF16)

    return pl.pallas_call(
        body, name=name, grid=(T // TCV,),
        in_specs=[_row(TCV, D), pl.BlockSpec((HALO, D), lambda i: (jnp.maximum(i * nb - 1, 0), 0)),
                  _layer((40, D), l)],
        out_specs=[_row(TCV, D), _row(TCV, D)],
        out_shape=[jax.ShapeDtypeStruct((T, D), BF16), jax.ShapeDtypeStruct((T, D), BF16)],
        scratch_shapes=[pltpu.VMEM((SUB, TCV + HALO, D), F32), pltpu.VMEM((TCV, D), F32)],
        compiler_params=_cp("parallel"),
    )(a, a, sm)


def mm_bias_res(xb, w, b, bl, res, name):
    T, K = xb.shape

    def body(x_ref, w_ref, b_ref, r_ref, o_ref):
        o_ref[...] = _dot(x_ref[...], w_ref[...]) + b_ref[...] + r_ref[...]

    return pl.pallas_call(
        body, name=name, grid=(T // TM,),
        in_specs=[_row(TM, K), _weight((K, D)), _layer((1, D), bl), _row(TM, D)],
        out_specs=_row(TM, D), out_shape=jax.ShapeDtypeStruct((T, D), F32),
        compiler_params=_cp("parallel"),
    )(xb, w, b, res)


def norm_mm_swiglu(h, g, l, w, name):
    T = h.shape[0]
    ns = w.shape[-1]

    def body(h_ref, g_ref, w_ref, xn_ref, gu_ref, f_ref):
        x = h_ref[...]
        xn = (x * _rms(x) * g_ref[...]).astype(BF16)
        xn_ref[...] = xn
        for s in range(2):
            lo, hi = s * ns, (s + 1) * ns
            gate = _dot(xn, w_ref[s])
            up = _dot(xn, w_ref[2 + s])
            gu_ref[:, lo:hi] = gate.astype(BF16)
            gu_ref[:, DFF + lo:DFF + hi] = up.astype(BF16)
            f_ref[:, lo:hi] = (gate * _sigmoid(gate) * up).astype(BF16)

    return pl.pallas_call(
        body, name=name, grid=(T // TM,),
        in_specs=[_row(TM, D), _layer((1, D), l), _weight((4, D, ns))],
        out_specs=[_row(TM, D), _row(TM, 2 * DFF), _row(TM, DFF)],
        out_shape=[jax.ShapeDtypeStruct((T, D), BF16), jax.ShapeDtypeStruct((T, 2 * DFF), BF16),
                   jax.ShapeDtypeStruct((T, DFF), BF16)],
        compiler_params=_cp("parallel"),
    )(h, g, w)


def norm_mm(h, g, gl, w, name, scale=1.0):
    T = h.shape[0]
    N = w.shape[-1]

    def body(h_ref, g_ref, w_ref, xn_ref, o_ref):
        x = h_ref[...]
        xn = (x * _rms(x) * g_ref[...]).astype(BF16)
        xn_ref[...] = xn
        o_ref[...] = (_dot(xn, w_ref[...]) * scale).astype(BF16)

    return pl.pallas_call(
        body, name=name, grid=(T // TM,),
        in_specs=[_row(TM, D), _layer((1, D), gl), _weight((D, N))],
        out_specs=[_row(TM, D), _row(TM, N)],
        out_shape=[jax.ShapeDtypeStruct((T, D), BF16), jax.ShapeDtypeStruct((T, N), BF16)],
        compiler_params=_cp("parallel"),
    )(h, g, w)


QB = 4
QW = GROUP * BLK


def band_mask():
    qi = np.arange(QW)[None, :] % BLK
    kj = np.arange(2 * BLK)[:, None]
    band = ((kj < BLK) & (kj > qi)) | ((kj >= BLK) & (kj - BLK <= qi))
    first = band & (kj >= BLK)
    return np.where(np.stack([first, band]), 0.0, NEG_INF).astype(np.float32)


def _softmax_cols(s, sink):
    m = jnp.maximum(jnp.max(s, axis=0, keepdims=True), sink)
    p = jnp.exp(s - m)
    es = jnp.exp(sink - m)
    inv = 1.0 / (jnp.sum(p, axis=0, keepdims=True) + es)
    return p, inv, es


def _attn_specs(T):
    W = QB * BLK
    qspec = pl.BlockSpec((None, GROUP, HD, W), lambda kv, n: (kv, 0, 0, n))
    kspec = pl.BlockSpec((None, T + BLK, HD), lambda kv, n: (kv, 0, 0))
    ktspec = [pl.BlockSpec((None, HD, W), lambda kv, n: (kv, 0, n)),
              pl.BlockSpec((None, HD, BLK), lambda kv, n: (kv, 0, (n + 1) * QB))]
    bspec = pl.BlockSpec((2, None, 2 * BLK, QW), lambda kv, n: (0, kv, 0, 0))
    sspec = pl.BlockSpec((None, 1, QW), lambda kv, n: (kv, 0, 0))
    return qspec, kspec, ktspec, bspec, sspec


def _attn_block(n, b):
    blk = n * QB + b
    rows = pl.ds(pl.multiple_of(blk * BLK, BLK), 2 * BLK)
    return rows, (jnp.minimum(blk, 1) if b == 0 else 1)


def _band_cols(main_ref, tail_ref, b):
    if b < QB - 1:
        return main_ref[:, b * BLK:(b + 2) * BLK]
    return jnp.concatenate([main_ref[:, b * BLK:], tail_ref[...]], axis=1)


def _heads_side_by_side(ref, qs):
    return jnp.concatenate([ref[g, :, qs] for g in range(GROUP)], axis=1)


def attn_fwd(q, kp, vt, bias, sink, name):
    T = q.shape[3]
    qspec, kspec, ktspec, bspec, sspec = _attn_specs(T)

    def body(q_ref, k_ref, vt_ref, vtt_ref, b_ref, s_ref, o_ref, pb):
        n = pl.program_id(1)
        for b in range(QB):
            rows, table = _attn_block(n, b)
            qs = slice(b * BLK, (b + 1) * BLK)
            st = _dot(k_ref[rows, :], _heads_side_by_side(q_ref, qs))
            for g in range(GROUP):
                hs = slice(g * BLK, (g + 1) * BLK)
                p, inv, _ = _softmax_cols(st[:, hs] + b_ref[table, :, hs], s_ref[:, hs])
                pb[:, hs] = (p * inv).astype(BF16)
            ot = _dot(_band_cols(vt_ref, vtt_ref, b), pb[...])
            for g in range(GROUP):
                o_ref[g, :, qs] = ot[:, g * BLK:(g + 1) * BLK].astype(BF16)

    return pl.pallas_call(
        body, name=name, grid=(N_KV, T // (QB * BLK)),
        in_specs=[qspec, kspec, *ktspec, bspec, sspec], out_specs=qspec,
        out_shape=jax.ShapeDtypeStruct((N_KV, GROUP, HD, T), BF16),
        scratch_shapes=[pltpu.VMEM((2 * BLK, QW), BF16)],
        compiler_params=_cp("parallel", "parallel"),
    )(q, kp, vt, vt, bias, sink)


def attn_bwd(q, kp, kt, vp, bias, sink, o, do, name):
    T = q.shape[3]
    qspec, kspec, ktspec, bspec, sspec = _attn_specs(T)

    def body(q_ref, k_ref, kt_ref, ktt_ref, v_ref, b_ref, s_ref, o_ref, do_ref,
             dq_ref, dk_ref, dv_ref, db_ref, ds_ref, pb, dsb):
        n = pl.program_id(1)

        @pl.when(n == 0)
        def _():
            dk_ref[...] = jnp.zeros_like(dk_ref)
            dv_ref[...] = jnp.zeros_like(dv_ref)
            db_ref[...] = jnp.zeros_like(db_ref)
            ds_ref[...] = jnp.zeros_like(ds_ref)

        for b in range(QB):
            rows, table = _attn_block(n, b)
            qs = slice(b * BLK, (b + 1) * BLK)
            q4 = _heads_side_by_side(q_ref, qs)
            do4 = _heads_side_by_side(do_ref, qs)
            st = _dot(k_ref[rows, :], q4)
            dpt = _dot(v_ref[rows, :], do4)
            for g in range(GROUP):
                hs = slice(g * BLK, (g + 1) * BLK)
                p, inv, es = _softmax_cols(st[:, hs] + b_ref[table, :, hs], s_ref[:, hs])
                probs = p * inv
                delta = jnp.sum(do_ref[g, :, qs].astype(F32) * o_ref[g, :, qs].astype(F32), axis=0, keepdims=True)
                dS = probs * (dpt[:, hs] - delta)
                ds_ref[:, hs] += -(es * inv) * delta
                db_ref[:, hs] += dS
                pb[:, hs] = probs.astype(BF16)
                dsb[:, hs] = dS.astype(BF16)
            dqt = _dot(_band_cols(kt_ref, ktt_ref, b), dsb[...]) * (HD ** -0.5)
            for g in range(GROUP):
                dq_ref[g, :, qs] = dqt[:, g * BLK:(g + 1) * BLK].astype(BF16)
            dk_ref[rows, :] += _dot_nt(dsb[...], q4)
            dv_ref[rows, :] += _dot_nt(pb[...], do4)

    kout = pl.BlockSpec((None, T + BLK, HD), lambda kv, n: (kv, 0, 0))
    dbspec = pl.BlockSpec((None, 2 * BLK, QW), lambda kv, n: (kv, 0, 0))
    return pl.pallas_call(
        body, name=name, grid=(N_KV, T // (QB * BLK)),
        in_specs=[qspec, kspec, *ktspec, kspec, bspec, sspec, qspec, qspec],
        out_specs=[qspec, kout, kout, dbspec, sspec],
        out_shape=[jax.ShapeDtypeStruct((N_KV, GROUP, HD, T), BF16),
                   jax.ShapeDtypeStruct((N_KV, T + BLK, HD), F32), jax.ShapeDtypeStruct((N_KV, T + BLK, HD), F32),
                   jax.ShapeDtypeStruct((N_KV, 2 * BLK, QW), F32), jax.ShapeDtypeStruct((N_KV, 1, QW), F32)],
        scratch_shapes=[pltpu.VMEM((2 * BLK, QW), BF16), pltpu.VMEM((2 * BLK, QW), BF16)],
        compiler_params=_cp("parallel", "arbitrary"),
    )(q, kp, kt, kt, vp, bias, sink, o, do)


def final_loss(h, g, target, name):
    T = h.shape[0]

    def body(h_ref, g_ref, t_ref, dh_ref, st_ref):
        i = pl.program_id(0)

        @pl.when(i == 0)
        def _():
            st_ref[...] = jnp.zeros_like(st_ref)

        x = h_ref[...]
        r = _rms(x)
        xh = x * r
        e = xh * g_ref[...] - t_ref[...]
        loss = 0.5 * jnp.sum(jnp.mean(e * e, axis=-1, keepdims=True))
        dy = e * (1.0 / D)
        st_ref[0:1, :] += jnp.sum(dy * xh, axis=0, keepdims=True)
        lane = lax.broadcasted_iota(jnp.int32, (1, D), 1)
        st_ref[1:2, :] += jnp.where(lane == 0, loss, 0.0)
        dxh = dy * g_ref[...]
        dh_ref[...] = r * (dxh - xh * jnp.mean(dxh * xh, axis=-1, keepdims=True))

    return pl.pallas_call(
        body, name=name, grid=(T // TM,),
        in_specs=[_row(TM, D), _const((1, D)), _row(TM, D)],
        out_specs=[_row(TM, D), _const((8, D))],
        out_shape=[jax.ShapeDtypeStruct((T, D), F32), jax.ShapeDtypeStruct((8, D), F32)],
        compiler_params=_cp("arbitrary"),
    )(h, g, target)


def mm_dw(x, dy, name, tn, slots, colsum=False):
    T, K = x.shape
    split = dy.ndim == 3
    N = dy.shape[-1] * (2 if split else 1)
    tt = min(T, 2048 if K <= 1024 else 1024)
    nt = T // tt
    ns = N // slots
    per = ns // tn

    def body(x_ref, dy_ref, *rest):
        if colsum:
            dw_ref, cs_ref, acc, cacc = rest
        else:
            dw_ref, acc = rest
        t = pl.program_id(1)

        @pl.when(t == 0)
        def _():
            acc[...] = jnp.zeros_like(acc)
            if colsum:
                cacc[...] = jnp.zeros_like(cacc)

        dyv = dy_ref[...]
        acc[...] += _dot_tn(x_ref[...].astype(BF16), dyv.astype(BF16))
        if colsum:
            cacc[...] += jnp.sum(dyv.astype(F32), axis=0, keepdims=True)

        @pl.when(t == nt - 1)
        def _():
            dw_ref[...] = acc[...].astype(BF16)
            if colsum:
                cs_ref[...] = cacc[...]

    if split:
        half = N // 2 // tn
        dy_spec = pl.BlockSpec((None, tt, tn), lambda j, t: (j // half, t, j % half))
    else:
        dy_spec = pl.BlockSpec((tt, tn), lambda j, t: (t, j))
    out_specs = [pl.BlockSpec((None, K, tn), lambda j, t: (j // per, 0, j % per))]
    out_shape = [jax.ShapeDtypeStruct((slots, K, ns), BF16)]
    scratch = [pltpu.VMEM((K, tn), F32)]
    if colsum:
        out_specs.append(pl.BlockSpec((1, tn), lambda j, t: (0, j)))
        out_shape.append(jax.ShapeDtypeStruct((1, N), F32))
        scratch.append(pltpu.VMEM((1, tn), F32))
    res = pl.pallas_call(
        body, name=name, grid=(N // tn, nt),
        in_specs=[pl.BlockSpec((tt, K), lambda j, t: (t, 0)), dy_spec],
        out_specs=out_specs, out_shape=out_shape, scratch_shapes=scratch,
        compiler_params=_cp("parallel", "arbitrary"),
    )(x, dy)
    return tuple(res) if colsum else res[0]


def mmT_swiglu_bwd(dh, w, gu, name, after=()):
    T = dh.shape[0]
    cw = 256

    def body(dh_ref, w_ref, gu_ref, *rest):
        du_ref = rest[-1]
        dhb = dh_ref[...].astype(BF16)
        for lo in range(0, DFF, cw):
            hi = lo + cw
            df = _dot_nt(dhb, w_ref[lo:hi, :])
            gate = gu_ref[:, lo:hi].astype(F32)
            up = gu_ref[:, DFF + lo:DFF + hi].astype(F32)
            sg = _sigmoid(gate)
            silu = gate * sg
            du_ref[:, lo:hi] = (df * (up * (sg + silu * (1.0 - sg)))).astype(BF16)
            du_ref[:, DFF + lo:DFF + hi] = (df * silu).astype(BF16)

    return pl.pallas_call(
        body, name=name, grid=(T // TM,),
        in_specs=[_row(TM, D), _weight((DFF, D)), _row(TM, 2 * DFF)] + [ANY] * len(after),
        out_specs=_row(TM, 2 * DFF), out_shape=jax.ShapeDtypeStruct((T, 2 * DFF), BF16),
        compiler_params=_cp("parallel"),
    )(dh, w, gu, *after)


def mmT_rmsbwd(du, w, h, g, gl, dh_in, name):
    split = du.ndim == 3
    T = du.shape[-2]
    N = du.shape[-1] * (2 if split else 1)
    slots = w.shape[0]
    ns = N // slots

    def piece(du_ref, s):
        if split:
            per = slots // 2
            return du_ref[s // per, :, (s % per) * ns:(s % per + 1) * ns]
        return du_ref[:, s * ns:(s + 1) * ns]

    def body(du_ref, w_ref, h_ref, g_ref, di_ref, dh_ref, dg_ref):
        i = pl.program_id(0)

        @pl.when(i == 0)
        def _():
            dg_ref[...] = jnp.zeros_like(dg_ref)

        dxn = _dot_nt(piece(du_ref, 0), w_ref[0])
        for s in range(1, slots):
            dxn = dxn + _dot_nt(piece(du_ref, s), w_ref[s])
        x = h_ref[...]
        r = _rms(x)
        xh = x * r
        dg_ref[0:1, :] += jnp.sum(dxn * xh, axis=0, keepdims=True)
        dxh = dxn * g_ref[...]
        dh_ref[...] = di_ref[...] + r * (dxh - xh * jnp.mean(dxh * xh, axis=-1, keepdims=True))

    return pl.pallas_call(
        body, name=name, grid=(T // TM,),
        in_specs=[pl.BlockSpec((2, TM, N // 2), lambda i: (0, i, 0)) if split else _row(TM, N),
                  _weight((slots, D, ns)), _row(TM, D), _layer((1, D), gl), _row(TM, D)],
        out_specs=[_row(TM, D), _const((8, D))],
        out_shape=[jax.ShapeDtypeStruct((T, D), F32), jax.ShapeDtypeStruct((8, D), F32)],
        compiler_params=_cp("arbitrary"),
    )(du, w, h, g, dh_in)


def mmT(dh, w, name):
    T = dh.shape[0]
    N = w.shape[0]

    def body(dh_ref, w_ref, o_ref):
        o_ref[...] = _dot_nt(dh_ref[...].astype(BF16), w_ref[...]).astype(BF16)

    return pl.pallas_call(
        body, name=name, grid=(T // TM,),
        in_specs=[_row(TM, D), _weight((N, D))],
        out_specs=_row(TM, N), out_shape=jax.ShapeDtypeStruct((T, N), BF16),
        compiler_params=_cp("parallel"),
    )(dh, w)


def mmT_lnbwd(dh, w, y, sm, l, name):
    T = dh.shape[0]

    def body(dh_ref, w_ref, y_ref, sm_ref, dy_ref, st_ref):
        i = pl.program_id(0)

        @pl.when(i == 0)
        def _():
            st_ref[...] = jnp.zeros_like(st_ref)

        ds = _dot_nt(dh_ref[...].astype(BF16), w_ref[...])
        y = y_ref[...].astype(F32)
        mu = jnp.mean(y, axis=-1, keepdims=True)
        yc = y - mu
        rstd = lax.rsqrt(jnp.mean(yc * yc, axis=-1, keepdims=True) + EPS)
        xh = yc * rstd
        gam = sm_ref[32:33, :]
        z = xh * gam + sm_ref[33:34, :]
        sg = _sigmoid(z)
        dz = ds * sg * (1.0 + z * (1.0 - sg))
        st_ref[0:1, :] += jnp.sum(dz * xh, axis=0, keepdims=True)
        st_ref[1:2, :] += jnp.sum(dz, axis=0, keepdims=True)
        dxh = dz * gam
        dy = rstd * (dxh - jnp.mean(dxh, axis=-1, keepdims=True) - xh * jnp.mean(dxh * xh, axis=-1, keepdims=True))
        st_ref[2:3, :] += jnp.sum(dy, axis=0, keepdims=True)
        dy_ref[...] = dy.astype(BF16)

    return pl.pallas_call(
        body, name=name, grid=(T // TM,),
        in_specs=[_row(TM, D), _weight((D, D)), _row(TM, D), _layer((40, D), l)],
        out_specs=[_row(TM, D), _const((8, D))],
        out_shape=[jax.ShapeDtypeStruct((T, D), BF16), jax.ShapeDtypeStruct((8, D), F32)],
        compiler_params=_cp("arbitrary"),
    )(dh, w, y, sm)


CH = 512


def dwconv_glu_bwd(dy, a, u, sm, smrev, l, name):
    T = dy.shape[0]
    nr, nc = T // TCV, D // CH
    nb = TCV // HALO
    last = T // HALO - 1

    def body(dy_ref, dyn_ref, a_ref, ap_ref, u1_ref, u2_ref, sm_ref, rev_ref, du_ref, dw_ref, shd, sha, da):
        i = pl.program_id(0)
        r = i % nr

        @pl.when(r == 0)
        def _():
            dw_ref[...] = jnp.zeros_like(dw_ref)

        shd[0, 0:TCV, :] = dy_ref[...].astype(F32)
        shd[0, TCV:TCV + HALO, :] = jnp.where(r < nr - 1, dyn_ref[...].astype(F32), 0.0)
        sha[0, 0:HALO, :] = jnp.where(r > 0, ap_ref[...].astype(F32), 0.0)
        sha[0, HALO:HALO + TCV, :] = a_ref[...].astype(F32)
        _make_shifts(shd)
        _make_shifts(sha)
        _conv_taps(shd, rev_ref, da, 0)
        for k in range(CONV_W):
            part = jnp.zeros((SUB, CH), F32)
            for r0 in range(0, TCV, SUB):
                part = part + shd[0, r0:r0 + SUB, :] * _shifted(sha, HALO - (CONV_W - 1) + k + r0, SUB, slice(None))
            dw_ref[k:k + 1, :] += jnp.sum(part, axis=0, keepdims=True)
        dav = da[...]
        u1 = u1_ref[...].astype(F32)
        sg = _sigmoid(u2_ref[...].astype(F32))
        du_ref[0] = (dav * sg).astype(BF16)
        du_ref[1] = (dav * u1 * sg * (1.0 - sg)).astype(BF16)

    tile = lambda i: (i % nr, i // nr)
    in_specs = [pl.BlockSpec((TCV, CH), tile),
                pl.BlockSpec((HALO, CH), lambda i: (jnp.minimum((i % nr + 1) * nb, last), i // nr)),
                pl.BlockSpec((TCV, CH), tile),
                pl.BlockSpec((HALO, CH), lambda i: (jnp.maximum((i % nr) * nb - 1, 0), i // nr)),
                pl.BlockSpec((TCV, CH), tile), pl.BlockSpec((TCV, CH), lambda i: (i % nr, nc + i // nr)),
                pl.BlockSpec((None, 40, CH), lambda i: (l, 0, i // nr)),
                pl.BlockSpec((None, 40, CH), lambda i: (l, 0, i // nr))]
    return pl.pallas_call(
        body, name=name, grid=(nr * nc,), in_specs=in_specs,
        out_specs=[pl.BlockSpec((2, TCV, CH), lambda i: (0, i % nr, i // nr)),
                   pl.BlockSpec((32, CH), lambda i: (0, i // nr))],
        out_shape=[jax.ShapeDtypeStruct((2, T, D), BF16), jax.ShapeDtypeStruct((32, D), F32)],
        scratch_shapes=[pltpu.VMEM((SUB, TCV + HALO, CH), F32), pltpu.VMEM((SUB, TCV + HALO, CH), F32),
                        pltpu.VMEM((TCV, CH), F32)],
        compiler_params=_cp("arbitrary"),
    )(dy, dy, a, a, u, u, sm, smrev)


def _rows_tile(R):
    for t in (512, 256, 128, 64, 32, 16, 8):
        if R % t == 0:
            return t
    return R


def add8_into(J, l, g, others, where, name):
    R, C = g.shape[2:]
    tr = R // 2

    def body(w_ref, g_ref, x_ref, j_in, j_ref):
        acc = g_ref[...].astype(F32)
        for k in range(7):
            acc = acc + x_ref[k].astype(F32)
        j_ref[...] = acc

    return pl.pallas_call(
        body, name=name,
        grid_spec=pltpu.PrefetchScalarGridSpec(
            num_scalar_prefetch=1, grid=(R // tr,),
            in_specs=[pl.BlockSpec((None, None, tr, C), lambda i, w: (w[0], w[1], i, 0)),
                      pl.BlockSpec((7, tr, C), lambda i, w: (0, i, 0)), ANY],
            out_specs=pl.BlockSpec((None, None, tr, C), lambda i, w: (l, w[1], i, 0))),
        out_shape=jax.ShapeDtypeStruct(J.shape, F32), input_output_aliases={3: 0},
        compiler_params=_cp("parallel"),
    )(where, g, others, J)


def adamw(w, g, m, v, name, copy_g=False):
    R, C = w.shape
    tr = _rows_tile(R)

    def body(w_ref, g_ref, m_ref, v_ref, *outs):
        d_ref, nm_ref, nv_ref = outs[-3:]
        gv = g_ref[...]
        if copy_g:
            outs[0][...] = gv
        nm = ADAM_B1 * m_ref[...] + (1.0 - ADAM_B1) * gv
        nv = ADAM_B2 * v_ref[...] + (1.0 - ADAM_B2) * (gv * gv)
        m_hat = nm / (1.0 - ADAM_B1 ** ADAM_STEP)
        v_hat = nv / (1.0 - ADAM_B2 ** ADAM_STEP)
        d_ref[...] = -ADAM_LR * (m_hat / (jnp.sqrt(v_hat) + ADAM_EPS) + ADAM_WD * w_ref[...])
        nm_ref[...] = nm
        nv_ref[...] = nv

    sd = jax.ShapeDtypeStruct((R, C), F32)
    n_out = 4 if copy_g else 3
    return pl.pallas_call(
        body, name=name, grid=(R // tr,),
        in_specs=[_row(tr, C)] * 4, out_specs=[_row(tr, C)] * n_out, out_shape=[sd] * n_out,
        compiler_params=_cp("parallel"),
    )(w, g, m, v)


ANY = pl.BlockSpec(memory_space=pl.ANY)
HBM = pl.BlockSpec(memory_space=pltpu.HBM)
SEM = pl.BlockSpec(memory_space=pltpu.SEMAPHORE)
EFFECT = pltpu.SideEffectType.DATAFLOW_SIDE_EFFECTING


def _place():
    x, y, c = lax.axis_index("x"), lax.axis_index("y"), lax.axis_index("c")
    chips = [(1 - x, y), (x, 1 - y), (1 - x, 1 - y)]
    return x, y, c, chips


def _copy(src, dst, send, recv, k, to):
    return pltpu.make_async_remote_copy(src_ref=src, dst_ref=dst, send_sem=send.at[k], recv_sem=recv.at[k],
                                        device_id=to, device_id_type=MESH)


def xchg_start(name, bufs, plan, n, after=()):
    nb = len(bufs)

    na = len(after)

    def body(*refs):
        send, recv, token = refs[nb + na], refs[nb + na + 1], refs[-1]
        for k, (src, dst, to) in enumerate(plan(refs[:nb])):
            _copy(src, dst, send, recv, k, to).start()
        token[...] = jnp.zeros_like(token)

    outs = pl.pallas_call(
        body, name=name,
        out_shape=(pltpu.SemaphoreType.DMA((n,)), pltpu.SemaphoreType.DMA((n,)),
                   *[pltpu.HBM(b.shape, b.dtype) for b in bufs], jax.ShapeDtypeStruct((8, 128), F32)),
        in_specs=[HBM] * nb + [ANY] * na,
        out_specs=(SEM, SEM, *[HBM] * nb, pl.BlockSpec(memory_space=pltpu.VMEM)),
        input_output_aliases={i: 2 + i for i in range(nb)},
        compiler_params=pltpu.CompilerParams(has_side_effects=EFFECT),
    )(*[pltpu.with_memory_space_constraint(b, pltpu.HBM) for b in bufs], *after)
    return dict(name=name, send=outs[0], recv=outs[1], bufs=list(outs[2:2 + nb]), plan=plan), outs[-1]


def xchg_wait(flight, after):
    bufs, plan = flight["bufs"], flight["plan"]
    nb = len(bufs)

    def body(*refs):
        send, recv = refs[nb], refs[nb + 1]
        for k, (src, dst, to) in enumerate(plan(refs[:nb])):
            cp = _copy(src, dst, send, recv, k, to)
            cp.wait_send()
            cp.wait_recv()

    outs = pl.pallas_call(
        body, name=flight["name"] + "_wait",
        out_shape=tuple(pltpu.HBM(b.shape, b.dtype) for b in bufs),
        in_specs=[HBM] * nb + [SEM, SEM] + [ANY] * len(after),
        out_specs=tuple([HBM] * nb), input_output_aliases={i: i for i in range(nb)},
        compiler_params=pltpu.CompilerParams(has_side_effects=EFFECT),
    )(*bufs, flight["send"], flight["recv"], *after)
    return list(outs)


def _flip(k, x, y, c):
    return ((1 - x) if k & 4 else x, (1 - y) if k & 2 else y, (1 - c) if k & 1 else c)


class WeightGather:
    def __init__(self, shard, groups):
        me = 2 * lax.axis_index("x") + lax.axis_index("y")
        self.names = dict(groups)
        self.ici, self.d2d = {}, {}
        self.token = None
        for gname, names in groups:
            nt = len(names)
            srcs = [shard(n, self.token) for n in names]
            lands = [lax.dynamic_update_slice(lax.empty((4,) + s.shape, s.dtype), s[None], (me, 0, 0, 0))
                     for s in srcs]

            def plan(refs, nt=nt):
                x, y, c, chips = _place()
                return [(refs[t].at[c], refs[nt + t].at[2 * x + y, c], (cx, cy, c))
                        for t in range(nt) for cx, cy in chips]

            self.ici[gname], self.token = xchg_start(f"ag_ici_{gname}", srcs + lands, plan, 3 * nt,
                                                     after=[] if self.token is None else [self.token])

    def forward(self, gname, after):
        nt = len(self.names[gname])
        lands = xchg_wait(self.ici.pop(gname), after)[nt:]

        def plan(refs):
            x, y, c, chips = _place()
            out = []
            for t in range(nt):
                for cx, cy in chips:
                    piece = refs[t].at[2 * cx + cy, c]
                    out.append((piece, piece, (x, y, 1 - c)))
            return out

        self.d2d[gname], token = xchg_start(f"ag_d2d_{gname}", lands, plan, 3 * nt)
        return token

    def get(self, gname, after):
        lands = xchg_wait(self.d2d.pop(gname), after)
        return dict(zip(self.names[gname], lands))


class GradReduce:
    def __init__(self, kinds):
        self.J = {k: lax.empty((L, 2, a2, b), F32) for k, (L, a2, b) in kinds.items()}
        self.x, self.j = {}, {}

    @staticmethod
    def _where(name):
        kind, _, l = name.partition("_")
        return kind, int(l or 0)

    def send(self, gname, grads, after=()):
        names = list(grads)
        nt = len(names)
        gs = [grads[n] for n in names]
        xs = [lax.empty((7,) + g.shape[2:], g.dtype) for g in gs]

        def plan(refs):
            x, y, c, _ = _place()
            out = []
            for t in range(nt):
                for k in range(1, 8):
                    px, py, pc = _flip(k, x, y, c)
                    out.append((refs[t].at[2 * px + py, pc], refs[nt + t].at[k - 1], (px, py, pc)))
            return out

        flight, token = xchg_start(f"rs_x_{gname}", gs + xs, plan, 7 * nt, after=after)
        self.x[gname] = (names, flight)
        return token

    def reduce(self, gname, after):
        names, flight = self.x.pop(gname)
        nt = len(names)
        bufs = xchg_wait(flight, after)
        mine = jnp.stack([2 * lax.axis_index("x") + lax.axis_index("y"), lax.axis_index("c")]).astype(jnp.int32)
        where = [self._where(n) for n in names]
        js = [add8_into(self.J[kind], l, bufs[t], bufs[nt + t], mine, f"rs_add_{names[t]}")
              for t, (kind, l) in enumerate(where)]

        def plan(refs):
            x, y, c, _ = _place()
            out = []
            for t in range(nt):
                half = refs[t].at[where[t][1], c]
                out.append((half, half, (x, y, 1 - c)))
            return out

        flight, token = xchg_start(f"rs_join_{gname}", js, plan, nt)
        self.j[gname] = (where, flight)
        return token

    def finish(self, gname, after):
        where, flight = self.j.pop(gname)
        for (kind, _), j in zip(where, xchg_wait(flight, after)):
            self.J[kind] = j


def small_allreduce_start(v, after):
    me = 4 * lax.axis_index("x") + 2 * lax.axis_index("y") + lax.axis_index("c")
    land = lax.dynamic_update_slice(lax.empty((8,) + v.shape, v.dtype), v[None], (me, 0, 0))

    def plan(refs):
        x, y, c, _ = _place()
        return [(refs[0], refs[1].at[4 * x + 2 * y + c], _flip(k, x, y, c)) for k in range(1, 8)]

    return xchg_start("small_allreduce", [v, land], plan, 7, after=after)


def sum8(all8, name):
    def body(x_ref, o_ref):
        acc = x_ref[0]
        for d in range(1, 8):
            acc = acc + x_ref[d]
        o_ref[...] = acc

    return pl.pallas_call(
        body, name=name,
        in_specs=[pl.BlockSpec(memory_space=pltpu.VMEM)], out_specs=pl.BlockSpec(memory_space=pltpu.VMEM),
        out_shape=jax.ShapeDtypeStruct(all8.shape[1:], F32),
        compiler_params=pltpu.CompilerParams(vmem_limit_bytes=VMEM_LIMIT),
    )(all8)


AG_GROUPS = (("a0", ("pw1_0", "pw2_0", "small")), ("f0", ("up_0", "down_0")),
             ("l1", ("pw1_1", "pw2_1", "up_1", "down_1")), ("l2", ("kv", "wq_0", "wo_0", "up_2", "down_2")),
             ("l3", ("wq_1", "wo_1", "up_3", "down_3")))


def _bucket_table():
    qi = np.arange(BLK)[:, None]
    kj = np.arange(2 * BLK)[None, :]
    d = np.maximum(qi + BLK - kj, 0)
    max_exact = N_BUCKETS // 2
    log_ratio = (np.log(np.maximum(d, 1).astype(np.float32) / np.float32(max_exact))
                 / np.float32(math.log(MAX_DISTANCE / max_exact))).astype(np.float32)
    large = max_exact + (log_ratio * np.float32(N_BUCKETS - max_exact)).astype(np.int32)
    large = np.minimum(large, N_BUCKETS - 1)
    return np.where(d < max_exact, d, large).astype(np.int32)


def _heads_major(a, nh):
    T = a.shape[0]
    return a.reshape(T, nh, HD).transpose(1, 0, 2)


def _heads_minor(a):
    nh, T, _ = a.shape
    return a.transpose(1, 0, 2).reshape(T, nh * HD)


def _slots(land):
    return land.reshape(4, 2 * land.shape[2], land.shape[3])


def _rows(land):
    return land.reshape(8 * land.shape[2], land.shape[3])


def _gview(g):
    s, K, n = g.shape
    return g.reshape(4, 2, K // 2, n) if s == 4 else g.reshape(4, 2, K // 8, n)


def _gate(a, token):
    return a * (1.0 + token[0, 0])


def _conv_small(f_small):
    fs = f_small.transpose(1, 2, 0, 3).reshape(2, 40, D)
    b_pw1 = f_small[:, :, 35:37, :].transpose(1, 0, 2, 3).reshape(2, 1, 2 * D)
    rev = jnp.concatenate([fs[:, CONV_W - 1::-1], jnp.zeros((2, 40 - CONV_W, D), F32)], axis=1)
    return dict(conv=fs, conv_rev=rev, b_pw1=b_pw1, b_pw2=fs[:, 34:35])


def run_step(x, target, P, ag, rs):
    T = x.shape[0]
    zero = jnp.zeros((1, 1, D), F32)
    nm, nf = P["norm_mix"], P["norm_ffn"]
    ag.forward("a0", [ag.token])
    W = ag.get("a0", [])
    sm = _conv_small(W["small"])
    h = x
    saved = []
    for l in range(2):
        xn, u, a = norm_mm_glu(h, nm, l, _slots(W[f"pw1_{l}"]), sm["b_pw1"], f"f_pw1_{l}")
        y, s = dwconv_ln_silu(a, sm["conv"], l, f"f_conv_{l}")
        b2 = sm["b_pw2"]
        if l == 0:
            b2 = _gate(b2, ag.forward("f0", [s]))
        h1 = mm_bias_res(s, _rows(W[f"pw2_{l}"]), b2, l, h, f"f_pw2_{l}")
        if l == 0:
            W.update(ag.get("f0", [h1]))
        xn2, gu, f = norm_mm_swiglu(h1, nf, l, _slots(W[f"up_{l}"]), f"f_up_{l}")
        nxt = "l1" if l == 0 else "l2"
        h2 = mm_bias_res(f, _rows(W[f"down_{l}"]), _gate(zero, ag.forward(nxt, [f])), 0, h1, f"f_down_{l}")
        W.update(ag.get(nxt, [h2]))
        saved.append(dict(h=h, xn=xn, u=u, a=a, y=y, s=s, h1=h1, xn2=xn2, gu=gu, f=f))
        h = h2
    h_kv = h
    kvn, kv = norm_mm(h, P["norm_kv"], 0, _rows(W["kv"]), "f_kv")
    kp = jnp.pad(_heads_major(kv[:, :N_KV * HD], N_KV), ((0, 0), (BLK, 0), (0, 0)))
    vp = jnp.pad(_heads_major(kv[:, N_KV * HD:], N_KV), ((0, 0), (BLK, 0), (0, 0)))
    kvt = jnp.pad(kv.T.reshape(2, N_KV, HD, T), ((0, 0), (0, 0), (0, 0), (BLK, 0)))
    kt, vt = kvt[0], kvt[1]
    bucket = _bucket_table()
    onehot = jnp.asarray(np.eye(N_BUCKETS, dtype=np.float32)[bucket])
    bias = jnp.einsum("qkb,bh->hkq", onehot, P["rel_bias"], precision=lax.Precision.HIGHEST)
    bias = bias.reshape(N_KV, GROUP, 2 * BLK, BLK).transpose(0, 2, 1, 3).reshape(1, N_KV, 2 * BLK, QW)
    bias = bias + jnp.asarray(band_mask())[:, None]
    for j in range(2):
        l = 2 + j
        xn, q = norm_mm(h, nm, l, _rows(W[f"wq_{j}"]), f"f_q_{j}", scale=HD ** -0.5)
        qh = q.T.reshape(N_KV, GROUP, HD, T)
        sink = jnp.broadcast_to(P["sinks"][j].reshape(N_KV, GROUP, 1), (N_KV, GROUP, BLK)).reshape(N_KV, 1, QW)
        oh = attn_fwd(qh, kp, vt, bias, sink, f"f_attn_{j}")
        attn = oh.reshape(N_HEADS * HD, T).T
        h1 = mm_bias_res(attn, _rows(W[f"wo_{j}"]), zero, 0, h, f"f_wo_{j}")
        xn2, gu, f = norm_mm_swiglu(h1, nf, l, _slots(W[f"up_{l}"]), f"f_up_{l}")
        zg = _gate(zero, ag.forward("l3", [f])) if j == 0 else zero
        h2 = mm_bias_res(f, _rows(W[f"down_{l}"]), zg, 0, h1, f"f_down_{l}")
        if j == 0:
            W.update(ag.get("l3", [h2]))
        saved.append(dict(h=h, xn=xn, qh=qh, oh=oh, sink=sink, attn=attn, h1=h1, xn2=xn2, gu=gu, f=f))
        h = h2

    dh, st_final = final_loss(h, P["norm_final"], target, "loss_head")

    S = dict(norm_ffn=[None] * 4, norm_mix=[None] * 4, conv=[None] * 2, taps=[None] * 2, b_pw1=[None] * 2,
             b_pw2=[None] * 2, sinks=[None] * 2)

    def ffn_bwd(dh, sv, l, nf, after=()):
        du = mmT_swiglu_bwd(dh, _rows(W[f"down_{l}"]), sv["gu"], f"b_down_{l}", after)
        gd = mm_dw(sv["f"], dh, f"w_down_{l}", 512, 1)
        gu = mm_dw(sv["xn2"], du, f"w_up_{l}", DFF // 2, 4)
        dh, dg = mmT_rmsbwd(du, _slots(W[f"up_{l}"]), sv["h1"], nf, l, dh, f"b_up_{l}")
        S["norm_ffn"][l] = dg[0]
        return dh, {f"down_{l}": _gview(gd), f"up_{l}": _gview(gu)}

    dk = dv = dbias = None
    sent = []
    for j in (1, 0):
        l = 2 + j
        sv = saved[l]
        dh, grads = ffn_bwd(dh, sv, l, nf, sent)
        dattn = mmT(dh, _rows(W[f"wo_{j}"]), f"b_wo_{j}")
        grads[f"wo_{j}"] = _gview(mm_dw(sv["attn"], dh, f"w_wo_{j}", 512, 1))
        doh = dattn.T.reshape(N_KV, GROUP, HD, T)
        dqh, dkj, dvj, dbj, dsj = attn_bwd(sv["qh"], kp, kt, vp, bias, sv["sink"], sv["oh"], doh, f"b_attn_{j}")
        dq = dqh.reshape(N_HEADS * HD, T).T
        grads[f"wq_{j}"] = _gview(mm_dw(sv["xn"], dq, f"w_q_{j}", 512, 1))
        dh, dg = mmT_rmsbwd(dq, _rows(W[f"wq_{j}"])[None], sv["h"], nm, l, dh, f"b_q_{j}")
        S["norm_mix"][l] = dg[0]
        S["sinks"][j] = jnp.sum(dsj.reshape(N_HEADS, BLK), axis=1)
        dk = dkj if dk is None else dk + dkj
        dv = dvj if dv is None else dv + dvj
        dbias = dbj if dbias is None else dbias + dbj
        if j == 1:
            sent = [rs.send("l3", grads)]

    dkv = jnp.concatenate([_heads_minor(dk[:, BLK:]), _heads_minor(dv[:, BLK:])], axis=1).astype(BF16)
    grads["kv"] = _gview(mm_dw(kvn, dkv, "w_kv", 512, 1))
    dh, dg = mmT_rmsbwd(dkv, _rows(W["kv"])[None], h_kv, P["norm_kv"], 0, dh, "b_kv")
    S["norm_kv"] = dg[0]
    dbh = dbias.reshape(N_KV, 2 * BLK, GROUP, BLK)
    S["rel_bias"] = jnp.einsum("vkgq,qkb->bvg", dbh, onehot, precision=lax.Precision.HIGHEST).reshape(N_BUCKETS, N_HEADS)
    sent = [rs.send("l2", grads)]
    nf = _gate(nf, rs.reduce("l3", [dh]))

    for l in (1, 0):
        sv = saved[l]
        dh, grads = ffn_bwd(dh, sv, l, nf, sent)
        conv = sm["conv"]
        if l == 0:
            conv = _gate(conv, rs.send("f0", grads))
            grads = {}
        dy, st = mmT_lnbwd(dh, _rows(W[f"pw2_{l}"]), sv["y"], conv, l, f"b_pw2_{l}")
        g2, S["b_pw2"][l] = mm_dw(sv["s"], dh, f"w_pw2_{l}", 512, 1, colsum=True)
        du, dtaps = dwconv_glu_bwd(dy, sv["a"], sv["u"], sm["conv"], sm["conv_rev"], l, f"b_conv_{l}")
        S["conv"][l] = st[0:3]
        S["taps"][l] = dtaps[0:CONV_W]
        if l == 0:
            rs.finish("l2", [du])
            nm = _gate(nm, rs.reduce("l1", [du]))
        g1, S["b_pw1"][l] = mm_dw(sv["xn"], du, f"w_pw1_{l}", 512, 4, colsum=True)
        grads[f"pw2_{l}"], grads[f"pw1_{l}"] = _gview(g2), _gview(g1)
        dh, dg = mmT_rmsbwd(du, _slots(W[f"pw1_{l}"]), sv["h"], nm, l, dh, f"b_pw1_{l}")
        S["norm_mix"][l] = dg[0]
        if l == 1:
            sent = [rs.send("l1", grads)]
            rs.finish("l3", [dh])
            nf = _gate(nf, rs.reduce("l2", [dh]))
    S["norm_final"] = st_final[0]
    S["loss"] = st_final[1]
    return grads, dh, S


R_CONV = 37
R_SMALL = 88


def _pack_small(S):
    rows = []
    for l in range(2):
        rows += [S["taps"][l], S["conv"][l][2:3], S["conv"][l][0:2], S["b_pw2"][l], S["b_pw1"][l].reshape(2, D)]
    rows += [jnp.stack(S["norm_mix"]), jnp.stack(S["norm_ffn"]), S["norm_kv"][None], S["norm_final"][None]]
    tail = jnp.concatenate([jnp.stack(S["sinks"]).reshape(-1), S["rel_bias"].reshape(-1)])
    rows += [jnp.pad(tail, (0, D - tail.shape[0]))[None], S["loss"][None]]
    v = jnp.concatenate(rows, axis=0)
    return jnp.pad(v, ((0, R_SMALL - v.shape[0]), (0, 0)))


def kernel(x, norm_mix, norm_ffn, conv_w_pw1, conv_b_pw1, conv_w_dw, conv_b_dw, conv_ln_g, conv_ln_b, conv_w_pw2, conv_b_pw2, norm_kv, w_kv, w_q, w_o, sinks, rel_bias, ffn_w_up, ffn_w_down, norm_final, loss_target, m_norm_mix, m_norm_ffn, m_conv_w_pw1, m_conv_b_pw1, m_conv_w_dw, m_conv_b_dw, m_conv_ln_g, m_conv_ln_b, m_conv_w_pw2, m_conv_b_pw2, m_norm_kv, m_w_kv, m_w_q, m_w_o, m_sinks, m_rel_bias, m_ffn_w_up, m_ffn_w_down, m_norm_final, v_norm_mix, v_norm_ffn, v_conv_w_pw1, v_conv_b_pw1, v_conv_w_dw, v_conv_b_dw, v_conv_ln_g, v_conv_ln_b, v_conv_w_pw2, v_conv_b_pw2, v_norm_kv, v_w_kv, v_w_q, v_w_o, v_sinks, v_rel_bias, v_ffn_w_up, v_ffn_w_down, v_norm_final):
    me = 2 * lax.axis_index("x") + lax.axis_index("y")
    weights = dict(norm_mix=norm_mix, norm_ffn=norm_ffn, conv_w_pw1=conv_w_pw1, conv_b_pw1=conv_b_pw1,
                   conv_w_dw=conv_w_dw, conv_b_dw=conv_b_dw, conv_ln_g=conv_ln_g, conv_ln_b=conv_ln_b,
                   conv_w_pw2=conv_w_pw2, conv_b_pw2=conv_b_pw2, norm_kv=norm_kv, w_kv=w_kv, w_q=w_q, w_o=w_o,
                   sinks=sinks, rel_bias=rel_bias, ffn_w_up=ffn_w_up, ffn_w_down=ffn_w_down, norm_final=norm_final)
    mom_m = dict(norm_mix=m_norm_mix, norm_ffn=m_norm_ffn, conv_w_pw1=m_conv_w_pw1, conv_b_pw1=m_conv_b_pw1,
                 conv_w_dw=m_conv_w_dw, conv_b_dw=m_conv_b_dw, conv_ln_g=m_conv_ln_g, conv_ln_b=m_conv_ln_b,
                 conv_w_pw2=m_conv_w_pw2, conv_b_pw2=m_conv_b_pw2, norm_kv=m_norm_kv, w_kv=m_w_kv, w_q=m_w_q,
                 w_o=m_w_o, sinks=m_sinks, rel_bias=m_rel_bias, ffn_w_up=m_ffn_w_up, ffn_w_down=m_ffn_w_down,
                 norm_final=m_norm_final)
    mom_v = dict(norm_mix=v_norm_mix, norm_ffn=v_norm_ffn, conv_w_pw1=v_conv_w_pw1, conv_b_pw1=v_conv_b_pw1,
                 conv_w_dw=v_conv_w_dw, conv_b_dw=v_conv_b_dw, conv_ln_g=v_conv_ln_g, conv_ln_b=v_conv_ln_b,
                 conv_w_pw2=v_conv_w_pw2, conv_b_pw2=v_conv_b_pw2, norm_kv=v_norm_kv, w_kv=v_w_kv, w_q=v_w_q,
                 w_o=v_w_o, sinks=v_sinks, rel_bias=v_rel_bias, ffn_w_up=v_ffn_w_up, ffn_w_down=v_ffn_w_down,
                 norm_final=v_norm_final)

    big = {"conv_w_pw1": "pw1", "conv_w_pw2": "pw2", "w_q": "wq", "w_o": "wo", "ffn_w_up": "up",
           "ffn_w_down": "down", "w_kv": "kv"}
    of_kind = {k: n for n, k in big.items()}

    def shard(name, token):
        if name == "small":
            a = jnp.concatenate(
                [conv_w_dw, conv_b_dw[:, None], conv_ln_g[:, None], conv_ln_b[:, None], conv_b_pw2[:, None],
                 conv_b_pw1.reshape(2, 2, 256), jnp.zeros((2, 3, 256), F32)], axis=1)
            return a if token is None else _gate(a, token)
        kind, _, l = name.partition("_")
        a = weights[of_kind[kind]]
        a = a[int(l)] if l else a
        if token is not None:
            a = _gate(a, token)
        return a.astype(BF16).reshape(2, a.shape[0] // 2, a.shape[1])

    ag = WeightGather(shard, AG_GROUPS)
    rs = GradReduce({"pw1": (2, 512, 512), "pw2": (2, 128, D), "wq": (2, 128, D), "wo": (2, 128, D),
                     "up": (4, 512, DFF // 2), "down": (4, DFF // 8, D), "kv": (1, 128, 512)})

    P = dict(norm_mix=norm_mix[:, None], norm_ffn=norm_ffn[:, None], norm_kv=norm_kv[None, None],
             norm_final=norm_final[None], sinks=sinks, rel_bias=rel_bias)
    last, grad_x, S = run_step(x[0], loss_target[0], P, ag, rs)

    rs.finish("l1", [grad_x])
    small_flight, token = small_allreduce_start(_gate(_pack_small(S), rs.reduce("f0", [grad_x])), [])
    token = rs.send("c0", last, after=[token])
    delta, new_m, new_v, big_grads = {}, {}, {}, {}

    def update(n):
        shp = weights[n].shape
        r2 = (int(np.prod(shp[:-1])), shp[-1])
        g, d, nm, nv = adamw(weights[n].reshape(r2), rs.J[big[n]].reshape(r2), mom_m[n].reshape(r2),
                             mom_v[n].reshape(r2), f"adamw_{n}", copy_g=True)
        big_grads[n], delta[n], new_m[n], new_v[n] = g.reshape(shp), d.reshape(shp), nm.reshape(shp), nv.reshape(shp)

    rs.finish("f0", [token])
    for n in ("ffn_w_up", "ffn_w_down"):
        update(n)
    vsum = sum8(xchg_wait(small_flight, [delta["ffn_w_up"], delta["ffn_w_down"]])[1], "small_sum")

    col = lambda a: lax.dynamic_slice_in_dim(a, me * 256, 256, axis=-1)
    grads = {}
    for l in range(2):
        base = l * R_CONV
        grads.setdefault("conv_w_dw", []).append(col(vsum[base:base + 31]))
        grads.setdefault("conv_b_dw", []).append(col(vsum[base + 31]))
        grads.setdefault("conv_ln_g", []).append(col(vsum[base + 32]))
        grads.setdefault("conv_ln_b", []).append(col(vsum[base + 33]))
        grads.setdefault("conv_b_pw2", []).append(col(vsum[base + 34]))
        grads.setdefault("conv_b_pw1", []).append(
            lax.dynamic_slice_in_dim(vsum[base + 35:base + 37].reshape(2 * D), me * 512, 512, axis=0))
    grads = {k: jnp.stack(v) for k, v in grads.items()}
    base = 2 * R_CONV
    grads["norm_mix"] = vsum[base:base + 4]
    grads["norm_ffn"] = vsum[base + 4:base + 8]
    grads["norm_kv"] = vsum[base + 8]
    grads["norm_final"] = vsum[base + 9]
    grads["sinks"] = vsum[base + 10, 0:32].reshape(2, 16)
    grads["rel_bias"] = vsum[base + 10, 32:32 + 512].reshape(32, 16)
    loss = vsum[base + 11, 0]

    for n in weights:
        if n not in big:
            shp = weights[n].shape
            r2 = (int(np.prod(shp[:-1])), shp[-1])
            d, nm, nv = adamw(weights[n].reshape(r2), grads[n].reshape(r2), mom_m[n].reshape(r2),
                              mom_v[n].reshape(r2), f"adamw_{n}")
            delta[n], new_m[n], new_v[n] = d.reshape(shp), nm.reshape(shp), nv.reshape(shp)

    rs.reduce("c0", [vsum])
    for n in ("w_q", "w_o", "w_kv"):
        update(n)
    rs.finish("c0", [delta["w_kv"]])
    for n in ("conv_w_pw1", "conv_w_pw2"):
        update(n)
    grads.update(big_grads)

    order = list(weights)
    return (loss, grad_x[None], *[grads[n] for n in order], *[delta[n] for n in order],
            *[new_m[n] for n in order], *[new_v[n] for n in order])
```

```python
import functools
import math

import numpy as np
import jax
import jax.numpy as jnp
from jax import lax
from jax.experimental import pallas as pl
from jax.experimental.pallas import tpu as pltpu

F32 = jnp.float32
BF16 = jnp.bfloat16
MESH = pl.DeviceIdType.MESH

D = 1024
DFF = 2816
N_HEADS = 16
N_KV = 4
GROUP = 4
HD = 64
BLK = 128
CONV_W = 31
HALO = 32
N_BUCKETS = 32
MAX_DISTANCE = 128
EPS = 1e-6
NEG_INF = -1e30
TM = 512
TCV = 256
VMEM_LIMIT = 56 * 2 ** 20

ADAM_LR, ADAM_B1, ADAM_B2, ADAM_EPS, ADAM_WD, ADAM_STEP = 0.001, 0.9, 0.999, 1e-08, 0.01, 10


def _cp(*sem):
    return pltpu.CompilerParams(dimension_semantics=sem, vmem_limit_bytes=VMEM_LIMIT)


def _sigmoid(x):
    return 1.0 / (1.0 + jnp.exp(-x))


def _row(tm, n):
    return pl.BlockSpec((tm, n), lambda i: (i, 0))


def _const(shape):
    nd = len(shape)
    return pl.BlockSpec(shape, lambda i: (0,) * nd)


def _weight(shape):
    nd = len(shape)
    return pl.BlockSpec(shape, lambda i: (0,) * nd, pipeline_mode=pl.Buffered(1))


def _layer(shape, l):
    nd = len(shape)
    return pl.BlockSpec((None,) + tuple(shape), lambda i: (l,) + (0,) * nd)


def _dot(a, b):
    return jnp.dot(a, b, preferred_element_type=F32)


def _dot_nt(a, b):
    return lax.dot_general(a, b, (((1,), (1,)), ((), ())), preferred_element_type=F32)


def _dot_tn(a, b):
    return lax.dot_general(a, b, (((0,), (0,)), ((), ())), preferred_element_type=F32)


def _rms(x):
    return lax.rsqrt(jnp.mean(x * x, axis=-1, keepdims=True) + EPS)


def norm_mm_glu(h, g, l, w, b, name):
    T = h.shape[0]
    ns = w.shape[-1]

    def body(h_ref, g_ref, w_ref, b_ref, xn_ref, u_ref, a_ref):
        x = h_ref[...]
        xn = (x * _rms(x) * g_ref[...]).astype(BF16)
        xn_ref[...] = xn
        for s in range(2):
            lo, hi = s * ns, (s + 1) * ns
            u1 = _dot(xn, w_ref[s]) + b_ref[:, lo:hi]
            u2 = _dot(xn, w_ref[2 + s]) + b_ref[:, D + lo:D + hi]
            u_ref[:, lo:hi] = u1.astype(BF16)
            u_ref[:, D + lo:D + hi] = u2.astype(BF16)
            a_ref[:, lo:hi] = (u1 * _sigmoid(u2)).astype(BF16)

    return pl.pallas_call(
        body, name=name, grid=(T // TM,),
        in_specs=[_row(TM, D), _layer((1, D), l), _weight((4, D, ns)), _layer((1, 2 * D), l)],
        out_specs=[_row(TM, D), _row(TM, 2 * D), _row(TM, D)],
        out_shape=[jax.ShapeDtypeStruct((T, D), BF16), jax.ShapeDtypeStruct((T, 2 * D), BF16),
                   jax.ShapeDtypeStruct((T, D), BF16)],
        compiler_params=_cp("parallel"),
    )(h, g, w, b)


SUB = 8


def _make_shifts(sh):
    n = TCV + HALO - SUB
    for r in range(1, SUB):
        for r0 in range(0, n, 40):
            sh[r, r0:r0 + 40, :] = sh[0, pl.ds(r + r0, 40), :]


def _shifted(sh, off, rows, cols):
    return sh[off % SUB, pl.ds(off - off % SUB, rows), cols]


def _conv_taps(sh, w_ref, out_ref, first):
    RB, LB = 32, 512
    for r0 in range(0, TCV, RB):
        for c0 in range(0, out_ref.shape[1], LB):
            acc = jnp.zeros((RB, LB), F32)
            for k in range(CONV_W):
                acc = acc + w_ref[k:k + 1, c0:c0 + LB] * _shifted(sh, first + k + r0, RB, slice(c0, c0 + LB))
            out_ref[r0:r0 + RB, c0:c0 + LB] = acc


def dwconv_ln_silu(a, sm, l, name):
    T = a.shape[0]
    nb = TCV // HALO

    def body(cur_ref, prev_ref, sm_ref, y_ref, s_ref, sh, yb):
        i = pl.program_id(0)
        sh[0, 0:HALO, :] = jnp.where(i > 0, prev_ref[...].astype(F32), 0.0)
        sh[0, HALO:HALO + TCV, :] = cur_ref[...].astype(F32)
        _make_shifts(sh)
        _conv_taps(sh, sm_ref, yb, HALO - (CONV_W - 1))
        y = yb[...] + sm_ref[31:32, :]
        y_ref[...] = y.astype(BF16)
        mu = jnp.mean(y, axis=-1, keepdims=True)
        yc = y - mu
        rstd = lax.rsqrt(jnp.mean(yc * yc, axis=-1, keepdims=True) + EPS)
        z = yc * rstd * sm_ref[32:33, :] + sm_ref[33:34, :]
        s_ref[...] = (z * _sigmoid(z)).astype(BF16)

    return pl.pallas_call(
        body, name=name, grid=(T // TCV,),
        in_specs=[_row(TCV, D), pl.BlockSpec((HALO, D), lambda i: (jnp.maximum(i * nb - 1, 0), 0)),
                  _layer((40, D), l)],
        out_specs=[_row(TCV, D), _row(TCV, D)],
        out_shape=[jax.ShapeDtypeStruct((T, D), BF16), jax.ShapeDtypeStruct((T, D), BF16)],
        scratch_shapes=[pltpu.VMEM((SUB, TCV + HALO, D), F32), pltpu.VMEM((TCV, D), F32)],
        compiler_params=_cp("parallel"),
    )(a, a, sm)


def mm_bias_res(xb, w, b, bl, res, name):
    T, K = xb.shape

    def body(x_ref, w_ref, b_ref, r_ref, o_ref):
        o_ref[...] = _dot(x_ref[...], w_ref[...]) + b_ref[...] + r_ref[...]

    return pl.pallas_call(
        body, name=name, grid=(T // TM,),
        in_specs=[_row(TM, K), _weight((K, D)), _layer((1, D), bl), _row(TM, D)],
        out_specs=_row(TM, D), out_shape=jax.ShapeDtypeStruct((T, D), F32),
        compiler_params=_cp("parallel"),
    )(xb, w, b, res)


def norm_mm_swiglu(h, g, l, w, name):
    T = h.shape[0]
    ns = w.shape[-1]

    def body(h_ref, g_ref, w_ref, xn_ref, gu_ref, f_ref):
        x = h_ref[...]
        xn = (x * _rms(x) * g_ref[...]).astype(BF16)
        xn_ref[...] = xn
        for s in range(2):
            lo, hi = s * ns, (s + 1) * ns
            gate = _dot(xn, w_ref[s])
            up = _dot(xn, w_ref[2 + s])
            gu_ref[:, lo:hi] = gate.astype(BF16)
            gu_ref[:, DFF + lo:DFF + hi] = up.astype(BF16)
            f_ref[:, lo:hi] = (gate * _sigmoid(gate) * up).astype(BF16)

    return pl.pallas_call(
        body, name=name, grid=(T // TM,),
        in_specs=[_row(TM, D), _layer((1, D), l), _weight((4, D, ns))],
        out_specs=[_row(TM, D), _row(TM, 2 * DFF), _row(TM, DFF)],
        out_shape=[jax.ShapeDtypeStruct((T, D), BF16), jax.ShapeDtypeStruct((T, 2 * DFF), BF16),
                   jax.ShapeDtypeStruct((T, DFF), BF16)],
        compiler_params=_cp("parallel"),
    )(h, g, w)


def norm_mm(h, g, gl, w, name, scale=1.0):
    T = h.shape[0]
    N = w.shape[-1]

    def body(h_ref, g_ref, w_ref, xn_ref, o_ref):
        x = h_ref[...]
        xn = (x * _rms(x) * g_ref[...]).astype(BF16)
        xn_ref[...] = xn
        o_ref[...] = (_dot(xn, w_ref[...]) * scale).astype(BF16)

    return pl.pallas_call(
        body, name=name, grid=(T // TM,),
        in_specs=[_row(TM, D), _layer((1, D), gl), _weight((D, N))],
        out_specs=[_row(TM, D), _row(TM, N)],
        out_shape=[jax.ShapeDtypeStruct((T, D), BF16), jax.ShapeDtypeStruct((T, N), BF16)],
        compiler_params=_cp("parallel"),
    )(h, g, w)


QB = 4
QW = GROUP * BLK


def band_mask():
    qi = np.arange(QW)[None, :] % BLK
    kj = np.arange(2 * BLK)[:, None]
    band = ((kj < BLK) & (kj > qi)) | ((kj >= BLK) & (kj - BLK <= qi))
    first = band & (kj >= BLK)
    return np.where(np.stack([first, band]), 0.0, NEG_INF).astype(np.float32)


def _softmax_cols(s, sink):
    m = jnp.maximum(jnp.max(s, axis=0, keepdims=True), sink)
    p = jnp.exp(s - m)
    es = jnp.exp(sink - m)
    inv = 1.0 / (jnp.sum(p, axis=0, keepdims=True) + es)
    return p, inv, es


def _attn_specs(T):
    W = QB * BLK
    qspec = pl.BlockSpec((None, GROUP, HD, W), lambda kv, n: (kv, 0, 0, n))
    kspec = pl.BlockSpec((None, T + BLK, HD), lambda kv, n: (kv, 0, 0))
    ktspec = [pl.BlockSpec((None, HD, W), lambda kv, n: (kv, 0, n)),
              pl.BlockSpec((None, HD, BLK), lambda kv, n: (kv, 0, (n + 1) * QB))]
    bspec = pl.BlockSpec((2, None, 2 * BLK, QW), lambda kv, n: (0, kv, 0, 0))
    sspec = pl.BlockSpec((None, 1, QW), lambda kv, n: (kv, 0, 0))
    return qspec, kspec, ktspec, bspec, sspec


def _attn_block(n, b):
    blk = n * QB + b
    rows = pl.ds(pl.multiple_of(blk * BLK, BLK), 2 * BLK)
    return rows, (jnp.minimum(blk, 1) if b == 0 else 1)


def _band_cols(main_ref, tail_ref, b):
    if b < QB - 1:
        return main_ref[:, b * BLK:(b + 2) * BLK]
    return jnp.concatenate([main_ref[:, b * BLK:], tail_ref[...]], axis=1)


def _heads_side_by_side(ref, qs):
    return jnp.concatenate([ref[g, :, qs] for g in range(GROUP)], axis=1)


def attn_fwd(q, kp, vt, bias, sink, name):
    T = q.shape[3]
    qspec, kspec, ktspec, bspec, sspec = _attn_specs(T)

    def body(q_ref, k_ref, vt_ref, vtt_ref, b_ref, s_ref, o_ref, pb):
        n = pl.program_id(1)

        def scores(b):
            return _dot(k_ref[_attn_block(n, b)[0], :], _heads_side_by_side(q_ref, slice(b * BLK, (b + 1) * BLK)))

        st_next = scores(0)
        for b in range(QB):
            rows, table = _attn_block(n, b)
            qs = slice(b * BLK, (b + 1) * BLK)
            st = st_next
            if b + 1 < QB:
                st_next = scores(b + 1)
            for g in range(GROUP):
                hs = slice(g * BLK, (g + 1) * BLK)
                p, inv, _ = _softmax_cols(st[:, hs] + b_ref[table, :, hs], s_ref[:, hs])
                pb[:, hs] = (p * inv).astype(BF16)
            ot = _dot(_band_cols(vt_ref, vtt_ref, b), pb[...])
            for g in range(GROUP):
                o_ref[g, :, qs] = ot[:, g * BLK:(g + 1) * BLK].astype(BF16)

    return pl.pallas_call(
        body, name=name, grid=(N_KV, T // (QB * BLK)),
        in_specs=[qspec, kspec, *ktspec, bspec, sspec], out_specs=qspec,
        out_shape=jax.ShapeDtypeStruct((N_KV, GROUP, HD, T), BF16),
        scratch_shapes=[pltpu.VMEM((2 * BLK, QW), BF16)],
        compiler_params=_cp("parallel", "parallel"),
    )(q, kp, vt, vt, bias, sink)


def attn_bwd(q, kp, kt, vp, bias, sink, o, do, name):
    T = q.shape[3]
    qspec, kspec, ktspec, bspec, sspec = _attn_specs(T)

    def body(q_ref, k_ref, kt_ref, ktt_ref, v_ref, b_ref, s_ref, o_ref, do_ref,
             dq_ref, dk_ref, dv_ref, db_ref, ds_ref, pb, dsb):
        n = pl.program_id(1)

        @pl.when(n == 0)
        def _():
            dk_ref[...] = jnp.zeros_like(dk_ref)
            dv_ref[...] = jnp.zeros_like(dv_ref)
            db_ref[...] = jnp.zeros_like(db_ref)
            ds_ref[...] = jnp.zeros_like(ds_ref)

        def products(b):
            rows = _attn_block(n, b)[0]
            qs = slice(b * BLK, (b + 1) * BLK)
            q4, do4 = _heads_side_by_side(q_ref, qs), _heads_side_by_side(do_ref, qs)
            return q4, do4, _dot(k_ref[rows, :], q4), _dot(v_ref[rows, :], do4)

        ahead = products(0)
        for b in range(QB):
            rows, table = _attn_block(n, b)
            qs = slice(b * BLK, (b + 1) * BLK)
            q4, do4, st, dpt = ahead
            if b + 1 < QB:
                ahead = products(b + 1)
            for g in range(GROUP):
                hs = slice(g * BLK, (g + 1) * BLK)
                p, inv, es = _softmax_cols(st[:, hs] + b_ref[table, :, hs], s_ref[:, hs])
                probs = p * inv
                delta = jnp.sum(do_ref[g, :, qs].astype(F32) * o_ref[g, :, qs].astype(F32), axis=0, keepdims=True)
                dS = probs * (dpt[:, hs] - delta)
                ds_ref[:, hs] += -(es * inv) * delta
                db_ref[:, hs] += dS
                pb[:, hs] = probs.astype(BF16)
                dsb[:, hs] = dS.astype(BF16)
            dqt = _dot(_band_cols(kt_ref, ktt_ref, b), dsb[...]) * (HD ** -0.5)
            for g in range(GROUP):
                dq_ref[g, :, qs] = dqt[:, g * BLK:(g + 1) * BLK].astype(BF16)
            dk_ref[rows, :] += _dot_nt(dsb[...], q4)
            dv_ref[rows, :] += _dot_nt(pb[...], do4)

    kout = pl.BlockSpec((None, T + BLK, HD), lambda kv, n: (kv, 0, 0))
    dbspec = pl.BlockSpec((None, 2 * BLK, QW), lambda kv, n: (kv, 0, 0))
    return pl.pallas_call(
        body, name=name, grid=(N_KV, T // (QB * BLK)),
        in_specs=[qspec, kspec, *ktspec, kspec, bspec, sspec, qspec, qspec],
        out_specs=[qspec, kout, kout, dbspec, sspec],
        out_shape=[jax.ShapeDtypeStruct((N_KV, GROUP, HD, T), BF16),
                   jax.ShapeDtypeStruct((N_KV, T + BLK, HD), F32), jax.ShapeDtypeStruct((N_KV, T + BLK, HD), F32),
                   jax.ShapeDtypeStruct((N_KV, 2 * BLK, QW), F32), jax.ShapeDtypeStruct((N_KV, 1, QW), F32)],
        scratch_shapes=[pltpu.VMEM((2 * BLK, QW), BF16), pltpu.VMEM((2 * BLK, QW), BF16)],
        compiler_params=_cp("parallel", "arbitrary"),
    )(q, kp, kt, kt, vp, bias, sink, o, do)


def final_loss(h, g, target, name):
    T = h.shape[0]

    def body(h_ref, g_ref, t_ref, dh_ref, st_ref):
        i = pl.program_id(0)

        @pl.when(i == 0)
        def _():
            st_ref[...] = jnp.zeros_like(st_ref)

        x = h_ref[...]
        r = _rms(x)
        xh = x * r
        e = xh * g_ref[...] - t_ref[...]
        loss = 0.5 * jnp.sum(jnp.mean(e * e, axis=-1, keepdims=True))
        dy = e * (1.0 / D)
        st_ref[0:1, :] += jnp.sum(dy * xh, axis=0, keepdims=True)
        lane = lax.broadcasted_iota(jnp.int32, (1, D), 1)
        st_ref[1:2, :] += jnp.where(lane == 0, loss, 0.0)
        dxh = dy * g_ref[...]
        dh_ref[...] = r * (dxh - xh * jnp.mean(dxh * xh, axis=-1, keepdims=True))

    return pl.pallas_call(
        body, name=name, grid=(T // TM,),
        in_specs=[_row(TM, D), _const((1, D)), _row(TM, D)],
        out_specs=[_row(TM, D), _const((8, D))],
        out_shape=[jax.ShapeDtypeStruct((T, D), F32), jax.ShapeDtypeStruct((8, D), F32)],
        compiler_params=_cp("arbitrary"),
    )(h, g, target)


def mm_dw(x, dy, name, tn, slots, colsum=False):
    T, K = x.shape
    split = dy.ndim == 3
    N = dy.shape[-1] * (2 if split else 1)
    tt = min(T, 2048 if K <= 1024 else 1024)
    nt = T // tt
    ns = N // slots
    per = ns // tn

    def body(x_ref, dy_ref, *rest):
        if colsum:
            dw_ref, cs_ref, acc, cacc = rest
        else:
            dw_ref, acc = rest
        t = pl.program_id(1)

        @pl.when(t == 0)
        def _():
            acc[...] = jnp.zeros_like(acc)
            if colsum:
                cacc[...] = jnp.zeros_like(cacc)

        dyv = dy_ref[...]
        acc[...] += _dot_tn(x_ref[...].astype(BF16), dyv.astype(BF16))
        if colsum:
            cacc[...] += jnp.sum(dyv.astype(F32), axis=0, keepdims=True)

        @pl.when(t == nt - 1)
        def _():
            dw_ref[...] = acc[...].astype(BF16)
            if colsum:
                cs_ref[...] = cacc[...]

    if split:
        half = N // 2 // tn
        dy_spec = pl.BlockSpec((None, tt, tn), lambda j, t: (j // half, t, j % half))
    else:
        dy_spec = pl.BlockSpec((tt, tn), lambda j, t: (t, j))
    out_specs = [pl.BlockSpec((None, K, tn), lambda j, t: (j // per, 0, j % per))]
    out_shape = [jax.ShapeDtypeStruct((slots, K, ns), BF16)]
    scratch = [pltpu.VMEM((K, tn), F32)]
    if colsum:
        out_specs.append(pl.BlockSpec((1, tn), lambda j, t: (0, j)))
        out_shape.append(jax.ShapeDtypeStruct((1, N), F32))
        scratch.append(pltpu.VMEM((1, tn), F32))
    res = pl.pallas_call(
        body, name=name, grid=(N // tn, nt),
        in_specs=[pl.BlockSpec((tt, K), lambda j, t: (t, 0)), dy_spec],
        out_specs=out_specs, out_shape=out_shape, scratch_shapes=scratch,
        compiler_params=_cp("parallel", "arbitrary"),
    )(x, dy)
    return tuple(res) if colsum else res[0]


def mmT_swiglu_bwd(dh, w, gu, name, after=()):
    T = dh.shape[0]
    cw = 256

    def body(dh_ref, w_ref, gu_ref, *rest):
        du_ref = rest[-1]
        dhb = dh_ref[...].astype(BF16)
        ahead = _dot_nt(dhb, w_ref[0:cw, :])
        for lo in range(0, DFF, cw):
            hi = lo + cw
            df = ahead
            if hi < DFF:
                ahead = _dot_nt(dhb, w_ref[hi:hi + cw, :])
            gate = gu_ref[:, lo:hi].astype(F32)
            up = gu_ref[:, DFF + lo:DFF + hi].astype(F32)
            sg = _sigmoid(gate)
            silu = gate * sg
            du_ref[:, lo:hi] = (df * (up * (sg + silu * (1.0 - sg)))).astype(BF16)
            du_ref[:, DFF + lo:DFF + hi] = (df * silu).astype(BF16)

    return pl.pallas_call(
        body, name=name, grid=(T // TM,),
        in_specs=[_row(TM, D), _weight((DFF, D)), _row(TM, 2 * DFF)] + [ANY] * len(after),
        out_specs=_row(TM, 2 * DFF), out_shape=jax.ShapeDtypeStruct((T, 2 * DFF), BF16),
        compiler_params=_cp("parallel"),
    )(dh, w, gu, *after)


def mmT_rmsbwd(du, w, h, g, gl, dh_in, name):
    split = du.ndim == 3
    T = du.shape[-2]
    N = du.shape[-1] * (2 if split else 1)
    slots = w.shape[0]
    ns = N // slots

    def piece(du_ref, s):
        if split:
            per = slots // 2
            return du_ref[s // per, :, (s % per) * ns:(s % per + 1) * ns]
        return du_ref[:, s * ns:(s + 1) * ns]

    def body(du_ref, w_ref, h_ref, g_ref, di_ref, dh_ref, dg_ref):
        i = pl.program_id(0)

        @pl.when(i == 0)
        def _():
            dg_ref[...] = jnp.zeros_like(dg_ref)

        dxn = _dot_nt(piece(du_ref, 0), w_ref[0])
        for s in range(1, slots):
            dxn = dxn + _dot_nt(piece(du_ref, s), w_ref[s])
        x = h_ref[...]
        r = _rms(x)
        xh = x * r
        dg_ref[0:1, :] += jnp.sum(dxn * xh, axis=0, keepdims=True)
        dxh = dxn * g_ref[...]
        dh_ref[...] = di_ref[...] + r * (dxh - xh * jnp.mean(dxh * xh, axis=-1, keepdims=True))

    return pl.pallas_call(
        body, name=name, grid=(T // TM,),
        in_specs=[pl.BlockSpec((2, TM, N // 2), lambda i: (0, i, 0)) if split else _row(TM, N),
                  _weight((slots, D, ns)), _row(TM, D), _layer((1, D), gl), _row(TM, D)],
        out_specs=[_row(TM, D), _const((8, D))],
        out_shape=[jax.ShapeDtypeStruct((T, D), F32), jax.ShapeDtypeStruct((8, D), F32)],
        compiler_params=_cp("arbitrary"),
    )(du, w, h, g, dh_in)


def mmT(dh, w, name):
    T = dh.shape[0]
    N = w.shape[0]

    def body(dh_ref, w_ref, o_ref):
        o_ref[...] = _dot_nt(dh_ref[...].astype(BF16), w_ref[...]).astype(BF16)

    return pl.pallas_call(
        body, name=name, grid=(T // TM,),
        in_specs=[_row(TM, D), _weight((N, D))],
        out_specs=_row(TM, N), out_shape=jax.ShapeDtypeStruct((T, N), BF16),
        compiler_params=_cp("parallel"),
    )(dh, w)


def mmT_lnbwd(dh, w, y, sm, l, name):
    T = dh.shape[0]

    def body(dh_ref, w_ref, y_ref, sm_ref, dy_ref, st_ref):
        i = pl.program_id(0)

        @pl.when(i == 0)
        def _():
            st_ref[...] = jnp.zeros_like(st_ref)

        ds = _dot_nt(dh_ref[...].astype(BF16), w_ref[...])
        y = y_ref[...].astype(F32)
        mu = jnp.mean(y, axis=-1, keepdims=True)
        yc = y - mu
        rstd = lax.rsqrt(jnp.mean(yc * yc, axis=-1, keepdims=True) + EPS)
        xh = yc * rstd
        gam = sm_ref[32:33, :]
        z = xh * gam + sm_ref[33:34, :]
        sg = _sigmoid(z)
        dz = ds * sg * (1.0 + z * (1.0 - sg))
        st_ref[0:1, :] += jnp.sum(dz * xh, axis=0, keepdims=True)
        st_ref[1:2, :] += jnp.sum(dz, axis=0, keepdims=True)
        dxh = dz * gam
        dy = rstd * (dxh - jnp.mean(dxh, axis=-1, keepdims=True) - xh * jnp.mean(dxh * xh, axis=-1, keepdims=True))
        st_ref[2:3, :] += jnp.sum(dy, axis=0, keepdims=True)
        dy_ref[...] = dy.astype(BF16)

    return pl.pallas_call(
        body, name=name, grid=(T // TM,),
        in_specs=[_row(TM, D), _weight((D, D)), _row(TM, D), _layer((40, D), l)],
        out_specs=[_row(TM, D), _const((8, D))],
        out_shape=[jax.ShapeDtypeStruct((T, D), BF16), jax.ShapeDtypeStruct((8, D), F32)],
        compiler_params=_cp("arbitrary"),
    )(dh, w, y, sm)


CH = 512


def dwconv_glu_bwd(dy, a, u, sm, smrev, l, name):
    T = dy.shape[0]
    nr, nc = T // TCV, D // CH
    nb = TCV // HALO
    last = T // HALO - 1

    def body(dy_ref, dyn_ref, a_ref, ap_ref, u1_ref, u2_ref, sm_ref, rev_ref, du_ref, dw_ref, shd, sha, da):
        i = pl.program_id(0)
        r = i % nr

        @pl.when(r == 0)
        def _():
            dw_ref[...] = jnp.zeros_like(dw_ref)

        shd[0, 0:TCV, :] = dy_ref[...].astype(F32)
        shd[0, TCV:TCV + HALO, :] = jnp.where(r < nr - 1, dyn_ref[...].astype(F32), 0.0)
        sha[0, 0:HALO, :] = jnp.where(r > 0, ap_ref[...].astype(F32), 0.0)
        sha[0, HALO:HALO + TCV, :] = a_ref[...].astype(F32)
        _make_shifts(shd)
        _make_shifts(sha)
        _conv_taps(shd, rev_ref, da, 0)
        for k in range(CONV_W):
            part = jnp.zeros((SUB, CH), F32)
            for r0 in range(0, TCV, SUB):
                part = part + shd[0, r0:r0 + SUB, :] * _shifted(sha, HALO - (CONV_W - 1) + k + r0, SUB, slice(None))
            dw_ref[k:k + 1, :] += jnp.sum(part, axis=0, keepdims=True)
        dav = da[...]
        u1 = u1_ref[...].astype(F32)
        sg = _sigmoid(u2_ref[...].astype(F32))
        du_ref[0] = (dav * sg).astype(BF16)
        du_ref[1] = (dav * u1 * sg * (1.0 - sg)).astype(BF16)

    tile = lambda i: (i % nr, i // nr)
    in_specs = [pl.BlockSpec((TCV, CH), tile),
                pl.BlockSpec((HALO, CH), lambda i: (jnp.minimum((i % nr + 1) * nb, last), i // nr)),
                pl.BlockSpec((TCV, CH), tile),
                pl.BlockSpec((HALO, CH), lambda i: (jnp.maximum((i % nr) * nb - 1, 0), i // nr)),
                pl.BlockSpec((TCV, CH), tile), pl.BlockSpec((TCV, CH), lambda i: (i % nr, nc + i // nr)),
                pl.BlockSpec((None, 40, CH), lambda i: (l, 0, i // nr)),
                pl.BlockSpec((None, 40, CH), lambda i: (l, 0, i // nr))]
    return pl.pallas_call(
        body, name=name, grid=(nr * nc,), in_specs=in_specs,
        out_specs=[pl.BlockSpec((2, TCV, CH), lambda i: (0, i % nr, i // nr)),
                   pl.BlockSpec((32, CH), lambda i: (0, i // nr))],
        out_shape=[jax.ShapeDtypeStruct((2, T, D), BF16), jax.ShapeDtypeStruct((32, D), F32)],
        scratch_shapes=[pltpu.VMEM((SUB, TCV + HALO, CH), F32), pltpu.VMEM((SUB, TCV + HALO, CH), F32),
                        pltpu.VMEM((TCV, CH), F32)],
        compiler_params=_cp("arbitrary"),
    )(dy, dy, a, a, u, u, sm, smrev)


def _rows_tile(R):
    for t in (512, 256, 128, 64, 32, 16, 8):
        if R % t == 0:
            return t
    return R


def add8_into(J, l, g, others, where, name):
    R, C = g.shape[2:]
    tr = R // 2

    def body(w_ref, g_ref, x_ref, j_in, j_ref):
        acc = g_ref[...].astype(F32)
        for k in range(7):
            acc = acc + x_ref[k].astype(F32)
        j_ref[...] = acc

    return pl.pallas_call(
        body, name=name,
        grid_spec=pltpu.PrefetchScalarGridSpec(
            num_scalar_prefetch=1, grid=(R // tr,),
            in_specs=[pl.BlockSpec((None, None, tr, C), lambda i, w: (w[0], w[1], i, 0)),
                      pl.BlockSpec((7, tr, C), lambda i, w: (0, i, 0)), ANY],
            out_specs=pl.BlockSpec((None, None, tr, C), lambda i, w: (l, w[1], i, 0))),
        out_shape=jax.ShapeDtypeStruct(J.shape, F32), input_output_aliases={3: 0},
        compiler_params=_cp("parallel"),
    )(where, g, others, J)


def adamw(w, g, m, v, name, copy_g=False):
    R, C = w.shape
    tr = _rows_tile(R)

    def body(w_ref, g_ref, m_ref, v_ref, *outs):
        d_ref, nm_ref, nv_ref = outs[-3:]
        gv = g_ref[...]
        if copy_g:
            outs[0][...] = gv
        nm = ADAM_B1 * m_ref[...] + (1.0 - ADAM_B1) * gv
        nv = ADAM_B2 * v_ref[...] + (1.0 - ADAM_B2) * (gv * gv)
        m_hat = nm / (1.0 - ADAM_B1 ** ADAM_STEP)
        v_hat = nv / (1.0 - ADAM_B2 ** ADAM_STEP)
        d_ref[...] = -ADAM_LR * (m_hat / (jnp.sqrt(v_hat) + ADAM_EPS) + ADAM_WD * w_ref[...])
        nm_ref[...] = nm
        nv_ref[...] = nv

    sd = jax.ShapeDtypeStruct((R, C), F32)
    n_out = 4 if copy_g else 3
    return pl.pallas_call(
        body, name=name, grid=(R // tr,),
        in_specs=[_row(tr, C)] * 4, out_specs=[_row(tr, C)] * n_out, out_shape=[sd] * n_out,
        compiler_params=_cp("parallel"),
    )(w, g, m, v)


ANY = pl.BlockSpec(memory_space=pl.ANY)
HBM = pl.BlockSpec(memory_space=pltpu.HBM)
SEM = pl.BlockSpec(memory_space=pltpu.SEMAPHORE)
EFFECT = pltpu.SideEffectType.DATAFLOW_SIDE_EFFECTING


def _place():
    x, y, c = lax.axis_index("x"), lax.axis_index("y"), lax.axis_index("c")
    chips = [(1 - x, y), (x, 1 - y), (1 - x, 1 - y)]
    return x, y, c, chips


def _copy(src, dst, send, recv, k, to):
    return pltpu.make_async_remote_copy(src_ref=src, dst_ref=dst, send_sem=send.at[k], recv_sem=recv.at[k],
                                        device_id=to, device_id_type=MESH)


def xchg_start(name, bufs, plan, n, after=()):
    nb = len(bufs)

    na = len(after)

    def body(*refs):
        send, recv, token = refs[nb + na], refs[nb + na + 1], refs[-1]
        for k, (src, dst, to) in enumerate(plan(refs[:nb])):
            _copy(src, dst, send, recv, k, to).start()
        token[...] = jnp.zeros_like(token)

    outs = pl.pallas_call(
        body, name=name,
        out_shape=(pltpu.SemaphoreType.DMA((n,)), pltpu.SemaphoreType.DMA((n,)),
                   *[pltpu.HBM(b.shape, b.dtype) for b in bufs], jax.ShapeDtypeStruct((8, 128), F32)),
        in_specs=[HBM] * nb + [ANY] * na,
        out_specs=(SEM, SEM, *[HBM] * nb, pl.BlockSpec(memory_space=pltpu.VMEM)),
        input_output_aliases={i: 2 + i for i in range(nb)},
        compiler_params=pltpu.CompilerParams(has_side_effects=EFFECT),
    )(*[pltpu.with_memory_space_constraint(b, pltpu.HBM) for b in bufs], *after)
    return dict(name=name, send=outs[0], recv=outs[1], bufs=list(outs[2:2 + nb]), plan=plan), outs[-1]


def xchg_wait(flight, after):
    bufs, plan = flight["bufs"], flight["plan"]
    nb = len(bufs)

    def body(*refs):
        send, recv = refs[nb], refs[nb + 1]
        for k, (src, dst, to) in enumerate(plan(refs[:nb])):
            cp = _copy(src, dst, send, recv, k, to)
            cp.wait_send()
            cp.wait_recv()

    outs = pl.pallas_call(
        body, name=flight["name"] + "_wait",
        out_shape=tuple(pltpu.HBM(b.shape, b.dtype) for b in bufs),
        in_specs=[HBM] * nb + [SEM, SEM] + [ANY] * len(after),
        out_specs=tuple([HBM] * nb), input_output_aliases={i: i for i in range(nb)},
        compiler_params=pltpu.CompilerParams(has_side_effects=EFFECT),
    )(*bufs, flight["send"], flight["recv"], *after)
    return list(outs)


def _flip(k, x, y, c):
    return ((1 - x) if k & 4 else x, (1 - y) if k & 2 else y, (1 - c) if k & 1 else c)


class WeightGather:
    def __init__(self, shard, groups):
        me = 2 * lax.axis_index("x") + lax.axis_index("y")
        self.names = dict(groups)
        self.ici, self.d2d = {}, {}
        self.token = None
        for gname, names in groups:
            nt = len(names)
            srcs = [shard(n, self.token) for n in names]
            lands = [lax.dynamic_update_slice(lax.empty((4,) + s.shape, s.dtype), s[None], (me, 0, 0, 0))
                     for s in srcs]

            def plan(refs, nt=nt):
                x, y, c, chips = _place()
                return [(refs[t].at[c], refs[nt + t].at[2 * x + y, c], (cx, cy, c))
                        for t in range(nt) for cx, cy in chips]

            self.ici[gname], self.token = xchg_start(f"ag_ici_{gname}", srcs + lands, plan, 3 * nt,
                                                     after=[] if self.token is None else [self.token])

    def forward(self, gname, after):
        nt = len(self.names[gname])
        lands = xchg_wait(self.ici.pop(gname), after)[nt:]

        def plan(refs):
            x, y, c, chips = _place()
            out = []
            for t in range(nt):
                for cx, cy in chips:
                    piece = refs[t].at[2 * cx + cy, c]
                    out.append((piece, piece, (x, y, 1 - c)))
            return out

        self.d2d[gname], token = xchg_start(f"ag_d2d_{gname}", lands, plan, 3 * nt)
        return token

    def get(self, gname, after):
        lands = xchg_wait(self.d2d.pop(gname), after)
        return dict(zip(self.names[gname], lands))


class GradReduce:
    def __init__(self, kinds):
        self.J = {k: lax.empty((L, 2, a2, b), F32) for k, (L, a2, b) in kinds.items()}
        self.x, self.j = {}, {}

    @staticmethod
    def _where(name):
        kind, _, l = name.partition("_")
        return kind, int(l or 0)

    def send(self, gname, grads, after=()):
        names = list(grads)
        nt = len(names)
        gs = [grads[n] for n in names]
        xs = [lax.empty((7,) + g.shape[2:], g.dtype) for g in gs]

        def plan(refs):
            x, y, c, _ = _place()
            out = []
            for t in range(nt):
                for k in range(1, 8):
                    px, py, pc = _flip(k, x, y, c)
                    out.append((refs[t].at[2 * px + py, pc], refs[nt + t].at[k - 1], (px, py, pc)))
            return out

        flight, token = xchg_start(f"rs_x_{gname}", gs + xs, plan, 7 * nt, after=after)
        self.x[gname] = (names, flight)
        return token

    def reduce(self, gname, after):
        names, flight = self.x.pop(gname)
        nt = len(names)
        bufs = xchg_wait(flight, after)
        mine = jnp.stack([2 * lax.axis_index("x") + lax.axis_index("y"), lax.axis_index("c")]).astype(jnp.int32)
        where = [self._where(n) for n in names]
        js = [add8_into(self.J[kind], l, bufs[t], bufs[nt + t], mine, f"rs_add_{names[t]}")
              for t, (kind, l) in enumerate(where)]

        def plan(refs):
            x, y, c, _ = _place()
            out = []
            for t in range(nt):
                half = refs[t].at[where[t][1], c]
                out.append((half, half, (x, y, 1 - c)))
            return out

        flight, token = xchg_start(f"rs_join_{gname}", js, plan, nt)
        self.j[gname] = (where, flight)
        return token

    def finish(self, gname, after):
        where, flight = self.j.pop(gname)
        for (kind, _), j in zip(where, xchg_wait(flight, after)):
            self.J[kind] = j


def small_allreduce_start(v, after):
    me = 4 * lax.axis_index("x") + 2 * lax.axis_index("y") + lax.axis_index("c")
    land = lax.dynamic_update_slice(lax.empty((8,) + v.shape, v.dtype), v[None], (me, 0, 0))

    def plan(refs):
        x, y, c, _ = _place()
        return [(refs[0], refs[1].at[4 * x + 2 * y + c], _flip(k, x, y, c)) for k in range(1, 8)]

    return xchg_start("small_allreduce", [v, land], plan, 7, after=after)


def sum8(all8, name):
    def body(x_ref, o_ref):
        acc = x_ref[0]
        for d in range(1, 8):
            acc = acc + x_ref[d]
        o_ref[...] = acc

    return pl.pallas_call(
        body, name=name,
        in_specs=[pl.BlockSpec(memory_space=pltpu.VMEM)], out_specs=pl.BlockSpec(memory_space=pltpu.VMEM),
        out_shape=jax.ShapeDtypeStruct(all8.shape[1:], F32),
        compiler_params=pltpu.CompilerParams(vmem_limit_bytes=VMEM_LIMIT),
    )(all8)


AG_GROUPS = (("a0", ("pw1_0", "pw2_0", "small")), ("f0", ("up_0", "down_0")),
             ("l1", ("pw1_1", "pw2_1", "up_1", "down_1")), ("l2", ("kv", "wq_0", "wo_0", "up_2", "down_2")),
             ("l3", ("wq_1", "wo_1", "up_3", "down_3")))


def _bucket_table():
    qi = np.arange(BLK)[:, None]
    kj = np.arange(2 * BLK)[None, :]
    d = np.maximum(qi + BLK - kj, 0)
    max_exact = N_BUCKETS // 2
    log_ratio = (np.log(np.maximum(d, 1).astype(np.float32) / np.float32(max_exact))
                 / np.float32(math.log(MAX_DISTANCE / max_exact))).astype(np.float32)
    large = max_exact + (log_ratio * np.float32(N_BUCKETS - max_exact)).astype(np.int32)
    large = np.minimum(large, N_BUCKETS - 1)
    return np.where(d < max_exact, d, large).astype(np.int32)


def _heads_major(a, nh):
    T = a.shape[0]
    return a.reshape(T, nh, HD).transpose(1, 0, 2)


def _heads_minor(a):
    nh, T, _ = a.shape
    return a.transpose(1, 0, 2).reshape(T, nh * HD)


def _slots(land):
    return land.reshape(4, 2 * land.shape[2], land.shape[3])


def _rows(land):
    return land.reshape(8 * land.shape[2], land.shape[3])


def _gview(g):
    s, K, n = g.shape
    return g.reshape(4, 2, K // 2, n) if s == 4 else g.reshape(4, 2, K // 8, n)


def _gate(a, token):
    return a * (1.0 + token[0, 0])


def _conv_small(f_small):
    fs = f_small.transpose(1, 2, 0, 3).reshape(2, 40, D)
    b_pw1 = f_small[:, :, 35:37, :].transpose(1, 0, 2, 3).reshape(2, 1, 2 * D)
    rev = jnp.concatenate([fs[:, CONV_W - 1::-1], jnp.zeros((2, 40 - CONV_W, D), F32)], axis=1)
    return dict(conv=fs, conv_rev=rev, b_pw1=b_pw1, b_pw2=fs[:, 34:35])


def run_step(x, target, P, ag, rs):
    T = x.shape[0]
    zero = jnp.zeros((1, 1, D), F32)
    nm, nf = P["norm_mix"], P["norm_ffn"]
    ag.forward("a0", [ag.token])
    W = ag.get("a0", [])
    sm = _conv_small(W["small"])
    h = x
    saved = []
    for l in range(2):
        xn, u, a = norm_mm_glu(h, nm, l, _slots(W[f"pw1_{l}"]), sm["b_pw1"], f"f_pw1_{l}")
        y, s = dwconv_ln_silu(a, sm["conv"], l, f"f_conv_{l}")
        b2 = sm["b_pw2"]
        if l == 0:
            b2 = _gate(b2, ag.forward("f0", [s]))
        h1 = mm_bias_res(s, _rows(W[f"pw2_{l}"]), b2, l, h, f"f_pw2_{l}")
        if l == 0:
            W.update(ag.get("f0", [h1]))
        xn2, gu, f = norm_mm_swiglu(h1, nf, l, _slots(W[f"up_{l}"]), f"f_up_{l}")
        nxt = "l1" if l == 0 else "l2"
        h2 = mm_bias_res(f, _rows(W[f"down_{l}"]), _gate(zero, ag.forward(nxt, [f])), 0, h1, f"f_down_{l}")
        W.update(ag.get(nxt, [h2]))
        saved.append(dict(h=h, xn=xn, u=u, a=a, y=y, s=s, h1=h1, xn2=xn2, gu=gu, f=f))
        h = h2
    h_kv = h
    kvn, kv = norm_mm(h, P["norm_kv"], 0, _rows(W["kv"]), "f_kv")
    kp = jnp.pad(_heads_major(kv[:, :N_KV * HD], N_KV), ((0, 0), (BLK, 0), (0, 0)))
    vp = jnp.pad(_heads_major(kv[:, N_KV * HD:], N_KV), ((0, 0), (BLK, 0), (0, 0)))
    kvt = jnp.pad(kv.T.reshape(2, N_KV, HD, T), ((0, 0), (0, 0), (0, 0), (BLK, 0)))
    kt, vt = kvt[0], kvt[1]
    bucket = _bucket_table()
    onehot = jnp.asarray(np.eye(N_BUCKETS, dtype=np.float32)[bucket])
    bias = jnp.einsum("qkb,bh->hkq", onehot, P["rel_bias"], precision=lax.Precision.HIGHEST)
    bias = bias.reshape(N_KV, GROUP, 2 * BLK, BLK).transpose(0, 2, 1, 3).reshape(1, N_KV, 2 * BLK, QW)
    bias = bias + jnp.asarray(band_mask())[:, None]
    for j in range(2):
        l = 2 + j
        xn, q = norm_mm(h, nm, l, _rows(W[f"wq_{j}"]), f"f_q_{j}", scale=HD ** -0.5)
        qh = q.T.reshape(N_KV, GROUP, HD, T)
        sink = jnp.broadcast_to(P["sinks"][j].reshape(N_KV, GROUP, 1), (N_KV, GROUP, BLK)).reshape(N_KV, 1, QW)
        oh = attn_fwd(qh, kp, vt, bias, sink, f"f_attn_{j}")
        attn = oh.reshape(N_HEADS * HD, T).T
        h1 = mm_bias_res(attn, _rows(W[f"wo_{j}"]), zero, 0, h, f"f_wo_{j}")
        xn2, gu, f = norm_mm_swiglu(h1, nf, l, _slots(W[f"up_{l}"]), f"f_up_{l}")
        zg = _gate(zero, ag.forward("l3", [f])) if j == 0 else zero
        h2 = mm_bias_res(f, _rows(W[f"down_{l}"]), zg, 0, h1, f"f_down_{l}")
        if j == 0:
            W.update(ag.get("l3", [h2]))
        saved.append(dict(h=h, xn=xn, qh=qh, oh=oh, sink=sink, attn=attn, h1=h1, xn2=xn2, gu=gu, f=f))
        h = h2

    dh, st_final = final_loss(h, P["norm_final"], target, "loss_head")

    S = dict(norm_ffn=[None] * 4, norm_mix=[None] * 4, conv=[None] * 2, taps=[None] * 2, b_pw1=[None] * 2,
             b_pw2=[None] * 2, sinks=[None] * 2)

    def ffn_bwd(dh, sv, l, nf, after=()):
        du = mmT_swiglu_bwd(dh, _rows(W[f"down_{l}"]), sv["gu"], f"b_down_{l}", after)
        gd = mm_dw(sv["f"], dh, f"w_down_{l}", 512, 1)
        gu = mm_dw(sv["xn2"], du, f"w_up_{l}", DFF // 2, 4)
        dh, dg = mmT_rmsbwd(du, _slots(W[f"up_{l}"]), sv["h1"], nf, l, dh, f"b_up_{l}")
        S["norm_ffn"][l] = dg[0]
        return dh, {f"down_{l}": _gview(gd), f"up_{l}": _gview(gu)}

    dk = dv = dbias = None
    sent = []
    for j in (1, 0):
        l = 2 + j
        sv = saved[l]
        dh, grads = ffn_bwd(dh, sv, l, nf, sent)
        dattn = mmT(dh, _rows(W[f"wo_{j}"]), f"b_wo_{j}")
        grads[f"wo_{j}"] = _gview(mm_dw(sv["attn"], dh, f"w_wo_{j}", 512, 1))
        doh = dattn.T.reshape(N_KV, GROUP, HD, T)
        dqh, dkj, dvj, dbj, dsj = attn_bwd(sv["qh"], kp, kt, vp, bias, sv["sink"], sv["oh"], doh, f"b_attn_{j}")
        dq = dqh.reshape(N_HEADS * HD, T).T
        grads[f"wq_{j}"] = _gview(mm_dw(sv["xn"], dq, f"w_q_{j}", 512, 1))
        dh, dg = mmT_rmsbwd(dq, _rows(W[f"wq_{j}"])[None], sv["h"], nm, l, dh, f"b_q_{j}")
        S["norm_mix"][l] = dg[0]
        S["sinks"][j] = jnp.sum(dsj.reshape(N_HEADS, BLK), axis=1)
        dk = dkj if dk is None else dk + dkj
        dv = dvj if dv is None else dv + dvj
        dbias = dbj if dbias is None else dbias + dbj
        if j == 1:
            sent = [rs.send("l3", grads)]

    dkv = jnp.concatenate([_heads_minor(dk[:, BLK:]), _heads_minor(dv[:, BLK:])], axis=1).astype(BF16)
    grads["kv"] = _gview(mm_dw(kvn, dkv, "w_kv", 512, 1))
    dh, dg = mmT_rmsbwd(dkv, _rows(W["kv"])[None], h_kv, P["norm_kv"], 0, dh, "b_kv")
    S["norm_kv"] = dg[0]
    dbh = dbias.reshape(N_KV, 2 * BLK, GROUP, BLK)
    S["rel_bias"] = jnp.einsum("vkgq,qkb->bvg", dbh, onehot, precision=lax.Precision.HIGHEST).reshape(N_BUCKETS, N_HEADS)
    sent = [rs.send("l2", grads)]
    nf = _gate(nf, rs.reduce("l3", [dh]))

    for l in (1, 0):
        sv = saved[l]
        dh, grads = ffn_bwd(dh, sv, l, nf, sent)
        conv = sm["conv"]
        if l == 0:
            conv = _gate(conv, rs.send("f0", grads))
            grads = {}
        dy, st = mmT_lnbwd(dh, _rows(W[f"pw2_{l}"]), sv["y"], conv, l, f"b_pw2_{l}")
        g2, S["b_pw2"][l] = mm_dw(sv["s"], dh, f"w_pw2_{l}", 512, 1, colsum=True)
        du, dtaps = dwconv_glu_bwd(dy, sv["a"], sv["u"], sm["conv"], sm["conv_rev"], l, f"b_conv_{l}")
        S["conv"][l] = st[0:3]
        S["taps"][l] = dtaps[0:CONV_W]
        if l == 0:
            rs.finish("l2", [du])
            nm = _gate(nm, rs.reduce("l1", [du]))
        g1, S["b_pw1"][l] = mm_dw(sv["xn"], du, f"w_pw1_{l}", 512, 4, colsum=True)
        grads[f"pw2_{l}"], grads[f"pw1_{l}"] = _gview(g2), _gview(g1)
        dh, dg = mmT_rmsbwd(du, _slots(W[f"pw1_{l}"]), sv["h"], nm, l, dh, f"b_pw1_{l}")
        S["norm_mix"][l] = dg[0]
        if l == 1:
            sent = [rs.send("l1", grads)]
            rs.finish("l3", [dh])
            nf = _gate(nf, rs.reduce("l2", [dh]))
    S["norm_final"] = st_final[0]
    S["loss"] = st_final[1]
    return grads, dh, S


R_CONV = 37
R_SMALL = 88


def _pack_small(S):
    rows = []
    for l in range(2):
        rows += [S["taps"][l], S["conv"][l][2:3], S["conv"][l][0:2], S["b_pw2"][l], S["b_pw1"][l].reshape(2, D)]
    rows += [jnp.stack(S["norm_mix"]), jnp.stack(S["norm_ffn"]), S["norm_kv"][None], S["norm_final"][None]]
    tail = jnp.concatenate([jnp.stack(S["sinks"]).reshape(-1), S["rel_bias"].reshape(-1)])
    rows += [jnp.pad(tail, (0, D - tail.shape[0]))[None], S["loss"][None]]
    v = jnp.concatenate(rows, axis=0)
    return jnp.pad(v, ((0, R_SMALL - v.shape[0]), (0, 0)))


def kernel(x, norm_mix, norm_ffn, conv_w_pw1, conv_b_pw1, conv_w_dw, conv_b_dw, conv_ln_g, conv_ln_b, conv_w_pw2, conv_b_pw2, norm_kv, w_kv, w_q, w_o, sinks, rel_bias, ffn_w_up, ffn_w_down, norm_final, loss_target, m_norm_mix, m_norm_ffn, m_conv_w_pw1, m_conv_b_pw1, m_conv_w_dw, m_conv_b_dw, m_conv_ln_g, m_conv_ln_b, m_conv_w_pw2, m_conv_b_pw2, m_norm_kv, m_w_kv, m_w_q, m_w_o, m_sinks, m_rel_bias, m_ffn_w_up, m_ffn_w_down, m_norm_final, v_norm_mix, v_norm_ffn, v_conv_w_pw1, v_conv_b_pw1, v_conv_w_dw, v_conv_b_dw, v_conv_ln_g, v_conv_ln_b, v_conv_w_pw2, v_conv_b_pw2, v_norm_kv, v_w_kv, v_w_q, v_w_o, v_sinks, v_rel_bias, v_ffn_w_up, v_ffn_w_down, v_norm_final):
    me = 2 * lax.axis_index("x") + lax.axis_index("y")
    weights = dict(norm_mix=norm_mix, norm_ffn=norm_ffn, conv_w_pw1=conv_w_pw1, conv_b_pw1=conv_b_pw1,
                   conv_w_dw=conv_w_dw, conv_b_dw=conv_b_dw, conv_ln_g=conv_ln_g, conv_ln_b=conv_ln_b,
                   conv_w_pw2=conv_w_pw2, conv_b_pw2=conv_b_pw2, norm_kv=norm_kv, w_kv=w_kv, w_q=w_q, w_o=w_o,
                   sinks=sinks, rel_bias=rel_bias, ffn_w_up=ffn_w_up, ffn_w_down=ffn_w_down, norm_final=norm_final)
    mom_m = dict(norm_mix=m_norm_mix, norm_ffn=m_norm_ffn, conv_w_pw1=m_conv_w_pw1, conv_b_pw1=m_conv_b_pw1,
                 conv_w_dw=m_conv_w_dw, conv_b_dw=m_conv_b_dw, conv_ln_g=m_conv_ln_g, conv_ln_b=m_conv_ln_b,
                 conv_w_pw2=m_conv_w_pw2, conv_b_pw2=m_conv_b_pw2, norm_kv=m_norm_kv, w_kv=m_w_kv, w_q=m_w_q,
                 w_o=m_w_o, sinks=m_sinks, rel_bias=m_rel_bias, ffn_w_up=m_ffn_w_up, ffn_w_down=m_ffn_w_down,
                 norm_final=m_norm_final)
    mom_v = dict(norm_mix=v_norm_mix, norm_ffn=v_norm_ffn, conv_w_pw1=v_conv_w_pw1, conv_b_pw1=v_conv_b_pw1,
                 conv_w_dw=v_conv_w_dw, conv_b_dw=v_conv_b_dw, conv_ln_g=v_conv_ln_g, conv_ln_b=v_conv_ln_b,
                 conv_w_pw2=v_conv_w_pw2, conv_b_pw2=v_conv_b_pw2, norm_kv=v_norm_kv, w_kv=v_w_kv, w_q=v_w_q,
                 w_o=v_w_o, sinks=v_sinks, rel_bias=v_rel_bias, ffn_w_up=v_ffn_w_up, ffn_w_down=v_ffn_w_down,
                 norm_final=v_norm_final)

    big = {"conv_w_pw1": "pw1", "conv_w_pw2": "pw2", "w_q": "wq", "w_o": "wo", "ffn_w_up": "up",
           "ffn_w_down": "down", "w_kv": "kv"}
    of_kind = {k: n for n, k in big.items()}

    def shard(name, token):
        if name == "small":
            a = jnp.concatenate(
                [conv_w_dw, conv_b_dw[:, None], conv_ln_g[:, None], conv_ln_b[:, None], conv_b_pw2[:, None],
                 conv_b_pw1.reshape(2, 2, 256), jnp.zeros((2, 3, 256), F32)], axis=1)
            return a if token is None else _gate(a, token)
        kind, _, l = name.partition("_")
        a = weights[of_kind[kind]]
        a = a[int(l)] if l else a
        if token is not None:
            a = _gate(a, token)
        return a.astype(BF16).reshape(2, a.shape[0] // 2, a.shape[1])

    ag = WeightGather(shard, AG_GROUPS)
    rs = GradReduce({"pw1": (2, 512, 512), "pw2": (2, 128, D), "wq": (2, 128, D), "wo": (2, 128, D),
                     "up": (4, 512, DFF // 2), "down": (4, DFF // 8, D), "kv": (1, 128, 512)})

    P = dict(norm_mix=norm_mix[:, None], norm_ffn=norm_ffn[:, None], norm_kv=norm_kv[None, None],
             norm_final=norm_final[None], sinks=sinks, rel_bias=rel_bias)
    last, grad_x, S = run_step(x[0], loss_target[0], P, ag, rs)

    rs.finish("l1", [grad_x])
    small_flight, token = small_allreduce_start(_gate(_pack_small(S), rs.reduce("f0", [grad_x])), [])
    token = rs.send("c0", last, after=[token])
    delta, new_m, new_v, big_grads = {}, {}, {}, {}

    def update(n):
        shp = weights[n].shape
        r2 = (int(np.prod(shp[:-1])), shp[-1])
        g, d, nm, nv = adamw(weights[n].reshape(r2), rs.J[big[n]].reshape(r2), mom_m[n].reshape(r2),
                             mom_v[n].reshape(r2), f"adamw_{n}", copy_g=True)
        big_grads[n], delta[n], new_m[n], new_v[n] = g.reshape(shp), d.reshape(shp), nm.reshape(shp), nv.reshape(shp)

    rs.finish("f0", [token])
    for n in ("ffn_w_up", "ffn_w_down"):
        update(n)
    vsum = sum8(xchg_wait(small_flight, [delta["ffn_w_up"], delta["ffn_w_down"]])[1], "small_sum")

    col = lambda a: lax.dynamic_slice_in_dim(a, me * 256, 256, axis=-1)
    grads = {}
    for l in range(2):
        base = l * R_CONV
        grads.setdefault("conv_w_dw", []).append(col(vsum[base:base + 31]))
        grads.setdefault("conv_b_dw", []).append(col(vsum[base + 31]))
        grads.setdefault("conv_ln_g", []).append(col(vsum[base + 32]))
        grads.setdefault("conv_ln_b", []).append(col(vsum[base + 33]))
        grads.setdefault("conv_b_pw2", []).append(col(vsum[base + 34]))
        grads.setdefault("conv_b_pw1", []).append(
            lax.dynamic_slice_in_dim(vsum[base + 35:base + 37].reshape(2 * D), me * 512, 512, axis=0))
    grads = {k: jnp.stack(v) for k, v in grads.items()}
    base = 2 * R_CONV
    grads["norm_mix"] = vsum[base:base + 4]
    grads["norm_ffn"] = vsum[base + 4:base + 8]
    grads["norm_kv"] = vsum[base + 8]
    grads["norm_final"] = vsum[base + 9]
    grads["sinks"] = vsum[base + 10, 0:32].reshape(2, 16)
    grads["rel_bias"] = vsum[base + 10, 32:32 + 512].reshape(32, 16)
    loss = vsum[base + 11, 0]

    for n in weights:
        if n not in big:
            shp = weights[n].shape
            r2 = (int(np.prod(shp[:-1])), shp[-1])
            d, nm, nv = adamw(weights[n].reshape(r2), grads[n].reshape(r2), mom_m[n].reshape(r2),
                              mom_v[n].reshape(r2), f"adamw_{n}")
            delta[n], new_m[n], new_v[n] = d.reshape(shp), nm.reshape(shp), nv.reshape(shp)

    rs.reduce("c0", [vsum])
    for n in ("w_q", "w_o", "w_kv"):
        update(n)
    rs.finish("c0", [delta["w_kv"]])
    for n in ("conv_w_pw1", "conv_w_pw2"):
        update(n)
    grads.update(big_grads)

    order = list(weights)
    return (loss, grad_x[None], *[grads[n] for n in order], *[delta[n] for n in order],
            *[new_m[n] for n in order], *[new_v[n] for n in order])
```

```python
import functools
import math

import numpy as np
import jax
import jax.numpy as jnp
from jax import lax
from jax.experimental import pallas as pl
from jax.experimental.pallas import tpu as pltpu

F32 = jnp.float32
BF16 = jnp.bfloat16
MESH = pl.DeviceIdType.MESH

D = 1024
DFF = 2816
N_HEADS = 16
N_KV = 4
GROUP = 4
HD = 64
BLK = 128
CONV_W = 31
HALO = 32
N_BUCKETS = 32
MAX_DISTANCE = 128
EPS = 1e-6
NEG_INF = -1e30
TM = 512
TCV = 256
VMEM_LIMIT = 56 * 2 ** 20

ADAM_LR, ADAM_B1, ADAM_B2, ADAM_EPS, ADAM_WD, ADAM_STEP = 0.001, 0.9, 0.999, 1e-08, 0.01, 10


def _cp(*sem):
    return pltpu.CompilerParams(dimension_semantics=sem, vmem_limit_bytes=VMEM_LIMIT)


def _sigmoid(x):
    return 1.0 / (1.0 + jnp.exp(-x))


def _row(tm, n):
    return pl.BlockSpec((tm, n), lambda i: (i, 0))


def _const(shape):
    nd = len(shape)
    return pl.BlockSpec(shape, lambda i: (0,) * nd)


def _weight(shape):
    nd = len(shape)
    return pl.BlockSpec(shape, lambda i: (0,) * nd, pipeline_mode=pl.Buffered(1))


def _layer(shape, l):
    nd = len(shape)
    return pl.BlockSpec((None,) + tuple(shape), lambda i: (l,) + (0,) * nd)


def _dot(a, b):
    return jnp.dot(a, b, preferred_element_type=F32)


def _dot_nt(a, b):
    return lax.dot_general(a, b, (((1,), (1,)), ((), ())), preferred_element_type=F32)


def _dot_tn(a, b):
    return lax.dot_general(a, b, (((0,), (0,)), ((), ())), preferred_element_type=F32)


def _rms(x):
    return lax.rsqrt(jnp.mean(x * x, axis=-1, keepdims=True) + EPS)


def norm_mm_glu(h, g, l, w, b, name):
    T = h.shape[0]
    ns = w.shape[-1]

    def body(h_ref, g_ref, w_ref, b_ref, xn_ref, u_ref, a_ref):
        x = h_ref[...]
        xn = (x * _rms(x) * g_ref[...]).astype(BF16)
        xn_ref[...] = xn
        for s in range(2):
            lo, hi = s * ns, (s + 1) * ns
            u1 = _dot(xn, w_ref[s]) + b_ref[:, lo:hi]
            u2 = _dot(xn, w_ref[2 + s]) + b_ref[:, D + lo:D + hi]
            u_ref[:, lo:hi] = u1.astype(BF16)
            u_ref[:, D + lo:D + hi] = u2.astype(BF16)
            a_ref[:, lo:hi] = (u1 * _sigmoid(u2)).astype(BF16)

    return pl.pallas_call(
        body, name=name, grid=(T // TM,),
        in_specs=[_row(TM, D), _layer((1, D), l), _weight((4, D, ns)), _layer((1, 2 * D), l)],
        out_specs=[_row(TM, D), _row(TM, 2 * D), _row(TM, D)],
        out_shape=[jax.ShapeDtypeStruct((T, D), BF16), jax.ShapeDtypeStruct((T, 2 * D), BF16),
                   jax.ShapeDtypeStruct((T, D), BF16)],
        compiler_params=_cp("parallel"),
    )(h, g, w, b)


SUB = 8


def _make_shifts(sh):
    n = TCV + HALO - SUB
    for r in range(1, SUB):
        for r0 in range(0, n, 40):
            sh[r, r0:r0 + 40, :] = sh[0, pl.ds(r + r0, 40), :]


def _shifted(sh, off, rows, cols):
    return sh[off % SUB, pl.ds(off - off % SUB, rows), cols]


def _conv_taps(sh, w_ref, out_ref, first):
    RB, LB = 32, 512
    for r0 in range(0, TCV, RB):
        for c0 in range(0, out_ref.shape[1], LB):
            acc = jnp.zeros((RB, LB), F32)
            for k in range(CONV_W):
                acc = acc + w_ref[k:k + 1, c0:c0 + LB] * _shifted(sh, first + k + r0, RB, slice(c0, c0 + LB))
            out_ref[r0:r0 + RB, c0:c0 + LB] = acc


def dwconv_ln_silu(a, sm, l, name):
    T = a.shape[0]
    nb = TCV // HALO

    def body(cur_ref, prev_ref, sm_ref, y_ref, s_ref, sh, yb):
        i = pl.program_id(0)
        sh[0, 0:HALO, :] = jnp.where(i > 0, prev_ref[...].astype(F32), 0.0)
        sh[0, HALO:HALO + TCV, :] = cur_ref[...].astype(F32)
        _make_shifts(sh)
        _conv_taps(sh, sm_ref, yb, HALO - (CONV_W - 1))
        y = yb[...] + sm_ref[31:32, :]
        y_ref[...] = y.astype(BF16)
        mu = jnp.mean(y, axis=-1, keepdims=True)
        yc = y - mu
        rstd = lax.rsqrt(jnp.mean(yc * yc, axis=-1, keepdims=True) + EPS)
        z = yc * rstd * sm_ref[32:33, :] + sm_ref[33:34, :]
        s_ref[...] = (z * _sigmoid(z)).astype(BF16)

    return pl.pallas_call(
        body, name=name, grid=(T // TCV,),
        in_specs=[_row(TCV, D), pl.BlockSpec((HALO, D), lambda i: (jnp.maximum(i * nb - 1, 0), 0)),
                  _layer((40, D), l)],
        out_specs=[_row(TCV, D), _row(TCV, D)],
        out_shape=[jax.ShapeDtypeStruct((T, D), BF16), jax.ShapeDtypeStruct((T, D), BF16)],
        scratch_shapes=[pltpu.VMEM((SUB, TCV + HALO, D), F32), pltpu.VMEM((TCV, D), F32)],
        compiler_params=_cp("parallel"),
    )(a, a, sm)


def mm_bias_res(xb, w, b, bl, res, name):
    T, K = xb.shape

    def body(x_ref, w_ref, b_ref, r_ref, o_ref):
        o_ref[...] = _dot(x_ref[...], w_ref[...]) + b_ref[...] + r_ref[...]

    return pl.pallas_call(
        body, name=name, grid=(T // TM,),
        in_specs=[_row(TM, K), _weight((K, D)), _layer((1, D), bl), _row(TM, D)],
        out_specs=_row(TM, D), out_shape=jax.ShapeDtypeStruct((T, D), F32),
        compiler_params=_cp("parallel"),
    )(xb, w, b, res)


def norm_mm_swiglu(h, g, l, w, name):
    T = h.shape[0]
    ns = w.shape[-1]

    def body(h_ref, g_ref, w_ref, xn_ref, gu_ref, f_ref):
        x = h_ref[...]
        xn = (x * _rms(x) * g_ref[...]).astype(BF16)
        xn_ref[...] = xn
        for s in range(2):
            lo, hi = s * ns, (s + 1) * ns
            gate = _dot(xn, w_ref[s])
            up = _dot(xn, w_ref[2 + s])
            gu_ref[:, lo:hi] = gate.astype(BF16)
            gu_ref[:, DFF + lo:DFF + hi] = up.astype(BF16)
            f_ref[:, lo:hi] = (gate * _sigmoid(gate) * up).astype(BF16)

    return pl.pallas_call(
        body, name=name, grid=(T // TM,),
        in_specs=[_row(TM, D), _layer((1, D), l), _weight((4, D, ns))],
        out_specs=[_row(TM, D), _row(TM, 2 * DFF), _row(TM, DFF)],
        out_shape=[jax.ShapeDtypeStruct((T, D), BF16), jax.ShapeDtypeStruct((T, 2 * DFF), BF16),
                   jax.ShapeDtypeStruct((T, DFF), BF16)],
        compiler_params=_cp("parallel"),
    )(h, g, w)


def norm_mm(h, g, gl, w, name, scale=1.0):
    T = h.shape[0]
    N = w.shape[-1]

    def body(h_ref, g_ref, w_ref, xn_ref, o_ref):
        x = h_ref[...]
        xn = (x * _rms(x) * g_ref[...]).astype(BF16)
        xn_ref[...] = xn
        o_ref[...] = (_dot(xn, w_ref[...]) * scale).astype(BF16)

    return pl.pallas_call(
        body, name=name, grid=(T // TM,),
        in_specs=[_row(TM, D), _layer((1, D), gl), _weight((D, N))],
        out_specs=[_row(TM, D), _row(TM, N)],
        out_shape=[jax.ShapeDtypeStruct((T, D), BF16), jax.ShapeDtypeStruct((T, N), BF16)],
        compiler_params=_cp("parallel"),
    )(h, g, w)


QB = 8
QW = GROUP * BLK


def band_mask():
    qi = np.arange(QW)[None, :] % BLK
    kj = np.arange(2 * BLK)[:, None]
    band = ((kj < BLK) & (kj > qi)) | ((kj >= BLK) & (kj - BLK <= qi))
    first = band & (kj >= BLK)
    return np.where(np.stack([first, band]), 0.0, NEG_INF).astype(np.float32)


def _softmax_cols(s, sink):
    m = jnp.maximum(jnp.max(s, axis=0, keepdims=True), sink)
    p = jnp.exp(s - m)
    es = jnp.exp(sink - m)
    inv = 1.0 / (jnp.sum(p, axis=0, keepdims=True) + es)
    return p, inv, es


def _attn_specs(T):
    W = QB * BLK
    qspec = pl.BlockSpec((None, GROUP, HD, W), lambda kv, n: (kv, 0, 0, n))
    kspec = pl.BlockSpec((None, T + BLK, HD), lambda kv, n: (kv, 0, 0))
    ktspec = [pl.BlockSpec((None, HD, W), lambda kv, n: (kv, 0, n)),
              pl.BlockSpec((None, HD, BLK), lambda kv, n: (kv, 0, (n + 1) * QB))]
    bspec = pl.BlockSpec((2, None, 2 * BLK, QW), lambda kv, n: (0, kv, 0, 0))
    sspec = pl.BlockSpec((None, 1, QW), lambda kv, n: (kv, 0, 0))
    return qspec, kspec, ktspec, bspec, sspec


def _attn_block(n, b):
    blk = n * QB + b
    rows = pl.ds(pl.multiple_of(blk * BLK, BLK), 2 * BLK)
    return rows, (jnp.minimum(blk, 1) if b == 0 else 1)


def _band_cols(main_ref, tail_ref, b):
    if b < QB - 1:
        return main_ref[:, b * BLK:(b + 2) * BLK]
    return jnp.concatenate([main_ref[:, b * BLK:], tail_ref[...]], axis=1)


def _heads_side_by_side(ref, qs):
    return jnp.concatenate([ref[g, :, qs] for g in range(GROUP)], axis=1)


def attn_fwd(q, kp, vt, bias, sink, name):
    T = q.shape[3]
    qspec, kspec, ktspec, bspec, sspec = _attn_specs(T)

    def body(q_ref, k_ref, vt_ref, vtt_ref, b_ref, s_ref, o_ref, pb):
        n = pl.program_id(1)

        def scores(b):
            return _dot(k_ref[_attn_block(n, b)[0], :], _heads_side_by_side(q_ref, slice(b * BLK, (b + 1) * BLK)))

        st_next = scores(0)
        for b in range(QB):
            rows, table = _attn_block(n, b)
            qs = slice(b * BLK, (b + 1) * BLK)
            st = st_next
            if b + 1 < QB:
                st_next = scores(b + 1)
            for g in range(GROUP):
                hs = slice(g * BLK, (g + 1) * BLK)
                p, inv, _ = _softmax_cols(st[:, hs] + b_ref[table, :, hs], s_ref[:, hs])
                pb[:, hs] = (p * inv).astype(BF16)
            ot = _dot(_band_cols(vt_ref, vtt_ref, b), pb[...])
            for g in range(GROUP):
                o_ref[g, :, qs] = ot[:, g * BLK:(g + 1) * BLK].astype(BF16)

    return pl.pallas_call(
        body, name=name, grid=(N_KV, T // (QB * BLK)),
        in_specs=[qspec, kspec, *ktspec, bspec, sspec], out_specs=qspec,
        out_shape=jax.ShapeDtypeStruct((N_KV, GROUP, HD, T), BF16),
        scratch_shapes=[pltpu.VMEM((2 * BLK, QW), BF16)],
        compiler_params=_cp("parallel", "parallel"),
    )(q, kp, vt, vt, bias, sink)


def attn_bwd(q, kp, kt, vp, bias, sink, o, do, name):
    T = q.shape[3]
    qspec, kspec, ktspec, bspec, sspec = _attn_specs(T)

    def body(q_ref, k_ref, kt_ref, ktt_ref, v_ref, b_ref, s_ref, o_ref, do_ref,
             dq_ref, dk_ref, dv_ref, db_ref, ds_ref, pb, dsb):
        n = pl.program_id(1)

        @pl.when(n == 0)
        def _():
            dk_ref[...] = jnp.zeros_like(dk_ref)
            dv_ref[...] = jnp.zeros_like(dv_ref)
            db_ref[...] = jnp.zeros_like(db_ref)
            ds_ref[...] = jnp.zeros_like(ds_ref)

        def products(b):
            rows = _attn_block(n, b)[0]
            qs = slice(b * BLK, (b + 1) * BLK)
            q4, do4 = _heads_side_by_side(q_ref, qs), _heads_side_by_side(do_ref, qs)
            return q4, do4, _dot(k_ref[rows, :], q4), _dot(v_ref[rows, :], do4)

        ahead = products(0)
        for b in range(QB):
            rows, table = _attn_block(n, b)
            qs = slice(b * BLK, (b + 1) * BLK)
            q4, do4, st, dpt = ahead
            if b + 1 < QB:
                ahead = products(b + 1)
            for g in range(GROUP):
                hs = slice(g * BLK, (g + 1) * BLK)
                p, inv, es = _softmax_cols(st[:, hs] + b_ref[table, :, hs], s_ref[:, hs])
                probs = p * inv
                delta = jnp.sum(do_ref[g, :, qs].astype(F32) * o_ref[g, :, qs].astype(F32), axis=0, keepdims=True)
                dS = probs * (dpt[:, hs] - delta)
                ds_ref[:, hs] += -(es * inv) * delta
                db_ref[:, hs] += dS
                pb[:, hs] = probs.astype(BF16)
                dsb[:, hs] = dS.astype(BF16)
            dqt = _dot(_band_cols(kt_ref, ktt_ref, b), dsb[...]) * (HD ** -0.5)
            for g in range(GROUP):
                dq_ref[g, :, qs] = dqt[:, g * BLK:(g + 1) * BLK].astype(BF16)
            dk_ref[rows, :] += _dot_nt(dsb[...], q4)
            dv_ref[rows, :] += _dot_nt(pb[...], do4)

    kout = pl.BlockSpec((None, T + BLK, HD), lambda kv, n: (kv, 0, 0))
    dbspec = pl.BlockSpec((None, 2 * BLK, QW), lambda kv, n: (kv, 0, 0))
    return pl.pallas_call(
        body, name=name, grid=(N_KV, T // (QB * BLK)),
        in_specs=[qspec, kspec, *ktspec, kspec, bspec, sspec, qspec, qspec],
        out_specs=[qspec, kout, kout, dbspec, sspec],
        out_shape=[jax.ShapeDtypeStruct((N_KV, GROUP, HD, T), BF16),
                   jax.ShapeDtypeStruct((N_KV, T + BLK, HD), F32), jax.ShapeDtypeStruct((N_KV, T + BLK, HD), F32),
                   jax.ShapeDtypeStruct((N_KV, 2 * BLK, QW), F32), jax.ShapeDtypeStruct((N_KV, 1, QW), F32)],
        scratch_shapes=[pltpu.VMEM((2 * BLK, QW), BF16), pltpu.VMEM((2 * BLK, QW), BF16)],
        compiler_params=_cp("parallel", "arbitrary"),
    )(q, kp, kt, kt, vp, bias, sink, o, do)


def final_loss(h, g, target, name):
    T = h.shape[0]

    def body(h_ref, g_ref, t_ref, dh_ref, st_ref):
        i = pl.program_id(0)

        @pl.when(i == 0)
        def _():
            st_ref[...] = jnp.zeros_like(st_ref)

        x = h_ref[...]
        r = _rms(x)
        xh = x * r
        e = xh * g_ref[...] - t_ref[...]
        loss = 0.5 * jnp.sum(jnp.mean(e * e, axis=-1, keepdims=True))
        dy = e * (1.0 / D)
        st_ref[0:1, :] += jnp.sum(dy * xh, axis=0, keepdims=True)
        lane = lax.broadcasted_iota(jnp.int32, (1, D), 1)
        st_ref[1:2, :] += jnp.where(lane == 0, loss, 0.0)
        dxh = dy * g_ref[...]
        dh_ref[...] = r * (dxh - xh * jnp.mean(dxh * xh, axis=-1, keepdims=True))

    return pl.pallas_call(
        body, name=name, grid=(T // TM,),
        in_specs=[_row(TM, D), _const((1, D)), _row(TM, D)],
        out_specs=[_row(TM, D), _const((8, D))],
        out_shape=[jax.ShapeDtypeStruct((T, D), F32), jax.ShapeDtypeStruct((8, D), F32)],
        compiler_params=_cp("arbitrary"),
    )(h, g, target)


def mm_dw(x, dy, name, tn, slots, colsum=False):
    T, K = x.shape
    split = dy.ndim == 3
    N = dy.shape[-1] * (2 if split else 1)
    tt = min(T, 2048 if K <= 1024 else 1024)
    nt = T // tt
    ns = N // slots
    per = ns // tn

    def body(x_ref, dy_ref, *rest):
        if colsum:
            dw_ref, cs_ref, acc, cacc = rest
        else:
            dw_ref, acc = rest
        t = pl.program_id(1)

        @pl.when(t == 0)
        def _():
            acc[...] = jnp.zeros_like(acc)
            if colsum:
                cacc[...] = jnp.zeros_like(cacc)

        dyv = dy_ref[...]
        acc[...] += _dot_tn(x_ref[...].astype(BF16), dyv.astype(BF16))
        if colsum:
            cacc[...] += jnp.sum(dyv.astype(F32), axis=0, keepdims=True)

        @pl.when(t == nt - 1)
        def _():
            dw_ref[...] = acc[...].astype(BF16)
            if colsum:
                cs_ref[...] = cacc[...]

    if split:
        half = N // 2 // tn
        dy_spec = pl.BlockSpec((None, tt, tn), lambda j, t: (j // half, t, j % half))
    else:
        dy_spec = pl.BlockSpec((tt, tn), lambda j, t: (t, j))
    out_specs = [pl.BlockSpec((None, K, tn), lambda j, t: (j // per, 0, j % per))]
    out_shape = [jax.ShapeDtypeStruct((slots, K, ns), BF16)]
    scratch = [pltpu.VMEM((K, tn), F32)]
    if colsum:
        out_specs.append(pl.BlockSpec((1, tn), lambda j, t: (0, j)))
        out_shape.append(jax.ShapeDtypeStruct((1, N), F32))
        scratch.append(pltpu.VMEM((1, tn), F32))
    res = pl.pallas_call(
        body, name=name, grid=(N // tn, nt),
        in_specs=[pl.BlockSpec((tt, K), lambda j, t: (t, 0)), dy_spec],
        out_specs=out_specs, out_shape=out_shape, scratch_shapes=scratch,
        compiler_params=_cp("parallel", "arbitrary"),
    )(x, dy)
    return tuple(res) if colsum else res[0]


def mmT_swiglu_bwd(dh, w, gu, name, after=()):
    T = dh.shape[0]
    cw = 256

    def body(dh_ref, w_ref, gu_ref, *rest):
        du_ref = rest[-1]
        dhb = dh_ref[...].astype(BF16)
        ahead = _dot_nt(dhb, w_ref[0:cw, :])
        for lo in range(0, DFF, cw):
            hi = lo + cw
            df = ahead
            if hi < DFF:
                ahead = _dot_nt(dhb, w_ref[hi:hi + cw, :])
            gate = gu_ref[:, lo:hi].astype(F32)
            up = gu_ref[:, DFF + lo:DFF + hi].astype(F32)
            sg = _sigmoid(gate)
            silu = gate * sg
            du_ref[:, lo:hi] = (df * (up * (sg + silu * (1.0 - sg)))).astype(BF16)
            du_ref[:, DFF + lo:DFF + hi] = (df * silu).astype(BF16)

    return pl.pallas_call(
        body, name=name, grid=(T // TM,),
        in_specs=[_row(TM, D), _weight((DFF, D)), _row(TM, 2 * DFF)] + [ANY] * len(after),
        out_specs=_row(TM, 2 * DFF), out_shape=jax.ShapeDtypeStruct((T, 2 * DFF), BF16),
        compiler_params=_cp("parallel"),
    )(dh, w, gu, *after)


def mmT_rmsbwd(du, w, h, g, gl, dh_in, name):
    split = du.ndim == 3
    T = du.shape[-2]
    N = du.shape[-1] * (2 if split else 1)
    slots = w.shape[0]
    ns = N // slots

    RH = TM // 2

    def piece(du_ref, s, rows):
        if split:
            per = slots // 2
            return du_ref[s // per, rows, (s % per) * ns:(s % per + 1) * ns]
        return du_ref[rows, s * ns:(s + 1) * ns]

    def body(du_ref, w_ref, h_ref, g_ref, di_ref, dh_ref, dg_ref):
        i = pl.program_id(0)

        @pl.when(i == 0)
        def _():
            dg_ref[...] = jnp.zeros_like(dg_ref)

        def products(k):
            rows = slice(k * RH, (k + 1) * RH)
            dxn = _dot_nt(piece(du_ref, 0, rows), w_ref[0])
            for s in range(1, slots):
                dxn = dxn + _dot_nt(piece(du_ref, s, rows), w_ref[s])
            return dxn

        ahead = products(0)
        for k in range(TM // RH):
            rows = slice(k * RH, (k + 1) * RH)
            dxn = ahead
            if (k + 1) * RH < TM:
                ahead = products(k + 1)
            x = h_ref[rows, :]
            r = _rms(x)
            xh = x * r
            dg_ref[0:1, :] += jnp.sum(dxn * xh, axis=0, keepdims=True)
            dxh = dxn * g_ref[...]
            dh_ref[rows, :] = di_ref[rows, :] + r * (dxh - xh * jnp.mean(dxh * xh, axis=-1, keepdims=True))

    return pl.pallas_call(
        body, name=name, grid=(T // TM,),
        in_specs=[pl.BlockSpec((2, TM, N // 2), lambda i: (0, i, 0)) if split else _row(TM, N),
                  _weight((slots, D, ns)), _row(TM, D), _layer((1, D), gl), _row(TM, D)],
        out_specs=[_row(TM, D), _const((8, D))],
        out_shape=[jax.ShapeDtypeStruct((T, D), F32), jax.ShapeDtypeStruct((8, D), F32)],
        compiler_params=_cp("arbitrary"),
    )(du, w, h, g, dh_in)


def mmT(dh, w, name):
    T = dh.shape[0]
    N = w.shape[0]

    def body(dh_ref, w_ref, o_ref):
        o_ref[...] = _dot_nt(dh_ref[...].astype(BF16), w_ref[...]).astype(BF16)

    return pl.pallas_call(
        body, name=name, grid=(T // TM,),
        in_specs=[_row(TM, D), _weight((N, D))],
        out_specs=_row(TM, N), out_shape=jax.ShapeDtypeStruct((T, N), BF16),
        compiler_params=_cp("parallel"),
    )(dh, w)


def mmT_lnbwd(dh, w, y, sm, l, name):
    T = dh.shape[0]

    def body(dh_ref, w_ref, y_ref, sm_ref, dy_ref, st_ref):
        i = pl.program_id(0)

        @pl.when(i == 0)
        def _():
            st_ref[...] = jnp.zeros_like(st_ref)

        ds = _dot_nt(dh_ref[...].astype(BF16), w_ref[...])
        y = y_ref[...].astype(F32)
        mu = jnp.mean(y, axis=-1, keepdims=True)
        yc = y - mu
        rstd = lax.rsqrt(jnp.mean(yc * yc, axis=-1, keepdims=True) + EPS)
        xh = yc * rstd
        gam = sm_ref[32:33, :]
        z = xh * gam + sm_ref[33:34, :]
        sg = _sigmoid(z)
        dz = ds * sg * (1.0 + z * (1.0 - sg))
        st_ref[0:1, :] += jnp.sum(dz * xh, axis=0, keepdims=True)
        st_ref[1:2, :] += jnp.sum(dz, axis=0, keepdims=True)
        dxh = dz * gam
        dy = rstd * (dxh - jnp.mean(dxh, axis=-1, keepdims=True) - xh * jnp.mean(dxh * xh, axis=-1, keepdims=True))
        st_ref[2:3, :] += jnp.sum(dy, axis=0, keepdims=True)
        dy_ref[...] = dy.astype(BF16)

    return pl.pallas_call(
        body, name=name, grid=(T // TM,),
        in_specs=[_row(TM, D), _weight((D, D)), _row(TM, D), _layer((40, D), l)],
        out_specs=[_row(TM, D), _const((8, D))],
        out_shape=[jax.ShapeDtypeStruct((T, D), BF16), jax.ShapeDtypeStruct((8, D), F32)],
        compiler_params=_cp("arbitrary"),
    )(dh, w, y, sm)


CH = 512


def dwconv_glu_bwd(dy, a, u, sm, smrev, l, name):
    T = dy.shape[0]
    nr, nc = T // TCV, D // CH
    nb = TCV // HALO
    last = T // HALO - 1

    def body(dy_ref, dyn_ref, a_ref, ap_ref, u1_ref, u2_ref, sm_ref, rev_ref, du_ref, dw_ref, shd, sha, da):
        i = pl.program_id(0)
        r = i % nr

        @pl.when(r == 0)
        def _():
            dw_ref[...] = jnp.zeros_like(dw_ref)

        shd[0, 0:TCV, :] = dy_ref[...].astype(F32)
        shd[0, TCV:TCV + HALO, :] = jnp.where(r < nr - 1, dyn_ref[...].astype(F32), 0.0)
        sha[0, 0:HALO, :] = jnp.where(r > 0, ap_ref[...].astype(F32), 0.0)
        sha[0, HALO:HALO + TCV, :] = a_ref[...].astype(F32)
        _make_shifts(shd)
        _make_shifts(sha)
        _conv_taps(shd, rev_ref, da, 0)
        for k in range(CONV_W):
            part = jnp.zeros((SUB, CH), F32)
            for r0 in range(0, TCV, SUB):
                part = part + shd[0, r0:r0 + SUB, :] * _shifted(sha, HALO - (CONV_W - 1) + k + r0, SUB, slice(None))
            dw_ref[k:k + 1, :] += jnp.sum(part, axis=0, keepdims=True)
        dav = da[...]
        u1 = u1_ref[...].astype(F32)
        sg = _sigmoid(u2_ref[...].astype(F32))
        du_ref[0] = (dav * sg).astype(BF16)
        du_ref[1] = (dav * u1 * sg * (1.0 - sg)).astype(BF16)

    tile = lambda i: (i % nr, i // nr)
    in_specs = [pl.BlockSpec((TCV, CH), tile),
                pl.BlockSpec((HALO, CH), lambda i: (jnp.minimum((i % nr + 1) * nb, last), i // nr)),
                pl.BlockSpec((TCV, CH), tile),
                pl.BlockSpec((HALO, CH), lambda i: (jnp.maximum((i % nr) * nb - 1, 0), i // nr)),
                pl.BlockSpec((TCV, CH), tile), pl.BlockSpec((TCV, CH), lambda i: (i % nr, nc + i // nr)),
                pl.BlockSpec((None, 40, CH), lambda i: (l, 0, i // nr)),
                pl.BlockSpec((None, 40, CH), lambda i: (l, 0, i // nr))]
    return pl.pallas_call(
        body, name=name, grid=(nr * nc,), in_specs=in_specs,
        out_specs=[pl.BlockSpec((2, TCV, CH), lambda i: (0, i % nr, i // nr)),
                   pl.BlockSpec((32, CH), lambda i: (0, i // nr))],
        out_shape=[jax.ShapeDtypeStruct((2, T, D), BF16), jax.ShapeDtypeStruct((32, D), F32)],
        scratch_shapes=[pltpu.VMEM((SUB, TCV + HALO, CH), F32), pltpu.VMEM((SUB, TCV + HALO, CH), F32),
                        pltpu.VMEM((TCV, CH), F32)],
        compiler_params=_cp("arbitrary"),
    )(dy, dy, a, a, u, u, sm, smrev)


def _rows_tile(R):
    for t in (512, 256, 128, 64, 32, 16, 8):
        if R % t == 0:
            return t
    return R


def add8_into(J, l, g, others, where, name):
    R, C = g.shape[2:]
    tr = R // 2

    def body(w_ref, g_ref, x_ref, j_in, j_ref):
        acc = g_ref[...].astype(F32)
        for k in range(7):
            acc = acc + x_ref[k].astype(F32)
        j_ref[...] = acc

    return pl.pallas_call(
        body, name=name,
        grid_spec=pltpu.PrefetchScalarGridSpec(
            num_scalar_prefetch=1, grid=(R // tr,),
            in_specs=[pl.BlockSpec((None, None, tr, C), lambda i, w: (w[0], w[1], i, 0)),
                      pl.BlockSpec((7, tr, C), lambda i, w: (0, i, 0)), ANY],
            out_specs=pl.BlockSpec((None, None, tr, C), lambda i, w: (l, w[1], i, 0))),
        out_shape=jax.ShapeDtypeStruct(J.shape, F32), input_output_aliases={3: 0},
        compiler_params=_cp("parallel"),
    )(where, g, others, J)


def adamw(w, g, m, v, name, copy_g=False):
    R, C = w.shape
    tr = _rows_tile(R)

    def body(w_ref, g_ref, m_ref, v_ref, *outs):
        d_ref, nm_ref, nv_ref = outs[-3:]
        gv = g_ref[...]
        if copy_g:
            outs[0][...] = gv
        nm = ADAM_B1 * m_ref[...] + (1.0 - ADAM_B1) * gv
        nv = ADAM_B2 * v_ref[...] + (1.0 - ADAM_B2) * (gv * gv)
        m_hat = nm / (1.0 - ADAM_B1 ** ADAM_STEP)
        v_hat = nv / (1.0 - ADAM_B2 ** ADAM_STEP)
        d_ref[...] = -ADAM_LR * (m_hat / (jnp.sqrt(v_hat) + ADAM_EPS) + ADAM_WD * w_ref[...])
        nm_ref[...] = nm
        nv_ref[...] = nv

    sd = jax.ShapeDtypeStruct((R, C), F32)
    n_out = 4 if copy_g else 3
    return pl.pallas_call(
        body, name=name, grid=(R // tr,),
        in_specs=[_row(tr, C)] * 4, out_specs=[_row(tr, C)] * n_out, out_shape=[sd] * n_out,
        compiler_params=_cp("parallel"),
    )(w, g, m, v)


ANY = pl.BlockSpec(memory_space=pl.ANY)
HBM = pl.BlockSpec(memory_space=pltpu.HBM)
SEM = pl.BlockSpec(memory_space=pltpu.SEMAPHORE)
EFFECT = pltpu.SideEffectType.DATAFLOW_SIDE_EFFECTING


def _place():
    x, y, c = lax.axis_index("x"), lax.axis_index("y"), lax.axis_index("c")
    chips = [(1 - x, y), (x, 1 - y), (1 - x, 1 - y)]
    return x, y, c, chips


def _copy(src, dst, send, recv, k, to):
    return pltpu.make_async_remote_copy(src_ref=src, dst_ref=dst, send_sem=send.at[k], recv_sem=recv.at[k],
                                        device_id=to, device_id_type=MESH)


def xchg_start(name, bufs, plan, n, after=()):
    nb = len(bufs)

    na = len(after)

    def body(*refs):
        send, recv, token = refs[nb + na], refs[nb + na + 1], refs[-1]
        for k, (src, dst, to) in enumerate(plan(refs[:nb])):
            _copy(src, dst, send, recv, k, to).start()
        token[...] = jnp.zeros_like(token)

    outs = pl.pallas_call(
        body, name=name,
        out_shape=(pltpu.SemaphoreType.DMA((n,)), pltpu.SemaphoreType.DMA((n,)),
                   *[pltpu.HBM(b.shape, b.dtype) for b in bufs], jax.ShapeDtypeStruct((8, 128), F32)),
        in_specs=[HBM] * nb + [ANY] * na,
        out_specs=(SEM, SEM, *[HBM] * nb, pl.BlockSpec(memory_space=pltpu.VMEM)),
        input_output_aliases={i: 2 + i for i in range(nb)},
        compiler_params=pltpu.CompilerParams(has_side_effects=EFFECT),
    )(*[pltpu.with_memory_space_constraint(b, pltpu.HBM) for b in bufs], *after)
    return dict(name=name, send=outs[0], recv=outs[1], bufs=list(outs[2:2 + nb]), plan=plan), outs[-1]


def xchg_wait(flight, after):
    bufs, plan = flight["bufs"], flight["plan"]
    nb = len(bufs)

    def body(*refs):
        send, recv = refs[nb], refs[nb + 1]
        for k, (src, dst, to) in enumerate(plan(refs[:nb])):
            cp = _copy(src, dst, send, recv, k, to)
            cp.wait_send()
            cp.wait_recv()

    outs = pl.pallas_call(
        body, name=flight["name"] + "_wait",
        out_shape=tuple(pltpu.HBM(b.shape, b.dtype) for b in bufs),
        in_specs=[HBM] * nb + [SEM, SEM] + [ANY] * len(after),
        out_specs=tuple([HBM] * nb), input_output_aliases={i: i for i in range(nb)},
        compiler_params=pltpu.CompilerParams(has_side_effects=EFFECT),
    )(*bufs, flight["send"], flight["recv"], *after)
    return list(outs)


def _flip(k, x, y, c):
    return ((1 - x) if k & 4 else x, (1 - y) if k & 2 else y, (1 - c) if k & 1 else c)


class WeightGather:
    def __init__(self, shard, groups):
        me = 2 * lax.axis_index("x") + lax.axis_index("y")
        self.names = dict(groups)
        self.ici, self.d2d = {}, {}
        self.token = None
        for gname, names in groups:
            nt = len(names)
            srcs = [shard(n, self.token) for n in names]
            lands = [lax.dynamic_update_slice(lax.empty((4,) + s.shape, s.dtype), s[None], (me, 0, 0, 0))
                     for s in srcs]

            def plan(refs, nt=nt):
                x, y, c, chips = _place()
                return [(refs[t].at[c], refs[nt + t].at[2 * x + y, c], (cx, cy, c))
                        for t in range(nt) for cx, cy in chips]

            self.ici[gname], self.token = xchg_start(f"ag_ici_{gname}", srcs + lands, plan, 3 * nt,
                                                     after=[] if self.token is None else [self.token])

    def forward(self, gname, after):
        nt = len(self.names[gname])
        lands = xchg_wait(self.ici.pop(gname), after)[nt:]

        def plan(refs):
            x, y, c, chips = _place()
            out = []
            for t in range(nt):
                for cx, cy in chips:
                    piece = refs[t].at[2 * cx + cy, c]
                    out.append((piece, piece, (x, y, 1 - c)))
            return out

        self.d2d[gname], token = xchg_start(f"ag_d2d_{gname}", lands, plan, 3 * nt)
        return token

    def get(self, gname, after):
        lands = xchg_wait(self.d2d.pop(gname), after)
        return dict(zip(self.names[gname], lands))


class GradReduce:
    def __init__(self, kinds):
        self.J = {k: lax.empty((L, 2, a2, b), F32) for k, (L, a2, b) in kinds.items()}
        self.x, self.j = {}, {}

    @staticmethod
    def _where(name):
        kind, _, l = name.partition("_")
        return kind, int(l or 0)

    def send(self, gname, grads, after=()):
        names = list(grads)
        nt = len(names)
        gs = [grads[n] for n in names]
        xs = [lax.empty((7,) + g.shape[2:], g.dtype) for g in gs]

        def plan(refs):
            x, y, c, _ = _place()
            out = []
            for t in range(nt):
                for k in range(1, 8):
                    px, py, pc = _flip(k, x, y, c)
                    out.append((refs[t].at[2 * px + py, pc], refs[nt + t].at[k - 1], (px, py, pc)))
            return out

        flight, token = xchg_start(f"rs_x_{gname}", gs + xs, plan, 7 * nt, after=after)
        self.x[gname] = (names, flight)
        return token

    def reduce(self, gname, after):
        names, flight = self.x.pop(gname)
        nt = len(names)
        bufs = xchg_wait(flight, after)
        mine = jnp.stack([2 * lax.axis_index("x") + lax.axis_index("y"), lax.axis_index("c")]).astype(jnp.int32)
        where = [self._where(n) for n in names]
        js = [add8_into(self.J[kind], l, bufs[t], bufs[nt + t], mine, f"rs_add_{names[t]}")
              for t, (kind, l) in enumerate(where)]

        def plan(refs):
            x, y, c, _ = _place()
            out = []
            for t in range(nt):
                half = refs[t].at[where[t][1], c]
                out.append((half, half, (x, y, 1 - c)))
            return out

        flight, token = xchg_start(f"rs_join_{gname}", js, plan, nt)
        self.j[gname] = (where, flight)
        return token

    def finish(self, gname, after):
        where, flight = self.j.pop(gname)
        for (kind, _), j in zip(where, xchg_wait(flight, after)):
            self.J[kind] = j


def small_allreduce_start(v, after):
    me = 4 * lax.axis_index("x") + 2 * lax.axis_index("y") + lax.axis_index("c")
    land = lax.dynamic_update_slice(lax.empty((8,) + v.shape, v.dtype), v[None], (me, 0, 0))

    def plan(refs):
        x, y, c, _ = _place()
        return [(refs[0], refs[1].at[4 * x + 2 * y + c], _flip(k, x, y, c)) for k in range(1, 8)]

    return xchg_start("small_allreduce", [v, land], plan, 7, after=after)


def sum8(all8, name):
    def body(x_ref, o_ref):
        acc = x_ref[0]
        for d in range(1, 8):
            acc = acc + x_ref[d]
        o_ref[...] = acc

    return pl.pallas_call(
        body, name=name,
        in_specs=[pl.BlockSpec(memory_space=pltpu.VMEM)], out_specs=pl.BlockSpec(memory_space=pltpu.VMEM),
        out_shape=jax.ShapeDtypeStruct(all8.shape[1:], F32),
        compiler_params=pltpu.CompilerParams(vmem_limit_bytes=VMEM_LIMIT),
    )(all8)


AG_GROUPS = (("a0", ("pw1_0", "pw2_0", "small")), ("f0", ("up_0", "down_0")),
             ("l1", ("pw1_1", "pw2_1", "up_1", "down_1")), ("l2", ("kv", "wq_0", "wo_0", "up_2", "down_2")),
             ("l3", ("wq_1", "wo_1", "up_3", "down_3")))


def _bucket_table():
    qi = np.arange(BLK)[:, None]
    kj = np.arange(2 * BLK)[None, :]
    d = np.maximum(qi + BLK - kj, 0)
    max_exact = N_BUCKETS // 2
    log_ratio = (np.log(np.maximum(d, 1).astype(np.float32) / np.float32(max_exact))
                 / np.float32(math.log(MAX_DISTANCE / max_exact))).astype(np.float32)
    large = max_exact + (log_ratio * np.float32(N_BUCKETS - max_exact)).astype(np.int32)
    large = np.minimum(large, N_BUCKETS - 1)
    return np.where(d < max_exact, d, large).astype(np.int32)


def _heads_major(a, nh):
    T = a.shape[0]
    return a.reshape(T, nh, HD).transpose(1, 0, 2)


def _heads_minor(a):
    nh, T, _ = a.shape
    return a.transpose(1, 0, 2).reshape(T, nh * HD)


def _slots(land):
    return land.reshape(4, 2 * land.shape[2], land.shape[3])


def _rows(land):
    return land.reshape(8 * land.shape[2], land.shape[3])


def _gview(g):
    s, K, n = g.shape
    return g.reshape(4, 2, K // 2, n) if s == 4 else g.reshape(4, 2, K // 8, n)


def _gate(a, token):
    return a * (1.0 + token[0, 0])


def _conv_small(f_small):
    fs = f_small.transpose(1, 2, 0, 3).reshape(2, 40, D)
    b_pw1 = f_small[:, :, 35:37, :].transpose(1, 0, 2, 3).reshape(2, 1, 2 * D)
    rev = jnp.concatenate([fs[:, CONV_W - 1::-1], jnp.zeros((2, 40 - CONV_W, D), F32)], axis=1)
    return dict(conv=fs, conv_rev=rev, b_pw1=b_pw1, b_pw2=fs[:, 34:35])


def run_step(x, target, P, ag, rs):
    T = x.shape[0]
    zero = jnp.zeros((1, 1, D), F32)
    nm, nf = P["norm_mix"], P["norm_ffn"]
    ag.forward("a0", [ag.token])
    W = ag.get("a0", [])
    sm = _conv_small(W["small"])
    h = x
    saved = []
    for l in range(2):
        xn, u, a = norm_mm_glu(h, nm, l, _slots(W[f"pw1_{l}"]), sm["b_pw1"], f"f_pw1_{l}")
        y, s = dwconv_ln_silu(a, sm["conv"], l, f"f_conv_{l}")
        b2 = sm["b_pw2"]
        if l == 0:
            b2 = _gate(b2, ag.forward("f0", [s]))
        h1 = mm_bias_res(s, _rows(W[f"pw2_{l}"]), b2, l, h, f"f_pw2_{l}")
        if l == 0:
            W.update(ag.get("f0", [h1]))
        xn2, gu, f = norm_mm_swiglu(h1, nf, l, _slots(W[f"up_{l}"]), f"f_up_{l}")
        nxt = "l1" if l == 0 else "l2"
        h2 = mm_bias_res(f, _rows(W[f"down_{l}"]), _gate(zero, ag.forward(nxt, [f])), 0, h1, f"f_down_{l}")
        W.update(ag.get(nxt, [h2]))
        saved.append(dict(h=h, xn=xn, u=u, a=a, y=y, s=s, h1=h1, xn2=xn2, gu=gu, f=f))
        h = h2
    h_kv = h
    kvn, kv = norm_mm(h, P["norm_kv"], 0, _rows(W["kv"]), "f_kv")
    kp = jnp.pad(_heads_major(kv[:, :N_KV * HD], N_KV), ((0, 0), (BLK, 0), (0, 0)))
    vp = jnp.pad(_heads_major(kv[:, N_KV * HD:], N_KV), ((0, 0), (BLK, 0), (0, 0)))
    kvt = jnp.pad(kv.T.reshape(2, N_KV, HD, T), ((0, 0), (0, 0), (0, 0), (BLK, 0)))
    kt, vt = kvt[0], kvt[1]
    bucket = _bucket_table()
    onehot = jnp.asarray(np.eye(N_BUCKETS, dtype=np.float32)[bucket])
    bias = jnp.einsum("qkb,bh->hkq", onehot, P["rel_bias"], precision=lax.Precision.HIGHEST)
    bias = bias.reshape(N_KV, GROUP, 2 * BLK, BLK).transpose(0, 2, 1, 3).reshape(1, N_KV, 2 * BLK, QW)
    bias = bias + jnp.asarray(band_mask())[:, None]
    for j in range(2):
        l = 2 + j
        xn, q = norm_mm(h, nm, l, _rows(W[f"wq_{j}"]), f"f_q_{j}", scale=HD ** -0.5)
        qh = q.T.reshape(N_KV, GROUP, HD, T)
        sink = jnp.broadcast_to(P["sinks"][j].reshape(N_KV, GROUP, 1), (N_KV, GROUP, BLK)).reshape(N_KV, 1, QW)
        oh = attn_fwd(qh, kp, vt, bias, sink, f"f_attn_{j}")
        attn = oh.reshape(N_HEADS * HD, T).T
        h1 = mm_bias_res(attn, _rows(W[f"wo_{j}"]), zero, 0, h, f"f_wo_{j}")
        xn2, gu, f = norm_mm_swiglu(h1, nf, l, _slots(W[f"up_{l}"]), f"f_up_{l}")
        zg = _gate(zero, ag.forward("l3", [f])) if j == 0 else zero
        h2 = mm_bias_res(f, _rows(W[f"down_{l}"]), zg, 0, h1, f"f_down_{l}")
        if j == 0:
            W.update(ag.get("l3", [h2]))
        saved.append(dict(h=h, xn=xn, qh=qh, oh=oh, sink=sink, attn=attn, h1=h1, xn2=xn2, gu=gu, f=f))
        h = h2

    dh, st_final = final_loss(h, P["norm_final"], target, "loss_head")

    S = dict(norm_ffn=[None] * 4, norm_mix=[None] * 4, conv=[None] * 2, taps=[None] * 2, b_pw1=[None] * 2,
             b_pw2=[None] * 2, sinks=[None] * 2)

    def ffn_bwd(dh, sv, l, nf, after=()):
        du = mmT_swiglu_bwd(dh, _rows(W[f"down_{l}"]), sv["gu"], f"b_down_{l}", after)
        gd = mm_dw(sv["f"], dh, f"w_down_{l}", 512, 1)
        gu = mm_dw(sv["xn2"], du, f"w_up_{l}", DFF // 2, 4)
        dh, dg = mmT_rmsbwd(du, _slots(W[f"up_{l}"]), sv["h1"], nf, l, dh, f"b_up_{l}")
        S["norm_ffn"][l] = dg[0]
        return dh, {f"down_{l}": _gview(gd), f"up_{l}": _gview(gu)}

    dk = dv = dbias = None
    sent = []
    for j in (1, 0):
        l = 2 + j
        sv = saved[l]
        dh, grads = ffn_bwd(dh, sv, l, nf, sent)
        dattn = mmT(dh, _rows(W[f"wo_{j}"]), f"b_wo_{j}")
        grads[f"wo_{j}"] = _gview(mm_dw(sv["attn"], dh, f"w_wo_{j}", 512, 1))
        doh = dattn.T.reshape(N_KV, GROUP, HD, T)
        dqh, dkj, dvj, dbj, dsj = attn_bwd(sv["qh"], kp, kt, vp, bias, sv["sink"], sv["oh"], doh, f"b_attn_{j}")
        dq = dqh.reshape(N_HEADS * HD, T).T
        grads[f"wq_{j}"] = _gview(mm_dw(sv["xn"], dq, f"w_q_{j}", 512, 1))
        dh, dg = mmT_rmsbwd(dq, _rows(W[f"wq_{j}"])[None], sv["h"], nm, l, dh, f"b_q_{j}")
        S["norm_mix"][l] = dg[0]
        S["sinks"][j] = jnp.sum(dsj.reshape(N_HEADS, BLK), axis=1)
        dk = dkj if dk is None else dk + dkj
        dv = dvj if dv is None else dv + dvj
        dbias = dbj if dbias is None else dbias + dbj
        if j == 1:
            sent = [rs.send("l3", grads)]

    dkv = jnp.concatenate([_heads_minor(dk[:, BLK:]), _heads_minor(dv[:, BLK:])], axis=1).astype(BF16)
    grads["kv"] = _gview(mm_dw(kvn, dkv, "w_kv", 512, 1))
    dh, dg = mmT_rmsbwd(dkv, _rows(W["kv"])[None], h_kv, P["norm_kv"], 0, dh, "b_kv")
    S["norm_kv"] = dg[0]
    dbh = dbias.reshape(N_KV, 2 * BLK, GROUP, BLK)
    S["rel_bias"] = jnp.einsum("vkgq,qkb->bvg", dbh, onehot, precision=lax.Precision.HIGHEST).reshape(N_BUCKETS, N_HEADS)
    sent = [rs.send("l2", grads)]
    nf = _gate(nf, rs.reduce("l3", [dh]))

    for l in (1, 0):
        sv = saved[l]
        dh, grads = ffn_bwd(dh, sv, l, nf, sent)
        conv = sm["conv"]
        if l == 0:
            conv = _gate(conv, rs.send("f0", grads))
            grads = {}
        dy, st = mmT_lnbwd(dh, _rows(W[f"pw2_{l}"]), sv["y"], conv, l, f"b_pw2_{l}")
        g2, S["b_pw2"][l] = mm_dw(sv["s"], dh, f"w_pw2_{l}", 512, 1, colsum=True)
        du, dtaps = dwconv_glu_bwd(dy, sv["a"], sv["u"], sm["conv"], sm["conv_rev"], l, f"b_conv_{l}")
        S["conv"][l] = st[0:3]
        S["taps"][l] = dtaps[0:CONV_W]
        if l == 0:
            rs.finish("l2", [du])
            nm = _gate(nm, rs.reduce("l1", [du]))
        g1, S["b_pw1"][l] = mm_dw(sv["xn"], du, f"w_pw1_{l}", 512, 4, colsum=True)
        grads[f"pw2_{l}"], grads[f"pw1_{l}"] = _gview(g2), _gview(g1)
        dh, dg = mmT_rmsbwd(du, _slots(W[f"pw1_{l}"]), sv["h"], nm, l, dh, f"b_pw1_{l}")
        S["norm_mix"][l] = dg[0]
        if l == 1:
            sent = [rs.send("l1", grads)]
            rs.finish("l3", [dh])
            nf = _gate(nf, rs.reduce("l2", [dh]))
    S["norm_final"] = st_final[0]
    S["loss"] = st_final[1]
    return grads, dh, S


R_CONV = 37
R_SMALL = 88


def _pack_small(S):
    rows = []
    for l in range(2):
        rows += [S["taps"][l], S["conv"][l][2:3], S["conv"][l][0:2], S["b_pw2"][l], S["b_pw1"][l].reshape(2, D)]
    rows += [jnp.stack(S["norm_mix"]), jnp.stack(S["norm_ffn"]), S["norm_kv"][None], S["norm_final"][None]]
    tail = jnp.concatenate([jnp.stack(S["sinks"]).reshape(-1), S["rel_bias"].reshape(-1)])
    rows += [jnp.pad(tail, (0, D - tail.shape[0]))[None], S["loss"][None]]
    v = jnp.concatenate(rows, axis=0)
    return jnp.pad(v, ((0, R_SMALL - v.shape[0]), (0, 0)))


def kernel(x, norm_mix, norm_ffn, conv_w_pw1, conv_b_pw1, conv_w_dw, conv_b_dw, conv_ln_g, conv_ln_b, conv_w_pw2, conv_b_pw2, norm_kv, w_kv, w_q, w_o, sinks, rel_bias, ffn_w_up, ffn_w_down, norm_final, loss_target, m_norm_mix, m_norm_ffn, m_conv_w_pw1, m_conv_b_pw1, m_conv_w_dw, m_conv_b_dw, m_conv_ln_g, m_conv_ln_b, m_conv_w_pw2, m_conv_b_pw2, m_norm_kv, m_w_kv, m_w_q, m_w_o, m_sinks, m_rel_bias, m_ffn_w_up, m_ffn_w_down, m_norm_final, v_norm_mix, v_norm_ffn, v_conv_w_pw1, v_conv_b_pw1, v_conv_w_dw, v_conv_b_dw, v_conv_ln_g, v_conv_ln_b, v_conv_w_pw2, v_conv_b_pw2, v_norm_kv, v_w_kv, v_w_q, v_w_o, v_sinks, v_rel_bias, v_ffn_w_up, v_ffn_w_down, v_norm_final):
    me = 2 * lax.axis_index("x") + lax.axis_index("y")
    weights = dict(norm_mix=norm_mix, norm_ffn=norm_ffn, conv_w_pw1=conv_w_pw1, conv_b_pw1=conv_b_pw1,
                   conv_w_dw=conv_w_dw, conv_b_dw=conv_b_dw, conv_ln_g=conv_ln_g, conv_ln_b=conv_ln_b,
                   conv_w_pw2=conv_w_pw2, conv_b_pw2=conv_b_pw2, norm_kv=norm_kv, w_kv=w_kv, w_q=w_q, w_o=w_o,
                   sinks=sinks, rel_bias=rel_bias, ffn_w_up=ffn_w_up, ffn_w_down=ffn_w_down, norm_final=norm_final)
    mom_m = dict(norm_mix=m_norm_mix, norm_ffn=m_norm_ffn, conv_w_pw1=m_conv_w_pw1, conv_b_pw1=m_conv_b_pw1,
                 conv_w_dw=m_conv_w_dw, conv_b_dw=m_conv_b_dw, conv_ln_g=m_conv_ln_g, conv_ln_b=m_conv_ln_b,
                 conv_w_pw2=m_conv_w_pw2, conv_b_pw2=m_conv_b_pw2, norm_kv=m_norm_kv, w_kv=m_w_kv, w_q=m_w_q,
                 w_o=m_w_o, sinks=m_sinks, rel_bias=m_rel_bias, ffn_w_up=m_ffn_w_up, ffn_w_down=m_ffn_w_down,
                 norm_final=m_norm_final)
    mom_v = dict(norm_mix=v_norm_mix, norm_ffn=v_norm_ffn, conv_w_pw1=v_conv_w_pw1, conv_b_pw1=v_conv_b_pw1,
                 conv_w_dw=v_conv_w_dw, conv_b_dw=v_conv_b_dw, conv_ln_g=v_conv_ln_g, conv_ln_b=v_conv_ln_b,
                 conv_w_pw2=v_conv_w_pw2, conv_b_pw2=v_conv_b_pw2, norm_kv=v_norm_kv, w_kv=v_w_kv, w_q=v_w_q,
                 w_o=v_w_o, sinks=v_sinks, rel_bias=v_rel_bias, ffn_w_up=v_ffn_w_up, ffn_w_down=v_ffn_w_down,
                 norm_final=v_norm_final)

    big = {"conv_w_pw1": "pw1", "conv_w_pw2": "pw2", "w_q": "wq", "w_o": "wo", "ffn_w_up": "up",
           "ffn_w_down": "down", "w_kv": "kv"}
    of_kind = {k: n for n, k in big.items()}

    def shard(name, token):
        if name == "small":
            a = jnp.concatenate(
                [conv_w_dw, conv_b_dw[:, None], conv_ln_g[:, None], conv_ln_b[:, None], conv_b_pw2[:, None],
                 conv_b_pw1.reshape(2, 2, 256), jnp.zeros((2, 3, 256), F32)], axis=1)
            return a if token is None else _gate(a, token)
        kind, _, l = name.partition("_")
        a = weights[of_kind[kind]]
        a = a[int(l)] if l else a
        if token is not None:
            a = _gate(a, token)
        return a.astype(BF16).reshape(2, a.shape[0] // 2, a.shape[1])

    ag = WeightGather(shard, AG_GROUPS)
    rs = GradReduce({"pw1": (2, 512, 512), "pw2": (2, 128, D), "wq": (2, 128, D), "wo": (2, 128, D),
                     "up": (4, 512, DFF // 2), "down": (4, DFF // 8, D), "kv": (1, 128, 512)})

    P = dict(norm_mix=norm_mix[:, None], norm_ffn=norm_ffn[:, None], norm_kv=norm_kv[None, None],
             norm_final=norm_final[None], sinks=sinks, rel_bias=rel_bias)
    last, grad_x, S = run_step(x[0], loss_target[0], P, ag, rs)

    rs.finish("l1", [grad_x])
    small_flight, token = small_allreduce_start(_gate(_pack_small(S), rs.reduce("f0", [grad_x])), [])
    token = rs.send("c0", last, after=[token])
    delta, new_m, new_v, big_grads = {}, {}, {}, {}

    def update(n):
        shp = weights[n].shape
        r2 = (int(np.prod(shp[:-1])), shp[-1])
        g, d, nm, nv = adamw(weights[n].reshape(r2), rs.J[big[n]].reshape(r2), mom_m[n].reshape(r2),
                             mom_v[n].reshape(r2), f"adamw_{n}", copy_g=True)
        big_grads[n], delta[n], new_m[n], new_v[n] = g.reshape(shp), d.reshape(shp), nm.reshape(shp), nv.reshape(shp)

    rs.finish("f0", [token])
    for n in ("ffn_w_up", "ffn_w_down"):
        update(n)
    vsum = sum8(xchg_wait(small_flight, [delta["ffn_w_up"], delta["ffn_w_down"]])[1], "small_sum")

    col = lambda a: lax.dynamic_slice_in_dim(a, me * 256, 256, axis=-1)
    grads = {}
    for l in range(2):
        base = l * R_CONV
        grads.setdefault("conv_w_dw", []).append(col(vsum[base:base + 31]))
        grads.setdefault("conv_b_dw", []).append(col(vsum[base + 31]))
        grads.setdefault("conv_ln_g", []).append(col(vsum[base + 32]))
        grads.setdefault("conv_ln_b", []).append(col(vsum[base + 33]))
        grads.setdefault("conv_b_pw2", []).append(col(vsum[base + 34]))
        grads.setdefault("conv_b_pw1", []).append(
            lax.dynamic_slice_in_dim(vsum[base + 35:base + 37].reshape(2 * D), me * 512, 512, axis=0))
    grads = {k: jnp.stack(v) for k, v in grads.items()}
    base = 2 * R_CONV
    grads["norm_mix"] = vsum[base:base + 4]
    grads["norm_ffn"] = vsum[base + 4:base + 8]
    grads["norm_kv"] = vsum[base + 8]
    grads["norm_final"] = vsum[base + 9]
    grads["sinks"] = vsum[base + 10, 0:32].reshape(2, 16)
    grads["rel_bias"] = vsum[base + 10, 32:32 + 512].reshape(32, 16)
    loss = vsum[base + 11, 0]

    for n in weights:
        if n not in big:
            shp = weights[n].shape
            r2 = (int(np.prod(shp[:-1])), shp[-1])
            d, nm, nv = adamw(weights[n].reshape(r2), grads[n].reshape(r2), mom_m[n].reshape(r2),
                              mom_v[n].reshape(r2), f"adamw_{n}")
            delta[n], new_m[n], new_v[n] = d.reshape(shp), nm.reshape(shp), nv.reshape(shp)

    rs.reduce("c0", [vsum])
    for n in ("w_q", "w_o", "w_kv"):
        update(n)
    rs.finish("c0", [delta["w_kv"]])
    for n in ("conv_w_pw1", "conv_w_pw2"):
        update(n)
    grads.update(big_grads)

    order = list(weights)
    return (loss, grad_x[None], *[grads[n] for n in order], *[delta[n] for n in order],
            *[new_m[n] for n in order], *[new_v[n] for n in order])
```

```python
import functools
import math

import numpy as np
import jax
import jax.numpy as jnp
from jax import lax
from jax.experimental import pallas as pl
from jax.experimental.pallas import tpu as pltpu

F32 = jnp.float32
BF16 = jnp.bfloat16
MESH = pl.DeviceIdType.MESH

D = 1024
DFF = 2816
N_HEADS = 16
N_KV = 4
GROUP = 4
HD = 64
BLK = 128
CONV_W = 31
HALO = 32
N_BUCKETS = 32
MAX_DISTANCE = 128
EPS = 1e-6
NEG_INF = -1e30
TM = 512
TCV = 256
VMEM_LIMIT = 56 * 2 ** 20

ADAM_LR, ADAM_B1, ADAM_B2, ADAM_EPS, ADAM_WD, ADAM_STEP = 0.001, 0.9, 0.999, 1e-08, 0.01, 10


def _cp(*sem):
    return pltpu.CompilerParams(dimension_semantics=sem, vmem_limit_bytes=VMEM_LIMIT)


def _sigmoid(x):
    return 1.0 / (1.0 + jnp.exp(-x))


def _row(tm, n):
    return pl.BlockSpec((tm, n), lambda i: (i, 0))


def _const(shape):
    nd = len(shape)
    return pl.BlockSpec(shape, lambda i: (0,) * nd)


def _weight(shape):
    nd = len(shape)
    return pl.BlockSpec(shape, lambda i: (0,) * nd, pipeline_mode=pl.Buffered(1))


def _layer(shape, l):
    nd = len(shape)
    return pl.BlockSpec((None,) + tuple(shape), lambda i: (l,) + (0,) * nd)


def _dot(a, b):
    return jnp.dot(a, b, preferred_element_type=F32)


def _dot_nt(a, b):
    return lax.dot_general(a, b, (((1,), (1,)), ((), ())), preferred_element_type=F32)


def _dot_tn(a, b):
    return lax.dot_general(a, b, (((0,), (0,)), ((), ())), preferred_element_type=F32)


def _rms(x):
    return lax.rsqrt(jnp.mean(x * x, axis=-1, keepdims=True) + EPS)


def norm_mm_glu(h, g, l, w, b, name):
    T = h.shape[0]
    ns = w.shape[-1]

    def body(h_ref, g_ref, w_ref, b_ref, xn_ref, u_ref, a_ref):
        x = h_ref[...]
        xn = (x * _rms(x) * g_ref[...]).astype(BF16)
        xn_ref[...] = xn
        ahead = _dot(xn, w_ref[0]), _dot(xn, w_ref[2])
        for s in range(2):
            lo, hi = s * ns, (s + 1) * ns
            p1, p2 = ahead
            if s == 0:
                ahead = _dot(xn, w_ref[1]), _dot(xn, w_ref[3])
            u1 = p1 + b_ref[:, lo:hi]
            u2 = p2 + b_ref[:, D + lo:D + hi]
            u_ref[:, lo:hi] = u1.astype(BF16)
            u_ref[:, D + lo:D + hi] = u2.astype(BF16)
            a_ref[:, lo:hi] = (u1 * _sigmoid(u2)).astype(BF16)

    return pl.pallas_call(
        body, name=name, grid=(T // TM,),
        in_specs=[_row(TM, D), _layer((1, D), l), _weight((4, D, ns)), _layer((1, 2 * D), l)],
        out_specs=[_row(TM, D), _row(TM, 2 * D), _row(TM, D)],
        out_shape=[jax.ShapeDtypeStruct((T, D), BF16), jax.ShapeDtypeStruct((T, 2 * D), BF16),
                   jax.ShapeDtypeStruct((T, D), BF16)],
        compiler_params=_cp("parallel"),
    )(h, g, w, b)


SUB = 8


def _make_shifts(sh):
    n = TCV + HALO - SUB
    for r in range(1, SUB):
        for r0 in range(0, n, 40):
            sh[r, r0:r0 + 40, :] = sh[0, pl.ds(r + r0, 40), :]


def _shifted(sh, off, rows, cols):
    return sh[off % SUB, pl.ds(off - off % SUB, rows), cols]


def _conv_taps(sh, w_ref, out_ref, first):
    RB, LB = 32, 512
    for r0 in range(0, TCV, RB):
        for c0 in range(0, out_ref.shape[1], LB):
            acc = jnp.zeros((RB, LB), F32)
            for k in range(CONV_W):
                acc = acc + w_ref[k:k + 1, c0:c0 + LB] * _shifted(sh, first + k + r0, RB, slice(c0, c0 + LB))
            out_ref[r0:r0 + RB, c0:c0 + LB] = acc


def dwconv_ln_silu(a, sm, l, name):
    T = a.shape[0]
    nb = TCV // HALO

    def body(cur_ref, prev_ref, sm_ref, y_ref, s_ref, sh, yb):
        i = pl.program_id(0)
        sh[0, 0:HALO, :] = jnp.where(i > 0, prev_ref[...].astype(F32), 0.0)
        sh[0, HALO:HALO + TCV, :] = cur_ref[...].astype(F32)
        _make_shifts(sh)
        _conv_taps(sh, sm_ref, yb, HALO - (CONV_W - 1))
        y = yb[...] + sm_ref[31:32, :]
        y_ref[...] = y.astype(BF16)
        mu = jnp.mean(y, axis=-1, keepdims=True)
        yc = y - mu
        rstd = lax.rsqrt(jnp.mean(yc * yc, axis=-1, keepdims=True) + EPS)
        z = yc * rstd * sm_ref[32:33, :] + sm_ref[33:34, :]
        s_ref[...] = (z * _sigmoid(z)).astype(BF16)

    return pl.pallas_call(
        body, name=name, grid=(T // TCV,),
        in_specs=[_row(TCV, D), pl.BlockSpec((HALO, D), lambda i: (jnp.maximum(i * nb - 1, 0), 0)),
                  _layer((40, D), l)],
        out_specs=[_row(TCV, D), _row(TCV, D)],
        out_shape=[jax.ShapeDtypeStruct((T, D), BF16), jax.ShapeDtypeStruct((T, D), BF16)],
        scratch_shapes=[pltpu.VMEM((SUB, TCV + HALO, D), F32), pltpu.VMEM((TCV, D), F32)],
        compiler_params=_cp("parallel"),
    )(a, a, sm)


def mm_bias_res(xb, w, b, bl, res, name):
    T, K = xb.shape

    def body(x_ref, w_ref, b_ref, r_ref, o_ref):
        o_ref[...] = _dot(x_ref[...], w_ref[...]) + b_ref[...] + r_ref[...]

    return pl.pallas_call(
        body, name=name, grid=(T // TM,),
        in_specs=[_row(TM, K), _weight((K, D)), _layer((1, D), bl), _row(TM, D)],
        out_specs=_row(TM, D), out_shape=jax.ShapeDtypeStruct((T, D), F32),
        compiler_params=_cp("parallel"),
    )(xb, w, b, res)


def norm_mm_swiglu(h, g, l, w, name):
    T = h.shape[0]
    ns = w.shape[-1]

    def body(h_ref, g_ref, w_ref, xn_ref, gu_ref, f_ref):
        x = h_ref[...]
        xn = (x * _rms(x) * g_ref[...]).astype(BF16)
        xn_ref[...] = xn
        for s in range(2):
            lo, hi = s * ns, (s + 1) * ns
            gate = _dot(xn, w_ref[s])
            up = _dot(xn, w_ref[2 + s])
            gu_ref[:, lo:hi] = gate.astype(BF16)
            gu_ref[:, DFF + lo:DFF + hi] = up.astype(BF16)
            f_ref[:, lo:hi] = (gate * _sigmoid(gate) * up).astype(BF16)

    return pl.pallas_call(
        body, name=name, grid=(T // TM,),
        in_specs=[_row(TM, D), _layer((1, D), l), _weight((4, D, ns))],
        out_specs=[_row(TM, D), _row(TM, 2 * DFF), _row(TM, DFF)],
        out_shape=[jax.ShapeDtypeStruct((T, D), BF16), jax.ShapeDtypeStruct((T, 2 * DFF), BF16),
                   jax.ShapeDtypeStruct((T, DFF), BF16)],
        compiler_params=_cp("parallel"),
    )(h, g, w)


def norm_mm(h, g, gl, w, name, scale=1.0):
    T = h.shape[0]
    N = w.shape[-1]

    def body(h_ref, g_ref, w_ref, xn_ref, o_ref):
        x = h_ref[...]
        xn = (x * _rms(x) * g_ref[...]).astype(BF16)
        xn_ref[...] = xn
        o_ref[...] = (_dot(xn, w_ref[...]) * scale).astype(BF16)

    return pl.pallas_call(
        body, name=name, grid=(T // TM,),
        in_specs=[_row(TM, D), _layer((1, D), gl), _weight((D, N))],
        out_specs=[_row(TM, D), _row(TM, N)],
        out_shape=[jax.ShapeDtypeStruct((T, D), BF16), jax.ShapeDtypeStruct((T, N), BF16)],
        compiler_params=_cp("parallel"),
    )(h, g, w)


QB = 8
QW = GROUP * BLK


def band_mask():
    qi = np.arange(QW)[None, :] % BLK
    kj = np.arange(2 * BLK)[:, None]
    band = ((kj < BLK) & (kj > qi)) | ((kj >= BLK) & (kj - BLK <= qi))
    first = band & (kj >= BLK)
    return np.where(np.stack([first, band]), 0.0, NEG_INF).astype(np.float32)


def _softmax_cols(s, sink):
    m = jnp.maximum(jnp.max(s, axis=0, keepdims=True), sink)
    p = jnp.exp(s - m)
    es = jnp.exp(sink - m)
    inv = 1.0 / (jnp.sum(p, axis=0, keepdims=True) + es)
    return p, inv, es


def _attn_specs(T):
    W = QB * BLK
    qspec = pl.BlockSpec((None, GROUP, HD, W), lambda kv, n: (kv, 0, 0, n))
    kspec = pl.BlockSpec((None, T + BLK, HD), lambda kv, n: (kv, 0, 0))
    ktspec = [pl.BlockSpec((None, HD, W), lambda kv, n: (kv, 0, n)),
              pl.BlockSpec((None, HD, BLK), lambda kv, n: (kv, 0, (n + 1) * QB))]
    bspec = pl.BlockSpec((2, None, 2 * BLK, QW), lambda kv, n: (0, kv, 0, 0))
    sspec = pl.BlockSpec((None, 1, QW), lambda kv, n: (kv, 0, 0))
    return qspec, kspec, ktspec, bspec, sspec


def _attn_block(n, b):
    blk = n * QB + b
    rows = pl.ds(pl.multiple_of(blk * BLK, BLK), 2 * BLK)
    return rows, (jnp.minimum(blk, 1) if b == 0 else 1)


def _band_cols(main_ref, tail_ref, b):
    if b < QB - 1:
        return main_ref[:, b * BLK:(b + 2) * BLK]
    return jnp.concatenate([main_ref[:, b * BLK:], tail_ref[...]], axis=1)


def _heads_side_by_side(ref, qs):
    return jnp.concatenate([ref[g, :, qs] for g in range(GROUP)], axis=1)


def attn_fwd(q, kp, vt, bias, sink, name):
    T = q.shape[3]
    qspec, kspec, ktspec, bspec, sspec = _attn_specs(T)

    def body(q_ref, k_ref, vt_ref, vtt_ref, b_ref, s_ref, o_ref, pb):
        n = pl.program_id(1)

        def scores(b):
            return _dot(k_ref[_attn_block(n, b)[0], :], _heads_side_by_side(q_ref, slice(b * BLK, (b + 1) * BLK)))

        st_next = scores(0)
        for b in range(QB):
            rows, table = _attn_block(n, b)
            qs = slice(b * BLK, (b + 1) * BLK)
            st = st_next
            if b + 1 < QB:
                st_next = scores(b + 1)
            for g in range(GROUP):
                hs = slice(g * BLK, (g + 1) * BLK)
                p, inv, _ = _softmax_cols(st[:, hs] + b_ref[table, :, hs], s_ref[:, hs])
                pb[:, hs] = (p * inv).astype(BF16)
            ot = _dot(_band_cols(vt_ref, vtt_ref, b), pb[...])
            for g in range(GROUP):
                o_ref[g, :, qs] = ot[:, g * BLK:(g + 1) * BLK].astype(BF16)

    return pl.pallas_call(
        body, name=name, grid=(N_KV, T // (QB * BLK)),
        in_specs=[qspec, kspec, *ktspec, bspec, sspec], out_specs=qspec,
        out_shape=jax.ShapeDtypeStruct((N_KV, GROUP, HD, T), BF16),
        scratch_shapes=[pltpu.VMEM((2 * BLK, QW), BF16)],
        compiler_params=_cp("parallel", "parallel"),
    )(q, kp, vt, vt, bias, sink)


def attn_bwd(q, kp, kt, vp, bias, sink, o, do, name):
    T = q.shape[3]
    qspec, kspec, ktspec, bspec, sspec = _attn_specs(T)

    def body(q_ref, k_ref, kt_ref, ktt_ref, v_ref, b_ref, s_ref, o_ref, do_ref,
             dq_ref, dk_ref, dv_ref, db_ref, ds_ref, pb, dsb):
        n = pl.program_id(1)

        @pl.when(n == 0)
        def _():
            dk_ref[...] = jnp.zeros_like(dk_ref)
            dv_ref[...] = jnp.zeros_like(dv_ref)
            db_ref[...] = jnp.zeros_like(db_ref)
            ds_ref[...] = jnp.zeros_like(ds_ref)

        def products(b):
            rows = _attn_block(n, b)[0]
            qs = slice(b * BLK, (b + 1) * BLK)
            q4, do4 = _heads_side_by_side(q_ref, qs), _heads_side_by_side(do_ref, qs)
            return q4, do4, _dot(k_ref[rows, :], q4), _dot(v_ref[rows, :], do4)

        ahead = products(0)
        for b in range(QB):
            rows, table = _attn_block(n, b)
            qs = slice(b * BLK, (b + 1) * BLK)
            q4, do4, st, dpt = ahead
            if b + 1 < QB:
                ahead = products(b + 1)
            for g in range(GROUP):
                hs = slice(g * BLK, (g + 1) * BLK)
                p, inv, es = _softmax_cols(st[:, hs] + b_ref[table, :, hs], s_ref[:, hs])
                probs = p * inv
                delta = jnp.sum(do_ref[g, :, qs].astype(F32) * o_ref[g, :, qs].astype(F32), axis=0, keepdims=True)
                dS = probs * (dpt[:, hs] - delta)
                ds_ref[:, hs] += -(es * inv) * delta
                db_ref[:, hs] += dS
                pb[:, hs] = probs.astype(BF16)
                dsb[:, hs] = dS.astype(BF16)
            dqt = _dot(_band_cols(kt_ref, ktt_ref, b), dsb[...]) * (HD ** -0.5)
            for g in range(GROUP):
                dq_ref[g, :, qs] = dqt[:, g * BLK:(g + 1) * BLK].astype(BF16)
            dk_ref[rows, :] += _dot_nt(dsb[...], q4)
            dv_ref[rows, :] += _dot_nt(pb[...], do4)

    kout = pl.BlockSpec((None, T + BLK, HD), lambda kv, n: (kv, 0, 0))
    dbspec = pl.BlockSpec((None, 2 * BLK, QW), lambda kv, n: (kv, 0, 0))
    return pl.pallas_call(
        body, name=name, grid=(N_KV, T // (QB * BLK)),
        in_specs=[qspec, kspec, *ktspec, kspec, bspec, sspec, qspec, qspec],
        out_specs=[qspec, kout, kout, dbspec, sspec],
        out_shape=[jax.ShapeDtypeStruct((N_KV, GROUP, HD, T), BF16),
                   jax.ShapeDtypeStruct((N_KV, T + BLK, HD), F32), jax.ShapeDtypeStruct((N_KV, T + BLK, HD), F32),
                   jax.ShapeDtypeStruct((N_KV, 2 * BLK, QW), F32), jax.ShapeDtypeStruct((N_KV, 1, QW), F32)],
        scratch_shapes=[pltpu.VMEM((2 * BLK, QW), BF16), pltpu.VMEM((2 * BLK, QW), BF16)],
        compiler_params=_cp("parallel", "arbitrary"),
    )(q, kp, kt, kt, vp, bias, sink, o, do)


def final_loss(h, g, target, name):
    T = h.shape[0]

    def body(h_ref, g_ref, t_ref, dh_ref, st_ref):
        i = pl.program_id(0)

        @pl.when(i == 0)
        def _():
            st_ref[...] = jnp.zeros_like(st_ref)

        x = h_ref[...]
        r = _rms(x)
        xh = x * r
        e = xh * g_ref[...] - t_ref[...]
        loss = 0.5 * jnp.sum(jnp.mean(e * e, axis=-1, keepdims=True))
        dy = e * (1.0 / D)
        st_ref[0:1, :] += jnp.sum(dy * xh, axis=0, keepdims=True)
        lane = lax.broadcasted_iota(jnp.int32, (1, D), 1)
        st_ref[1:2, :] += jnp.where(lane == 0, loss, 0.0)
        dxh = dy * g_ref[...]
        dh_ref[...] = r * (dxh - xh * jnp.mean(dxh * xh, axis=-1, keepdims=True))

    return pl.pallas_call(
        body, name=name, grid=(T // TM,),
        in_specs=[_row(TM, D), _const((1, D)), _row(TM, D)],
        out_specs=[_row(TM, D), _const((8, D))],
        out_shape=[jax.ShapeDtypeStruct((T, D), F32), jax.ShapeDtypeStruct((8, D), F32)],
        compiler_params=_cp("arbitrary"),
    )(h, g, target)


def mm_dw(x, dy, name, tn, slots, colsum=False):
    T, K = x.shape
    split = dy.ndim == 3
    N = dy.shape[-1] * (2 if split else 1)
    tt = min(T, 2048 if K <= 1024 else 1024)
    nt = T // tt
    ns = N // slots
    per = ns // tn

    def body(x_ref, dy_ref, *rest):
        if colsum:
            dw_ref, cs_ref, acc, cacc = rest
        else:
            dw_ref, acc = rest
        t = pl.program_id(1)

        @pl.when(t == 0)
        def _():
            acc[...] = jnp.zeros_like(acc)
            if colsum:
                cacc[...] = jnp.zeros_like(cacc)

        dyv = dy_ref[...]
        dyb = dyv.astype(BF16)
        kc = 256

        def product(k0):
            return _dot_tn(x_ref[:, k0:k0 + kc].astype(BF16), dyb)

        ahead = product(0)
        for k0 in range(0, K, kc):
            cur = ahead
            if k0 + kc < K:
                ahead = product(k0 + kc)
            acc[k0:k0 + kc, :] += cur
        if colsum:
            cacc[...] += jnp.sum(dyv.astype(F32), axis=0, keepdims=True)

        @pl.when(t == nt - 1)
        def _():
            dw_ref[...] = acc[...].astype(BF16)
            if colsum:
                cs_ref[...] = cacc[...]

    if split:
        half = N // 2 // tn
        dy_spec = pl.BlockSpec((None, tt, tn), lambda j, t: (j // half, t, j % half))
    else:
        dy_spec = pl.BlockSpec((tt, tn), lambda j, t: (t, j))
    out_specs = [pl.BlockSpec((None, K, tn), lambda j, t: (j // per, 0, j % per))]
    out_shape = [jax.ShapeDtypeStruct((slots, K, ns), BF16)]
    scratch = [pltpu.VMEM((K, tn), F32)]
    if colsum:
        out_specs.append(pl.BlockSpec((1, tn), lambda j, t: (0, j)))
        out_shape.append(jax.ShapeDtypeStruct((1, N), F32))
        scratch.append(pltpu.VMEM((1, tn), F32))
    res = pl.pallas_call(
        body, name=name, grid=(N // tn, nt),
        in_specs=[pl.BlockSpec((tt, K), lambda j, t: (t, 0)), dy_spec],
        out_specs=out_specs, out_shape=out_shape, scratch_shapes=scratch,
        compiler_params=_cp("parallel", "arbitrary"),
    )(x, dy)
    return tuple(res) if colsum else res[0]


def mmT_swiglu_bwd(dh, w, gu, name, after=()):
    T = dh.shape[0]
    cw = 256

    def body(dh_ref, w_ref, gu_ref, *rest):
        du_ref = rest[-1]
        dhb = dh_ref[...].astype(BF16)
        ahead = _dot_nt(dhb, w_ref[0:cw, :])
        for lo in range(0, DFF, cw):
            hi = lo + cw
            df = ahead
            if hi < DFF:
                ahead = _dot_nt(dhb, w_ref[hi:hi + cw, :])
            gate = gu_ref[:, lo:hi].astype(F32)
            up = gu_ref[:, DFF + lo:DFF + hi].astype(F32)
            sg = _sigmoid(gate)
            silu = gate * sg
            du_ref[:, lo:hi] = (df * (up * (sg + silu * (1.0 - sg)))).astype(BF16)
            du_ref[:, DFF + lo:DFF + hi] = (df * silu).astype(BF16)

    return pl.pallas_call(
        body, name=name, grid=(T // TM,),
        in_specs=[_row(TM, D), _weight((DFF, D)), _row(TM, 2 * DFF)] + [ANY] * len(after),
        out_specs=_row(TM, 2 * DFF), out_shape=jax.ShapeDtypeStruct((T, 2 * DFF), BF16),
        compiler_params=_cp("parallel"),
    )(dh, w, gu, *after)


def mmT_rmsbwd(du, w, h, g, gl, dh_in, name):
    split = du.ndim == 3
    T = du.shape[-2]
    N = du.shape[-1] * (2 if split else 1)
    slots = w.shape[0]
    ns = N // slots

    RH = TM // 2

    def piece(du_ref, s, rows):
        if split:
            per = slots // 2
            return du_ref[s // per, rows, (s % per) * ns:(s % per + 1) * ns]
        return du_ref[rows, s * ns:(s + 1) * ns]

    def body(du_ref, w_ref, h_ref, g_ref, di_ref, dh_ref, dg_ref):
        i = pl.program_id(0)

        @pl.when(i == 0)
        def _():
            dg_ref[...] = jnp.zeros_like(dg_ref)

        def products(k):
            rows = slice(k * RH, (k + 1) * RH)
            dxn = _dot_nt(piece(du_ref, 0, rows), w_ref[0])
            for s in range(1, slots):
                dxn = dxn + _dot_nt(piece(du_ref, s, rows), w_ref[s])
            return dxn

        ahead = products(0)
        for k in range(TM // RH):
            rows = slice(k * RH, (k + 1) * RH)
            dxn = ahead
            if (k + 1) * RH < TM:
                ahead = products(k + 1)
            x = h_ref[rows, :]
            r = _rms(x)
            xh = x * r
            dg_ref[0:1, :] += jnp.sum(dxn * xh, axis=0, keepdims=True)
            dxh = dxn * g_ref[...]
            dh_ref[rows, :] = di_ref[rows, :] + r * (dxh - xh * jnp.mean(dxh * xh, axis=-1, keepdims=True))

    return pl.pallas_call(
        body, name=name, grid=(T // TM,),
        in_specs=[pl.BlockSpec((2, TM, N // 2), lambda i: (0, i, 0)) if split else _row(TM, N),
                  _weight((slots, D, ns)), _row(TM, D), _layer((1, D), gl), _row(TM, D)],
        out_specs=[_row(TM, D), _const((8, D))],
        out_shape=[jax.ShapeDtypeStruct((T, D), F32), jax.ShapeDtypeStruct((8, D), F32)],
        compiler_params=_cp("arbitrary"),
    )(du, w, h, g, dh_in)


def mmT(dh, w, name):
    T = dh.shape[0]
    N = w.shape[0]

    def body(dh_ref, w_ref, o_ref):
        o_ref[...] = _dot_nt(dh_ref[...].astype(BF16), w_ref[...]).astype(BF16)

    return pl.pallas_call(
        body, name=name, grid=(T // TM,),
        in_specs=[_row(TM, D), _weight((N, D))],
        out_specs=_row(TM, N), out_shape=jax.ShapeDtypeStruct((T, N), BF16),
        compiler_params=_cp("parallel"),
    )(dh, w)


def mmT_lnbwd(dh, w, y, sm, l, name):
    T = dh.shape[0]

    def body(dh_ref, w_ref, y_ref, sm_ref, dy_ref, st_ref):
        i = pl.program_id(0)

        @pl.when(i == 0)
        def _():
            st_ref[...] = jnp.zeros_like(st_ref)

        RH = TM // 4

        def product(k):
            return _dot_nt(dh_ref[k * RH:(k + 1) * RH, :].astype(BF16), w_ref[...])

        gam = sm_ref[32:33, :]
        ahead = product(0)
        for k in range(TM // RH):
            rows = slice(k * RH, (k + 1) * RH)
            ds = ahead
            if (k + 1) * RH < TM:
                ahead = product(k + 1)
            y = y_ref[rows, :].astype(F32)
            mu = jnp.mean(y, axis=-1, keepdims=True)
            yc = y - mu
            rstd = lax.rsqrt(jnp.mean(yc * yc, axis=-1, keepdims=True) + EPS)
            xh = yc * rstd
            z = xh * gam + sm_ref[33:34, :]
            sg = _sigmoid(z)
            dz = ds * sg * (1.0 + z * (1.0 - sg))
            st_ref[0:1, :] += jnp.sum(dz * xh, axis=0, keepdims=True)
            st_ref[1:2, :] += jnp.sum(dz, axis=0, keepdims=True)
            dxh = dz * gam
            dy = rstd * (dxh - jnp.mean(dxh, axis=-1, keepdims=True) - xh * jnp.mean(dxh * xh, axis=-1, keepdims=True))
            st_ref[2:3, :] += jnp.sum(dy, axis=0, keepdims=True)
            dy_ref[rows, :] = dy.astype(BF16)

    return pl.pallas_call(
        body, name=name, grid=(T // TM,),
        in_specs=[_row(TM, D), _weight((D, D)), _row(TM, D), _layer((40, D), l)],
        out_specs=[_row(TM, D), _const((8, D))],
        out_shape=[jax.ShapeDtypeStruct((T, D), BF16), jax.ShapeDtypeStruct((8, D), F32)],
        compiler_params=_cp("arbitrary"),
    )(dh, w, y, sm)


CH = 512


def dwconv_glu_bwd(dy, a, u, sm, smrev, l, name):
    T = dy.shape[0]
    nr, nc = T // TCV, D // CH
    nb = TCV // HALO
    last = T // HALO - 1

    def body(dy_ref, dyn_ref, a_ref, ap_ref, u1_ref, u2_ref, sm_ref, rev_ref, du_ref, dw_ref, shd, sha, da):
        i = pl.program_id(0)
        r = i % nr

        @pl.when(r == 0)
        def _():
            dw_ref[...] = jnp.zeros_like(dw_ref)

        shd[0, 0:TCV, :] = dy_ref[...].astype(F32)
        shd[0, TCV:TCV + HALO, :] = jnp.where(r < nr - 1, dyn_ref[...].astype(F32), 0.0)
        sha[0, 0:HALO, :] = jnp.where(r > 0, ap_ref[...].astype(F32), 0.0)
        sha[0, HALO:HALO + TCV, :] = a_ref[...].astype(F32)
        _make_shifts(shd)
        _make_shifts(sha)
        _conv_taps(shd, rev_ref, da, 0)
        for k in range(CONV_W):
            part = jnp.zeros((SUB, CH), F32)
            for r0 in range(0, TCV, SUB):
                part = part + shd[0, r0:r0 + SUB, :] * _shifted(sha, HALO - (CONV_W - 1) + k + r0, SUB, slice(None))
            dw_ref[k:k + 1, :] += jnp.sum(part, axis=0, keepdims=True)
        dav = da[...]
        u1 = u1_ref[...].astype(F32)
        sg = _sigmoid(u2_ref[...].astype(F32))
        du_ref[0] = (dav * sg).astype(BF16)
        du_ref[1] = (dav * u1 * sg * (1.0 - sg)).astype(BF16)

    tile = lambda i: (i % nr, i // nr)
    in_specs = [pl.BlockSpec((TCV, CH), tile),
                pl.BlockSpec((HALO, CH), lambda i: (jnp.minimum((i % nr + 1) * nb, last), i // nr)),
                pl.BlockSpec((TCV, CH), tile),
                pl.BlockSpec((HALO, CH), lambda i: (jnp.maximum((i % nr) * nb - 1, 0), i // nr)),
                pl.BlockSpec((TCV, CH), tile), pl.BlockSpec((TCV, CH), lambda i: (i % nr, nc + i // nr)),
                pl.BlockSpec((None, 40, CH), lambda i: (l, 0, i // nr)),
                pl.BlockSpec((None, 40, CH), lambda i: (l, 0, i // nr))]
    return pl.pallas_call(
        body, name=name, grid=(nr * nc,), in_specs=in_specs,
        out_specs=[pl.BlockSpec((2, TCV, CH), lambda i: (0, i % nr, i // nr)),
                   pl.BlockSpec((32, CH), lambda i: (0, i // nr))],
        out_shape=[jax.ShapeDtypeStruct((2, T, D), BF16), jax.ShapeDtypeStruct((32, D), F32)],
        scratch_shapes=[pltpu.VMEM((SUB, TCV + HALO, CH), F32), pltpu.VMEM((SUB, TCV + HALO, CH), F32),
                        pltpu.VMEM((TCV, CH), F32)],
        compiler_params=_cp("arbitrary"),
    )(dy, dy, a, a, u, u, sm, smrev)


def _rows_tile(R):
    for t in (512, 256, 128, 64, 32, 16, 8):
        if R % t == 0:
            return t
    return R


def add8_into(J, l, g, others, where, name):
    R, C = g.shape[2:]
    tr = R // 2

    def body(w_ref, g_ref, x_ref, j_in, j_ref):
        acc = g_ref[...].astype(F32)
        for k in range(7):
            acc = acc + x_ref[k].astype(F32)
        j_ref[...] = acc

    return pl.pallas_call(
        body, name=name,
        grid_spec=pltpu.PrefetchScalarGridSpec(
            num_scalar_prefetch=1, grid=(R // tr,),
            in_specs=[pl.BlockSpec((None, None, tr, C), lambda i, w: (w[0], w[1], i, 0)),
                      pl.BlockSpec((7, tr, C), lambda i, w: (0, i, 0)), ANY],
            out_specs=pl.BlockSpec((None, None, tr, C), lambda i, w: (l, w[1], i, 0))),
        out_shape=jax.ShapeDtypeStruct(J.shape, F32), input_output_aliases={3: 0},
        compiler_params=_cp("parallel"),
    )(where, g, others, J)


def adamw(w, g, m, v, name, copy_g=False):
    R, C = w.shape
    tr = _rows_tile(R)

    def body(w_ref, g_ref, m_ref, v_ref, *outs):
        d_ref, nm_ref, nv_ref = outs[-3:]
        gv = g_ref[...]
        if copy_g:
            outs[0][...] = gv
        nm = ADAM_B1 * m_ref[...] + (1.0 - ADAM_B1) * gv
        nv = ADAM_B2 * v_ref[...] + (1.0 - ADAM_B2) * (gv * gv)
        m_hat = nm / (1.0 - ADAM_B1 ** ADAM_STEP)
        v_hat = nv / (1.0 - ADAM_B2 ** ADAM_STEP)
        d_ref[...] = -ADAM_LR * (m_hat / (jnp.sqrt(v_hat) + ADAM_EPS) + ADAM_WD * w_ref[...])
        nm_ref[...] = nm
        nv_ref[...] = nv

    sd = jax.ShapeDtypeStruct((R, C), F32)
    n_out = 4 if copy_g else 3
    return pl.pallas_call(
        body, name=name, grid=(R // tr,),
        in_specs=[_row(tr, C)] * 4, out_specs=[_row(tr, C)] * n_out, out_shape=[sd] * n_out,
        compiler_params=_cp("parallel"),
    )(w, g, m, v)


ANY = pl.BlockSpec(memory_space=pl.ANY)
HBM = pl.BlockSpec(memory_space=pltpu.HBM)
SEM = pl.BlockSpec(memory_space=pltpu.SEMAPHORE)
EFFECT = pltpu.SideEffectType.DATAFLOW_SIDE_EFFECTING


def _place():
    x, y, c = lax.axis_index("x"), lax.axis_index("y"), lax.axis_index("c")
    chips = [(1 - x, y), (x, 1 - y), (1 - x, 1 - y)]
    return x, y, c, chips


def _copy(src, dst, send, recv, k, to):
    return pltpu.make_async_remote_copy(src_ref=src, dst_ref=dst, send_sem=send.at[k], recv_sem=recv.at[k],
                                        device_id=to, device_id_type=MESH)


def xchg_start(name, bufs, plan, n, after=()):
    nb = len(bufs)

    na = len(after)

    def body(*refs):
        send, recv, token = refs[nb + na], refs[nb + na + 1], refs[-1]
        for k, (src, dst, to) in enumerate(plan(refs[:nb])):
            _copy(src, dst, send, recv, k, to).start()
        token[...] = jnp.zeros_like(token)

    outs = pl.pallas_call(
        body, name=name,
        out_shape=(pltpu.SemaphoreType.DMA((n,)), pltpu.SemaphoreType.DMA((n,)),
                   *[pltpu.HBM(b.shape, b.dtype) for b in bufs], jax.ShapeDtypeStruct((8, 128), F32)),
        in_specs=[HBM] * nb + [ANY] * na,
        out_specs=(SEM, SEM, *[HBM] * nb, pl.BlockSpec(memory_space=pltpu.VMEM)),
        input_output_aliases={i: 2 + i for i in range(nb)},
        compiler_params=pltpu.CompilerParams(has_side_effects=EFFECT),
    )(*[pltpu.with_memory_space_constraint(b, pltpu.HBM) for b in bufs], *after)
    return dict(name=name, send=outs[0], recv=outs[1], bufs=list(outs[2:2 + nb]), plan=plan), outs[-1]


def xchg_wait(flight, after):
    bufs, plan = flight["bufs"], flight["plan"]
    nb = len(bufs)

    def body(*refs):
        send, recv = refs[nb], refs[nb + 1]
        for k, (src, dst, to) in enumerate(plan(refs[:nb])):
            cp = _copy(src, dst, send, recv, k, to)
            cp.wait_send()
            cp.wait_recv()

    outs = pl.pallas_call(
        body, name=flight["name"] + "_wait",
        out_shape=tuple(pltpu.HBM(b.shape, b.dtype) for b in bufs),
        in_specs=[HBM] * nb + [SEM, SEM] + [ANY] * len(after),
        out_specs=tuple([HBM] * nb), input_output_aliases={i: i for i in range(nb)},
        compiler_params=pltpu.CompilerParams(has_side_effects=EFFECT),
    )(*bufs, flight["send"], flight["recv"], *after)
    return list(outs)


def _flip(k, x, y, c):
    return ((1 - x) if k & 4 else x, (1 - y) if k & 2 else y, (1 - c) if k & 1 else c)


class WeightGather:
    def __init__(self, shard, groups):
        me = 2 * lax.axis_index("x") + lax.axis_index("y")
        self.names = dict(groups)
        self.ici, self.d2d = {}, {}
        self.token = None
        for gname, names in groups:
            nt = len(names)
            srcs = [shard(n, self.token) for n in names]
            lands = [lax.dynamic_update_slice(lax.empty((4,) + s.shape, s.dtype), s[None], (me, 0, 0, 0))
                     for s in srcs]

            def plan(refs, nt=nt):
                x, y, c, chips = _place()
                return [(refs[t].at[c], refs[nt + t].at[2 * x + y, c], (cx, cy, c))
                        for t in range(nt) for cx, cy in chips]

            self.ici[gname], self.token = xchg_start(f"ag_ici_{gname}", srcs + lands, plan, 3 * nt,
                                                     after=[] if self.token is None else [self.token])

    def forward(self, gname, after):
        nt = len(self.names[gname])
        lands = xchg_wait(self.ici.pop(gname), after)[nt:]

        def plan(refs):
            x, y, c, chips = _place()
            out = []
            for t in range(nt):
                for cx, cy in chips:
                    piece = refs[t].at[2 * cx + cy, c]
                    out.append((piece, piece, (x, y, 1 - c)))
            return out

        self.d2d[gname], token = xchg_start(f"ag_d2d_{gname}", lands, plan, 3 * nt)
        return token

    def get(self, gname, after):
        lands = xchg_wait(self.d2d.pop(gname), after)
        return dict(zip(self.names[gname], lands))


class GradReduce:
    def __init__(self, kinds):
        self.J = {k: lax.empty((L, 2, a2, b), F32) for k, (L, a2, b) in kinds.items()}
        self.x, self.j = {}, {}

    @staticmethod
    def _where(name):
        kind, _, l = name.partition("_")
        return kind, int(l or 0)

    def send(self, gname, grads, after=()):
        names = list(grads)
        nt = len(names)
        gs = [grads[n] for n in names]
        xs = [lax.empty((7,) + g.shape[2:], g.dtype) for g in gs]

        def plan(refs):
            x, y, c, _ = _place()
            out = []
            for t in range(nt):
                for k in range(1, 8):
                    px, py, pc = _flip(k, x, y, c)
                    out.append((refs[t].at[2 * px + py, pc], refs[nt + t].at[k - 1], (px, py, pc)))
            return out

        flight, token = xchg_start(f"rs_x_{gname}", gs + xs, plan, 7 * nt, after=after)
        self.x[gname] = (names, flight)
        return token

    def reduce(self, gname, after):
        names, flight = self.x.pop(gname)
        nt = len(names)
        bufs = xchg_wait(flight, after)
        mine = jnp.stack([2 * lax.axis_index("x") + lax.axis_index("y"), lax.axis_index("c")]).astype(jnp.int32)
        where = [self._where(n) for n in names]
        js = [add8_into(self.J[kind], l, bufs[t], bufs[nt + t], mine, f"rs_add_{names[t]}")
              for t, (kind, l) in enumerate(where)]

        def plan(refs):
            x, y, c, _ = _place()
            out = []
            for t in range(nt):
                half = refs[t].at[where[t][1], c]
                out.append((half, half, (x, y, 1 - c)))
            return out

        flight, token = xchg_start(f"rs_join_{gname}", js, plan, nt)
        self.j[gname] = (where, flight)
        return token

    def finish(self, gname, after):
        where, flight = self.j.pop(gname)
        for (kind, _), j in zip(where, xchg_wait(flight, after)):
            self.J[kind] = j


def small_allreduce_start(v, after):
    me = 4 * lax.axis_index("x") + 2 * lax.axis_index("y") + lax.axis_index("c")
    land = lax.dynamic_update_slice(lax.empty((8,) + v.shape, v.dtype), v[None], (me, 0, 0))

    def plan(refs):
        x, y, c, _ = _place()
        return [(refs[0], refs[1].at[4 * x + 2 * y + c], _flip(k, x, y, c)) for k in range(1, 8)]

    return xchg_start("small_allreduce", [v, land], plan, 7, after=after)


def sum8(all8, name):
    def body(x_ref, o_ref):
        acc = x_ref[0]
        for d in range(1, 8):
            acc = acc + x_ref[d]
        o_ref[...] = acc

    return pl.pallas_call(
        body, name=name,
        in_specs=[pl.BlockSpec(memory_space=pltpu.VMEM)], out_specs=pl.BlockSpec(memory_space=pltpu.VMEM),
        out_shape=jax.ShapeDtypeStruct(all8.shape[1:], F32),
        compiler_params=pltpu.CompilerParams(vmem_limit_bytes=VMEM_LIMIT),
    )(all8)


AG_GROUPS = (("a0", ("pw1_0", "pw2_0", "small")), ("f0", ("up_0", "down_0")),
             ("l1", ("pw1_1", "pw2_1", "up_1", "down_1")), ("l2", ("kv", "wq_0", "wo_0", "up_2", "down_2")),
             ("l3", ("wq_1", "wo_1", "up_3", "down_3")))


def _bucket_table():
    qi = np.arange(BLK)[:, None]
    kj = np.arange(2 * BLK)[None, :]
    d = np.maximum(qi + BLK - kj, 0)
    max_exact = N_BUCKETS // 2
    log_ratio = (np.log(np.maximum(d, 1).astype(np.float32) / np.float32(max_exact))
                 / np.float32(math.log(MAX_DISTANCE / max_exact))).astype(np.float32)
    large = max_exact + (log_ratio * np.float32(N_BUCKETS - max_exact)).astype(np.int32)
    large = np.minimum(large, N_BUCKETS - 1)
    return np.where(d < max_exact, d, large).astype(np.int32)


def _heads_major(a, nh):
    T = a.shape[0]
    return a.reshape(T, nh, HD).transpose(1, 0, 2)


def _heads_minor(a):
    nh, T, _ = a.shape
    return a.transpose(1, 0, 2).reshape(T, nh * HD)


def _slots(land):
    return land.reshape(4, 2 * land.shape[2], land.shape[3])


def _rows(land):
    return land.reshape(8 * land.shape[2], land.shape[3])


def _gview(g):
    s, K, n = g.shape
    return g.reshape(4, 2, K // 2, n) if s == 4 else g.reshape(4, 2, K // 8, n)


def _gate(a, token):
    return a * (1.0 + token[0, 0])


def _conv_small(f_small):
    fs = f_small.transpose(1, 2, 0, 3).reshape(2, 40, D)
    b_pw1 = f_small[:, :, 35:37, :].transpose(1, 0, 2, 3).reshape(2, 1, 2 * D)
    rev = jnp.concatenate([fs[:, CONV_W - 1::-1], jnp.zeros((2, 40 - CONV_W, D), F32)], axis=1)
    return dict(conv=fs, conv_rev=rev, b_pw1=b_pw1, b_pw2=fs[:, 34:35])


def run_step(x, target, P, ag, rs):
    T = x.shape[0]
    zero = jnp.zeros((1, 1, D), F32)
    nm, nf = P["norm_mix"], P["norm_ffn"]
    ag.forward("a0", [ag.token])
    W = ag.get("a0", [])
    sm = _conv_small(W["small"])
    h = x
    saved = []
    for l in range(2):
        xn, u, a = norm_mm_glu(h, nm, l, _slots(W[f"pw1_{l}"]), sm["b_pw1"], f"f_pw1_{l}")
        y, s = dwconv_ln_silu(a, sm["conv"], l, f"f_conv_{l}")
        b2 = sm["b_pw2"]
        if l == 0:
            b2 = _gate(b2, ag.forward("f0", [s]))
        h1 = mm_bias_res(s, _rows(W[f"pw2_{l}"]), b2, l, h, f"f_pw2_{l}")
        if l == 0:
            W.update(ag.get("f0", [h1]))
        xn2, gu, f = norm_mm_swiglu(h1, nf, l, _slots(W[f"up_{l}"]), f"f_up_{l}")
        nxt = "l1" if l == 0 else "l2"
        h2 = mm_bias_res(f, _rows(W[f"down_{l}"]), _gate(zero, ag.forward(nxt, [f])), 0, h1, f"f_down_{l}")
        W.update(ag.get(nxt, [h2]))
        saved.append(dict(h=h, xn=xn, u=u, a=a, y=y, s=s, h1=h1, xn2=xn2, gu=gu, f=f))
        h = h2
    h_kv = h
    kvn, kv = norm_mm(h, P["norm_kv"], 0, _rows(W["kv"]), "f_kv")
    kp = jnp.pad(_heads_major(kv[:, :N_KV * HD], N_KV), ((0, 0), (BLK, 0), (0, 0)))
    vp = jnp.pad(_heads_major(kv[:, N_KV * HD:], N_KV), ((0, 0), (BLK, 0), (0, 0)))
    kvt = jnp.pad(kv.T.reshape(2, N_KV, HD, T), ((0, 0), (0, 0), (0, 0), (BLK, 0)))
    kt, vt = kvt[0], kvt[1]
    bucket = _bucket_table()
    onehot = jnp.asarray(np.eye(N_BUCKETS, dtype=np.float32)[bucket])
    bias = jnp.einsum("qkb,bh->hkq", onehot, P["rel_bias"], precision=lax.Precision.HIGHEST)
    bias = bias.reshape(N_KV, GROUP, 2 * BLK, BLK).transpose(0, 2, 1, 3).reshape(1, N_KV, 2 * BLK, QW)
    bias = bias + jnp.asarray(band_mask())[:, None]
    for j in range(2):
        l = 2 + j
        xn, q = norm_mm(h, nm, l, _rows(W[f"wq_{j}"]), f"f_q_{j}", scale=HD ** -0.5)
        qh = q.T.reshape(N_KV, GROUP, HD, T)
        sink = jnp.broadcast_to(P["sinks"][j].reshape(N_KV, GROUP, 1), (N_KV, GROUP, BLK)).reshape(N_KV, 1, QW)
        oh = attn_fwd(qh, kp, vt, bias, sink, f"f_attn_{j}")
        attn = oh.reshape(N_HEADS * HD, T).T
        h1 = mm_bias_res(attn, _rows(W[f"wo_{j}"]), zero, 0, h, f"f_wo_{j}")
        xn2, gu, f = norm_mm_swiglu(h1, nf, l, _slots(W[f"up_{l}"]), f"f_up_{l}")
        zg = _gate(zero, ag.forward("l3", [f])) if j == 0 else zero
        h2 = mm_bias_res(f, _rows(W[f"down_{l}"]), zg, 0, h1, f"f_down_{l}")
        if j == 0:
            W.update(ag.get("l3", [h2]))
        saved.append(dict(h=h, xn=xn, qh=qh, oh=oh, sink=sink, attn=attn, h1=h1, xn2=xn2, gu=gu, f=f))
        h = h2

    dh, st_final = final_loss(h, P["norm_final"], target, "loss_head")

    S = dict(norm_ffn=[None] * 4, norm_mix=[None] * 4, conv=[None] * 2, taps=[None] * 2, b_pw1=[None] * 2,
             b_pw2=[None] * 2, sinks=[None] * 2)

    def ffn_bwd(dh, sv, l, nf, after=()):
        du = mmT_swiglu_bwd(dh, _rows(W[f"down_{l}"]), sv["gu"], f"b_down_{l}", after)
        gd = mm_dw(sv["f"], dh, f"w_down_{l}", 512, 1)
        gu = mm_dw(sv["xn2"], du, f"w_up_{l}", DFF // 2, 4)
        dh, dg = mmT_rmsbwd(du, _slots(W[f"up_{l}"]), sv["h1"], nf, l, dh, f"b_up_{l}")
        S["norm_ffn"][l] = dg[0]
        return dh, {f"down_{l}": _gview(gd), f"up_{l}": _gview(gu)}

    dk = dv = dbias = None
    sent = []
    for j in (1, 0):
        l = 2 + j
        sv = saved[l]
        dh, grads = ffn_bwd(dh, sv, l, nf, sent)
        dattn = mmT(dh, _rows(W[f"wo_{j}"]), f"b_wo_{j}")
        grads[f"wo_{j}"] = _gview(mm_dw(sv["attn"], dh, f"w_wo_{j}", 512, 1))
        doh = dattn.T.reshape(N_KV, GROUP, HD, T)
        dqh, dkj, dvj, dbj, dsj = attn_bwd(sv["qh"], kp, kt, vp, bias, sv["sink"], sv["oh"], doh, f"b_attn_{j}")
        dq = dqh.reshape(N_HEADS * HD, T).T
        grads[f"wq_{j}"] = _gview(mm_dw(sv["xn"], dq, f"w_q_{j}", 512, 1))
        dh, dg = mmT_rmsbwd(dq, _rows(W[f"wq_{j}"])[None], sv["h"], nm, l, dh, f"b_q_{j}")
        S["norm_mix"][l] = dg[0]
        S["sinks"][j] = jnp.sum(dsj.reshape(N_HEADS, BLK), axis=1)
        dk = dkj if dk is None else dk + dkj
        dv = dvj if dv is None else dv + dvj
        dbias = dbj if dbias is None else dbias + dbj
        if j == 1:
            sent = [rs.send("l3", grads)]

    dkv = jnp.concatenate([_heads_minor(dk[:, BLK:]), _heads_minor(dv[:, BLK:])], axis=1).astype(BF16)
    grads["kv"] = _gview(mm_dw(kvn, dkv, "w_kv", 512, 1))
    dh, dg = mmT_rmsbwd(dkv, _rows(W["kv"])[None], h_kv, P["norm_kv"], 0, dh, "b_kv")
    S["norm_kv"] = dg[0]
    dbh = dbias.reshape(N_KV, 2 * BLK, GROUP, BLK)
    S["rel_bias"] = jnp.einsum("vkgq,qkb->bvg", dbh, onehot, precision=lax.Precision.HIGHEST).reshape(N_BUCKETS, N_HEADS)
    sent = [rs.send("l2", grads)]
    nf = _gate(nf, rs.reduce("l3", [dh]))

    for l in (1, 0):
        sv = saved[l]
        dh, grads = ffn_bwd(dh, sv, l, nf, sent)
        conv = sm["conv"]
        if l == 0:
            conv = _gate(conv, rs.send("f0", grads))
            grads = {}
        dy, st = mmT_lnbwd(dh, _rows(W[f"pw2_{l}"]), sv["y"], conv, l, f"b_pw2_{l}")
        g2, S["b_pw2"][l] = mm_dw(sv["s"], dh, f"w_pw2_{l}", 512, 1, colsum=True)
        du, dtaps = dwconv_glu_bwd(dy, sv["a"], sv["u"], sm["conv"], sm["conv_rev"], l, f"b_conv_{l}")
        S["conv"][l] = st[0:3]
        S["taps"][l] = dtaps[0:CONV_W]
        if l == 0:
            rs.finish("l2", [du])
            nm = _gate(nm, rs.reduce("l1", [du]))
        g1, S["b_pw1"][l] = mm_dw(sv["xn"], du, f"w_pw1_{l}", 512, 4, colsum=True)
        grads[f"pw2_{l}"], grads[f"pw1_{l}"] = _gview(g2), _gview(g1)
        dh, dg = mmT_rmsbwd(du, _slots(W[f"pw1_{l}"]), sv["h"], nm, l, dh, f"b_pw1_{l}")
        S["norm_mix"][l] = dg[0]
        if l == 1:
            sent = [rs.send("l1", grads)]
            rs.finish("l3", [dh])
            nf = _gate(nf, rs.reduce("l2", [dh]))
    S["norm_final"] = st_final[0]
    S["loss"] = st_final[1]
    return grads, dh, S


R_CONV = 37
R_SMALL = 88


def _pack_small(S):
    rows = []
    for l in range(2):
        rows += [S["taps"][l], S["conv"][l][2:3], S["conv"][l][0:2], S["b_pw2"][l], S["b_pw1"][l].reshape(2, D)]
    rows += [jnp.stack(S["norm_mix"]), jnp.stack(S["norm_ffn"]), S["norm_kv"][None], S["norm_final"][None]]
    tail = jnp.concatenate([jnp.stack(S["sinks"]).reshape(-1), S["rel_bias"].reshape(-1)])
    rows += [jnp.pad(tail, (0, D - tail.shape[0]))[None], S["loss"][None]]
    v = jnp.concatenate(rows, axis=0)
    return jnp.pad(v, ((0, R_SMALL - v.shape[0]), (0, 0)))


def kernel(x, norm_mix, norm_ffn, conv_w_pw1, conv_b_pw1, conv_w_dw, conv_b_dw, conv_ln_g, conv_ln_b, conv_w_pw2, conv_b_pw2, norm_kv, w_kv, w_q, w_o, sinks, rel_bias, ffn_w_up, ffn_w_down, norm_final, loss_target, m_norm_mix, m_norm_ffn, m_conv_w_pw1, m_conv_b_pw1, m_conv_w_dw, m_conv_b_dw, m_conv_ln_g, m_conv_ln_b, m_conv_w_pw2, m_conv_b_pw2, m_norm_kv, m_w_kv, m_w_q, m_w_o, m_sinks, m_rel_bias, m_ffn_w_up, m_ffn_w_down, m_norm_final, v_norm_mix, v_norm_ffn, v_conv_w_pw1, v_conv_b_pw1, v_conv_w_dw, v_conv_b_dw, v_conv_ln_g, v_conv_ln_b, v_conv_w_pw2, v_conv_b_pw2, v_norm_kv, v_w_kv, v_w_q, v_w_o, v_sinks, v_rel_bias, v_ffn_w_up, v_ffn_w_down, v_norm_final):
    me = 2 * lax.axis_index("x") + lax.axis_index("y")
    weights = dict(norm_mix=norm_mix, norm_ffn=norm_ffn, conv_w_pw1=conv_w_pw1, conv_b_pw1=conv_b_pw1,
                   conv_w_dw=conv_w_dw, conv_b_dw=conv_b_dw, conv_ln_g=conv_ln_g, conv_ln_b=conv_ln_b,
                   conv_w_pw2=conv_w_pw2, conv_b_pw2=conv_b_pw2, norm_kv=norm_kv, w_kv=w_kv, w_q=w_q, w_o=w_o,
                   sinks=sinks, rel_bias=rel_bias, ffn_w_up=ffn_w_up, ffn_w_down=ffn_w_down, norm_final=norm_final)
    mom_m = dict(norm_mix=m_norm_mix, norm_ffn=m_norm_ffn, conv_w_pw1=m_conv_w_pw1, conv_b_pw1=m_conv_b_pw1,
                 conv_w_dw=m_conv_w_dw, conv_b_dw=m_conv_b_dw, conv_ln_g=m_conv_ln_g, conv_ln_b=m_conv_ln_b,
                 conv_w_pw2=m_conv_w_pw2, conv_b_pw2=m_conv_b_pw2, norm_kv=m_norm_kv, w_kv=m_w_kv, w_q=m_w_q,
                 w_o=m_w_o, sinks=m_sinks, rel_bias=m_rel_bias, ffn_w_up=m_ffn_w_up, ffn_w_down=m_ffn_w_down,
                 norm_final=m_norm_final)
    mom_v = dict(norm_mix=v_norm_mix, norm_ffn=v_norm_ffn, conv_w_pw1=v_conv_w_pw1, conv_b_pw1=v_conv_b_pw1,
                 conv_w_dw=v_conv_w_dw, conv_b_dw=v_conv_b_dw, conv_ln_g=v_conv_ln_g, conv_ln_b=v_conv_ln_b,
                 conv_w_pw2=v_conv_w_pw2, conv_b_pw2=v_conv_b_pw2, norm_kv=v_norm_kv, w_kv=v_w_kv, w_q=v_w_q,
                 w_o=v_w_o, sinks=v_sinks, rel_bias=v_rel_bias, ffn_w_up=v_ffn_w_up, ffn_w_down=v_ffn_w_down,
                 norm_final=v_norm_final)

    big = {"conv_w_pw1": "pw1", "conv_w_pw2": "pw2", "w_q": "wq", "w_o": "wo", "ffn_w_up": "up",
           "ffn_w_down": "down", "w_kv": "kv"}
    of_kind = {k: n for n, k in big.items()}

    def shard(name, token):
        if name == "small":
            a = jnp.concatenate(
                [conv_w_dw, conv_b_dw[:, None], conv_ln_g[:, None], conv_ln_b[:, None], conv_b_pw2[:, None],
                 conv_b_pw1.reshape(2, 2, 256), jnp.zeros((2, 3, 256), F32)], axis=1)
            return a if token is None else _gate(a, token)
        kind, _, l = name.partition("_")
        a = weights[of_kind[kind]]
        a = a[int(l)] if l else a
        if token is not None:
            a = _gate(a, token)
        return a.astype(BF16).reshape(2, a.shape[0] // 2, a.shape[1])

    ag = WeightGather(shard, AG_GROUPS)
    rs = GradReduce({"pw1": (2, 512, 512), "pw2": (2, 128, D), "wq": (2, 128, D), "wo": (2, 128, D),
                     "up": (4, 512, DFF // 2), "down": (4, DFF // 8, D), "kv": (1, 128, 512)})

    P = dict(norm_mix=norm_mix[:, None], norm_ffn=norm_ffn[:, None], norm_kv=norm_kv[None, None],
             norm_final=norm_final[None], sinks=sinks, rel_bias=rel_bias)
    last, grad_x, S = run_step(x[0], loss_target[0], P, ag, rs)

    rs.finish("l1", [grad_x])
    small_flight, token = small_allreduce_start(_gate(_pack_small(S), rs.reduce("f0", [grad_x])), [])
    token = rs.send("c0", last, after=[token])
    delta, new_m, new_v, big_grads = {}, {}, {}, {}

    def update(n):
        shp = weights[n].shape
        r2 = (int(np.prod(shp[:-1])), shp[-1])
        g, d, nm, nv = adamw(weights[n].reshape(r2), rs.J[big[n]].reshape(r2), mom_m[n].reshape(r2),
                             mom_v[n].reshape(r2), f"adamw_{n}", copy_g=True)
        big_grads[n], delta[n], new_m[n], new_v[n] = g.reshape(shp), d.reshape(shp), nm.reshape(shp), nv.reshape(shp)

    rs.finish("f0", [token])
    for n in ("ffn_w_up", "ffn_w_down"):
        update(n)
    vsum = sum8(xchg_wait(small_flight, [delta["ffn_w_up"], delta["ffn_w_down"]])[1], "small_sum")

    col = lambda a: lax.dynamic_slice_in_dim(a, me * 256, 256, axis=-1)
    grads = {}
    for l in range(2):
        base = l * R_CONV
        grads.setdefault("conv_w_dw", []).append(col(vsum[base:base + 31]))
        grads.setdefault("conv_b_dw", []).append(col(vsum[base + 31]))
        grads.setdefault("conv_ln_g", []).append(col(vsum[base + 32]))
        grads.setdefault("conv_ln_b", []).append(col(vsum[base + 33]))
        grads.setdefault("conv_b_pw2", []).append(col(vsum[base + 34]))
        grads.setdefault("conv_b_pw1", []).append(
            lax.dynamic_slice_in_dim(vsum[base + 35:base + 37].reshape(2 * D), me * 512, 512, axis=0))
    grads = {k: jnp.stack(v) for k, v in grads.items()}
    base = 2 * R_CONV
    grads["norm_mix"] = vsum[base:base + 4]
    grads["norm_ffn"] = vsum[base + 4:base + 8]
    grads["norm_kv"] = vsum[base + 8]
    grads["norm_final"] = vsum[base + 9]
    grads["sinks"] = vsum[base + 10, 0:32].reshape(2, 16)
    grads["rel_bias"] = vsum[base + 10, 32:32 + 512].reshape(32, 16)
    loss = vsum[base + 11, 0]

    for n in weights:
        if n not in big:
            shp = weights[n].shape
            r2 = (int(np.prod(shp[:-1])), shp[-1])
            d, nm, nv = adamw(weights[n].reshape(r2), grads[n].reshape(r2), mom_m[n].reshape(r2),
                              mom_v[n].reshape(r2), f"adamw_{n}")
            delta[n], new_m[n], new_v[n] = d.reshape(shp), nm.reshape(shp), nv.reshape(shp)

    rs.reduce("c0", [vsum])
    for n in ("w_q", "w_o", "w_kv"):
        update(n)
    rs.finish("c0", [delta["w_kv"]])
    for n in ("conv_w_pw1", "conv_w_pw2"):
        update(n)
    grads.update(big_grads)

    order = list(weights)
    return (loss, grad_x[None], *[grads[n] for n in order], *[delta[n] for n in order],
            *[new_m[n] for n in order], *[new_v[n] for n in order])
```

```python
import functools
import math

import numpy as np
import jax
import jax.numpy as jnp
from jax import lax
from jax.experimental import pallas as pl
from jax.experimental.pallas import tpu as pltpu

F32 = jnp.float32
BF16 = jnp.bfloat16
MESH = pl.DeviceIdType.MESH

D = 1024
DFF = 2816
N_HEADS = 16
N_KV = 4
GROUP = 4
HD = 64
BLK = 128
CONV_W = 31
HALO = 32
N_BUCKETS = 32
MAX_DISTANCE = 128
EPS = 1e-6
NEG_INF = -1e30
TM = 512
TCV = 256
VMEM_LIMIT = 56 * 2 ** 20

ADAM_LR, ADAM_B1, ADAM_B2, ADAM_EPS, ADAM_WD, ADAM_STEP = 0.001, 0.9, 0.999, 1e-08, 0.01, 10


def _cp(*sem):
    return pltpu.CompilerParams(dimension_semantics=sem, vmem_limit_bytes=VMEM_LIMIT)


def _sigmoid(x):
    return 1.0 / (1.0 + jnp.exp(-x))


def _row(tm, n):
    return pl.BlockSpec((tm, n), lambda i: (i, 0))


def _const(shape):
    nd = len(shape)
    return pl.BlockSpec(shape, lambda i: (0,) * nd)


def _weight(shape):
    nd = len(shape)
    return pl.BlockSpec(shape, lambda i: (0,) * nd, pipeline_mode=pl.Buffered(1))


def _layer(shape, l):
    nd = len(shape)
    return pl.BlockSpec((None,) + tuple(shape), lambda i: (l,) + (0,) * nd)


def _dot(a, b):
    return jnp.dot(a, b, preferred_element_type=F32)


def _dot_nt(a, b):
    return lax.dot_general(a, b, (((1,), (1,)), ((), ())), preferred_element_type=F32)


def _dot_tn(a, b):
    return lax.dot_general(a, b, (((0,), (0,)), ((), ())), preferred_element_type=F32)


def _rms(x):
    return lax.rsqrt(jnp.mean(x * x, axis=-1, keepdims=True) + EPS)


def norm_mm_glu(h, g, l, w, b, name):
    T = h.shape[0]
    ns = w.shape[-1]

    def body(h_ref, g_ref, w_ref, b_ref, xn_ref, u_ref, a_ref):
        x = h_ref[...]
        xn = (x * _rms(x) * g_ref[...]).astype(BF16)
        xn_ref[...] = xn
        for s in range(2):
            lo, hi = s * ns, (s + 1) * ns
            u1 = _dot(xn, w_ref[s]) + b_ref[:, lo:hi]
            u2 = _dot(xn, w_ref[2 + s]) + b_ref[:, D + lo:D + hi]
            u_ref[:, lo:hi] = u1.astype(BF16)
            u_ref[:, D + lo:D + hi] = u2.astype(BF16)
            a_ref[:, lo:hi] = (u1 * _sigmoid(u2)).astype(BF16)

    return pl.pallas_call(
        body, name=name, grid=(T // TM,),
        in_specs=[_row(TM, D), _layer((1, D), l), _weight((4, D, ns)), _layer((1, 2 * D), l)],
        out_specs=[_row(TM, D), _row(TM, 2 * D), _row(TM, D)],
        out_shape=[jax.ShapeDtypeStruct((T, D), BF16), jax.ShapeDtypeStruct((T, 2 * D), BF16),
                   jax.ShapeDtypeStruct((T, D), BF16)],
        compiler_params=_cp("parallel"),
    )(h, g, w, b)


SUB = 8


def _make_shifts(sh):
    n = TCV + HALO - SUB
    for r in range(1, SUB):
        for r0 in range(0, n, 40):
            sh[r, r0:r0 + 40, :] = sh[0, pl.ds(r + r0, 40), :]


def _shifted(sh, off, rows, cols):
    return sh[off % SUB, pl.ds(off - off % SUB, rows), cols]


def _conv_taps(sh, w_ref, out_ref, first):
    RB, LB = 32, 512
    for r0 in range(0, TCV, RB):
        for c0 in range(0, out_ref.shape[1], LB):
            acc = jnp.zeros((RB, LB), F32)
            for k in range(CONV_W):
                acc = acc + w_ref[k:k + 1, c0:c0 + LB] * _shifted(sh, first + k + r0, RB, slice(c0, c0 + LB))
            out_ref[r0:r0 + RB, c0:c0 + LB] = acc


def dwconv_ln_silu(a, sm, l, name):
    T = a.shape[0]
    nb = TCV // HALO

    def body(cur_ref, prev_ref, sm_ref, y_ref, s_ref, sh, yb):
        i = pl.program_id(0)
        sh[0, 0:HALO, :] = jnp.where(i > 0, prev_ref[...].astype(F32), 0.0)
        sh[0, HALO:HALO + TCV, :] = cur_ref[...].astype(F32)
        _make_shifts(sh)
        _conv_taps(sh, sm_ref, yb, HALO - (CONV_W - 1))
        y = yb[...] + sm_ref[31:32, :]
        y_ref[...] = y.astype(BF16)
        mu = jnp.mean(y, axis=-1, keepdims=True)
        yc = y - mu
        rstd = lax.rsqrt(jnp.mean(yc * yc, axis=-1, keepdims=True) + EPS)
        z = yc * rstd * sm_ref[32:33, :] + sm_ref[33:34, :]
        s_ref[...] = (z * _sigmoid(z)).astype(BF16)

    return pl.pallas_call(
        body, name=name, grid=(T // TCV,),
        in_specs=[_row(TCV, D), pl.BlockSpec((HALO, D), lambda i: (jnp.maximum(i * nb - 1, 0), 0)),
                  _layer((40, D), l)],
        out_specs=[_row(TCV, D), _row(TCV, D)],
        out_shape=[jax.ShapeDtypeStruct((T, D), BF16), jax.ShapeDtypeStruct((T, D), BF16)],
        scratch_shapes=[pltpu.VMEM((SUB, TCV + HALO, D), F32), pltpu.VMEM((TCV, D), F32)],
        compiler_params=_cp("parallel"),
    )(a, a, sm)


def mm_bias_res(xb, w, b, bl, res, name):
    T, K = xb.shape

    def body(x_ref, w_ref, b_ref, r_ref, o_ref):
        o_ref[...] = _dot(x_ref[...], w_ref[...]) + b_ref[...] + r_ref[...]

    return pl.pallas_call(
        body, name=name, grid=(T // TM,),
        in_specs=[_row(TM, K), _weight((K, D)), _layer((1, D), bl), _row(TM, D)],
        out_specs=_row(TM, D), out_shape=jax.ShapeDtypeStruct((T, D), F32),
        compiler_params=_cp("parallel"),
    )(xb, w, b, res)


def norm_mm_swiglu(h, g, l, w, name):
    T = h.shape[0]
    ns = w.shape[-1]

    def body(h_ref, g_ref, w_ref, xn_ref, gu_ref, f_ref):
        x = h_ref[...]
        xn = (x * _rms(x) * g_ref[...]).astype(BF16)
        xn_ref[...] = xn
        for s in range(2):
            lo, hi = s * ns, (s + 1) * ns
            gate = _dot(xn, w_ref[s])
            up = _dot(xn, w_ref[2 + s])
            gu_ref[:, lo:hi] = gate.astype(BF16)
            gu_ref[:, DFF + lo:DFF + hi] = up.astype(BF16)
            f_ref[:, lo:hi] = (gate * _sigmoid(gate) * up).astype(BF16)

    return pl.pallas_call(
        body, name=name, grid=(T // TM,),
        in_specs=[_row(TM, D), _layer((1, D), l), _weight((4, D, ns))],
        out_specs=[_row(TM, D), _row(TM, 2 * DFF), _row(TM, DFF)],
        out_shape=[jax.ShapeDtypeStruct((T, D), BF16), jax.ShapeDtypeStruct((T, 2 * DFF), BF16),
                   jax.ShapeDtypeStruct((T, DFF), BF16)],
        compiler_params=_cp("parallel"),
    )(h, g, w)


def norm_mm(h, g, gl, w, name, scale=1.0):
    T = h.shape[0]
    N = w.shape[-1]

    def body(h_ref, g_ref, w_ref, xn_ref, o_ref):
        x = h_ref[...]
        xn = (x * _rms(x) * g_ref[...]).astype(BF16)
        xn_ref[...] = xn
        o_ref[...] = (_dot(xn, w_ref[...]) * scale).astype(BF16)

    return pl.pallas_call(
        body, name=name, grid=(T // TM,),
        in_specs=[_row(TM, D), _layer((1, D), gl), _weight((D, N))],
        out_specs=[_row(TM, D), _row(TM, N)],
        out_shape=[jax.ShapeDtypeStruct((T, D), BF16), jax.ShapeDtypeStruct((T, N), BF16)],
        compiler_params=_cp("parallel"),
    )(h, g, w)


QB = 8
QW = GROUP * BLK


def band_mask():
    qi = np.arange(QW)[None, :] % BLK
    kj = np.arange(2 * BLK)[:, None]
    band = ((kj < BLK) & (kj > qi)) | ((kj >= BLK) & (kj - BLK <= qi))
    first = band & (kj >= BLK)
    return np.where(np.stack([first, band]), 0.0, NEG_INF).astype(np.float32)


def _softmax_cols(s, sink):
    m = jnp.maximum(jnp.max(s, axis=0, keepdims=True), sink)
    p = jnp.exp(s - m)
    es = jnp.exp(sink - m)
    inv = 1.0 / (jnp.sum(p, axis=0, keepdims=True) + es)
    return p, inv, es


def _attn_specs(T):
    W = QB * BLK
    qspec = pl.BlockSpec((None, GROUP, HD, W), lambda kv, n: (kv, 0, 0, n))
    kspec = pl.BlockSpec((None, T + BLK, HD), lambda kv, n: (kv, 0, 0))
    ktspec = [pl.BlockSpec((None, HD, W), lambda kv, n: (kv, 0, n)),
              pl.BlockSpec((None, HD, BLK), lambda kv, n: (kv, 0, (n + 1) * QB))]
    bspec = pl.BlockSpec((2, None, 2 * BLK, QW), lambda kv, n: (0, kv, 0, 0))
    sspec = pl.BlockSpec((None, 1, QW), lambda kv, n: (kv, 0, 0))
    return qspec, kspec, ktspec, bspec, sspec


def _attn_block(n, b):
    blk = n * QB + b
    rows = pl.ds(pl.multiple_of(blk * BLK, BLK), 2 * BLK)
    return rows, (jnp.minimum(blk, 1) if b == 0 else 1)


def _band_cols(main_ref, tail_ref, b):
    if b < QB - 1:
        return main_ref[:, b * BLK:(b + 2) * BLK]
    return jnp.concatenate([main_ref[:, b * BLK:], tail_ref[...]], axis=1)


def _heads_side_by_side(ref, qs):
    return jnp.concatenate([ref[g, :, qs] for g in range(GROUP)], axis=1)


def attn_fwd(q, kp, vt, bias, sink, name):
    T = q.shape[3]
    qspec, kspec, ktspec, bspec, sspec = _attn_specs(T)

    def body(q_ref, k_ref, vt_ref, vtt_ref, b_ref, s_ref, o_ref, pb):
        n = pl.program_id(1)

        def scores(b):
            return _dot(k_ref[_attn_block(n, b)[0], :], _heads_side_by_side(q_ref, slice(b * BLK, (b + 1) * BLK)))

        st_next = scores(0)
        for b in range(QB):
            rows, table = _attn_block(n, b)
            qs = slice(b * BLK, (b + 1) * BLK)
            st = st_next
            if b + 1 < QB:
                st_next = scores(b + 1)
            for g in range(GROUP):
                hs = slice(g * BLK, (g + 1) * BLK)
                p, inv, _ = _softmax_cols(st[:, hs] + b_ref[table, :, hs], s_ref[:, hs])
                pb[:, hs] = (p * inv).astype(BF16)
            ot = _dot(_band_cols(vt_ref, vtt_ref, b), pb[...])
            for g in range(GROUP):
                o_ref[g, :, qs] = ot[:, g * BLK:(g + 1) * BLK].astype(BF16)

    return pl.pallas_call(
        body, name=name, grid=(N_KV, T // (QB * BLK)),
        in_specs=[qspec, kspec, *ktspec, bspec, sspec], out_specs=qspec,
        out_shape=jax.ShapeDtypeStruct((N_KV, GROUP, HD, T), BF16),
        scratch_shapes=[pltpu.VMEM((2 * BLK, QW), BF16)],
        compiler_params=_cp("parallel", "parallel"),
    )(q, kp, vt, vt, bias, sink)


def attn_bwd(q, kp, kt, vp, bias, sink, o, do, name):
    T = q.shape[3]
    qspec, kspec, ktspec, bspec, sspec = _attn_specs(T)

    def body(q_ref, k_ref, kt_ref, ktt_ref, v_ref, b_ref, s_ref, o_ref, do_ref,
             dq_ref, dk_ref, dv_ref, db_ref, ds_ref, pb, dsb):
        n = pl.program_id(1)

        @pl.when(n == 0)
        def _():
            dk_ref[...] = jnp.zeros_like(dk_ref)
            dv_ref[...] = jnp.zeros_like(dv_ref)
            db_ref[...] = jnp.zeros_like(db_ref)
            ds_ref[...] = jnp.zeros_like(ds_ref)

        def products(b):
            rows = _attn_block(n, b)[0]
            qs = slice(b * BLK, (b + 1) * BLK)
            q4, do4 = _heads_side_by_side(q_ref, qs), _heads_side_by_side(do_ref, qs)
            return q4, do4, _dot(k_ref[rows, :], q4), _dot(v_ref[rows, :], do4)

        ahead = products(0)
        for b in range(QB):
            rows, table = _attn_block(n, b)
            qs = slice(b * BLK, (b + 1) * BLK)
            q4, do4, st, dpt = ahead
            if b + 1 < QB:
                ahead = products(b + 1)
            for g in range(GROUP):
                hs = slice(g * BLK, (g + 1) * BLK)
                p, inv, es = _softmax_cols(st[:, hs] + b_ref[table, :, hs], s_ref[:, hs])
                probs = p * inv
                delta = jnp.sum(do_ref[g, :, qs].astype(F32) * o_ref[g, :, qs].astype(F32), axis=0, keepdims=True)
                dS = probs * (dpt[:, hs] - delta)
                ds_ref[:, hs] += -(es * inv) * delta
                db_ref[:, hs] += dS
                pb[:, hs] = probs.astype(BF16)
                dsb[:, hs] = dS.astype(BF16)
            dqt = _dot(_band_cols(kt_ref, ktt_ref, b), dsb[...]) * (HD ** -0.5)
            for g in range(GROUP):
                dq_ref[g, :, qs] = dqt[:, g * BLK:(g + 1) * BLK].astype(BF16)
            dk_ref[rows, :] += _dot_nt(dsb[...], q4)
            dv_ref[rows, :] += _dot_nt(pb[...], do4)

    kout = pl.BlockSpec((None, T + BLK, HD), lambda kv, n: (kv, 0, 0))
    dbspec = pl.BlockSpec((None, 2 * BLK, QW), lambda kv, n: (kv, 0, 0))
    return pl.pallas_call(
        body, name=name, grid=(N_KV, T // (QB * BLK)),
        in_specs=[qspec, kspec, *ktspec, kspec, bspec, sspec, qspec, qspec],
        out_specs=[qspec, kout, kout, dbspec, sspec],
        out_shape=[jax.ShapeDtypeStruct((N_KV, GROUP, HD, T), BF16),
                   jax.ShapeDtypeStruct((N_KV, T + BLK, HD), F32), jax.ShapeDtypeStruct((N_KV, T + BLK, HD), F32),
                   jax.ShapeDtypeStruct((N_KV, 2 * BLK, QW), F32), jax.ShapeDtypeStruct((N_KV, 1, QW), F32)],
        scratch_shapes=[pltpu.VMEM((2 * BLK, QW), BF16), pltpu.VMEM((2 * BLK, QW), BF16)],
        compiler_params=_cp("parallel", "arbitrary"),
    )(q, kp, kt, kt, vp, bias, sink, o, do)


def final_loss(h, g, target, name):
    T = h.shape[0]

    def body(h_ref, g_ref, t_ref, dh_ref, st_ref):
        i = pl.program_id(0)

        @pl.when(i == 0)
        def _():
            st_ref[...] = jnp.zeros_like(st_ref)

        x = h_ref[...]
        r = _rms(x)
        xh = x * r
        e = xh * g_ref[...] - t_ref[...]
        loss = 0.5 * jnp.sum(jnp.mean(e * e, axis=-1, keepdims=True))
        dy = e * (1.0 / D)
        st_ref[0:1, :] += jnp.sum(dy * xh, axis=0, keepdims=True)
        lane = lax.broadcasted_iota(jnp.int32, (1, D), 1)
        st_ref[1:2, :] += jnp.where(lane == 0, loss, 0.0)
        dxh = dy * g_ref[...]
        dh_ref[...] = r * (dxh - xh * jnp.mean(dxh * xh, axis=-1, keepdims=True))

    return pl.pallas_call(
        body, name=name, grid=(T // TM,),
        in_specs=[_row(TM, D), _const((1, D)), _row(TM, D)],
        out_specs=[_row(TM, D), _const((8, D))],
        out_shape=[jax.ShapeDtypeStruct((T, D), F32), jax.ShapeDtypeStruct((8, D), F32)],
        compiler_params=_cp("arbitrary"),
    )(h, g, target)


def mm_dw(x, dy, name, tn, slots, colsum=False):
    T, K = x.shape
    split = dy.ndim == 3
    N = dy.shape[-1] * (2 if split else 1)
    tt = min(T, 2048 if K <= 1024 else 1024)
    nt = T // tt
    ns = N // slots
    per = ns // tn

    def body(x_ref, dy_ref, *rest):
        if colsum:
            dw_ref, cs_ref, acc, cacc = rest
        else:
            dw_ref, acc = rest
        t = pl.program_id(1)

        @pl.when(t == 0)
        def _():
            acc[...] = jnp.zeros_like(acc)
            if colsum:
                cacc[...] = jnp.zeros_like(cacc)

        dyv = dy_ref[...]
        acc[...] += _dot_tn(x_ref[...].astype(BF16), dyv.astype(BF16))
        if colsum:
            cacc[...] += jnp.sum(dyv.astype(F32), axis=0, keepdims=True)

        @pl.when(t == nt - 1)
        def _():
            dw_ref[...] = acc[...].astype(BF16)
            if colsum:
                cs_ref[...] = cacc[...]

    if split:
        half = N // 2 // tn
        dy_spec = pl.BlockSpec((None, tt, tn), lambda j, t: (j // half, t, j % half))
    else:
        dy_spec = pl.BlockSpec((tt, tn), lambda j, t: (t, j))
    out_specs = [pl.BlockSpec((None, K, tn), lambda j, t: (j // per, 0, j % per))]
    out_shape = [jax.ShapeDtypeStruct((slots, K, ns), BF16)]
    scratch = [pltpu.VMEM((K, tn), F32)]
    if colsum:
        out_specs.append(pl.BlockSpec((1, tn), lambda j, t: (0, j)))
        out_shape.append(jax.ShapeDtypeStruct((1, N), F32))
        scratch.append(pltpu.VMEM((1, tn), F32))
    res = pl.pallas_call(
        body, name=name, grid=(N // tn, nt),
        in_specs=[pl.BlockSpec((tt, K), lambda j, t: (t, 0)), dy_spec],
        out_specs=out_specs, out_shape=out_shape, scratch_shapes=scratch,
        compiler_params=_cp("parallel", "arbitrary"),
    )(x, dy)
    return tuple(res) if colsum else res[0]


def mmT_swiglu_bwd(dh, w, gu, name, after=()):
    T = dh.shape[0]
    cw = 256

    def body(dh_ref, w_ref, gu_ref, *rest):
        du_ref = rest[-1]
        dhb = dh_ref[...].astype(BF16)
        ahead = _dot_nt(dhb, w_ref[0:cw, :])
        for lo in range(0, DFF, cw):
            hi = lo + cw
            df = ahead
            if hi < DFF:
                ahead = _dot_nt(dhb, w_ref[hi:hi + cw, :])
            gate = gu_ref[:, lo:hi].astype(F32)
            up = gu_ref[:, DFF + lo:DFF + hi].astype(F32)
            sg = _sigmoid(gate)
            silu = gate * sg
            du_ref[:, lo:hi] = (df * (up * (sg + silu * (1.0 - sg)))).astype(BF16)
            du_ref[:, DFF + lo:DFF + hi] = (df * silu).astype(BF16)

    return pl.pallas_call(
        body, name=name, grid=(T // TM,),
        in_specs=[_row(TM, D), _weight((DFF, D)), _row(TM, 2 * DFF)] + [ANY] * len(after),
        out_specs=_row(TM, 2 * DFF), out_shape=jax.ShapeDtypeStruct((T, 2 * DFF), BF16),
        compiler_params=_cp("parallel"),
    )(dh, w, gu, *after)


def mmT_rmsbwd(du, w, h, g, gl, dh_in, name):
    split = du.ndim == 3
    T = du.shape[-2]
    N = du.shape[-1] * (2 if split else 1)
    slots = w.shape[0]
    ns = N // slots

    RH = TM // 2

    def piece(du_ref, s, rows):
        if split:
            per = slots // 2
            return du_ref[s // per, rows, (s % per) * ns:(s % per + 1) * ns]
        return du_ref[rows, s * ns:(s + 1) * ns]

    def body(du_ref, w_ref, h_ref, g_ref, di_ref, dh_ref, dg_ref):
        i = pl.program_id(0)

        @pl.when(i == 0)
        def _():
            dg_ref[...] = jnp.zeros_like(dg_ref)

        def products(k):
            rows = slice(k * RH, (k + 1) * RH)
            dxn = _dot_nt(piece(du_ref, 0, rows), w_ref[0])
            for s in range(1, slots):
                dxn = dxn + _dot_nt(piece(du_ref, s, rows), w_ref[s])
            return dxn

        ahead = products(0)
        for k in range(TM // RH):
            rows = slice(k * RH, (k + 1) * RH)
            dxn = ahead
            if (k + 1) * RH < TM:
                ahead = products(k + 1)
            x = h_ref[rows, :]
            r = _rms(x)
            xh = x * r
            dg_ref[0:1, :] += jnp.sum(dxn * xh, axis=0, keepdims=True)
            dxh = dxn * g_ref[...]
            dh_ref[rows, :] = di_ref[rows, :] + r * (dxh - xh * jnp.mean(dxh * xh, axis=-1, keepdims=True))

    return pl.pallas_call(
        body, name=name, grid=(T // TM,),
        in_specs=[pl.BlockSpec((2, TM, N // 2), lambda i: (0, i, 0)) if split else _row(TM, N),
                  _weight((slots, D, ns)), _row(TM, D), _layer((1, D), gl), _row(TM, D)],
        out_specs=[_row(TM, D), _const((8, D))],
        out_shape=[jax.ShapeDtypeStruct((T, D), F32), jax.ShapeDtypeStruct((8, D), F32)],
        compiler_params=_cp("arbitrary"),
    )(du, w, h, g, dh_in)


def mmT(dh, w, name):
    T = dh.shape[0]
    N = w.shape[0]

    def body(dh_ref, w_ref, o_ref):
        o_ref[...] = _dot_nt(dh_ref[...].astype(BF16), w_ref[...]).astype(BF16)

    return pl.pallas_call(
        body, name=name, grid=(T // TM,),
        in_specs=[_row(TM, D), _weight((N, D))],
        out_specs=_row(TM, N), out_shape=jax.ShapeDtypeStruct((T, N), BF16),
        compiler_params=_cp("parallel"),
    )(dh, w)


def mmT_lnbwd(dh, w, y, sm, l, name):
    T = dh.shape[0]

    def body(dh_ref, w_ref, y_ref, sm_ref, dy_ref, st_ref):
        i = pl.program_id(0)

        @pl.when(i == 0)
        def _():
            st_ref[...] = jnp.zeros_like(st_ref)

        ds = _dot_nt(dh_ref[...].astype(BF16), w_ref[...])
        y = y_ref[...].astype(F32)
        mu = jnp.mean(y, axis=-1, keepdims=True)
        yc = y - mu
        rstd = lax.rsqrt(jnp.mean(yc * yc, axis=-1, keepdims=True) + EPS)
        xh = yc * rstd
        gam = sm_ref[32:33, :]
        z = xh * gam + sm_ref[33:34, :]
        sg = _sigmoid(z)
        dz = ds * sg * (1.0 + z * (1.0 - sg))
        st_ref[0:1, :] += jnp.sum(dz * xh, axis=0, keepdims=True)
        st_ref[1:2, :] += jnp.sum(dz, axis=0, keepdims=True)
        dxh = dz * gam
        dy = rstd * (dxh - jnp.mean(dxh, axis=-1, keepdims=True) - xh * jnp.mean(dxh * xh, axis=-1, keepdims=True))
        st_ref[2:3, :] += jnp.sum(dy, axis=0, keepdims=True)
        dy_ref[...] = dy.astype(BF16)

    return pl.pallas_call(
        body, name=name, grid=(T // TM,),
        in_specs=[_row(TM, D), _weight((D, D)), _row(TM, D), _layer((40, D), l)],
        out_specs=[_row(TM, D), _const((8, D))],
        out_shape=[jax.ShapeDtypeStruct((T, D), BF16), jax.ShapeDtypeStruct((8, D), F32)],
        compiler_params=_cp("arbitrary"),
    )(dh, w, y, sm)


CH = 512


def dwconv_glu_bwd(dy, a, u, sm, smrev, l, name):
    T = dy.shape[0]
    nr, nc = T // TCV, D // CH
    nb = TCV // HALO
    last = T // HALO - 1

    def body(dy_ref, dyn_ref, a_ref, ap_ref, u1_ref, u2_ref, sm_ref, rev_ref, du_ref, dw_ref, shd, sha, da):
        i = pl.program_id(0)
        r = i % nr

        @pl.when(r == 0)
        def _():
            dw_ref[...] = jnp.zeros_like(dw_ref)

        shd[0, 0:TCV, :] = dy_ref[...].astype(F32)
        shd[0, TCV:TCV + HALO, :] = jnp.where(r < nr - 1, dyn_ref[...].astype(F32), 0.0)
        sha[0, 0:HALO, :] = jnp.where(r > 0, ap_ref[...].astype(F32), 0.0)
        sha[0, HALO:HALO + TCV, :] = a_ref[...].astype(F32)
        _make_shifts(shd)
        _make_shifts(sha)
        _conv_taps(shd, rev_ref, da, 0)
        for k in range(CONV_W):
            part = jnp.zeros((SUB, CH), F32)
            for r0 in range(0, TCV, SUB):
                part = part + shd[0, r0:r0 + SUB, :] * _shifted(sha, HALO - (CONV_W - 1) + k + r0, SUB, slice(None))
            dw_ref[k:k + 1, :] += jnp.sum(part, axis=0, keepdims=True)
        dav = da[...]
        u1 = u1_ref[...].astype(F32)
        sg = _sigmoid(u2_ref[...].astype(F32))
        du_ref[0] = (dav * sg).astype(BF16)
        du_ref[1] = (dav * u1 * sg * (1.0 - sg)).astype(BF16)

    tile = lambda i: (i % nr, i // nr)
    in_specs = [pl.BlockSpec((TCV, CH), tile),
                pl.BlockSpec((HALO, CH), lambda i: (jnp.minimum((i % nr + 1) * nb, last), i // nr)),
                pl.BlockSpec((TCV, CH), tile),
                pl.BlockSpec((HALO, CH), lambda i: (jnp.maximum((i % nr) * nb - 1, 0), i // nr)),
                pl.BlockSpec((TCV, CH), tile), pl.BlockSpec((TCV, CH), lambda i: (i % nr, nc + i // nr)),
                pl.BlockSpec((None, 40, CH), lambda i: (l, 0, i // nr)),
                pl.BlockSpec((None, 40, CH), lambda i: (l, 0, i // nr))]
    return pl.pallas_call(
        body, name=name, grid=(nr * nc,), in_specs=in_specs,
        out_specs=[pl.BlockSpec((2, TCV, CH), lambda i: (0, i % nr, i // nr)),
                   pl.BlockSpec((32, CH), lambda i: (0, i // nr))],
        out_shape=[jax.ShapeDtypeStruct((2, T, D), BF16), jax.ShapeDtypeStruct((32, D), F32)],
        scratch_shapes=[pltpu.VMEM((SUB, TCV + HALO, CH), F32), pltpu.VMEM((SUB, TCV + HALO, CH), F32),
                        pltpu.VMEM((TCV, CH), F32)],
        compiler_params=_cp("arbitrary"),
    )(dy, dy, a, a, u, u, sm, smrev)


def _rows_tile(R):
    for t in (512, 256, 128, 64, 32, 16, 8):
        if R % t == 0:
            return t
    return R


def add8_into(J, l, g, others, where, name):
    R, C = g.shape[2:]
    tr = R // 2

    def body(w_ref, g_ref, x_ref, j_in, j_ref):
        acc = g_ref[...].astype(F32)
        for k in range(7):
            acc = acc + x_ref[k].astype(F32)
        j_ref[...] = acc

    return pl.pallas_call(
        body, name=name,
        grid_spec=pltpu.PrefetchScalarGridSpec(
            num_scalar_prefetch=1, grid=(R // tr,),
            in_specs=[pl.BlockSpec((None, None, tr, C), lambda i, w: (w[0], w[1], i, 0)),
                      pl.BlockSpec((7, tr, C), lambda i, w: (0, i, 0)), ANY],
            out_specs=pl.BlockSpec((None, None, tr, C), lambda i, w: (l, w[1], i, 0))),
        out_shape=jax.ShapeDtypeStruct(J.shape, F32), input_output_aliases={3: 0},
        compiler_params=_cp("parallel"),
    )(where, g, others, J)


def adamw(w, g, m, v, name, copy_g=False):
    R, C = w.shape
    tr = _rows_tile(R)

    def body(w_ref, g_ref, m_ref, v_ref, *outs):
        d_ref, nm_ref, nv_ref = outs[-3:]
        gv = g_ref[...]
        if copy_g:
            outs[0][...] = gv
        nm = ADAM_B1 * m_ref[...] + (1.0 - ADAM_B1) * gv
        nv = ADAM_B2 * v_ref[...] + (1.0 - ADAM_B2) * (gv * gv)
        m_hat = nm / (1.0 - ADAM_B1 ** ADAM_STEP)
        v_hat = nv / (1.0 - ADAM_B2 ** ADAM_STEP)
        d_ref[...] = -ADAM_LR * (m_hat / (jnp.sqrt(v_hat) + ADAM_EPS) + ADAM_WD * w_ref[...])
        nm_ref[...] = nm
        nv_ref[...] = nv

    sd = jax.ShapeDtypeStruct((R, C), F32)
    n_out = 4 if copy_g else 3
    return pl.pallas_call(
        body, name=name, grid=(R // tr,),
        in_specs=[_row(tr, C)] * 4, out_specs=[_row(tr, C)] * n_out, out_shape=[sd] * n_out,
        compiler_params=_cp("parallel"),
    )(w, g, m, v)


ANY = pl.BlockSpec(memory_space=pl.ANY)
HBM = pl.BlockSpec(memory_space=pltpu.HBM)
SEM = pl.BlockSpec(memory_space=pltpu.SEMAPHORE)
EFFECT = pltpu.SideEffectType.DATAFLOW_SIDE_EFFECTING


def _place():
    x, y, c = lax.axis_index("x"), lax.axis_index("y"), lax.axis_index("c")
    chips = [(1 - x, y), (x, 1 - y), (1 - x, 1 - y)]
    return x, y, c, chips


def _copy(src, dst, send, recv, k, to):
    return pltpu.make_async_remote_copy(src_ref=src, dst_ref=dst, send_sem=send.at[k], recv_sem=recv.at[k],
                                        device_id=to, device_id_type=MESH)


def xchg_start(name, bufs, plan, n, after=()):
    nb = len(bufs)

    na = len(after)

    def body(*refs):
        send, recv, token = refs[nb + na], refs[nb + na + 1], refs[-1]
        for k, (src, dst, to) in enumerate(plan(refs[:nb])):
            _copy(src, dst, send, recv, k, to).start()
        token[...] = jnp.zeros_like(token)

    outs = pl.pallas_call(
        body, name=name,
        out_shape=(pltpu.SemaphoreType.DMA((n,)), pltpu.SemaphoreType.DMA((n,)),
                   *[pltpu.HBM(b.shape, b.dtype) for b in bufs], jax.ShapeDtypeStruct((8, 128), F32)),
        in_specs=[HBM] * nb + [ANY] * na,
        out_specs=(SEM, SEM, *[HBM] * nb, pl.BlockSpec(memory_space=pltpu.VMEM)),
        input_output_aliases={i: 2 + i for i in range(nb)},
        compiler_params=pltpu.CompilerParams(has_side_effects=EFFECT),
    )(*[pltpu.with_memory_space_constraint(b, pltpu.HBM) for b in bufs], *after)
    return dict(name=name, send=outs[0], recv=outs[1], bufs=list(outs[2:2 + nb]), plan=plan), outs[-1]


def xchg_wait(flight, after):
    bufs, plan = flight["bufs"], flight["plan"]
    nb = len(bufs)

    def body(*refs):
        send, recv = refs[nb], refs[nb + 1]
        for k, (src, dst, to) in enumerate(plan(refs[:nb])):
            cp = _copy(src, dst, send, recv, k, to)
            cp.wait_send()
            cp.wait_recv()

    outs = pl.pallas_call(
        body, name=flight["name"] + "_wait",
        out_shape=tuple(pltpu.HBM(b.shape, b.dtype) for b in bufs),
        in_specs=[HBM] * nb + [SEM, SEM] + [ANY] * len(after),
        out_specs=tuple([HBM] * nb), input_output_aliases={i: i for i in range(nb)},
        compiler_params=pltpu.CompilerParams(has_side_effects=EFFECT),
    )(*bufs, flight["send"], flight["recv"], *after)
    return list(outs)


def _flip(k, x, y, c):
    return ((1 - x) if k & 4 else x, (1 - y) if k & 2 else y, (1 - c) if k & 1 else c)


def cast_into_slot(srcs, name, after):
    me = (2 * lax.axis_index("x") + lax.axis_index("y")).astype(jnp.int32).reshape(1)
    ns = len(srcs)

    def body(me_ref, *refs):
        outs = refs[ns + len(after):]
        for t in range(ns):
            outs[t][...] = refs[t][...].astype(outs[t].dtype).reshape(outs[t].shape)

    in_specs, out_specs, out_shape = [], [], []
    for arr, l in srcs:
        if l is None:
            in_specs.append(pl.BlockSpec(arr.shape, lambda i, w, nd=arr.ndim: (0,) * nd))
            a2, b, dt = (arr.shape[0] // 2, arr.shape[1], BF16) if arr.ndim == 2 else (arr.shape[1], arr.shape[2], F32)
        else:
            in_specs.append(pl.BlockSpec((None,) + arr.shape[1:], lambda i, w, l=l: (l, 0, 0)))
            a2, b, dt = arr.shape[1] // 2, arr.shape[2], BF16
        out_specs.append(pl.BlockSpec((None, 2, a2, b), lambda i, w: (w[0], 0, 0, 0)))
        out_shape.append(jax.ShapeDtypeStruct((4, 2, a2, b), dt))
    in_specs += [ANY] * len(after)
    return pl.pallas_call(
        body, name=name,
        grid_spec=pltpu.PrefetchScalarGridSpec(num_scalar_prefetch=1, grid=(1,), in_specs=in_specs,
                                               out_specs=out_specs),
        out_shape=out_shape, compiler_params=_cp("arbitrary"),
    )(me, *[arr for arr, _ in srcs], *after)


class WeightGather:
    def __init__(self, source, groups):
        self.names = dict(groups)
        self.ici, self.d2d = {}, {}
        self.token = None
        for gname, names in groups:
            nt = len(names)
            after = [] if self.token is None else [self.token]
            lands = cast_into_slot([source(n) for n in names], f"ag_cast_{gname}", after)

            def plan(refs, nt=nt):
                x, y, c, chips = _place()
                out = []
                for t in range(nt):
                    mine = refs[t].at[2 * x + y, c]
                    out += [(mine, mine, (cx, cy, c)) for cx, cy in chips]
                return out

            self.ici[gname], self.token = xchg_start(f"ag_ici_{gname}", lands, plan, 3 * nt, after=after)

    def forward(self, gname, after):
        nt = len(self.names[gname])
        lands = xchg_wait(self.ici.pop(gname), after)

        def plan(refs):
            x, y, c, chips = _place()
            out = []
            for t in range(nt):
                for cx, cy in chips:
                    piece = refs[t].at[2 * cx + cy, c]
                    out.append((piece, piece, (x, y, 1 - c)))
            return out

        self.d2d[gname], token = xchg_start(f"ag_d2d_{gname}", lands, plan, 3 * nt)
        return token

    def get(self, gname, after):
        lands = xchg_wait(self.d2d.pop(gname), after)
        return dict(zip(self.names[gname], lands))


class GradReduce:
    def __init__(self, kinds):
        self.J = {k: lax.empty((L, 2, a2, b), F32) for k, (L, a2, b) in kinds.items()}
        self.x, self.j = {}, {}

    @staticmethod
    def _where(name):
        kind, _, l = name.partition("_")
        return kind, int(l or 0)

    def send(self, gname, grads, after=()):
        names = list(grads)
        nt = len(names)
        gs = [grads[n] for n in names]
        xs = [lax.empty((7,) + g.shape[2:], g.dtype) for g in gs]

        def plan(refs):
            x, y, c, _ = _place()
            out = []
            for t in range(nt):
                for k in range(1, 8):
                    px, py, pc = _flip(k, x, y, c)
                    out.append((refs[t].at[2 * px + py, pc], refs[nt + t].at[k - 1], (px, py, pc)))
            return out

        flight, token = xchg_start(f"rs_x_{gname}", gs + xs, plan, 7 * nt, after=after)
        self.x[gname] = (names, flight)
        return token

    def reduce(self, gname, after):
        names, flight = self.x.pop(gname)
        nt = len(names)
        bufs = xchg_wait(flight, after)
        mine = jnp.stack([2 * lax.axis_index("x") + lax.axis_index("y"), lax.axis_index("c")]).astype(jnp.int32)
        where = [self._where(n) for n in names]
        js = [add8_into(self.J[kind], l, bufs[t], bufs[nt + t], mine, f"rs_add_{names[t]}")
              for t, (kind, l) in enumerate(where)]

        def plan(refs):
            x, y, c, _ = _place()
            out = []
            for t in range(nt):
                half = refs[t].at[where[t][1], c]
                out.append((half, half, (x, y, 1 - c)))
            return out

        flight, token = xchg_start(f"rs_join_{gname}", js, plan, nt)
        self.j[gname] = (where, flight)
        return token

    def finish(self, gname, after):
        where, flight = self.j.pop(gname)
        for (kind, _), j in zip(where, xchg_wait(flight, after)):
            self.J[kind] = j


def small_allreduce_start(v, after):
    me = 4 * lax.axis_index("x") + 2 * lax.axis_index("y") + lax.axis_index("c")
    land = lax.dynamic_update_slice(lax.empty((8,) + v.shape, v.dtype), v[None], (me, 0, 0))

    def plan(refs):
        x, y, c, _ = _place()
        return [(refs[0], refs[1].at[4 * x + 2 * y + c], _flip(k, x, y, c)) for k in range(1, 8)]

    return xchg_start("small_allreduce", [v, land], plan, 7, after=after)


def sum8(all8, name):
    def body(x_ref, o_ref):
        acc = x_ref[0]
        for d in range(1, 8):
            acc = acc + x_ref[d]
        o_ref[...] = acc

    return pl.pallas_call(
        body, name=name,
        in_specs=[pl.BlockSpec(memory_space=pltpu.VMEM)], out_specs=pl.BlockSpec(memory_space=pltpu.VMEM),
        out_shape=jax.ShapeDtypeStruct(all8.shape[1:], F32),
        compiler_params=pltpu.CompilerParams(vmem_limit_bytes=VMEM_LIMIT),
    )(all8)


AG_GROUPS = (("a0", ("pw1_0", "pw2_0", "small")), ("f0", ("up_0", "down_0")),
             ("l1", ("pw1_1", "pw2_1", "up_1", "down_1")), ("l2", ("kv", "wq_0", "wo_0", "up_2", "down_2")),
             ("l3", ("wq_1", "wo_1", "up_3", "down_3")))


def _bucket_table():
    qi = np.arange(BLK)[:, None]
    kj = np.arange(2 * BLK)[None, :]
    d = np.maximum(qi + BLK - kj, 0)
    max_exact = N_BUCKETS // 2
    log_ratio = (np.log(np.maximum(d, 1).astype(np.float32) / np.float32(max_exact))
                 / np.float32(math.log(MAX_DISTANCE / max_exact))).astype(np.float32)
    large = max_exact + (log_ratio * np.float32(N_BUCKETS - max_exact)).astype(np.int32)
    large = np.minimum(large, N_BUCKETS - 1)
    return np.where(d < max_exact, d, large).astype(np.int32)


def _heads_major(a, nh):
    T = a.shape[0]
    return a.reshape(T, nh, HD).transpose(1, 0, 2)


def _heads_minor(a):
    nh, T, _ = a.shape
    return a.transpose(1, 0, 2).reshape(T, nh * HD)


def _slots(land):
    return land.reshape(4, 2 * land.shape[2], land.shape[3])


def _rows(land):
    return land.reshape(8 * land.shape[2], land.shape[3])


def _gview(g):
    s, K, n = g.shape
    return g.reshape(4, 2, K // 2, n) if s == 4 else g.reshape(4, 2, K // 8, n)


def _gate(a, token):
    return a * (1.0 + token[0, 0])


def _conv_small(f_small):
    fs = f_small.transpose(1, 2, 0, 3).reshape(2, 40, D)
    b_pw1 = f_small[:, :, 35:37, :].transpose(1, 0, 2, 3).reshape(2, 1, 2 * D)
    rev = jnp.concatenate([fs[:, CONV_W - 1::-1], jnp.zeros((2, 40 - CONV_W, D), F32)], axis=1)
    return dict(conv=fs, conv_rev=rev, b_pw1=b_pw1, b_pw2=fs[:, 34:35])


def run_step(x, target, P, ag, rs):
    T = x.shape[0]
    zero = jnp.zeros((1, 1, D), F32)
    nm, nf = P["norm_mix"], P["norm_ffn"]
    ag.forward("a0", [ag.token])
    W = ag.get("a0", [])
    sm = _conv_small(W["small"])
    h = x
    saved = []
    for l in range(2):
        xn, u, a = norm_mm_glu(h, nm, l, _slots(W[f"pw1_{l}"]), sm["b_pw1"], f"f_pw1_{l}")
        y, s = dwconv_ln_silu(a, sm["conv"], l, f"f_conv_{l}")
        b2 = sm["b_pw2"]
        if l == 0:
            b2 = _gate(b2, ag.forward("f0", [s]))
        h1 = mm_bias_res(s, _rows(W[f"pw2_{l}"]), b2, l, h, f"f_pw2_{l}")
        if l == 0:
            W.update(ag.get("f0", [h1]))
        xn2, gu, f = norm_mm_swiglu(h1, nf, l, _slots(W[f"up_{l}"]), f"f_up_{l}")
        nxt = "l1" if l == 0 else "l2"
        h2 = mm_bias_res(f, _rows(W[f"down_{l}"]), _gate(zero, ag.forward(nxt, [f])), 0, h1, f"f_down_{l}")
        W.update(ag.get(nxt, [h2]))
        saved.append(dict(h=h, xn=xn, u=u, a=a, y=y, s=s, h1=h1, xn2=xn2, gu=gu, f=f))
        h = h2
    h_kv = h
    kvn, kv = norm_mm(h, P["norm_kv"], 0, _rows(W["kv"]), "f_kv")
    kp = jnp.pad(_heads_major(kv[:, :N_KV * HD], N_KV), ((0, 0), (BLK, 0), (0, 0)))
    vp = jnp.pad(_heads_major(kv[:, N_KV * HD:], N_KV), ((0, 0), (BLK, 0), (0, 0)))
    kvt = jnp.pad(kv.T.reshape(2, N_KV, HD, T), ((0, 0), (0, 0), (0, 0), (BLK, 0)))
    kt, vt = kvt[0], kvt[1]
    bucket = _bucket_table()
    onehot = jnp.asarray(np.eye(N_BUCKETS, dtype=np.float32)[bucket])
    bias = jnp.einsum("qkb,bh->hkq", onehot, P["rel_bias"], precision=lax.Precision.HIGHEST)
    bias = bias.reshape(N_KV, GROUP, 2 * BLK, BLK).transpose(0, 2, 1, 3).reshape(1, N_KV, 2 * BLK, QW)
    bias = bias + jnp.asarray(band_mask())[:, None]
    for j in range(2):
        l = 2 + j
        xn, q = norm_mm(h, nm, l, _rows(W[f"wq_{j}"]), f"f_q_{j}", scale=HD ** -0.5)
        qh = q.T.reshape(N_KV, GROUP, HD, T)
        sink = jnp.broadcast_to(P["sinks"][j].reshape(N_KV, GROUP, 1), (N_KV, GROUP, BLK)).reshape(N_KV, 1, QW)
        oh = attn_fwd(qh, kp, vt, bias, sink, f"f_attn_{j}")
        attn = oh.reshape(N_HEADS * HD, T).T
        h1 = mm_bias_res(attn, _rows(W[f"wo_{j}"]), zero, 0, h, f"f_wo_{j}")
        xn2, gu, f = norm_mm_swiglu(h1, nf, l, _slots(W[f"up_{l}"]), f"f_up_{l}")
        zg = _gate(zero, ag.forward("l3", [f])) if j == 0 else zero
        h2 = mm_bias_res(f, _rows(W[f"down_{l}"]), zg, 0, h1, f"f_down_{l}")
        if j == 0:
            W.update(ag.get("l3", [h2]))
        saved.append(dict(h=h, xn=xn, qh=qh, oh=oh, sink=sink, attn=attn, h1=h1, xn2=xn2, gu=gu, f=f))
        h = h2

    dh, st_final = final_loss(h, P["norm_final"], target, "loss_head")

    S = dict(norm_ffn=[None] * 4, norm_mix=[None] * 4, conv=[None] * 2, taps=[None] * 2, b_pw1=[None] * 2,
             b_pw2=[None] * 2, sinks=[None] * 2)

    def ffn_bwd(dh, sv, l, nf, after=()):
        du = mmT_swiglu_bwd(dh, _rows(W[f"down_{l}"]), sv["gu"], f"b_down_{l}", after)
        gd = mm_dw(sv["f"], dh, f"w_down_{l}", 512, 1)
        gu = mm_dw(sv["xn2"], du, f"w_up_{l}", DFF // 2, 4)
        dh, dg = mmT_rmsbwd(du, _slots(W[f"up_{l}"]), sv["h1"], nf, l, dh, f"b_up_{l}")
        S["norm_ffn"][l] = dg[0]
        return dh, {f"down_{l}": _gview(gd), f"up_{l}": _gview(gu)}

    dk = dv = dbias = None
    sent = []
    for j in (1, 0):
        l = 2 + j
        sv = saved[l]
        dh, grads = ffn_bwd(dh, sv, l, nf, sent)
        dattn = mmT(dh, _rows(W[f"wo_{j}"]), f"b_wo_{j}")
        grads[f"wo_{j}"] = _gview(mm_dw(sv["attn"], dh, f"w_wo_{j}", 512, 1))
        doh = dattn.T.reshape(N_KV, GROUP, HD, T)
        dqh, dkj, dvj, dbj, dsj = attn_bwd(sv["qh"], kp, kt, vp, bias, sv["sink"], sv["oh"], doh, f"b_attn_{j}")
        dq = dqh.reshape(N_HEADS * HD, T).T
        grads[f"wq_{j}"] = _gview(mm_dw(sv["xn"], dq, f"w_q_{j}", 512, 1))
        dh, dg = mmT_rmsbwd(dq, _rows(W[f"wq_{j}"])[None], sv["h"], nm, l, dh, f"b_q_{j}")
        S["norm_mix"][l] = dg[0]
        S["sinks"][j] = jnp.sum(dsj.reshape(N_HEADS, BLK), axis=1)
        dk = dkj if dk is None else dk + dkj
        dv = dvj if dv is None else dv + dvj
        dbias = dbj if dbias is None else dbias + dbj
        if j == 1:
            sent = [rs.send("l3", grads)]

    dkv = jnp.concatenate([_heads_minor(dk[:, BLK:]), _heads_minor(dv[:, BLK:])], axis=1).astype(BF16)
    grads["kv"] = _gview(mm_dw(kvn, dkv, "w_kv", 512, 1))
    dh, dg = mmT_rmsbwd(dkv, _rows(W["kv"])[None], h_kv, P["norm_kv"], 0, dh, "b_kv")
    S["norm_kv"] = dg[0]
    dbh = dbias.reshape(N_KV, 2 * BLK, GROUP, BLK)
    S["rel_bias"] = jnp.einsum("vkgq,qkb->bvg", dbh, onehot, precision=lax.Precision.HIGHEST).reshape(N_BUCKETS, N_HEADS)
    sent = [rs.send("l2", grads)]
    nf = _gate(nf, rs.reduce("l3", [dh]))

    for l in (1, 0):
        sv = saved[l]
        dh, grads = ffn_bwd(dh, sv, l, nf, sent)
        conv = sm["conv"]
        if l == 0:
            conv = _gate(conv, rs.send("f0", grads))
            grads = {}
        dy, st = mmT_lnbwd(dh, _rows(W[f"pw2_{l}"]), sv["y"], conv, l, f"b_pw2_{l}")
        g2, S["b_pw2"][l] = mm_dw(sv["s"], dh, f"w_pw2_{l}", 512, 1, colsum=True)
        du, dtaps = dwconv_glu_bwd(dy, sv["a"], sv["u"], sm["conv"], sm["conv_rev"], l, f"b_conv_{l}")
        S["conv"][l] = st[0:3]
        S["taps"][l] = dtaps[0:CONV_W]
        if l == 0:
            rs.finish("l2", [du])
            nm = _gate(nm, rs.reduce("l1", [du]))
        g1, S["b_pw1"][l] = mm_dw(sv["xn"], du, f"w_pw1_{l}", 512, 4, colsum=True)
        grads[f"pw2_{l}"], grads[f"pw1_{l}"] = _gview(g2), _gview(g1)
        dh, dg = mmT_rmsbwd(du, _slots(W[f"pw1_{l}"]), sv["h"], nm, l, dh, f"b_pw1_{l}")
        S["norm_mix"][l] = dg[0]
        if l == 1:
            sent = [rs.send("l1", grads)]
            rs.finish("l3", [dh])
            nf = _gate(nf, rs.reduce("l2", [dh]))
    S["norm_final"] = st_final[0]
    S["loss"] = st_final[1]
    return grads, dh, S


R_CONV = 37
R_SMALL = 88


def _pack_small(S):
    rows = []
    for l in range(2):
        rows += [S["taps"][l], S["conv"][l][2:3], S["conv"][l][0:2], S["b_pw2"][l], S["b_pw1"][l].reshape(2, D)]
    rows += [jnp.stack(S["norm_mix"]), jnp.stack(S["norm_ffn"]), S["norm_kv"][None], S["norm_final"][None]]
    tail = jnp.concatenate([jnp.stack(S["sinks"]).reshape(-1), S["rel_bias"].reshape(-1)])
    rows += [jnp.pad(tail, (0, D - tail.shape[0]))[None], S["loss"][None]]
    v = jnp.concatenate(rows, axis=0)
    return jnp.pad(v, ((0, R_SMALL - v.shape[0]), (0, 0)))


def kernel(x, norm_mix, norm_ffn, conv_w_pw1, conv_b_pw1, conv_w_dw, conv_b_dw, conv_ln_g, conv_ln_b, conv_w_pw2, conv_b_pw2, norm_kv, w_kv, w_q, w_o, sinks, rel_bias, ffn_w_up, ffn_w_down, norm_final, loss_target, m_norm_mix, m_norm_ffn, m_conv_w_pw1, m_conv_b_pw1, m_conv_w_dw, m_conv_b_dw, m_conv_ln_g, m_conv_ln_b, m_conv_w_pw2, m_conv_b_pw2, m_norm_kv, m_w_kv, m_w_q, m_w_o, m_sinks, m_rel_bias, m_ffn_w_up, m_ffn_w_down, m_norm_final, v_norm_mix, v_norm_ffn, v_conv_w_pw1, v_conv_b_pw1, v_conv_w_dw, v_conv_b_dw, v_conv_ln_g, v_conv_ln_b, v_conv_w_pw2, v_conv_b_pw2, v_norm_kv, v_w_kv, v_w_q, v_w_o, v_sinks, v_rel_bias, v_ffn_w_up, v_ffn_w_down, v_norm_final):
    me = 2 * lax.axis_index("x") + lax.axis_index("y")
    weights = dict(norm_mix=norm_mix, norm_ffn=norm_ffn, conv_w_pw1=conv_w_pw1, conv_b_pw1=conv_b_pw1,
                   conv_w_dw=conv_w_dw, conv_b_dw=conv_b_dw, conv_ln_g=conv_ln_g, conv_ln_b=conv_ln_b,
                   conv_w_pw2=conv_w_pw2, conv_b_pw2=conv_b_pw2, norm_kv=norm_kv, w_kv=w_kv, w_q=w_q, w_o=w_o,
                   sinks=sinks, rel_bias=rel_bias, ffn_w_up=ffn_w_up, ffn_w_down=ffn_w_down, norm_final=norm_final)
    mom_m = dict(norm_mix=m_norm_mix, norm_ffn=m_norm_ffn, conv_w_pw1=m_conv_w_pw1, conv_b_pw1=m_conv_b_pw1,
                 conv_w_dw=m_conv_w_dw, conv_b_dw=m_conv_b_dw, conv_ln_g=m_conv_ln_g, conv_ln_b=m_conv_ln_b,
                 conv_w_pw2=m_conv_w_pw2, conv_b_pw2=m_conv_b_pw2, norm_kv=m_norm_kv, w_kv=m_w_kv, w_q=m_w_q,
                 w_o=m_w_o, sinks=m_sinks, rel_bias=m_rel_bias, ffn_w_up=m_ffn_w_up, ffn_w_down=m_ffn_w_down,
                 norm_final=m_norm_final)
    mom_v = dict(norm_mix=v_norm_mix, norm_ffn=v_norm_ffn, conv_w_pw1=v_conv_w_pw1, conv_b_pw1=v_conv_b_pw1,
                 conv_w_dw=v_conv_w_dw, conv_b_dw=v_conv_b_dw, conv_ln_g=v_conv_ln_g, conv_ln_b=v_conv_ln_b,
                 conv_w_pw2=v_conv_w_pw2, conv_b_pw2=v_conv_b_pw2, norm_kv=v_norm_kv, w_kv=v_w_kv, w_q=v_w_q,
                 w_o=v_w_o, sinks=v_sinks, rel_bias=v_rel_bias, ffn_w_up=v_ffn_w_up, ffn_w_down=v_ffn_w_down,
                 norm_final=v_norm_final)

    big = {"conv_w_pw1": "pw1", "conv_w_pw2": "pw2", "w_q": "wq", "w_o": "wo", "ffn_w_up": "up",
           "ffn_w_down": "down", "w_kv": "kv"}
    of_kind = {k: n for n, k in big.items()}

    def source(name):
        if name == "small":
            return jnp.concatenate(
                [conv_w_dw, conv_b_dw[:, None], conv_ln_g[:, None], conv_ln_b[:, None], conv_b_pw2[:, None],
                 conv_b_pw1.reshape(2, 2, 256), jnp.zeros((2, 3, 256), F32)], axis=1), None
        kind, _, l = name.partition("_")
        return weights[of_kind[kind]], (int(l) if l else None)

    ag = WeightGather(source, AG_GROUPS)
    rs = GradReduce({"pw1": (2, 512, 512), "pw2": (2, 128, D), "wq": (2, 128, D), "wo": (2, 128, D),
                     "up": (4, 512, DFF // 2), "down": (4, DFF // 8, D), "kv": (1, 128, 512)})

    P = dict(norm_mix=norm_mix[:, None], norm_ffn=norm_ffn[:, None], norm_kv=norm_kv[None, None],
             norm_final=norm_final[None], sinks=sinks, rel_bias=rel_bias)
    last, grad_x, S = run_step(x[0], loss_target[0], P, ag, rs)

    rs.finish("l1", [grad_x])
    small_flight, token = small_allreduce_start(_gate(_pack_small(S), rs.reduce("f0", [grad_x])), [])
    token = rs.send("c0", last, after=[token])
    delta, new_m, new_v, big_grads = {}, {}, {}, {}

    def update(n):
        shp = weights[n].shape
        r2 = (int(np.prod(shp[:-1])), shp[-1])
        g, d, nm, nv = adamw(weights[n].reshape(r2), rs.J[big[n]].reshape(r2), mom_m[n].reshape(r2),
                             mom_v[n].reshape(r2), f"adamw_{n}", copy_g=True)
        big_grads[n], delta[n], new_m[n], new_v[n] = g.reshape(shp), d.reshape(shp), nm.reshape(shp), nv.reshape(shp)

    rs.finish("f0", [token])
    for n in ("ffn_w_up", "ffn_w_down"):
        update(n)
    vsum = sum8(xchg_wait(small_flight, [delta["ffn_w_up"], delta["ffn_w_down"]])[1], "small_sum")

    col = lambda a: lax.dynamic_slice_in_dim(a, me * 256, 256, axis=-1)
    grads = {}
    for l in range(2):
        base = l * R_CONV
        grads.setdefault("conv_w_dw", []).append(col(vsum[base:base + 31]))
        grads.setdefault("conv_b_dw", []).append(col(vsum[base + 31]))
        grads.setdefault("conv_ln_g", []).append(col(vsum[base + 32]))
        grads.setdefault("conv_ln_b", []).append(col(vsum[base + 33]))
        grads.setdefault("conv_b_pw2", []).append(col(vsum[base + 34]))
        grads.setdefault("conv_b_pw1", []).append(
            lax.dynamic_slice_in_dim(vsum[base + 35:base + 37].reshape(2 * D), me * 512, 512, axis=0))
    grads = {k: jnp.stack(v) for k, v in grads.items()}
    base = 2 * R_CONV
    grads["norm_mix"] = vsum[base:base + 4]
    grads["norm_ffn"] = vsum[base + 4:base + 8]
    grads["norm_kv"] = vsum[base + 8]
    grads["norm_final"] = vsum[base + 9]
    grads["sinks"] = vsum[base + 10, 0:32].reshape(2, 16)
    grads["rel_bias"] = vsum[base + 10, 32:32 + 512].reshape(32, 16)
    loss = vsum[base + 11, 0]

    for n in weights:
        if n not in big:
            shp = weights[n].shape
            r2 = (int(np.prod(shp[:-1])), shp[-1])
            d, nm, nv = adamw(weights[n].reshape(r2), grads[n].reshape(r2), mom_m[n].reshape(r2),
                              mom_v[n].reshape(r2), f"adamw_{n}")
            delta[n], new_m[n], new_v[n] = d.reshape(shp), nm.reshape(shp), nv.reshape(shp)

    rs.reduce("c0", [vsum])
    for n in ("w_q", "w_o", "w_kv"):
        update(n)
    rs.finish("c0", [delta["w_kv"]])
    for n in ("conv_w_pw1", "conv_w_pw2"):
        update(n)
    grads.update(big_grads)

    order = list(weights)
    return (loss, grad_x[None], *[grads[n] for n in order], *[delta[n] for n in order],
            *[new_m[n] for n in order], *[new_v[n] for n in order])
```

```python
import functools
import math

import numpy as np
import jax
import jax.numpy as jnp
from jax import lax
from jax.experimental import pallas as pl
from jax.experimental.pallas import tpu as pltpu

F32 = jnp.float32
BF16 = jnp.bfloat16
MESH = pl.DeviceIdType.MESH

D = 1024
DFF = 2816
N_HEADS = 16
N_KV = 4
GROUP = 4
HD = 64
BLK = 128
CONV_W = 31
HALO = 32
N_BUCKETS = 32
MAX_DISTANCE = 128
EPS = 1e-6
NEG_INF = -1e30
TM = 512
TCV = 256
VMEM_LIMIT = 56 * 2 ** 20

ADAM_LR, ADAM_B1, ADAM_B2, ADAM_EPS, ADAM_WD, ADAM_STEP = 0.001, 0.9, 0.999, 1e-08, 0.01, 10


def _cp(*sem):
    return pltpu.CompilerParams(dimension_semantics=sem, vmem_limit_bytes=VMEM_LIMIT)


def _sigmoid(x):
    return 1.0 / (1.0 + jnp.exp(-x))


def _row(tm, n):
    return pl.BlockSpec((tm, n), lambda i: (i, 0))


def _const(shape):
    nd = len(shape)
    return pl.BlockSpec(shape, lambda i: (0,) * nd)


def _weight(shape):
    nd = len(shape)
    return pl.BlockSpec(shape, lambda i: (0,) * nd, pipeline_mode=pl.Buffered(1))


def _layer(shape, l):
    nd = len(shape)
    return pl.BlockSpec((None,) + tuple(shape), lambda i: (l,) + (0,) * nd)


def _dot(a, b):
    return jnp.dot(a, b, preferred_element_type=F32)


def _dot_nt(a, b):
    return lax.dot_general(a, b, (((1,), (1,)), ((), ())), preferred_element_type=F32)


def _dot_tn(a, b):
    return lax.dot_general(a, b, (((0,), (0,)), ((), ())), preferred_element_type=F32)


def _rms(x):
    return lax.rsqrt(jnp.mean(x * x, axis=-1, keepdims=True) + EPS)


def norm_mm_glu(h, g, l, w, b, name):
    T = h.shape[0]
    ns = w.shape[-1]

    def body(h_ref, g_ref, w_ref, b_ref, xn_ref, u_ref, a_ref):
        x = h_ref[...]
        xn = (x * _rms(x) * g_ref[...]).astype(BF16)
        xn_ref[...] = xn
        for s in range(2):
            lo, hi = s * ns, (s + 1) * ns
            u1 = _dot(xn, w_ref[s]) + b_ref[:, lo:hi]
            u2 = _dot(xn, w_ref[2 + s]) + b_ref[:, D + lo:D + hi]
            u_ref[:, lo:hi] = u1.astype(BF16)
            u_ref[:, D + lo:D + hi] = u2.astype(BF16)
            a_ref[:, lo:hi] = (u1 * _sigmoid(u2)).astype(BF16)

    return pl.pallas_call(
        body, name=name, grid=(T // TM,),
        in_specs=[_row(TM, D), _layer((1, D), l), _weight((4, D, ns)), _layer((1, 2 * D), l)],
        out_specs=[_row(TM, D), _row(TM, 2 * D), _row(TM, D)],
        out_shape=[jax.ShapeDtypeStruct((T, D), BF16), jax.ShapeDtypeStruct((T, 2 * D), BF16),
                   jax.ShapeDtypeStruct((T, D), BF16)],
        compiler_params=_cp("parallel"),
    )(h, g, w, b)


SUB = 8


def _make_shifts(sh):
    n = TCV + HALO - SUB
    for r in range(1, SUB):
        for r0 in range(0, n, 40):
            sh[r, r0:r0 + 40, :] = sh[0, pl.ds(r + r0, 40), :]


def _shifted(sh, off, rows, cols):
    return sh[off % SUB, pl.ds(off - off % SUB, rows), cols]


def _conv_taps(sh, w_ref, out_ref, first):
    RB, LB = 32, 512
    for r0 in range(0, TCV, RB):
        for c0 in range(0, out_ref.shape[1], LB):
            acc = jnp.zeros((RB, LB), F32)
            for k in range(CONV_W):
                acc = acc + w_ref[k:k + 1, c0:c0 + LB] * _shifted(sh, first + k + r0, RB, slice(c0, c0 + LB))
            out_ref[r0:r0 + RB, c0:c0 + LB] = acc


def dwconv_ln_silu(a, sm, l, name):
    T = a.shape[0]
    nb = TCV // HALO

    def body(cur_ref, prev_ref, sm_ref, y_ref, s_ref, sh, yb):
        i = pl.program_id(0)
        sh[0, 0:HALO, :] = jnp.where(i > 0, prev_ref[...].astype(F32), 0.0)
        sh[0, HALO:HALO + TCV, :] = cur_ref[...].astype(F32)
        _make_shifts(sh)
        _conv_taps(sh, sm_ref, yb, HALO - (CONV_W - 1))
        y = yb[...] + sm_ref[31:32, :]
        y_ref[...] = y.astype(BF16)
        mu = jnp.mean(y, axis=-1, keepdims=True)
        yc = y - mu
        rstd = lax.rsqrt(jnp.mean(yc * yc, axis=-1, keepdims=True) + EPS)
        z = yc * rstd * sm_ref[32:33, :] + sm_ref[33:34, :]
        s_ref[...] = (z * _sigmoid(z)).astype(BF16)

    return pl.pallas_call(
        body, name=name, grid=(T // TCV,),
        in_specs=[_row(TCV, D), pl.BlockSpec((HALO, D), lambda i: (jnp.maximum(i * nb - 1, 0), 0)),
                  _layer((40, D), l)],
        out_specs=[_row(TCV, D), _row(TCV, D)],
        out_shape=[jax.ShapeDtypeStruct((T, D), BF16), jax.ShapeDtypeStruct((T, D), BF16)],
        scratch_shapes=[pltpu.VMEM((SUB, TCV + HALO, D), F32), pltpu.VMEM((TCV, D), F32)],
        compiler_params=_cp("parallel"),
    )(a, a, sm)


def mm_bias_res(xb, w, b, bl, res, name):
    T, K = xb.shape

    def body(x_ref, w_ref, b_ref, r_ref, o_ref):
        o_ref[...] = _dot(x_ref[...], w_ref[...]) + b_ref[...] + r_ref[...]

    return pl.pallas_call(
        body, name=name, grid=(T // TM,),
        in_specs=[_row(TM, K), _weight((K, D)), _layer((1, D), bl), _row(TM, D)],
        out_specs=_row(TM, D), out_shape=jax.ShapeDtypeStruct((T, D), F32),
        compiler_params=_cp("parallel"),
    )(xb, w, b, res)


def norm_mm_swiglu(h, g, l, w, name):
    T = h.shape[0]
    ns = w.shape[-1]

    def body(h_ref, g_ref, w_ref, xn_ref, gu_ref, f_ref):
        x = h_ref[...]
        xn = (x * _rms(x) * g_ref[...]).astype(BF16)
        xn_ref[...] = xn
        for s in range(2):
            lo, hi = s * ns, (s + 1) * ns
            gate = _dot(xn, w_ref[s])
            up = _dot(xn, w_ref[2 + s])
            gu_ref[:, lo:hi] = gate.astype(BF16)
            gu_ref[:, DFF + lo:DFF + hi] = up.astype(BF16)
            f_ref[:, lo:hi] = (gate * _sigmoid(gate) * up).astype(BF16)

    return pl.pallas_call(
        body, name=name, grid=(T // TM,),
        in_specs=[_row(TM, D), _layer((1, D), l), _weight((4, D, ns))],
        out_specs=[_row(TM, D), _row(TM, 2 * DFF), _row(TM, DFF)],
        out_shape=[jax.ShapeDtypeStruct((T, D), BF16), jax.ShapeDtypeStruct((T, 2 * DFF), BF16),
                   jax.ShapeDtypeStruct((T, DFF), BF16)],
        compiler_params=_cp("parallel"),
    )(h, g, w)


def norm_mm(h, g, gl, w, name, scale=1.0):
    T = h.shape[0]
    N = w.shape[-1]

    def body(h_ref, g_ref, w_ref, xn_ref, o_ref):
        x = h_ref[...]
        xn = (x * _rms(x) * g_ref[...]).astype(BF16)
        xn_ref[...] = xn
        o_ref[...] = (_dot(xn, w_ref[...]) * scale).astype(BF16)

    return pl.pallas_call(
        body, name=name, grid=(T // TM,),
        in_specs=[_row(TM, D), _layer((1, D), gl), _weight((D, N))],
        out_specs=[_row(TM, D), _row(TM, N)],
        out_shape=[jax.ShapeDtypeStruct((T, D), BF16), jax.ShapeDtypeStruct((T, N), BF16)],
        compiler_params=_cp("parallel"),
    )(h, g, w)


QB = 16
QW = GROUP * BLK


def band_mask():
    qi = np.arange(QW)[None, :] % BLK
    kj = np.arange(2 * BLK)[:, None]
    band = ((kj < BLK) & (kj > qi)) | ((kj >= BLK) & (kj - BLK <= qi))
    first = band & (kj >= BLK)
    return np.where(np.stack([first, band]), 0.0, NEG_INF).astype(np.float32)


def _softmax_cols(s, sink):
    m = jnp.maximum(jnp.max(s, axis=0, keepdims=True), sink)
    p = jnp.exp(s - m)
    es = jnp.exp(sink - m)
    inv = 1.0 / (jnp.sum(p, axis=0, keepdims=True) + es)
    return p, inv, es


def _attn_specs(T):
    W = QB * BLK
    qspec = pl.BlockSpec((None, GROUP, HD, W), lambda kv, n: (kv, 0, 0, n))
    kspec = pl.BlockSpec((None, T + BLK, HD), lambda kv, n: (kv, 0, 0))
    ktspec = [pl.BlockSpec((None, HD, W), lambda kv, n: (kv, 0, n)),
              pl.BlockSpec((None, HD, BLK), lambda kv, n: (kv, 0, (n + 1) * QB))]
    bspec = pl.BlockSpec((2, None, 2 * BLK, QW), lambda kv, n: (0, kv, 0, 0))
    sspec = pl.BlockSpec((None, 1, QW), lambda kv, n: (kv, 0, 0))
    return qspec, kspec, ktspec, bspec, sspec


def _attn_block(n, b):
    blk = n * QB + b
    rows = pl.ds(pl.multiple_of(blk * BLK, BLK), 2 * BLK)
    return rows, (jnp.minimum(blk, 1) if b == 0 else 1)


def _band_cols(main_ref, tail_ref, b):
    if b < QB - 1:
        return main_ref[:, b * BLK:(b + 2) * BLK]
    return jnp.concatenate([main_ref[:, b * BLK:], tail_ref[...]], axis=1)


def _heads_side_by_side(ref, qs):
    return jnp.concatenate([ref[g, :, qs] for g in range(GROUP)], axis=1)


def attn_fwd(q, kp, vt, bias, sink, name):
    T = q.shape[3]
    qspec, kspec, ktspec, bspec, sspec = _attn_specs(T)

    def body(q_ref, k_ref, vt_ref, vtt_ref, b_ref, s_ref, o_ref, pb):
        n = pl.program_id(1)

        def scores(b):
            return _dot(k_ref[_attn_block(n, b)[0], :], _heads_side_by_side(q_ref, slice(b * BLK, (b + 1) * BLK)))

        st_next = scores(0)
        for b in range(QB):
            rows, table = _attn_block(n, b)
            qs = slice(b * BLK, (b + 1) * BLK)
            st = st_next
            if b + 1 < QB:
                st_next = scores(b + 1)
            for g in range(GROUP):
                hs = slice(g * BLK, (g + 1) * BLK)
                p, inv, _ = _softmax_cols(st[:, hs] + b_ref[table, :, hs], s_ref[:, hs])
                pb[:, hs] = (p * inv).astype(BF16)
            ot = _dot(_band_cols(vt_ref, vtt_ref, b), pb[...])
            for g in range(GROUP):
                o_ref[g, :, qs] = ot[:, g * BLK:(g + 1) * BLK].astype(BF16)

    return pl.pallas_call(
        body, name=name, grid=(N_KV, T // (QB * BLK)),
        in_specs=[qspec, kspec, *ktspec, bspec, sspec], out_specs=qspec,
        out_shape=jax.ShapeDtypeStruct((N_KV, GROUP, HD, T), BF16),
        scratch_shapes=[pltpu.VMEM((2 * BLK, QW), BF16)],
        compiler_params=_cp("parallel", "parallel"),
    )(q, kp, vt, vt, bias, sink)


def attn_bwd(q, kp, kt, vp, bias, sink, o, do, name):
    T = q.shape[3]
    qspec, kspec, ktspec, bspec, sspec = _attn_specs(T)

    def body(q_ref, k_ref, kt_ref, ktt_ref, v_ref, b_ref, s_ref, o_ref, do_ref,
             dq_ref, dk_ref, dv_ref, db_ref, ds_ref, pb, dsb):
        n = pl.program_id(1)

        @pl.when(n == 0)
        def _():
            dk_ref[...] = jnp.zeros_like(dk_ref)
            dv_ref[...] = jnp.zeros_like(dv_ref)
            db_ref[...] = jnp.zeros_like(db_ref)
            ds_ref[...] = jnp.zeros_like(ds_ref)

        def products(b):
            rows = _attn_block(n, b)[0]
            qs = slice(b * BLK, (b + 1) * BLK)
            q4, do4 = _heads_side_by_side(q_ref, qs), _heads_side_by_side(do_ref, qs)
            return q4, do4, _dot(k_ref[rows, :], q4), _dot(v_ref[rows, :], do4)

        ahead = products(0)
        for b in range(QB):
            rows, table = _attn_block(n, b)
            qs = slice(b * BLK, (b + 1) * BLK)
            q4, do4, st, dpt = ahead
            if b + 1 < QB:
                ahead = products(b + 1)
            for g in range(GROUP):
                hs = slice(g * BLK, (g + 1) * BLK)
                p, inv, es = _softmax_cols(st[:, hs] + b_ref[table, :, hs], s_ref[:, hs])
                probs = p * inv
                delta = jnp.sum(do_ref[g, :, qs].astype(F32) * o_ref[g, :, qs].astype(F32), axis=0, keepdims=True)
                dS = probs * (dpt[:, hs] - delta)
                ds_ref[:, hs] += -(es * inv) * delta
                db_ref[:, hs] += dS
                pb[:, hs] = probs.astype(BF16)
                dsb[:, hs] = dS.astype(BF16)
            dqt = _dot(_band_cols(kt_ref, ktt_ref, b), dsb[...]) * (HD ** -0.5)
            for g in range(GROUP):
                dq_ref[g, :, qs] = dqt[:, g * BLK:(g + 1) * BLK].astype(BF16)
            dk_ref[rows, :] += _dot_nt(dsb[...], q4)
            dv_ref[rows, :] += _dot_nt(pb[...], do4)

    kout = pl.BlockSpec((None, T + BLK, HD), lambda kv, n: (kv, 0, 0))
    dbspec = pl.BlockSpec((None, 2 * BLK, QW), lambda kv, n: (kv, 0, 0))
    return pl.pallas_call(
        body, name=name, grid=(N_KV, T // (QB * BLK)),
        in_specs=[qspec, kspec, *ktspec, kspec, bspec, sspec, qspec, qspec],
        out_specs=[qspec, kout, kout, dbspec, sspec],
        out_shape=[jax.ShapeDtypeStruct((N_KV, GROUP, HD, T), BF16),
                   jax.ShapeDtypeStruct((N_KV, T + BLK, HD), F32), jax.ShapeDtypeStruct((N_KV, T + BLK, HD), F32),
                   jax.ShapeDtypeStruct((N_KV, 2 * BLK, QW), F32), jax.ShapeDtypeStruct((N_KV, 1, QW), F32)],
        scratch_shapes=[pltpu.VMEM((2 * BLK, QW), BF16), pltpu.VMEM((2 * BLK, QW), BF16)],
        compiler_params=_cp("parallel", "arbitrary"),
    )(q, kp, kt, kt, vp, bias, sink, o, do)


def final_loss(h, g, target, name):
    T = h.shape[0]

    def body(h_ref, g_ref, t_ref, dh_ref, st_ref):
        i = pl.program_id(0)

        @pl.when(i == 0)
        def _():
            st_ref[...] = jnp.zeros_like(st_ref)

        x = h_ref[...]
        r = _rms(x)
        xh = x * r
        e = xh * g_ref[...] - t_ref[...]
        loss = 0.5 * jnp.sum(jnp.mean(e * e, axis=-1, keepdims=True))
        dy = e * (1.0 / D)
        st_ref[0:1, :] += jnp.sum(dy * xh, axis=0, keepdims=True)
        lane = lax.broadcasted_iota(jnp.int32, (1, D), 1)
        st_ref[1:2, :] += jnp.where(lane == 0, loss, 0.0)
        dxh = dy * g_ref[...]
        dh_ref[...] = r * (dxh - xh * jnp.mean(dxh * xh, axis=-1, keepdims=True))

    return pl.pallas_call(
        body, name=name, grid=(T // TM,),
        in_specs=[_row(TM, D), _const((1, D)), _row(TM, D)],
        out_specs=[_row(TM, D), _const((8, D))],
        out_shape=[jax.ShapeDtypeStruct((T, D), F32), jax.ShapeDtypeStruct((8, D), F32)],
        compiler_params=_cp("arbitrary"),
    )(h, g, target)


def mm_dw(x, dy, name, tn, slots, colsum=False):
    T, K = x.shape
    split = dy.ndim == 3
    N = dy.shape[-1] * (2 if split else 1)
    tt = min(T, 2048 if K <= 1024 else 1024)
    nt = T // tt
    ns = N // slots
    per = ns // tn

    def body(x_ref, dy_ref, *rest):
        if colsum:
            dw_ref, cs_ref, acc, cacc = rest
        else:
            dw_ref, acc = rest
        t = pl.program_id(1)

        @pl.when(t == 0)
        def _():
            acc[...] = jnp.zeros_like(acc)
            if colsum:
                cacc[...] = jnp.zeros_like(cacc)

        dyv = dy_ref[...]
        acc[...] += _dot_tn(x_ref[...].astype(BF16), dyv.astype(BF16))
        if colsum:
            cacc[...] += jnp.sum(dyv.astype(F32), axis=0, keepdims=True)

        @pl.when(t == nt - 1)
        def _():
            dw_ref[...] = acc[...].astype(BF16)
            if colsum:
                cs_ref[...] = cacc[...]

    if split:
        half = N // 2 // tn
        dy_spec = pl.BlockSpec((None, tt, tn), lambda j, t: (j // half, t, j % half))
    else:
        dy_spec = pl.BlockSpec((tt, tn), lambda j, t: (t, j))
    out_specs = [pl.BlockSpec((None, K, tn), lambda j, t: (j // per, 0, j % per))]
    out_shape = [jax.ShapeDtypeStruct((slots, K, ns), BF16)]
    scratch = [pltpu.VMEM((K, tn), F32)]
    if colsum:
        out_specs.append(pl.BlockSpec((1, tn), lambda j, t: (0, j)))
        out_shape.append(jax.ShapeDtypeStruct((1, N), F32))
        scratch.append(pltpu.VMEM((1, tn), F32))
    res = pl.pallas_call(
        body, name=name, grid=(N // tn, nt),
        in_specs=[pl.BlockSpec((tt, K), lambda j, t: (t, 0)), dy_spec],
        out_specs=out_specs, out_shape=out_shape, scratch_shapes=scratch,
        compiler_params=_cp("parallel", "arbitrary"),
    )(x, dy)
    return tuple(res) if colsum else res[0]


def mmT_swiglu_bwd(dh, w, gu, name, after=()):
    T = dh.shape[0]
    cw = 256

    def body(dh_ref, w_ref, gu_ref, *rest):
        du_ref = rest[-1]
        dhb = dh_ref[...].astype(BF16)
        ahead = _dot_nt(dhb, w_ref[0:cw, :])
        for lo in range(0, DFF, cw):
            hi = lo + cw
            df = ahead
            if hi < DFF:
                ahead = _dot_nt(dhb, w_ref[hi:hi + cw, :])
            gate = gu_ref[:, lo:hi].astype(F32)
            up = gu_ref[:, DFF + lo:DFF + hi].astype(F32)
            sg = _sigmoid(gate)
            silu = gate * sg
            du_ref[:, lo:hi] = (df * (up * (sg + silu * (1.0 - sg)))).astype(BF16)
            du_ref[:, DFF + lo:DFF + hi] = (df * silu).astype(BF16)

    return pl.pallas_call(
        body, name=name, grid=(T // TM,),
        in_specs=[_row(TM, D), _weight((DFF, D)), _row(TM, 2 * DFF)] + [ANY] * len(after),
        out_specs=_row(TM, 2 * DFF), out_shape=jax.ShapeDtypeStruct((T, 2 * DFF), BF16),
        compiler_params=_cp("parallel"),
    )(dh, w, gu, *after)


def mmT_rmsbwd(du, w, h, g, gl, dh_in, name):
    split = du.ndim == 3
    T = du.shape[-2]
    N = du.shape[-1] * (2 if split else 1)
    slots = w.shape[0]
    ns = N // slots

    RH = TM // 2

    def piece(du_ref, s, rows):
        if split:
            per = slots // 2
            return du_ref[s // per, rows, (s % per) * ns:(s % per + 1) * ns]
        return du_ref[rows, s * ns:(s + 1) * ns]

    def body(du_ref, w_ref, h_ref, g_ref, di_ref, dh_ref, dg_ref):
        i = pl.program_id(0)

        @pl.when(i == 0)
        def _():
            dg_ref[...] = jnp.zeros_like(dg_ref)

        def products(k):
            rows = slice(k * RH, (k + 1) * RH)
            dxn = _dot_nt(piece(du_ref, 0, rows), w_ref[0])
            for s in range(1, slots):
                dxn = dxn + _dot_nt(piece(du_ref, s, rows), w_ref[s])
            return dxn

        ahead = products(0)
        for k in range(TM // RH):
            rows = slice(k * RH, (k + 1) * RH)
            dxn = ahead
            if (k + 1) * RH < TM:
                ahead = products(k + 1)
            x = h_ref[rows, :]
            r = _rms(x)
            xh = x * r
            dg_ref[0:1, :] += jnp.sum(dxn * xh, axis=0, keepdims=True)
            dxh = dxn * g_ref[...]
            dh_ref[rows, :] = di_ref[rows, :] + r * (dxh - xh * jnp.mean(dxh * xh, axis=-1, keepdims=True))

    return pl.pallas_call(
        body, name=name, grid=(T // TM,),
        in_specs=[pl.BlockSpec((2, TM, N // 2), lambda i: (0, i, 0)) if split else _row(TM, N),
                  _weight((slots, D, ns)), _row(TM, D), _layer((1, D), gl), _row(TM, D)],
        out_specs=[_row(TM, D), _const((8, D))],
        out_shape=[jax.ShapeDtypeStruct((T, D), F32), jax.ShapeDtypeStruct((8, D), F32)],
        compiler_params=_cp("arbitrary"),
    )(du, w, h, g, dh_in)


def mmT(dh, w, name):
    T = dh.shape[0]
    N = w.shape[0]

    def body(dh_ref, w_ref, o_ref):
        o_ref[...] = _dot_nt(dh_ref[...].astype(BF16), w_ref[...]).astype(BF16)

    return pl.pallas_call(
        body, name=name, grid=(T // TM,),
        in_specs=[_row(TM, D), _weight((N, D))],
        out_specs=_row(TM, N), out_shape=jax.ShapeDtypeStruct((T, N), BF16),
        compiler_params=_cp("parallel"),
    )(dh, w)


def mmT_lnbwd(dh, w, y, sm, l, name):
    T = dh.shape[0]

    def body(dh_ref, w_ref, y_ref, sm_ref, dy_ref, st_ref):
        i = pl.program_id(0)

        @pl.when(i == 0)
        def _():
            st_ref[...] = jnp.zeros_like(st_ref)

        ds = _dot_nt(dh_ref[...].astype(BF16), w_ref[...])
        y = y_ref[...].astype(F32)
        mu = jnp.mean(y, axis=-1, keepdims=True)
        yc = y - mu
        rstd = lax.rsqrt(jnp.mean(yc * yc, axis=-1, keepdims=True) + EPS)
        xh = yc * rstd
        gam = sm_ref[32:33, :]
        z = xh * gam + sm_ref[33:34, :]
        sg = _sigmoid(z)
        dz = ds * sg * (1.0 + z * (1.0 - sg))
        st_ref[0:1, :] += jnp.sum(dz * xh, axis=0, keepdims=True)
        st_ref[1:2, :] += jnp.sum(dz, axis=0, keepdims=True)
        dxh = dz * gam
        dy = rstd * (dxh - jnp.mean(dxh, axis=-1, keepdims=True) - xh * jnp.mean(dxh * xh, axis=-1, keepdims=True))
        st_ref[2:3, :] += jnp.sum(dy, axis=0, keepdims=True)
        dy_ref[...] = dy.astype(BF16)

    return pl.pallas_call(
        body, name=name, grid=(T // TM,),
        in_specs=[_row(TM, D), _weight((D, D)), _row(TM, D), _layer((40, D), l)],
        out_specs=[_row(TM, D), _const((8, D))],
        out_shape=[jax.ShapeDtypeStruct((T, D), BF16), jax.ShapeDtypeStruct((8, D), F32)],
        compiler_params=_cp("arbitrary"),
    )(dh, w, y, sm)


CH = 512


def dwconv_glu_bwd(dy, a, u, sm, smrev, l, name):
    T = dy.shape[0]
    nr, nc = T // TCV, D // CH
    nb = TCV // HALO
    last = T // HALO - 1

    def body(dy_ref, dyn_ref, a_ref, ap_ref, u1_ref, u2_ref, sm_ref, rev_ref, du_ref, dw_ref, shd, sha, da):
        i = pl.program_id(0)
        r = i % nr

        @pl.when(r == 0)
        def _():
            dw_ref[...] = jnp.zeros_like(dw_ref)

        shd[0, 0:TCV, :] = dy_ref[...].astype(F32)
        shd[0, TCV:TCV + HALO, :] = jnp.where(r < nr - 1, dyn_ref[...].astype(F32), 0.0)
        sha[0, 0:HALO, :] = jnp.where(r > 0, ap_ref[...].astype(F32), 0.0)
        sha[0, HALO:HALO + TCV, :] = a_ref[...].astype(F32)
        _make_shifts(shd)
        _make_shifts(sha)
        _conv_taps(shd, rev_ref, da, 0)
        for kg in range(0, CONV_W, SUB):
            taps = range(kg, min(kg + SUB, CONV_W))
            part = [jnp.zeros((SUB, CH), F32) for _ in taps]
            for r0 in range(0, TCV, SUB):
                d = shd[0, r0:r0 + SUB, :]
                for j, k in enumerate(taps):
                    part[j] = part[j] + d * _shifted(sha, HALO - (CONV_W - 1) + k + r0, SUB, slice(None))
            for j, k in enumerate(taps):
                dw_ref[k:k + 1, :] += jnp.sum(part[j], axis=0, keepdims=True)
        dav = da[...]
        u1 = u1_ref[...].astype(F32)
        sg = _sigmoid(u2_ref[...].astype(F32))
        du_ref[0] = (dav * sg).astype(BF16)
        du_ref[1] = (dav * u1 * sg * (1.0 - sg)).astype(BF16)

    tile = lambda i: (i % nr, i // nr)
    in_specs = [pl.BlockSpec((TCV, CH), tile),
                pl.BlockSpec((HALO, CH), lambda i: (jnp.minimum((i % nr + 1) * nb, last), i // nr)),
                pl.BlockSpec((TCV, CH), tile),
                pl.BlockSpec((HALO, CH), lambda i: (jnp.maximum((i % nr) * nb - 1, 0), i // nr)),
                pl.BlockSpec((TCV, CH), tile), pl.BlockSpec((TCV, CH), lambda i: (i % nr, nc + i // nr)),
                pl.BlockSpec((None, 40, CH), lambda i: (l, 0, i // nr)),
                pl.BlockSpec((None, 40, CH), lambda i: (l, 0, i // nr))]
    return pl.pallas_call(
        body, name=name, grid=(nr * nc,), in_specs=in_specs,
        out_specs=[pl.BlockSpec((2, TCV, CH), lambda i: (0, i % nr, i // nr)),
                   pl.BlockSpec((32, CH), lambda i: (0, i // nr))],
        out_shape=[jax.ShapeDtypeStruct((2, T, D), BF16), jax.ShapeDtypeStruct((32, D), F32)],
        scratch_shapes=[pltpu.VMEM((SUB, TCV + HALO, CH), F32), pltpu.VMEM((SUB, TCV + HALO, CH), F32),
                        pltpu.VMEM((TCV, CH), F32)],
        compiler_params=_cp("arbitrary"),
    )(dy, dy, a, a, u, u, sm, smrev)


def _rows_tile(R):
    for t in (512, 256, 128, 64, 32, 16, 8):
        if R % t == 0:
            return t
    return R


def add8_into(J, l, g, others, where, name):
    R, C = g.shape[2:]
    tr = R // 2

    def body(w_ref, g_ref, x_ref, j_in, j_ref):
        acc = g_ref[...].astype(F32)
        for k in range(7):
            acc = acc + x_ref[k].astype(F32)
        j_ref[...] = acc

    return pl.pallas_call(
        body, name=name,
        grid_spec=pltpu.PrefetchScalarGridSpec(
            num_scalar_prefetch=1, grid=(R // tr,),
            in_specs=[pl.BlockSpec((None, None, tr, C), lambda i, w: (w[0], w[1], i, 0)),
                      pl.BlockSpec((7, tr, C), lambda i, w: (0, i, 0)), ANY],
            out_specs=pl.BlockSpec((None, None, tr, C), lambda i, w: (l, w[1], i, 0))),
        out_shape=jax.ShapeDtypeStruct(J.shape, F32), input_output_aliases={3: 0},
        compiler_params=_cp("parallel"),
    )(where, g, others, J)


def adamw(w, g, m, v, name, copy_g=False):
    R, C = w.shape
    tr = _rows_tile(R)

    def body(w_ref, g_ref, m_ref, v_ref, *outs):
        d_ref, nm_ref, nv_ref = outs[-3:]
        gv = g_ref[...]
        if copy_g:
            outs[0][...] = gv
        nm = ADAM_B1 * m_ref[...] + (1.0 - ADAM_B1) * gv
        nv = ADAM_B2 * v_ref[...] + (1.0 - ADAM_B2) * (gv * gv)
        m_hat = nm / (1.0 - ADAM_B1 ** ADAM_STEP)
        v_hat = nv / (1.0 - ADAM_B2 ** ADAM_STEP)
        d_ref[...] = -ADAM_LR * (m_hat / (jnp.sqrt(v_hat) + ADAM_EPS) + ADAM_WD * w_ref[...])
        nm_ref[...] = nm
        nv_ref[...] = nv

    sd = jax.ShapeDtypeStruct((R, C), F32)
    n_out = 4 if copy_g else 3
    return pl.pallas_call(
        body, name=name, grid=(R // tr,),
        in_specs=[_row(tr, C)] * 4, out_specs=[_row(tr, C)] * n_out, out_shape=[sd] * n_out,
        compiler_params=_cp("parallel"),
    )(w, g, m, v)


ANY = pl.BlockSpec(memory_space=pl.ANY)
HBM = pl.BlockSpec(memory_space=pltpu.HBM)
SEM = pl.BlockSpec(memory_space=pltpu.SEMAPHORE)
EFFECT = pltpu.SideEffectType.DATAFLOW_SIDE_EFFECTING


def _place():
    x, y, c = lax.axis_index("x"), lax.axis_index("y"), lax.axis_index("c")
    chips = [(1 - x, y), (x, 1 - y), (1 - x, 1 - y)]
    return x, y, c, chips


def _copy(src, dst, send, recv, k, to):
    return pltpu.make_async_remote_copy(src_ref=src, dst_ref=dst, send_sem=send.at[k], recv_sem=recv.at[k],
                                        device_id=to, device_id_type=MESH)


def xchg_start(name, bufs, plan, n, after=()):
    nb = len(bufs)

    na = len(after)

    def body(*refs):
        send, recv, token = refs[nb + na], refs[nb + na + 1], refs[-1]
        for k, (src, dst, to) in enumerate(plan(refs[:nb])):
            _copy(src, dst, send, recv, k, to).start()
        token[...] = jnp.zeros_like(token)

    outs = pl.pallas_call(
        body, name=name,
        out_shape=(pltpu.SemaphoreType.DMA((n,)), pltpu.SemaphoreType.DMA((n,)),
                   *[pltpu.HBM(b.shape, b.dtype) for b in bufs], jax.ShapeDtypeStruct((8, 128), F32)),
        in_specs=[HBM] * nb + [ANY] * na,
        out_specs=(SEM, SEM, *[HBM] * nb, pl.BlockSpec(memory_space=pltpu.VMEM)),
        input_output_aliases={i: 2 + i for i in range(nb)},
        compiler_params=pltpu.CompilerParams(has_side_effects=EFFECT),
    )(*[pltpu.with_memory_space_constraint(b, pltpu.HBM) for b in bufs], *after)
    return dict(name=name, send=outs[0], recv=outs[1], bufs=list(outs[2:2 + nb]), plan=plan), outs[-1]


def xchg_wait(flight, after):
    bufs, plan = flight["bufs"], flight["plan"]
    nb = len(bufs)

    def body(*refs):
        send, recv = refs[nb], refs[nb + 1]
        for k, (src, dst, to) in enumerate(plan(refs[:nb])):
            cp = _copy(src, dst, send, recv, k, to)
            cp.wait_send()
            cp.wait_recv()

    outs = pl.pallas_call(
        body, name=flight["name"] + "_wait",
        out_shape=tuple(pltpu.HBM(b.shape, b.dtype) for b in bufs),
        in_specs=[HBM] * nb + [SEM, SEM] + [ANY] * len(after),
        out_specs=tuple([HBM] * nb), input_output_aliases={i: i for i in range(nb)},
        compiler_params=pltpu.CompilerParams(has_side_effects=EFFECT),
    )(*bufs, flight["send"], flight["recv"], *after)
    return list(outs)


def _flip(k, x, y, c):
    return ((1 - x) if k & 4 else x, (1 - y) if k & 2 else y, (1 - c) if k & 1 else c)


def cast_into_slot(srcs, name, after):
    me = (2 * lax.axis_index("x") + lax.axis_index("y")).astype(jnp.int32).reshape(1)
    ns = len(srcs)

    def body(me_ref, *refs):
        outs = refs[ns + len(after):]
        for t in range(ns):
            outs[t][...] = refs[t][...].astype(outs[t].dtype).reshape(outs[t].shape)

    in_specs, out_specs, out_shape = [], [], []
    for arr, l in srcs:
        if l is None:
            in_specs.append(pl.BlockSpec(arr.shape, lambda i, w, nd=arr.ndim: (0,) * nd))
            a2, b, dt = (arr.shape[0] // 2, arr.shape[1], BF16) if arr.ndim == 2 else (arr.shape[1], arr.shape[2], F32)
        else:
            in_specs.append(pl.BlockSpec((None,) + arr.shape[1:], lambda i, w, l=l: (l, 0, 0)))
            a2, b, dt = arr.shape[1] // 2, arr.shape[2], BF16
        out_specs.append(pl.BlockSpec((None, 2, a2, b), lambda i, w: (w[0], 0, 0, 0)))
        out_shape.append(jax.ShapeDtypeStruct((4, 2, a2, b), dt))
    in_specs += [ANY] * len(after)
    return pl.pallas_call(
        body, name=name,
        grid_spec=pltpu.PrefetchScalarGridSpec(num_scalar_prefetch=1, grid=(1,), in_specs=in_specs,
                                               out_specs=out_specs),
        out_shape=out_shape, compiler_params=_cp("arbitrary"),
    )(me, *[arr for arr, _ in srcs], *after)


class WeightGather:
    def __init__(self, source, groups):
        self.names = dict(groups)
        self.ici, self.d2d = {}, {}
        self.token = None
        for gname, names in groups:
            nt = len(names)
            after = [] if self.token is None else [self.token]
            lands = cast_into_slot([source(n) for n in names], f"ag_cast_{gname}", after)

            def plan(refs, nt=nt):
                x, y, c, chips = _place()
                out = []
                for t in range(nt):
                    mine = refs[t].at[2 * x + y, c]
                    out += [(mine, mine, (cx, cy, c)) for cx, cy in chips]
                return out

            self.ici[gname], self.token = xchg_start(f"ag_ici_{gname}", lands, plan, 3 * nt, after=after)

    def forward(self, gname, after):
        nt = len(self.names[gname])
        lands = xchg_wait(self.ici.pop(gname), after)

        def plan(refs):
            x, y, c, chips = _place()
            out = []
            for t in range(nt):
                for cx, cy in chips:
                    piece = refs[t].at[2 * cx + cy, c]
                    out.append((piece, piece, (x, y, 1 - c)))
            return out

        self.d2d[gname], token = xchg_start(f"ag_d2d_{gname}", lands, plan, 3 * nt)
        return token

    def get(self, gname, after):
        lands = xchg_wait(self.d2d.pop(gname), after)
        return dict(zip(self.names[gname], lands))


class GradReduce:
    def __init__(self, kinds):
        self.J = {k: lax.empty((L, 2, a2, b), F32) for k, (L, a2, b) in kinds.items()}
        self.x, self.j = {}, {}

    @staticmethod
    def _where(name):
        kind, _, l = name.partition("_")
        return kind, int(l or 0)

    def send(self, gname, grads, after=()):
        names = list(grads)
        nt = len(names)
        gs = [grads[n] for n in names]
        xs = [lax.empty((7,) + g.shape[2:], g.dtype) for g in gs]

        def plan(refs):
            x, y, c, _ = _place()
            out = []
            for t in range(nt):
                for k in range(1, 8):
                    px, py, pc = _flip(k, x, y, c)
                    out.append((refs[t].at[2 * px + py, pc], refs[nt + t].at[k - 1], (px, py, pc)))
            return out

        flight, token = xchg_start(f"rs_x_{gname}", gs + xs, plan, 7 * nt, after=after)
        self.x[gname] = (names, flight)
        return token

    def reduce(self, gname, after):
        names, flight = self.x.pop(gname)
        nt = len(names)
        bufs = xchg_wait(flight, after)
        mine = jnp.stack([2 * lax.axis_index("x") + lax.axis_index("y"), lax.axis_index("c")]).astype(jnp.int32)
        where = [self._where(n) for n in names]
        js = [add8_into(self.J[kind], l, bufs[t], bufs[nt + t], mine, f"rs_add_{names[t]}")
              for t, (kind, l) in enumerate(where)]

        def plan(refs):
            x, y, c, _ = _place()
            out = []
            for t in range(nt):
                half = refs[t].at[where[t][1], c]
                out.append((half, half, (x, y, 1 - c)))
            return out

        flight, token = xchg_start(f"rs_join_{gname}", js, plan, nt)
        self.j[gname] = (where, flight)
        return token

    def finish(self, gname, after):
        where, flight = self.j.pop(gname)
        for (kind, _), j in zip(where, xchg_wait(flight, after)):
            self.J[kind] = j


def small_allreduce_start(v, after):
    me = 4 * lax.axis_index("x") + 2 * lax.axis_index("y") + lax.axis_index("c")
    land = lax.dynamic_update_slice(lax.empty((8,) + v.shape, v.dtype), v[None], (me, 0, 0))

    def plan(refs):
        x, y, c, _ = _place()
        return [(refs[0], refs[1].at[4 * x + 2 * y + c], _flip(k, x, y, c)) for k in range(1, 8)]

    return xchg_start("small_allreduce", [v, land], plan, 7, after=after)


def sum8(all8, name):
    def body(x_ref, o_ref):
        acc = x_ref[0]
        for d in range(1, 8):
            acc = acc + x_ref[d]
        o_ref[...] = acc

    return pl.pallas_call(
        body, name=name,
        in_specs=[pl.BlockSpec(memory_space=pltpu.VMEM)], out_specs=pl.BlockSpec(memory_space=pltpu.VMEM),
        out_shape=jax.ShapeDtypeStruct(all8.shape[1:], F32),
        compiler_params=pltpu.CompilerParams(vmem_limit_bytes=VMEM_LIMIT),
    )(all8)


AG_GROUPS = (("a0", ("pw1_0", "pw2_0", "small")), ("f0", ("up_0", "down_0")),
             ("l1", ("pw1_1", "pw2_1", "up_1", "down_1")), ("l2", ("kv", "wq_0", "wo_0", "up_2", "down_2")),
             ("l3", ("wq_1", "wo_1", "up_3", "down_3")))


def _bucket_table():
    qi = np.arange(BLK)[:, None]
    kj = np.arange(2 * BLK)[None, :]
    d = np.maximum(qi + BLK - kj, 0)
    max_exact = N_BUCKETS // 2
    log_ratio = (np.log(np.maximum(d, 1).astype(np.float32) / np.float32(max_exact))
                 / np.float32(math.log(MAX_DISTANCE / max_exact))).astype(np.float32)
    large = max_exact + (log_ratio * np.float32(N_BUCKETS - max_exact)).astype(np.int32)
    large = np.minimum(large, N_BUCKETS - 1)
    return np.where(d < max_exact, d, large).astype(np.int32)


def _heads_major(a, nh):
    T = a.shape[0]
    return a.reshape(T, nh, HD).transpose(1, 0, 2)


def _heads_minor(a):
    nh, T, _ = a.shape
    return a.transpose(1, 0, 2).reshape(T, nh * HD)


def _slots(land):
    return land.reshape(4, 2 * land.shape[2], land.shape[3])


def _rows(land):
    return land.reshape(8 * land.shape[2], land.shape[3])


def _gview(g):
    s, K, n = g.shape
    return g.reshape(4, 2, K // 2, n) if s == 4 else g.reshape(4, 2, K // 8, n)


def _gate(a, token):
    return a * (1.0 + token[0, 0])


def _conv_small(f_small):
    fs = f_small.transpose(1, 2, 0, 3).reshape(2, 40, D)
    b_pw1 = f_small[:, :, 35:37, :].transpose(1, 0, 2, 3).reshape(2, 1, 2 * D)
    rev = jnp.concatenate([fs[:, CONV_W - 1::-1], jnp.zeros((2, 40 - CONV_W, D), F32)], axis=1)
    return dict(conv=fs, conv_rev=rev, b_pw1=b_pw1, b_pw2=fs[:, 34:35])


def run_step(x, target, P, ag, rs):
    T = x.shape[0]
    zero = jnp.zeros((1, 1, D), F32)
    nm, nf = P["norm_mix"], P["norm_ffn"]
    ag.forward("a0", [ag.token])
    W = ag.get("a0", [])
    sm = _conv_small(W["small"])
    h = x
    saved = []
    for l in range(2):
        xn, u, a = norm_mm_glu(h, nm, l, _slots(W[f"pw1_{l}"]), sm["b_pw1"], f"f_pw1_{l}")
        y, s = dwconv_ln_silu(a, sm["conv"], l, f"f_conv_{l}")
        b2 = sm["b_pw2"]
        if l == 0:
            b2 = _gate(b2, ag.forward("f0", [s]))
        h1 = mm_bias_res(s, _rows(W[f"pw2_{l}"]), b2, l, h, f"f_pw2_{l}")
        if l == 0:
            W.update(ag.get("f0", [h1]))
        xn2, gu, f = norm_mm_swiglu(h1, nf, l, _slots(W[f"up_{l}"]), f"f_up_{l}")
        nxt = "l1" if l == 0 else "l2"
        h2 = mm_bias_res(f, _rows(W[f"down_{l}"]), _gate(zero, ag.forward(nxt, [f])), 0, h1, f"f_down_{l}")
        W.update(ag.get(nxt, [h2]))
        saved.append(dict(h=h, xn=xn, u=u, a=a, y=y, s=s, h1=h1, xn2=xn2, gu=gu, f=f))
        h = h2
    h_kv = h
    kvn, kv = norm_mm(h, P["norm_kv"], 0, _rows(W["kv"]), "f_kv")
    kp = jnp.pad(_heads_major(kv[:, :N_KV * HD], N_KV), ((0, 0), (BLK, 0), (0, 0)))
    vp = jnp.pad(_heads_major(kv[:, N_KV * HD:], N_KV), ((0, 0), (BLK, 0), (0, 0)))
    kvt = jnp.pad(kv.T.reshape(2, N_KV, HD, T), ((0, 0), (0, 0), (0, 0), (BLK, 0)))
    kt, vt = kvt[0], kvt[1]
    bucket = _bucket_table()
    onehot = jnp.asarray(np.eye(N_BUCKETS, dtype=np.float32)[bucket])
    bias = jnp.einsum("qkb,bh->hkq", onehot, P["rel_bias"], precision=lax.Precision.HIGHEST)
    bias = bias.reshape(N_KV, GROUP, 2 * BLK, BLK).transpose(0, 2, 1, 3).reshape(1, N_KV, 2 * BLK, QW)
    bias = bias + jnp.asarray(band_mask())[:, None]
    for j in range(2):
        l = 2 + j
        xn, q = norm_mm(h, nm, l, _rows(W[f"wq_{j}"]), f"f_q_{j}", scale=HD ** -0.5)
        qh = q.T.reshape(N_KV, GROUP, HD, T)
        sink = jnp.broadcast_to(P["sinks"][j].reshape(N_KV, GROUP, 1), (N_KV, GROUP, BLK)).reshape(N_KV, 1, QW)
        oh = attn_fwd(qh, kp, vt, bias, sink, f"f_attn_{j}")
        attn = oh.reshape(N_HEADS * HD, T).T
        h1 = mm_bias_res(attn, _rows(W[f"wo_{j}"]), zero, 0, h, f"f_wo_{j}")
        xn2, gu, f = norm_mm_swiglu(h1, nf, l, _slots(W[f"up_{l}"]), f"f_up_{l}")
        zg = _gate(zero, ag.forward("l3", [f])) if j == 0 else zero
        h2 = mm_bias_res(f, _rows(W[f"down_{l}"]), zg, 0, h1, f"f_down_{l}")
        if j == 0:
            W.update(ag.get("l3", [h2]))
        saved.append(dict(h=h, xn=xn, qh=qh, oh=oh, sink=sink, attn=attn, h1=h1, xn2=xn2, gu=gu, f=f))
        h = h2

    dh, st_final = final_loss(h, P["norm_final"], target, "loss_head")

    S = dict(norm_ffn=[None] * 4, norm_mix=[None] * 4, conv=[None] * 2, taps=[None] * 2, b_pw1=[None] * 2,
             b_pw2=[None] * 2, sinks=[None] * 2)

    def ffn_bwd(dh, sv, l, nf, after=()):
        du = mmT_swiglu_bwd(dh, _rows(W[f"down_{l}"]), sv["gu"], f"b_down_{l}", after)
        gd = mm_dw(sv["f"], dh, f"w_down_{l}", 512, 1)
        gu = mm_dw(sv["xn2"], du, f"w_up_{l}", DFF // 2, 4)
        dh, dg = mmT_rmsbwd(du, _slots(W[f"up_{l}"]), sv["h1"], nf, l, dh, f"b_up_{l}")
        S["norm_ffn"][l] = dg[0]
        return dh, {f"down_{l}": _gview(gd), f"up_{l}": _gview(gu)}

    dk = dv = dbias = None
    sent = []
    for j in (1, 0):
        l = 2 + j
        sv = saved[l]
        dh, grads = ffn_bwd(dh, sv, l, nf, sent)
        dattn = mmT(dh, _rows(W[f"wo_{j}"]), f"b_wo_{j}")
        grads[f"wo_{j}"] = _gview(mm_dw(sv["attn"], dh, f"w_wo_{j}", 512, 1))
        doh = dattn.T.reshape(N_KV, GROUP, HD, T)
        dqh, dkj, dvj, dbj, dsj = attn_bwd(sv["qh"], kp, kt, vp, bias, sv["sink"], sv["oh"], doh, f"b_attn_{j}")
        dq = dqh.reshape(N_HEADS * HD, T).T
        grads[f"wq_{j}"] = _gview(mm_dw(sv["xn"], dq, f"w_q_{j}", 512, 1))
        dh, dg = mmT_rmsbwd(dq, _rows(W[f"wq_{j}"])[None], sv["h"], nm, l, dh, f"b_q_{j}")
        S["norm_mix"][l] = dg[0]
        S["sinks"][j] = jnp.sum(dsj.reshape(N_HEADS, BLK), axis=1)
        dk = dkj if dk is None else dk + dkj
        dv = dvj if dv is None else dv + dvj
        dbias = dbj if dbias is None else dbias + dbj
        if j == 1:
            sent = [rs.send("l3", grads)]

    dkv = jnp.concatenate([_heads_minor(dk[:, BLK:]), _heads_minor(dv[:, BLK:])], axis=1).astype(BF16)
    grads["kv"] = _gview(mm_dw(kvn, dkv, "w_kv", 512, 1))
    dh, dg = mmT_rmsbwd(dkv, _rows(W["kv"])[None], h_kv, P["norm_kv"], 0, dh, "b_kv")
    S["norm_kv"] = dg[0]
    dbh = dbias.reshape(N_KV, 2 * BLK, GROUP, BLK)
    S["rel_bias"] = jnp.einsum("vkgq,qkb->bvg", dbh, onehot, precision=lax.Precision.HIGHEST).reshape(N_BUCKETS, N_HEADS)
    sent = [rs.send("l2", grads)]
    nf = _gate(nf, rs.reduce("l3", [dh]))

    for l in (1, 0):
        sv = saved[l]
        dh, grads = ffn_bwd(dh, sv, l, nf, sent)
        conv = sm["conv"]
        if l == 0:
            conv = _gate(conv, rs.send("f0", grads))
            grads = {}
        dy, st = mmT_lnbwd(dh, _rows(W[f"pw2_{l}"]), sv["y"], conv, l, f"b_pw2_{l}")
        g2, S["b_pw2"][l] = mm_dw(sv["s"], dh, f"w_pw2_{l}", 512, 1, colsum=True)
        du, dtaps = dwconv_glu_bwd(dy, sv["a"], sv["u"], sm["conv"], sm["conv_rev"], l, f"b_conv_{l}")
        S["conv"][l] = st[0:3]
        S["taps"][l] = dtaps[0:CONV_W]
        if l == 0:
            rs.finish("l2", [du])
            nm = _gate(nm, rs.reduce("l1", [du]))
        g1, S["b_pw1"][l] = mm_dw(sv["xn"], du, f"w_pw1_{l}", 512, 4, colsum=True)
        grads[f"pw2_{l}"], grads[f"pw1_{l}"] = _gview(g2), _gview(g1)
        dh, dg = mmT_rmsbwd(du, _slots(W[f"pw1_{l}"]), sv["h"], nm, l, dh, f"b_pw1_{l}")
        S["norm_mix"][l] = dg[0]
        if l == 1:
            sent = [rs.send("l1", grads)]
            rs.finish("l3", [dh])
            nf = _gate(nf, rs.reduce("l2", [dh]))
    S["norm_final"] = st_final[0]
    S["loss"] = st_final[1]
    return grads, dh, S


R_CONV = 37
R_SMALL = 88


def _pack_small(S):
    rows = []
    for l in range(2):
        rows += [S["taps"][l], S["conv"][l][2:3], S["conv"][l][0:2], S["b_pw2"][l], S["b_pw1"][l].reshape(2, D)]
    rows += [jnp.stack(S["norm_mix"]), jnp.stack(S["norm_ffn"]), S["norm_kv"][None], S["norm_final"][None]]
    tail = jnp.concatenate([jnp.stack(S["sinks"]).reshape(-1), S["rel_bias"].reshape(-1)])
    rows += [jnp.pad(tail, (0, D - tail.shape[0]))[None], S["loss"][None]]
    v = jnp.concatenate(rows, axis=0)
    return jnp.pad(v, ((0, R_SMALL - v.shape[0]), (0, 0)))


def kernel(x, norm_mix, norm_ffn, conv_w_pw1, conv_b_pw1, conv_w_dw, conv_b_dw, conv_ln_g, conv_ln_b, conv_w_pw2, conv_b_pw2, norm_kv, w_kv, w_q, w_o, sinks, rel_bias, ffn_w_up, ffn_w_down, norm_final, loss_target, m_norm_mix, m_norm_ffn, m_conv_w_pw1, m_conv_b_pw1, m_conv_w_dw, m_conv_b_dw, m_conv_ln_g, m_conv_ln_b, m_conv_w_pw2, m_conv_b_pw2, m_norm_kv, m_w_kv, m_w_q, m_w_o, m_sinks, m_rel_bias, m_ffn_w_up, m_ffn_w_down, m_norm_final, v_norm_mix, v_norm_ffn, v_conv_w_pw1, v_conv_b_pw1, v_conv_w_dw, v_conv_b_dw, v_conv_ln_g, v_conv_ln_b, v_conv_w_pw2, v_conv_b_pw2, v_norm_kv, v_w_kv, v_w_q, v_w_o, v_sinks, v_rel_bias, v_ffn_w_up, v_ffn_w_down, v_norm_final):
    me = 2 * lax.axis_index("x") + lax.axis_index("y")
    weights = dict(norm_mix=norm_mix, norm_ffn=norm_ffn, conv_w_pw1=conv_w_pw1, conv_b_pw1=conv_b_pw1,
                   conv_w_dw=conv_w_dw, conv_b_dw=conv_b_dw, conv_ln_g=conv_ln_g, conv_ln_b=conv_ln_b,
                   conv_w_pw2=conv_w_pw2, conv_b_pw2=conv_b_pw2, norm_kv=norm_kv, w_kv=w_kv, w_q=w_q, w_o=w_o,
                   sinks=sinks, rel_bias=rel_bias, ffn_w_up=ffn_w_up, ffn_w_down=ffn_w_down, norm_final=norm_final)
    mom_m = dict(norm_mix=m_norm_mix, norm_ffn=m_norm_ffn, conv_w_pw1=m_conv_w_pw1, conv_b_pw1=m_conv_b_pw1,
                 conv_w_dw=m_conv_w_dw, conv_b_dw=m_conv_b_dw, conv_ln_g=m_conv_ln_g, conv_ln_b=m_conv_ln_b,
                 conv_w_pw2=m_conv_w_pw2, conv_b_pw2=m_conv_b_pw2, norm_kv=m_norm_kv, w_kv=m_w_kv, w_q=m_w_q,
                 w_o=m_w_o, sinks=m_sinks, rel_bias=m_rel_bias, ffn_w_up=m_ffn_w_up, ffn_w_down=m_ffn_w_down,
                 norm_final=m_norm_final)
    mom_v = dict(norm_mix=v_norm_mix, norm_ffn=v_norm_ffn, conv_w_pw1=v_conv_w_pw1, conv_b_pw1=v_conv_b_pw1,
                 conv_w_dw=v_conv_w_dw, conv_b_dw=v_conv_b_dw, conv_ln_g=v_conv_ln_g, conv_ln_b=v_conv_ln_b,
                 conv_w_pw2=v_conv_w_pw2, conv_b_pw2=v_conv_b_pw2, norm_kv=v_norm_kv, w_kv=v_w_kv, w_q=v_w_q,
                 w_o=v_w_o, sinks=v_sinks, rel_bias=v_rel_bias, ffn_w_up=v_ffn_w_up, ffn_w_down=v_ffn_w_down,
                 norm_final=v_norm_final)

    big = {"conv_w_pw1": "pw1", "conv_w_pw2": "pw2", "w_q": "wq", "w_o": "wo", "ffn_w_up": "up",
           "ffn_w_down": "down", "w_kv": "kv"}
    of_kind = {k: n for n, k in big.items()}

    def source(name):
        if name == "small":
            return jnp.concatenate(
                [conv_w_dw, conv_b_dw[:, None], conv_ln_g[:, None], conv_ln_b[:, None], conv_b_pw2[:, None],
                 conv_b_pw1.reshape(2, 2, 256), jnp.zeros((2, 3, 256), F32)], axis=1), None
        kind, _, l = name.partition("_")
        return weights[of_kind[kind]], (int(l) if l else None)

    ag = WeightGather(source, AG_GROUPS)
    rs = GradReduce({"pw1": (2, 512, 512), "pw2": (2, 128, D), "wq": (2, 128, D), "wo": (2, 128, D),
                     "up": (4, 512, DFF // 2), "down": (4, DFF // 8, D), "kv": (1, 128, 512)})

    P = dict(norm_mix=norm_mix[:, None], norm_ffn=norm_ffn[:, None], norm_kv=norm_kv[None, None],
             norm_final=norm_final[None], sinks=sinks, rel_bias=rel_bias)
    last, grad_x, S = run_step(x[0], loss_target[0], P, ag, rs)

    rs.finish("l1", [grad_x])
    small_flight, token = small_allreduce_start(_gate(_pack_small(S), rs.reduce("f0", [grad_x])), [])
    token = rs.send("c0", last, after=[token])
    delta, new_m, new_v, big_grads = {}, {}, {}, {}

    def update(n):
        shp = weights[n].shape
        r2 = (int(np.prod(shp[:-1])), shp[-1])
        g, d, nm, nv = adamw(weights[n].reshape(r2), rs.J[big[n]].reshape(r2), mom_m[n].reshape(r2),
                             mom_v[n].reshape(r2), f"adamw_{n}", copy_g=True)
        big_grads[n], delta[n], new_m[n], new_v[n] = g.reshape(shp), d.reshape(shp), nm.reshape(shp), nv.reshape(shp)

    rs.finish("f0", [token])
    for n in ("ffn_w_up", "ffn_w_down"):
        update(n)
    vsum = sum8(xchg_wait(small_flight, [delta["ffn_w_up"], delta["ffn_w_down"]])[1], "small_sum")

    col = lambda a: lax.dynamic_slice_in_dim(a, me * 256, 256, axis=-1)
    grads = {}
    for l in range(2):
        base = l * R_CONV
        grads.setdefault("conv_w_dw", []).append(col(vsum[base:base + 31]))
        grads.setdefault("conv_b_dw", []).append(col(vsum[base + 31]))
        grads.setdefault("conv_ln_g", []).append(col(vsum[base + 32]))
        grads.setdefault("conv_ln_b", []).append(col(vsum[base + 33]))
        grads.setdefault("conv_b_pw2", []).append(col(vsum[base + 34]))
        grads.setdefault("conv_b_pw1", []).append(
            lax.dynamic_slice_in_dim(vsum[base + 35:base + 37].reshape(2 * D), me * 512, 512, axis=0))
    grads = {k: jnp.stack(v) for k, v in grads.items()}
    base = 2 * R_CONV
    grads["norm_mix"] = vsum[base:base + 4]
    grads["norm_ffn"] = vsum[base + 4:base + 8]
    grads["norm_kv"] = vsum[base + 8]
    grads["norm_final"] = vsum[base + 9]
    grads["sinks"] = vsum[base + 10, 0:32].reshape(2, 16)
    grads["rel_bias"] = vsum[base + 10, 32:32 + 512].reshape(32, 16)
    loss = vsum[base + 11, 0]

    for n in weights:
        if n not in big:
            shp = weights[n].shape
            r2 = (int(np.prod(shp[:-1])), shp[-1])
            d, nm, nv = adamw(weights[n].reshape(r2), grads[n].reshape(r2), mom_m[n].reshape(r2),
                              mom_v[n].reshape(r2), f"adamw_{n}")
            delta[n], new_m[n], new_v[n] = d.reshape(shp), nm.reshape(shp), nv.reshape(shp)

    rs.reduce("c0", [vsum])
    for n in ("w_q", "w_o", "w_kv"):
        update(n)
    rs.finish("c0", [delta["w_kv"]])
    for n in ("conv_w_pw1", "conv_w_pw2"):
        update(n)
    grads.update(big_grads)

    order = list(weights)
    return (loss, grad_x[None], *[grads[n] for n in order], *[delta[n] for n in order],
            *[new_m[n] for n in order], *[new_v[n] for n in order])
```

```python
import functools
import math

import numpy as np
import jax
import jax.numpy as jnp
from jax import lax
from jax.experimental import pallas as pl
from jax.experimental.pallas import tpu as pltpu

F32 = jnp.float32
BF16 = jnp.bfloat16
MESH = pl.DeviceIdType.MESH

D = 1024
DFF = 2816
N_HEADS = 16
N_KV = 4
GROUP = 4
HD = 64
BLK = 128
CONV_W = 31
HALO = 32
N_BUCKETS = 32
MAX_DISTANCE = 128
EPS = 1e-6
NEG_INF = -1e30
TM = 512
TCV = 256
TCB = 512
VMEM_LIMIT = 56 * 2 ** 20

ADAM_LR, ADAM_B1, ADAM_B2, ADAM_EPS, ADAM_WD, ADAM_STEP = 0.001, 0.9, 0.999, 1e-08, 0.01, 10


def _cp(*sem):
    return pltpu.CompilerParams(dimension_semantics=sem, vmem_limit_bytes=VMEM_LIMIT)


def _sigmoid(x):
    return 1.0 / (1.0 + jnp.exp(-x))


def _row(tm, n):
    return pl.BlockSpec((tm, n), lambda i: (i, 0))


def _const(shape):
    nd = len(shape)
    return pl.BlockSpec(shape, lambda i: (0,) * nd)


def _weight(shape):
    nd = len(shape)
    return pl.BlockSpec(shape, lambda i: (0,) * nd, pipeline_mode=pl.Buffered(1))


def _layer(shape, l):
    nd = len(shape)
    return pl.BlockSpec((None,) + tuple(shape), lambda i: (l,) + (0,) * nd)


def _dot(a, b):
    return jnp.dot(a, b, preferred_element_type=F32)


def _dot_nt(a, b):
    return lax.dot_general(a, b, (((1,), (1,)), ((), ())), preferred_element_type=F32)


def _dot_tn(a, b):
    return lax.dot_general(a, b, (((0,), (0,)), ((), ())), preferred_element_type=F32)


def _rms(x):
    return lax.rsqrt(jnp.mean(x * x, axis=-1, keepdims=True) + EPS)


def norm_mm_glu(h, g, l, w, b, name):
    T = h.shape[0]
    ns = w.shape[-1]

    def body(h_ref, g_ref, w_ref, b_ref, xn_ref, u_ref, a_ref):
        x = h_ref[...]
        xn = (x * _rms(x) * g_ref[...]).astype(BF16)
        xn_ref[...] = xn
        for s in range(2):
            lo, hi = s * ns, (s + 1) * ns
            u1 = _dot(xn, w_ref[s]) + b_ref[:, lo:hi]
            u2 = _dot(xn, w_ref[2 + s]) + b_ref[:, D + lo:D + hi]
            u_ref[:, lo:hi] = u1.astype(BF16)
            u_ref[:, D + lo:D + hi] = u2.astype(BF16)
            a_ref[:, lo:hi] = (u1 * _sigmoid(u2)).astype(BF16)

    return pl.pallas_call(
        body, name=name, grid=(T // TM,),
        in_specs=[_row(TM, D), _layer((1, D), l), _weight((4, D, ns)), _layer((1, 2 * D), l)],
        out_specs=[_row(TM, D), _row(TM, 2 * D), _row(TM, D)],
        out_shape=[jax.ShapeDtypeStruct((T, D), BF16), jax.ShapeDtypeStruct((T, 2 * D), BF16),
                   jax.ShapeDtypeStruct((T, D), BF16)],
        compiler_params=_cp("parallel"),
    )(h, g, w, b)


SUB = 8


def _make_shifts(sh):
    n = sh.shape[1] - SUB
    for r in range(1, SUB):
        for r0 in range(0, n, 40):
            rows = min(40, n - r0)
            sh[r, r0:r0 + rows, :] = sh[0, pl.ds(r + r0, rows), :]


def _shifted(sh, off, rows, cols):
    return sh[off % SUB, pl.ds(off - off % SUB, rows), cols]


def _conv_taps(sh, w_ref, out_ref, first):
    RB, LB = 32, 512
    for r0 in range(0, out_ref.shape[0], RB):
        for c0 in range(0, out_ref.shape[1], LB):
            acc = jnp.zeros((RB, LB), F32)
            for k in range(CONV_W):
                acc = acc + w_ref[k:k + 1, c0:c0 + LB] * _shifted(sh, first + k + r0, RB, slice(c0, c0 + LB))
            out_ref[r0:r0 + RB, c0:c0 + LB] = acc


def dwconv_ln_silu(a, sm, l, name):
    T = a.shape[0]
    nb = TCV // HALO

    def body(cur_ref, prev_ref, sm_ref, y_ref, s_ref, sh, yb):
        i = pl.program_id(0)
        sh[0, 0:HALO, :] = jnp.where(i > 0, prev_ref[...].astype(F32), 0.0)
        sh[0, HALO:HALO + TCV, :] = cur_ref[...].astype(F32)
        _make_shifts(sh)
        _conv_taps(sh, sm_ref, yb, HALO - (CONV_W - 1))
        y = yb[...] + sm_ref[31:32, :]
        y_ref[...] = y.astype(BF16)
        mu = jnp.mean(y, axis=-1, keepdims=True)
        yc = y - mu
        rstd = lax.rsqrt(jnp.mean(yc * yc, axis=-1, keepdims=True) + EPS)
        z = yc * rstd * sm_ref[32:33, :] + sm_ref[33:34, :]
        s_ref[...] = (z * _sigmoid(z)).astype(BF16)

    return pl.pallas_call(
        body, name=name, grid=(T // TCV,),
        in_specs=[_row(TCV, D), pl.BlockSpec((HALO, D), lambda i: (jnp.maximum(i * nb - 1, 0), 0)),
                  _layer((40, D), l)],
        out_specs=[_row(TCV, D), _row(TCV, D)],
        out_shape=[jax.ShapeDtypeStruct((T, D), BF16), jax.ShapeDtypeStruct((T, D), BF16)],
        scratch_shapes=[pltpu.VMEM((SUB, TCV + HALO, D), F32), pltpu.VMEM((TCV, D), F32)],
        compiler_params=_cp("parallel"),
    )(a, a, sm)


def mm_bias_res(xb, w, b, bl, res, name):
    T, K = xb.shape

    def body(x_ref, w_ref, b_ref, r_ref, o_ref):
        o_ref[...] = _dot(x_ref[...], w_ref[...]) + b_ref[...] + r_ref[...]

    return pl.pallas_call(
        body, name=name, grid=(T // TM,),
        in_specs=[_row(TM, K), _weight((K, D)), _layer((1, D), bl), _row(TM, D)],
        out_specs=_row(TM, D), out_shape=jax.ShapeDtypeStruct((T, D), F32),
        compiler_params=_cp("parallel"),
    )(xb, w, b, res)


def norm_mm_swiglu(h, g, l, w, name):
    T = h.shape[0]
    ns = w.shape[-1]

    def body(h_ref, g_ref, w_ref, xn_ref, gu_ref, f_ref):
        x = h_ref[...]
        xn = (x * _rms(x) * g_ref[...]).astype(BF16)
        xn_ref[...] = xn
        for s in range(2):
            lo, hi = s * ns, (s + 1) * ns
            gate = _dot(xn, w_ref[s])
            up = _dot(xn, w_ref[2 + s])
            gu_ref[:, lo:hi] = gate.astype(BF16)
            gu_ref[:, DFF + lo:DFF + hi] = up.astype(BF16)
            f_ref[:, lo:hi] = (gate * _sigmoid(gate) * up).astype(BF16)

    return pl.pallas_call(
        body, name=name, grid=(T // TM,),
        in_specs=[_row(TM, D), _layer((1, D), l), _weight((4, D, ns))],
        out_specs=[_row(TM, D), _row(TM, 2 * DFF), _row(TM, DFF)],
        out_shape=[jax.ShapeDtypeStruct((T, D), BF16), jax.ShapeDtypeStruct((T, 2 * DFF), BF16),
                   jax.ShapeDtypeStruct((T, DFF), BF16)],
        compiler_params=_cp("parallel"),
    )(h, g, w)


def norm_mm(h, g, gl, w, name, scale=1.0):
    T = h.shape[0]
    N = w.shape[-1]

    def body(h_ref, g_ref, w_ref, xn_ref, o_ref):
        x = h_ref[...]
        xn = (x * _rms(x) * g_ref[...]).astype(BF16)
        xn_ref[...] = xn
        o_ref[...] = (_dot(xn, w_ref[...]) * scale).astype(BF16)

    return pl.pallas_call(
        body, name=name, grid=(T // TM,),
        in_specs=[_row(TM, D), _layer((1, D), gl), _weight((D, N))],
        out_specs=[_row(TM, D), _row(TM, N)],
        out_shape=[jax.ShapeDtypeStruct((T, D), BF16), jax.ShapeDtypeStruct((T, N), BF16)],
        compiler_params=_cp("parallel"),
    )(h, g, w)


QB = 16
QW = GROUP * BLK


def band_mask():
    qi = np.arange(QW)[None, :] % BLK
    kj = np.arange(2 * BLK)[:, None]
    band = ((kj < BLK) & (kj > qi)) | ((kj >= BLK) & (kj - BLK <= qi))
    first = band & (kj >= BLK)
    return np.where(np.stack([first, band]), 0.0, NEG_INF).astype(np.float32)


def _softmax_cols(s, sink):
    m = jnp.maximum(jnp.max(s, axis=0, keepdims=True), sink)
    p = jnp.exp(s - m)
    es = jnp.exp(sink - m)
    inv = 1.0 / (jnp.sum(p, axis=0, keepdims=True) + es)
    return p, inv, es


def _attn_specs(T):
    W = QB * BLK
    qspec = pl.BlockSpec((None, GROUP, HD, W), lambda kv, n: (kv, 0, 0, n))
    kspec = pl.BlockSpec((None, T + BLK, HD), lambda kv, n: (kv, 0, 0))
    ktspec = [pl.BlockSpec((None, HD, W), lambda kv, n: (kv, 0, n)),
              pl.BlockSpec((None, HD, BLK), lambda kv, n: (kv, 0, (n + 1) * QB))]
    bspec = pl.BlockSpec((2, None, 2 * BLK, QW), lambda kv, n: (0, kv, 0, 0))
    sspec = pl.BlockSpec((None, 1, QW), lambda kv, n: (kv, 0, 0))
    return qspec, kspec, ktspec, bspec, sspec


def _attn_block(n, b):
    blk = n * QB + b
    rows = pl.ds(pl.multiple_of(blk * BLK, BLK), 2 * BLK)
    return rows, (jnp.minimum(blk, 1) if b == 0 else 1)


def _band_cols(main_ref, tail_ref, b):
    if b < QB - 1:
        return main_ref[:, b * BLK:(b + 2) * BLK]
    return jnp.concatenate([main_ref[:, b * BLK:], tail_ref[...]], axis=1)


def _heads_side_by_side(ref, qs):
    return jnp.concatenate([ref[g, :, qs] for g in range(GROUP)], axis=1)


def attn_fwd(q, kp, vt, bias, sink, name):
    T = q.shape[3]
    qspec, kspec, ktspec, bspec, sspec = _attn_specs(T)

    def body(q_ref, k_ref, vt_ref, vtt_ref, b_ref, s_ref, o_ref, pb):
        n = pl.program_id(1)

        def scores(b):
            return _dot(k_ref[_attn_block(n, b)[0], :], _heads_side_by_side(q_ref, slice(b * BLK, (b + 1) * BLK)))

        st_next = scores(0)
        for b in range(QB):
            rows, table = _attn_block(n, b)
            qs = slice(b * BLK, (b + 1) * BLK)
            st = st_next
            if b + 1 < QB:
                st_next = scores(b + 1)
            for g in range(GROUP):
                hs = slice(g * BLK, (g + 1) * BLK)
                p, inv, _ = _softmax_cols(st[:, hs] + b_ref[table, :, hs], s_ref[:, hs])
                pb[:, hs] = (p * inv).astype(BF16)
            ot = _dot(_band_cols(vt_ref, vtt_ref, b), pb[...])
            for g in range(GROUP):
                o_ref[g, :, qs] = ot[:, g * BLK:(g + 1) * BLK].astype(BF16)

    return pl.pallas_call(
        body, name=name, grid=(N_KV, T // (QB * BLK)),
        in_specs=[qspec, kspec, *ktspec, bspec, sspec], out_specs=qspec,
        out_shape=jax.ShapeDtypeStruct((N_KV, GROUP, HD, T), BF16),
        scratch_shapes=[pltpu.VMEM((2 * BLK, QW), BF16)],
        compiler_params=_cp("parallel", "parallel"),
    )(q, kp, vt, vt, bias, sink)


def attn_bwd(q, kp, kt, vp, bias, sink, o, do, name):
    T = q.shape[3]
    qspec, kspec, ktspec, bspec, sspec = _attn_specs(T)

    def body(q_ref, k_ref, kt_ref, ktt_ref, v_ref, b_ref, s_ref, o_ref, do_ref,
             dq_ref, dk_ref, dv_ref, db_ref, ds_ref, pb, dsb):
        n = pl.program_id(1)

        @pl.when(n == 0)
        def _():
            dk_ref[...] = jnp.zeros_like(dk_ref)
            dv_ref[...] = jnp.zeros_like(dv_ref)
            db_ref[...] = jnp.zeros_like(db_ref)
            ds_ref[...] = jnp.zeros_like(ds_ref)

        def products(b):
            rows = _attn_block(n, b)[0]
            qs = slice(b * BLK, (b + 1) * BLK)
            q4, do4 = _heads_side_by_side(q_ref, qs), _heads_side_by_side(do_ref, qs)
            return q4, do4, _dot(k_ref[rows, :], q4), _dot(v_ref[rows, :], do4)

        ahead = products(0)
        for b in range(QB):
            rows, table = _attn_block(n, b)
            qs = slice(b * BLK, (b + 1) * BLK)
            q4, do4, st, dpt = ahead
            if b + 1 < QB:
                ahead = products(b + 1)
            for g in range(GROUP):
                hs = slice(g * BLK, (g + 1) * BLK)
                p, inv, es = _softmax_cols(st[:, hs] + b_ref[table, :, hs], s_ref[:, hs])
                probs = p * inv
                delta = jnp.sum(do_ref[g, :, qs].astype(F32) * o_ref[g, :, qs].astype(F32), axis=0, keepdims=True)
                dS = probs * (dpt[:, hs] - delta)
                ds_ref[:, hs] += -(es * inv) * delta
                db_ref[:, hs] += dS
                pb[:, hs] = probs.astype(BF16)
                dsb[:, hs] = dS.astype(BF16)
            dqt = _dot(_band_cols(kt_ref, ktt_ref, b), dsb[...]) * (HD ** -0.5)
            for g in range(GROUP):
                dq_ref[g, :, qs] = dqt[:, g * BLK:(g + 1) * BLK].astype(BF16)
            dk_ref[rows, :] += _dot_nt(dsb[...], q4)
            dv_ref[rows, :] += _dot_nt(pb[...], do4)

    kout = pl.BlockSpec((None, T + BLK, HD), lambda kv, n: (kv, 0, 0))
    dbspec = pl.BlockSpec((None, 2 * BLK, QW), lambda kv, n: (kv, 0, 0))
    return pl.pallas_call(
        body, name=name, grid=(N_KV, T // (QB * BLK)),
        in_specs=[qspec, kspec, *ktspec, kspec, bspec, sspec, qspec, qspec],
        out_specs=[qspec, kout, kout, dbspec, sspec],
        out_shape=[jax.ShapeDtypeStruct((N_KV, GROUP, HD, T), BF16),
                   jax.ShapeDtypeStruct((N_KV, T + BLK, HD), F32), jax.ShapeDtypeStruct((N_KV, T + BLK, HD), F32),
                   jax.ShapeDtypeStruct((N_KV, 2 * BLK, QW), F32), jax.ShapeDtypeStruct((N_KV, 1, QW), F32)],
        scratch_shapes=[pltpu.VMEM((2 * BLK, QW), BF16), pltpu.VMEM((2 * BLK, QW), BF16)],
        compiler_params=_cp("parallel", "arbitrary"),
    )(q, kp, kt, kt, vp, bias, sink, o, do)


def final_loss(h, g, target, name):
    T = h.shape[0]

    def body(h_ref, g_ref, t_ref, dh_ref, st_ref):
        i = pl.program_id(0)

        @pl.when(i == 0)
        def _():
            st_ref[...] = jnp.zeros_like(st_ref)

        x = h_ref[...]
        r = _rms(x)
        xh = x * r
        e = xh * g_ref[...] - t_ref[...]
        loss = 0.5 * jnp.sum(jnp.mean(e * e, axis=-1, keepdims=True))
        dy = e * (1.0 / D)
        st_ref[0:1, :] += jnp.sum(dy * xh, axis=0, keepdims=True)
        lane = lax.broadcasted_iota(jnp.int32, (1, D), 1)
        st_ref[1:2, :] += jnp.where(lane == 0, loss, 0.0)
        dxh = dy * g_ref[...]
        dh_ref[...] = r * (dxh - xh * jnp.mean(dxh * xh, axis=-1, keepdims=True))

    return pl.pallas_call(
        body, name=name, grid=(T // TM,),
        in_specs=[_row(TM, D), _const((1, D)), _row(TM, D)],
        out_specs=[_row(TM, D), _const((8, D))],
        out_shape=[jax.ShapeDtypeStruct((T, D), F32), jax.ShapeDtypeStruct((8, D), F32)],
        compiler_params=_cp("arbitrary"),
    )(h, g, target)


def mm_dw(x, dy, name, tn, slots, colsum=False):
    T, K = x.shape
    split = dy.ndim == 3
    N = dy.shape[-1] * (2 if split else 1)
    tt = min(T, 2048 if K <= 1024 else 1024)
    nt = T // tt
    ns = N // slots
    per = ns // tn

    def body(x_ref, dy_ref, *rest):
        if colsum:
            dw_ref, cs_ref, acc, cacc = rest
        else:
            dw_ref, acc = rest
        t = pl.program_id(1)

        @pl.when(t == 0)
        def _():
            acc[...] = jnp.zeros_like(acc)
            if colsum:
                cacc[...] = jnp.zeros_like(cacc)

        dyv = dy_ref[...]
        acc[...] += _dot_tn(x_ref[...].astype(BF16), dyv.astype(BF16))
        if colsum:
            cacc[...] += jnp.sum(dyv.astype(F32), axis=0, keepdims=True)

        @pl.when(t == nt - 1)
        def _():
            dw_ref[...] = acc[...].astype(BF16)
            if colsum:
                cs_ref[...] = cacc[...]

    if split:
        half = N // 2 // tn
        dy_spec = pl.BlockSpec((None, tt, tn), lambda j, t: (j // half, t, j % half))
    else:
        dy_spec = pl.BlockSpec((tt, tn), lambda j, t: (t, j))
    out_specs = [pl.BlockSpec((None, K, tn), lambda j, t: (j // per, 0, j % per))]
    out_shape = [jax.ShapeDtypeStruct((slots, K, ns), BF16)]
    scratch = [pltpu.VMEM((K, tn), F32)]
    if colsum:
        out_specs.append(pl.BlockSpec((1, tn), lambda j, t: (0, j)))
        out_shape.append(jax.ShapeDtypeStruct((1, N), F32))
        scratch.append(pltpu.VMEM((1, tn), F32))
    res = pl.pallas_call(
        body, name=name, grid=(N // tn, nt),
        in_specs=[pl.BlockSpec((tt, K), lambda j, t: (t, 0)), dy_spec],
        out_specs=out_specs, out_shape=out_shape, scratch_shapes=scratch,
        compiler_params=_cp("parallel", "arbitrary"),
    )(x, dy)
    return tuple(res) if colsum else res[0]


def mmT_swiglu_bwd(dh, w, gu, name, after=()):
    T = dh.shape[0]
    cw = 256

    def body(dh_ref, w_ref, gu_ref, *rest):
        du_ref = rest[-1]
        dhb = dh_ref[...].astype(BF16)
        ahead = _dot_nt(dhb, w_ref[0:cw, :])
        for lo in range(0, DFF, cw):
            hi = lo + cw
            df = ahead
            if hi < DFF:
                ahead = _dot_nt(dhb, w_ref[hi:hi + cw, :])
            gate = gu_ref[:, lo:hi].astype(F32)
            up = gu_ref[:, DFF + lo:DFF + hi].astype(F32)
            sg = _sigmoid(gate)
            silu = gate * sg
            du_ref[:, lo:hi] = (df * (up * (sg + silu * (1.0 - sg)))).astype(BF16)
            du_ref[:, DFF + lo:DFF + hi] = (df * silu).astype(BF16)

    return pl.pallas_call(
        body, name=name, grid=(T // TM,),
        in_specs=[_row(TM, D), _weight((DFF, D)), _row(TM, 2 * DFF)] + [ANY] * len(after),
        out_specs=_row(TM, 2 * DFF), out_shape=jax.ShapeDtypeStruct((T, 2 * DFF), BF16),
        compiler_params=_cp("parallel"),
    )(dh, w, gu, *after)


def mmT_rmsbwd(du, w, h, g, gl, dh_in, name):
    split = du.ndim == 3
    T = du.shape[-2]
    N = du.shape[-1] * (2 if split else 1)
    slots = w.shape[0]
    ns = N // slots

    RH = TM // 2

    def piece(du_ref, s, rows):
        if split:
            per = slots // 2
            return du_ref[s // per, rows, (s % per) * ns:(s % per + 1) * ns]
        return du_ref[rows, s * ns:(s + 1) * ns]

    def body(du_ref, w_ref, h_ref, g_ref, di_ref, dh_ref, dg_ref):
        i = pl.program_id(0)

        @pl.when(i == 0)
        def _():
            dg_ref[...] = jnp.zeros_like(dg_ref)

        def products(k):
            rows = slice(k * RH, (k + 1) * RH)
            dxn = _dot_nt(piece(du_ref, 0, rows), w_ref[0])
            for s in range(1, slots):
                dxn = dxn + _dot_nt(piece(du_ref, s, rows), w_ref[s])
            return dxn

        ahead = products(0)
        for k in range(TM // RH):
            rows = slice(k * RH, (k + 1) * RH)
            dxn = ahead
            if (k + 1) * RH < TM:
                ahead = products(k + 1)
            x = h_ref[rows, :]
            r = _rms(x)
            xh = x * r
            dg_ref[0:1, :] += jnp.sum(dxn * xh, axis=0, keepdims=True)
            dxh = dxn * g_ref[...]
            dh_ref[rows, :] = di_ref[rows, :] + r * (dxh - xh * jnp.mean(dxh * xh, axis=-1, keepdims=True))

    return pl.pallas_call(
        body, name=name, grid=(T // TM,),
        in_specs=[pl.BlockSpec((2, TM, N // 2), lambda i: (0, i, 0)) if split else _row(TM, N),
                  _weight((slots, D, ns)), _row(TM, D), _layer((1, D), gl), _row(TM, D)],
        out_specs=[_row(TM, D), _const((8, D))],
        out_shape=[jax.ShapeDtypeStruct((T, D), F32), jax.ShapeDtypeStruct((8, D), F32)],
        compiler_params=_cp("arbitrary"),
    )(du, w, h, g, dh_in)


def mmT(dh, w, name):
    T = dh.shape[0]
    N = w.shape[0]

    def body(dh_ref, w_ref, o_ref):
        o_ref[...] = _dot_nt(dh_ref[...].astype(BF16), w_ref[...]).astype(BF16)

    return pl.pallas_call(
        body, name=name, grid=(T // TM,),
        in_specs=[_row(TM, D), _weight((N, D))],
        out_specs=_row(TM, N), out_shape=jax.ShapeDtypeStruct((T, N), BF16),
        compiler_params=_cp("parallel"),
    )(dh, w)


def mmT_lnbwd(dh, w, y, sm, l, name):
    T = dh.shape[0]

    def body(dh_ref, w_ref, y_ref, sm_ref, dy_ref, st_ref):
        i = pl.program_id(0)

        @pl.when(i == 0)
        def _():
            st_ref[...] = jnp.zeros_like(st_ref)

        ds = _dot_nt(dh_ref[...].astype(BF16), w_ref[...])
        y = y_ref[...].astype(F32)
        mu = jnp.mean(y, axis=-1, keepdims=True)
        yc = y - mu
        rstd = lax.rsqrt(jnp.mean(yc * yc, axis=-1, keepdims=True) + EPS)
        xh = yc * rstd
        gam = sm_ref[32:33, :]
        z = xh * gam + sm_ref[33:34, :]
        sg = _sigmoid(z)
        dz = ds * sg * (1.0 + z * (1.0 - sg))
        st_ref[0:1, :] += jnp.sum(dz * xh, axis=0, keepdims=True)
        st_ref[1:2, :] += jnp.sum(dz, axis=0, keepdims=True)
        dxh = dz * gam
        dy = rstd * (dxh - jnp.mean(dxh, axis=-1, keepdims=True) - xh * jnp.mean(dxh * xh, axis=-1, keepdims=True))
        st_ref[2:3, :] += jnp.sum(dy, axis=0, keepdims=True)
        dy_ref[...] = dy.astype(BF16)

    return pl.pallas_call(
        body, name=name, grid=(T // TM,),
        in_specs=[_row(TM, D), _weight((D, D)), _row(TM, D), _layer((40, D), l)],
        out_specs=[_row(TM, D), _const((8, D))],
        out_shape=[jax.ShapeDtypeStruct((T, D), BF16), jax.ShapeDtypeStruct((8, D), F32)],
        compiler_params=_cp("arbitrary"),
    )(dh, w, y, sm)


CH = 512


def dwconv_glu_bwd(dy, a, u, sm, smrev, l, name):
    T = dy.shape[0]
    nr, nc = T // TCB, D // CH
    nb = TCB // HALO
    last = T // HALO - 1

    def body(dy_ref, dyn_ref, a_ref, ap_ref, u1_ref, u2_ref, sm_ref, rev_ref, du_ref, dw_ref, shd, sha, da):
        i = pl.program_id(0)
        r = i % nr

        @pl.when(r == 0)
        def _():
            dw_ref[...] = jnp.zeros_like(dw_ref)

        shd[0, 0:TCB, :] = dy_ref[...].astype(F32)
        shd[0, TCB:TCB + HALO, :] = jnp.where(r < nr - 1, dyn_ref[...].astype(F32), 0.0)
        sha[0, 0:HALO, :] = jnp.where(r > 0, ap_ref[...].astype(F32), 0.0)
        sha[0, HALO:HALO + TCB, :] = a_ref[...].astype(F32)
        _make_shifts(shd)
        _make_shifts(sha)
        _conv_taps(shd, rev_ref, da, 0)
        for kg in range(0, CONV_W, SUB):
            taps = range(kg, min(kg + SUB, CONV_W))
            part = [jnp.zeros((SUB, CH), F32) for _ in taps]
            for r0 in range(0, TCB, SUB):
                d = shd[0, r0:r0 + SUB, :]
                for j, k in enumerate(taps):
                    part[j] = part[j] + d * _shifted(sha, HALO - (CONV_W - 1) + k + r0, SUB, slice(None))
            for j, k in enumerate(taps):
                dw_ref[k:k + 1, :] += jnp.sum(part[j], axis=0, keepdims=True)
        dav = da[...]
        u1 = u1_ref[...].astype(F32)
        sg = _sigmoid(u2_ref[...].astype(F32))
        du_ref[0] = (dav * sg).astype(BF16)
        du_ref[1] = (dav * u1 * sg * (1.0 - sg)).astype(BF16)

    tile = lambda i: (i % nr, i // nr)
    in_specs = [pl.BlockSpec((TCB, CH), tile),
                pl.BlockSpec((HALO, CH), lambda i: (jnp.minimum((i % nr + 1) * nb, last), i // nr)),
                pl.BlockSpec((TCB, CH), tile),
                pl.BlockSpec((HALO, CH), lambda i: (jnp.maximum((i % nr) * nb - 1, 0), i // nr)),
                pl.BlockSpec((TCB, CH), tile), pl.BlockSpec((TCB, CH), lambda i: (i % nr, nc + i // nr)),
                pl.BlockSpec((None, 40, CH), lambda i: (l, 0, i // nr)),
                pl.BlockSpec((None, 40, CH), lambda i: (l, 0, i // nr))]
    return pl.pallas_call(
        body, name=name, grid=(nr * nc,), in_specs=in_specs,
        out_specs=[pl.BlockSpec((2, TCB, CH), lambda i: (0, i % nr, i // nr)),
                   pl.BlockSpec((32, CH), lambda i: (0, i // nr))],
        out_shape=[jax.ShapeDtypeStruct((2, T, D), BF16), jax.ShapeDtypeStruct((32, D), F32)],
        scratch_shapes=[pltpu.VMEM((SUB, TCB + HALO, CH), F32), pltpu.VMEM((SUB, TCB + HALO, CH), F32),
                        pltpu.VMEM((TCB, CH), F32)],
        compiler_params=_cp("arbitrary"),
    )(dy, dy, a, a, u, u, sm, smrev)


def _rows_tile(R):
    for t in (512, 256, 128, 64, 32, 16, 8):
        if R % t == 0:
            return t
    return R


def add8_into(J, l, g, others, where, name):
    R, C = g.shape[2:]
    tr = R // 2

    def body(w_ref, g_ref, x_ref, j_in, j_ref):
        acc = g_ref[...].astype(F32)
        for k in range(7):
            acc = acc + x_ref[k].astype(F32)
        j_ref[...] = acc

    return pl.pallas_call(
        body, name=name,
        grid_spec=pltpu.PrefetchScalarGridSpec(
            num_scalar_prefetch=1, grid=(R // tr,),
            in_specs=[pl.BlockSpec((None, None, tr, C), lambda i, w: (w[0], w[1], i, 0)),
                      pl.BlockSpec((7, tr, C), lambda i, w: (0, i, 0)), ANY],
            out_specs=pl.BlockSpec((None, None, tr, C), lambda i, w: (l, w[1], i, 0))),
        out_shape=jax.ShapeDtypeStruct(J.shape, F32), input_output_aliases={3: 0},
        compiler_params=_cp("parallel"),
    )(where, g, others, J)


def adamw(w, g, m, v, name, copy_g=False):
    R, C = w.shape
    tr = _rows_tile(R)

    def body(w_ref, g_ref, m_ref, v_ref, *outs):
        d_ref, nm_ref, nv_ref = outs[-3:]
        gv = g_ref[...]
        if copy_g:
            outs[0][...] = gv
        nm = ADAM_B1 * m_ref[...] + (1.0 - ADAM_B1) * gv
        nv = ADAM_B2 * v_ref[...] + (1.0 - ADAM_B2) * (gv * gv)
        m_hat = nm / (1.0 - ADAM_B1 ** ADAM_STEP)
        v_hat = nv / (1.0 - ADAM_B2 ** ADAM_STEP)
        d_ref[...] = -ADAM_LR * (m_hat / (jnp.sqrt(v_hat) + ADAM_EPS) + ADAM_WD * w_ref[...])
        nm_ref[...] = nm
        nv_ref[...] = nv

    sd = jax.ShapeDtypeStruct((R, C), F32)
    n_out = 4 if copy_g else 3
    return pl.pallas_call(
        body, name=name, grid=(R // tr,),
        in_specs=[_row(tr, C)] * 4, out_specs=[_row(tr, C)] * n_out, out_shape=[sd] * n_out,
        compiler_params=_cp("parallel"),
    )(w, g, m, v)


ANY = pl.BlockSpec(memory_space=pl.ANY)
HBM = pl.BlockSpec(memory_space=pltpu.HBM)
SEM = pl.BlockSpec(memory_space=pltpu.SEMAPHORE)
EFFECT = pltpu.SideEffectType.DATAFLOW_SIDE_EFFECTING


def _place():
    x, y, c = lax.axis_index("x"), lax.axis_index("y"), lax.axis_index("c")
    chips = [(1 - x, y), (x, 1 - y), (1 - x, 1 - y)]
    return x, y, c, chips


def _copy(src, dst, send, recv, k, to):
    return pltpu.make_async_remote_copy(src_ref=src, dst_ref=dst, send_sem=send.at[k], recv_sem=recv.at[k],
                                        device_id=to, device_id_type=MESH)


def xchg_start(name, bufs, plan, n, after=()):
    nb = len(bufs)

    na = len(after)

    def body(*refs):
        send, recv, token = refs[nb + na], refs[nb + na + 1], refs[-1]
        for k, (src, dst, to) in enumerate(plan(refs[:nb])):
            _copy(src, dst, send, recv, k, to).start()
        token[...] = jnp.zeros_like(token)

    outs = pl.pallas_call(
        body, name=name,
        out_shape=(pltpu.SemaphoreType.DMA((n,)), pltpu.SemaphoreType.DMA((n,)),
                   *[pltpu.HBM(b.shape, b.dtype) for b in bufs], jax.ShapeDtypeStruct((8, 128), F32)),
        in_specs=[HBM] * nb + [ANY] * na,
        out_specs=(SEM, SEM, *[HBM] * nb, pl.BlockSpec(memory_space=pltpu.VMEM)),
        input_output_aliases={i: 2 + i for i in range(nb)},
        compiler_params=pltpu.CompilerParams(has_side_effects=EFFECT),
    )(*[pltpu.with_memory_space_constraint(b, pltpu.HBM) for b in bufs], *after)
    return dict(name=name, send=outs[0], recv=outs[1], bufs=list(outs[2:2 + nb]), plan=plan), outs[-1]


def xchg_wait(flight, after):
    bufs, plan = flight["bufs"], flight["plan"]
    nb = len(bufs)

    def body(*refs):
        send, recv = refs[nb], refs[nb + 1]
        for k, (src, dst, to) in enumerate(plan(refs[:nb])):
            cp = _copy(src, dst, send, recv, k, to)
            cp.wait_send()
            cp.wait_recv()

    outs = pl.pallas_call(
        body, name=flight["name"] + "_wait",
        out_shape=tuple(pltpu.HBM(b.shape, b.dtype) for b in bufs),
        in_specs=[HBM] * nb + [SEM, SEM] + [ANY] * len(after),
        out_specs=tuple([HBM] * nb), input_output_aliases={i: i for i in range(nb)},
        compiler_params=pltpu.CompilerParams(has_side_effects=EFFECT),
    )(*bufs, flight["send"], flight["recv"], *after)
    return list(outs)


def _flip(k, x, y, c):
    return ((1 - x) if k & 4 else x, (1 - y) if k & 2 else y, (1 - c) if k & 1 else c)


def cast_into_slot(srcs, name, after):
    me = (2 * lax.axis_index("x") + lax.axis_index("y")).astype(jnp.int32).reshape(1)
    ns = len(srcs)

    def body(me_ref, *refs):
        outs = refs[ns + len(after):]
        for t in range(ns):
            outs[t][...] = refs[t][...].astype(outs[t].dtype).reshape(outs[t].shape)

    in_specs, out_specs, out_shape = [], [], []
    for arr, l in srcs:
        if l is None:
            in_specs.append(pl.BlockSpec(arr.shape, lambda i, w, nd=arr.ndim: (0,) * nd))
            a2, b, dt = (arr.shape[0] // 2, arr.shape[1], BF16) if arr.ndim == 2 else (arr.shape[1], arr.shape[2], F32)
        else:
            in_specs.append(pl.BlockSpec((None,) + arr.shape[1:], lambda i, w, l=l: (l, 0, 0)))
            a2, b, dt = arr.shape[1] // 2, arr.shape[2], BF16
        out_specs.append(pl.BlockSpec((None, 2, a2, b), lambda i, w: (w[0], 0, 0, 0)))
        out_shape.append(jax.ShapeDtypeStruct((4, 2, a2, b), dt))
    in_specs += [ANY] * len(after)
    return pl.pallas_call(
        body, name=name,
        grid_spec=pltpu.PrefetchScalarGridSpec(num_scalar_prefetch=1, grid=(1,), in_specs=in_specs,
                                               out_specs=out_specs),
        out_shape=out_shape, compiler_params=_cp("arbitrary"),
    )(me, *[arr for arr, _ in srcs], *after)


class WeightGather:
    def __init__(self, source, groups):
        self.names = dict(groups)
        self.ici, self.d2d = {}, {}
        self.token = None
        for gname, names in groups:
            nt = len(names)
            after = [] if self.token is None else [self.token]
            lands = cast_into_slot([source(n) for n in names], f"ag_cast_{gname}", after)

            def plan(refs, nt=nt):
                x, y, c, chips = _place()
                out = []
                for t in range(nt):
                    mine = refs[t].at[2 * x + y, c]
                    out += [(mine, mine, (cx, cy, c)) for cx, cy in chips]
                return out

            self.ici[gname], self.token = xchg_start(f"ag_ici_{gname}", lands, plan, 3 * nt, after=after)

    def forward(self, gname, after):
        nt = len(self.names[gname])
        lands = xchg_wait(self.ici.pop(gname), after)

        def plan(refs):
            x, y, c, chips = _place()
            out = []
            for t in range(nt):
                for cx, cy in chips:
                    piece = refs[t].at[2 * cx + cy, c]
                    out.append((piece, piece, (x, y, 1 - c)))
            return out

        self.d2d[gname], token = xchg_start(f"ag_d2d_{gname}", lands, plan, 3 * nt)
        return token

    def get(self, gname, after):
        lands = xchg_wait(self.d2d.pop(gname), after)
        return dict(zip(self.names[gname], lands))


class GradReduce:
    def __init__(self, kinds):
        self.J = {k: lax.empty((L, 2, a2, b), F32) for k, (L, a2, b) in kinds.items()}
        self.x, self.j = {}, {}

    @staticmethod
    def _where(name):
        kind, _, l = name.partition("_")
        return kind, int(l or 0)

    def send(self, gname, grads, after=()):
        names = list(grads)
        nt = len(names)
        gs = [grads[n] for n in names]
        xs = [lax.empty((7,) + g.shape[2:], g.dtype) for g in gs]

        def plan(refs):
            x, y, c, _ = _place()
            out = []
            for t in range(nt):
                for k in range(1, 8):
                    px, py, pc = _flip(k, x, y, c)
                    out.append((refs[t].at[2 * px + py, pc], refs[nt + t].at[k - 1], (px, py, pc)))
            return out

        flight, token = xchg_start(f"rs_x_{gname}", gs + xs, plan, 7 * nt, after=after)
        self.x[gname] = (names, flight)
        return token

    def reduce(self, gname, after):
        names, flight = self.x.pop(gname)
        nt = len(names)
        bufs = xchg_wait(flight, after)
        mine = jnp.stack([2 * lax.axis_index("x") + lax.axis_index("y"), lax.axis_index("c")]).astype(jnp.int32)
        where = [self._where(n) for n in names]
        js = [add8_into(self.J[kind], l, bufs[t], bufs[nt + t], mine, f"rs_add_{names[t]}")
              for t, (kind, l) in enumerate(where)]

        def plan(refs):
            x, y, c, _ = _place()
            out = []
            for t in range(nt):
                half = refs[t].at[where[t][1], c]
                out.append((half, half, (x, y, 1 - c)))
            return out

        flight, token = xchg_start(f"rs_join_{gname}", js, plan, nt)
        self.j[gname] = (where, flight)
        return token

    def finish(self, gname, after):
        where, flight = self.j.pop(gname)
        for (kind, _), j in zip(where, xchg_wait(flight, after)):
            self.J[kind] = j


def small_allreduce_start(v, after):
    me = 4 * lax.axis_index("x") + 2 * lax.axis_index("y") + lax.axis_index("c")
    land = lax.dynamic_update_slice(lax.empty((8,) + v.shape, v.dtype), v[None], (me, 0, 0))

    def plan(refs):
        x, y, c, _ = _place()
        return [(refs[0], refs[1].at[4 * x + 2 * y + c], _flip(k, x, y, c)) for k in range(1, 8)]

    return xchg_start("small_allreduce", [v, land], plan, 7, after=after)


def sum8(all8, name):
    def body(x_ref, o_ref):
        acc = x_ref[0]
        for d in range(1, 8):
            acc = acc + x_ref[d]
        o_ref[...] = acc

    return pl.pallas_call(
        body, name=name,
        in_specs=[pl.BlockSpec(memory_space=pltpu.VMEM)], out_specs=pl.BlockSpec(memory_space=pltpu.VMEM),
        out_shape=jax.ShapeDtypeStruct(all8.shape[1:], F32),
        compiler_params=pltpu.CompilerParams(vmem_limit_bytes=VMEM_LIMIT),
    )(all8)


AG_GROUPS = (("a0", ("pw1_0", "pw2_0", "small")), ("f0", ("up_0", "down_0")),
             ("l1", ("pw1_1", "pw2_1", "up_1", "down_1")), ("l2", ("kv", "wq_0", "wo_0", "up_2", "down_2")),
             ("l3", ("wq_1", "wo_1", "up_3", "down_3")))


def _bucket_table():
    qi = np.arange(BLK)[:, None]
    kj = np.arange(2 * BLK)[None, :]
    d = np.maximum(qi + BLK - kj, 0)
    max_exact = N_BUCKETS // 2
    log_ratio = (np.log(np.maximum(d, 1).astype(np.float32) / np.float32(max_exact))
                 / np.float32(math.log(MAX_DISTANCE / max_exact))).astype(np.float32)
    large = max_exact + (log_ratio * np.float32(N_BUCKETS - max_exact)).astype(np.int32)
    large = np.minimum(large, N_BUCKETS - 1)
    return np.where(d < max_exact, d, large).astype(np.int32)


def _heads_major(a, nh):
    T = a.shape[0]
    return a.reshape(T, nh, HD).transpose(1, 0, 2)


def _heads_minor(a):
    nh, T, _ = a.shape
    return a.transpose(1, 0, 2).reshape(T, nh * HD)


def _slots(land):
    return land.reshape(4, 2 * land.shape[2], land.shape[3])


def _rows(land):
    return land.reshape(8 * land.shape[2], land.shape[3])


def _gview(g):
    s, K, n = g.shape
    return g.reshape(4, 2, K // 2, n) if s == 4 else g.reshape(4, 2, K // 8, n)


def _gate(a, token):
    return a * (1.0 + token[0, 0])


def _conv_small(f_small):
    fs = f_small.transpose(1, 2, 0, 3).reshape(2, 40, D)
    b_pw1 = f_small[:, :, 35:37, :].transpose(1, 0, 2, 3).reshape(2, 1, 2 * D)
    rev = jnp.concatenate([fs[:, CONV_W - 1::-1], jnp.zeros((2, 40 - CONV_W, D), F32)], axis=1)
    return dict(conv=fs, conv_rev=rev, b_pw1=b_pw1, b_pw2=fs[:, 34:35])


def run_step(x, target, P, ag, rs):
    T = x.shape[0]
    zero = jnp.zeros((1, 1, D), F32)
    nm, nf = P["norm_mix"], P["norm_ffn"]
    ag.forward("a0", [ag.token])
    W = ag.get("a0", [])
    sm = _conv_small(W["small"])
    h = x
    saved = []
    for l in range(2):
        xn, u, a = norm_mm_glu(h, nm, l, _slots(W[f"pw1_{l}"]), sm["b_pw1"], f"f_pw1_{l}")
        y, s = dwconv_ln_silu(a, sm["conv"], l, f"f_conv_{l}")
        b2 = sm["b_pw2"]
        if l == 0:
            b2 = _gate(b2, ag.forward("f0", [s]))
        h1 = mm_bias_res(s, _rows(W[f"pw2_{l}"]), b2, l, h, f"f_pw2_{l}")
        if l == 0:
            W.update(ag.get("f0", [h1]))
        xn2, gu, f = norm_mm_swiglu(h1, nf, l, _slots(W[f"up_{l}"]), f"f_up_{l}")
        nxt = "l1" if l == 0 else "l2"
        h2 = mm_bias_res(f, _rows(W[f"down_{l}"]), _gate(zero, ag.forward(nxt, [f])), 0, h1, f"f_down_{l}")
        W.update(ag.get(nxt, [h2]))
        saved.append(dict(h=h, xn=xn, u=u, a=a, y=y, s=s, h1=h1, xn2=xn2, gu=gu, f=f))
        h = h2
    h_kv = h
    kvn, kv = norm_mm(h, P["norm_kv"], 0, _rows(W["kv"]), "f_kv")
    kp = jnp.pad(_heads_major(kv[:, :N_KV * HD], N_KV), ((0, 0), (BLK, 0), (0, 0)))
    vp = jnp.pad(_heads_major(kv[:, N_KV * HD:], N_KV), ((0, 0), (BLK, 0), (0, 0)))
    kvt = jnp.pad(kv.T.reshape(2, N_KV, HD, T), ((0, 0), (0, 0), (0, 0), (BLK, 0)))
    kt, vt = kvt[0], kvt[1]
    bucket = _bucket_table()
    onehot = jnp.asarray(np.eye(N_BUCKETS, dtype=np.float32)[bucket])
    bias = jnp.einsum("qkb,bh->hkq", onehot, P["rel_bias"], precision=lax.Precision.HIGHEST)
    bias = bias.reshape(N_KV, GROUP, 2 * BLK, BLK).transpose(0, 2, 1, 3).reshape(1, N_KV, 2 * BLK, QW)
    bias = bias + jnp.asarray(band_mask())[:, None]
    for j in range(2):
        l = 2 + j
        xn, q = norm_mm(h, nm, l, _rows(W[f"wq_{j}"]), f"f_q_{j}", scale=HD ** -0.5)
        qh = q.T.reshape(N_KV, GROUP, HD, T)
        sink = jnp.broadcast_to(P["sinks"][j].reshape(N_KV, GROUP, 1), (N_KV, GROUP, BLK)).reshape(N_KV, 1, QW)
        oh = attn_fwd(qh, kp, vt, bias, sink, f"f_attn_{j}")
        attn = oh.reshape(N_HEADS * HD, T).T
        h1 = mm_bias_res(attn, _rows(W[f"wo_{j}"]), zero, 0, h, f"f_wo_{j}")
        xn2, gu, f = norm_mm_swiglu(h1, nf, l, _slots(W[f"up_{l}"]), f"f_up_{l}")
        zg = _gate(zero, ag.forward("l3", [f])) if j == 0 else zero
        h2 = mm_bias_res(f, _rows(W[f"down_{l}"]), zg, 0, h1, f"f_down_{l}")
        if j == 0:
            W.update(ag.get("l3", [h2]))
        saved.append(dict(h=h, xn=xn, qh=qh, oh=oh, sink=sink, attn=attn, h1=h1, xn2=xn2, gu=gu, f=f))
        h = h2

    dh, st_final = final_loss(h, P["norm_final"], target, "loss_head")

    S = dict(norm_ffn=[None] * 4, norm_mix=[None] * 4, conv=[None] * 2, taps=[None] * 2, b_pw1=[None] * 2,
             b_pw2=[None] * 2, sinks=[None] * 2)

    def ffn_bwd(dh, sv, l, nf, after=()):
        du = mmT_swiglu_bwd(dh, _rows(W[f"down_{l}"]), sv["gu"], f"b_down_{l}", after)
        gd = mm_dw(sv["f"], dh, f"w_down_{l}", 512, 1)
        gu = mm_dw(sv["xn2"], du, f"w_up_{l}", DFF // 2, 4)
        dh, dg = mmT_rmsbwd(du, _slots(W[f"up_{l}"]), sv["h1"], nf, l, dh, f"b_up_{l}")
        S["norm_ffn"][l] = dg[0]
        return dh, {f"down_{l}": _gview(gd), f"up_{l}": _gview(gu)}

    dk = dv = dbias = None
    sent = []
    for j in (1, 0):
        l = 2 + j
        sv = saved[l]
        dh, grads = ffn_bwd(dh, sv, l, nf, sent)
        dattn = mmT(dh, _rows(W[f"wo_{j}"]), f"b_wo_{j}")
        grads[f"wo_{j}"] = _gview(mm_dw(sv["attn"], dh, f"w_wo_{j}", 512, 1))
        doh = dattn.T.reshape(N_KV, GROUP, HD, T)
        dqh, dkj, dvj, dbj, dsj = attn_bwd(sv["qh"], kp, kt, vp, bias, sv["sink"], sv["oh"], doh, f"b_attn_{j}")
        dq = dqh.reshape(N_HEADS * HD, T).T
        grads[f"wq_{j}"] = _gview(mm_dw(sv["xn"], dq, f"w_q_{j}", 512, 1))
        dh, dg = mmT_rmsbwd(dq, _rows(W[f"wq_{j}"])[None], sv["h"], nm, l, dh, f"b_q_{j}")
        S["norm_mix"][l] = dg[0]
        S["sinks"][j] = jnp.sum(dsj.reshape(N_HEADS, BLK), axis=1)
        dk = dkj if dk is None else dk + dkj
        dv = dvj if dv is None else dv + dvj
        dbias = dbj if dbias is None else dbias + dbj
        if j == 1:
            sent = [rs.send("l3", grads)]

    dkv = jnp.concatenate([_heads_minor(dk[:, BLK:]), _heads_minor(dv[:, BLK:])], axis=1).astype(BF16)
    grads["kv"] = _gview(mm_dw(kvn, dkv, "w_kv", 512, 1))
    dh, dg = mmT_rmsbwd(dkv, _rows(W["kv"])[None], h_kv, P["norm_kv"], 0, dh, "b_kv")
    S["norm_kv"] = dg[0]
    dbh = dbias.reshape(N_KV, 2 * BLK, GROUP, BLK)
    S["rel_bias"] = jnp.einsum("vkgq,qkb->bvg", dbh, onehot, precision=lax.Precision.HIGHEST).reshape(N_BUCKETS, N_HEADS)
    sent = [rs.send("l2", grads)]
    nf = _gate(nf, rs.reduce("l3", [dh]))

    for l in (1, 0):
        sv = saved[l]
        dh, grads = ffn_bwd(dh, sv, l, nf, sent)
        conv = sm["conv"]
        if l == 0:
            conv = _gate(conv, rs.send("f0", grads))
            grads = {}
        dy, st = mmT_lnbwd(dh, _rows(W[f"pw2_{l}"]), sv["y"], conv, l, f"b_pw2_{l}")
        g2, S["b_pw2"][l] = mm_dw(sv["s"], dh, f"w_pw2_{l}", 512, 1, colsum=True)
        du, dtaps = dwconv_glu_bwd(dy, sv["a"], sv["u"], sm["conv"], sm["conv_rev"], l, f"b_conv_{l}")
        S["conv"][l] = st[0:3]
        S["taps"][l] = dtaps[0:CONV_W]
        if l == 0:
            rs.finish("l2", [du])
            nm = _gate(nm, rs.reduce("l1", [du]))
        g1, S["b_pw1"][l] = mm_dw(sv["xn"], du, f"w_pw1_{l}", 512, 4, colsum=True)
        grads[f"pw2_{l}"], grads[f"pw1_{l}"] = _gview(g2), _gview(g1)
        dh, dg = mmT_rmsbwd(du, _slots(W[f"pw1_{l}"]), sv["h"], nm, l, dh, f"b_pw1_{l}")
        S["norm_mix"][l] = dg[0]
        if l == 1:
            sent = [rs.send("l1", grads)]
            rs.finish("l3", [dh])
            nf = _gate(nf, rs.reduce("l2", [dh]))
    S["norm_final"] = st_final[0]
    S["loss"] = st_final[1]
    return grads, dh, S


R_CONV = 37
R_SMALL = 88


def _pack_small(S):
    rows = []
    for l in range(2):
        rows += [S["taps"][l], S["conv"][l][2:3], S["conv"][l][0:2], S["b_pw2"][l], S["b_pw1"][l].reshape(2, D)]
    rows += [jnp.stack(S["norm_mix"]), jnp.stack(S["norm_ffn"]), S["norm_kv"][None], S["norm_final"][None]]
    tail = jnp.concatenate([jnp.stack(S["sinks"]).reshape(-1), S["rel_bias"].reshape(-1)])
    rows += [jnp.pad(tail, (0, D - tail.shape[0]))[None], S["loss"][None]]
    v = jnp.concatenate(rows, axis=0)
    return jnp.pad(v, ((0, R_SMALL - v.shape[0]), (0, 0)))


def kernel(x, norm_mix, norm_ffn, conv_w_pw1, conv_b_pw1, conv_w_dw, conv_b_dw, conv_ln_g, conv_ln_b, conv_w_pw2, conv_b_pw2, norm_kv, w_kv, w_q, w_o, sinks, rel_bias, ffn_w_up, ffn_w_down, norm_final, loss_target, m_norm_mix, m_norm_ffn, m_conv_w_pw1, m_conv_b_pw1, m_conv_w_dw, m_conv_b_dw, m_conv_ln_g, m_conv_ln_b, m_conv_w_pw2, m_conv_b_pw2, m_norm_kv, m_w_kv, m_w_q, m_w_o, m_sinks, m_rel_bias, m_ffn_w_up, m_ffn_w_down, m_norm_final, v_norm_mix, v_norm_ffn, v_conv_w_pw1, v_conv_b_pw1, v_conv_w_dw, v_conv_b_dw, v_conv_ln_g, v_conv_ln_b, v_conv_w_pw2, v_conv_b_pw2, v_norm_kv, v_w_kv, v_w_q, v_w_o, v_sinks, v_rel_bias, v_ffn_w_up, v_ffn_w_down, v_norm_final):
    me = 2 * lax.axis_index("x") + lax.axis_index("y")
    weights = dict(norm_mix=norm_mix, norm_ffn=norm_ffn, conv_w_pw1=conv_w_pw1, conv_b_pw1=conv_b_pw1,
                   conv_w_dw=conv_w_dw, conv_b_dw=conv_b_dw, conv_ln_g=conv_ln_g, conv_ln_b=conv_ln_b,
                   conv_w_pw2=conv_w_pw2, conv_b_pw2=conv_b_pw2, norm_kv=norm_kv, w_kv=w_kv, w_q=w_q, w_o=w_o,
                   sinks=sinks, rel_bias=rel_bias, ffn_w_up=ffn_w_up, ffn_w_down=ffn_w_down, norm_final=norm_final)
    mom_m = dict(norm_mix=m_norm_mix, norm_ffn=m_norm_ffn, conv_w_pw1=m_conv_w_pw1, conv_b_pw1=m_conv_b_pw1,
                 conv_w_dw=m_conv_w_dw, conv_b_dw=m_conv_b_dw, conv_ln_g=m_conv_ln_g, conv_ln_b=m_conv_ln_b,
                 conv_w_pw2=m_conv_w_pw2, conv_b_pw2=m_conv_b_pw2, norm_kv=m_norm_kv, w_kv=m_w_kv, w_q=m_w_q,
                 w_o=m_w_o, sinks=m_sinks, rel_bias=m_rel_bias, ffn_w_up=m_ffn_w_up, ffn_w_down=m_ffn_w_down,
                 norm_final=m_norm_final)
    mom_v = dict(norm_mix=v_norm_mix, norm_ffn=v_norm_ffn, conv_w_pw1=v_conv_w_pw1, conv_b_pw1=v_conv_b_pw1,
                 conv_w_dw=v_conv_w_dw, conv_b_dw=v_conv_b_dw, conv_ln_g=v_conv_ln_g, conv_ln_b=v_conv_ln_b,
                 conv_w_pw2=v_conv_w_pw2, conv_b_pw2=v_conv_b_pw2, norm_kv=v_norm_kv, w_kv=v_w_kv, w_q=v_w_q,
                 w_o=v_w_o, sinks=v_sinks, rel_bias=v_rel_bias, ffn_w_up=v_ffn_w_up, ffn_w_down=v_ffn_w_down,
                 norm_final=v_norm_final)

    big = {"conv_w_pw1": "pw1", "conv_w_pw2": "pw2", "w_q": "wq", "w_o": "wo", "ffn_w_up": "up",
           "ffn_w_down": "down", "w_kv": "kv"}
    of_kind = {k: n for n, k in big.items()}

    def source(name):
        if name == "small":
            return jnp.concatenate(
                [conv_w_dw, conv_b_dw[:, None], conv_ln_g[:, None], conv_ln_b[:, None], conv_b_pw2[:, None],
                 conv_b_pw1.reshape(2, 2, 256), jnp.zeros((2, 3, 256), F32)], axis=1), None
        kind, _, l = name.partition("_")
        return weights[of_kind[kind]], (int(l) if l else None)

    ag = WeightGather(source, AG_GROUPS)
    rs = GradReduce({"pw1": (2, 512, 512), "pw2": (2, 128, D), "wq": (2, 128, D), "wo": (2, 128, D),
                     "up": (4, 512, DFF // 2), "down": (4, DFF // 8, D), "kv": (1, 128, 512)})

    P = dict(norm_mix=norm_mix[:, None], norm_ffn=norm_ffn[:, None], norm_kv=norm_kv[None, None],
             norm_final=norm_final[None], sinks=sinks, rel_bias=rel_bias)
    last, grad_x, S = run_step(x[0], loss_target[0], P, ag, rs)

    rs.finish("l1", [grad_x])
    small_flight, token = small_allreduce_start(_gate(_pack_small(S), rs.reduce("f0", [grad_x])), [])
    token = rs.send("c0", last, after=[token])
    delta, new_m, new_v, big_grads = {}, {}, {}, {}

    def update(n):
        shp = weights[n].shape
        r2 = (int(np.prod(shp[:-1])), shp[-1])
        g, d, nm, nv = adamw(weights[n].reshape(r2), rs.J[big[n]].reshape(r2), mom_m[n].reshape(r2),
                             mom_v[n].reshape(r2), f"adamw_{n}", copy_g=True)
        big_grads[n], delta[n], new_m[n], new_v[n] = g.reshape(shp), d.reshape(shp), nm.reshape(shp), nv.reshape(shp)

    rs.finish("f0", [token])
    for n in ("ffn_w_up", "ffn_w_down"):
        update(n)
    vsum = sum8(xchg_wait(small_flight, [delta["ffn_w_up"], delta["ffn_w_down"]])[1], "small_sum")

    col = lambda a: lax.dynamic_slice_in_dim(a, me * 256, 256, axis=-1)
    grads = {}
    for l in range(2):
        base = l * R_CONV
        grads.setdefault("conv_w_dw", []).append(col(vsum[base:base + 31]))
        grads.setdefault("conv_b_dw", []).append(col(vsum[base + 31]))
        grads.setdefault("conv_ln_g", []).append(col(vsum[base + 32]))
        grads.setdefault("conv_ln_b", []).append(col(vsum[base + 33]))
        grads.setdefault("conv_b_pw2", []).append(col(vsum[base + 34]))
        grads.setdefault("conv_b_pw1", []).append(
            lax.dynamic_slice_in_dim(vsum[base + 35:base + 37].reshape(2 * D), me * 512, 512, axis=0))
    grads = {k: jnp.stack(v) for k, v in grads.items()}
    base = 2 * R_CONV
    grads["norm_mix"] = vsum[base:base + 4]
    grads["norm_ffn"] = vsum[base + 4:base + 8]
    grads["norm_kv"] = vsum[base + 8]
    grads["norm_final"] = vsum[base + 9]
    grads["sinks"] = vsum[base + 10, 0:32].reshape(2, 16)
    grads["rel_bias"] = vsum[base + 10, 32:32 + 512].reshape(32, 16)
    loss = vsum[base + 11, 0]

    for n in weights:
        if n not in big:
            shp = weights[n].shape
            r2 = (int(np.prod(shp[:-1])), shp[-1])
            d, nm, nv = adamw(weights[n].reshape(r2), grads[n].reshape(r2), mom_m[n].reshape(r2),
                              mom_v[n].reshape(r2), f"adamw_{n}")
            delta[n], new_m[n], new_v[n] = d.reshape(shp), nm.reshape(shp), nv.reshape(shp)

    rs.reduce("c0", [vsum])
    for n in ("w_q", "w_o", "w_kv"):
        update(n)
    rs.finish("c0", [delta["w_kv"]])
    for n in ("conv_w_pw1", "conv_w_pw2"):
        update(n)
    grads.update(big_grads)

    order = list(weights)
    return (loss, grad_x[None], *[grads[n] for n in order], *[delta[n] for n in order],
            *[new_m[n] for n in order], *[new_v[n] for n in order])
```

```python
import functools
import math

import numpy as np
import jax
import jax.numpy as jnp
from jax import lax
from jax.experimental import pallas as pl
from jax.experimental.pallas import tpu as pltpu

F32 = jnp.float32
BF16 = jnp.bfloat16
MESH = pl.DeviceIdType.MESH

D = 1024
DFF = 2816
N_HEADS = 16
N_KV = 4
GROUP = 4
HD = 64
BLK = 128
CONV_W = 31
HALO = 32
N_BUCKETS = 32
MAX_DISTANCE = 128
EPS = 1e-6
NEG_INF = -1e30
TM = 512
TCV = 256
VMEM_LIMIT = 56 * 2 ** 20

ADAM_LR, ADAM_B1, ADAM_B2, ADAM_EPS, ADAM_WD, ADAM_STEP = 0.001, 0.9, 0.999, 1e-08, 0.01, 10


def _cp(*sem):
    return pltpu.CompilerParams(dimension_semantics=sem, vmem_limit_bytes=VMEM_LIMIT)


def _sigmoid(x):
    return 1.0 / (1.0 + jnp.exp(-x))


def _row(tm, n):
    return pl.BlockSpec((tm, n), lambda i: (i, 0))


def _const(shape):
    nd = len(shape)
    return pl.BlockSpec(shape, lambda i: (0,) * nd)


def _weight(shape):
    nd = len(shape)
    return pl.BlockSpec(shape, lambda i: (0,) * nd, pipeline_mode=pl.Buffered(1))


def _layer(shape, l):
    nd = len(shape)
    return pl.BlockSpec((None,) + tuple(shape), lambda i: (l,) + (0,) * nd)


def _dot(a, b):
    return jnp.dot(a, b, preferred_element_type=F32)


def _dot_nt(a, b):
    return lax.dot_general(a, b, (((1,), (1,)), ((), ())), preferred_element_type=F32)


def _dot_tn(a, b):
    return lax.dot_general(a, b, (((0,), (0,)), ((), ())), preferred_element_type=F32)


def _rms(x):
    return lax.rsqrt(jnp.mean(x * x, axis=-1, keepdims=True) + EPS)


def norm_mm_glu(h, g, l, w, b, name):
    T = h.shape[0]
    ns = w.shape[-1]

    def body(h_ref, g_ref, w_ref, b_ref, xn_ref, u_ref, a_ref):
        x = h_ref[...]
        xn = (x * _rms(x) * g_ref[...]).astype(BF16)
        xn_ref[...] = xn
        for s in range(2):
            lo, hi = s * ns, (s + 1) * ns
            u1 = _dot(xn, w_ref[s]) + b_ref[:, lo:hi]
            u2 = _dot(xn, w_ref[2 + s]) + b_ref[:, D + lo:D + hi]
            u_ref[:, lo:hi] = u1.astype(BF16)
            u_ref[:, D + lo:D + hi] = u2.astype(BF16)
            a_ref[:, lo:hi] = (u1 * _sigmoid(u2)).astype(BF16)

    return pl.pallas_call(
        body, name=name, grid=(T // TM,),
        in_specs=[_row(TM, D), _layer((1, D), l), _weight((4, D, ns)), _layer((1, 2 * D), l)],
        out_specs=[_row(TM, D), _row(TM, 2 * D), _row(TM, D)],
        out_shape=[jax.ShapeDtypeStruct((T, D), BF16), jax.ShapeDtypeStruct((T, 2 * D), BF16),
                   jax.ShapeDtypeStruct((T, D), BF16)],
        compiler_params=_cp("parallel"),
    )(h, g, w, b)


SUB = 8


def _make_shifts(sh):
    n = TCV + HALO - SUB
    for r in range(1, SUB):
        for r0 in range(0, n, 40):
            sh[r, r0:r0 + 40, :] = sh[0, pl.ds(r + r0, 40), :]


def _shifted(sh, off, rows, cols):
    return sh[off % SUB, pl.ds(off - off % SUB, rows), cols]


def _conv_taps(sh, w_ref, out_ref, first):
    RB, LB = 32, 512
    for r0 in range(0, TCV, RB):
        for c0 in range(0, out_ref.shape[1], LB):
            acc = jnp.zeros((RB, LB), F32)
            for k in range(CONV_W):
                acc = acc + w_ref[k:k + 1, c0:c0 + LB] * _shifted(sh, first + k + r0, RB, slice(c0, c0 + LB))
            out_ref[r0:r0 + RB, c0:c0 + LB] = acc


def dwconv_ln_silu(a, sm, l, name):
    T = a.shape[0]
    nb = TCV // HALO

    def body(cur_ref, prev_ref, sm_ref, y_ref, s_ref, sh, yb):
        i = pl.program_id(0)
        sh[0, 0:HALO, :] = jnp.where(i > 0, prev_ref[...].astype(F32), 0.0)
        sh[0, HALO:HALO + TCV, :] = cur_ref[...].astype(F32)
        _make_shifts(sh)
        _conv_taps(sh, sm_ref, yb, HALO - (CONV_W - 1))
        y = yb[...] + sm_ref[31:32, :]
        y_ref[...] = y.astype(BF16)
        mu = jnp.mean(y, axis=-1, keepdims=True)
        yc = y - mu
        rstd = lax.rsqrt(jnp.mean(yc * yc, axis=-1, keepdims=True) + EPS)
        z = yc * rstd * sm_ref[32:33, :] + sm_ref[33:34, :]
        s_ref[...] = (z * _sigmoid(z)).astype(BF16)

    return pl.pallas_call(
        body, name=name, grid=(T // TCV,),
        in_specs=[_row(TCV, D), pl.BlockSpec((HALO, D), lambda i: (jnp.maximum(i * nb - 1, 0), 0)),
                  _layer((40, D), l)],
        out_specs=[_row(TCV, D), _row(TCV, D)],
        out_shape=[jax.ShapeDtypeStruct((T, D), BF16), jax.ShapeDtypeStruct((T, D), BF16)],
        scratch_shapes=[pltpu.VMEM((SUB, TCV + HALO, D), F32), pltpu.VMEM((TCV, D), F32)],
        compiler_params=_cp("parallel"),
    )(a, a, sm)


def mm_bias_res(xb, w, b, bl, res, name):
    T, K = xb.shape

    def body(x_ref, w_ref, b_ref, r_ref, o_ref):
        o_ref[...] = _dot(x_ref[...], w_ref[...]) + b_ref[...] + r_ref[...]

    return pl.pallas_call(
        body, name=name, grid=(T // TM,),
        in_specs=[_row(TM, K), _weight((K, D)), _layer((1, D), bl), _row(TM, D)],
        out_specs=_row(TM, D), out_shape=jax.ShapeDtypeStruct((T, D), F32),
        compiler_params=_cp("parallel"),
    )(xb, w, b, res)


def norm_mm_swiglu(h, g, l, w, name):
    T = h.shape[0]
    ns = w.shape[-1]

    def body(h_ref, g_ref, w_ref, xn_ref, gu_ref, f_ref):
        x = h_ref[...]
        xn = (x * _rms(x) * g_ref[...]).astype(BF16)
        xn_ref[...] = xn
        for s in range(2):
            lo, hi = s * ns, (s + 1) * ns
            gate = _dot(xn, w_ref[s])
            up = _dot(xn, w_ref[2 + s])
            gu_ref[:, lo:hi] = gate.astype(BF16)
            gu_ref[:, DFF + lo:DFF + hi] = up.astype(BF16)
            f_ref[:, lo:hi] = (gate * _sigmoid(gate) * up).astype(BF16)

    return pl.pallas_call(
        body, name=name, grid=(T // TM,),
        in_specs=[_row(TM, D), _layer((1, D), l), _weight((4, D, ns))],
        out_specs=[_row(TM, D), _row(TM, 2 * DFF), _row(TM, DFF)],
        out_shape=[jax.ShapeDtypeStruct((T, D), BF16), jax.ShapeDtypeStruct((T, 2 * DFF), BF16),
                   jax.ShapeDtypeStruct((T, DFF), BF16)],
        compiler_params=_cp("parallel"),
    )(h, g, w)


def norm_mm(h, g, gl, w, name, scale=1.0):
    T = h.shape[0]
    N = w.shape[-1]

    def body(h_ref, g_ref, w_ref, xn_ref, o_ref):
        x = h_ref[...]
        xn = (x * _rms(x) * g_ref[...]).astype(BF16)
        xn_ref[...] = xn
        o_ref[...] = (_dot(xn, w_ref[...]) * scale).astype(BF16)

    return pl.pallas_call(
        body, name=name, grid=(T // TM,),
        in_specs=[_row(TM, D), _layer((1, D), gl), _weight((D, N))],
        out_specs=[_row(TM, D), _row(TM, N)],
        out_shape=[jax.ShapeDtypeStruct((T, D), BF16), jax.ShapeDtypeStruct((T, N), BF16)],
        compiler_params=_cp("parallel"),
    )(h, g, w)


QB = 16
QW = GROUP * BLK


def band_mask():
    qi = np.arange(QW)[None, :] % BLK
    kj = np.arange(2 * BLK)[:, None]
    band = ((kj < BLK) & (kj > qi)) | ((kj >= BLK) & (kj - BLK <= qi))
    first = band & (kj >= BLK)
    return np.where(np.stack([first, band]), 0.0, NEG_INF).astype(np.float32)


def _softmax_cols(s, sink):
    m = jnp.maximum(jnp.max(s, axis=0, keepdims=True), sink)
    p = jnp.exp(s - m)
    es = jnp.exp(sink - m)
    inv = 1.0 / (jnp.sum(p, axis=0, keepdims=True) + es)
    return p, inv, es


def _attn_specs(T):
    W = QB * BLK
    qspec = pl.BlockSpec((None, GROUP, HD, W), lambda kv, n: (kv, 0, 0, n))
    kspec = pl.BlockSpec((None, T + BLK, HD), lambda kv, n: (kv, 0, 0))
    ktspec = [pl.BlockSpec((None, HD, W), lambda kv, n: (kv, 0, n)),
              pl.BlockSpec((None, HD, BLK), lambda kv, n: (kv, 0, (n + 1) * QB))]
    bspec = pl.BlockSpec((2, None, 2 * BLK, QW), lambda kv, n: (0, kv, 0, 0))
    sspec = pl.BlockSpec((None, 1, QW), lambda kv, n: (kv, 0, 0))
    return qspec, kspec, ktspec, bspec, sspec


def _attn_block(n, b):
    blk = n * QB + b
    rows = pl.ds(pl.multiple_of(blk * BLK, BLK), 2 * BLK)
    return rows, (jnp.minimum(blk, 1) if b == 0 else 1)


def _band_cols(main_ref, tail_ref, b):
    if b < QB - 1:
        return main_ref[:, b * BLK:(b + 2) * BLK]
    return jnp.concatenate([main_ref[:, b * BLK:], tail_ref[...]], axis=1)


def _heads_side_by_side(ref, qs):
    return jnp.concatenate([ref[g, :, qs] for g in range(GROUP)], axis=1)


def attn_fwd(q, kp, vt, bias, sink, name):
    T = q.shape[3]
    qspec, kspec, ktspec, bspec, sspec = _attn_specs(T)

    def body(q_ref, k_ref, vt_ref, vtt_ref, b_ref, s_ref, o_ref, pb):
        n = pl.program_id(1)

        def scores(b):
            return _dot(k_ref[_attn_block(n, b)[0], :], _heads_side_by_side(q_ref, slice(b * BLK, (b + 1) * BLK)))

        st_next = scores(0)
        for b in range(QB):
            rows, table = _attn_block(n, b)
            qs = slice(b * BLK, (b + 1) * BLK)
            st = st_next
            if b + 1 < QB:
                st_next = scores(b + 1)
            for g in range(GROUP):
                hs = slice(g * BLK, (g + 1) * BLK)
                p, inv, _ = _softmax_cols(st[:, hs] + b_ref[table, :, hs], s_ref[:, hs])
                pb[:, hs] = (p * inv).astype(BF16)
            ot = _dot(_band_cols(vt_ref, vtt_ref, b), pb[...])
            for g in range(GROUP):
                o_ref[g, :, qs] = ot[:, g * BLK:(g + 1) * BLK].astype(BF16)

    return pl.pallas_call(
        body, name=name, grid=(N_KV, T // (QB * BLK)),
        in_specs=[qspec, kspec, *ktspec, bspec, sspec], out_specs=qspec,
        out_shape=jax.ShapeDtypeStruct((N_KV, GROUP, HD, T), BF16),
        scratch_shapes=[pltpu.VMEM((2 * BLK, QW), BF16)],
        compiler_params=_cp("parallel", "parallel"),
    )(q, kp, vt, vt, bias, sink)


def attn_bwd(q, kp, kt, vp, bias, sink, o, do, name):
    T = q.shape[3]
    qspec, kspec, ktspec, bspec, sspec = _attn_specs(T)

    def body(q_ref, k_ref, kt_ref, ktt_ref, v_ref, b_ref, s_ref, o_ref, do_ref,
             dq_ref, dk_ref, dv_ref, db_ref, ds_ref, pb, dsb):
        n = pl.program_id(1)

        @pl.when(n == 0)
        def _():
            dk_ref[...] = jnp.zeros_like(dk_ref)
            dv_ref[...] = jnp.zeros_like(dv_ref)
            db_ref[...] = jnp.zeros_like(db_ref)
            ds_ref[...] = jnp.zeros_like(ds_ref)

        def products(b):
            rows = _attn_block(n, b)[0]
            qs = slice(b * BLK, (b + 1) * BLK)
            q4, do4 = _heads_side_by_side(q_ref, qs), _heads_side_by_side(do_ref, qs)
            return q4, do4, _dot(k_ref[rows, :], q4), _dot(v_ref[rows, :], do4)

        ahead = products(0)
        for b in range(QB):
            rows, table = _attn_block(n, b)
            qs = slice(b * BLK, (b + 1) * BLK)
            q4, do4, st, dpt = ahead
            if b + 1 < QB:
                ahead = products(b + 1)
            for g in range(GROUP):
                hs = slice(g * BLK, (g + 1) * BLK)
                p, inv, es = _softmax_cols(st[:, hs] + b_ref[table, :, hs], s_ref[:, hs])
                probs = p * inv
                delta = jnp.sum(do_ref[g, :, qs].astype(F32) * o_ref[g, :, qs].astype(F32), axis=0, keepdims=True)
                dS = probs * (dpt[:, hs] - delta)
                ds_ref[:, hs] += -(es * inv) * delta
                db_ref[:, hs] += dS
                pb[:, hs] = probs.astype(BF16)
                dsb[:, hs] = dS.astype(BF16)
            dqt = _dot(_band_cols(kt_ref, ktt_ref, b), dsb[...]) * (HD ** -0.5)
            for g in range(GROUP):
                dq_ref[g, :, qs] = dqt[:, g * BLK:(g + 1) * BLK].astype(BF16)
            dk_ref[rows, :] += _dot_nt(dsb[...], q4)
            dv_ref[rows, :] += _dot_nt(pb[...], do4)

    kout = pl.BlockSpec((None, T + BLK, HD), lambda kv, n: (kv, 0, 0))
    dbspec = pl.BlockSpec((None, 2 * BLK, QW), lambda kv, n: (kv, 0, 0))
    return pl.pallas_call(
        body, name=name, grid=(N_KV, T // (QB * BLK)),
        in_specs=[qspec, kspec, *ktspec, kspec, bspec, sspec, qspec, qspec],
        out_specs=[qspec, kout, kout, dbspec, sspec],
        out_shape=[jax.ShapeDtypeStruct((N_KV, GROUP, HD, T), BF16),
                   jax.ShapeDtypeStruct((N_KV, T + BLK, HD), F32), jax.ShapeDtypeStruct((N_KV, T + BLK, HD), F32),
                   jax.ShapeDtypeStruct((N_KV, 2 * BLK, QW), F32), jax.ShapeDtypeStruct((N_KV, 1, QW), F32)],
        scratch_shapes=[pltpu.VMEM((2 * BLK, QW), BF16), pltpu.VMEM((2 * BLK, QW), BF16)],
        compiler_params=_cp("parallel", "arbitrary"),
    )(q, kp, kt, kt, vp, bias, sink, o, do)


def final_loss(h, g, target, name):
    T = h.shape[0]

    def body(h_ref, g_ref, t_ref, dh_ref, st_ref):
        i = pl.program_id(0)

        @pl.when(i == 0)
        def _():
            st_ref[...] = jnp.zeros_like(st_ref)

        x = h_ref[...]
        r = _rms(x)
        xh = x * r
        e = xh * g_ref[...] - t_ref[...]
        loss = 0.5 * jnp.sum(jnp.mean(e * e, axis=-1, keepdims=True))
        dy = e * (1.0 / D)
        st_ref[0:1, :] += jnp.sum(dy * xh, axis=0, keepdims=True)
        lane = lax.broadcasted_iota(jnp.int32, (1, D), 1)
        st_ref[1:2, :] += jnp.where(lane == 0, loss, 0.0)
        dxh = dy * g_ref[...]
        dh_ref[...] = r * (dxh - xh * jnp.mean(dxh * xh, axis=-1, keepdims=True))

    return pl.pallas_call(
        body, name=name, grid=(T // TM,),
        in_specs=[_row(TM, D), _const((1, D)), _row(TM, D)],
        out_specs=[_row(TM, D), _const((2, D))],
        out_shape=[jax.ShapeDtypeStruct((T, D), F32), jax.ShapeDtypeStruct((2, D), F32)],
        compiler_params=_cp("arbitrary"),
    )(h, g, target)


def mm_dw(x, dy, name, tn, slots, colsum=False):
    T, K = x.shape
    split = dy.ndim == 3
    N = dy.shape[-1] * (2 if split else 1)
    tt = min(T, 2048 if K <= 1024 else 1024)
    nt = T // tt
    ns = N // slots
    per = ns // tn

    def body(x_ref, dy_ref, *rest):
        if colsum:
            dw_ref, cs_ref, acc, cacc = rest
        else:
            dw_ref, acc = rest
        t = pl.program_id(1)

        @pl.when(t == 0)
        def _():
            acc[...] = jnp.zeros_like(acc)
            if colsum:
                cacc[...] = jnp.zeros_like(cacc)

        dyv = dy_ref[...]
        acc[...] += _dot_tn(x_ref[...].astype(BF16), dyv.astype(BF16))
        if colsum:
            cacc[...] += jnp.sum(dyv.astype(F32), axis=0, keepdims=True)

        @pl.when(t == nt - 1)
        def _():
            dw_ref[...] = acc[...].astype(BF16)
            if colsum:
                cs_ref[...] = cacc[...]

    if split:
        half = N // 2 // tn
        dy_spec = pl.BlockSpec((None, tt, tn), lambda j, t: (j // half, t, j % half))
    else:
        dy_spec = pl.BlockSpec((tt, tn), lambda j, t: (t, j))
    out_specs = [pl.BlockSpec((None, K, tn), lambda j, t: (j // per, 0, j % per))]
    out_shape = [jax.ShapeDtypeStruct((slots, K, ns), BF16)]
    scratch = [pltpu.VMEM((K, tn), F32)]
    if colsum:
        out_specs.append(pl.BlockSpec((1, tn), lambda j, t: (0, j)))
        out_shape.append(jax.ShapeDtypeStruct((1, N), F32))
        scratch.append(pltpu.VMEM((1, tn), F32))
    res = pl.pallas_call(
        body, name=name, grid=(N // tn, nt),
        in_specs=[pl.BlockSpec((tt, K), lambda j, t: (t, 0)), dy_spec],
        out_specs=out_specs, out_shape=out_shape, scratch_shapes=scratch,
        compiler_params=_cp("parallel", "arbitrary"),
    )(x, dy)
    return tuple(res) if colsum else res[0]


def mmT_swiglu_bwd(dh, w, gu, name, after=()):
    T = dh.shape[0]
    cw = 256

    def body(dh_ref, w_ref, gu_ref, *rest):
        du_ref = rest[-1]
        dhb = dh_ref[...].astype(BF16)
        ahead = _dot_nt(dhb, w_ref[0:cw, :])
        for lo in range(0, DFF, cw):
            hi = lo + cw
            df = ahead
            if hi < DFF:
                ahead = _dot_nt(dhb, w_ref[hi:hi + cw, :])
            gate = gu_ref[:, lo:hi].astype(F32)
            up = gu_ref[:, DFF + lo:DFF + hi].astype(F32)
            sg = _sigmoid(gate)
            silu = gate * sg
            du_ref[:, lo:hi] = (df * (up * (sg + silu * (1.0 - sg)))).astype(BF16)
            du_ref[:, DFF + lo:DFF + hi] = (df * silu).astype(BF16)

    return pl.pallas_call(
        body, name=name, grid=(T // TM,),
        in_specs=[_row(TM, D), _weight((DFF, D)), _row(TM, 2 * DFF)] + [ANY] * len(after),
        out_specs=_row(TM, 2 * DFF), out_shape=jax.ShapeDtypeStruct((T, 2 * DFF), BF16),
        compiler_params=_cp("parallel"),
    )(dh, w, gu, *after)


def mmT_rmsbwd(du, w, h, g, gl, dh_in, name):
    split = du.ndim == 3
    T = du.shape[-2]
    N = du.shape[-1] * (2 if split else 1)
    slots = w.shape[0]
    ns = N // slots

    RH = TM // 2

    def piece(du_ref, s, rows):
        if split:
            per = slots // 2
            return du_ref[s // per, rows, (s % per) * ns:(s % per + 1) * ns]
        return du_ref[rows, s * ns:(s + 1) * ns]

    def body(du_ref, w_ref, h_ref, g_ref, di_ref, dh_ref, dg_ref):
        i = pl.program_id(0)

        @pl.when(i == 0)
        def _():
            dg_ref[...] = jnp.zeros_like(dg_ref)

        def products(k):
            rows = slice(k * RH, (k + 1) * RH)
            dxn = _dot_nt(piece(du_ref, 0, rows), w_ref[0])
            for s in range(1, slots):
                dxn = dxn + _dot_nt(piece(du_ref, s, rows), w_ref[s])
            return dxn

        ahead = products(0)
        for k in range(TM // RH):
            rows = slice(k * RH, (k + 1) * RH)
            dxn = ahead
            if (k + 1) * RH < TM:
                ahead = products(k + 1)
            x = h_ref[rows, :]
            r = _rms(x)
            xh = x * r
            dg_ref[0:1, :] += jnp.sum(dxn * xh, axis=0, keepdims=True)
            dxh = dxn * g_ref[...]
            dh_ref[rows, :] = di_ref[rows, :] + r * (dxh - xh * jnp.mean(dxh * xh, axis=-1, keepdims=True))

    return pl.pallas_call(
        body, name=name, grid=(T // TM,),
        in_specs=[pl.BlockSpec((2, TM, N // 2), lambda i: (0, i, 0)) if split else _row(TM, N),
                  _weight((slots, D, ns)), _row(TM, D), _layer((1, D), gl), _row(TM, D)],
        out_specs=[_row(TM, D), _const((1, D))],
        out_shape=[jax.ShapeDtypeStruct((T, D), F32), jax.ShapeDtypeStruct((1, D), F32)],
        compiler_params=_cp("arbitrary"),
    )(du, w, h, g, dh_in)


def mmT(dh, w, name):
    T = dh.shape[0]
    N = w.shape[0]

    def body(dh_ref, w_ref, o_ref):
        o_ref[...] = _dot_nt(dh_ref[...].astype(BF16), w_ref[...]).astype(BF16)

    return pl.pallas_call(
        body, name=name, grid=(T // TM,),
        in_specs=[_row(TM, D), _weight((N, D))],
        out_specs=_row(TM, N), out_shape=jax.ShapeDtypeStruct((T, N), BF16),
        compiler_params=_cp("parallel"),
    )(dh, w)


def mmT_lnbwd(dh, w, y, sm, l, name):
    T = dh.shape[0]

    def body(dh_ref, w_ref, y_ref, sm_ref, dy_ref, st_ref):
        i = pl.program_id(0)

        @pl.when(i == 0)
        def _():
            st_ref[...] = jnp.zeros_like(st_ref)

        ds = _dot_nt(dh_ref[...].astype(BF16), w_ref[...])
        y = y_ref[...].astype(F32)
        mu = jnp.mean(y, axis=-1, keepdims=True)
        yc = y - mu
        rstd = lax.rsqrt(jnp.mean(yc * yc, axis=-1, keepdims=True) + EPS)
        xh = yc * rstd
        gam = sm_ref[32:33, :]
        z = xh * gam + sm_ref[33:34, :]
        sg = _sigmoid(z)
        dz = ds * sg * (1.0 + z * (1.0 - sg))
        st_ref[1:2, :] += jnp.sum(dz * xh, axis=0, keepdims=True)
        st_ref[2:3, :] += jnp.sum(dz, axis=0, keepdims=True)
        dxh = dz * gam
        dy = rstd * (dxh - jnp.mean(dxh, axis=-1, keepdims=True) - xh * jnp.mean(dxh * xh, axis=-1, keepdims=True))
        st_ref[0:1, :] += jnp.sum(dy, axis=0, keepdims=True)
        dy_ref[...] = dy.astype(BF16)

    return pl.pallas_call(
        body, name=name, grid=(T // TM,),
        in_specs=[_row(TM, D), _weight((D, D)), _row(TM, D), _layer((40, D), l)],
        out_specs=[_row(TM, D), _const((3, D))],
        out_shape=[jax.ShapeDtypeStruct((T, D), BF16), jax.ShapeDtypeStruct((3, D), F32)],
        compiler_params=_cp("arbitrary"),
    )(dh, w, y, sm)


CH = 512


def dwconv_glu_bwd(dy, a, u, sm, smrev, l, name):
    T = dy.shape[0]
    nr, nc = T // TCV, D // CH
    nb = TCV // HALO
    last = T // HALO - 1

    def body(dy_ref, dyn_ref, a_ref, ap_ref, u1_ref, u2_ref, sm_ref, rev_ref, du_ref, dw_ref, shd, sha, da):
        i = pl.program_id(0)
        r = i % nr

        @pl.when(r == 0)
        def _():
            dw_ref[...] = jnp.zeros_like(dw_ref)

        shd[0, 0:TCV, :] = dy_ref[...].astype(F32)
        shd[0, TCV:TCV + HALO, :] = jnp.where(r < nr - 1, dyn_ref[...].astype(F32), 0.0)
        sha[0, 0:HALO, :] = jnp.where(r > 0, ap_ref[...].astype(F32), 0.0)
        sha[0, HALO:HALO + TCV, :] = a_ref[...].astype(F32)
        _make_shifts(shd)
        _make_shifts(sha)
        _conv_taps(shd, rev_ref, da, 0)
        for kg in range(0, CONV_W, SUB):
            taps = range(kg, min(kg + SUB, CONV_W))
            part = [jnp.zeros((SUB, CH), F32) for _ in taps]
            for r0 in range(0, TCV, SUB):
                d = shd[0, r0:r0 + SUB, :]
                for j, k in enumerate(taps):
                    part[j] = part[j] + d * _shifted(sha, HALO - (CONV_W - 1) + k + r0, SUB, slice(None))
            for j, k in enumerate(taps):
                dw_ref[k:k + 1, :] += jnp.sum(part[j], axis=0, keepdims=True)
        dav = da[...]
        u1 = u1_ref[...].astype(F32)
        sg = _sigmoid(u2_ref[...].astype(F32))
        du_ref[0] = (dav * sg).astype(BF16)
        du_ref[1] = (dav * u1 * sg * (1.0 - sg)).astype(BF16)

    tile = lambda i: (i % nr, i // nr)
    in_specs = [pl.BlockSpec((TCV, CH), tile),
                pl.BlockSpec((HALO, CH), lambda i: (jnp.minimum((i % nr + 1) * nb, last), i // nr)),
                pl.BlockSpec((TCV, CH), tile),
                pl.BlockSpec((HALO, CH), lambda i: (jnp.maximum((i % nr) * nb - 1, 0), i // nr)),
                pl.BlockSpec((TCV, CH), tile), pl.BlockSpec((TCV, CH), lambda i: (i % nr, nc + i // nr)),
                pl.BlockSpec((None, 40, CH), lambda i: (l, 0, i // nr)),
                pl.BlockSpec((None, 40, CH), lambda i: (l, 0, i // nr))]
    return pl.pallas_call(
        body, name=name, grid=(nr * nc,), in_specs=in_specs,
        out_specs=[pl.BlockSpec((2, TCV, CH), lambda i: (0, i % nr, i // nr)),
                   pl.BlockSpec((CONV_W, CH), lambda i: (0, i // nr))],
        out_shape=[jax.ShapeDtypeStruct((2, T, D), BF16), jax.ShapeDtypeStruct((CONV_W, D), F32)],
        scratch_shapes=[pltpu.VMEM((SUB, TCV + HALO, CH), F32), pltpu.VMEM((SUB, TCV + HALO, CH), F32),
                        pltpu.VMEM((TCV, CH), F32)],
        compiler_params=_cp("arbitrary"),
    )(dy, dy, a, a, u, u, sm, smrev)


def _rows_tile(R):
    for t in (512, 256, 128, 64, 32, 16, 8):
        if R % t == 0:
            return t
    return R


def add8_into(J, l, g, others, where, name):
    R, C = g.shape[2:]
    tr = R // 2

    def body(w_ref, g_ref, x_ref, j_in, j_ref):
        acc = g_ref[...].astype(F32)
        for k in range(7):
            acc = acc + x_ref[k].astype(F32)
        j_ref[...] = acc

    return pl.pallas_call(
        body, name=name,
        grid_spec=pltpu.PrefetchScalarGridSpec(
            num_scalar_prefetch=1, grid=(R // tr,),
            in_specs=[pl.BlockSpec((None, None, tr, C), lambda i, w: (w[0], w[1], i, 0)),
                      pl.BlockSpec((7, tr, C), lambda i, w: (0, i, 0)), ANY],
            out_specs=pl.BlockSpec((None, None, tr, C), lambda i, w: (l, w[1], i, 0))),
        out_shape=jax.ShapeDtypeStruct(J.shape, F32), input_output_aliases={3: 0},
        compiler_params=_cp("parallel"),
    )(where, g, others, J)


def adamw(w, g, m, v, name, copy_g=False):
    R, C = w.shape
    tr = _rows_tile(R)

    def body(w_ref, g_ref, m_ref, v_ref, *outs):
        d_ref, nm_ref, nv_ref = outs[-3:]
        gv = g_ref[...]
        if copy_g:
            outs[0][...] = gv
        nm = ADAM_B1 * m_ref[...] + (1.0 - ADAM_B1) * gv
        nv = ADAM_B2 * v_ref[...] + (1.0 - ADAM_B2) * (gv * gv)
        m_hat = nm / (1.0 - ADAM_B1 ** ADAM_STEP)
        v_hat = nv / (1.0 - ADAM_B2 ** ADAM_STEP)
        d_ref[...] = -ADAM_LR * (m_hat / (jnp.sqrt(v_hat) + ADAM_EPS) + ADAM_WD * w_ref[...])
        nm_ref[...] = nm
        nv_ref[...] = nv

    sd = jax.ShapeDtypeStruct((R, C), F32)
    n_out = 4 if copy_g else 3
    return pl.pallas_call(
        body, name=name, grid=(R // tr,),
        in_specs=[_row(tr, C)] * 4, out_specs=[_row(tr, C)] * n_out, out_shape=[sd] * n_out,
        compiler_params=_cp("parallel"),
    )(w, g, m, v)


ANY = pl.BlockSpec(memory_space=pl.ANY)
HBM = pl.BlockSpec(memory_space=pltpu.HBM)
SEM = pl.BlockSpec(memory_space=pltpu.SEMAPHORE)
EFFECT = pltpu.SideEffectType.DATAFLOW_SIDE_EFFECTING


def _place():
    x, y, c = lax.axis_index("x"), lax.axis_index("y"), lax.axis_index("c")
    chips = [(1 - x, y), (x, 1 - y), (1 - x, 1 - y)]
    return x, y, c, chips


def _copy(src, dst, send, recv, k, to):
    return pltpu.make_async_remote_copy(src_ref=src, dst_ref=dst, send_sem=send.at[k], recv_sem=recv.at[k],
                                        device_id=to, device_id_type=MESH)


def xchg_start(name, bufs, plan, n, after=()):
    nb = len(bufs)

    na = len(after)

    def body(*refs):
        send, recv, token = refs[nb + na], refs[nb + na + 1], refs[-1]
        for k, (src, dst, to) in enumerate(plan(refs[:nb])):
            _copy(src, dst, send, recv, k, to).start()
        token[...] = jnp.zeros_like(token)

    outs = pl.pallas_call(
        body, name=name,
        out_shape=(pltpu.SemaphoreType.DMA((n,)), pltpu.SemaphoreType.DMA((n,)),
                   *[pltpu.HBM(b.shape, b.dtype) for b in bufs], jax.ShapeDtypeStruct((8, 128), F32)),
        in_specs=[HBM] * nb + [ANY] * na,
        out_specs=(SEM, SEM, *[HBM] * nb, pl.BlockSpec(memory_space=pltpu.VMEM)),
        input_output_aliases={i: 2 + i for i in range(nb)},
        compiler_params=pltpu.CompilerParams(has_side_effects=EFFECT),
    )(*[pltpu.with_memory_space_constraint(b, pltpu.HBM) for b in bufs], *after)
    return dict(name=name, send=outs[0], recv=outs[1], bufs=list(outs[2:2 + nb]), plan=plan), outs[-1]


def xchg_wait(flight, after):
    bufs, plan = flight["bufs"], flight["plan"]
    nb = len(bufs)

    def body(*refs):
        send, recv = refs[nb], refs[nb + 1]
        for k, (src, dst, to) in enumerate(plan(refs[:nb])):
            cp = _copy(src, dst, send, recv, k, to)
            cp.wait_send()
            cp.wait_recv()

    outs = pl.pallas_call(
        body, name=flight["name"] + "_wait",
        out_shape=tuple(pltpu.HBM(b.shape, b.dtype) for b in bufs),
        in_specs=[HBM] * nb + [SEM, SEM] + [ANY] * len(after),
        out_specs=tuple([HBM] * nb), input_output_aliases={i: i for i in range(nb)},
        compiler_params=pltpu.CompilerParams(has_side_effects=EFFECT),
    )(*bufs, flight["send"], flight["recv"], *after)
    return list(outs)


def _flip(k, x, y, c):
    return ((1 - x) if k & 4 else x, (1 - y) if k & 2 else y, (1 - c) if k & 1 else c)


def cast_into_slot(srcs, name, after):
    me = (2 * lax.axis_index("x") + lax.axis_index("y")).astype(jnp.int32).reshape(1)
    ns = len(srcs)

    def body(me_ref, *refs):
        outs = refs[ns + len(after):]
        for t in range(ns):
            outs[t][...] = refs[t][...].astype(outs[t].dtype).reshape(outs[t].shape)

    in_specs, out_specs, out_shape = [], [], []
    for arr, l in srcs:
        if l is None:
            in_specs.append(pl.BlockSpec(arr.shape, lambda i, w, nd=arr.ndim: (0,) * nd))
            a2, b, dt = (arr.shape[0] // 2, arr.shape[1], BF16) if arr.ndim == 2 else (arr.shape[1], arr.shape[2], F32)
        else:
            in_specs.append(pl.BlockSpec((None,) + arr.shape[1:], lambda i, w, l=l: (l, 0, 0)))
            a2, b, dt = arr.shape[1] // 2, arr.shape[2], BF16
        out_specs.append(pl.BlockSpec((None, 2, a2, b), lambda i, w: (w[0], 0, 0, 0)))
        out_shape.append(jax.ShapeDtypeStruct((4, 2, a2, b), dt))
    in_specs += [ANY] * len(after)
    return pl.pallas_call(
        body, name=name,
        grid_spec=pltpu.PrefetchScalarGridSpec(num_scalar_prefetch=1, grid=(1,), in_specs=in_specs,
                                               out_specs=out_specs),
        out_shape=out_shape, compiler_params=_cp("arbitrary"),
    )(me, *[arr for arr, _ in srcs], *after)


class WeightGather:
    def __init__(self, source, groups):
        self.names = dict(groups)
        self.ici, self.d2d = {}, {}
        self.token = None
        for gname, names in groups:
            nt = len(names)
            after = [] if self.token is None else [self.token]
            lands = cast_into_slot([source(n) for n in names], f"ag_cast_{gname}", after)

            def plan(refs, nt=nt):
                x, y, c, chips = _place()
                out = []
                for t in range(nt):
                    mine = refs[t].at[2 * x + y, c]
                    out += [(mine, mine, (cx, cy, c)) for cx, cy in chips]
                return out

            self.ici[gname], self.token = xchg_start(f"ag_ici_{gname}", lands, plan, 3 * nt, after=after)

    def forward(self, gname, after):
        nt = len(self.names[gname])
        lands = xchg_wait(self.ici.pop(gname), after)

        def plan(refs):
            x, y, c, chips = _place()
            out = []
            for t in range(nt):
                for cx, cy in chips:
                    piece = refs[t].at[2 * cx + cy, c]
                    out.append((piece, piece, (x, y, 1 - c)))
            return out

        self.d2d[gname], token = xchg_start(f"ag_d2d_{gname}", lands, plan, 3 * nt)
        return token

    def get(self, gname, after):
        lands = xchg_wait(self.d2d.pop(gname), after)
        return dict(zip(self.names[gname], lands))


class GradReduce:
    def __init__(self, kinds):
        self.J = {k: lax.empty((L, 2, a2, b), F32) for k, (L, a2, b) in kinds.items()}
        self.x, self.j = {}, {}

    @staticmethod
    def _where(name):
        kind, _, l = name.partition("_")
        return kind, int(l or 0)

    def send(self, gname, grads, after=()):
        names = list(grads)
        nt = len(names)
        gs = [grads[n] for n in names]
        xs = [lax.empty((7,) + g.shape[2:], g.dtype) for g in gs]

        def plan(refs):
            x, y, c, _ = _place()
            out = []
            for t in range(nt):
                for k in range(1, 8):
                    px, py, pc = _flip(k, x, y, c)
                    out.append((refs[t].at[2 * px + py, pc], refs[nt + t].at[k - 1], (px, py, pc)))
            return out

        flight, token = xchg_start(f"rs_x_{gname}", gs + xs, plan, 7 * nt, after=after)
        self.x[gname] = (names, flight)
        return token

    def reduce(self, gname, after):
        names, flight = self.x.pop(gname)
        nt = len(names)
        bufs = xchg_wait(flight, after)
        mine = jnp.stack([2 * lax.axis_index("x") + lax.axis_index("y"), lax.axis_index("c")]).astype(jnp.int32)
        where = [self._where(n) for n in names]
        js = [add8_into(self.J[kind], l, bufs[t], bufs[nt + t], mine, f"rs_add_{names[t]}")
              for t, (kind, l) in enumerate(where)]

        def plan(refs):
            x, y, c, _ = _place()
            out = []
            for t in range(nt):
                half = refs[t].at[where[t][1], c]
                out.append((half, half, (x, y, 1 - c)))
            return out

        flight, token = xchg_start(f"rs_join_{gname}", js, plan, nt)
        self.j[gname] = (where, flight)
        return token

    def finish(self, gname, after):
        where, flight = self.j.pop(gname)
        for (kind, _), j in zip(where, xchg_wait(flight, after)):
            self.J[kind] = j


def small_allreduce_start(v, after):
    me = 4 * lax.axis_index("x") + 2 * lax.axis_index("y") + lax.axis_index("c")
    land = lax.dynamic_update_slice(lax.empty((8,) + v.shape, v.dtype), v[None], (me, 0, 0))

    def plan(refs):
        x, y, c, _ = _place()
        return [(refs[0], refs[1].at[4 * x + 2 * y + c], _flip(k, x, y, c)) for k in range(1, 8)]

    return xchg_start("small_allreduce", [v, land], plan, 7, after=after)


def sum8(all8, name):
    def body(x_ref, o_ref):
        acc = x_ref[0]
        for d in range(1, 8):
            acc = acc + x_ref[d]
        o_ref[...] = acc

    return pl.pallas_call(
        body, name=name,
        in_specs=[pl.BlockSpec(memory_space=pltpu.VMEM)], out_specs=pl.BlockSpec(memory_space=pltpu.VMEM),
        out_shape=jax.ShapeDtypeStruct(all8.shape[1:], F32),
        compiler_params=pltpu.CompilerParams(vmem_limit_bytes=VMEM_LIMIT),
    )(all8)


AG_GROUPS = (("a0", ("pw1_0", "pw2_0", "small")), ("f0", ("up_0", "down_0")),
             ("l1", ("pw1_1", "pw2_1", "up_1", "down_1")), ("l2", ("kv", "wq_0", "wo_0", "up_2", "down_2")),
             ("l3", ("wq_1", "wo_1", "up_3", "down_3")))


def _bucket_table():
    qi = np.arange(BLK)[:, None]
    kj = np.arange(2 * BLK)[None, :]
    d = np.maximum(qi + BLK - kj, 0)
    max_exact = N_BUCKETS // 2
    log_ratio = (np.log(np.maximum(d, 1).astype(np.float32) / np.float32(max_exact))
                 / np.float32(math.log(MAX_DISTANCE / max_exact))).astype(np.float32)
    large = max_exact + (log_ratio * np.float32(N_BUCKETS - max_exact)).astype(np.int32)
    large = np.minimum(large, N_BUCKETS - 1)
    return np.where(d < max_exact, d, large).astype(np.int32)


def _heads_major(a, nh):
    T = a.shape[0]
    return a.reshape(T, nh, HD).transpose(1, 0, 2)


def _heads_minor(a):
    nh, T, _ = a.shape
    return a.transpose(1, 0, 2).reshape(T, nh * HD)


def _slots(land):
    return land.reshape(4, 2 * land.shape[2], land.shape[3])


def _rows(land):
    return land.reshape(8 * land.shape[2], land.shape[3])


def _gview(g):
    s, K, n = g.shape
    return g.reshape(4, 2, K // 2, n) if s == 4 else g.reshape(4, 2, K // 8, n)


def _gate(a, token):
    return a * (1.0 + token[0, 0])


def _conv_small(f_small):
    fs = f_small.transpose(1, 2, 0, 3).reshape(2, 40, D)
    b_pw1 = f_small[:, :, 35:37, :].transpose(1, 0, 2, 3).reshape(2, 1, 2 * D)
    rev = jnp.concatenate([fs[:, CONV_W - 1::-1], jnp.zeros((2, 40 - CONV_W, D), F32)], axis=1)
    return dict(conv=fs, conv_rev=rev, b_pw1=b_pw1, b_pw2=fs[:, 34:35])


def run_step(x, target, P, ag, rs):
    T = x.shape[0]
    zero = jnp.zeros((1, 1, D), F32)
    nm, nf = P["norm_mix"], P["norm_ffn"]
    ag.forward("a0", [ag.token])
    W = ag.get("a0", [])
    sm = _conv_small(W["small"])
    h = x
    saved = []
    for l in range(2):
        xn, u, a = norm_mm_glu(h, nm, l, _slots(W[f"pw1_{l}"]), sm["b_pw1"], f"f_pw1_{l}")
        y, s = dwconv_ln_silu(a, sm["conv"], l, f"f_conv_{l}")
        b2 = sm["b_pw2"]
        if l == 0:
            b2 = _gate(b2, ag.forward("f0", [s]))
        h1 = mm_bias_res(s, _rows(W[f"pw2_{l}"]), b2, l, h, f"f_pw2_{l}")
        if l == 0:
            W.update(ag.get("f0", [h1]))
        xn2, gu, f = norm_mm_swiglu(h1, nf, l, _slots(W[f"up_{l}"]), f"f_up_{l}")
        nxt = "l1" if l == 0 else "l2"
        h2 = mm_bias_res(f, _rows(W[f"down_{l}"]), _gate(zero, ag.forward(nxt, [f])), 0, h1, f"f_down_{l}")
        W.update(ag.get(nxt, [h2]))
        saved.append(dict(h=h, xn=xn, u=u, a=a, y=y, s=s, h1=h1, xn2=xn2, gu=gu, f=f))
        h = h2
    h_kv = h
    kvn, kv = norm_mm(h, P["norm_kv"], 0, _rows(W["kv"]), "f_kv")
    kp = jnp.pad(_heads_major(kv[:, :N_KV * HD], N_KV), ((0, 0), (BLK, 0), (0, 0)))
    vp = jnp.pad(_heads_major(kv[:, N_KV * HD:], N_KV), ((0, 0), (BLK, 0), (0, 0)))
    kvt = jnp.pad(kv.T.reshape(2, N_KV, HD, T), ((0, 0), (0, 0), (0, 0), (BLK, 0)))
    kt, vt = kvt[0], kvt[1]
    bucket = _bucket_table()
    onehot = jnp.asarray(np.eye(N_BUCKETS, dtype=np.float32)[bucket])
    bias = jnp.einsum("qkb,bh->hkq", onehot, P["rel_bias"], precision=lax.Precision.HIGHEST)
    bias = bias.reshape(N_KV, GROUP, 2 * BLK, BLK).transpose(0, 2, 1, 3).reshape(1, N_KV, 2 * BLK, QW)
    bias = bias + jnp.asarray(band_mask())[:, None]
    for j in range(2):
        l = 2 + j
        xn, q = norm_mm(h, nm, l, _rows(W[f"wq_{j}"]), f"f_q_{j}", scale=HD ** -0.5)
        qh = q.T.reshape(N_KV, GROUP, HD, T)
        sink = jnp.broadcast_to(P["sinks"][j].reshape(N_KV, GROUP, 1), (N_KV, GROUP, BLK)).reshape(N_KV, 1, QW)
        oh = attn_fwd(qh, kp, vt, bias, sink, f"f_attn_{j}")
        attn = oh.reshape(N_HEADS * HD, T).T
        h1 = mm_bias_res(attn, _rows(W[f"wo_{j}"]), zero, 0, h, f"f_wo_{j}")
        xn2, gu, f = norm_mm_swiglu(h1, nf, l, _slots(W[f"up_{l}"]), f"f_up_{l}")
        zg = _gate(zero, ag.forward("l3", [f])) if j == 0 else zero
        h2 = mm_bias_res(f, _rows(W[f"down_{l}"]), zg, 0, h1, f"f_down_{l}")
        if j == 0:
            W.update(ag.get("l3", [h2]))
        saved.append(dict(h=h, xn=xn, qh=qh, oh=oh, sink=sink, attn=attn, h1=h1, xn2=xn2, gu=gu, f=f))
        h = h2

    dh, st_final = final_loss(h, P["norm_final"], target, "loss_head")

    S = dict(norm_ffn=[None] * 4, norm_mix=[None] * 4, conv=[None] * 2, taps=[None] * 2, b_pw1=[None] * 2,
             b_pw2=[None] * 2, sinks=[None] * 2)

    def ffn_bwd(dh, sv, l, nf, after=()):
        du = mmT_swiglu_bwd(dh, _rows(W[f"down_{l}"]), sv["gu"], f"b_down_{l}", after)
        gd = mm_dw(sv["f"], dh, f"w_down_{l}", 512, 1)
        gu = mm_dw(sv["xn2"], du, f"w_up_{l}", DFF // 2, 4)
        dh, dg = mmT_rmsbwd(du, _slots(W[f"up_{l}"]), sv["h1"], nf, l, dh, f"b_up_{l}")
        S["norm_ffn"][l] = dg
        return dh, {f"down_{l}": _gview(gd), f"up_{l}": _gview(gu)}

    dk = dv = dbias = None
    sent = []
    for j in (1, 0):
        l = 2 + j
        sv = saved[l]
        dh, grads = ffn_bwd(dh, sv, l, nf, sent)
        dattn = mmT(dh, _rows(W[f"wo_{j}"]), f"b_wo_{j}")
        grads[f"wo_{j}"] = _gview(mm_dw(sv["attn"], dh, f"w_wo_{j}", 512, 1))
        doh = dattn.T.reshape(N_KV, GROUP, HD, T)
        dqh, dkj, dvj, dbj, dsj = attn_bwd(sv["qh"], kp, kt, vp, bias, sv["sink"], sv["oh"], doh, f"b_attn_{j}")
        dq = dqh.reshape(N_HEADS * HD, T).T
        grads[f"wq_{j}"] = _gview(mm_dw(sv["xn"], dq, f"w_q_{j}", 512, 1))
        dh, dg = mmT_rmsbwd(dq, _rows(W[f"wq_{j}"])[None], sv["h"], nm, l, dh, f"b_q_{j}")
        S["norm_mix"][l] = dg
        S["sinks"][j] = jnp.sum(dsj.reshape(N_HEADS, BLK), axis=1)
        dk = dkj if dk is None else dk + dkj
        dv = dvj if dv is None else dv + dvj
        dbias = dbj if dbias is None else dbias + dbj
        if j == 1:
            sent = [rs.send("l3", grads)]

    dkv = jnp.concatenate([_heads_minor(dk[:, BLK:]), _heads_minor(dv[:, BLK:])], axis=1).astype(BF16)
    grads["kv"] = _gview(mm_dw(kvn, dkv, "w_kv", 512, 1))
    dh, dg = mmT_rmsbwd(dkv, _rows(W["kv"])[None], h_kv, P["norm_kv"], 0, dh, "b_kv")
    S["norm_kv"] = dg
    dbh = dbias.reshape(N_KV, 2 * BLK, GROUP, BLK)
    S["rel_bias"] = jnp.einsum("vkgq,qkb->bvg", dbh, onehot, precision=lax.Precision.HIGHEST).reshape(N_BUCKETS, N_HEADS)
    sent = [rs.send("l2", grads)]
    nf = _gate(nf, rs.reduce("l3", [dh]))

    for l in (1, 0):
        sv = saved[l]
        dh, grads = ffn_bwd(dh, sv, l, nf, sent)
        conv = sm["conv"]
        if l == 0:
            conv = _gate(conv, rs.send("f0", grads))
            grads = {}
        dy, st = mmT_lnbwd(dh, _rows(W[f"pw2_{l}"]), sv["y"], conv, l, f"b_pw2_{l}")
        g2, S["b_pw2"][l] = mm_dw(sv["s"], dh, f"w_pw2_{l}", 512, 1, colsum=True)
        du, dtaps = dwconv_glu_bwd(dy, sv["a"], sv["u"], sm["conv"], sm["conv_rev"], l, f"b_conv_{l}")
        S["conv"][l] = st
        S["taps"][l] = dtaps
        if l == 0:
            rs.finish("l2", [du])
            nm = _gate(nm, rs.reduce("l1", [du]))
        g1, S["b_pw1"][l] = mm_dw(sv["xn"], du, f"w_pw1_{l}", 512, 4, colsum=True)
        grads[f"pw2_{l}"], grads[f"pw1_{l}"] = _gview(g2), _gview(g1)
        dh, dg = mmT_rmsbwd(du, _slots(W[f"pw1_{l}"]), sv["h"], nm, l, dh, f"b_pw1_{l}")
        S["norm_mix"][l] = dg
        if l == 1:
            sent = [rs.send("l1", grads)]
            rs.finish("l3", [dh])
            nf = _gate(nf, rs.reduce("l2", [dh]))
    S["final"] = st_final
    return grads, dh, S


R_CONV = 37
R_SMALL = 88


def _pack_small(S):
    rows = []
    for l in range(2):
        rows += [S["taps"][l], S["conv"][l], S["b_pw2"][l], S["b_pw1"][l].reshape(2, D)]
    rows += S["norm_mix"] + S["norm_ffn"] + [S["norm_kv"], S["final"]]
    tail = jnp.concatenate([jnp.stack(S["sinks"]).reshape(-1), S["rel_bias"].reshape(-1)])
    rows += [jnp.pad(tail, (0, D - tail.shape[0]))[None]]
    v = jnp.concatenate(rows, axis=0)
    return jnp.pad(v, ((0, R_SMALL - v.shape[0]), (0, 0)))


def kernel(x, norm_mix, norm_ffn, conv_w_pw1, conv_b_pw1, conv_w_dw, conv_b_dw, conv_ln_g, conv_ln_b, conv_w_pw2, conv_b_pw2, norm_kv, w_kv, w_q, w_o, sinks, rel_bias, ffn_w_up, ffn_w_down, norm_final, loss_target, m_norm_mix, m_norm_ffn, m_conv_w_pw1, m_conv_b_pw1, m_conv_w_dw, m_conv_b_dw, m_conv_ln_g, m_conv_ln_b, m_conv_w_pw2, m_conv_b_pw2, m_norm_kv, m_w_kv, m_w_q, m_w_o, m_sinks, m_rel_bias, m_ffn_w_up, m_ffn_w_down, m_norm_final, v_norm_mix, v_norm_ffn, v_conv_w_pw1, v_conv_b_pw1, v_conv_w_dw, v_conv_b_dw, v_conv_ln_g, v_conv_ln_b, v_conv_w_pw2, v_conv_b_pw2, v_norm_kv, v_w_kv, v_w_q, v_w_o, v_sinks, v_rel_bias, v_ffn_w_up, v_ffn_w_down, v_norm_final):
    me = 2 * lax.axis_index("x") + lax.axis_index("y")
    weights = dict(norm_mix=norm_mix, norm_ffn=norm_ffn, conv_w_pw1=conv_w_pw1, conv_b_pw1=conv_b_pw1,
                   conv_w_dw=conv_w_dw, conv_b_dw=conv_b_dw, conv_ln_g=conv_ln_g, conv_ln_b=conv_ln_b,
                   conv_w_pw2=conv_w_pw2, conv_b_pw2=conv_b_pw2, norm_kv=norm_kv, w_kv=w_kv, w_q=w_q, w_o=w_o,
                   sinks=sinks, rel_bias=rel_bias, ffn_w_up=ffn_w_up, ffn_w_down=ffn_w_down, norm_final=norm_final)
    mom_m = dict(norm_mix=m_norm_mix, norm_ffn=m_norm_ffn, conv_w_pw1=m_conv_w_pw1, conv_b_pw1=m_conv_b_pw1,
                 conv_w_dw=m_conv_w_dw, conv_b_dw=m_conv_b_dw, conv_ln_g=m_conv_ln_g, conv_ln_b=m_conv_ln_b,
                 conv_w_pw2=m_conv_w_pw2, conv_b_pw2=m_conv_b_pw2, norm_kv=m_norm_kv, w_kv=m_w_kv, w_q=m_w_q,
                 w_o=m_w_o, sinks=m_sinks, rel_bias=m_rel_bias, ffn_w_up=m_ffn_w_up, ffn_w_down=m_ffn_w_down,
                 norm_final=m_norm_final)
    mom_v = dict(norm_mix=v_norm_mix, norm_ffn=v_norm_ffn, conv_w_pw1=v_conv_w_pw1, conv_b_pw1=v_conv_b_pw1,
                 conv_w_dw=v_conv_w_dw, conv_b_dw=v_conv_b_dw, conv_ln_g=v_conv_ln_g, conv_ln_b=v_conv_ln_b,
                 conv_w_pw2=v_conv_w_pw2, conv_b_pw2=v_conv_b_pw2, norm_kv=v_norm_kv, w_kv=v_w_kv, w_q=v_w_q,
                 w_o=v_w_o, sinks=v_sinks, rel_bias=v_rel_bias, ffn_w_up=v_ffn_w_up, ffn_w_down=v_ffn_w_down,
                 norm_final=v_norm_final)

    big = {"conv_w_pw1": "pw1", "conv_w_pw2": "pw2", "w_q": "wq", "w_o": "wo", "ffn_w_up": "up",
           "ffn_w_down": "down", "w_kv": "kv"}
    of_kind = {k: n for n, k in big.items()}

    def source(name):
        if name == "small":
            return jnp.concatenate(
                [conv_w_dw, conv_b_dw[:, None], conv_ln_g[:, None], conv_ln_b[:, None], conv_b_pw2[:, None],
                 conv_b_pw1.reshape(2, 2, 256), jnp.zeros((2, 3, 256), F32)], axis=1), None
        kind, _, l = name.partition("_")
        return weights[of_kind[kind]], (int(l) if l else None)

    ag = WeightGather(source, AG_GROUPS)
    rs = GradReduce({"pw1": (2, 512, 512), "pw2": (2, 128, D), "wq": (2, 128, D), "wo": (2, 128, D),
                     "up": (4, 512, DFF // 2), "down": (4, DFF // 8, D), "kv": (1, 128, 512)})

    P = dict(norm_mix=norm_mix[:, None], norm_ffn=norm_ffn[:, None], norm_kv=norm_kv[None, None],
             norm_final=norm_final[None], sinks=sinks, rel_bias=rel_bias)
    last, grad_x, S = run_step(x[0], loss_target[0], P, ag, rs)

    rs.finish("l1", [grad_x])
    small_flight, token = small_allreduce_start(_gate(_pack_small(S), rs.reduce("f0", [grad_x])), [])
    token = rs.send("c0", last, after=[token])
    delta, new_m, new_v, big_grads = {}, {}, {}, {}

    def update(n):
        shp = weights[n].shape
        r2 = (int(np.prod(shp[:-1])), shp[-1])
        g, d, nm, nv = adamw(weights[n].reshape(r2), rs.J[big[n]].reshape(r2), mom_m[n].reshape(r2),
                             mom_v[n].reshape(r2), f"adamw_{n}", copy_g=True)
        big_grads[n], delta[n], new_m[n], new_v[n] = g.reshape(shp), d.reshape(shp), nm.reshape(shp), nv.reshape(shp)

    rs.finish("f0", [token])
    for n in ("ffn_w_up", "ffn_w_down"):
        update(n)
    vsum = sum8(xchg_wait(small_flight, [delta["ffn_w_up"], delta["ffn_w_down"]])[1], "small_sum")

    col = lambda a: lax.dynamic_slice_in_dim(a, me * 256, 256, axis=-1)
    grads = {}
    for l in range(2):
        base = l * R_CONV
        grads.setdefault("conv_w_dw", []).append(col(vsum[base:base + 31]))
        grads.setdefault("conv_b_dw", []).append(col(vsum[base + 31]))
        grads.setdefault("conv_ln_g", []).append(col(vsum[base + 32]))
        grads.setdefault("conv_ln_b", []).append(col(vsum[base + 33]))
        grads.setdefault("conv_b_pw2", []).append(col(vsum[base + 34]))
        grads.setdefault("conv_b_pw1", []).append(
            lax.dynamic_slice_in_dim(vsum[base + 35:base + 37].reshape(2 * D), me * 512, 512, axis=0))
    grads = {k: jnp.stack(v) for k, v in grads.items()}
    base = 2 * R_CONV
    grads["norm_mix"] = vsum[base:base + 4]
    grads["norm_ffn"] = vsum[base + 4:base + 8]
    grads["norm_kv"] = vsum[base + 8]
    grads["norm_final"] = vsum[base + 9]
    loss = vsum[base + 10, 0]
    grads["sinks"] = vsum[base + 11, 0:32].reshape(2, 16)
    grads["rel_bias"] = vsum[base + 11, 32:32 + 512].reshape(32, 16)

    for n in weights:
        if n not in big:
            shp = weights[n].shape
            r2 = (int(np.prod(shp[:-1])), shp[-1])
            d, nm, nv = adamw(weights[n].reshape(r2), grads[n].reshape(r2), mom_m[n].reshape(r2),
                              mom_v[n].reshape(r2), f"adamw_{n}")
            delta[n], new_m[n], new_v[n] = d.reshape(shp), nm.reshape(shp), nv.reshape(shp)

    rs.reduce("c0", [vsum])
    for n in ("w_q", "w_o", "w_kv"):
        update(n)
    rs.finish("c0", [delta["w_kv"]])
    for n in ("conv_w_pw1", "conv_w_pw2"):
        update(n)
    grads.update(big_grads)

    order = list(weights)
    return (loss, grad_x[None], *[grads[n] for n in order], *[delta[n] for n in order],
            *[new_m[n] for n in order], *[new_v[n] for n in order])
```

```python
import math

import numpy as np
import jax
import jax.numpy as jnp
from jax import lax
from jax.experimental import pallas as pl
from jax.experimental.pallas import tpu as pltpu

F32 = jnp.float32
BF16 = jnp.bfloat16
MESH = pl.DeviceIdType.MESH

D = 1024
DFF = 2816
N_HEADS = 16
N_KV = 4
GROUP = 4
HD = 64
BLK = 128
CONV_W = 31
HALO = 32
N_BUCKETS = 32
MAX_DISTANCE = 128
EPS = 1e-6
NEG_INF = -1e30
TM = 512
TCV = 256
VMEM_LIMIT = 56 * 2 ** 20

ADAM_LR, ADAM_B1, ADAM_B2, ADAM_EPS, ADAM_WD, ADAM_STEP = 0.001, 0.9, 0.999, 1e-08, 0.01, 10


def _cp(*sem):
    return pltpu.CompilerParams(dimension_semantics=sem, vmem_limit_bytes=VMEM_LIMIT)


def _sigmoid(x):
    return 1.0 / (1.0 + jnp.exp(-x))


def _row(tm, n):
    return pl.BlockSpec((tm, n), lambda i: (i, 0))


def _const(shape):
    nd = len(shape)
    return pl.BlockSpec(shape, lambda i: (0,) * nd)


def _weight(shape):
    nd = len(shape)
    return pl.BlockSpec(shape, lambda i: (0,) * nd, pipeline_mode=pl.Buffered(1))


def _layer(shape, l):
    nd = len(shape)
    return pl.BlockSpec((None,) + tuple(shape), lambda i: (l,) + (0,) * nd)


def _dot(a, b):
    return jnp.dot(a, b, preferred_element_type=F32)


def _dot_nt(a, b):
    return lax.dot_general(a, b, (((1,), (1,)), ((), ())), preferred_element_type=F32)


def _dot_tn(a, b):
    return lax.dot_general(a, b, (((0,), (0,)), ((), ())), preferred_element_type=F32)


def _rms(x):
    return lax.rsqrt(jnp.mean(x * x, axis=-1, keepdims=True) + EPS)


def norm_mm_glu(h, g, l, w, b, name):
    T = h.shape[0]
    ns = w.shape[-1]

    def body(h_ref, g_ref, w_ref, b_ref, xn_ref, u_ref, a_ref):
        x = h_ref[...]
        xn = (x * _rms(x) * g_ref[...]).astype(BF16)
        xn_ref[...] = xn
        for s in range(2):
            lo, hi = s * ns, (s + 1) * ns
            u1 = _dot(xn, w_ref[s]) + b_ref[:, lo:hi]
            u2 = _dot(xn, w_ref[2 + s]) + b_ref[:, D + lo:D + hi]
            u_ref[:, lo:hi] = u1.astype(BF16)
            u_ref[:, D + lo:D + hi] = u2.astype(BF16)
            a_ref[:, lo:hi] = (u1 * _sigmoid(u2)).astype(BF16)

    return pl.pallas_call(
        body, name=name, grid=(T // TM,),
        in_specs=[_row(TM, D), _layer((1, D), l), _weight((4, D, ns)), _layer((1, 2 * D), l)],
        out_specs=[_row(TM, D), _row(TM, 2 * D), _row(TM, D)],
        out_shape=[jax.ShapeDtypeStruct((T, D), BF16), jax.ShapeDtypeStruct((T, 2 * D), BF16),
                   jax.ShapeDtypeStruct((T, D), BF16)],
        compiler_params=_cp("parallel"),
    )(h, g, w, b)


SUB = 8


def _make_shifts(sh):
    n = TCV + HALO - SUB
    for r in range(1, SUB):
        for r0 in range(0, n, 40):
            sh[r, r0:r0 + 40, :] = sh[0, pl.ds(r + r0, 40), :]


def _shifted(sh, off, rows, cols):
    return sh[off % SUB, pl.ds(off - off % SUB, rows), cols]


def _conv_taps(sh, w_ref, out_ref, first):
    RB, LB = 32, 512
    for r0 in range(0, TCV, RB):
        for c0 in range(0, out_ref.shape[1], LB):
            acc = jnp.zeros((RB, LB), F32)
            for k in range(CONV_W):
                acc = acc + w_ref[k:k + 1, c0:c0 + LB] * _shifted(sh, first + k + r0, RB, slice(c0, c0 + LB))
            out_ref[r0:r0 + RB, c0:c0 + LB] = acc


def dwconv_ln_silu(a, sm, l, name):
    T = a.shape[0]
    nb = TCV // HALO

    def body(cur_ref, prev_ref, sm_ref, y_ref, s_ref, sh, yb):
        i = pl.program_id(0)
        sh[0, 0:HALO, :] = jnp.where(i > 0, prev_ref[...].astype(F32), 0.0)
        sh[0, HALO:HALO + TCV, :] = cur_ref[...].astype(F32)
        _make_shifts(sh)
        _conv_taps(sh, sm_ref, yb, HALO - (CONV_W - 1))
        y = yb[...] + sm_ref[31:32, :]
        y_ref[...] = y.astype(BF16)
        mu = jnp.mean(y, axis=-1, keepdims=True)
        yc = y - mu
        rstd = lax.rsqrt(jnp.mean(yc * yc, axis=-1, keepdims=True) + EPS)
        z = yc * rstd * sm_ref[32:33, :] + sm_ref[33:34, :]
        s_ref[...] = (z * _sigmoid(z)).astype(BF16)

    return pl.pallas_call(
        body, name=name, grid=(T // TCV,),
        in_specs=[_row(TCV, D), pl.BlockSpec((HALO, D), lambda i: (jnp.maximum(i * nb - 1, 0), 0)),
                  _layer((40, D), l)],
        out_specs=[_row(TCV, D), _row(TCV, D)],
        out_shape=[jax.ShapeDtypeStruct((T, D), BF16), jax.ShapeDtypeStruct((T, D), BF16)],
        scratch_shapes=[pltpu.VMEM((SUB, TCV + HALO, D), F32), pltpu.VMEM((TCV, D), F32)],
        compiler_params=_cp("parallel"),
    )(a, a, sm)


def mm_bias_res(xb, w, b, bl, res, name):
    T, K = xb.shape

    def body(x_ref, w_ref, b_ref, r_ref, o_ref):
        o_ref[...] = _dot(x_ref[...], w_ref[...]) + b_ref[...] + r_ref[...]

    return pl.pallas_call(
        body, name=name, grid=(T // TM,),
        in_specs=[_row(TM, K), _weight((K, D)), _layer((1, D), bl), _row(TM, D)],
        out_specs=_row(TM, D), out_shape=jax.ShapeDtypeStruct((T, D), F32),
        compiler_params=_cp("parallel"),
    )(xb, w, b, res)


def norm_mm_swiglu(h, g, l, w, name):
    T = h.shape[0]
    ns = w.shape[-1]

    def body(h_ref, g_ref, w_ref, xn_ref, gu_ref, f_ref):
        x = h_ref[...]
        xn = (x * _rms(x) * g_ref[...]).astype(BF16)
        xn_ref[...] = xn
        for s in range(2):
            lo, hi = s * ns, (s + 1) * ns
            gate = _dot(xn, w_ref[s])
            up = _dot(xn, w_ref[2 + s])
            gu_ref[:, lo:hi] = gate.astype(BF16)
            gu_ref[:, DFF + lo:DFF + hi] = up.astype(BF16)
            f_ref[:, lo:hi] = (gate * _sigmoid(gate) * up).astype(BF16)

    return pl.pallas_call(
        body, name=name, grid=(T // TM,),
        in_specs=[_row(TM, D), _layer((1, D), l), _weight((4, D, ns))],
        out_specs=[_row(TM, D), _row(TM, 2 * DFF), _row(TM, DFF)],
        out_shape=[jax.ShapeDtypeStruct((T, D), BF16), jax.ShapeDtypeStruct((T, 2 * DFF), BF16),
                   jax.ShapeDtypeStruct((T, DFF), BF16)],
        compiler_params=_cp("parallel"),
    )(h, g, w)


def norm_mm(h, g, gl, w, name, scale=1.0):
    T = h.shape[0]
    N = w.shape[-1]

    def body(h_ref, g_ref, w_ref, xn_ref, o_ref):
        x = h_ref[...]
        xn = (x * _rms(x) * g_ref[...]).astype(BF16)
        xn_ref[...] = xn
        o_ref[...] = (_dot(xn, w_ref[...]) * scale).astype(BF16)

    return pl.pallas_call(
        body, name=name, grid=(T // TM,),
        in_specs=[_row(TM, D), _layer((1, D), gl), _weight((D, N))],
        out_specs=[_row(TM, D), _row(TM, N)],
        out_shape=[jax.ShapeDtypeStruct((T, D), BF16), jax.ShapeDtypeStruct((T, N), BF16)],
        compiler_params=_cp("parallel"),
    )(h, g, w)


QB = 16
QW = GROUP * BLK


def band_mask():
    qi = np.arange(QW)[None, :] % BLK
    kj = np.arange(2 * BLK)[:, None]
    band = ((kj < BLK) & (kj > qi)) | ((kj >= BLK) & (kj - BLK <= qi))
    first = band & (kj >= BLK)
    return np.where(np.stack([first, band]), 0.0, NEG_INF).astype(np.float32)


def _softmax_cols(s, sink):
    m = jnp.maximum(jnp.max(s, axis=0, keepdims=True), sink)
    p = jnp.exp(s - m)
    es = jnp.exp(sink - m)
    inv = 1.0 / (jnp.sum(p, axis=0, keepdims=True) + es)
    return p, inv, es


def _attn_specs(T):
    W = QB * BLK
    qspec = pl.BlockSpec((None, GROUP, HD, W), lambda kv, n: (kv, 0, 0, n))
    kspec = pl.BlockSpec((None, T + BLK, HD), lambda kv, n: (kv, 0, 0))
    ktspec = [pl.BlockSpec((None, HD, W), lambda kv, n: (kv, 0, n)),
              pl.BlockSpec((None, HD, BLK), lambda kv, n: (kv, 0, (n + 1) * QB))]
    bspec = pl.BlockSpec((2, None, 2 * BLK, QW), lambda kv, n: (0, kv, 0, 0))
    sspec = pl.BlockSpec((None, 1, QW), lambda kv, n: (kv, 0, 0))
    return qspec, kspec, ktspec, bspec, sspec


def _attn_block(n, b):
    blk = n * QB + b
    rows = pl.ds(pl.multiple_of(blk * BLK, BLK), 2 * BLK)
    return rows, (jnp.minimum(blk, 1) if b == 0 else 1)


def _band_cols(main_ref, tail_ref, b):
    if b < QB - 1:
        return main_ref[:, b * BLK:(b + 2) * BLK]
    return jnp.concatenate([main_ref[:, b * BLK:], tail_ref[...]], axis=1)


def _heads_side_by_side(ref, qs):
    return jnp.concatenate([ref[g, :, qs] for g in range(GROUP)], axis=1)


def attn_fwd(q, kp, vt, bias, sink, name):
    T = q.shape[3]
    qspec, kspec, ktspec, bspec, sspec = _attn_specs(T)

    def body(q_ref, k_ref, vt_ref, vtt_ref, b_ref, s_ref, o_ref, pb):
        n = pl.program_id(1)

        def scores(b):
            return _dot(k_ref[_attn_block(n, b)[0], :], _heads_side_by_side(q_ref, slice(b * BLK, (b + 1) * BLK)))

        st_next = scores(0)
        for b in range(QB):
            rows, table = _attn_block(n, b)
            qs = slice(b * BLK, (b + 1) * BLK)
            st = st_next
            if b + 1 < QB:
                st_next = scores(b + 1)
            for g in range(GROUP):
                hs = slice(g * BLK, (g + 1) * BLK)
                p, inv, _ = _softmax_cols(st[:, hs] + b_ref[table, :, hs], s_ref[:, hs])
                pb[:, hs] = (p * inv).astype(BF16)
            ot = _dot(_band_cols(vt_ref, vtt_ref, b), pb[...])
            for g in range(GROUP):
                o_ref[g, :, qs] = ot[:, g * BLK:(g + 1) * BLK].astype(BF16)

    return pl.pallas_call(
        body, name=name, grid=(N_KV, T // (QB * BLK)),
        in_specs=[qspec, kspec, *ktspec, bspec, sspec], out_specs=qspec,
        out_shape=jax.ShapeDtypeStruct((N_KV, GROUP, HD, T), BF16),
        scratch_shapes=[pltpu.VMEM((2 * BLK, QW), BF16)],
        compiler_params=_cp("parallel", "parallel"),
    )(q, kp, vt, vt, bias, sink)


def attn_bwd(q, kp, kt, vp, bias, sink, o, do, name):
    T = q.shape[3]
    qspec, kspec, ktspec, bspec, sspec = _attn_specs(T)

    def body(q_ref, k_ref, kt_ref, ktt_ref, v_ref, b_ref, s_ref, o_ref, do_ref,
             dq_ref, dk_ref, dv_ref, db_ref, ds_ref, pb, dsb):
        n = pl.program_id(1)

        @pl.when(n == 0)
        def _():
            dk_ref[...] = jnp.zeros_like(dk_ref)
            dv_ref[...] = jnp.zeros_like(dv_ref)
            db_ref[...] = jnp.zeros_like(db_ref)
            ds_ref[...] = jnp.zeros_like(ds_ref)

        def products(b):
            rows = _attn_block(n, b)[0]
            qs = slice(b * BLK, (b + 1) * BLK)
            q4, do4 = _heads_side_by_side(q_ref, qs), _heads_side_by_side(do_ref, qs)
            return q4, do4, _dot(k_ref[rows, :], q4), _dot(v_ref[rows, :], do4)

        ahead = products(0)
        for b in range(QB):
            rows, table = _attn_block(n, b)
            qs = slice(b * BLK, (b + 1) * BLK)
            q4, do4, st, dpt = ahead
            if b + 1 < QB:
                ahead = products(b + 1)
            for g in range(GROUP):
                hs = slice(g * BLK, (g + 1) * BLK)
                p, inv, es = _softmax_cols(st[:, hs] + b_ref[table, :, hs], s_ref[:, hs])
                probs = p * inv
                delta = jnp.sum(do_ref[g, :, qs].astype(F32) * o_ref[g, :, qs].astype(F32), axis=0, keepdims=True)
                dS = probs * (dpt[:, hs] - delta)
                ds_ref[:, hs] += -(es * inv) * delta
                db_ref[:, hs] += dS
                pb[:, hs] = probs.astype(BF16)
                dsb[:, hs] = dS.astype(BF16)
            dqt = _dot(_band_cols(kt_ref, ktt_ref, b), dsb[...]) * (HD ** -0.5)
            for g in range(GROUP):
                dq_ref[g, :, qs] = dqt[:, g * BLK:(g + 1) * BLK].astype(BF16)
            dk_ref[rows, :] += _dot_nt(dsb[...], q4)
            dv_ref[rows, :] += _dot_nt(pb[...], do4)

    kout = pl.BlockSpec((None, T + BLK, HD), lambda kv, n: (kv, 0, 0))
    dbspec = pl.BlockSpec((None, 2 * BLK, QW), lambda kv, n: (kv, 0, 0))
    return pl.pallas_call(
        body, name=name, grid=(N_KV, T // (QB * BLK)),
        in_specs=[qspec, kspec, *ktspec, kspec, bspec, sspec, qspec, qspec],
        out_specs=[qspec, kout, kout, dbspec, sspec],
        out_shape=[jax.ShapeDtypeStruct((N_KV, GROUP, HD, T), BF16),
                   jax.ShapeDtypeStruct((N_KV, T + BLK, HD), F32), jax.ShapeDtypeStruct((N_KV, T + BLK, HD), F32),
                   jax.ShapeDtypeStruct((N_KV, 2 * BLK, QW), F32), jax.ShapeDtypeStruct((N_KV, 1, QW), F32)],
        scratch_shapes=[pltpu.VMEM((2 * BLK, QW), BF16), pltpu.VMEM((2 * BLK, QW), BF16)],
        compiler_params=_cp("parallel", "arbitrary"),
    )(q, kp, kt, kt, vp, bias, sink, o, do)


def final_loss(h, g, target, name):
    T = h.shape[0]

    def body(h_ref, g_ref, t_ref, dh_ref, st_ref):
        i = pl.program_id(0)

        @pl.when(i == 0)
        def _():
            st_ref[...] = jnp.zeros_like(st_ref)

        x = h_ref[...]
        r = _rms(x)
        xh = x * r
        e = xh * g_ref[...] - t_ref[...]
        loss = 0.5 * jnp.sum(jnp.mean(e * e, axis=-1, keepdims=True))
        dy = e * (1.0 / D)
        st_ref[0:1, :] += jnp.sum(dy * xh, axis=0, keepdims=True)
        lane = lax.broadcasted_iota(jnp.int32, (1, D), 1)
        st_ref[1:2, :] += jnp.where(lane == 0, loss, 0.0)
        dxh = dy * g_ref[...]
        dh_ref[...] = r * (dxh - xh * jnp.mean(dxh * xh, axis=-1, keepdims=True))

    return pl.pallas_call(
        body, name=name, grid=(T // TM,),
        in_specs=[_row(TM, D), _const((1, D)), _row(TM, D)],
        out_specs=[_row(TM, D), _const((2, D))],
        out_shape=[jax.ShapeDtypeStruct((T, D), F32), jax.ShapeDtypeStruct((2, D), F32)],
        compiler_params=_cp("arbitrary"),
    )(h, g, target)


def mm_dw(x, dy, name, tn, slots, colsum=False):
    T, K = x.shape
    split = dy.ndim == 3
    N = dy.shape[-1] * (2 if split else 1)
    tt = min(T, 2048 if K <= 1024 else 1024)
    nt = T // tt
    ns = N // slots
    per = ns // tn

    def body(x_ref, dy_ref, *rest):
        if colsum:
            dw_ref, cs_ref, acc, cacc = rest
        else:
            dw_ref, acc = rest
        t = pl.program_id(1)

        @pl.when(t == 0)
        def _():
            acc[...] = jnp.zeros_like(acc)
            if colsum:
                cacc[...] = jnp.zeros_like(cacc)

        dyv = dy_ref[...]
        acc[...] += _dot_tn(x_ref[...].astype(BF16), dyv.astype(BF16))
        if colsum:
            cacc[...] += jnp.sum(dyv.astype(F32), axis=0, keepdims=True)

        @pl.when(t == nt - 1)
        def _():
            dw_ref[...] = acc[...].astype(BF16)
            if colsum:
                cs_ref[...] = cacc[...]

    if split:
        half = N // 2 // tn
        dy_spec = pl.BlockSpec((None, tt, tn), lambda j, t: (j // half, t, j % half))
    else:
        dy_spec = pl.BlockSpec((tt, tn), lambda j, t: (t, j))
    out_specs = [pl.BlockSpec((None, K, tn), lambda j, t: (j // per, 0, j % per))]
    out_shape = [jax.ShapeDtypeStruct((slots, K, ns), BF16)]
    scratch = [pltpu.VMEM((K, tn), F32)]
    if colsum:
        out_specs.append(pl.BlockSpec((1, tn), lambda j, t: (0, j)))
        out_shape.append(jax.ShapeDtypeStruct((1, N), F32))
        scratch.append(pltpu.VMEM((1, tn), F32))
    res = pl.pallas_call(
        body, name=name, grid=(N // tn, nt),
        in_specs=[pl.BlockSpec((tt, K), lambda j, t: (t, 0)), dy_spec],
        out_specs=out_specs, out_shape=out_shape, scratch_shapes=scratch,
        compiler_params=_cp("parallel", "arbitrary"),
    )(x, dy)
    return tuple(res) if colsum else res[0]


def mmT_swiglu_bwd(dh, w, gu, name, after=()):
    T = dh.shape[0]
    cw = 256

    def body(dh_ref, w_ref, gu_ref, *rest):
        du_ref = rest[-1]
        dhb = dh_ref[...].astype(BF16)
        ahead = _dot_nt(dhb, w_ref[0:cw, :])
        for lo in range(0, DFF, cw):
            hi = lo + cw
            df = ahead
            if hi < DFF:
                ahead = _dot_nt(dhb, w_ref[hi:hi + cw, :])
            gate = gu_ref[:, lo:hi].astype(F32)
            up = gu_ref[:, DFF + lo:DFF + hi].astype(F32)
            sg = _sigmoid(gate)
            silu = gate * sg
            du_ref[:, lo:hi] = (df * (up * (sg + silu * (1.0 - sg)))).astype(BF16)
            du_ref[:, DFF + lo:DFF + hi] = (df * silu).astype(BF16)

    return pl.pallas_call(
        body, name=name, grid=(T // TM,),
        in_specs=[_row(TM, D), _weight((DFF, D)), _row(TM, 2 * DFF)] + [ANY] * len(after),
        out_specs=_row(TM, 2 * DFF), out_shape=jax.ShapeDtypeStruct((T, 2 * DFF), BF16),
        compiler_params=_cp("parallel"),
    )(dh, w, gu, *after)


def mmT_rmsbwd(du, w, h, g, gl, dh_in, name):
    split = du.ndim == 3
    T = du.shape[-2]
    N = du.shape[-1] * (2 if split else 1)
    slots = w.shape[0]
    ns = N // slots

    RH = TM // 2

    def piece(du_ref, s, rows):
        if split:
            per = slots // 2
            return du_ref[s // per, rows, (s % per) * ns:(s % per + 1) * ns]
        return du_ref[rows, s * ns:(s + 1) * ns]

    def body(du_ref, w_ref, h_ref, g_ref, di_ref, dh_ref, dg_ref):
        i = pl.program_id(0)

        @pl.when(i == 0)
        def _():
            dg_ref[...] = jnp.zeros_like(dg_ref)

        def products(k):
            rows = slice(k * RH, (k + 1) * RH)
            dxn = _dot_nt(piece(du_ref, 0, rows), w_ref[0])
            for s in range(1, slots):
                dxn = dxn + _dot_nt(piece(du_ref, s, rows), w_ref[s])
            return dxn

        ahead = products(0)
        for k in range(TM // RH):
            rows = slice(k * RH, (k + 1) * RH)
            dxn = ahead
            if (k + 1) * RH < TM:
                ahead = products(k + 1)
            x = h_ref[rows, :]
            r = _rms(x)
            xh = x * r
            dg_ref[0:1, :] += jnp.sum(dxn * xh, axis=0, keepdims=True)
            dxh = dxn * g_ref[...]
            dh_ref[rows, :] = di_ref[rows, :] + r * (dxh - xh * jnp.mean(dxh * xh, axis=-1, keepdims=True))

    return pl.pallas_call(
        body, name=name, grid=(T // TM,),
        in_specs=[pl.BlockSpec((2, TM, N // 2), lambda i: (0, i, 0)) if split else _row(TM, N),
                  _weight((slots, D, ns)), _row(TM, D), _layer((1, D), gl), _row(TM, D)],
        out_specs=[_row(TM, D), _const((1, D))],
        out_shape=[jax.ShapeDtypeStruct((T, D), F32), jax.ShapeDtypeStruct((1, D), F32)],
        compiler_params=_cp("arbitrary"),
    )(du, w, h, g, dh_in)


def mmT(dh, w, name):
    T = dh.shape[0]
    N = w.shape[0]

    def body(dh_ref, w_ref, o_ref):
        o_ref[...] = _dot_nt(dh_ref[...].astype(BF16), w_ref[...]).astype(BF16)

    return pl.pallas_call(
        body, name=name, grid=(T // TM,),
        in_specs=[_row(TM, D), _weight((N, D))],
        out_specs=_row(TM, N), out_shape=jax.ShapeDtypeStruct((T, N), BF16),
        compiler_params=_cp("parallel"),
    )(dh, w)


def mmT_lnbwd(dh, w, y, sm, l, name):
    T = dh.shape[0]

    def body(dh_ref, w_ref, y_ref, sm_ref, dy_ref, st_ref):
        i = pl.program_id(0)

        @pl.when(i == 0)
        def _():
            st_ref[...] = jnp.zeros_like(st_ref)

        ds = _dot_nt(dh_ref[...].astype(BF16), w_ref[...])
        y = y_ref[...].astype(F32)
        mu = jnp.mean(y, axis=-1, keepdims=True)
        yc = y - mu
        rstd = lax.rsqrt(jnp.mean(yc * yc, axis=-1, keepdims=True) + EPS)
        xh = yc * rstd
        gam = sm_ref[32:33, :]
        z = xh * gam + sm_ref[33:34, :]
        sg = _sigmoid(z)
        dz = ds * sg * (1.0 + z * (1.0 - sg))
        st_ref[1:2, :] += jnp.sum(dz * xh, axis=0, keepdims=True)
        st_ref[2:3, :] += jnp.sum(dz, axis=0, keepdims=True)
        dxh = dz * gam
        dy = rstd * (dxh - jnp.mean(dxh, axis=-1, keepdims=True) - xh * jnp.mean(dxh * xh, axis=-1, keepdims=True))
        st_ref[0:1, :] += jnp.sum(dy, axis=0, keepdims=True)
        dy_ref[...] = dy.astype(BF16)

    return pl.pallas_call(
        body, name=name, grid=(T // TM,),
        in_specs=[_row(TM, D), _weight((D, D)), _row(TM, D), _layer((40, D), l)],
        out_specs=[_row(TM, D), _const((3, D))],
        out_shape=[jax.ShapeDtypeStruct((T, D), BF16), jax.ShapeDtypeStruct((3, D), F32)],
        compiler_params=_cp("arbitrary"),
    )(dh, w, y, sm)


CH = 512


def dwconv_glu_bwd(dy, a, u, sm, smrev, l, name):
    T = dy.shape[0]
    nr, nc = T // TCV, D // CH
    nb = TCV // HALO
    last = T // HALO - 1

    def body(dy_ref, dyn_ref, a_ref, ap_ref, u1_ref, u2_ref, sm_ref, rev_ref, du_ref, dw_ref, shd, sha, da):
        i = pl.program_id(0)
        r = i % nr

        @pl.when(r == 0)
        def _():
            dw_ref[...] = jnp.zeros_like(dw_ref)

        shd[0, 0:TCV, :] = dy_ref[...].astype(F32)
        shd[0, TCV:TCV + HALO, :] = jnp.where(r < nr - 1, dyn_ref[...].astype(F32), 0.0)
        sha[0, 0:HALO, :] = jnp.where(r > 0, ap_ref[...].astype(F32), 0.0)
        sha[0, HALO:HALO + TCV, :] = a_ref[...].astype(F32)
        _make_shifts(shd)
        _make_shifts(sha)
        _conv_taps(shd, rev_ref, da, 0)
        for kg in range(0, CONV_W, SUB):
            taps = range(kg, min(kg + SUB, CONV_W))
            part = [jnp.zeros((SUB, CH), F32) for _ in taps]
            for r0 in range(0, TCV, SUB):
                d = shd[0, r0:r0 + SUB, :]
                for j, k in enumerate(taps):
                    part[j] = part[j] + d * _shifted(sha, HALO - (CONV_W - 1) + k + r0, SUB, slice(None))
            for j, k in enumerate(taps):
                dw_ref[k:k + 1, :] += jnp.sum(part[j], axis=0, keepdims=True)
        dav = da[...]
        u1 = u1_ref[...].astype(F32)
        sg = _sigmoid(u2_ref[...].astype(F32))
        du_ref[0] = (dav * sg).astype(BF16)
        du_ref[1] = (dav * u1 * sg * (1.0 - sg)).astype(BF16)

    tile = lambda i: (i % nr, i // nr)
    in_specs = [pl.BlockSpec((TCV, CH), tile),
                pl.BlockSpec((HALO, CH), lambda i: (jnp.minimum((i % nr + 1) * nb, last), i // nr)),
                pl.BlockSpec((TCV, CH), tile),
                pl.BlockSpec((HALO, CH), lambda i: (jnp.maximum((i % nr) * nb - 1, 0), i // nr)),
                pl.BlockSpec((TCV, CH), tile), pl.BlockSpec((TCV, CH), lambda i: (i % nr, nc + i // nr)),
                pl.BlockSpec((None, 40, CH), lambda i: (l, 0, i // nr)),
                pl.BlockSpec((None, 40, CH), lambda i: (l, 0, i // nr))]
    return pl.pallas_call(
        body, name=name, grid=(nr * nc,), in_specs=in_specs,
        out_specs=[pl.BlockSpec((2, TCV, CH), lambda i: (0, i % nr, i // nr)),
                   pl.BlockSpec((CONV_W, CH), lambda i: (0, i // nr))],
        out_shape=[jax.ShapeDtypeStruct((2, T, D), BF16), jax.ShapeDtypeStruct((CONV_W, D), F32)],
        scratch_shapes=[pltpu.VMEM((SUB, TCV + HALO, CH), F32), pltpu.VMEM((SUB, TCV + HALO, CH), F32),
                        pltpu.VMEM((TCV, CH), F32)],
        compiler_params=_cp("arbitrary"),
    )(dy, dy, a, a, u, u, sm, smrev)


def _rows_tile(R):
    for t in (512, 256, 128, 64, 32, 16, 8):
        if R % t == 0:
            return t
    return R


def add8_into(J, l, g, others, where, name):
    R, C = g.shape[2:]
    tr = R // 2

    def body(w_ref, g_ref, x_ref, j_in, j_ref):
        acc = g_ref[...].astype(F32)
        for k in range(7):
            acc = acc + x_ref[k].astype(F32)
        j_ref[...] = acc

    return pl.pallas_call(
        body, name=name,
        grid_spec=pltpu.PrefetchScalarGridSpec(
            num_scalar_prefetch=1, grid=(R // tr,),
            in_specs=[pl.BlockSpec((None, None, tr, C), lambda i, w: (w[0], w[1], i, 0)),
                      pl.BlockSpec((7, tr, C), lambda i, w: (0, i, 0)), ANY],
            out_specs=pl.BlockSpec((None, None, tr, C), lambda i, w: (l, w[1], i, 0))),
        out_shape=jax.ShapeDtypeStruct(J.shape, F32), input_output_aliases={3: 0},
        compiler_params=_cp("parallel"),
    )(where, g, others, J)


def adamw(w, g, m, v, name, copy_g=False):
    R, C = w.shape
    tr = _rows_tile(R)

    def body(w_ref, g_ref, m_ref, v_ref, *outs):
        d_ref, nm_ref, nv_ref = outs[-3:]
        gv = g_ref[...]
        if copy_g:
            outs[0][...] = gv
        nm = ADAM_B1 * m_ref[...] + (1.0 - ADAM_B1) * gv
        nv = ADAM_B2 * v_ref[...] + (1.0 - ADAM_B2) * (gv * gv)
        m_hat = nm / (1.0 - ADAM_B1 ** ADAM_STEP)
        v_hat = nv / (1.0 - ADAM_B2 ** ADAM_STEP)
        d_ref[...] = -ADAM_LR * (m_hat / (jnp.sqrt(v_hat) + ADAM_EPS) + ADAM_WD * w_ref[...])
        nm_ref[...] = nm
        nv_ref[...] = nv

    sd = jax.ShapeDtypeStruct((R, C), F32)
    n_out = 4 if copy_g else 3
    return pl.pallas_call(
        body, name=name, grid=(R // tr,),
        in_specs=[_row(tr, C)] * 4, out_specs=[_row(tr, C)] * n_out, out_shape=[sd] * n_out,
        compiler_params=_cp("parallel"),
    )(w, g, m, v)


ANY = pl.BlockSpec(memory_space=pl.ANY)
HBM = pl.BlockSpec(memory_space=pltpu.HBM)
SEM = pl.BlockSpec(memory_space=pltpu.SEMAPHORE)
EFFECT = pltpu.SideEffectType.DATAFLOW_SIDE_EFFECTING


def _place():
    x, y, c = lax.axis_index("x"), lax.axis_index("y"), lax.axis_index("c")
    chips = [(1 - x, y), (x, 1 - y), (1 - x, 1 - y)]
    return x, y, c, chips


def _copy(src, dst, send, recv, k, to):
    return pltpu.make_async_remote_copy(src_ref=src, dst_ref=dst, send_sem=send.at[k], recv_sem=recv.at[k],
                                        device_id=to, device_id_type=MESH)


def xchg_start(name, bufs, plan, n, after=()):
    nb = len(bufs)

    na = len(after)

    def body(*refs):
        send, recv, token = refs[nb + na], refs[nb + na + 1], refs[-1]
        for k, (src, dst, to) in enumerate(plan(refs[:nb])):
            _copy(src, dst, send, recv, k, to).start()
        token[...] = jnp.zeros_like(token)

    outs = pl.pallas_call(
        body, name=name,
        out_shape=(pltpu.SemaphoreType.DMA((n,)), pltpu.SemaphoreType.DMA((n,)),
                   *[pltpu.HBM(b.shape, b.dtype) for b in bufs], jax.ShapeDtypeStruct((8, 128), F32)),
        in_specs=[HBM] * nb + [ANY] * na,
        out_specs=(SEM, SEM, *[HBM] * nb, pl.BlockSpec(memory_space=pltpu.VMEM)),
        input_output_aliases={i: 2 + i for i in range(nb)},
        compiler_params=pltpu.CompilerParams(has_side_effects=EFFECT),
    )(*[pltpu.with_memory_space_constraint(b, pltpu.HBM) for b in bufs], *after)
    return dict(name=name, send=outs[0], recv=outs[1], bufs=list(outs[2:2 + nb]), plan=plan), outs[-1]


def xchg_wait(flight, after):
    bufs, plan = flight["bufs"], flight["plan"]
    nb = len(bufs)

    def body(*refs):
        send, recv = refs[nb], refs[nb + 1]
        for k, (src, dst, to) in enumerate(plan(refs[:nb])):
            cp = _copy(src, dst, send, recv, k, to)
            cp.wait_send()
            cp.wait_recv()

    outs = pl.pallas_call(
        body, name=flight["name"] + "_wait",
        out_shape=tuple(pltpu.HBM(b.shape, b.dtype) for b in bufs),
        in_specs=[HBM] * nb + [SEM, SEM] + [ANY] * len(after),
        out_specs=tuple([HBM] * nb), input_output_aliases={i: i for i in range(nb)},
        compiler_params=pltpu.CompilerParams(has_side_effects=EFFECT),
    )(*bufs, flight["send"], flight["recv"], *after)
    return list(outs)


def _flip(k, x, y, c):
    return ((1 - x) if k & 4 else x, (1 - y) if k & 2 else y, (1 - c) if k & 1 else c)


def cast_into_slot(srcs, name, after):
    me = (2 * lax.axis_index("x") + lax.axis_index("y")).astype(jnp.int32).reshape(1)
    ns = len(srcs)

    def body(me_ref, *refs):
        outs = refs[ns + len(after):]
        for t in range(ns):
            outs[t][...] = refs[t][...].astype(outs[t].dtype).reshape(outs[t].shape)

    in_specs, out_specs, out_shape = [], [], []
    for arr, l in srcs:
        if l is None:
            in_specs.append(pl.BlockSpec(arr.shape, lambda i, w, nd=arr.ndim: (0,) * nd))
            a2, b, dt = (arr.shape[0] // 2, arr.shape[1], BF16) if arr.ndim == 2 else (arr.shape[1], arr.shape[2], F32)
        else:
            in_specs.append(pl.BlockSpec((None,) + arr.shape[1:], lambda i, w, l=l: (l, 0, 0)))
            a2, b, dt = arr.shape[1] // 2, arr.shape[2], BF16
        out_specs.append(pl.BlockSpec((None, 2, a2, b), lambda i, w: (w[0], 0, 0, 0)))
        out_shape.append(jax.ShapeDtypeStruct((4, 2, a2, b), dt))
    in_specs += [ANY] * len(after)
    return pl.pallas_call(
        body, name=name,
        grid_spec=pltpu.PrefetchScalarGridSpec(num_scalar_prefetch=1, grid=(1,), in_specs=in_specs,
                                               out_specs=out_specs),
        out_shape=out_shape, compiler_params=_cp("arbitrary"),
    )(me, *[arr for arr, _ in srcs], *after)


class WeightGather:
    def __init__(self, source, groups):
        self.names = dict(groups)
        self.ici, self.d2d = {}, {}
        self.token = None
        for gname, names in groups:
            nt = len(names)
            after = [] if self.token is None else [self.token]
            lands = cast_into_slot([source(n) for n in names], f"ag_cast_{gname}", after)

            def plan(refs, nt=nt):
                x, y, c, chips = _place()
                out = []
                for t in range(nt):
                    mine = refs[t].at[2 * x + y, c]
                    out += [(mine, mine, (cx, cy, c)) for cx, cy in chips]
                return out

            self.ici[gname], self.token = xchg_start(f"ag_ici_{gname}", lands, plan, 3 * nt, after=after)

    def forward(self, gname, after):
        nt = len(self.names[gname])
        lands = xchg_wait(self.ici.pop(gname), after)

        def plan(refs):
            x, y, c, chips = _place()
            out = []
            for t in range(nt):
                for cx, cy in chips:
                    piece = refs[t].at[2 * cx + cy, c]
                    out.append((piece, piece, (x, y, 1 - c)))
            return out

        self.d2d[gname], token = xchg_start(f"ag_d2d_{gname}", lands, plan, 3 * nt)
        return token

    def get(self, gname, after):
        lands = xchg_wait(self.d2d.pop(gname), after)
        return dict(zip(self.names[gname], lands))


class GradReduce:
    def __init__(self, kinds):
        self.J = {k: lax.empty((L, 2, a2, b), F32) for k, (L, a2, b) in kinds.items()}
        self.x, self.j = {}, {}

    @staticmethod
    def _where(name):
        kind, _, l = name.partition("_")
        return kind, int(l or 0)

    def send(self, gname, grads, after=()):
        names = list(grads)
        nt = len(names)
        gs = [grads[n] for n in names]
        xs = [lax.empty((7,) + g.shape[2:], g.dtype) for g in gs]

        def plan(refs):
            x, y, c, _ = _place()
            out = []
            for t in range(nt):
                for k in range(1, 8):
                    px, py, pc = _flip(k, x, y, c)
                    out.append((refs[t].at[2 * px + py, pc], refs[nt + t].at[k - 1], (px, py, pc)))
            return out

        flight, token = xchg_start(f"rs_x_{gname}", gs + xs, plan, 7 * nt, after=after)
        self.x[gname] = (names, flight)
        return token

    def reduce(self, gname, after):
        names, flight = self.x.pop(gname)
        nt = len(names)
        bufs = xchg_wait(flight, after)
        mine = jnp.stack([2 * lax.axis_index("x") + lax.axis_index("y"), lax.axis_index("c")]).astype(jnp.int32)
        where = [self._where(n) for n in names]
        js = [add8_into(self.J[kind], l, bufs[t], bufs[nt + t], mine, f"rs_add_{names[t]}")
              for t, (kind, l) in enumerate(where)]

        def plan(refs):
            x, y, c, _ = _place()
            out = []
            for t in range(nt):
                half = refs[t].at[where[t][1], c]
                out.append((half, half, (x, y, 1 - c)))
            return out

        flight, token = xchg_start(f"rs_join_{gname}", js, plan, nt)
        self.j[gname] = (where, flight)
        return token

    def finish(self, gname, after):
        where, flight = self.j.pop(gname)
        for (kind, _), j in zip(where, xchg_wait(flight, after)):
            self.J[kind] = j


def small_allreduce_start(v, after):
    me = 4 * lax.axis_index("x") + 2 * lax.axis_index("y") + lax.axis_index("c")
    land = lax.dynamic_update_slice(lax.empty((8,) + v.shape, v.dtype), v[None], (me, 0, 0))

    def plan(refs):
        x, y, c, _ = _place()
        return [(refs[0], refs[1].at[4 * x + 2 * y + c], _flip(k, x, y, c)) for k in range(1, 8)]

    return xchg_start("small_allreduce", [v, land], plan, 7, after=after)


def sum8(all8, name):
    def body(x_ref, o_ref):
        acc = x_ref[0]
        for d in range(1, 8):
            acc = acc + x_ref[d]
        o_ref[...] = acc

    return pl.pallas_call(
        body, name=name,
        in_specs=[pl.BlockSpec(memory_space=pltpu.VMEM)], out_specs=pl.BlockSpec(memory_space=pltpu.VMEM),
        out_shape=jax.ShapeDtypeStruct(all8.shape[1:], F32),
        compiler_params=pltpu.CompilerParams(vmem_limit_bytes=VMEM_LIMIT),
    )(all8)


AG_GROUPS = (("a0", ("pw1_0", "pw2_0", "small")), ("f0", ("up_0", "down_0")),
             ("l1", ("pw1_1", "pw2_1", "up_1", "down_1")), ("l2", ("kv", "wq_0", "wo_0", "up_2", "down_2")),
             ("l3", ("wq_1", "wo_1", "up_3", "down_3")))


def _bucket_table():
    qi = np.arange(BLK)[:, None]
    kj = np.arange(2 * BLK)[None, :]
    d = np.maximum(qi + BLK - kj, 0)
    max_exact = N_BUCKETS // 2
    log_ratio = (np.log(np.maximum(d, 1).astype(np.float32) / np.float32(max_exact))
                 / np.float32(math.log(MAX_DISTANCE / max_exact))).astype(np.float32)
    large = max_exact + (log_ratio * np.float32(N_BUCKETS - max_exact)).astype(np.int32)
    large = np.minimum(large, N_BUCKETS - 1)
    return np.where(d < max_exact, d, large).astype(np.int32)


def _heads_major(a, nh):
    T = a.shape[0]
    return a.reshape(T, nh, HD).transpose(1, 0, 2)


def _heads_minor(a):
    nh, T, _ = a.shape
    return a.transpose(1, 0, 2).reshape(T, nh * HD)


def _slots(land):
    return land.reshape(4, 2 * land.shape[2], land.shape[3])


def _rows(land):
    return land.reshape(8 * land.shape[2], land.shape[3])


def _gview(g):
    s, K, n = g.shape
    return g.reshape(4, 2, K // 2, n) if s == 4 else g.reshape(4, 2, K // 8, n)


def _gate(a, token):
    return a * (1.0 + token[0, 0])


def _conv_small(f_small):
    fs = f_small.transpose(1, 2, 0, 3).reshape(2, 40, D)
    b_pw1 = f_small[:, :, 35:37, :].transpose(1, 0, 2, 3).reshape(2, 1, 2 * D)
    rev = jnp.concatenate([fs[:, CONV_W - 1::-1], jnp.zeros((2, 40 - CONV_W, D), F32)], axis=1)
    return dict(conv=fs, conv_rev=rev, b_pw1=b_pw1, b_pw2=fs[:, 34:35])


def run_step(x, target, P, ag, rs):
    T = x.shape[0]
    zero = jnp.zeros((1, 1, D), F32)
    nm, nf = P["norm_mix"], P["norm_ffn"]
    ag.forward("a0", [ag.token])
    W = ag.get("a0", [])
    sm = _conv_small(W["small"])
    h = x
    saved = []
    for l in range(2):
        xn, u, a = norm_mm_glu(h, nm, l, _slots(W[f"pw1_{l}"]), sm["b_pw1"], f"f_pw1_{l}")
        y, s = dwconv_ln_silu(a, sm["conv"], l, f"f_conv_{l}")
        b2 = sm["b_pw2"]
        if l == 0:
            b2 = _gate(b2, ag.forward("f0", [s]))
        h1 = mm_bias_res(s, _rows(W[f"pw2_{l}"]), b2, l, h, f"f_pw2_{l}")
        if l == 0:
            W.update(ag.get("f0", [h1]))
        xn2, gu, f = norm_mm_swiglu(h1, nf, l, _slots(W[f"up_{l}"]), f"f_up_{l}")
        nxt = "l1" if l == 0 else "l2"
        h2 = mm_bias_res(f, _rows(W[f"down_{l}"]), _gate(zero, ag.forward(nxt, [f])), 0, h1, f"f_down_{l}")
        W.update(ag.get(nxt, [h2]))
        saved.append(dict(h=h, xn=xn, u=u, a=a, y=y, s=s, h1=h1, xn2=xn2, gu=gu, f=f))
        h = h2
    h_kv = h
    kvn, kv = norm_mm(h, P["norm_kv"], 0, _rows(W["kv"]), "f_kv")
    kp = jnp.pad(_heads_major(kv[:, :N_KV * HD], N_KV), ((0, 0), (BLK, 0), (0, 0)))
    vp = jnp.pad(_heads_major(kv[:, N_KV * HD:], N_KV), ((0, 0), (BLK, 0), (0, 0)))
    kvt = jnp.pad(kv.T.reshape(2, N_KV, HD, T), ((0, 0), (0, 0), (0, 0), (BLK, 0)))
    kt, vt = kvt[0], kvt[1]
    bucket = _bucket_table()
    onehot = jnp.asarray(np.eye(N_BUCKETS, dtype=np.float32)[bucket])
    bias = jnp.einsum("qkb,bh->hkq", onehot, P["rel_bias"], precision=lax.Precision.HIGHEST)
    bias = bias.reshape(N_KV, GROUP, 2 * BLK, BLK).transpose(0, 2, 1, 3).reshape(1, N_KV, 2 * BLK, QW)
    bias = bias + jnp.asarray(band_mask())[:, None]
    for j in range(2):
        l = 2 + j
        xn, q = norm_mm(h, nm, l, _rows(W[f"wq_{j}"]), f"f_q_{j}", scale=HD ** -0.5)
        qh = q.T.reshape(N_KV, GROUP, HD, T)
        sink = jnp.broadcast_to(P["sinks"][j].reshape(N_KV, GROUP, 1), (N_KV, GROUP, BLK)).reshape(N_KV, 1, QW)
        oh = attn_fwd(qh, kp, vt, bias, sink, f"f_attn_{j}")
        attn = oh.reshape(N_HEADS * HD, T).T
        h1 = mm_bias_res(attn, _rows(W[f"wo_{j}"]), zero, 0, h, f"f_wo_{j}")
        xn2, gu, f = norm_mm_swiglu(h1, nf, l, _slots(W[f"up_{l}"]), f"f_up_{l}")
        zg = _gate(zero, ag.forward("l3", [f])) if j == 0 else zero
        h2 = mm_bias_res(f, _rows(W[f"down_{l}"]), zg, 0, h1, f"f_down_{l}")
        if j == 0:
            W.update(ag.get("l3", [h2]))
        saved.append(dict(h=h, xn=xn, qh=qh, oh=oh, sink=sink, attn=attn, h1=h1, xn2=xn2, gu=gu, f=f))
        h = h2

    dh, st_final = final_loss(h, P["norm_final"], target, "loss_head")

    S = dict(norm_ffn=[None] * 4, norm_mix=[None] * 4, conv=[None] * 2, taps=[None] * 2, b_pw1=[None] * 2,
             b_pw2=[None] * 2, sinks=[None] * 2)

    def ffn_bwd(dh, sv, l, nf, after=()):
        du = mmT_swiglu_bwd(dh, _rows(W[f"down_{l}"]), sv["gu"], f"b_down_{l}", after)
        gd = mm_dw(sv["f"], dh, f"w_down_{l}", 512, 1)
        gu = mm_dw(sv["xn2"], du, f"w_up_{l}", DFF // 2, 4)
        dh, dg = mmT_rmsbwd(du, _slots(W[f"up_{l}"]), sv["h1"], nf, l, dh, f"b_up_{l}")
        S["norm_ffn"][l] = dg
        return dh, {f"down_{l}": _gview(gd), f"up_{l}": _gview(gu)}

    dk = dv = dbias = None
    sent = []
    for j in (1, 0):
        l = 2 + j
        sv = saved[l]
        dh, grads = ffn_bwd(dh, sv, l, nf, sent)
        dattn = mmT(dh, _rows(W[f"wo_{j}"]), f"b_wo_{j}")
        grads[f"wo_{j}"] = _gview(mm_dw(sv["attn"], dh, f"w_wo_{j}", 512, 1))
        doh = dattn.T.reshape(N_KV, GROUP, HD, T)
        dqh, dkj, dvj, dbj, dsj = attn_bwd(sv["qh"], kp, kt, vp, bias, sv["sink"], sv["oh"], doh, f"b_attn_{j}")
        dq = dqh.reshape(N_HEADS * HD, T).T
        grads[f"wq_{j}"] = _gview(mm_dw(sv["xn"], dq, f"w_q_{j}", 512, 1))
        dh, dg = mmT_rmsbwd(dq, _rows(W[f"wq_{j}"])[None], sv["h"], nm, l, dh, f"b_q_{j}")
        S["norm_mix"][l] = dg
        S["sinks"][j] = jnp.sum(dsj.reshape(N_HEADS, BLK), axis=1)
        dk = dkj if dk is None else dk + dkj
        dv = dvj if dv is None else dv + dvj
        dbias = dbj if dbias is None else dbias + dbj
        if j == 1:
            sent = [rs.send("l3", grads)]

    dkv = jnp.concatenate([_heads_minor(dk[:, BLK:]), _heads_minor(dv[:, BLK:])], axis=1).astype(BF16)
    grads["kv"] = _gview(mm_dw(kvn, dkv, "w_kv", 512, 1))
    dh, dg = mmT_rmsbwd(dkv, _rows(W["kv"])[None], h_kv, P["norm_kv"], 0, dh, "b_kv")
    S["norm_kv"] = dg
    dbh = dbias.reshape(N_KV, 2 * BLK, GROUP, BLK)
    S["rel_bias"] = jnp.einsum("vkgq,qkb->bvg", dbh, onehot, precision=lax.Precision.HIGHEST).reshape(N_BUCKETS, N_HEADS)
    sent = [rs.send("l2", grads)]
    nf = _gate(nf, rs.reduce("l3", [dh]))

    for l in (1, 0):
        sv = saved[l]
        dh, grads = ffn_bwd(dh, sv, l, nf, sent)
        conv = sm["conv"]
        if l == 0:
            conv = _gate(conv, rs.send("f0", grads))
            grads = {}
        dy, st = mmT_lnbwd(dh, _rows(W[f"pw2_{l}"]), sv["y"], conv, l, f"b_pw2_{l}")
        g2, S["b_pw2"][l] = mm_dw(sv["s"], dh, f"w_pw2_{l}", 512, 1, colsum=True)
        du, dtaps = dwconv_glu_bwd(dy, sv["a"], sv["u"], sm["conv"], sm["conv_rev"], l, f"b_conv_{l}")
        S["conv"][l] = st
        S["taps"][l] = dtaps
        if l == 0:
            rs.finish("l2", [du])
            nm = _gate(nm, rs.reduce("l1", [du]))
        g1, S["b_pw1"][l] = mm_dw(sv["xn"], du, f"w_pw1_{l}", 512, 4, colsum=True)
        grads[f"pw2_{l}"], grads[f"pw1_{l}"] = _gview(g2), _gview(g1)
        dh, dg = mmT_rmsbwd(du, _slots(W[f"pw1_{l}"]), sv["h"], nm, l, dh, f"b_pw1_{l}")
        S["norm_mix"][l] = dg
        if l == 1:
            sent = [rs.send("l1", grads)]
            rs.finish("l3", [dh])
            nf = _gate(nf, rs.reduce("l2", [dh]))
    S["final"] = st_final
    return grads, dh, S


R_CONV = 37
R_SMALL = 88


def _pack_small(S):
    rows = []
    for l in range(2):
        rows += [S["taps"][l], S["conv"][l], S["b_pw2"][l], S["b_pw1"][l].reshape(2, D)]
    rows += S["norm_mix"] + S["norm_ffn"] + [S["norm_kv"], S["final"]]
    tail = jnp.concatenate([jnp.stack(S["sinks"]).reshape(-1), S["rel_bias"].reshape(-1)])
    rows += [jnp.pad(tail, (0, D - tail.shape[0]))[None]]
    v = jnp.concatenate(rows, axis=0)
    return jnp.pad(v, ((0, R_SMALL - v.shape[0]), (0, 0)))


def kernel(x, norm_mix, norm_ffn, conv_w_pw1, conv_b_pw1, conv_w_dw, conv_b_dw, conv_ln_g, conv_ln_b, conv_w_pw2, conv_b_pw2, norm_kv, w_kv, w_q, w_o, sinks, rel_bias, ffn_w_up, ffn_w_down, norm_final, loss_target, m_norm_mix, m_norm_ffn, m_conv_w_pw1, m_conv_b_pw1, m_conv_w_dw, m_conv_b_dw, m_conv_ln_g, m_conv_ln_b, m_conv_w_pw2, m_conv_b_pw2, m_norm_kv, m_w_kv, m_w_q, m_w_o, m_sinks, m_rel_bias, m_ffn_w_up, m_ffn_w_down, m_norm_final, v_norm_mix, v_norm_ffn, v_conv_w_pw1, v_conv_b_pw1, v_conv_w_dw, v_conv_b_dw, v_conv_ln_g, v_conv_ln_b, v_conv_w_pw2, v_conv_b_pw2, v_norm_kv, v_w_kv, v_w_q, v_w_o, v_sinks, v_rel_bias, v_ffn_w_up, v_ffn_w_down, v_norm_final):
    me = 2 * lax.axis_index("x") + lax.axis_index("y")
    weights = dict(norm_mix=norm_mix, norm_ffn=norm_ffn, conv_w_pw1=conv_w_pw1, conv_b_pw1=conv_b_pw1,
                   conv_w_dw=conv_w_dw, conv_b_dw=conv_b_dw, conv_ln_g=conv_ln_g, conv_ln_b=conv_ln_b,
                   conv_w_pw2=conv_w_pw2, conv_b_pw2=conv_b_pw2, norm_kv=norm_kv, w_kv=w_kv, w_q=w_q, w_o=w_o,
                   sinks=sinks, rel_bias=rel_bias, ffn_w_up=ffn_w_up, ffn_w_down=ffn_w_down, norm_final=norm_final)
    mom_m = dict(norm_mix=m_norm_mix, norm_ffn=m_norm_ffn, conv_w_pw1=m_conv_w_pw1, conv_b_pw1=m_conv_b_pw1,
                 conv_w_dw=m_conv_w_dw, conv_b_dw=m_conv_b_dw, conv_ln_g=m_conv_ln_g, conv_ln_b=m_conv_ln_b,
                 conv_w_pw2=m_conv_w_pw2, conv_b_pw2=m_conv_b_pw2, norm_kv=m_norm_kv, w_kv=m_w_kv, w_q=m_w_q,
                 w_o=m_w_o, sinks=m_sinks, rel_bias=m_rel_bias, ffn_w_up=m_ffn_w_up, ffn_w_down=m_ffn_w_down,
                 norm_final=m_norm_final)
    mom_v = dict(norm_mix=v_norm_mix, norm_ffn=v_norm_ffn, conv_w_pw1=v_conv_w_pw1, conv_b_pw1=v_conv_b_pw1,
                 conv_w_dw=v_conv_w_dw, conv_b_dw=v_conv_b_dw, conv_ln_g=v_conv_ln_g, conv_ln_b=v_conv_ln_b,
                 conv_w_pw2=v_conv_w_pw2, conv_b_pw2=v_conv_b_pw2, norm_kv=v_norm_kv, w_kv=v_w_kv, w_q=v_w_q,
                 w_o=v_w_o, sinks=v_sinks, rel_bias=v_rel_bias, ffn_w_up=v_ffn_w_up, ffn_w_down=v_ffn_w_down,
                 norm_final=v_norm_final)

    big = {"conv_w_pw1": "pw1", "conv_w_pw2": "pw2", "w_q": "wq", "w_o": "wo", "ffn_w_up": "up",
           "ffn_w_down": "down", "w_kv": "kv"}
    of_kind = {k: n for n, k in big.items()}

    def source(name):
        if name == "small":
            return jnp.concatenate(
                [conv_w_dw, conv_b_dw[:, None], conv_ln_g[:, None], conv_ln_b[:, None], conv_b_pw2[:, None],
                 conv_b_pw1.reshape(2, 2, 256), jnp.zeros((2, 3, 256), F32)], axis=1), None
        kind, _, l = name.partition("_")
        return weights[of_kind[kind]], (int(l) if l else None)

    ag = WeightGather(source, AG_GROUPS)
    rs = GradReduce({"pw1": (2, 512, 512), "pw2": (2, 128, D), "wq": (2, 128, D), "wo": (2, 128, D),
                     "up": (4, 512, DFF // 2), "down": (4, DFF // 8, D), "kv": (1, 128, 512)})

    P = dict(norm_mix=norm_mix[:, None], norm_ffn=norm_ffn[:, None], norm_kv=norm_kv[None, None],
             norm_final=norm_final[None], sinks=sinks, rel_bias=rel_bias)
    last, grad_x, S = run_step(x[0], loss_target[0], P, ag, rs)

    rs.finish("l1", [grad_x])
    small_flight, token = small_allreduce_start(_gate(_pack_small(S), rs.reduce("f0", [grad_x])), [])
    token = rs.send("c0", last, after=[token])
    delta, new_m, new_v, big_grads = {}, {}, {}, {}

    def update(n):
        shp = weights[n].shape
        r2 = (int(np.prod(shp[:-1])), shp[-1])
        g, d, nm, nv = adamw(weights[n].reshape(r2), rs.J[big[n]].reshape(r2), mom_m[n].reshape(r2),
                             mom_v[n].reshape(r2), f"adamw_{n}", copy_g=True)
        big_grads[n], delta[n], new_m[n], new_v[n] = g.reshape(shp), d.reshape(shp), nm.reshape(shp), nv.reshape(shp)

    rs.finish("f0", [token])
    for n in ("ffn_w_up", "ffn_w_down"):
        update(n)
    vsum = sum8(xchg_wait(small_flight, [delta["ffn_w_up"], delta["ffn_w_down"]])[1], "small_sum")

    mine = lax.dynamic_slice_in_dim(vsum[0:2 * R_CONV].reshape(2, R_CONV, D), me * 256, 256, axis=2)
    b_pw1 = vsum[0:2 * R_CONV].reshape(2, R_CONV, D)[:, 35:37].reshape(2, 2 * D)
    grads = {"conv_w_dw": mine[:, 0:31], "conv_b_dw": mine[:, 31], "conv_ln_g": mine[:, 32],
             "conv_ln_b": mine[:, 33], "conv_b_pw2": mine[:, 34],
             "conv_b_pw1": lax.dynamic_slice_in_dim(b_pw1, me * 512, 512, axis=1)}
    base = 2 * R_CONV
    grads["norm_mix"] = vsum[base:base + 4]
    grads["norm_ffn"] = vsum[base + 4:base + 8]
    grads["norm_kv"] = vsum[base + 8]
    grads["norm_final"] = vsum[base + 9]
    loss = vsum[base + 10, 0]
    grads["sinks"] = vsum[base + 11, 0:32].reshape(2, 16)
    grads["rel_bias"] = vsum[base + 11, 32:32 + 512].reshape(32, 16)

    for n in weights:
        if n not in big:
            shp = weights[n].shape
            r2 = (int(np.prod(shp[:-1])), shp[-1])
            d, nm, nv = adamw(weights[n].reshape(r2), grads[n].reshape(r2), mom_m[n].reshape(r2),
                              mom_v[n].reshape(r2), f"adamw_{n}")
            delta[n], new_m[n], new_v[n] = d.reshape(shp), nm.reshape(shp), nv.reshape(shp)

    rs.reduce("c0", [vsum])
    for n in ("w_q", "w_o", "w_kv"):
        update(n)
    rs.finish("c0", [delta["w_kv"]])
    for n in ("conv_w_pw1", "conv_w_pw2"):
        update(n)
    grads.update(big_grads)

    order = list(weights)
    return (loss, grad_x[None], *[grads[n] for n in order], *[delta[n] for n in order],
            *[new_m[n] for n in order], *[new_v[n] for n in order])
```

```python
import functools
import math

import numpy as np
import jax
import jax.numpy as jnp
from jax import lax
from jax.experimental import pallas as pl
from jax.experimental.pallas import tpu as pltpu

F32 = jnp.float32
BF16 = jnp.bfloat16
MESH = pl.DeviceIdType.MESH

D = 1024
DFF = 2816
N_HEADS = 16
N_KV = 4
GROUP = 4
HD = 64
BLK = 128
CONV_W = 31
HALO = 32
N_BUCKETS = 32
MAX_DISTANCE = 128
EPS = 1e-6
NEG_INF = -1e30
TM = 512
TCV = 256
VMEM_LIMIT = 56 * 2 ** 20

ADAM_LR, ADAM_B1, ADAM_B2, ADAM_EPS, ADAM_WD, ADAM_STEP = 0.001, 0.9, 0.999, 1e-08, 0.01, 10


def _cp(*sem):
    return pltpu.CompilerParams(dimension_semantics=sem, vmem_limit_bytes=VMEM_LIMIT)


def _sigmoid(x):
    return 1.0 / (1.0 + jnp.exp(-x))


def _row(tm, n):
    return pl.BlockSpec((tm, n), lambda i: (i, 0))


def _const(shape):
    nd = len(shape)
    return pl.BlockSpec(shape, lambda i: (0,) * nd)


def _weight(shape):
    nd = len(shape)
    return pl.BlockSpec(shape, lambda i: (0,) * nd, pipeline_mode=pl.Buffered(1))


def _layer(shape, l):
    nd = len(shape)
    return pl.BlockSpec((None,) + tuple(shape), lambda i: (l,) + (0,) * nd)


def _dot(a, b):
    return jnp.dot(a, b, preferred_element_type=F32)


def _dot_nt(a, b):
    return lax.dot_general(a, b, (((1,), (1,)), ((), ())), preferred_element_type=F32)


def _dot_tn(a, b):
    return lax.dot_general(a, b, (((0,), (0,)), ((), ())), preferred_element_type=F32)


def _rms(x):
    return lax.rsqrt(jnp.mean(x * x, axis=-1, keepdims=True) + EPS)


def norm_mm_glu(h, g, l, w, b, name):
    T = h.shape[0]
    ns = w.shape[-1]

    def body(h_ref, g_ref, w_ref, b_ref, xn_ref, u_ref, a_ref):
        x = h_ref[...]
        xn = (x * _rms(x) * g_ref[...]).astype(BF16)
        xn_ref[...] = xn
        for s in range(2):
            lo, hi = s * ns, (s + 1) * ns
            u1 = _dot(xn, w_ref[s]) + b_ref[:, lo:hi]
            u2 = _dot(xn, w_ref[2 + s]) + b_ref[:, D + lo:D + hi]
            u_ref[:, lo:hi] = u1.astype(BF16)
            u_ref[:, D + lo:D + hi] = u2.astype(BF16)
            a_ref[:, lo:hi] = (u1 * _sigmoid(u2)).astype(BF16)

    return pl.pallas_call(
        body, name=name, grid=(T // TM,),
        in_specs=[_row(TM, D), _layer((1, D), l), _weight((4, D, ns)), _layer((1, 2 * D), l)],
        out_specs=[_row(TM, D), _row(TM, 2 * D), _row(TM, D)],
        out_shape=[jax.ShapeDtypeStruct((T, D), BF16), jax.ShapeDtypeStruct((T, 2 * D), BF16),
                   jax.ShapeDtypeStruct((T, D), BF16)],
        compiler_params=_cp("parallel"),
    )(h, g, w, b)


SUB = 8


def _make_shifts(sh):
    n = TCV + HALO - SUB
    for r in range(1, SUB):
        for r0 in range(0, n, 40):
            sh[r, r0:r0 + 40, :] = sh[0, pl.ds(r + r0, 40), :]


def _shifted(sh, off, rows, cols):
    return sh[off % SUB, pl.ds(off - off % SUB, rows), cols]


def _conv_taps(sh, w_ref, out_ref, first):
    RB, LB = 32, 512
    for r0 in range(0, TCV, RB):
        for c0 in range(0, out_ref.shape[1], LB):
            acc = jnp.zeros((RB, LB), F32)
            for k in range(CONV_W):
                acc = acc + w_ref[k:k + 1, c0:c0 + LB] * _shifted(sh, first + k + r0, RB, slice(c0, c0 + LB))
            out_ref[r0:r0 + RB, c0:c0 + LB] = acc


def dwconv_ln_silu(a, sm, l, name):
    T = a.shape[0]
    nb = TCV // HALO

    def body(cur_ref, prev_ref, sm_ref, y_ref, s_ref, sh, yb):
        i = pl.program_id(0)
        sh[0, 0:HALO, :] = jnp.where(i > 0, prev_ref[...].astype(F32), 0.0)
        sh[0, HALO:HALO + TCV, :] = cur_ref[...].astype(F32)
        _make_shifts(sh)
        _conv_taps(sh, sm_ref, yb, HALO - (CONV_W - 1))
        y = yb[...] + sm_ref[31:32, :]
        y_ref[...] = y.astype(BF16)
        mu = jnp.mean(y, axis=-1, keepdims=True)
        yc = y - mu
        rstd = lax.rsqrt(jnp.mean(yc * yc, axis=-1, keepdims=True) + EPS)
        z = yc * rstd * sm_ref[32:33, :] + sm_ref[33:34, :]
        s_ref[...] = (z * _sigmoid(z)).astype(BF16)

    return pl.pallas_call(
        body, name=name, grid=(T // TCV,),
        in_specs=[_row(TCV, D), pl.BlockSpec((HALO, D), lambda i: (jnp.maximum(i * nb - 1, 0), 0)),
                  _layer((40, D), l)],
        out_specs=[_row(TCV, D), _row(TCV, D)],
        out_shape=[jax.ShapeDtypeStruct((T, D), BF16), jax.ShapeDtypeStruct((T, D), BF16)],
        scratch_shapes=[pltpu.VMEM((SUB, TCV + HALO, D), F32), pltpu.VMEM((TCV, D), F32)],
        compiler_params=_cp("parallel"),
    )(a, a, sm)


def mm_bias_res(xb, w, b, bl, res, name):
    T, K = xb.shape

    def body(x_ref, w_ref, b_ref, r_ref, o_ref):
        o_ref[...] = _dot(x_ref[...], w_ref[...]) + b_ref[...] + r_ref[...]

    return pl.pallas_call(
        body, name=name, grid=(T // TM,),
        in_specs=[_row(TM, K), _weight((K, D)), _layer((1, D), bl), _row(TM, D)],
        out_specs=_row(TM, D), out_shape=jax.ShapeDtypeStruct((T, D), F32),
        compiler_params=_cp("parallel"),
    )(xb, w, b, res)


def norm_mm_swiglu(h, g, l, w, name):
    T = h.shape[0]
    ns = w.shape[-1]

    def body(h_ref, g_ref, w_ref, xn_ref, gu_ref, f_ref):
        x = h_ref[...]
        xn = (x * _rms(x) * g_ref[...]).astype(BF16)
        xn_ref[...] = xn
        for s in range(2):
            lo, hi = s * ns, (s + 1) * ns
            gate = _dot(xn, w_ref[s])
            up = _dot(xn, w_ref[2 + s])
            gu_ref[:, lo:hi] = gate.astype(BF16)
            gu_ref[:, DFF + lo:DFF + hi] = up.astype(BF16)
            f_ref[:, lo:hi] = (gate * _sigmoid(gate) * up).astype(BF16)

    return pl.pallas_call(
        body, name=name, grid=(T // TM,),
        in_specs=[_row(TM, D), _layer((1, D), l), _weight((4, D, ns))],
        out_specs=[_row(TM, D), _row(TM, 2 * DFF), _row(TM, DFF)],
        out_shape=[jax.ShapeDtypeStruct((T, D), BF16), jax.ShapeDtypeStruct((T, 2 * DFF), BF16),
                   jax.ShapeDtypeStruct((T, DFF), BF16)],
        compiler_params=_cp("parallel"),
    )(h, g, w)


def norm_mm(h, g, gl, w, name, scale=1.0):
    T = h.shape[0]
    N = w.shape[-1]

    def body(h_ref, g_ref, w_ref, xn_ref, o_ref):
        x = h_ref[...]
        xn = (x * _rms(x) * g_ref[...]).astype(BF16)
        xn_ref[...] = xn
        o_ref[...] = (_dot(xn, w_ref[...]) * scale).astype(BF16)

    return pl.pallas_call(
        body, name=name, grid=(T // TM,),
        in_specs=[_row(TM, D), _layer((1, D), gl), _weight((D, N))],
        out_specs=[_row(TM, D), _row(TM, N)],
        out_shape=[jax.ShapeDtypeStruct((T, D), BF16), jax.ShapeDtypeStruct((T, N), BF16)],
        compiler_params=_cp("parallel"),
    )(h, g, w)


QB = 32
QW = GROUP * BLK


def band_mask():
    qi = np.arange(QW)[None, :] % BLK
    kj = np.arange(2 * BLK)[:, None]
    band = ((kj < BLK) & (kj > qi)) | ((kj >= BLK) & (kj - BLK <= qi))
    first = band & (kj >= BLK)
    return np.where(np.stack([first, band]), 0.0, NEG_INF).astype(np.float32)


def _softmax_cols(s, sink):
    m = jnp.maximum(jnp.max(s, axis=0, keepdims=True), sink)
    p = jnp.exp(s - m)
    es = jnp.exp(sink - m)
    inv = 1.0 / (jnp.sum(p, axis=0, keepdims=True) + es)
    return p, inv, es


def _attn_specs(T):
    W = QB * BLK
    qspec = pl.BlockSpec((None, GROUP, HD, W), lambda kv, n: (kv, 0, 0, n))
    kspec = pl.BlockSpec((None, T + BLK, HD), lambda kv, n: (kv, 0, 0))
    ktspec = [pl.BlockSpec((None, HD, W), lambda kv, n: (kv, 0, n)),
              pl.BlockSpec((None, HD, BLK), lambda kv, n: (kv, 0, (n + 1) * QB))]
    bspec = pl.BlockSpec((2, None, 2 * BLK, QW), lambda kv, n: (0, kv, 0, 0))
    sspec = pl.BlockSpec((None, 1, QW), lambda kv, n: (kv, 0, 0))
    return qspec, kspec, ktspec, bspec, sspec


def _attn_block(n, b):
    blk = n * QB + b
    rows = pl.ds(pl.multiple_of(blk * BLK, BLK), 2 * BLK)
    return rows, (jnp.minimum(blk, 1) if b == 0 else 1)


def _band_cols(main_ref, tail_ref, b):
    if b < QB - 1:
        return main_ref[:, b * BLK:(b + 2) * BLK]
    return jnp.concatenate([main_ref[:, b * BLK:], tail_ref[...]], axis=1)


def _heads_side_by_side(ref, qs):
    return jnp.concatenate([ref[g, :, qs] for g in range(GROUP)], axis=1)


def attn_fwd(q, kp, vt, bias, sink, name):
    T = q.shape[3]
    qspec, kspec, ktspec, bspec, sspec = _attn_specs(T)

    def body(q_ref, k_ref, vt_ref, vtt_ref, b_ref, s_ref, o_ref, pb):
        n = pl.program_id(1)

        def scores(b):
            return _dot(k_ref[_attn_block(n, b)[0], :], _heads_side_by_side(q_ref, slice(b * BLK, (b + 1) * BLK)))

        st_next = scores(0)
        for b in range(QB):
            rows, table = _attn_block(n, b)
            qs = slice(b * BLK, (b + 1) * BLK)
            st = st_next
            if b + 1 < QB:
                st_next = scores(b + 1)
            for g in range(GROUP):
                hs = slice(g * BLK, (g + 1) * BLK)
                p, inv, _ = _softmax_cols(st[:, hs] + b_ref[table, :, hs], s_ref[:, hs])
                pb[:, hs] = (p * inv).astype(BF16)
            ot = _dot(_band_cols(vt_ref, vtt_ref, b), pb[...])
            for g in range(GROUP):
                o_ref[g, :, qs] = ot[:, g * BLK:(g + 1) * BLK].astype(BF16)

    return pl.pallas_call(
        body, name=name, grid=(N_KV, T // (QB * BLK)),
        in_specs=[qspec, kspec, *ktspec, bspec, sspec], out_specs=qspec,
        out_shape=jax.ShapeDtypeStruct((N_KV, GROUP, HD, T), BF16),
        scratch_shapes=[pltpu.VMEM((2 * BLK, QW), BF16)],
        compiler_params=_cp("parallel", "parallel"),
    )(q, kp, vt, vt, bias, sink)


def attn_bwd(q, kp, kt, vp, bias, sink, o, do, name):
    T = q.shape[3]
    qspec, kspec, ktspec, bspec, sspec = _attn_specs(T)

    def body(q_ref, k_ref, kt_ref, ktt_ref, v_ref, b_ref, s_ref, o_ref, do_ref,
             dq_ref, dk_ref, dv_ref, db_ref, ds_ref, pb, dsb):
        n = pl.program_id(1)

        @pl.when(n == 0)
        def _():
            dk_ref[...] = jnp.zeros_like(dk_ref)
            dv_ref[...] = jnp.zeros_like(dv_ref)
            db_ref[...] = jnp.zeros_like(db_ref)
            ds_ref[...] = jnp.zeros_like(ds_ref)

        def products(b):
            rows = _attn_block(n, b)[0]
            qs = slice(b * BLK, (b + 1) * BLK)
            q4, do4 = _heads_side_by_side(q_ref, qs), _heads_side_by_side(do_ref, qs)
            return q4, do4, _dot(k_ref[rows, :], q4), _dot(v_ref[rows, :], do4)

        ahead = products(0)
        for b in range(QB):
            rows, table = _attn_block(n, b)
            qs = slice(b * BLK, (b + 1) * BLK)
            q4, do4, st, dpt = ahead
            if b + 1 < QB:
                ahead = products(b + 1)
            for g in range(GROUP):
                hs = slice(g * BLK, (g + 1) * BLK)
                p, inv, es = _softmax_cols(st[:, hs] + b_ref[table, :, hs], s_ref[:, hs])
                probs = p * inv
                delta = jnp.sum(do_ref[g, :, qs].astype(F32) * o_ref[g, :, qs].astype(F32), axis=0, keepdims=True)
                dS = probs * (dpt[:, hs] - delta)
                ds_ref[:, hs] += -(es * inv) * delta
                db_ref[:, hs] += dS
                pb[:, hs] = probs.astype(BF16)
                dsb[:, hs] = dS.astype(BF16)
            dqt = _dot(_band_cols(kt_ref, ktt_ref, b), dsb[...]) * (HD ** -0.5)
            for g in range(GROUP):
                dq_ref[g, :, qs] = dqt[:, g * BLK:(g + 1) * BLK].astype(BF16)
            dk_ref[rows, :] += _dot_nt(dsb[...], q4)
            dv_ref[rows, :] += _dot_nt(pb[...], do4)

    kout = pl.BlockSpec((None, T + BLK, HD), lambda kv, n: (kv, 0, 0))
    dbspec = pl.BlockSpec((None, 2 * BLK, QW), lambda kv, n: (kv, 0, 0))
    return pl.pallas_call(
        body, name=name, grid=(N_KV, T // (QB * BLK)),
        in_specs=[qspec, kspec, *ktspec, kspec, bspec, sspec, qspec, qspec],
        out_specs=[qspec, kout, kout, dbspec, sspec],
        out_shape=[jax.ShapeDtypeStruct((N_KV, GROUP, HD, T), BF16),
                   jax.ShapeDtypeStruct((N_KV, T + BLK, HD), F32), jax.ShapeDtypeStruct((N_KV, T + BLK, HD), F32),
                   jax.ShapeDtypeStruct((N_KV, 2 * BLK, QW), F32), jax.ShapeDtypeStruct((N_KV, 1, QW), F32)],
        scratch_shapes=[pltpu.VMEM((2 * BLK, QW), BF16), pltpu.VMEM((2 * BLK, QW), BF16)],
        compiler_params=_cp("parallel", "arbitrary"),
    )(q, kp, kt, kt, vp, bias, sink, o, do)


def final_loss(h, g, target, name):
    T = h.shape[0]

    def body(h_ref, g_ref, t_ref, dh_ref, st_ref):
        i = pl.program_id(0)

        @pl.when(i == 0)
        def _():
            st_ref[...] = jnp.zeros_like(st_ref)

        x = h_ref[...]
        r = _rms(x)
        xh = x * r
        e = xh * g_ref[...] - t_ref[...]
        loss = 0.5 * jnp.sum(jnp.mean(e * e, axis=-1, keepdims=True))
        dy = e * (1.0 / D)
        st_ref[0:1, :] += jnp.sum(dy * xh, axis=0, keepdims=True)
        lane = lax.broadcasted_iota(jnp.int32, (1, D), 1)
        st_ref[1:2, :] += jnp.where(lane == 0, loss, 0.0)
        dxh = dy * g_ref[...]
        dh_ref[...] = r * (dxh - xh * jnp.mean(dxh * xh, axis=-1, keepdims=True))

    return pl.pallas_call(
        body, name=name, grid=(T // TM,),
        in_specs=[_row(TM, D), _const((1, D)), _row(TM, D)],
        out_specs=[_row(TM, D), _const((2, D))],
        out_shape=[jax.ShapeDtypeStruct((T, D), F32), jax.ShapeDtypeStruct((2, D), F32)],
        compiler_params=_cp("arbitrary"),
    )(h, g, target)


def mm_dw(x, dy, name, tn, slots, colsum=False):
    T, K = x.shape
    split = dy.ndim == 3
    N = dy.shape[-1] * (2 if split else 1)
    tt = min(T, 2048 if K <= 1024 else 1024)
    nt = T // tt
    ns = N // slots
    per = ns // tn

    def body(x_ref, dy_ref, *rest):
        if colsum:
            dw_ref, cs_ref, acc, cacc = rest
        else:
            dw_ref, acc = rest
        t = pl.program_id(1)

        @pl.when(t == 0)
        def _():
            acc[...] = jnp.zeros_like(acc)
            if colsum:
                cacc[...] = jnp.zeros_like(cacc)

        dyv = dy_ref[...]
        acc[...] += _dot_tn(x_ref[...].astype(BF16), dyv.astype(BF16))
        if colsum:
            cacc[...] += jnp.sum(dyv.astype(F32), axis=0, keepdims=True)

        @pl.when(t == nt - 1)
        def _():
            dw_ref[...] = acc[...].astype(BF16)
            if colsum:
                cs_ref[...] = cacc[...]

    if split:
        half = N // 2 // tn
        dy_spec = pl.BlockSpec((None, tt, tn), lambda j, t: (j // half, t, j % half))
    else:
        dy_spec = pl.BlockSpec((tt, tn), lambda j, t: (t, j))
    out_specs = [pl.BlockSpec((None, K, tn), lambda j, t: (j // per, 0, j % per))]
    out_shape = [jax.ShapeDtypeStruct((slots, K, ns), BF16)]
    scratch = [pltpu.VMEM((K, tn), F32)]
    if colsum:
        out_specs.append(pl.BlockSpec((1, tn), lambda j, t: (0, j)))
        out_shape.append(jax.ShapeDtypeStruct((1, N), F32))
        scratch.append(pltpu.VMEM((1, tn), F32))
    res = pl.pallas_call(
        body, name=name, grid=(N // tn, nt),
        in_specs=[pl.BlockSpec((tt, K), lambda j, t: (t, 0)), dy_spec],
        out_specs=out_specs, out_shape=out_shape, scratch_shapes=scratch,
        compiler_params=_cp("parallel", "arbitrary"),
    )(x, dy)
    return tuple(res) if colsum else res[0]


def mmT_swiglu_bwd(dh, w, gu, name, after=()):
    T = dh.shape[0]
    cw = 256

    def body(dh_ref, w_ref, gu_ref, *rest):
        du_ref = rest[-1]
        dhb = dh_ref[...].astype(BF16)
        ahead = _dot_nt(dhb, w_ref[0:cw, :])
        for lo in range(0, DFF, cw):
            hi = lo + cw
            df = ahead
            if hi < DFF:
                ahead = _dot_nt(dhb, w_ref[hi:hi + cw, :])
            gate = gu_ref[:, lo:hi].astype(F32)
            up = gu_ref[:, DFF + lo:DFF + hi].astype(F32)
            sg = _sigmoid(gate)
            silu = gate * sg
            du_ref[:, lo:hi] = (df * (up * (sg + silu * (1.0 - sg)))).astype(BF16)
            du_ref[:, DFF + lo:DFF + hi] = (df * silu).astype(BF16)

    return pl.pallas_call(
        body, name=name, grid=(T // TM,),
        in_specs=[_row(TM, D), _weight((DFF, D)), _row(TM, 2 * DFF)] + [ANY] * len(after),
        out_specs=_row(TM, 2 * DFF), out_shape=jax.ShapeDtypeStruct((T, 2 * DFF), BF16),
        compiler_params=_cp("parallel"),
    )(dh, w, gu, *after)


def mmT_rmsbwd(du, w, h, g, gl, dh_in, name):
    split = du.ndim == 3
    T = du.shape[-2]
    N = du.shape[-1] * (2 if split else 1)
    slots = w.shape[0]
    ns = N // slots

    RH = TM // 2

    def piece(du_ref, s, rows):
        if split:
            per = slots // 2
            return du_ref[s // per, rows, (s % per) * ns:(s % per + 1) * ns]
        return du_ref[rows, s * ns:(s + 1) * ns]

    def body(du_ref, w_ref, h_ref, g_ref, di_ref, dh_ref, dg_ref):
        i = pl.program_id(0)

        @pl.when(i == 0)
        def _():
            dg_ref[...] = jnp.zeros_like(dg_ref)

        def products(k):
            rows = slice(k * RH, (k + 1) * RH)
            dxn = _dot_nt(piece(du_ref, 0, rows), w_ref[0])
            for s in range(1, slots):
                dxn = dxn + _dot_nt(piece(du_ref, s, rows), w_ref[s])
            return dxn

        ahead = products(0)
        for k in range(TM // RH):
            rows = slice(k * RH, (k + 1) * RH)
            dxn = ahead
            if (k + 1) * RH < TM:
                ahead = products(k + 1)
            x = h_ref[rows, :]
            r = _rms(x)
            xh = x * r
            dg_ref[0:1, :] += jnp.sum(dxn * xh, axis=0, keepdims=True)
            dxh = dxn * g_ref[...]
            dh_ref[rows, :] = di_ref[rows, :] + r * (dxh - xh * jnp.mean(dxh * xh, axis=-1, keepdims=True))

    return pl.pallas_call(
        body, name=name, grid=(T // TM,),
        in_specs=[pl.BlockSpec((2, TM, N // 2), lambda i: (0, i, 0)) if split else _row(TM, N),
                  _weight((slots, D, ns)), _row(TM, D), _layer((1, D), gl), _row(TM, D)],
        out_specs=[_row(TM, D), _const((1, D))],
        out_shape=[jax.ShapeDtypeStruct((T, D), F32), jax.ShapeDtypeStruct((1, D), F32)],
        compiler_params=_cp("arbitrary"),
    )(du, w, h, g, dh_in)


def mmT(dh, w, name):
    T = dh.shape[0]
    N = w.shape[0]

    def body(dh_ref, w_ref, o_ref):
        o_ref[...] = _dot_nt(dh_ref[...].astype(BF16), w_ref[...]).astype(BF16)

    return pl.pallas_call(
        body, name=name, grid=(T // TM,),
        in_specs=[_row(TM, D), _weight((N, D))],
        out_specs=_row(TM, N), out_shape=jax.ShapeDtypeStruct((T, N), BF16),
        compiler_params=_cp("parallel"),
    )(dh, w)


def mmT_lnbwd(dh, w, y, sm, l, name):
    T = dh.shape[0]

    def body(dh_ref, w_ref, y_ref, sm_ref, dy_ref, st_ref):
        i = pl.program_id(0)

        @pl.when(i == 0)
        def _():
            st_ref[...] = jnp.zeros_like(st_ref)

        ds = _dot_nt(dh_ref[...].astype(BF16), w_ref[...])
        y = y_ref[...].astype(F32)
        mu = jnp.mean(y, axis=-1, keepdims=True)
        yc = y - mu
        rstd = lax.rsqrt(jnp.mean(yc * yc, axis=-1, keepdims=True) + EPS)
        xh = yc * rstd
        gam = sm_ref[32:33, :]
        z = xh * gam + sm_ref[33:34, :]
        sg = _sigmoid(z)
        dz = ds * sg * (1.0 + z * (1.0 - sg))
        st_ref[1:2, :] += jnp.sum(dz * xh, axis=0, keepdims=True)
        st_ref[2:3, :] += jnp.sum(dz, axis=0, keepdims=True)
        dxh = dz * gam
        dy = rstd * (dxh - jnp.mean(dxh, axis=-1, keepdims=True) - xh * jnp.mean(dxh * xh, axis=-1, keepdims=True))
        st_ref[0:1, :] += jnp.sum(dy, axis=0, keepdims=True)
        dy_ref[...] = dy.astype(BF16)

    return pl.pallas_call(
        body, name=name, grid=(T // TM,),
        in_specs=[_row(TM, D), _weight((D, D)), _row(TM, D), _layer((40, D), l)],
        out_specs=[_row(TM, D), _const((3, D))],
        out_shape=[jax.ShapeDtypeStruct((T, D), BF16), jax.ShapeDtypeStruct((3, D), F32)],
        compiler_params=_cp("arbitrary"),
    )(dh, w, y, sm)


CH = 512


def dwconv_glu_bwd(dy, a, u, sm, smrev, l, name):
    T = dy.shape[0]
    nr, nc = T // TCV, D // CH
    nb = TCV // HALO
    last = T // HALO - 1

    def body(dy_ref, dyn_ref, a_ref, ap_ref, u1_ref, u2_ref, sm_ref, rev_ref, du_ref, dw_ref, shd, sha, da):
        i = pl.program_id(0)
        r = i % nr

        @pl.when(r == 0)
        def _():
            dw_ref[...] = jnp.zeros_like(dw_ref)

        shd[0, 0:TCV, :] = dy_ref[...].astype(F32)
        shd[0, TCV:TCV + HALO, :] = jnp.where(r < nr - 1, dyn_ref[...].astype(F32), 0.0)
        sha[0, 0:HALO, :] = jnp.where(r > 0, ap_ref[...].astype(F32), 0.0)
        sha[0, HALO:HALO + TCV, :] = a_ref[...].astype(F32)
        _make_shifts(shd)
        _make_shifts(sha)
        _conv_taps(shd, rev_ref, da, 0)
        for kg in range(0, CONV_W, SUB):
            taps = range(kg, min(kg + SUB, CONV_W))
            part = [jnp.zeros((SUB, CH), F32) for _ in taps]
            for r0 in range(0, TCV, SUB):
                d = shd[0, r0:r0 + SUB, :]
                for j, k in enumerate(taps):
                    part[j] = part[j] + d * _shifted(sha, HALO - (CONV_W - 1) + k + r0, SUB, slice(None))
            for j, k in enumerate(taps):
                dw_ref[k:k + 1, :] += jnp.sum(part[j], axis=0, keepdims=True)
        dav = da[...]
        u1 = u1_ref[...].astype(F32)
        sg = _sigmoid(u2_ref[...].astype(F32))
        du_ref[0] = (dav * sg).astype(BF16)
        du_ref[1] = (dav * u1 * sg * (1.0 - sg)).astype(BF16)

    tile = lambda i: (i % nr, i // nr)
    in_specs = [pl.BlockSpec((TCV, CH), tile),
                pl.BlockSpec((HALO, CH), lambda i: (jnp.minimum((i % nr + 1) * nb, last), i // nr)),
                pl.BlockSpec((TCV, CH), tile),
                pl.BlockSpec((HALO, CH), lambda i: (jnp.maximum((i % nr) * nb - 1, 0), i // nr)),
                pl.BlockSpec((TCV, CH), tile), pl.BlockSpec((TCV, CH), lambda i: (i % nr, nc + i // nr)),
                pl.BlockSpec((None, 40, CH), lambda i: (l, 0, i // nr)),
                pl.BlockSpec((None, 40, CH), lambda i: (l, 0, i // nr))]
    return pl.pallas_call(
        body, name=name, grid=(nr * nc,), in_specs=in_specs,
        out_specs=[pl.BlockSpec((2, TCV, CH), lambda i: (0, i % nr, i // nr)),
                   pl.BlockSpec((CONV_W, CH), lambda i: (0, i // nr))],
        out_shape=[jax.ShapeDtypeStruct((2, T, D), BF16), jax.ShapeDtypeStruct((CONV_W, D), F32)],
        scratch_shapes=[pltpu.VMEM((SUB, TCV + HALO, CH), F32), pltpu.VMEM((SUB, TCV + HALO, CH), F32),
                        pltpu.VMEM((TCV, CH), F32)],
        compiler_params=_cp("arbitrary"),
    )(dy, dy, a, a, u, u, sm, smrev)


def _rows_tile(R):
    for t in (512, 256, 128, 64, 32, 16, 8):
        if R % t == 0:
            return t
    return R


def add8_into(J, l, g, others, where, name):
    R, C = g.shape[2:]
    tr = R // 2

    def body(w_ref, g_ref, x_ref, j_in, j_ref):
        acc = g_ref[...].astype(F32)
        for k in range(7):
            acc = acc + x_ref[k].astype(F32)
        j_ref[...] = acc

    return pl.pallas_call(
        body, name=name,
        grid_spec=pltpu.PrefetchScalarGridSpec(
            num_scalar_prefetch=1, grid=(R // tr,),
            in_specs=[pl.BlockSpec((None, None, tr, C), lambda i, w: (w[0], w[1], i, 0)),
                      pl.BlockSpec((7, tr, C), lambda i, w: (0, i, 0)), ANY],
            out_specs=pl.BlockSpec((None, None, tr, C), lambda i, w: (l, w[1], i, 0))),
        out_shape=jax.ShapeDtypeStruct(J.shape, F32), input_output_aliases={3: 0},
        compiler_params=_cp("parallel"),
    )(where, g, others, J)


def adamw(w, g, m, v, name, copy_g=False):
    R, C = w.shape
    tr = _rows_tile(R)

    def body(w_ref, g_ref, m_ref, v_ref, *outs):
        d_ref, nm_ref, nv_ref = outs[-3:]
        gv = g_ref[...]
        if copy_g:
            outs[0][...] = gv
        nm = ADAM_B1 * m_ref[...] + (1.0 - ADAM_B1) * gv
        nv = ADAM_B2 * v_ref[...] + (1.0 - ADAM_B2) * (gv * gv)
        m_hat = nm / (1.0 - ADAM_B1 ** ADAM_STEP)
        v_hat = nv / (1.0 - ADAM_B2 ** ADAM_STEP)
        d_ref[...] = -ADAM_LR * (m_hat / (jnp.sqrt(v_hat) + ADAM_EPS) + ADAM_WD * w_ref[...])
        nm_ref[...] = nm
        nv_ref[...] = nv

    sd = jax.ShapeDtypeStruct((R, C), F32)
    n_out = 4 if copy_g else 3
    return pl.pallas_call(
        body, name=name, grid=(R // tr,),
        in_specs=[_row(tr, C)] * 4, out_specs=[_row(tr, C)] * n_out, out_shape=[sd] * n_out,
        compiler_params=_cp("parallel"),
    )(w, g, m, v)


ANY = pl.BlockSpec(memory_space=pl.ANY)
HBM = pl.BlockSpec(memory_space=pltpu.HBM)
SEM = pl.BlockSpec(memory_space=pltpu.SEMAPHORE)
EFFECT = pltpu.SideEffectType.DATAFLOW_SIDE_EFFECTING


def _place():
    x, y, c = lax.axis_index("x"), lax.axis_index("y"), lax.axis_index("c")
    chips = [(1 - x, y), (x, 1 - y), (1 - x, 1 - y)]
    return x, y, c, chips


def _copy(src, dst, send, recv, k, to):
    return pltpu.make_async_remote_copy(src_ref=src, dst_ref=dst, send_sem=send.at[k], recv_sem=recv.at[k],
                                        device_id=to, device_id_type=MESH)


def xchg_start(name, bufs, plan, n, after=()):
    nb = len(bufs)

    na = len(after)

    def body(*refs):
        send, recv, token = refs[nb + na], refs[nb + na + 1], refs[-1]
        for k, (src, dst, to) in enumerate(plan(refs[:nb])):
            _copy(src, dst, send, recv, k, to).start()
        token[...] = jnp.zeros_like(token)

    outs = pl.pallas_call(
        body, name=name,
        out_shape=(pltpu.SemaphoreType.DMA((n,)), pltpu.SemaphoreType.DMA((n,)),
                   *[pltpu.HBM(b.shape, b.dtype) for b in bufs], jax.ShapeDtypeStruct((8, 128), F32)),
        in_specs=[HBM] * nb + [ANY] * na,
        out_specs=(SEM, SEM, *[HBM] * nb, pl.BlockSpec(memory_space=pltpu.VMEM)),
        input_output_aliases={i: 2 + i for i in range(nb)},
        compiler_params=pltpu.CompilerParams(has_side_effects=EFFECT),
    )(*[pltpu.with_memory_space_constraint(b, pltpu.HBM) for b in bufs], *after)
    return dict(name=name, send=outs[0], recv=outs[1], bufs=list(outs[2:2 + nb]), plan=plan), outs[-1]


def xchg_wait(flight, after):
    bufs, plan = flight["bufs"], flight["plan"]
    nb = len(bufs)

    def body(*refs):
        send, recv = refs[nb], refs[nb + 1]
        for k, (src, dst, to) in enumerate(plan(refs[:nb])):
            cp = _copy(src, dst, send, recv, k, to)
            cp.wait_send()
            cp.wait_recv()

    outs = pl.pallas_call(
        body, name=flight["name"] + "_wait",
        out_shape=tuple(pltpu.HBM(b.shape, b.dtype) for b in bufs),
        in_specs=[HBM] * nb + [SEM, SEM] + [ANY] * len(after),
        out_specs=tuple([HBM] * nb), input_output_aliases={i: i for i in range(nb)},
        compiler_params=pltpu.CompilerParams(has_side_effects=EFFECT),
    )(*bufs, flight["send"], flight["recv"], *after)
    return list(outs)


def _flip(k, x, y, c):
    return ((1 - x) if k & 4 else x, (1 - y) if k & 2 else y, (1 - c) if k & 1 else c)


def cast_into_slot(srcs, name, after):
    me = (2 * lax.axis_index("x") + lax.axis_index("y")).astype(jnp.int32).reshape(1)
    ns = len(srcs)

    def body(me_ref, *refs):
        outs = refs[ns + len(after):]
        for t in range(ns):
            outs[t][...] = refs[t][...].astype(outs[t].dtype).reshape(outs[t].shape)

    in_specs, out_specs, out_shape = [], [], []
    for arr, l in srcs:
        if l is None:
            in_specs.append(pl.BlockSpec(arr.shape, lambda i, w, nd=arr.ndim: (0,) * nd))
            a2, b, dt = (arr.shape[0] // 2, arr.shape[1], BF16) if arr.ndim == 2 else (arr.shape[1], arr.shape[2], F32)
        else:
            in_specs.append(pl.BlockSpec((None,) + arr.shape[1:], lambda i, w, l=l: (l, 0, 0)))
            a2, b, dt = arr.shape[1] // 2, arr.shape[2], BF16
        out_specs.append(pl.BlockSpec((None, 2, a2, b), lambda i, w: (w[0], 0, 0, 0)))
        out_shape.append(jax.ShapeDtypeStruct((4, 2, a2, b), dt))
    in_specs += [ANY] * len(after)
    return pl.pallas_call(
        body, name=name,
        grid_spec=pltpu.PrefetchScalarGridSpec(num_scalar_prefetch=1, grid=(1,), in_specs=in_specs,
                                               out_specs=out_specs),
        out_shape=out_shape, compiler_params=_cp("arbitrary"),
    )(me, *[arr for arr, _ in srcs], *after)


class WeightGather:
    def __init__(self, source, groups):
        self.names = dict(groups)
        self.ici, self.d2d = {}, {}
        self.token = None
        for gname, names in groups:
            nt = len(names)
            after = [] if self.token is None else [self.token]
            lands = cast_into_slot([source(n) for n in names], f"ag_cast_{gname}", after)

            def plan(refs, nt=nt):
                x, y, c, chips = _place()
                out = []
                for t in range(nt):
                    mine = refs[t].at[2 * x + y, c]
                    out += [(mine, mine, (cx, cy, c)) for cx, cy in chips]
                return out

            self.ici[gname], self.token = xchg_start(f"ag_ici_{gname}", lands, plan, 3 * nt, after=after)

    def forward(self, gname, after):
        nt = len(self.names[gname])
        lands = xchg_wait(self.ici.pop(gname), after)

        def plan(refs):
            x, y, c, chips = _place()
            out = []
            for t in range(nt):
                for cx, cy in chips:
                    piece = refs[t].at[2 * cx + cy, c]
                    out.append((piece, piece, (x, y, 1 - c)))
            return out

        self.d2d[gname], token = xchg_start(f"ag_d2d_{gname}", lands, plan, 3 * nt)
        return token

    def get(self, gname, after):
        lands = xchg_wait(self.d2d.pop(gname), after)
        return dict(zip(self.names[gname], lands))


class GradReduce:
    def __init__(self, kinds):
        self.J = {k: lax.empty((L, 2, a2, b), F32) for k, (L, a2, b) in kinds.items()}
        self.x, self.j = {}, {}

    @staticmethod
    def _where(name):
        kind, _, l = name.partition("_")
        return kind, int(l or 0)

    def send(self, gname, grads, after=()):
        names = list(grads)
        nt = len(names)
        gs = [grads[n] for n in names]
        xs = [lax.empty((7,) + g.shape[2:], g.dtype) for g in gs]

        def plan(refs):
            x, y, c, _ = _place()
            out = []
            for t in range(nt):
                for k in range(1, 8):
                    px, py, pc = _flip(k, x, y, c)
                    out.append((refs[t].at[2 * px + py, pc], refs[nt + t].at[k - 1], (px, py, pc)))
            return out

        flight, token = xchg_start(f"rs_x_{gname}", gs + xs, plan, 7 * nt, after=after)
        self.x[gname] = (names, flight)
        return token

    def reduce(self, gname, after):
        names, flight = self.x.pop(gname)
        nt = len(names)
        bufs = xchg_wait(flight, after)
        mine = jnp.stack([2 * lax.axis_index("x") + lax.axis_index("y"), lax.axis_index("c")]).astype(jnp.int32)
        where = [self._where(n) for n in names]
        js = [add8_into(self.J[kind], l, bufs[t], bufs[nt + t], mine, f"rs_add_{names[t]}")
              for t, (kind, l) in enumerate(where)]

        def plan(refs):
            x, y, c, _ = _place()
            out = []
            for t in range(nt):
                half = refs[t].at[where[t][1], c]
                out.append((half, half, (x, y, 1 - c)))
            return out

        flight, token = xchg_start(f"rs_join_{gname}", js, plan, nt)
        self.j[gname] = (where, flight)
        return token

    def finish(self, gname, after):
        where, flight = self.j.pop(gname)
        for (kind, _), j in zip(where, xchg_wait(flight, after)):
            self.J[kind] = j


def small_allreduce_start(v, after):
    me = 4 * lax.axis_index("x") + 2 * lax.axis_index("y") + lax.axis_index("c")
    land = lax.dynamic_update_slice(lax.empty((8,) + v.shape, v.dtype), v[None], (me, 0, 0))

    def plan(refs):
        x, y, c, _ = _place()
        return [(refs[0], refs[1].at[4 * x + 2 * y + c], _flip(k, x, y, c)) for k in range(1, 8)]

    return xchg_start("small_allreduce", [v, land], plan, 7, after=after)


def sum8(all8, name):
    def body(x_ref, o_ref):
        acc = x_ref[0]
        for d in range(1, 8):
            acc = acc + x_ref[d]
        o_ref[...] = acc

    return pl.pallas_call(
        body, name=name,
        in_specs=[pl.BlockSpec(memory_space=pltpu.VMEM)], out_specs=pl.BlockSpec(memory_space=pltpu.VMEM),
        out_shape=jax.ShapeDtypeStruct(all8.shape[1:], F32),
        compiler_params=pltpu.CompilerParams(vmem_limit_bytes=VMEM_LIMIT),
    )(all8)


AG_GROUPS = (("a0", ("pw1_0", "pw2_0", "small")), ("f0", ("up_0", "down_0")),
             ("l1", ("pw1_1", "pw2_1", "up_1", "down_1")), ("l2", ("kv", "wq_0", "wo_0", "up_2", "down_2")),
             ("l3", ("wq_1", "wo_1", "up_3", "down_3")))


def _bucket_table():
    qi = np.arange(BLK)[:, None]
    kj = np.arange(2 * BLK)[None, :]
    d = np.maximum(qi + BLK - kj, 0)
    max_exact = N_BUCKETS // 2
    log_ratio = (np.log(np.maximum(d, 1).astype(np.float32) / np.float32(max_exact))
                 / np.float32(math.log(MAX_DISTANCE / max_exact))).astype(np.float32)
    large = max_exact + (log_ratio * np.float32(N_BUCKETS - max_exact)).astype(np.int32)
    large = np.minimum(large, N_BUCKETS - 1)
    return np.where(d < max_exact, d, large).astype(np.int32)


def _heads_major(a, nh):
    T = a.shape[0]
    return a.reshape(T, nh, HD).transpose(1, 0, 2)


def _heads_minor(a):
    nh, T, _ = a.shape
    return a.transpose(1, 0, 2).reshape(T, nh * HD)


def _slots(land):
    return land.reshape(4, 2 * land.shape[2], land.shape[3])


def _rows(land):
    return land.reshape(8 * land.shape[2], land.shape[3])


def _gview(g):
    s, K, n = g.shape
    return g.reshape(4, 2, K // 2, n) if s == 4 else g.reshape(4, 2, K // 8, n)


def _gate(a, token):
    return a * (1.0 + token[0, 0])


def _conv_small(f_small):
    fs = f_small.transpose(1, 2, 0, 3).reshape(2, 40, D)
    b_pw1 = f_small[:, :, 35:37, :].transpose(1, 0, 2, 3).reshape(2, 1, 2 * D)
    rev = jnp.concatenate([fs[:, CONV_W - 1::-1], jnp.zeros((2, 40 - CONV_W, D), F32)], axis=1)
    return dict(conv=fs, conv_rev=rev, b_pw1=b_pw1, b_pw2=fs[:, 34:35])


def run_step(x, target, P, ag, rs):
    T = x.shape[0]
    zero = jnp.zeros((1, 1, D), F32)
    nm, nf = P["norm_mix"], P["norm_ffn"]
    ag.forward("a0", [ag.token])
    W = ag.get("a0", [])
    sm = _conv_small(W["small"])
    h = x
    saved = []
    for l in range(2):
        xn, u, a = norm_mm_glu(h, nm, l, _slots(W[f"pw1_{l}"]), sm["b_pw1"], f"f_pw1_{l}")
        y, s = dwconv_ln_silu(a, sm["conv"], l, f"f_conv_{l}")
        b2 = sm["b_pw2"]
        if l == 0:
            b2 = _gate(b2, ag.forward("f0", [s]))
        h1 = mm_bias_res(s, _rows(W[f"pw2_{l}"]), b2, l, h, f"f_pw2_{l}")
        if l == 0:
            W.update(ag.get("f0", [h1]))
        xn2, gu, f = norm_mm_swiglu(h1, nf, l, _slots(W[f"up_{l}"]), f"f_up_{l}")
        nxt = "l1" if l == 0 else "l2"
        h2 = mm_bias_res(f, _rows(W[f"down_{l}"]), _gate(zero, ag.forward(nxt, [f])), 0, h1, f"f_down_{l}")
        W.update(ag.get(nxt, [h2]))
        saved.append(dict(h=h, xn=xn, u=u, a=a, y=y, s=s, h1=h1, xn2=xn2, gu=gu, f=f))
        h = h2
    h_kv = h
    kvn, kv = norm_mm(h, P["norm_kv"], 0, _rows(W["kv"]), "f_kv")
    kp = jnp.pad(_heads_major(kv[:, :N_KV * HD], N_KV), ((0, 0), (BLK, 0), (0, 0)))
    vp = jnp.pad(_heads_major(kv[:, N_KV * HD:], N_KV), ((0, 0), (BLK, 0), (0, 0)))
    kvt = jnp.pad(kv.T.reshape(2, N_KV, HD, T), ((0, 0), (0, 0), (0, 0), (BLK, 0)))
    kt, vt = kvt[0], kvt[1]
    bucket = _bucket_table()
    onehot = jnp.asarray(np.eye(N_BUCKETS, dtype=np.float32)[bucket])
    bias = jnp.einsum("qkb,bh->hkq", onehot, P["rel_bias"], precision=lax.Precision.HIGHEST)
    bias = bias.reshape(N_KV, GROUP, 2 * BLK, BLK).transpose(0, 2, 1, 3).reshape(1, N_KV, 2 * BLK, QW)
    bias = bias + jnp.asarray(band_mask())[:, None]
    for j in range(2):
        l = 2 + j
        xn, q = norm_mm(h, nm, l, _rows(W[f"wq_{j}"]), f"f_q_{j}", scale=HD ** -0.5)
        qh = q.T.reshape(N_KV, GROUP, HD, T)
        sink = jnp.broadcast_to(P["sinks"][j].reshape(N_KV, GROUP, 1), (N_KV, GROUP, BLK)).reshape(N_KV, 1, QW)
        oh = attn_fwd(qh, kp, vt, bias, sink, f"f_attn_{j}")
        attn = oh.reshape(N_HEADS * HD, T).T
        h1 = mm_bias_res(attn, _rows(W[f"wo_{j}"]), zero, 0, h, f"f_wo_{j}")
        xn2, gu, f = norm_mm_swiglu(h1, nf, l, _slots(W[f"up_{l}"]), f"f_up_{l}")
        zg = _gate(zero, ag.forward("l3", [f])) if j == 0 else zero
        h2 = mm_bias_res(f, _rows(W[f"down_{l}"]), zg, 0, h1, f"f_down_{l}")
        if j == 0:
            W.update(ag.get("l3", [h2]))
        saved.append(dict(h=h, xn=xn, qh=qh, oh=oh, sink=sink, attn=attn, h1=h1, xn2=xn2, gu=gu, f=f))
        h = h2

    dh, st_final = final_loss(h, P["norm_final"], target, "loss_head")

    S = dict(norm_ffn=[None] * 4, norm_mix=[None] * 4, conv=[None] * 2, taps=[None] * 2, b_pw1=[None] * 2,
             b_pw2=[None] * 2, sinks=[None] * 2)

    def ffn_bwd(dh, sv, l, nf, after=()):
        du = mmT_swiglu_bwd(dh, _rows(W[f"down_{l}"]), sv["gu"], f"b_down_{l}", after)
        gd = mm_dw(sv["f"], dh, f"w_down_{l}", 512, 1)
        gu = mm_dw(sv["xn2"], du, f"w_up_{l}", DFF // 2, 4)
        dh, dg = mmT_rmsbwd(du, _slots(W[f"up_{l}"]), sv["h1"], nf, l, dh, f"b_up_{l}")
        S["norm_ffn"][l] = dg
        return dh, {f"down_{l}": _gview(gd), f"up_{l}": _gview(gu)}

    dk = dv = dbias = None
    sent = []
    for j in (1, 0):
        l = 2 + j
        sv = saved[l]
        dh, grads = ffn_bwd(dh, sv, l, nf, sent)
        dattn = mmT(dh, _rows(W[f"wo_{j}"]), f"b_wo_{j}")
        grads[f"wo_{j}"] = _gview(mm_dw(sv["attn"], dh, f"w_wo_{j}", 512, 1))
        doh = dattn.T.reshape(N_KV, GROUP, HD, T)
        dqh, dkj, dvj, dbj, dsj = attn_bwd(sv["qh"], kp, kt, vp, bias, sv["sink"], sv["oh"], doh, f"b_attn_{j}")
        dq = dqh.reshape(N_HEADS * HD, T).T
        grads[f"wq_{j}"] = _gview(mm_dw(sv["xn"], dq, f"w_q_{j}", 512, 1))
        dh, dg = mmT_rmsbwd(dq, _rows(W[f"wq_{j}"])[None], sv["h"], nm, l, dh, f"b_q_{j}")
        S["norm_mix"][l] = dg
        S["sinks"][j] = jnp.sum(dsj.reshape(N_HEADS, BLK), axis=1)
        dk = dkj if dk is None else dk + dkj
        dv = dvj if dv is None else dv + dvj
        dbias = dbj if dbias is None else dbias + dbj
        if j == 1:
            sent = [rs.send("l3", grads)]

    dkv = jnp.concatenate([_heads_minor(dk[:, BLK:]), _heads_minor(dv[:, BLK:])], axis=1).astype(BF16)
    grads["kv"] = _gview(mm_dw(kvn, dkv, "w_kv", 512, 1))
    dh, dg = mmT_rmsbwd(dkv, _rows(W["kv"])[None], h_kv, P["norm_kv"], 0, dh, "b_kv")
    S["norm_kv"] = dg
    dbh = dbias.reshape(N_KV, 2 * BLK, GROUP, BLK)
    S["rel_bias"] = jnp.einsum("vkgq,qkb->bvg", dbh, onehot, precision=lax.Precision.HIGHEST).reshape(N_BUCKETS, N_HEADS)
    sent = [rs.send("l2", grads)]
    nf = _gate(nf, rs.reduce("l3", [dh]))

    for l in (1, 0):
        sv = saved[l]
        dh, grads = ffn_bwd(dh, sv, l, nf, sent)
        conv = sm["conv"]
        if l == 0:
            conv = _gate(conv, rs.send("f0", grads))
            grads = {}
        dy, st = mmT_lnbwd(dh, _rows(W[f"pw2_{l}"]), sv["y"], conv, l, f"b_pw2_{l}")
        g2, S["b_pw2"][l] = mm_dw(sv["s"], dh, f"w_pw2_{l}", 512, 1, colsum=True)
        du, dtaps = dwconv_glu_bwd(dy, sv["a"], sv["u"], sm["conv"], sm["conv_rev"], l, f"b_conv_{l}")
        S["conv"][l] = st
        S["taps"][l] = dtaps
        if l == 0:
            rs.finish("l2", [du])
            nm = _gate(nm, rs.reduce("l1", [du]))
        g1, S["b_pw1"][l] = mm_dw(sv["xn"], du, f"w_pw1_{l}", 512, 4, colsum=True)
        grads[f"pw2_{l}"], grads[f"pw1_{l}"] = _gview(g2), _gview(g1)
        dh, dg = mmT_rmsbwd(du, _slots(W[f"pw1_{l}"]), sv["h"], nm, l, dh, f"b_pw1_{l}")
        S["norm_mix"][l] = dg
        if l == 1:
            sent = [rs.send("l1", grads)]
            rs.finish("l3", [dh])
            nf = _gate(nf, rs.reduce("l2", [dh]))
    S["final"] = st_final
    return grads, dh, S


R_CONV = 37
R_SMALL = 88


def _pack_small(S):
    rows = []
    for l in range(2):
        rows += [S["taps"][l], S["conv"][l], S["b_pw2"][l], S["b_pw1"][l].reshape(2, D)]
    rows += S["norm_mix"] + S["norm_ffn"] + [S["norm_kv"], S["final"]]
    tail = jnp.concatenate([jnp.stack(S["sinks"]).reshape(-1), S["rel_bias"].reshape(-1)])
    rows += [jnp.pad(tail, (0, D - tail.shape[0]))[None]]
    v = jnp.concatenate(rows, axis=0)
    return jnp.pad(v, ((0, R_SMALL - v.shape[0]), (0, 0)))


def kernel(x, norm_mix, norm_ffn, conv_w_pw1, conv_b_pw1, conv_w_dw, conv_b_dw, conv_ln_g, conv_ln_b, conv_w_pw2, conv_b_pw2, norm_kv, w_kv, w_q, w_o, sinks, rel_bias, ffn_w_up, ffn_w_down, norm_final, loss_target, m_norm_mix, m_norm_ffn, m_conv_w_pw1, m_conv_b_pw1, m_conv_w_dw, m_conv_b_dw, m_conv_ln_g, m_conv_ln_b, m_conv_w_pw2, m_conv_b_pw2, m_norm_kv, m_w_kv, m_w_q, m_w_o, m_sinks, m_rel_bias, m_ffn_w_up, m_ffn_w_down, m_norm_final, v_norm_mix, v_norm_ffn, v_conv_w_pw1, v_conv_b_pw1, v_conv_w_dw, v_conv_b_dw, v_conv_ln_g, v_conv_ln_b, v_conv_w_pw2, v_conv_b_pw2, v_norm_kv, v_w_kv, v_w_q, v_w_o, v_sinks, v_rel_bias, v_ffn_w_up, v_ffn_w_down, v_norm_final):
    me = 2 * lax.axis_index("x") + lax.axis_index("y")
    weights = dict(norm_mix=norm_mix, norm_ffn=norm_ffn, conv_w_pw1=conv_w_pw1, conv_b_pw1=conv_b_pw1,
                   conv_w_dw=conv_w_dw, conv_b_dw=conv_b_dw, conv_ln_g=conv_ln_g, conv_ln_b=conv_ln_b,
                   conv_w_pw2=conv_w_pw2, conv_b_pw2=conv_b_pw2, norm_kv=norm_kv, w_kv=w_kv, w_q=w_q, w_o=w_o,
                   sinks=sinks, rel_bias=rel_bias, ffn_w_up=ffn_w_up, ffn_w_down=ffn_w_down, norm_final=norm_final)
    mom_m = dict(norm_mix=m_norm_mix, norm_ffn=m_norm_ffn, conv_w_pw1=m_conv_w_pw1, conv_b_pw1=m_conv_b_pw1,
                 conv_w_dw=m_conv_w_dw, conv_b_dw=m_conv_b_dw, conv_ln_g=m_conv_ln_g, conv_ln_b=m_conv_ln_b,
                 conv_w_pw2=m_conv_w_pw2, conv_b_pw2=m_conv_b_pw2, norm_kv=m_norm_kv, w_kv=m_w_kv, w_q=m_w_q,
                 w_o=m_w_o, sinks=m_sinks, rel_bias=m_rel_bias, ffn_w_up=m_ffn_w_up, ffn_w_down=m_ffn_w_down,
                 norm_final=m_norm_final)
    mom_v = dict(norm_mix=v_norm_mix, norm_ffn=v_norm_ffn, conv_w_pw1=v_conv_w_pw1, conv_b_pw1=v_conv_b_pw1,
                 conv_w_dw=v_conv_w_dw, conv_b_dw=v_conv_b_dw, conv_ln_g=v_conv_ln_g, conv_ln_b=v_conv_ln_b,
                 conv_w_pw2=v_conv_w_pw2, conv_b_pw2=v_conv_b_pw2, norm_kv=v_norm_kv, w_kv=v_w_kv, w_q=v_w_q,
                 w_o=v_w_o, sinks=v_sinks, rel_bias=v_rel_bias, ffn_w_up=v_ffn_w_up, ffn_w_down=v_ffn_w_down,
                 norm_final=v_norm_final)

    big = {"conv_w_pw1": "pw1", "conv_w_pw2": "pw2", "w_q": "wq", "w_o": "wo", "ffn_w_up": "up",
           "ffn_w_down": "down", "w_kv": "kv"}
    of_kind = {k: n for n, k in big.items()}

    def source(name):
        if name == "small":
            return jnp.concatenate(
                [conv_w_dw, conv_b_dw[:, None], conv_ln_g[:, None], conv_ln_b[:, None], conv_b_pw2[:, None],
                 conv_b_pw1.reshape(2, 2, 256), jnp.zeros((2, 3, 256), F32)], axis=1), None
        kind, _, l = name.partition("_")
        return weights[of_kind[kind]], (int(l) if l else None)

    ag = WeightGather(source, AG_GROUPS)
    rs = GradReduce({"pw1": (2, 512, 512), "pw2": (2, 128, D), "wq": (2, 128, D), "wo": (2, 128, D),
                     "up": (4, 512, DFF // 2), "down": (4, DFF // 8, D), "kv": (1, 128, 512)})

    P = dict(norm_mix=norm_mix[:, None], norm_ffn=norm_ffn[:, None], norm_kv=norm_kv[None, None],
             norm_final=norm_final[None], sinks=sinks, rel_bias=rel_bias)
    last, grad_x, S = run_step(x[0], loss_target[0], P, ag, rs)

    rs.finish("l1", [grad_x])
    small_flight, token = small_allreduce_start(_gate(_pack_small(S), rs.reduce("f0", [grad_x])), [])
    token = rs.send("c0", last, after=[token])
    delta, new_m, new_v, big_grads = {}, {}, {}, {}

    def update(n):
        shp = weights[n].shape
        r2 = (int(np.prod(shp[:-1])), shp[-1])
        g, d, nm, nv = adamw(weights[n].reshape(r2), rs.J[big[n]].reshape(r2), mom_m[n].reshape(r2),
                             mom_v[n].reshape(r2), f"adamw_{n}", copy_g=True)
        big_grads[n], delta[n], new_m[n], new_v[n] = g.reshape(shp), d.reshape(shp), nm.reshape(shp), nv.reshape(shp)

    rs.finish("f0", [token])
    for n in ("ffn_w_up", "ffn_w_down"):
        update(n)
    vsum = sum8(xchg_wait(small_flight, [delta["ffn_w_up"], delta["ffn_w_down"]])[1], "small_sum")

    col = lambda a: lax.dynamic_slice_in_dim(a, me * 256, 256, axis=-1)
    grads = {}
    for l in range(2):
        base = l * R_CONV
        grads.setdefault("conv_w_dw", []).append(col(vsum[base:base + 31]))
        grads.setdefault("conv_b_dw", []).append(col(vsum[base + 31]))
        grads.setdefault("conv_ln_g", []).append(col(vsum[base + 32]))
        grads.setdefault("conv_ln_b", []).append(col(vsum[base + 33]))
        grads.setdefault("conv_b_pw2", []).append(col(vsum[base + 34]))
        grads.setdefault("conv_b_pw1", []).append(
            lax.dynamic_slice_in_dim(vsum[base + 35:base + 37].reshape(2 * D), me * 512, 512, axis=0))
    grads = {k: jnp.stack(v) for k, v in grads.items()}
    base = 2 * R_CONV
    grads["norm_mix"] = vsum[base:base + 4]
    grads["norm_ffn"] = vsum[base + 4:base + 8]
    grads["norm_kv"] = vsum[base + 8]
    grads["norm_final"] = vsum[base + 9]
    loss = vsum[base + 10, 0]
    grads["sinks"] = vsum[base + 11, 0:32].reshape(2, 16)
    grads["rel_bias"] = vsum[base + 11, 32:32 + 512].reshape(32, 16)

    for n in weights:
        if n not in big:
            shp = weights[n].shape
            r2 = (int(np.prod(shp[:-1])), shp[-1])
            d, nm, nv = adamw(weights[n].reshape(r2), grads[n].reshape(r2), mom_m[n].reshape(r2),
                              mom_v[n].reshape(r2), f"adamw_{n}")
            delta[n], new_m[n], new_v[n] = d.reshape(shp), nm.reshape(shp), nv.reshape(shp)

    rs.reduce("c0", [vsum])
    for n in ("w_q", "w_o", "w_kv"):
        update(n)
    rs.finish("c0", [delta["w_kv"]])
    for n in ("conv_w_pw1", "conv_w_pw2"):
        update(n)
    grads.update(big_grads)

    order = list(weights)
    return (loss, grad_x[None], *[grads[n] for n in order], *[delta[n] for n in order],
            *[new_m[n] for n in order], *[new_v[n] for n in order])
```

```python
import functools
import math

import numpy as np
import jax
import jax.numpy as jnp
from jax import lax
from jax.experimental import pallas as pl
from jax.experimental.pallas import tpu as pltpu

F32 = jnp.float32
BF16 = jnp.bfloat16
MESH = pl.DeviceIdType.MESH

D = 1024
DFF = 2816
N_HEADS = 16
N_KV = 4
GROUP = 4
HD = 64
BLK = 128
CONV_W = 31
HALO = 32
N_BUCKETS = 32
MAX_DISTANCE = 128
EPS = 1e-6
NEG_INF = -1e30
TM = 512
TCV = 256
VMEM_LIMIT = 56 * 2 ** 20

ADAM_LR, ADAM_B1, ADAM_B2, ADAM_EPS, ADAM_WD, ADAM_STEP = 0.001, 0.9, 0.999, 1e-08, 0.01, 10


def _cp(*sem):
    return pltpu.CompilerParams(dimension_semantics=sem, vmem_limit_bytes=VMEM_LIMIT)


def _sigmoid(x):
    return 1.0 / (1.0 + jnp.exp(-x))


def _row(tm, n):
    return pl.BlockSpec((tm, n), lambda i: (i, 0))


def _const(shape):
    nd = len(shape)
    return pl.BlockSpec(shape, lambda i: (0,) * nd)


def _weight(shape):
    nd = len(shape)
    return pl.BlockSpec(shape, lambda i: (0,) * nd, pipeline_mode=pl.Buffered(1))


def _layer(shape, l):
    nd = len(shape)
    return pl.BlockSpec((None,) + tuple(shape), lambda i: (l,) + (0,) * nd)


def _dot(a, b):
    return jnp.dot(a, b, preferred_element_type=F32)


def _dot_nt(a, b):
    return lax.dot_general(a, b, (((1,), (1,)), ((), ())), preferred_element_type=F32)


def _dot_tn(a, b):
    return lax.dot_general(a, b, (((0,), (0,)), ((), ())), preferred_element_type=F32)


def _rms(x):
    return lax.rsqrt(jnp.mean(x * x, axis=-1, keepdims=True) + EPS)


def norm_mm_glu(h, g, l, w, b, name):
    T = h.shape[0]
    ns = w.shape[-1]

    def body(h_ref, g_ref, w_ref, b_ref, xn_ref, u_ref, a_ref):
        x = h_ref[...]
        xn = (x * _rms(x) * g_ref[...]).astype(BF16)
        xn_ref[...] = xn
        for s in range(2):
            lo, hi = s * ns, (s + 1) * ns
            u1 = _dot(xn, w_ref[s]) + b_ref[:, lo:hi]
            u2 = _dot(xn, w_ref[2 + s]) + b_ref[:, D + lo:D + hi]
            u_ref[:, lo:hi] = u1.astype(BF16)
            u_ref[:, D + lo:D + hi] = u2.astype(BF16)
            a_ref[:, lo:hi] = (u1 * _sigmoid(u2)).astype(BF16)

    return pl.pallas_call(
        body, name=name, grid=(T // TM,),
        in_specs=[_row(TM, D), _layer((1, D), l), _weight((4, D, ns)), _layer((1, 2 * D), l)],
        out_specs=[_row(TM, D), _row(TM, 2 * D), _row(TM, D)],
        out_shape=[jax.ShapeDtypeStruct((T, D), BF16), jax.ShapeDtypeStruct((T, 2 * D), BF16),
                   jax.ShapeDtypeStruct((T, D), BF16)],
        compiler_params=_cp("parallel"),
    )(h, g, w, b)


SUB = 8


def _make_shifts(sh):
    n = TCV + HALO - SUB
    for r in range(1, SUB):
        for r0 in range(0, n, 40):
            sh[r, r0:r0 + 40, :] = sh[0, pl.ds(r + r0, 40), :]


def _shifted(sh, off, rows, cols):
    return sh[off % SUB, pl.ds(off - off % SUB, rows), cols]


def _conv_taps(sh, w_ref, out_ref, first):
    RB, LB = 32, 512
    for r0 in range(0, TCV, RB):
        for c0 in range(0, out_ref.shape[1], LB):
            acc = jnp.zeros((RB, LB), F32)
            for k in range(CONV_W):
                acc = acc + w_ref[k:k + 1, c0:c0 + LB] * _shifted(sh, first + k + r0, RB, slice(c0, c0 + LB))
            out_ref[r0:r0 + RB, c0:c0 + LB] = acc


def dwconv_ln_silu(a, sm, l, name):
    T = a.shape[0]
    nb = TCV // HALO

    def body(cur_ref, prev_ref, sm_ref, y_ref, s_ref, sh, yb):
        i = pl.program_id(0)
        sh[0, 0:HALO, :] = jnp.where(i > 0, prev_ref[...].astype(F32), 0.0)
        sh[0, HALO:HALO + TCV, :] = cur_ref[...].astype(F32)
        _make_shifts(sh)
        _conv_taps(sh, sm_ref, yb, HALO - (CONV_W - 1))
        y = yb[...] + sm_ref[31:32, :]
        y_ref[...] = y.astype(BF16)
        mu = jnp.mean(y, axis=-1, keepdims=True)
        yc = y - mu
        rstd = lax.rsqrt(jnp.mean(yc * yc, axis=-1, keepdims=True) + EPS)
        z = yc * rstd * sm_ref[32:33, :] + sm_ref[33:34, :]
        s_ref[...] = (z * _sigmoid(z)).astype(BF16)

    return pl.pallas_call(
        body, name=name, grid=(T // TCV,),
        in_specs=[_row(TCV, D), pl.BlockSpec((HALO, D), lambda i: (jnp.maximum(i * nb - 1, 0), 0)),
                  _layer((40, D), l)],
        out_specs=[_row(TCV, D), _row(TCV, D)],
        out_shape=[jax.ShapeDtypeStruct((T, D), BF16), jax.ShapeDtypeStruct((T, D), BF16)],
        scratch_shapes=[pltpu.VMEM((SUB, TCV + HALO, D), F32), pltpu.VMEM((TCV, D), F32)],
        compiler_params=_cp("parallel"),
    )(a, a, sm)


def mm_bias_res(xb, w, b, bl, res, name):
    T, K = xb.shape

    def body(x_ref, w_ref, b_ref, r_ref, o_ref):
        o_ref[...] = _dot(x_ref[...], w_ref[...]) + b_ref[...] + r_ref[...]

    return pl.pallas_call(
        body, name=name, grid=(T // TM,),
        in_specs=[_row(TM, K), _weight((K, D)), _layer((1, D), bl), _row(TM, D)],
        out_specs=_row(TM, D), out_shape=jax.ShapeDtypeStruct((T, D), F32),
        compiler_params=_cp("parallel"),
    )(xb, w, b, res)


def norm_mm_swiglu(h, g, l, w, name):
    T = h.shape[0]
    ns = w.shape[-1]

    def body(h_ref, g_ref, w_ref, xn_ref, gu_ref, f_ref):
        x = h_ref[...]
        xn = (x * _rms(x) * g_ref[...]).astype(BF16)
        xn_ref[...] = xn
        for s in range(2):
            lo, hi = s * ns, (s + 1) * ns
            gate = _dot(xn, w_ref[s])
            up = _dot(xn, w_ref[2 + s])
            gu_ref[:, lo:hi] = gate.astype(BF16)
            gu_ref[:, DFF + lo:DFF + hi] = up.astype(BF16)
            f_ref[:, lo:hi] = (gate * _sigmoid(gate) * up).astype(BF16)

    return pl.pallas_call(
        body, name=name, grid=(T // TM,),
        in_specs=[_row(TM, D), _layer((1, D), l), _weight((4, D, ns))],
        out_specs=[_row(TM, D), _row(TM, 2 * DFF), _row(TM, DFF)],
        out_shape=[jax.ShapeDtypeStruct((T, D), BF16), jax.ShapeDtypeStruct((T, 2 * DFF), BF16),
                   jax.ShapeDtypeStruct((T, DFF), BF16)],
        compiler_params=_cp("parallel"),
    )(h, g, w)


def norm_mm(h, g, gl, w, name, scale=1.0):
    T = h.shape[0]
    N = w.shape[-1]

    def body(h_ref, g_ref, w_ref, xn_ref, o_ref):
        x = h_ref[...]
        xn = (x * _rms(x) * g_ref[...]).astype(BF16)
        xn_ref[...] = xn
        o_ref[...] = (_dot(xn, w_ref[...]) * scale).astype(BF16)

    return pl.pallas_call(
        body, name=name, grid=(T // TM,),
        in_specs=[_row(TM, D), _layer((1, D), gl), _weight((D, N))],
        out_specs=[_row(TM, D), _row(TM, N)],
        out_shape=[jax.ShapeDtypeStruct((T, D), BF16), jax.ShapeDtypeStruct((T, N), BF16)],
        compiler_params=_cp("parallel"),
    )(h, g, w)


QB = 16
QW = GROUP * BLK


def band_mask():
    qi = np.arange(QW)[None, :] % BLK
    kj = np.arange(2 * BLK)[:, None]
    band = ((kj < BLK) & (kj > qi)) | ((kj >= BLK) & (kj - BLK <= qi))
    first = band & (kj >= BLK)
    return np.where(np.stack([first, band]), 0.0, NEG_INF).astype(np.float32)


def _softmax_cols(s, sink):
    m = jnp.maximum(jnp.max(s, axis=0, keepdims=True), sink)
    p = jnp.exp(s - m)
    es = jnp.exp(sink - m)
    inv = 1.0 / (jnp.sum(p, axis=0, keepdims=True) + es)
    return p, inv, es


def _attn_specs(T):
    W = QB * BLK
    qspec = pl.BlockSpec((None, GROUP, HD, W), lambda kv, n: (kv, 0, 0, n))
    kspec = pl.BlockSpec((None, T + BLK, HD), lambda kv, n: (kv, 0, 0))
    ktspec = [pl.BlockSpec((None, HD, W), lambda kv, n: (kv, 0, n)),
              pl.BlockSpec((None, HD, BLK), lambda kv, n: (kv, 0, (n + 1) * QB))]
    bspec = pl.BlockSpec((2, None, 2 * BLK, QW), lambda kv, n: (0, kv, 0, 0))
    sspec = pl.BlockSpec((None, 1, QW), lambda kv, n: (kv, 0, 0))
    return qspec, kspec, ktspec, bspec, sspec


def _attn_block(n, b):
    blk = n * QB + b
    rows = pl.ds(pl.multiple_of(blk * BLK, BLK), 2 * BLK)
    return rows, (jnp.minimum(blk, 1) if b == 0 else 1)


def _band_cols(main_ref, tail_ref, b):
    if b < QB - 1:
        return main_ref[:, b * BLK:(b + 2) * BLK]
    return jnp.concatenate([main_ref[:, b * BLK:], tail_ref[...]], axis=1)


def _heads_side_by_side(ref, qs):
    return jnp.concatenate([ref[g, :, qs] for g in range(GROUP)], axis=1)


def attn_fwd(q, kp, vt, bias, sink, name):
    T = q.shape[3]
    qspec, kspec, ktspec, bspec, sspec = _attn_specs(T)

    def body(q_ref, k_ref, vt_ref, vtt_ref, b_ref, s_ref, o_ref, pb):
        n = pl.program_id(1)

        def scores(b):
            return _dot(k_ref[_attn_block(n, b)[0], :], _heads_side_by_side(q_ref, slice(b * BLK, (b + 1) * BLK)))

        st_next = scores(0)
        for b in range(QB):
            rows, table = _attn_block(n, b)
            qs = slice(b * BLK, (b + 1) * BLK)
            st = st_next
            if b + 1 < QB:
                st_next = scores(b + 1)
            for g in range(GROUP):
                hs = slice(g * BLK, (g + 1) * BLK)
                p, inv, _ = _softmax_cols(st[:, hs] + b_ref[table, :, hs], s_ref[:, hs])
                pb[:, hs] = (p * inv).astype(BF16)
            ot = _dot(_band_cols(vt_ref, vtt_ref, b), pb[...])
            for g in range(GROUP):
                o_ref[g, :, qs] = ot[:, g * BLK:(g + 1) * BLK].astype(BF16)

    return pl.pallas_call(
        body, name=name, grid=(N_KV, T // (QB * BLK)),
        in_specs=[qspec, kspec, *ktspec, bspec, sspec], out_specs=qspec,
        out_shape=jax.ShapeDtypeStruct((N_KV, GROUP, HD, T), BF16),
        scratch_shapes=[pltpu.VMEM((2 * BLK, QW), BF16)],
        compiler_params=_cp("parallel", "parallel"),
    )(q, kp, vt, vt, bias, sink)


def attn_bwd(q, kp, kt, vp, bias, sink, o, do, name, dkv=None):
    T = q.shape[3]
    qspec, kspec, ktspec, bspec, sspec = _attn_specs(T)

    def body(q_ref, k_ref, kt_ref, ktt_ref, v_ref, b_ref, s_ref, o_ref, do_ref, *rest):
        dq_ref, dk_ref, dv_ref, db_ref, ds_ref, pb, dsb = rest[-7:]
        n = pl.program_id(1)

        @pl.when(n == 0)
        def _():
            dk_ref[...] = jnp.zeros_like(dk_ref) if dkv is None else rest[0][...]
            dv_ref[...] = jnp.zeros_like(dv_ref) if dkv is None else rest[1][...]
            db_ref[...] = jnp.zeros_like(db_ref)
            ds_ref[...] = jnp.zeros_like(ds_ref)

        def products(b):
            rows = _attn_block(n, b)[0]
            qs = slice(b * BLK, (b + 1) * BLK)
            q4, do4 = _heads_side_by_side(q_ref, qs), _heads_side_by_side(do_ref, qs)
            return q4, do4, _dot(k_ref[rows, :], q4), _dot(v_ref[rows, :], do4)

        ahead = products(0)
        for b in range(QB):
            rows, table = _attn_block(n, b)
            qs = slice(b * BLK, (b + 1) * BLK)
            q4, do4, st, dpt = ahead
            if b + 1 < QB:
                ahead = products(b + 1)
            for g in range(GROUP):
                hs = slice(g * BLK, (g + 1) * BLK)
                p, inv, es = _softmax_cols(st[:, hs] + b_ref[table, :, hs], s_ref[:, hs])
                probs = p * inv
                delta = jnp.sum(do_ref[g, :, qs].astype(F32) * o_ref[g, :, qs].astype(F32), axis=0, keepdims=True)
                dS = probs * (dpt[:, hs] - delta)
                ds_ref[:, hs] += -(es * inv) * delta
                db_ref[:, hs] += dS
                pb[:, hs] = probs.astype(BF16)
                dsb[:, hs] = dS.astype(BF16)
            dqt = _dot(_band_cols(kt_ref, ktt_ref, b), dsb[...]) * (HD ** -0.5)
            for g in range(GROUP):
                dq_ref[g, :, qs] = dqt[:, g * BLK:(g + 1) * BLK].astype(BF16)
            dk_ref[rows, :] += _dot_nt(dsb[...], q4)
            dv_ref[rows, :] += _dot_nt(pb[...], do4)

    kout = pl.BlockSpec((None, T + BLK, HD), lambda kv, n: (kv, 0, 0))
    dbspec = pl.BlockSpec((None, 2 * BLK, QW), lambda kv, n: (kv, 0, 0))
    return pl.pallas_call(
        body, name=name, grid=(N_KV, T // (QB * BLK)),
        in_specs=[qspec, kspec, *ktspec, kspec, bspec, sspec, qspec, qspec] + ([] if dkv is None else [kout, kout]),
        out_specs=[qspec, kout, kout, dbspec, sspec],
        out_shape=[jax.ShapeDtypeStruct((N_KV, GROUP, HD, T), BF16),
                   jax.ShapeDtypeStruct((N_KV, T + BLK, HD), F32), jax.ShapeDtypeStruct((N_KV, T + BLK, HD), F32),
                   jax.ShapeDtypeStruct((N_KV, 2 * BLK, QW), F32), jax.ShapeDtypeStruct((N_KV, 1, QW), F32)],
        scratch_shapes=[pltpu.VMEM((2 * BLK, QW), BF16), pltpu.VMEM((2 * BLK, QW), BF16)],
        compiler_params=_cp("parallel", "arbitrary"),
    )(q, kp, kt, kt, vp, bias, sink, o, do, *(dkv or ()))


def final_loss(h, g, target, name):
    T = h.shape[0]

    def body(h_ref, g_ref, t_ref, dh_ref, st_ref):
        i = pl.program_id(0)

        @pl.when(i == 0)
        def _():
            st_ref[...] = jnp.zeros_like(st_ref)

        x = h_ref[...]
        r = _rms(x)
        xh = x * r
        e = xh * g_ref[...] - t_ref[...]
        loss = 0.5 * jnp.sum(jnp.mean(e * e, axis=-1, keepdims=True))
        dy = e * (1.0 / D)
        st_ref[0:1, :] += jnp.sum(dy * xh, axis=0, keepdims=True)
        lane = lax.broadcasted_iota(jnp.int32, (1, D), 1)
        st_ref[1:2, :] += jnp.where(lane == 0, loss, 0.0)
        dxh = dy * g_ref[...]
        dh_ref[...] = r * (dxh - xh * jnp.mean(dxh * xh, axis=-1, keepdims=True))

    return pl.pallas_call(
        body, name=name, grid=(T // TM,),
        in_specs=[_row(TM, D), _const((1, D)), _row(TM, D)],
        out_specs=[_row(TM, D), _const((2, D))],
        out_shape=[jax.ShapeDtypeStruct((T, D), F32), jax.ShapeDtypeStruct((2, D), F32)],
        compiler_params=_cp("arbitrary"),
    )(h, g, target)


def mm_dw(x, dy, name, tn, slots, colsum=False):
    T, K = x.shape
    split = dy.ndim == 3
    N = dy.shape[-1] * (2 if split else 1)
    tt = min(T, 2048 if K <= 1024 else 1024)
    nt = T // tt
    ns = N // slots
    per = ns // tn

    def body(x_ref, dy_ref, *rest):
        if colsum:
            dw_ref, cs_ref, acc, cacc = rest
        else:
            dw_ref, acc = rest
        t = pl.program_id(1)

        @pl.when(t == 0)
        def _():
            acc[...] = jnp.zeros_like(acc)
            if colsum:
                cacc[...] = jnp.zeros_like(cacc)

        dyv = dy_ref[...]
        acc[...] += _dot_tn(x_ref[...].astype(BF16), dyv.astype(BF16))
        if colsum:
            cacc[...] += jnp.sum(dyv.astype(F32), axis=0, keepdims=True)

        @pl.when(t == nt - 1)
        def _():
            dw_ref[...] = acc[...].astype(BF16)
            if colsum:
                cs_ref[...] = cacc[...]

    if split:
        half = N // 2 // tn
        dy_spec = pl.BlockSpec((None, tt, tn), lambda j, t: (j // half, t, j % half))
    else:
        dy_spec = pl.BlockSpec((tt, tn), lambda j, t: (t, j))
    out_specs = [pl.BlockSpec((None, K, tn), lambda j, t: (j // per, 0, j % per))]
    out_shape = [jax.ShapeDtypeStruct((slots, K, ns), BF16)]
    scratch = [pltpu.VMEM((K, tn), F32)]
    if colsum:
        out_specs.append(pl.BlockSpec((1, tn), lambda j, t: (0, j)))
        out_shape.append(jax.ShapeDtypeStruct((1, N), F32))
        scratch.append(pltpu.VMEM((1, tn), F32))
    res = pl.pallas_call(
        body, name=name, grid=(N // tn, nt),
        in_specs=[pl.BlockSpec((tt, K), lambda j, t: (t, 0)), dy_spec],
        out_specs=out_specs, out_shape=out_shape, scratch_shapes=scratch,
        compiler_params=_cp("parallel", "arbitrary"),
    )(x, dy)
    return tuple(res) if colsum else res[0]


def mmT_swiglu_bwd(dh, w, gu, name, after=()):
    T = dh.shape[0]
    cw = 256

    def body(dh_ref, w_ref, gu_ref, *rest):
        du_ref = rest[-1]
        dhb = dh_ref[...].astype(BF16)
        ahead = _dot_nt(dhb, w_ref[0:cw, :])
        for lo in range(0, DFF, cw):
            hi = lo + cw
            df = ahead
            if hi < DFF:
                ahead = _dot_nt(dhb, w_ref[hi:hi + cw, :])
            gate = gu_ref[:, lo:hi].astype(F32)
            up = gu_ref[:, DFF + lo:DFF + hi].astype(F32)
            sg = _sigmoid(gate)
            silu = gate * sg
            du_ref[:, lo:hi] = (df * (up * (sg + silu * (1.0 - sg)))).astype(BF16)
            du_ref[:, DFF + lo:DFF + hi] = (df * silu).astype(BF16)

    return pl.pallas_call(
        body, name=name, grid=(T // TM,),
        in_specs=[_row(TM, D), _weight((DFF, D)), _row(TM, 2 * DFF)] + [ANY] * len(after),
        out_specs=_row(TM, 2 * DFF), out_shape=jax.ShapeDtypeStruct((T, 2 * DFF), BF16),
        compiler_params=_cp("parallel"),
    )(dh, w, gu, *after)


def mmT_rmsbwd(du, w, h, g, gl, dh_in, name):
    split = du.ndim == 3
    T = du.shape[-2]
    N = du.shape[-1] * (2 if split else 1)
    slots = w.shape[0]
    ns = N // slots

    RH = TM // 2

    def piece(du_ref, s, rows):
        if split:
            per = slots // 2
            return du_ref[s // per, rows, (s % per) * ns:(s % per + 1) * ns]
        return du_ref[rows, s * ns:(s + 1) * ns]

    def body(du_ref, w_ref, h_ref, g_ref, di_ref, dh_ref, dg_ref):
        i = pl.program_id(0)

        @pl.when(i == 0)
        def _():
            dg_ref[...] = jnp.zeros_like(dg_ref)

        def products(k):
            rows = slice(k * RH, (k + 1) * RH)
            dxn = _dot_nt(piece(du_ref, 0, rows), w_ref[0])
            for s in range(1, slots):
                dxn = dxn + _dot_nt(piece(du_ref, s, rows), w_ref[s])
            return dxn

        ahead = products(0)
        for k in range(TM // RH):
            rows = slice(k * RH, (k + 1) * RH)
            dxn = ahead
            if (k + 1) * RH < TM:
                ahead = products(k + 1)
            x = h_ref[rows, :]
            r = _rms(x)
            xh = x * r
            dg_ref[0:1, :] += jnp.sum(dxn * xh, axis=0, keepdims=True)
            dxh = dxn * g_ref[...]
            dh_ref[rows, :] = di_ref[rows, :] + r * (dxh - xh * jnp.mean(dxh * xh, axis=-1, keepdims=True))

    return pl.pallas_call(
        body, name=name, grid=(T // TM,),
        in_specs=[pl.BlockSpec((2, TM, N // 2), lambda i: (0, i, 0)) if split else _row(TM, N),
                  _weight((slots, D, ns)), _row(TM, D), _layer((1, D), gl), _row(TM, D)],
        out_specs=[_row(TM, D), _const((1, D))],
        out_shape=[jax.ShapeDtypeStruct((T, D), F32), jax.ShapeDtypeStruct((1, D), F32)],
        compiler_params=_cp("arbitrary"),
    )(du, w, h, g, dh_in)


def mmT(dh, w, name):
    T = dh.shape[0]
    N = w.shape[0]

    def body(dh_ref, w_ref, o_ref):
        o_ref[...] = _dot_nt(dh_ref[...].astype(BF16), w_ref[...]).astype(BF16)

    return pl.pallas_call(
        body, name=name, grid=(T // TM,),
        in_specs=[_row(TM, D), _weight((N, D))],
        out_specs=_row(TM, N), out_shape=jax.ShapeDtypeStruct((T, N), BF16),
        compiler_params=_cp("parallel"),
    )(dh, w)


def mmT_lnbwd(dh, w, y, sm, l, name):
    T = dh.shape[0]

    def body(dh_ref, w_ref, y_ref, sm_ref, dy_ref, st_ref):
        i = pl.program_id(0)

        @pl.when(i == 0)
        def _():
            st_ref[...] = jnp.zeros_like(st_ref)

        ds = _dot_nt(dh_ref[...].astype(BF16), w_ref[...])
        y = y_ref[...].astype(F32)
        mu = jnp.mean(y, axis=-1, keepdims=True)
        yc = y - mu
        rstd = lax.rsqrt(jnp.mean(yc * yc, axis=-1, keepdims=True) + EPS)
        xh = yc * rstd
        gam = sm_ref[32:33, :]
        z = xh * gam + sm_ref[33:34, :]
        sg = _sigmoid(z)
        dz = ds * sg * (1.0 + z * (1.0 - sg))
        st_ref[1:2, :] += jnp.sum(dz * xh, axis=0, keepdims=True)
        st_ref[2:3, :] += jnp.sum(dz, axis=0, keepdims=True)
        dxh = dz * gam
        dy = rstd * (dxh - jnp.mean(dxh, axis=-1, keepdims=True) - xh * jnp.mean(dxh * xh, axis=-1, keepdims=True))
        st_ref[0:1, :] += jnp.sum(dy, axis=0, keepdims=True)
        dy_ref[...] = dy.astype(BF16)

    return pl.pallas_call(
        body, name=name, grid=(T // TM,),
        in_specs=[_row(TM, D), _weight((D, D)), _row(TM, D), _layer((40, D), l)],
        out_specs=[_row(TM, D), _const((3, D))],
        out_shape=[jax.ShapeDtypeStruct((T, D), BF16), jax.ShapeDtypeStruct((3, D), F32)],
        compiler_params=_cp("arbitrary"),
    )(dh, w, y, sm)


CH = 512


def dwconv_glu_bwd(dy, a, u, sm, smrev, l, name):
    T = dy.shape[0]
    nr, nc = T // TCV, D // CH
    nb = TCV // HALO
    last = T // HALO - 1

    def body(dy_ref, dyn_ref, a_ref, ap_ref, u1_ref, u2_ref, sm_ref, rev_ref, du_ref, dw_ref, shd, sha, da):
        i = pl.program_id(0)
        r = i % nr

        @pl.when(r == 0)
        def _():
            dw_ref[...] = jnp.zeros_like(dw_ref)

        shd[0, 0:TCV, :] = dy_ref[...].astype(F32)
        shd[0, TCV:TCV + HALO, :] = jnp.where(r < nr - 1, dyn_ref[...].astype(F32), 0.0)
        sha[0, 0:HALO, :] = jnp.where(r > 0, ap_ref[...].astype(F32), 0.0)
        sha[0, HALO:HALO + TCV, :] = a_ref[...].astype(F32)
        _make_shifts(shd)
        _make_shifts(sha)
        _conv_taps(shd, rev_ref, da, 0)
        for kg in range(0, CONV_W, SUB):
            taps = range(kg, min(kg + SUB, CONV_W))
            part = [jnp.zeros((SUB, CH), F32) for _ in taps]
            for r0 in range(0, TCV, SUB):
                d = shd[0, r0:r0 + SUB, :]
                for j, k in enumerate(taps):
                    part[j] = part[j] + d * _shifted(sha, HALO - (CONV_W - 1) + k + r0, SUB, slice(None))
            for j, k in enumerate(taps):
                dw_ref[k:k + 1, :] += jnp.sum(part[j], axis=0, keepdims=True)
        dav = da[...]
        u1 = u1_ref[...].astype(F32)
        sg = _sigmoid(u2_ref[...].astype(F32))
        du_ref[0] = (dav * sg).astype(BF16)
        du_ref[1] = (dav * u1 * sg * (1.0 - sg)).astype(BF16)

    tile = lambda i: (i % nr, i // nr)
    in_specs = [pl.BlockSpec((TCV, CH), tile),
                pl.BlockSpec((HALO, CH), lambda i: (jnp.minimum((i % nr + 1) * nb, last), i // nr)),
                pl.BlockSpec((TCV, CH), tile),
                pl.BlockSpec((HALO, CH), lambda i: (jnp.maximum((i % nr) * nb - 1, 0), i // nr)),
                pl.BlockSpec((TCV, CH), tile), pl.BlockSpec((TCV, CH), lambda i: (i % nr, nc + i // nr)),
                pl.BlockSpec((None, 40, CH), lambda i: (l, 0, i // nr)),
                pl.BlockSpec((None, 40, CH), lambda i: (l, 0, i // nr))]
    return pl.pallas_call(
        body, name=name, grid=(nr * nc,), in_specs=in_specs,
        out_specs=[pl.BlockSpec((2, TCV, CH), lambda i: (0, i % nr, i // nr)),
                   pl.BlockSpec((CONV_W, CH), lambda i: (0, i // nr))],
        out_shape=[jax.ShapeDtypeStruct((2, T, D), BF16), jax.ShapeDtypeStruct((CONV_W, D), F32)],
        scratch_shapes=[pltpu.VMEM((SUB, TCV + HALO, CH), F32), pltpu.VMEM((SUB, TCV + HALO, CH), F32),
                        pltpu.VMEM((TCV, CH), F32)],
        compiler_params=_cp("arbitrary"),
    )(dy, dy, a, a, u, u, sm, smrev)


def _rows_tile(R):
    for t in (512, 256, 128, 64, 32, 16, 8):
        if R % t == 0:
            return t
    return R


def add8_into(J, l, g, others, where, name):
    R, C = g.shape[2:]
    tr = R // 2

    def body(w_ref, g_ref, x_ref, j_in, j_ref):
        acc = g_ref[...].astype(F32)
        for k in range(7):
            acc = acc + x_ref[k].astype(F32)
        j_ref[...] = acc

    return pl.pallas_call(
        body, name=name,
        grid_spec=pltpu.PrefetchScalarGridSpec(
            num_scalar_prefetch=1, grid=(R // tr,),
            in_specs=[pl.BlockSpec((None, None, tr, C), lambda i, w: (w[0], w[1], i, 0)),
                      pl.BlockSpec((7, tr, C), lambda i, w: (0, i, 0)), ANY],
            out_specs=pl.BlockSpec((None, None, tr, C), lambda i, w: (l, w[1], i, 0))),
        out_shape=jax.ShapeDtypeStruct(J.shape, F32), input_output_aliases={3: 0},
        compiler_params=_cp("parallel"),
    )(where, g, others, J)


def adamw(w, g, m, v, name, copy_g=False):
    R, C = w.shape
    tr = _rows_tile(R)

    def body(w_ref, g_ref, m_ref, v_ref, *outs):
        d_ref, nm_ref, nv_ref = outs[-3:]
        gv = g_ref[...]
        if copy_g:
            outs[0][...] = gv
        nm = ADAM_B1 * m_ref[...] + (1.0 - ADAM_B1) * gv
        nv = ADAM_B2 * v_ref[...] + (1.0 - ADAM_B2) * (gv * gv)
        m_hat = nm / (1.0 - ADAM_B1 ** ADAM_STEP)
        v_hat = nv / (1.0 - ADAM_B2 ** ADAM_STEP)
        d_ref[...] = -ADAM_LR * (m_hat / (jnp.sqrt(v_hat) + ADAM_EPS) + ADAM_WD * w_ref[...])
        nm_ref[...] = nm
        nv_ref[...] = nv

    sd = jax.ShapeDtypeStruct((R, C), F32)
    n_out = 4 if copy_g else 3
    return pl.pallas_call(
        body, name=name, grid=(R // tr,),
        in_specs=[_row(tr, C)] * 4, out_specs=[_row(tr, C)] * n_out, out_shape=[sd] * n_out,
        compiler_params=_cp("parallel"),
    )(w, g, m, v)


ANY = pl.BlockSpec(memory_space=pl.ANY)
HBM = pl.BlockSpec(memory_space=pltpu.HBM)
SEM = pl.BlockSpec(memory_space=pltpu.SEMAPHORE)
EFFECT = pltpu.SideEffectType.DATAFLOW_SIDE_EFFECTING


def _place():
    x, y, c = lax.axis_index("x"), lax.axis_index("y"), lax.axis_index("c")
    chips = [(1 - x, y), (x, 1 - y), (1 - x, 1 - y)]
    return x, y, c, chips


def _copy(src, dst, send, recv, k, to):
    return pltpu.make_async_remote_copy(src_ref=src, dst_ref=dst, send_sem=send.at[k], recv_sem=recv.at[k],
                                        device_id=to, device_id_type=MESH)


def xchg_start(name, bufs, plan, n, after=()):
    nb = len(bufs)

    na = len(after)

    def body(*refs):
        send, recv, token = refs[nb + na], refs[nb + na + 1], refs[-1]
        for k, (src, dst, to) in enumerate(plan(refs[:nb])):
            _copy(src, dst, send, recv, k, to).start()
        token[...] = jnp.zeros_like(token)

    outs = pl.pallas_call(
        body, name=name,
        out_shape=(pltpu.SemaphoreType.DMA((n,)), pltpu.SemaphoreType.DMA((n,)),
                   *[pltpu.HBM(b.shape, b.dtype) for b in bufs], jax.ShapeDtypeStruct((8, 128), F32)),
        in_specs=[HBM] * nb + [ANY] * na,
        out_specs=(SEM, SEM, *[HBM] * nb, pl.BlockSpec(memory_space=pltpu.VMEM)),
        input_output_aliases={i: 2 + i for i in range(nb)},
        compiler_params=pltpu.CompilerParams(has_side_effects=EFFECT),
    )(*[pltpu.with_memory_space_constraint(b, pltpu.HBM) for b in bufs], *after)
    return dict(name=name, send=outs[0], recv=outs[1], bufs=list(outs[2:2 + nb]), plan=plan), outs[-1]


def xchg_wait(flight, after):
    bufs, plan = flight["bufs"], flight["plan"]
    nb = len(bufs)

    def body(*refs):
        send, recv = refs[nb], refs[nb + 1]
        for k, (src, dst, to) in enumerate(plan(refs[:nb])):
            cp = _copy(src, dst, send, recv, k, to)
            cp.wait_send()
            cp.wait_recv()

    outs = pl.pallas_call(
        body, name=flight["name"] + "_wait",
        out_shape=tuple(pltpu.HBM(b.shape, b.dtype) for b in bufs),
        in_specs=[HBM] * nb + [SEM, SEM] + [ANY] * len(after),
        out_specs=tuple([HBM] * nb), input_output_aliases={i: i for i in range(nb)},
        compiler_params=pltpu.CompilerParams(has_side_effects=EFFECT),
    )(*bufs, flight["send"], flight["recv"], *after)
    return list(outs)


def _flip(k, x, y, c):
    return ((1 - x) if k & 4 else x, (1 - y) if k & 2 else y, (1 - c) if k & 1 else c)


def cast_into_slot(srcs, name, after):
    me = (2 * lax.axis_index("x") + lax.axis_index("y")).astype(jnp.int32).reshape(1)
    ns = len(srcs)

    def body(me_ref, *refs):
        outs = refs[ns + len(after):]
        for t in range(ns):
            outs[t][...] = refs[t][...].astype(outs[t].dtype).reshape(outs[t].shape)

    in_specs, out_specs, out_shape = [], [], []
    for arr, l in srcs:
        if l is None:
            in_specs.append(pl.BlockSpec(arr.shape, lambda i, w, nd=arr.ndim: (0,) * nd))
            a2, b, dt = (arr.shape[0] // 2, arr.shape[1], BF16) if arr.ndim == 2 else (arr.shape[1], arr.shape[2], F32)
        else:
            in_specs.append(pl.BlockSpec((None,) + arr.shape[1:], lambda i, w, l=l: (l, 0, 0)))
            a2, b, dt = arr.shape[1] // 2, arr.shape[2], BF16
        out_specs.append(pl.BlockSpec((None, 2, a2, b), lambda i, w: (w[0], 0, 0, 0)))
        out_shape.append(jax.ShapeDtypeStruct((4, 2, a2, b), dt))
    in_specs += [ANY] * len(after)
    return pl.pallas_call(
        body, name=name,
        grid_spec=pltpu.PrefetchScalarGridSpec(num_scalar_prefetch=1, grid=(1,), in_specs=in_specs,
                                               out_specs=out_specs),
        out_shape=out_shape, compiler_params=_cp("arbitrary"),
    )(me, *[arr for arr, _ in srcs], *after)


class WeightGather:
    def __init__(self, source, groups):
        self.names = dict(groups)
        self.ici, self.d2d = {}, {}
        self.token = None
        for gname, names in groups:
            nt = len(names)
            after = [] if self.token is None else [self.token]
            lands = cast_into_slot([source(n) for n in names], f"ag_cast_{gname}", after)

            def plan(refs, nt=nt):
                x, y, c, chips = _place()
                out = []
                for t in range(nt):
                    mine = refs[t].at[2 * x + y, c]
                    out += [(mine, mine, (cx, cy, c)) for cx, cy in chips]
                return out

            self.ici[gname], self.token = xchg_start(f"ag_ici_{gname}", lands, plan, 3 * nt, after=after)

    def forward(self, gname, after):
        nt = len(self.names[gname])
        lands = xchg_wait(self.ici.pop(gname), after)

        def plan(refs):
            x, y, c, chips = _place()
            out = []
            for t in range(nt):
                for cx, cy in chips:
                    piece = refs[t].at[2 * cx + cy, c]
                    out.append((piece, piece, (x, y, 1 - c)))
            return out

        self.d2d[gname], token = xchg_start(f"ag_d2d_{gname}", lands, plan, 3 * nt)
        return token

    def get(self, gname, after):
        lands = xchg_wait(self.d2d.pop(gname), after)
        return dict(zip(self.names[gname], lands))


class GradReduce:
    def __init__(self, kinds):
        self.J = {k: lax.empty((L, 2, a2, b), F32) for k, (L, a2, b) in kinds.items()}
        self.x, self.j = {}, {}

    @staticmethod
    def _where(name):
        kind, _, l = name.partition("_")
        return kind, int(l or 0)

    def send(self, gname, grads, after=()):
        names = list(grads)
        nt = len(names)
        gs = [grads[n] for n in names]
        xs = [lax.empty((7,) + g.shape[2:], g.dtype) for g in gs]

        def plan(refs):
            x, y, c, _ = _place()
            out = []
            for t in range(nt):
                for k in range(1, 8):
                    px, py, pc = _flip(k, x, y, c)
                    out.append((refs[t].at[2 * px + py, pc], refs[nt + t].at[k - 1], (px, py, pc)))
            return out

        flight, token = xchg_start(f"rs_x_{gname}", gs + xs, plan, 7 * nt, after=after)
        self.x[gname] = (names, flight)
        return token

    def reduce(self, gname, after):
        names, flight = self.x.pop(gname)
        nt = len(names)
        bufs = xchg_wait(flight, after)
        mine = jnp.stack([2 * lax.axis_index("x") + lax.axis_index("y"), lax.axis_index("c")]).astype(jnp.int32)
        where = [self._where(n) for n in names]
        js = [add8_into(self.J[kind], l, bufs[t], bufs[nt + t], mine, f"rs_add_{names[t]}")
              for t, (kind, l) in enumerate(where)]

        def plan(refs):
            x, y, c, _ = _place()
            out = []
            for t in range(nt):
                half = refs[t].at[where[t][1], c]
                out.append((half, half, (x, y, 1 - c)))
            return out

        flight, token = xchg_start(f"rs_join_{gname}", js, plan, nt)
        self.j[gname] = (where, flight)
        return token

    def finish(self, gname, after):
        where, flight = self.j.pop(gname)
        for (kind, _), j in zip(where, xchg_wait(flight, after)):
            self.J[kind] = j


def small_allreduce_start(v, after):
    me = 4 * lax.axis_index("x") + 2 * lax.axis_index("y") + lax.axis_index("c")
    land = lax.dynamic_update_slice(lax.empty((8,) + v.shape, v.dtype), v[None], (me, 0, 0))

    def plan(refs):
        x, y, c, _ = _place()
        return [(refs[0], refs[1].at[4 * x + 2 * y + c], _flip(k, x, y, c)) for k in range(1, 8)]

    return xchg_start("small_allreduce", [v, land], plan, 7, after=after)


def sum8(all8, name):
    def body(x_ref, o_ref):
        acc = x_ref[0]
        for d in range(1, 8):
            acc = acc + x_ref[d]
        o_ref[...] = acc

    return pl.pallas_call(
        body, name=name,
        in_specs=[pl.BlockSpec(memory_space=pltpu.VMEM)], out_specs=pl.BlockSpec(memory_space=pltpu.VMEM),
        out_shape=jax.ShapeDtypeStruct(all8.shape[1:], F32),
        compiler_params=pltpu.CompilerParams(vmem_limit_bytes=VMEM_LIMIT),
    )(all8)


AG_GROUPS = (("a0", ("pw1_0", "pw2_0", "small")), ("f0", ("up_0", "down_0")),
             ("l1", ("pw1_1", "pw2_1", "up_1", "down_1")), ("l2", ("kv", "wq_0", "wo_0", "up_2", "down_2")),
             ("l3", ("wq_1", "wo_1", "up_3", "down_3")))


def _bucket_table():
    qi = np.arange(BLK)[:, None]
    kj = np.arange(2 * BLK)[None, :]
    d = np.maximum(qi + BLK - kj, 0)
    max_exact = N_BUCKETS // 2
    log_ratio = (np.log(np.maximum(d, 1).astype(np.float32) / np.float32(max_exact))
                 / np.float32(math.log(MAX_DISTANCE / max_exact))).astype(np.float32)
    large = max_exact + (log_ratio * np.float32(N_BUCKETS - max_exact)).astype(np.int32)
    large = np.minimum(large, N_BUCKETS - 1)
    return np.where(d < max_exact, d, large).astype(np.int32)


def _heads_major(a, nh):
    T = a.shape[0]
    return a.reshape(T, nh, HD).transpose(1, 0, 2)


def _heads_minor(a):
    nh, T, _ = a.shape
    return a.transpose(1, 0, 2).reshape(T, nh * HD)


def _slots(land):
    return land.reshape(4, 2 * land.shape[2], land.shape[3])


def _rows(land):
    return land.reshape(8 * land.shape[2], land.shape[3])


def _gview(g):
    s, K, n = g.shape
    return g.reshape(4, 2, K // 2, n) if s == 4 else g.reshape(4, 2, K // 8, n)


def _gate(a, token):
    return a * (1.0 + token[0, 0])


def _conv_small(f_small):
    fs = f_small.transpose(1, 2, 0, 3).reshape(2, 40, D)
    b_pw1 = f_small[:, :, 35:37, :].transpose(1, 0, 2, 3).reshape(2, 1, 2 * D)
    rev = jnp.concatenate([fs[:, CONV_W - 1::-1], jnp.zeros((2, 40 - CONV_W, D), F32)], axis=1)
    return dict(conv=fs, conv_rev=rev, b_pw1=b_pw1, b_pw2=fs[:, 34:35])


def run_step(x, target, P, ag, rs):
    T = x.shape[0]
    zero = jnp.zeros((1, 1, D), F32)
    nm, nf = P["norm_mix"], P["norm_ffn"]
    ag.forward("a0", [ag.token])
    W = ag.get("a0", [])
    sm = _conv_small(W["small"])
    h = x
    saved = []
    for l in range(2):
        xn, u, a = norm_mm_glu(h, nm, l, _slots(W[f"pw1_{l}"]), sm["b_pw1"], f"f_pw1_{l}")
        y, s = dwconv_ln_silu(a, sm["conv"], l, f"f_conv_{l}")
        b2 = sm["b_pw2"]
        if l == 0:
            b2 = _gate(b2, ag.forward("f0", [s]))
        h1 = mm_bias_res(s, _rows(W[f"pw2_{l}"]), b2, l, h, f"f_pw2_{l}")
        if l == 0:
            W.update(ag.get("f0", [h1]))
        xn2, gu, f = norm_mm_swiglu(h1, nf, l, _slots(W[f"up_{l}"]), f"f_up_{l}")
        nxt = "l1" if l == 0 else "l2"
        h2 = mm_bias_res(f, _rows(W[f"down_{l}"]), _gate(zero, ag.forward(nxt, [f])), 0, h1, f"f_down_{l}")
        W.update(ag.get(nxt, [h2]))
        saved.append(dict(h=h, xn=xn, u=u, a=a, y=y, s=s, h1=h1, xn2=xn2, gu=gu, f=f))
        h = h2
    h_kv = h
    kvn, kv = norm_mm(h, P["norm_kv"], 0, _rows(W["kv"]), "f_kv")
    kp = jnp.pad(_heads_major(kv[:, :N_KV * HD], N_KV), ((0, 0), (BLK, 0), (0, 0)))
    vp = jnp.pad(_heads_major(kv[:, N_KV * HD:], N_KV), ((0, 0), (BLK, 0), (0, 0)))
    kvt = jnp.pad(kv.T.reshape(2, N_KV, HD, T), ((0, 0), (0, 0), (0, 0), (BLK, 0)))
    kt, vt = kvt[0], kvt[1]
    bucket = _bucket_table()
    onehot = jnp.asarray(np.eye(N_BUCKETS, dtype=np.float32)[bucket])
    bias = jnp.einsum("qkb,bh->hkq", onehot, P["rel_bias"], precision=lax.Precision.HIGHEST)
    bias = bias.reshape(N_KV, GROUP, 2 * BLK, BLK).transpose(0, 2, 1, 3).reshape(1, N_KV, 2 * BLK, QW)
    bias = bias + jnp.asarray(band_mask())[:, None]
    for j in range(2):
        l = 2 + j
        xn, q = norm_mm(h, nm, l, _rows(W[f"wq_{j}"]), f"f_q_{j}", scale=HD ** -0.5)
        qh = q.T.reshape(N_KV, GROUP, HD, T)
        sink = jnp.broadcast_to(P["sinks"][j].reshape(N_KV, GROUP, 1), (N_KV, GROUP, BLK)).reshape(N_KV, 1, QW)
        oh = attn_fwd(qh, kp, vt, bias, sink, f"f_attn_{j}")
        attn = oh.reshape(N_HEADS * HD, T).T
        h1 = mm_bias_res(attn, _rows(W[f"wo_{j}"]), zero, 0, h, f"f_wo_{j}")
        xn2, gu, f = norm_mm_swiglu(h1, nf, l, _slots(W[f"up_{l}"]), f"f_up_{l}")
        zg = _gate(zero, ag.forward("l3", [f])) if j == 0 else zero
        h2 = mm_bias_res(f, _rows(W[f"down_{l}"]), zg, 0, h1, f"f_down_{l}")
        if j == 0:
            W.update(ag.get("l3", [h2]))
        saved.append(dict(h=h, xn=xn, qh=qh, oh=oh, sink=sink, attn=attn, h1=h1, xn2=xn2, gu=gu, f=f))
        h = h2

    dh, st_final = final_loss(h, P["norm_final"], target, "loss_head")

    S = dict(norm_ffn=[None] * 4, norm_mix=[None] * 4, conv=[None] * 2, taps=[None] * 2, b_pw1=[None] * 2,
             b_pw2=[None] * 2, sinks=[None] * 2)

    def ffn_bwd(dh, sv, l, nf, after=()):
        du = mmT_swiglu_bwd(dh, _rows(W[f"down_{l}"]), sv["gu"], f"b_down_{l}", after)
        gd = mm_dw(sv["f"], dh, f"w_down_{l}", 512, 1)
        gu = mm_dw(sv["xn2"], du, f"w_up_{l}", DFF // 2, 4)
        dh, dg = mmT_rmsbwd(du, _slots(W[f"up_{l}"]), sv["h1"], nf, l, dh, f"b_up_{l}")
        S["norm_ffn"][l] = dg
        return dh, {f"down_{l}": _gview(gd), f"up_{l}": _gview(gu)}

    dk = dv = dbias = None
    sent = []
    for j in (1, 0):
        l = 2 + j
        sv = saved[l]
        dh, grads = ffn_bwd(dh, sv, l, nf, sent)
        dattn = mmT(dh, _rows(W[f"wo_{j}"]), f"b_wo_{j}")
        grads[f"wo_{j}"] = _gview(mm_dw(sv["attn"], dh, f"w_wo_{j}", 512, 1))
        doh = dattn.T.reshape(N_KV, GROUP, HD, T)
        dqh, dk, dv, dbj, dsj = attn_bwd(sv["qh"], kp, kt, vp, bias, sv["sink"], sv["oh"], doh, f"b_attn_{j}",
                                         None if dk is None else (dk, dv))
        dq = dqh.reshape(N_HEADS * HD, T).T
        grads[f"wq_{j}"] = _gview(mm_dw(sv["xn"], dq, f"w_q_{j}", 512, 1))
        dh, dg = mmT_rmsbwd(dq, _rows(W[f"wq_{j}"])[None], sv["h"], nm, l, dh, f"b_q_{j}")
        S["norm_mix"][l] = dg
        S["sinks"][j] = jnp.sum(dsj.reshape(N_HEADS, BLK), axis=1)
        dbias = dbj if dbias is None else dbias + dbj
        if j == 1:
            sent = [rs.send("l3", grads)]

    dkv = jnp.concatenate([_heads_minor(dk[:, BLK:]), _heads_minor(dv[:, BLK:])], axis=1).astype(BF16)
    grads["kv"] = _gview(mm_dw(kvn, dkv, "w_kv", 512, 1))
    dh, dg = mmT_rmsbwd(dkv, _rows(W["kv"])[None], h_kv, P["norm_kv"], 0, dh, "b_kv")
    S["norm_kv"] = dg
    dbh = dbias.reshape(N_KV, 2 * BLK, GROUP, BLK)
    S["rel_bias"] = jnp.einsum("vkgq,qkb->bvg", dbh, onehot, precision=lax.Precision.HIGHEST).reshape(N_BUCKETS, N_HEADS)
    sent = [rs.send("l2", grads)]
    nf = _gate(nf, rs.reduce("l3", [dh]))

    for l in (1, 0):
        sv = saved[l]
        dh, grads = ffn_bwd(dh, sv, l, nf, sent)
        conv = sm["conv"]
        if l == 0:
            conv = _gate(conv, rs.send("f0", grads))
            grads = {}
        dy, st = mmT_lnbwd(dh, _rows(W[f"pw2_{l}"]), sv["y"], conv, l, f"b_pw2_{l}")
        g2, S["b_pw2"][l] = mm_dw(sv["s"], dh, f"w_pw2_{l}", 512, 1, colsum=True)
        du, dtaps = dwconv_glu_bwd(dy, sv["a"], sv["u"], sm["conv"], sm["conv_rev"], l, f"b_conv_{l}")
        S["conv"][l] = st
        S["taps"][l] = dtaps
        if l == 0:
            rs.finish("l2", [du])
            nm = _gate(nm, rs.reduce("l1", [du]))
        g1, S["b_pw1"][l] = mm_dw(sv["xn"], du, f"w_pw1_{l}", 512, 4, colsum=True)
        grads[f"pw2_{l}"], grads[f"pw1_{l}"] = _gview(g2), _gview(g1)
        dh, dg = mmT_rmsbwd(du, _slots(W[f"pw1_{l}"]), sv["h"], nm, l, dh, f"b_pw1_{l}")
        S["norm_mix"][l] = dg
        if l == 1:
            sent = [rs.send("l1", grads)]
            rs.finish("l3", [dh])
            nf = _gate(nf, rs.reduce("l2", [dh]))
    S["final"] = st_final
    return grads, dh, S


R_CONV = 37
R_SMALL = 88


def _pack_small(S):
    rows = []
    for l in range(2):
        rows += [S["taps"][l], S["conv"][l], S["b_pw2"][l], S["b_pw1"][l].reshape(2, D)]
    rows += S["norm_mix"] + S["norm_ffn"] + [S["norm_kv"], S["final"]]
    tail = jnp.concatenate([jnp.stack(S["sinks"]).reshape(-1), S["rel_bias"].reshape(-1)])
    rows += [jnp.pad(tail, (0, D - tail.shape[0]))[None]]
    v = jnp.concatenate(rows, axis=0)
    return jnp.pad(v, ((0, R_SMALL - v.shape[0]), (0, 0)))


def kernel(x, norm_mix, norm_ffn, conv_w_pw1, conv_b_pw1, conv_w_dw, conv_b_dw, conv_ln_g, conv_ln_b, conv_w_pw2, conv_b_pw2, norm_kv, w_kv, w_q, w_o, sinks, rel_bias, ffn_w_up, ffn_w_down, norm_final, loss_target, m_norm_mix, m_norm_ffn, m_conv_w_pw1, m_conv_b_pw1, m_conv_w_dw, m_conv_b_dw, m_conv_ln_g, m_conv_ln_b, m_conv_w_pw2, m_conv_b_pw2, m_norm_kv, m_w_kv, m_w_q, m_w_o, m_sinks, m_rel_bias, m_ffn_w_up, m_ffn_w_down, m_norm_final, v_norm_mix, v_norm_ffn, v_conv_w_pw1, v_conv_b_pw1, v_conv_w_dw, v_conv_b_dw, v_conv_ln_g, v_conv_ln_b, v_conv_w_pw2, v_conv_b_pw2, v_norm_kv, v_w_kv, v_w_q, v_w_o, v_sinks, v_rel_bias, v_ffn_w_up, v_ffn_w_down, v_norm_final):
    me = 2 * lax.axis_index("x") + lax.axis_index("y")
    weights = dict(norm_mix=norm_mix, norm_ffn=norm_ffn, conv_w_pw1=conv_w_pw1, conv_b_pw1=conv_b_pw1,
                   conv_w_dw=conv_w_dw, conv_b_dw=conv_b_dw, conv_ln_g=conv_ln_g, conv_ln_b=conv_ln_b,
                   conv_w_pw2=conv_w_pw2, conv_b_pw2=conv_b_pw2, norm_kv=norm_kv, w_kv=w_kv, w_q=w_q, w_o=w_o,
                   sinks=sinks, rel_bias=rel_bias, ffn_w_up=ffn_w_up, ffn_w_down=ffn_w_down, norm_final=norm_final)
    mom_m = dict(norm_mix=m_norm_mix, norm_ffn=m_norm_ffn, conv_w_pw1=m_conv_w_pw1, conv_b_pw1=m_conv_b_pw1,
                 conv_w_dw=m_conv_w_dw, conv_b_dw=m_conv_b_dw, conv_ln_g=m_conv_ln_g, conv_ln_b=m_conv_ln_b,
                 conv_w_pw2=m_conv_w_pw2, conv_b_pw2=m_conv_b_pw2, norm_kv=m_norm_kv, w_kv=m_w_kv, w_q=m_w_q,
                 w_o=m_w_o, sinks=m_sinks, rel_bias=m_rel_bias, ffn_w_up=m_ffn_w_up, ffn_w_down=m_ffn_w_down,
                 norm_final=m_norm_final)
    mom_v = dict(norm_mix=v_norm_mix, norm_ffn=v_norm_ffn, conv_w_pw1=v_conv_w_pw1, conv_b_pw1=v_conv_b_pw1,
                 conv_w_dw=v_conv_w_dw, conv_b_dw=v_conv_b_dw, conv_ln_g=v_conv_ln_g, conv_ln_b=v_conv_ln_b,
                 conv_w_pw2=v_conv_w_pw2, conv_b_pw2=v_conv_b_pw2, norm_kv=v_norm_kv, w_kv=v_w_kv, w_q=v_w_q,
                 w_o=v_w_o, sinks=v_sinks, rel_bias=v_rel_bias, ffn_w_up=v_ffn_w_up, ffn_w_down=v_ffn_w_down,
                 norm_final=v_norm_final)

    big = {"conv_w_pw1": "pw1", "conv_w_pw2": "pw2", "w_q": "wq", "w_o": "wo", "ffn_w_up": "up",
           "ffn_w_down": "down", "w_kv": "kv"}
    of_kind = {k: n for n, k in big.items()}

    def source(name):
        if name == "small":
            return jnp.concatenate(
                [conv_w_dw, conv_b_dw[:, None], conv_ln_g[:, None], conv_ln_b[:, None], conv_b_pw2[:, None],
                 conv_b_pw1.reshape(2, 2, 256), jnp.zeros((2, 3, 256), F32)], axis=1), None
        kind, _, l = name.partition("_")
        return weights[of_kind[kind]], (int(l) if l else None)

    ag = WeightGather(source, AG_GROUPS)
    rs = GradReduce({"pw1": (2, 512, 512), "pw2": (2, 128, D), "wq": (2, 128, D), "wo": (2, 128, D),
                     "up": (4, 512, DFF // 2), "down": (4, DFF // 8, D), "kv": (1, 128, 512)})

    P = dict(norm_mix=norm_mix[:, None], norm_ffn=norm_ffn[:, None], norm_kv=norm_kv[None, None],
             norm_final=norm_final[None], sinks=sinks, rel_bias=rel_bias)
    last, grad_x, S = run_step(x[0], loss_target[0], P, ag, rs)

    rs.finish("l1", [grad_x])
    small_flight, token = small_allreduce_start(_gate(_pack_small(S), rs.reduce("f0", [grad_x])), [])
    token = rs.send("c0", last, after=[token])
    delta, new_m, new_v, big_grads = {}, {}, {}, {}

    def update(n):
        shp = weights[n].shape
        r2 = (int(np.prod(shp[:-1])), shp[-1])
        g, d, nm, nv = adamw(weights[n].reshape(r2), rs.J[big[n]].reshape(r2), mom_m[n].reshape(r2),
                             mom_v[n].reshape(r2), f"adamw_{n}", copy_g=True)
        big_grads[n], delta[n], new_m[n], new_v[n] = g.reshape(shp), d.reshape(shp), nm.reshape(shp), nv.reshape(shp)

    rs.finish("f0", [token])
    for n in ("ffn_w_up", "ffn_w_down"):
        update(n)
    vsum = sum8(xchg_wait(small_flight, [delta["ffn_w_up"], delta["ffn_w_down"]])[1], "small_sum")

    col = lambda a: lax.dynamic_slice_in_dim(a, me * 256, 256, axis=-1)
    grads = {}
    for l in range(2):
        base = l * R_CONV
        grads.setdefault("conv_w_dw", []).append(col(vsum[base:base + 31]))
        grads.setdefault("conv_b_dw", []).append(col(vsum[base + 31]))
        grads.setdefault("conv_ln_g", []).append(col(vsum[base + 32]))
        grads.setdefault("conv_ln_b", []).append(col(vsum[base + 33]))
        grads.setdefault("conv_b_pw2", []).append(col(vsum[base + 34]))
        grads.setdefault("conv_b_pw1", []).append(
            lax.dynamic_slice_in_dim(vsum[base + 35:base + 37].reshape(2 * D), me * 512, 512, axis=0))
    grads = {k: jnp.stack(v) for k, v in grads.items()}
    base = 2 * R_CONV
    grads["norm_mix"] = vsum[base:base + 4]
    grads["norm_ffn"] = vsum[base + 4:base + 8]
    grads["norm_kv"] = vsum[base + 8]
    grads["norm_final"] = vsum[base + 9]
    loss = vsum[base + 10, 0]
    grads["sinks"] = vsum[base + 11, 0:32].reshape(2, 16)
    grads["rel_bias"] = vsum[base + 11, 32:32 + 512].reshape(32, 16)

    for n in weights:
        if n not in big:
            shp = weights[n].shape
            r2 = (int(np.prod(shp[:-1])), shp[-1])
            d, nm, nv = adamw(weights[n].reshape(r2), grads[n].reshape(r2), mom_m[n].reshape(r2),
                              mom_v[n].reshape(r2), f"adamw_{n}")
            delta[n], new_m[n], new_v[n] = d.reshape(shp), nm.reshape(shp), nv.reshape(shp)

    rs.reduce("c0", [vsum])
    for n in ("w_q", "w_o", "w_kv"):
        update(n)
    rs.finish("c0", [delta["w_kv"]])
    for n in ("conv_w_pw1", "conv_w_pw2"):
        update(n)
    grads.update(big_grads)

    order = list(weights)
    return (loss, grad_x[None], *[grads[n] for n in order], *[delta[n] for n in order],
            *[new_m[n] for n in order], *[new_v[n] for n in order])
```

```python
import functools
import math

import numpy as np
import jax
import jax.numpy as jnp
from jax import lax
from jax.experimental import pallas as pl
from jax.experimental.pallas import tpu as pltpu

F32 = jnp.float32
BF16 = jnp.bfloat16
MESH = pl.DeviceIdType.MESH

D = 1024
DFF = 2816
N_HEADS = 16
N_KV = 4
GROUP = 4
HD = 64
BLK = 128
CONV_W = 31
HALO = 32
N_BUCKETS = 32
MAX_DISTANCE = 128
EPS = 1e-6
NEG_INF = -1e30
TM = 512
TCV = 256
VMEM_LIMIT = 56 * 2 ** 20

ADAM_LR, ADAM_B1, ADAM_B2, ADAM_EPS, ADAM_WD, ADAM_STEP = 0.001, 0.9, 0.999, 1e-08, 0.01, 10


def _cp(*sem):
    return pltpu.CompilerParams(dimension_semantics=sem, vmem_limit_bytes=VMEM_LIMIT)


def _sigmoid(x):
    return 1.0 / (1.0 + jnp.exp(-x))


def _row(tm, n):
    return pl.BlockSpec((tm, n), lambda i: (i, 0))


def _const(shape):
    nd = len(shape)
    return pl.BlockSpec(shape, lambda i: (0,) * nd)


def _weight(shape):
    nd = len(shape)
    return pl.BlockSpec(shape, lambda i: (0,) * nd, pipeline_mode=pl.Buffered(1))


def _layer(shape, l):
    nd = len(shape)
    return pl.BlockSpec((None,) + tuple(shape), lambda i: (l,) + (0,) * nd)


def _dot(a, b):
    return jnp.dot(a, b, preferred_element_type=F32)


def _dot_nt(a, b):
    return lax.dot_general(a, b, (((1,), (1,)), ((), ())), preferred_element_type=F32)


def _dot_tn(a, b):
    return lax.dot_general(a, b, (((0,), (0,)), ((), ())), preferred_element_type=F32)


def _rms(x):
    return lax.rsqrt(jnp.mean(x * x, axis=-1, keepdims=True) + EPS)


def norm_mm_glu(h, g, l, w, b, name):
    T = h.shape[0]
    ns = w.shape[-1]

    def body(h_ref, g_ref, w_ref, b_ref, xn_ref, u_ref, a_ref):
        x = h_ref[...]
        xn = (x * _rms(x) * g_ref[...]).astype(BF16)
        xn_ref[...] = xn
        for s in range(2):
            lo, hi = s * ns, (s + 1) * ns
            u1 = _dot(xn, w_ref[s]) + b_ref[:, lo:hi]
            u2 = _dot(xn, w_ref[2 + s]) + b_ref[:, D + lo:D + hi]
            u_ref[:, lo:hi] = u1.astype(BF16)
            u_ref[:, D + lo:D + hi] = u2.astype(BF16)
            a_ref[:, lo:hi] = (u1 * _sigmoid(u2)).astype(BF16)

    return pl.pallas_call(
        body, name=name, grid=(T // TM,),
        in_specs=[_row(TM, D), _layer((1, D), l), _weight((4, D, ns)), _layer((1, 2 * D), l)],
        out_specs=[_row(TM, D), _row(TM, 2 * D), _row(TM, D)],
        out_shape=[jax.ShapeDtypeStruct((T, D), BF16), jax.ShapeDtypeStruct((T, 2 * D), BF16),
                   jax.ShapeDtypeStruct((T, D), BF16)],
        compiler_params=_cp("parallel"),
    )(h, g, w, b)


SUB = 8


def _make_shifts(sh):
    n = TCV + HALO - SUB
    for r in range(1, SUB):
        for r0 in range(0, n, 40):
            sh[r, r0:r0 + 40, :] = sh[0, pl.ds(r + r0, 40), :]


def _shifted(sh, off, rows, cols):
    return sh[off % SUB, pl.ds(off - off % SUB, rows), cols]


def _conv_taps(sh, w_ref, out_ref, first):
    RB, LB = 32, 512
    for r0 in range(0, TCV, RB):
        for c0 in range(0, out_ref.shape[1], LB):
            acc = jnp.zeros((RB, LB), F32)
            for k in range(CONV_W):
                acc = acc + w_ref[k:k + 1, c0:c0 + LB] * _shifted(sh, first + k + r0, RB, slice(c0, c0 + LB))
            out_ref[r0:r0 + RB, c0:c0 + LB] = acc


def dwconv_ln_silu(a, sm, l, name):
    T = a.shape[0]
    nb = TCV // HALO

    def body(cur_ref, prev_ref, sm_ref, y_ref, s_ref, sh, yb):
        i = pl.program_id(0)
        sh[0, 0:HALO, :] = jnp.where(i > 0, prev_ref[...].astype(F32), 0.0)
        sh[0, HALO:HALO + TCV, :] = cur_ref[...].astype(F32)
        _make_shifts(sh)
        _conv_taps(sh, sm_ref, yb, HALO - (CONV_W - 1))
        y = yb[...] + sm_ref[31:32, :]
        y_ref[...] = y.astype(BF16)
        mu = jnp.mean(y, axis=-1, keepdims=True)
        yc = y - mu
        rstd = lax.rsqrt(jnp.mean(yc * yc, axis=-1, keepdims=True) + EPS)
        z = yc * rstd * sm_ref[32:33, :] + sm_ref[33:34, :]
        s_ref[...] = (z * _sigmoid(z)).astype(BF16)

    return pl.pallas_call(
        body, name=name, grid=(T // TCV,),
        in_specs=[_row(TCV, D), pl.BlockSpec((HALO, D), lambda i: (jnp.maximum(i * nb - 1, 0), 0)),
                  _layer((40, D), l)],
        out_specs=[_row(TCV, D), _row(TCV, D)],
        out_shape=[jax.ShapeDtypeStruct((T, D), BF16), jax.ShapeDtypeStruct((T, D), BF16)],
        scratch_shapes=[pltpu.VMEM((SUB, TCV + HALO, D), F32), pltpu.VMEM((TCV, D), F32)],
        compiler_params=_cp("parallel"),
    )(a, a, sm)


def mm_bias_res(xb, w, b, bl, res, name):
    T, K = xb.shape

    def body(x_ref, w_ref, b_ref, r_ref, o_ref):
        o_ref[...] = _dot(x_ref[...], w_ref[...]) + b_ref[...] + r_ref[...]

    return pl.pallas_call(
        body, name=name, grid=(T // TM,),
        in_specs=[_row(TM, K), _weight((K, D)), _layer((1, D), bl), _row(TM, D)],
        out_specs=_row(TM, D), out_shape=jax.ShapeDtypeStruct((T, D), F32),
        compiler_params=_cp("parallel"),
    )(xb, w, b, res)


def norm_mm_swiglu(h, g, l, w, name):
    T = h.shape[0]
    ns = w.shape[-1]

    def body(h_ref, g_ref, w_ref, xn_ref, gu_ref, f_ref):
        x = h_ref[...]
        xn = (x * _rms(x) * g_ref[...]).astype(BF16)
        xn_ref[...] = xn
        for s in range(2):
            lo, hi = s * ns, (s + 1) * ns
            gate = _dot(xn, w_ref[s])
            up = _dot(xn, w_ref[2 + s])
            gu_ref[:, lo:hi] = gate.astype(BF16)
            gu_ref[:, DFF + lo:DFF + hi] = up.astype(BF16)
            f_ref[:, lo:hi] = (gate * _sigmoid(gate) * up).astype(BF16)

    return pl.pallas_call(
        body, name=name, grid=(T // TM,),
        in_specs=[_row(TM, D), _layer((1, D), l), _weight((4, D, ns))],
        out_specs=[_row(TM, D), _row(TM, 2 * DFF), _row(TM, DFF)],
        out_shape=[jax.ShapeDtypeStruct((T, D), BF16), jax.ShapeDtypeStruct((T, 2 * DFF), BF16),
                   jax.ShapeDtypeStruct((T, DFF), BF16)],
        compiler_params=_cp("parallel"),
    )(h, g, w)


def norm_mm(h, g, gl, w, name, scale=1.0):
    T = h.shape[0]
    N = w.shape[-1]

    def body(h_ref, g_ref, w_ref, xn_ref, o_ref):
        x = h_ref[...]
        xn = (x * _rms(x) * g_ref[...]).astype(BF16)
        xn_ref[...] = xn
        o_ref[...] = (_dot(xn, w_ref[...]) * scale).astype(BF16)

    return pl.pallas_call(
        body, name=name, grid=(T // TM,),
        in_specs=[_row(TM, D), _layer((1, D), gl), _weight((D, N))],
        out_specs=[_row(TM, D), _row(TM, N)],
        out_shape=[jax.ShapeDtypeStruct((T, D), BF16), jax.ShapeDtypeStruct((T, N), BF16)],
        compiler_params=_cp("parallel"),
    )(h, g, w)


def norm_mm_pair(h, ga, gla, wa, gb, glb, wb, name, scale_b=1.0):
    T = h.shape[0]
    Na, Nb = wa.shape[-1], wb.shape[-1]

    def body(h_ref, ga_ref, wa_ref, gb_ref, wb_ref, xa_ref, oa_ref, xb_ref, ob_ref):
        x = h_ref[...]
        xr = x * _rms(x)
        xa = (xr * ga_ref[...]).astype(BF16)
        xa_ref[...] = xa
        oa_ref[...] = _dot(xa, wa_ref[...]).astype(BF16)
        xb = (xr * gb_ref[...]).astype(BF16)
        xb_ref[...] = xb
        ob_ref[...] = (_dot(xb, wb_ref[...]) * scale_b).astype(BF16)

    return pl.pallas_call(
        body, name=name, grid=(T // TM,),
        in_specs=[_row(TM, D), _layer((1, D), gla), _weight((D, Na)), _layer((1, D), glb), _weight((D, Nb))],
        out_specs=[_row(TM, D), _row(TM, Na), _row(TM, D), _row(TM, Nb)],
        out_shape=[jax.ShapeDtypeStruct((T, D), BF16), jax.ShapeDtypeStruct((T, Na), BF16),
                   jax.ShapeDtypeStruct((T, D), BF16), jax.ShapeDtypeStruct((T, Nb), BF16)],
        compiler_params=_cp("parallel"),
    )(h, ga, wa, gb, wb)


QB = 16
QW = GROUP * BLK


def band_mask():
    qi = np.arange(QW)[None, :] % BLK
    kj = np.arange(2 * BLK)[:, None]
    band = ((kj < BLK) & (kj > qi)) | ((kj >= BLK) & (kj - BLK <= qi))
    first = band & (kj >= BLK)
    return np.where(np.stack([first, band]), 0.0, NEG_INF).astype(np.float32)


def _softmax_cols(s, sink):
    m = jnp.maximum(jnp.max(s, axis=0, keepdims=True), sink)
    p = jnp.exp(s - m)
    es = jnp.exp(sink - m)
    inv = 1.0 / (jnp.sum(p, axis=0, keepdims=True) + es)
    return p, inv, es


def _attn_specs(T):
    W = QB * BLK
    qspec = pl.BlockSpec((None, GROUP, HD, W), lambda kv, n: (kv, 0, 0, n))
    kspec = pl.BlockSpec((None, T + BLK, HD), lambda kv, n: (kv, 0, 0))
    ktspec = [pl.BlockSpec((None, HD, W), lambda kv, n: (kv, 0, n)),
              pl.BlockSpec((None, HD, BLK), lambda kv, n: (kv, 0, (n + 1) * QB))]
    bspec = pl.BlockSpec((2, None, 2 * BLK, QW), lambda kv, n: (0, kv, 0, 0))
    sspec = pl.BlockSpec((None, 1, QW), lambda kv, n: (kv, 0, 0))
    return qspec, kspec, ktspec, bspec, sspec


def _attn_block(n, b):
    blk = n * QB + b
    rows = pl.ds(pl.multiple_of(blk * BLK, BLK), 2 * BLK)
    return rows, (jnp.minimum(blk, 1) if b == 0 else 1)


def _band_cols(main_ref, tail_ref, b):
    if b < QB - 1:
        return main_ref[:, b * BLK:(b + 2) * BLK]
    return jnp.concatenate([main_ref[:, b * BLK:], tail_ref[...]], axis=1)


def _heads_side_by_side(ref, qs):
    return jnp.concatenate([ref[g, :, qs] for g in range(GROUP)], axis=1)


def attn_fwd(q, kp, vt, bias, sink, name):
    T = q.shape[3]
    qspec, kspec, ktspec, bspec, sspec = _attn_specs(T)

    def body(q_ref, k_ref, vt_ref, vtt_ref, b_ref, s_ref, o_ref, pb):
        n = pl.program_id(1)

        def scores(b):
            return _dot(k_ref[_attn_block(n, b)[0], :], _heads_side_by_side(q_ref, slice(b * BLK, (b + 1) * BLK)))

        st_next = scores(0)
        for b in range(QB):
            rows, table = _attn_block(n, b)
            qs = slice(b * BLK, (b + 1) * BLK)
            st = st_next
            if b + 1 < QB:
                st_next = scores(b + 1)
            for g in range(GROUP):
                hs = slice(g * BLK, (g + 1) * BLK)
                p, inv, _ = _softmax_cols(st[:, hs] + b_ref[table, :, hs], s_ref[:, hs])
                pb[:, hs] = (p * inv).astype(BF16)
            ot = _dot(_band_cols(vt_ref, vtt_ref, b), pb[...])
            for g in range(GROUP):
                o_ref[g, :, qs] = ot[:, g * BLK:(g + 1) * BLK].astype(BF16)

    return pl.pallas_call(
        body, name=name, grid=(N_KV, T // (QB * BLK)),
        in_specs=[qspec, kspec, *ktspec, bspec, sspec], out_specs=qspec,
        out_shape=jax.ShapeDtypeStruct((N_KV, GROUP, HD, T), BF16),
        scratch_shapes=[pltpu.VMEM((2 * BLK, QW), BF16)],
        compiler_params=_cp("parallel", "parallel"),
    )(q, kp, vt, vt, bias, sink)


def attn_bwd(q, kp, kt, vp, bias, sink, o, do, name, dkv=None):
    T = q.shape[3]
    qspec, kspec, ktspec, bspec, sspec = _attn_specs(T)

    def body(q_ref, k_ref, kt_ref, ktt_ref, v_ref, b_ref, s_ref, o_ref, do_ref, *rest):
        dq_ref, dk_ref, dv_ref, db_ref, ds_ref, pb, dsb = rest[-7:]
        n = pl.program_id(1)

        @pl.when(n == 0)
        def _():
            dk_ref[...] = jnp.zeros_like(dk_ref) if dkv is None else rest[0][...]
            dv_ref[...] = jnp.zeros_like(dv_ref) if dkv is None else rest[1][...]
            db_ref[...] = jnp.zeros_like(db_ref)
            ds_ref[...] = jnp.zeros_like(ds_ref)

        def products(b):
            rows = _attn_block(n, b)[0]
            qs = slice(b * BLK, (b + 1) * BLK)
            q4, do4 = _heads_side_by_side(q_ref, qs), _heads_side_by_side(do_ref, qs)
            return q4, do4, _dot(k_ref[rows, :], q4), _dot(v_ref[rows, :], do4)

        ahead = products(0)
        for b in range(QB):
            rows, table = _attn_block(n, b)
            qs = slice(b * BLK, (b + 1) * BLK)
            q4, do4, st, dpt = ahead
            if b + 1 < QB:
                ahead = products(b + 1)
            for g in range(GROUP):
                hs = slice(g * BLK, (g + 1) * BLK)
                p, inv, es = _softmax_cols(st[:, hs] + b_ref[table, :, hs], s_ref[:, hs])
                probs = p * inv
                delta = jnp.sum(do_ref[g, :, qs].astype(F32) * o_ref[g, :, qs].astype(F32), axis=0, keepdims=True)
                dS = probs * (dpt[:, hs] - delta)
                ds_ref[:, hs] += -(es * inv) * delta
                db_ref[:, hs] += dS
                pb[:, hs] = probs.astype(BF16)
                dsb[:, hs] = dS.astype(BF16)
            dqt = _dot(_band_cols(kt_ref, ktt_ref, b), dsb[...]) * (HD ** -0.5)
            for g in range(GROUP):
                dq_ref[g, :, qs] = dqt[:, g * BLK:(g + 1) * BLK].astype(BF16)
            dk_ref[rows, :] += _dot_nt(dsb[...], q4)
            dv_ref[rows, :] += _dot_nt(pb[...], do4)

    kout = pl.BlockSpec((None, T + BLK, HD), lambda kv, n: (kv, 0, 0))
    dbspec = pl.BlockSpec((None, 2 * BLK, QW), lambda kv, n: (kv, 0, 0))
    return pl.pallas_call(
        body, name=name, grid=(N_KV, T // (QB * BLK)),
        in_specs=[qspec, kspec, *ktspec, kspec, bspec, sspec, qspec, qspec] + ([] if dkv is None else [kout, kout]),
        out_specs=[qspec, kout, kout, dbspec, sspec],
        out_shape=[jax.ShapeDtypeStruct((N_KV, GROUP, HD, T), BF16),
                   jax.ShapeDtypeStruct((N_KV, T + BLK, HD), F32), jax.ShapeDtypeStruct((N_KV, T + BLK, HD), F32),
                   jax.ShapeDtypeStruct((N_KV, 2 * BLK, QW), F32), jax.ShapeDtypeStruct((N_KV, 1, QW), F32)],
        scratch_shapes=[pltpu.VMEM((2 * BLK, QW), BF16), pltpu.VMEM((2 * BLK, QW), BF16)],
        compiler_params=_cp("parallel", "arbitrary"),
    )(q, kp, kt, kt, vp, bias, sink, o, do, *(dkv or ()))


def final_loss(h, g, target, name):
    T = h.shape[0]

    def body(h_ref, g_ref, t_ref, dh_ref, st_ref):
        i = pl.program_id(0)

        @pl.when(i == 0)
        def _():
            st_ref[...] = jnp.zeros_like(st_ref)

        x = h_ref[...]
        r = _rms(x)
        xh = x * r
        e = xh * g_ref[...] - t_ref[...]
        loss = 0.5 * jnp.sum(jnp.mean(e * e, axis=-1, keepdims=True))
        dy = e * (1.0 / D)
        st_ref[0:1, :] += jnp.sum(dy * xh, axis=0, keepdims=True)
        lane = lax.broadcasted_iota(jnp.int32, (1, D), 1)
        st_ref[1:2, :] += jnp.where(lane == 0, loss, 0.0)
        dxh = dy * g_ref[...]
        dh_ref[...] = r * (dxh - xh * jnp.mean(dxh * xh, axis=-1, keepdims=True))

    return pl.pallas_call(
        body, name=name, grid=(T // TM,),
        in_specs=[_row(TM, D), _const((1, D)), _row(TM, D)],
        out_specs=[_row(TM, D), _const((2, D))],
        out_shape=[jax.ShapeDtypeStruct((T, D), F32), jax.ShapeDtypeStruct((2, D), F32)],
        compiler_params=_cp("arbitrary"),
    )(h, g, target)


def mm_dw(x, dy, name, tn, slots, colsum=False):
    T, K = x.shape
    split = dy.ndim == 3
    N = dy.shape[-1] * (2 if split else 1)
    tt = min(T, 2048 if K <= 1024 else 1024)
    nt = T // tt
    ns = N // slots
    per = ns // tn

    def body(x_ref, dy_ref, *rest):
        if colsum:
            dw_ref, cs_ref, acc, cacc = rest
        else:
            dw_ref, acc = rest
        t = pl.program_id(1)

        @pl.when(t == 0)
        def _():
            acc[...] = jnp.zeros_like(acc)
            if colsum:
                cacc[...] = jnp.zeros_like(cacc)

        dyv = dy_ref[...]
        acc[...] += _dot_tn(x_ref[...].astype(BF16), dyv.astype(BF16))
        if colsum:
            cacc[...] += jnp.sum(dyv.astype(F32), axis=0, keepdims=True)

        @pl.when(t == nt - 1)
        def _():
            dw_ref[...] = acc[...].astype(BF16)
            if colsum:
                cs_ref[...] = cacc[...]

    if split:
        half = N // 2 // tn
        dy_spec = pl.BlockSpec((None, tt, tn), lambda j, t: (j // half, t, j % half))
    else:
        dy_spec = pl.BlockSpec((tt, tn), lambda j, t: (t, j))
    out_specs = [pl.BlockSpec((None, K, tn), lambda j, t: (j // per, 0, j % per))]
    out_shape = [jax.ShapeDtypeStruct((slots, K, ns), BF16)]
    scratch = [pltpu.VMEM((K, tn), F32)]
    if colsum:
        out_specs.append(pl.BlockSpec((1, tn), lambda j, t: (0, j)))
        out_shape.append(jax.ShapeDtypeStruct((1, N), F32))
        scratch.append(pltpu.VMEM((1, tn), F32))
    res = pl.pallas_call(
        body, name=name, grid=(N // tn, nt),
        in_specs=[pl.BlockSpec((tt, K), lambda j, t: (t, 0)), dy_spec],
        out_specs=out_specs, out_shape=out_shape, scratch_shapes=scratch,
        compiler_params=_cp("parallel", "arbitrary"),
    )(x, dy)
    return tuple(res) if colsum else res[0]


def mmT_swiglu_bwd(dh, w, gu, name, after=()):
    T = dh.shape[0]
    cw = 256

    def body(dh_ref, w_ref, gu_ref, *rest):
        du_ref = rest[-1]
        dhb = dh_ref[...].astype(BF16)
        ahead = _dot_nt(dhb, w_ref[0:cw, :])
        for lo in range(0, DFF, cw):
            hi = lo + cw
            df = ahead
            if hi < DFF:
                ahead = _dot_nt(dhb, w_ref[hi:hi + cw, :])
            gate = gu_ref[:, lo:hi].astype(F32)
            up = gu_ref[:, DFF + lo:DFF + hi].astype(F32)
            sg = _sigmoid(gate)
            silu = gate * sg
            du_ref[:, lo:hi] = (df * (up * (sg + silu * (1.0 - sg)))).astype(BF16)
            du_ref[:, DFF + lo:DFF + hi] = (df * silu).astype(BF16)

    return pl.pallas_call(
        body, name=name, grid=(T // TM,),
        in_specs=[_row(TM, D), _weight((DFF, D)), _row(TM, 2 * DFF)] + [ANY] * len(after),
        out_specs=_row(TM, 2 * DFF), out_shape=jax.ShapeDtypeStruct((T, 2 * DFF), BF16),
        compiler_params=_cp("parallel"),
    )(dh, w, gu, *after)


def mmT_rmsbwd(du, w, h, g, gl, dh_in, name):
    split = du.ndim == 3
    T = du.shape[-2]
    N = du.shape[-1] * (2 if split else 1)
    slots = w.shape[0]
    ns = N // slots

    RH = TM // 2

    def piece(du_ref, s, rows):
        if split:
            per = slots // 2
            return du_ref[s // per, rows, (s % per) * ns:(s % per + 1) * ns]
        return du_ref[rows, s * ns:(s + 1) * ns]

    def body(du_ref, w_ref, h_ref, g_ref, di_ref, dh_ref, dg_ref):
        i = pl.program_id(0)

        @pl.when(i == 0)
        def _():
            dg_ref[...] = jnp.zeros_like(dg_ref)

        def products(k):
            rows = slice(k * RH, (k + 1) * RH)
            dxn = _dot_nt(piece(du_ref, 0, rows), w_ref[0])
            for s in range(1, slots):
                dxn = dxn + _dot_nt(piece(du_ref, s, rows), w_ref[s])
            return dxn

        ahead = products(0)
        for k in range(TM // RH):
            rows = slice(k * RH, (k + 1) * RH)
            dxn = ahead
            if (k + 1) * RH < TM:
                ahead = products(k + 1)
            x = h_ref[rows, :]
            r = _rms(x)
            xh = x * r
            dg_ref[0:1, :] += jnp.sum(dxn * xh, axis=0, keepdims=True)
            dxh = dxn * g_ref[...]
            dh_ref[rows, :] = di_ref[rows, :] + r * (dxh - xh * jnp.mean(dxh * xh, axis=-1, keepdims=True))

    return pl.pallas_call(
        body, name=name, grid=(T // TM,),
        in_specs=[pl.BlockSpec((2, TM, N // 2), lambda i: (0, i, 0)) if split else _row(TM, N),
                  _weight((slots, D, ns)), _row(TM, D), _layer((1, D), gl), _row(TM, D)],
        out_specs=[_row(TM, D), _const((1, D))],
        out_shape=[jax.ShapeDtypeStruct((T, D), F32), jax.ShapeDtypeStruct((1, D), F32)],
        compiler_params=_cp("arbitrary"),
    )(du, w, h, g, dh_in)


def mmT(dh, w, name):
    T = dh.shape[0]
    N = w.shape[0]

    def body(dh_ref, w_ref, o_ref):
        o_ref[...] = _dot_nt(dh_ref[...].astype(BF16), w_ref[...]).astype(BF16)

    return pl.pallas_call(
        body, name=name, grid=(T // TM,),
        in_specs=[_row(TM, D), _weight((N, D))],
        out_specs=_row(TM, N), out_shape=jax.ShapeDtypeStruct((T, N), BF16),
        compiler_params=_cp("parallel"),
    )(dh, w)


def mmT_lnbwd(dh, w, y, sm, l, name):
    T = dh.shape[0]

    def body(dh_ref, w_ref, y_ref, sm_ref, dy_ref, st_ref):
        i = pl.program_id(0)

        @pl.when(i == 0)
        def _():
            st_ref[...] = jnp.zeros_like(st_ref)

        ds = _dot_nt(dh_ref[...].astype(BF16), w_ref[...])
        y = y_ref[...].astype(F32)
        mu = jnp.mean(y, axis=-1, keepdims=True)
        yc = y - mu
        rstd = lax.rsqrt(jnp.mean(yc * yc, axis=-1, keepdims=True) + EPS)
        xh = yc * rstd
        gam = sm_ref[32:33, :]
        z = xh * gam + sm_ref[33:34, :]
        sg = _sigmoid(z)
        dz = ds * sg * (1.0 + z * (1.0 - sg))
        st_ref[1:2, :] += jnp.sum(dz * xh, axis=0, keepdims=True)
        st_ref[2:3, :] += jnp.sum(dz, axis=0, keepdims=True)
        dxh = dz * gam
        dy = rstd * (dxh - jnp.mean(dxh, axis=-1, keepdims=True) - xh * jnp.mean(dxh * xh, axis=-1, keepdims=True))
        st_ref[0:1, :] += jnp.sum(dy, axis=0, keepdims=True)
        dy_ref[...] = dy.astype(BF16)

    return pl.pallas_call(
        body, name=name, grid=(T // TM,),
        in_specs=[_row(TM, D), _weight((D, D)), _row(TM, D), _layer((40, D), l)],
        out_specs=[_row(TM, D), _const((3, D))],
        out_shape=[jax.ShapeDtypeStruct((T, D), BF16), jax.ShapeDtypeStruct((3, D), F32)],
        compiler_params=_cp("arbitrary"),
    )(dh, w, y, sm)


CH = 512


def dwconv_glu_bwd(dy, a, u, sm, smrev, l, name):
    T = dy.shape[0]
    nr, nc = T // TCV, D // CH
    nb = TCV // HALO
    last = T // HALO - 1

    def body(dy_ref, dyn_ref, a_ref, ap_ref, u1_ref, u2_ref, sm_ref, rev_ref, du_ref, dw_ref, shd, sha, da):
        i = pl.program_id(0)
        r = i % nr

        @pl.when(r == 0)
        def _():
            dw_ref[...] = jnp.zeros_like(dw_ref)

        shd[0, 0:TCV, :] = dy_ref[...].astype(F32)
        shd[0, TCV:TCV + HALO, :] = jnp.where(r < nr - 1, dyn_ref[...].astype(F32), 0.0)
        sha[0, 0:HALO, :] = jnp.where(r > 0, ap_ref[...].astype(F32), 0.0)
        sha[0, HALO:HALO + TCV, :] = a_ref[...].astype(F32)
        _make_shifts(shd)
        _make_shifts(sha)
        _conv_taps(shd, rev_ref, da, 0)
        for kg in range(0, CONV_W, SUB):
            taps = range(kg, min(kg + SUB, CONV_W))
            part = [jnp.zeros((SUB, CH), F32) for _ in taps]
            for r0 in range(0, TCV, SUB):
                d = shd[0, r0:r0 + SUB, :]
                for j, k in enumerate(taps):
                    part[j] = part[j] + d * _shifted(sha, HALO - (CONV_W - 1) + k + r0, SUB, slice(None))
            for j, k in enumerate(taps):
                dw_ref[k:k + 1, :] += jnp.sum(part[j], axis=0, keepdims=True)
        dav = da[...]
        u1 = u1_ref[...].astype(F32)
        sg = _sigmoid(u2_ref[...].astype(F32))
        du_ref[0] = (dav * sg).astype(BF16)
        du_ref[1] = (dav * u1 * sg * (1.0 - sg)).astype(BF16)

    tile = lambda i: (i % nr, i // nr)
    in_specs = [pl.BlockSpec((TCV, CH), tile),
                pl.BlockSpec((HALO, CH), lambda i: (jnp.minimum((i % nr + 1) * nb, last), i // nr)),
                pl.BlockSpec((TCV, CH), tile),
                pl.BlockSpec((HALO, CH), lambda i: (jnp.maximum((i % nr) * nb - 1, 0), i // nr)),
                pl.BlockSpec((TCV, CH), tile), pl.BlockSpec((TCV, CH), lambda i: (i % nr, nc + i // nr)),
                pl.BlockSpec((None, 40, CH), lambda i: (l, 0, i // nr)),
                pl.BlockSpec((None, 40, CH), lambda i: (l, 0, i // nr))]
    return pl.pallas_call(
        body, name=name, grid=(nr * nc,), in_specs=in_specs,
        out_specs=[pl.BlockSpec((2, TCV, CH), lambda i: (0, i % nr, i // nr)),
                   pl.BlockSpec((CONV_W, CH), lambda i: (0, i // nr))],
        out_shape=[jax.ShapeDtypeStruct((2, T, D), BF16), jax.ShapeDtypeStruct((CONV_W, D), F32)],
        scratch_shapes=[pltpu.VMEM((SUB, TCV + HALO, CH), F32), pltpu.VMEM((SUB, TCV + HALO, CH), F32),
                        pltpu.VMEM((TCV, CH), F32)],
        compiler_params=_cp("arbitrary"),
    )(dy, dy, a, a, u, u, sm, smrev)


def _rows_tile(R):
    for t in (512, 256, 128, 64, 32, 16, 8):
        if R % t == 0:
            return t
    return R


def add8_into(J, l, g, others, where, name):
    R, C = g.shape[2:]
    tr = R // 2

    def body(w_ref, g_ref, x_ref, j_in, j_ref):
        acc = g_ref[...].astype(F32)
        for k in range(7):
            acc = acc + x_ref[k].astype(F32)
        j_ref[...] = acc

    return pl.pallas_call(
        body, name=name,
        grid_spec=pltpu.PrefetchScalarGridSpec(
            num_scalar_prefetch=1, grid=(R // tr,),
            in_specs=[pl.BlockSpec((None, None, tr, C), lambda i, w: (w[0], w[1], i, 0)),
                      pl.BlockSpec((7, tr, C), lambda i, w: (0, i, 0)), ANY],
            out_specs=pl.BlockSpec((None, None, tr, C), lambda i, w: (l, w[1], i, 0))),
        out_shape=jax.ShapeDtypeStruct(J.shape, F32), input_output_aliases={3: 0},
        compiler_params=_cp("parallel"),
    )(where, g, others, J)


def adamw(w, g, m, v, name, copy_g=False):
    R, C = w.shape
    tr = _rows_tile(R)

    def body(w_ref, g_ref, m_ref, v_ref, *outs):
        d_ref, nm_ref, nv_ref = outs[-3:]
        gv = g_ref[...]
        if copy_g:
            outs[0][...] = gv
        nm = ADAM_B1 * m_ref[...] + (1.0 - ADAM_B1) * gv
        nv = ADAM_B2 * v_ref[...] + (1.0 - ADAM_B2) * (gv * gv)
        m_hat = nm / (1.0 - ADAM_B1 ** ADAM_STEP)
        v_hat = nv / (1.0 - ADAM_B2 ** ADAM_STEP)
        d_ref[...] = -ADAM_LR * (m_hat / (jnp.sqrt(v_hat) + ADAM_EPS) + ADAM_WD * w_ref[...])
        nm_ref[...] = nm
        nv_ref[...] = nv

    sd = jax.ShapeDtypeStruct((R, C), F32)
    n_out = 4 if copy_g else 3
    return pl.pallas_call(
        body, name=name, grid=(R // tr,),
        in_specs=[_row(tr, C)] * 4, out_specs=[_row(tr, C)] * n_out, out_shape=[sd] * n_out,
        compiler_params=_cp("parallel"),
    )(w, g, m, v)


ANY = pl.BlockSpec(memory_space=pl.ANY)
HBM = pl.BlockSpec(memory_space=pltpu.HBM)
SEM = pl.BlockSpec(memory_space=pltpu.SEMAPHORE)
EFFECT = pltpu.SideEffectType.DATAFLOW_SIDE_EFFECTING


def _place():
    x, y, c = lax.axis_index("x"), lax.axis_index("y"), lax.axis_index("c")
    chips = [(1 - x, y), (x, 1 - y), (1 - x, 1 - y)]
    return x, y, c, chips


def _copy(src, dst, send, recv, k, to):
    return pltpu.make_async_remote_copy(src_ref=src, dst_ref=dst, send_sem=send.at[k], recv_sem=recv.at[k],
                                        device_id=to, device_id_type=MESH)


def xchg_start(name, bufs, plan, n, after=()):
    nb = len(bufs)

    na = len(after)

    def body(*refs):
        send, recv, token = refs[nb + na], refs[nb + na + 1], refs[-1]
        for k, (src, dst, to) in enumerate(plan(refs[:nb])):
            _copy(src, dst, send, recv, k, to).start()
        token[...] = jnp.zeros_like(token)

    outs = pl.pallas_call(
        body, name=name,
        out_shape=(pltpu.SemaphoreType.DMA((n,)), pltpu.SemaphoreType.DMA((n,)),
                   *[pltpu.HBM(b.shape, b.dtype) for b in bufs], jax.ShapeDtypeStruct((8, 128), F32)),
        in_specs=[HBM] * nb + [ANY] * na,
        out_specs=(SEM, SEM, *[HBM] * nb, pl.BlockSpec(memory_space=pltpu.VMEM)),
        input_output_aliases={i: 2 + i for i in range(nb)},
        compiler_params=pltpu.CompilerParams(has_side_effects=EFFECT),
    )(*[pltpu.with_memory_space_constraint(b, pltpu.HBM) for b in bufs], *after)
    return dict(name=name, send=outs[0], recv=outs[1], bufs=list(outs[2:2 + nb]), plan=plan), outs[-1]


def xchg_wait(flight, after):
    bufs, plan = flight["bufs"], flight["plan"]
    nb = len(bufs)

    def body(*refs):
        send, recv = refs[nb], refs[nb + 1]
        for k, (src, dst, to) in enumerate(plan(refs[:nb])):
            cp = _copy(src, dst, send, recv, k, to)
            cp.wait_send()
            cp.wait_recv()

    outs = pl.pallas_call(
        body, name=flight["name"] + "_wait",
        out_shape=tuple(pltpu.HBM(b.shape, b.dtype) for b in bufs),
        in_specs=[HBM] * nb + [SEM, SEM] + [ANY] * len(after),
        out_specs=tuple([HBM] * nb), input_output_aliases={i: i for i in range(nb)},
        compiler_params=pltpu.CompilerParams(has_side_effects=EFFECT),
    )(*bufs, flight["send"], flight["recv"], *after)
    return list(outs)


def _flip(k, x, y, c):
    return ((1 - x) if k & 4 else x, (1 - y) if k & 2 else y, (1 - c) if k & 1 else c)


def cast_into_slot(srcs, name, after):
    me = (2 * lax.axis_index("x") + lax.axis_index("y")).astype(jnp.int32).reshape(1)
    ns = len(srcs)

    def body(me_ref, *refs):
        outs = refs[ns + len(after):]
        for t in range(ns):
            outs[t][...] = refs[t][...].astype(outs[t].dtype).reshape(outs[t].shape)

    in_specs, out_specs, out_shape = [], [], []
    for arr, l in srcs:
        if l is None:
            in_specs.append(pl.BlockSpec(arr.shape, lambda i, w, nd=arr.ndim: (0,) * nd))
            a2, b, dt = (arr.shape[0] // 2, arr.shape[1], BF16) if arr.ndim == 2 else (arr.shape[1], arr.shape[2], F32)
        else:
            in_specs.append(pl.BlockSpec((None,) + arr.shape[1:], lambda i, w, l=l: (l, 0, 0)))
            a2, b, dt = arr.shape[1] // 2, arr.shape[2], BF16
        out_specs.append(pl.BlockSpec((None, 2, a2, b), lambda i, w: (w[0], 0, 0, 0)))
        out_shape.append(jax.ShapeDtypeStruct((4, 2, a2, b), dt))
    in_specs += [ANY] * len(after)
    return pl.pallas_call(
        body, name=name,
        grid_spec=pltpu.PrefetchScalarGridSpec(num_scalar_prefetch=1, grid=(1,), in_specs=in_specs,
                                               out_specs=out_specs),
        out_shape=out_shape, compiler_params=_cp("arbitrary"),
    )(me, *[arr for arr, _ in srcs], *after)


class WeightGather:
    def __init__(self, source, groups):
        self.names = dict(groups)
        self.ici, self.d2d = {}, {}
        self.token = None
        for gname, names in groups:
            nt = len(names)
            after = [] if self.token is None else [self.token]
            lands = cast_into_slot([source(n) for n in names], f"ag_cast_{gname}", after)

            def plan(refs, nt=nt):
                x, y, c, chips = _place()
                out = []
                for t in range(nt):
                    mine = refs[t].at[2 * x + y, c]
                    out += [(mine, mine, (cx, cy, c)) for cx, cy in chips]
                return out

            self.ici[gname], self.token = xchg_start(f"ag_ici_{gname}", lands, plan, 3 * nt, after=after)

    def forward(self, gname, after):
        nt = len(self.names[gname])
        lands = xchg_wait(self.ici.pop(gname), after)

        def plan(refs):
            x, y, c, chips = _place()
            out = []
            for t in range(nt):
                for cx, cy in chips:
                    piece = refs[t].at[2 * cx + cy, c]
                    out.append((piece, piece, (x, y, 1 - c)))
            return out

        self.d2d[gname], token = xchg_start(f"ag_d2d_{gname}", lands, plan, 3 * nt)
        return token

    def get(self, gname, after):
        lands = xchg_wait(self.d2d.pop(gname), after)
        return dict(zip(self.names[gname], lands))


class GradReduce:
    def __init__(self, kinds):
        self.J = {k: lax.empty((L, 2, a2, b), F32) for k, (L, a2, b) in kinds.items()}
        self.x, self.j = {}, {}

    @staticmethod
    def _where(name):
        kind, _, l = name.partition("_")
        return kind, int(l or 0)

    def send(self, gname, grads, after=()):
        names = list(grads)
        nt = len(names)
        gs = [grads[n] for n in names]
        xs = [lax.empty((7,) + g.shape[2:], g.dtype) for g in gs]

        def plan(refs):
            x, y, c, _ = _place()
            out = []
            for t in range(nt):
                for k in range(1, 8):
                    px, py, pc = _flip(k, x, y, c)
                    out.append((refs[t].at[2 * px + py, pc], refs[nt + t].at[k - 1], (px, py, pc)))
            return out

        flight, token = xchg_start(f"rs_x_{gname}", gs + xs, plan, 7 * nt, after=after)
        self.x[gname] = (names, flight)
        return token

    def reduce(self, gname, after):
        names, flight = self.x.pop(gname)
        nt = len(names)
        bufs = xchg_wait(flight, after)
        mine = jnp.stack([2 * lax.axis_index("x") + lax.axis_index("y"), lax.axis_index("c")]).astype(jnp.int32)
        where = [self._where(n) for n in names]
        js = [add8_into(self.J[kind], l, bufs[t], bufs[nt + t], mine, f"rs_add_{names[t]}")
              for t, (kind, l) in enumerate(where)]

        def plan(refs):
            x, y, c, _ = _place()
            out = []
            for t in range(nt):
                half = refs[t].at[where[t][1], c]
                out.append((half, half, (x, y, 1 - c)))
            return out

        flight, token = xchg_start(f"rs_join_{gname}", js, plan, nt)
        self.j[gname] = (where, flight)
        return token

    def finish(self, gname, after):
        where, flight = self.j.pop(gname)
        for (kind, _), j in zip(where, xchg_wait(flight, after)):
            self.J[kind] = j


def small_allreduce_start(v, after):
    me = 4 * lax.axis_index("x") + 2 * lax.axis_index("y") + lax.axis_index("c")
    land = lax.dynamic_update_slice(lax.empty((8,) + v.shape, v.dtype), v[None], (me, 0, 0))

    def plan(refs):
        x, y, c, _ = _place()
        return [(refs[0], refs[1].at[4 * x + 2 * y + c], _flip(k, x, y, c)) for k in range(1, 8)]

    return xchg_start("small_allreduce", [v, land], plan, 7, after=after)


def sum8(all8, name):
    def body(x_ref, o_ref):
        acc = x_ref[0]
        for d in range(1, 8):
            acc = acc + x_ref[d]
        o_ref[...] = acc

    return pl.pallas_call(
        body, name=name,
        in_specs=[pl.BlockSpec(memory_space=pltpu.VMEM)], out_specs=pl.BlockSpec(memory_space=pltpu.VMEM),
        out_shape=jax.ShapeDtypeStruct(all8.shape[1:], F32),
        compiler_params=pltpu.CompilerParams(vmem_limit_bytes=VMEM_LIMIT),
    )(all8)


AG_GROUPS = (("a0", ("pw1_0", "pw2_0", "small")), ("f0", ("up_0", "down_0")),
             ("l1", ("pw1_1", "pw2_1", "up_1", "down_1")), ("l2", ("kv", "wq_0", "wo_0", "up_2", "down_2")),
             ("l3", ("wq_1", "wo_1", "up_3", "down_3")))


def _bucket_table():
    qi = np.arange(BLK)[:, None]
    kj = np.arange(2 * BLK)[None, :]
    d = np.maximum(qi + BLK - kj, 0)
    max_exact = N_BUCKETS // 2
    log_ratio = (np.log(np.maximum(d, 1).astype(np.float32) / np.float32(max_exact))
                 / np.float32(math.log(MAX_DISTANCE / max_exact))).astype(np.float32)
    large = max_exact + (log_ratio * np.float32(N_BUCKETS - max_exact)).astype(np.int32)
    large = np.minimum(large, N_BUCKETS - 1)
    return np.where(d < max_exact, d, large).astype(np.int32)


def _heads_major(a, nh):
    T = a.shape[0]
    return a.reshape(T, nh, HD).transpose(1, 0, 2)


def _heads_minor(a):
    nh, T, _ = a.shape
    return a.transpose(1, 0, 2).reshape(T, nh * HD)


def _slots(land):
    return land.reshape(4, 2 * land.shape[2], land.shape[3])


def _rows(land):
    return land.reshape(8 * land.shape[2], land.shape[3])


def _gview(g):
    s, K, n = g.shape
    return g.reshape(4, 2, K // 2, n) if s == 4 else g.reshape(4, 2, K // 8, n)


def _gate(a, token):
    return a * (1.0 + token[0, 0])


def _conv_small(f_small):
    fs = f_small.transpose(1, 2, 0, 3).reshape(2, 40, D)
    b_pw1 = f_small[:, :, 35:37, :].transpose(1, 0, 2, 3).reshape(2, 1, 2 * D)
    rev = jnp.concatenate([fs[:, CONV_W - 1::-1], jnp.zeros((2, 40 - CONV_W, D), F32)], axis=1)
    return dict(conv=fs, conv_rev=rev, b_pw1=b_pw1, b_pw2=fs[:, 34:35])


def run_step(x, target, P, ag, rs):
    T = x.shape[0]
    zero = jnp.zeros((1, 1, D), F32)
    nm, nf = P["norm_mix"], P["norm_ffn"]
    ag.forward("a0", [ag.token])
    W = ag.get("a0", [])
    sm = _conv_small(W["small"])
    h = x
    saved = []
    for l in range(2):
        xn, u, a = norm_mm_glu(h, nm, l, _slots(W[f"pw1_{l}"]), sm["b_pw1"], f"f_pw1_{l}")
        y, s = dwconv_ln_silu(a, sm["conv"], l, f"f_conv_{l}")
        b2 = sm["b_pw2"]
        if l == 0:
            b2 = _gate(b2, ag.forward("f0", [s]))
        h1 = mm_bias_res(s, _rows(W[f"pw2_{l}"]), b2, l, h, f"f_pw2_{l}")
        if l == 0:
            W.update(ag.get("f0", [h1]))
        xn2, gu, f = norm_mm_swiglu(h1, nf, l, _slots(W[f"up_{l}"]), f"f_up_{l}")
        nxt = "l1" if l == 0 else "l2"
        h2 = mm_bias_res(f, _rows(W[f"down_{l}"]), _gate(zero, ag.forward(nxt, [f])), 0, h1, f"f_down_{l}")
        W.update(ag.get(nxt, [h2]))
        saved.append(dict(h=h, xn=xn, u=u, a=a, y=y, s=s, h1=h1, xn2=xn2, gu=gu, f=f))
        h = h2
    h_kv = h
    kvn, kv, xn_q0, q_0 = norm_mm_pair(h, P["norm_kv"], 0, _rows(W["kv"]), nm, 2, _rows(W["wq_0"]), "f_kv_q_0",
                                       scale_b=HD ** -0.5)
    kp = jnp.pad(_heads_major(kv[:, :N_KV * HD], N_KV), ((0, 0), (BLK, 0), (0, 0)))
    vp = jnp.pad(_heads_major(kv[:, N_KV * HD:], N_KV), ((0, 0), (BLK, 0), (0, 0)))
    kvt = jnp.pad(kv.T.reshape(2, N_KV, HD, T), ((0, 0), (0, 0), (0, 0), (BLK, 0)))
    kt, vt = kvt[0], kvt[1]
    bucket = _bucket_table()
    onehot = jnp.asarray(np.eye(N_BUCKETS, dtype=np.float32)[bucket])
    bias = jnp.einsum("qkb,bh->hkq", onehot, P["rel_bias"], precision=lax.Precision.HIGHEST)
    bias = bias.reshape(N_KV, GROUP, 2 * BLK, BLK).transpose(0, 2, 1, 3).reshape(1, N_KV, 2 * BLK, QW)
    bias = bias + jnp.asarray(band_mask())[:, None]
    for j in range(2):
        l = 2 + j
        if j == 0:
            xn, q = xn_q0, q_0
        else:
            xn, q = norm_mm(h, nm, l, _rows(W[f"wq_{j}"]), f"f_q_{j}", scale=HD ** -0.5)
        qh = q.T.reshape(N_KV, GROUP, HD, T)
        sink = jnp.broadcast_to(P["sinks"][j].reshape(N_KV, GROUP, 1), (N_KV, GROUP, BLK)).reshape(N_KV, 1, QW)
        oh = attn_fwd(qh, kp, vt, bias, sink, f"f_attn_{j}")
        attn = oh.reshape(N_HEADS * HD, T).T
        h1 = mm_bias_res(attn, _rows(W[f"wo_{j}"]), zero, 0, h, f"f_wo_{j}")
        xn2, gu, f = norm_mm_swiglu(h1, nf, l, _slots(W[f"up_{l}"]), f"f_up_{l}")
        zg = _gate(zero, ag.forward("l3", [f])) if j == 0 else zero
        h2 = mm_bias_res(f, _rows(W[f"down_{l}"]), zg, 0, h1, f"f_down_{l}")
        if j == 0:
            W.update(ag.get("l3", [h2]))
        saved.append(dict(h=h, xn=xn, qh=qh, oh=oh, sink=sink, attn=attn, h1=h1, xn2=xn2, gu=gu, f=f))
        h = h2

    dh, st_final = final_loss(h, P["norm_final"], target, "loss_head")

    S = dict(norm_ffn=[None] * 4, norm_mix=[None] * 4, conv=[None] * 2, taps=[None] * 2, b_pw1=[None] * 2,
             b_pw2=[None] * 2, sinks=[None] * 2)

    def ffn_bwd(dh, sv, l, nf, after=()):
        du = mmT_swiglu_bwd(dh, _rows(W[f"down_{l}"]), sv["gu"], f"b_down_{l}", after)
        gd = mm_dw(sv["f"], dh, f"w_down_{l}", 512, 1)
        gu = mm_dw(sv["xn2"], du, f"w_up_{l}", DFF // 2, 4)
        dh, dg = mmT_rmsbwd(du, _slots(W[f"up_{l}"]), sv["h1"], nf, l, dh, f"b_up_{l}")
        S["norm_ffn"][l] = dg
        return dh, {f"down_{l}": _gview(gd), f"up_{l}": _gview(gu)}

    dk = dv = dbias = None
    sent = []
    for j in (1, 0):
        l = 2 + j
        sv = saved[l]
        dh, grads = ffn_bwd(dh, sv, l, nf, sent)
        dattn = mmT(dh, _rows(W[f"wo_{j}"]), f"b_wo_{j}")
        grads[f"wo_{j}"] = _gview(mm_dw(sv["attn"], dh, f"w_wo_{j}", 512, 1))
        doh = dattn.T.reshape(N_KV, GROUP, HD, T)
        dqh, dk, dv, dbj, dsj = attn_bwd(sv["qh"], kp, kt, vp, bias, sv["sink"], sv["oh"], doh, f"b_attn_{j}",
                                         None if dk is None else (dk, dv))
        dq = dqh.reshape(N_HEADS * HD, T).T
        grads[f"wq_{j}"] = _gview(mm_dw(sv["xn"], dq, f"w_q_{j}", 512, 1))
        dh, dg = mmT_rmsbwd(dq, _rows(W[f"wq_{j}"])[None], sv["h"], nm, l, dh, f"b_q_{j}")
        S["norm_mix"][l] = dg
        S["sinks"][j] = jnp.sum(dsj.reshape(N_HEADS, BLK), axis=1)
        dbias = dbj if dbias is None else dbias + dbj
        if j == 1:
            sent = [rs.send("l3", grads)]

    dkv = jnp.concatenate([_heads_minor(dk[:, BLK:]), _heads_minor(dv[:, BLK:])], axis=1).astype(BF16)
    grads["kv"] = _gview(mm_dw(kvn, dkv, "w_kv", 512, 1))
    dh, dg = mmT_rmsbwd(dkv, _rows(W["kv"])[None], h_kv, P["norm_kv"], 0, dh, "b_kv")
    S["norm_kv"] = dg
    dbh = dbias.reshape(N_KV, 2 * BLK, GROUP, BLK)
    S["rel_bias"] = jnp.einsum("vkgq,qkb->bvg", dbh, onehot, precision=lax.Precision.HIGHEST).reshape(N_BUCKETS, N_HEADS)
    sent = [rs.send("l2", grads)]
    nf = _gate(nf, rs.reduce("l3", [dh]))

    for l in (1, 0):
        sv = saved[l]
        dh, grads = ffn_bwd(dh, sv, l, nf, sent)
        conv = sm["conv"]
        if l == 0:
            conv = _gate(conv, rs.send("f0", grads))
            grads = {}
        dy, st = mmT_lnbwd(dh, _rows(W[f"pw2_{l}"]), sv["y"], conv, l, f"b_pw2_{l}")
        g2, S["b_pw2"][l] = mm_dw(sv["s"], dh, f"w_pw2_{l}", 512, 1, colsum=True)
        du, dtaps = dwconv_glu_bwd(dy, sv["a"], sv["u"], sm["conv"], sm["conv_rev"], l, f"b_conv_{l}")
        S["conv"][l] = st
        S["taps"][l] = dtaps
        if l == 0:
            rs.finish("l2", [du])
            nm = _gate(nm, rs.reduce("l1", [du]))
        g1, S["b_pw1"][l] = mm_dw(sv["xn"], du, f"w_pw1_{l}", 512, 4, colsum=True)
        grads[f"pw2_{l}"], grads[f"pw1_{l}"] = _gview(g2), _gview(g1)
        dh, dg = mmT_rmsbwd(du, _slots(W[f"pw1_{l}"]), sv["h"], nm, l, dh, f"b_pw1_{l}")
        S["norm_mix"][l] = dg
        if l == 1:
            sent = [rs.send("l1", grads)]
            rs.finish("l3", [dh])
            nf = _gate(nf, rs.reduce("l2", [dh]))
    S["final"] = st_final
    return grads, dh, S


R_CONV = 37
R_SMALL = 88


def _pack_small(S):
    rows = []
    for l in range(2):
        rows += [S["taps"][l], S["conv"][l], S["b_pw2"][l], S["b_pw1"][l].reshape(2, D)]
    rows += S["norm_mix"] + S["norm_ffn"] + [S["norm_kv"], S["final"]]
    tail = jnp.concatenate([jnp.stack(S["sinks"]).reshape(-1), S["rel_bias"].reshape(-1)])
    rows += [jnp.pad(tail, (0, D - tail.shape[0]))[None]]
    v = jnp.concatenate(rows, axis=0)
    return jnp.pad(v, ((0, R_SMALL - v.shape[0]), (0, 0)))


def kernel(x, norm_mix, norm_ffn, conv_w_pw1, conv_b_pw1, conv_w_dw, conv_b_dw, conv_ln_g, conv_ln_b, conv_w_pw2, conv_b_pw2, norm_kv, w_kv, w_q, w_o, sinks, rel_bias, ffn_w_up, ffn_w_down, norm_final, loss_target, m_norm_mix, m_norm_ffn, m_conv_w_pw1, m_conv_b_pw1, m_conv_w_dw, m_conv_b_dw, m_conv_ln_g, m_conv_ln_b, m_conv_w_pw2, m_conv_b_pw2, m_norm_kv, m_w_kv, m_w_q, m_w_o, m_sinks, m_rel_bias, m_ffn_w_up, m_ffn_w_down, m_norm_final, v_norm_mix, v_norm_ffn, v_conv_w_pw1, v_conv_b_pw1, v_conv_w_dw, v_conv_b_dw, v_conv_ln_g, v_conv_ln_b, v_conv_w_pw2, v_conv_b_pw2, v_norm_kv, v_w_kv, v_w_q, v_w_o, v_sinks, v_rel_bias, v_ffn_w_up, v_ffn_w_down, v_norm_final):
    me = 2 * lax.axis_index("x") + lax.axis_index("y")
    weights = dict(norm_mix=norm_mix, norm_ffn=norm_ffn, conv_w_pw1=conv_w_pw1, conv_b_pw1=conv_b_pw1,
                   conv_w_dw=conv_w_dw, conv_b_dw=conv_b_dw, conv_ln_g=conv_ln_g, conv_ln_b=conv_ln_b,
                   conv_w_pw2=conv_w_pw2, conv_b_pw2=conv_b_pw2, norm_kv=norm_kv, w_kv=w_kv, w_q=w_q, w_o=w_o,
                   sinks=sinks, rel_bias=rel_bias, ffn_w_up=ffn_w_up, ffn_w_down=ffn_w_down, norm_final=norm_final)
    mom_m = dict(norm_mix=m_norm_mix, norm_ffn=m_norm_ffn, conv_w_pw1=m_conv_w_pw1, conv_b_pw1=m_conv_b_pw1,
                 conv_w_dw=m_conv_w_dw, conv_b_dw=m_conv_b_dw, conv_ln_g=m_conv_ln_g, conv_ln_b=m_conv_ln_b,
                 conv_w_pw2=m_conv_w_pw2, conv_b_pw2=m_conv_b_pw2, norm_kv=m_norm_kv, w_kv=m_w_kv, w_q=m_w_q,
                 w_o=m_w_o, sinks=m_sinks, rel_bias=m_rel_bias, ffn_w_up=m_ffn_w_up, ffn_w_down=m_ffn_w_down,
                 norm_final=m_norm_final)
    mom_v = dict(norm_mix=v_norm_mix, norm_ffn=v_norm_ffn, conv_w_pw1=v_conv_w_pw1, conv_b_pw1=v_conv_b_pw1,
                 conv_w_dw=v_conv_w_dw, conv_b_dw=v_conv_b_dw, conv_ln_g=v_conv_ln_g, conv_ln_b=v_conv_ln_b,
                 conv_w_pw2=v_conv_w_pw2, conv_b_pw2=v_conv_b_pw2, norm_kv=v_norm_kv, w_kv=v_w_kv, w_q=v_w_q,
                 w_o=v_w_o, sinks=v_sinks, rel_bias=v_rel_bias, ffn_w_up=v_ffn_w_up, ffn_w_down=v_ffn_w_down,
                 norm_final=v_norm_final)

    big = {"conv_w_pw1": "pw1", "conv_w_pw2": "pw2", "w_q": "wq", "w_o": "wo", "ffn_w_up": "up",
           "ffn_w_down": "down", "w_kv": "kv"}
    of_kind = {k: n for n, k in big.items()}

    def source(name):
        if name == "small":
            return jnp.concatenate(
                [conv_w_dw, conv_b_dw[:, None], conv_ln_g[:, None], conv_ln_b[:, None], conv_b_pw2[:, None],
                 conv_b_pw1.reshape(2, 2, 256), jnp.zeros((2, 3, 256), F32)], axis=1), None
        kind, _, l = name.partition("_")
        return weights[of_kind[kind]], (int(l) if l else None)

    ag = WeightGather(source, AG_GROUPS)
    rs = GradReduce({"pw1": (2, 512, 512), "pw2": (2, 128, D), "wq": (2, 128, D), "wo": (2, 128, D),
                     "up": (4, 512, DFF // 2), "down": (4, DFF // 8, D), "kv": (1, 128, 512)})

    P = dict(norm_mix=norm_mix[:, None], norm_ffn=norm_ffn[:, None], norm_kv=norm_kv[None, None],
             norm_final=norm_final[None], sinks=sinks, rel_bias=rel_bias)
    last, grad_x, S = run_step(x[0], loss_target[0], P, ag, rs)

    rs.finish("l1", [grad_x])
    small_flight, token = small_allreduce_start(_gate(_pack_small(S), rs.reduce("f0", [grad_x])), [])
    token = rs.send("c0", last, after=[token])
    delta, new_m, new_v, big_grads = {}, {}, {}, {}

    def update(n):
        shp = weights[n].shape
        r2 = (int(np.prod(shp[:-1])), shp[-1])
        g, d, nm, nv = adamw(weights[n].reshape(r2), rs.J[big[n]].reshape(r2), mom_m[n].reshape(r2),
                             mom_v[n].reshape(r2), f"adamw_{n}", copy_g=True)
        big_grads[n], delta[n], new_m[n], new_v[n] = g.reshape(shp), d.reshape(shp), nm.reshape(shp), nv.reshape(shp)

    rs.finish("f0", [token])
    for n in ("ffn_w_up", "ffn_w_down"):
        update(n)
    vsum = sum8(xchg_wait(small_flight, [delta["ffn_w_up"], delta["ffn_w_down"]])[1], "small_sum")

    col = lambda a: lax.dynamic_slice_in_dim(a, me * 256, 256, axis=-1)
    grads = {}
    for l in range(2):
        base = l * R_CONV
        grads.setdefault("conv_w_dw", []).append(col(vsum[base:base + 31]))
        grads.setdefault("conv_b_dw", []).append(col(vsum[base + 31]))
        grads.setdefault("conv_ln_g", []).append(col(vsum[base + 32]))
        grads.setdefault("conv_ln_b", []).append(col(vsum[base + 33]))
        grads.setdefault("conv_b_pw2", []).append(col(vsum[base + 34]))
        grads.setdefault("conv_b_pw1", []).append(
            lax.dynamic_slice_in_dim(vsum[base + 35:base + 37].reshape(2 * D), me * 512, 512, axis=0))
    grads = {k: jnp.stack(v) for k, v in grads.items()}
    base = 2 * R_CONV
    grads["norm_mix"] = vsum[base:base + 4]
    grads["norm_ffn"] = vsum[base + 4:base + 8]
    grads["norm_kv"] = vsum[base + 8]
    grads["norm_final"] = vsum[base + 9]
    loss = vsum[base + 10, 0]
    grads["sinks"] = vsum[base + 11, 0:32].reshape(2, 16)
    grads["rel_bias"] = vsum[base + 11, 32:32 + 512].reshape(32, 16)

    for n in weights:
        if n not in big:
            shp = weights[n].shape
            r2 = (int(np.prod(shp[:-1])), shp[-1])
            d, nm, nv = adamw(weights[n].reshape(r2), grads[n].reshape(r2), mom_m[n].reshape(r2),
                              mom_v[n].reshape(r2), f"adamw_{n}")
            delta[n], new_m[n], new_v[n] = d.reshape(shp), nm.reshape(shp), nv.reshape(shp)

    rs.reduce("c0", [vsum])
    for n in ("w_q", "w_o", "w_kv"):
        update(n)
    rs.finish("c0", [delta["w_kv"]])
    for n in ("conv_w_pw1", "conv_w_pw2"):
        update(n)
    grads.update(big_grads)

    order = list(weights)
    return (loss, grad_x[None], *[grads[n] for n in order], *[delta[n] for n in order],
            *[new_m[n] for n in order], *[new_v[n] for n in order])
```
